```python
import jax, jax.numpy as jnp
from jax import lax
import numpy as np

D_MODEL = 1024
BATCH = 2
SEQ = 8192
DEPTH = 2

D_FF = 2816
N_HEADS = 16
N_KV_GROUPS = 2
HEADS_PER_GROUP = N_HEADS // N_KV_GROUPS
HEAD_DIM = 64
CMP_LEN = 32
CMP_STRIDE = 16
CMP_HIDDEN = 256
SEL_LEN = 64
N_SELECT = 16
WINDOW = 512
Q_BLOCK = 128
POOL_WINDOWS = (2, 4, 8, 16)
POOL_GROUP_DIM = 128
POOL_WIDTH = len(POOL_WINDOWS) * POOL_GROUP_DIM
N_BRANCHES = 2
Q_WIDTH = N_HEADS * HEAD_DIM
KV_WIDTH = N_KV_GROUPS * HEAD_DIM
N_NSA_GATES = 3 * N_HEADS
IN_SPLITS = (Q_WIDTH, KV_WIDTH, KV_WIDTH, KV_WIDTH, KV_WIDTH, KV_WIDTH, KV_WIDTH,
             N_NSA_GATES, POOL_WIDTH, N_BRANCHES * D_MODEL)
IN_WIDTH = sum(IN_SPLITS)
RMS_EPS = 1e-6
ALIBI_MAX_BIAS = 8.0

kernel_name = "hybrid_pool_nsa_macaron_gated"


def rmsnorm(x, g):
    xf = x.astype(jnp.float32)
    y = xf * lax.rsqrt(jnp.mean(xf * xf, axis=-1, keepdims=True) + RMS_EPS)
    return (y * g.astype(jnp.float32)).astype(x.dtype)


def swiglu(x, w_gate, w_up, w_down):
    return (jax.nn.silu(x @ w_gate) * (x @ w_up)) @ w_down


def masked_softmax(s, mask):
    s = jnp.where(mask, s.astype(jnp.float32), -1e30)
    s = s - jnp.max(s, axis=-1, keepdims=True)
    p = jnp.where(mask, jnp.exp(s), 0.0)
    return p / jnp.maximum(jnp.sum(p, axis=-1, keepdims=True), 1e-30)


def alibi_slopes():
    h = jnp.arange(1, N_HEADS + 1, dtype=jnp.float32)
    m = jnp.exp2(-ALIBI_MAX_BIAS * h / N_HEADS)
    return m.reshape(N_KV_GROUPS, HEADS_PER_GROUP)


def multiscale_pool(u, w_group, scale):
    B, T, _ = u.shape
    n_g = len(POOL_WINDOWS)
    uf = u.astype(jnp.float32)
    cs = jnp.cumsum(uf, axis=1)
    pos = jnp.arange(T, dtype=jnp.float32)
    outs = []
    for gi, w in enumerate(POOL_WINDOWS):
        c = cs[..., gi * POOL_GROUP_DIM:(gi + 1) * POOL_GROUP_DIM]
        lag = jnp.pad(c, ((0, 0), (w, 0), (0, 0)))[:, :T]
        cnt = jnp.minimum(pos + 1.0, float(w))[None, :, None]
        outs.append((c - lag) / cnt)
    pooled = jnp.stack(outs, axis=2)
    delta = (pooled - uf.reshape(B, T, n_g, POOL_GROUP_DIM)).astype(u.dtype)
    mixed = jnp.einsum('btgc,gcd->btgd', delta, w_group).reshape(B, T, POOL_WIDTH)
    return mixed * scale


def compress_tokens(kv, pos_emb, w1, w2):
    B, T = kv.shape[:2]
    n_cmp = (T - CMP_LEN) // CMP_STRIDE + 1
    idx = jnp.arange(n_cmp)[:, None] * CMP_STRIDE + jnp.arange(CMP_LEN)[None, :]
    blocks = kv[:, idx] + pos_emb[None, None, :, None, :]
    blocks = blocks.transpose(0, 1, 3, 2, 4).reshape(B, n_cmp, N_KV_GROUPS, CMP_LEN * HEAD_DIM)
    return jax.nn.gelu(blocks @ w1) @ w2


def cmp_to_sel_overlap(T):
    n_cmp = (T - CMP_LEN) // CMP_STRIDE + 1
    n_sel = T // SEL_LEN
    cs = np.arange(n_cmp)[:, None] * CMP_STRIDE
    ss = np.arange(n_sel)[None, :] * SEL_LEN
    ov = np.clip(np.minimum(cs + CMP_LEN, ss + SEL_LEN) - np.maximum(cs, ss), 0, None) / CMP_LEN
    return jnp.asarray(ov, dtype=jnp.float32)


def nsa_attention(q, k_cmp, v_cmp, k_slc, v_slc, k_win, v_win, gates):
    B, T = q.shape[:2]
    n_cmp = k_cmp.shape[1]
    n_sel = T // SEL_LEN
    n_top = min(N_SELECT, n_sel)
    slopes = alibi_slopes()[None, :, :, None, None]
    overlap = cmp_to_sel_overlap(T)
    cmp_end = jnp.arange(n_cmp) * CMP_STRIDE + CMP_LEN - 1
    kb = k_slc.reshape(B, n_sel, SEL_LEN, N_KV_GROUPS, HEAD_DIM).transpose(0, 3, 1, 2, 4)
    vb = v_slc.reshape(B, n_sel, SEL_LEN, N_KV_GROUPS, HEAD_DIM).transpose(0, 3, 1, 2, 4)
    kw_pad = jnp.pad(k_win, ((0, 0), (WINDOW, 0), (0, 0), (0, 0)))
    vw_pad = jnp.pad(v_win, ((0, 0), (WINDOW, 0), (0, 0), (0, 0)))
    gather = jax.vmap(jax.vmap(lambda blk, ix: blk[ix]))
    sel_ids = jnp.arange(n_sel)
    n_sel_keys = n_top * SEL_LEN

    def query_block(c):
        q0 = c * Q_BLOCK
        t = q0 + jnp.arange(Q_BLOCK)
        qc = lax.dynamic_slice_in_dim(q, q0, Q_BLOCK, axis=1)
        gc = lax.dynamic_slice_in_dim(gates, q0, Q_BLOCK, axis=1)

        dist_c = (t[:, None] - cmp_end[None, :]).astype(jnp.float32)
        s = jnp.einsum('bqghd,bngd->bghqn', qc, k_cmp) - slopes * dist_c
        p_cmp = masked_softmax(s, dist_c >= 0)
        o_cmp = jnp.einsum('bghqn,bngd->bqghd', p_cmp.astype(v_cmp.dtype), v_cmp)

        score = jnp.einsum('bghqn,nj->bgqj', p_cmp, overlap)
        cur = t // SEL_LEN
        valid = sel_ids[None, :] * SEL_LEN <= t[:, None]
        forced = valid & ((sel_ids[None, :] == 0) | (sel_ids[None, :] == cur[:, None])
                          | (sel_ids[None, :] == cur[:, None] - 1))
        score = jnp.where(forced, jnp.inf, jnp.where(valid, score, -jnp.inf))
        _, idx = lax.top_k(score, n_top)
        ks = gather(kb, idx).reshape(B, N_KV_GROUPS, Q_BLOCK, n_sel_keys, HEAD_DIM)
        vs = gather(vb, idx).reshape(B, N_KV_GROUPS, Q_BLOCK, n_sel_keys, HEAD_DIM)
        kpos = (idx[..., None] * SEL_LEN + jnp.arange(SEL_LEN)).reshape(B, N_KV_GROUPS, Q_BLOCK, n_sel_keys)
        dist_s = (t[None, None, :, None] - kpos)[:, :, None]
        s = jnp.einsum('bqghd,bgqkd->bghqk', qc, ks) - slopes * dist_s.astype(jnp.float32)
        p = masked_softmax(s, dist_s >= 0)
        o_slc = jnp.einsum('bghqk,bgqkd->bqghd', p.astype(vs.dtype), vs)

        kw = lax.dynamic_slice_in_dim(kw_pad, q0, WINDOW + Q_BLOCK, axis=1)
        vw = lax.dynamic_slice_in_dim(vw_pad, q0, WINDOW + Q_BLOCK, axis=1)
        wpos = q0 - WINDOW + jnp.arange(WINDOW + Q_BLOCK)
        dist_w = t[:, None] - wpos[None, :]
        mask_w = (dist_w >= 0) & (dist_w < WINDOW) & (wpos[None, :] >= 0)
        s = jnp.einsum('bqghd,bkgd->bghqk', qc, kw) - slopes * dist_w.astype(jnp.float32)
        p = masked_softmax(s, mask_w)
        o_win = jnp.einsum('bghqk,bkgd->bqghd', p.astype(vw.dtype), vw)

        return gc[..., 0:1] * o_cmp + gc[..., 1:2] * o_slc + gc[..., 2:3] * o_win

    out = lax.map(query_block, jnp.arange(T // Q_BLOCK))
    return out.transpose(1, 0, 2, 3, 4, 5).reshape(B, T, N_HEADS * HEAD_DIM)


def setup_inputs(seed: int = 0) -> dict:
    key = jax.random.key(seed)
    ks = jax.random.split(key, 24)

    def dense(k, shape, fan_in):
        return jax.random.normal(k, shape, jnp.float32) * (fan_in ** -0.5)

    def gain(k, shape):
        return 1.0 + 0.01 * jax.random.normal(k, shape, jnp.float32)

    L = DEPTH
    return {
        "x": jax.random.normal(ks[0], (BATCH, SEQ, D_MODEL), jnp.float32),
        "ffn1_norm": gain(ks[1], (L, D_MODEL)),
        "ffn1_w_gate": dense(ks[2], (L, D_MODEL, D_FF), D_MODEL),
        "ffn1_w_up": dense(ks[3], (L, D_MODEL, D_FF), D_MODEL),
        "ffn1_w_down": dense(ks[4], (L, D_FF, D_MODEL), D_FF),
        "mix_norm": gain(ks[5], (L, D_MODEL)),
        "w_in": dense(ks[6], (L, D_MODEL, IN_WIDTH), D_MODEL),
        "cmp_pos": 0.1 * jax.random.normal(ks[7], (L, CMP_LEN, HEAD_DIM), jnp.float32),
        "cmp_k_w1": dense(ks[8], (L, CMP_LEN * HEAD_DIM, CMP_HIDDEN), CMP_LEN * HEAD_DIM),
        "cmp_k_w2": dense(ks[9], (L, CMP_HIDDEN, HEAD_DIM), CMP_HIDDEN),
        "cmp_v_w1": dense(ks[10], (L, CMP_LEN * HEAD_DIM, CMP_HIDDEN), CMP_LEN * HEAD_DIM),
        "cmp_v_w2": dense(ks[11], (L, CMP_HIDDEN, HEAD_DIM), CMP_HIDDEN),
        "pool_w": dense(ks[12], (L, len(POOL_WINDOWS), POOL_GROUP_DIM, POOL_GROUP_DIM), POOL_GROUP_DIM),
        "pool_scale": 1.0 + 0.1 * jax.random.normal(ks[13], (L, POOL_WIDTH), jnp.float32),
        "w_branch_pool": dense(ks[14], (L, POOL_WIDTH, D_MODEL), POOL_WIDTH),
        "w_branch_nsa": dense(ks[15], (L, Q_WIDTH, D_MODEL), Q_WIDTH),
        "w_out": dense(ks[16], (L, D_MODEL, D_MODEL), D_MODEL),
        "ffn2_norm": gain(ks[17], (L, D_MODEL)),
        "ffn2_w_gate": dense(ks[18], (L, D_MODEL, D_FF), D_MODEL),
        "ffn2_w_up": dense(ks[19], (L, D_MODEL, D_FF), D_MODEL),
        "ffn2_w_down": dense(ks[20], (L, D_FF, D_MODEL), D_FF),
        "final_norm": gain(ks[21], (D_MODEL,)),
    }


def reference(x, ffn1_norm, ffn1_w_gate, ffn1_w_up, ffn1_w_down, mix_norm, w_in, cmp_pos,
              cmp_k_w1, cmp_k_w2, cmp_v_w1, cmp_v_w2, pool_w, pool_scale, w_branch_pool,
              w_branch_nsa, w_out, ffn2_norm, ffn2_w_gate, ffn2_w_up, ffn2_w_down, final_norm):
    B, T, _ = x.shape
    split_points = list(np.cumsum(IN_SPLITS)[:-1])
    q_scale = HEAD_DIM ** -0.5
    for l in range(DEPTH):
        x = x + 0.5 * swiglu(rmsnorm(x, ffn1_norm[l]), ffn1_w_gate[l], ffn1_w_up[l], ffn1_w_down[l])

        h = rmsnorm(x, mix_norm[l])
        proj = h @ w_in[l]
        (q, kc, vc, ksl, vsl, kwn, vwn, g_nsa, u_pool, g_merge) = jnp.split(proj, split_points, axis=-1)
        kv_shape = (B, T, N_KV_GROUPS, HEAD_DIM)
        q = (q * q_scale).reshape(B, T, N_KV_GROUPS, HEADS_PER_GROUP, HEAD_DIM)
        k_cmp = compress_tokens(kc.reshape(kv_shape), cmp_pos[l], cmp_k_w1[l], cmp_k_w2[l])
        v_cmp = compress_tokens(vc.reshape(kv_shape), cmp_pos[l], cmp_v_w1[l], cmp_v_w2[l])
        nsa_gates = jax.nn.sigmoid(g_nsa).reshape(B, T, N_KV_GROUPS, HEADS_PER_GROUP, 3)
        o_nsa = nsa_attention(q, k_cmp, v_cmp, ksl.reshape(kv_shape), vsl.reshape(kv_shape),
                              kwn.reshape(kv_shape), vwn.reshape(kv_shape), nsa_gates)
        o_pool = multiscale_pool(u_pool, pool_w[l], pool_scale[l])

        g_pool, g_attn = jnp.split(jax.nn.sigmoid(g_merge), 2, axis=-1)
        merged = g_pool * (o_pool @ w_branch_pool[l]) + g_attn * (o_nsa @ w_branch_nsa[l])
        x = x + merged @ w_out[l]

        x = x + 0.5 * swiglu(rmsnorm(x, ffn2_norm[l]), ffn2_w_gate[l], ffn2_w_up[l], ffn2_w_down[l])
    return rmsnorm(x, final_norm)
```

```python
import functools

import jax
import jax.numpy as jnp
import numpy as np
from jax import lax
from jax.experimental import pallas as pl
from jax.experimental.pallas import tpu as pltpu

D_FF = 2816
N_HEADS = 16
N_GROUPS = 2
HEADS_PER_GROUP = N_HEADS // N_GROUPS
HEAD_DIM = 64
CMP_LEN = 32
CMP_STRIDE = 16
SEL_LEN = 64
N_SELECT = 16
WINDOW = 512
POOL_WINDOWS = (2, 4, 8, 16)
POOL_GROUP_DIM = 128
RMS_EPS = 1e-6
ALIBI_MAX_BIAS = 8.0

LANES = 128
NEG_BIG = -1e30
TOPK_BIG = 1e30
VMEM_LIMIT = 56 * 1024 * 1024

BF16 = jnp.bfloat16
F32 = jnp.float32


def _dot(a, b):
    return jnp.dot(a, b, preferred_element_type=F32)


def _dot_nt(a, b):
    return lax.dot_general(a, b, (((1,), (1,)), ((), ())), preferred_element_type=F32)


def _rms(x, g):
    return x * lax.rsqrt(jnp.mean(x * x, axis=-1, keepdims=True) + RMS_EPS) * g


def _params(*sem):
    return pltpu.CompilerParams(dimension_semantics=sem, vmem_limit_bytes=VMEM_LIMIT)


def _ffn_kernel(x_ref, g_ref, wg_ref, wu_ref, wd_ref, fin_ref, o_ref, xn_ref, acc_ref, *, final_norm):
    j = pl.program_id(1)

    @pl.when(j == 0)
    def _():
        xn_ref[...] = _rms(x_ref[...], g_ref[...]).astype(BF16)
        acc_ref[...] = jnp.zeros_like(acc_ref)

    xn = xn_ref[...]
    gate = _dot(xn, wg_ref[...])
    up = _dot(xn, wu_ref[...])
    act = (gate * jax.nn.sigmoid(gate)) * up
    acc_ref[...] += _dot(act.astype(BF16), wd_ref[...])

    @pl.when(j == pl.num_programs(1) - 1)
    def _():
        y = x_ref[...] + 0.5 * acc_ref[...]
        if final_norm:
            y = _rms(y, fin_ref[...])
        o_ref[...] = y


def _ffn(x, norm_g, wg, wu, wd, fin_g, *, final_norm, tm=512, tf=256):
    n, d = x.shape
    grid = (n // tm, D_FF // tf)
    return pl.pallas_call(
        functools.partial(_ffn_kernel, final_norm=final_norm),
        out_shape=jax.ShapeDtypeStruct((n, d), F32),
        grid=grid,
        in_specs=[
            pl.BlockSpec((tm, d), lambda i, j: (i, 0)),
            pl.BlockSpec((1, d), lambda i, j: (0, 0)),
            pl.BlockSpec((d, tf), lambda i, j: (0, j)),
            pl.BlockSpec((d, tf), lambda i, j: (0, j)),
            pl.BlockSpec((tf, d), lambda i, j: (j, 0)),
            pl.BlockSpec((1, d), lambda i, j: (0, 0)),
        ],
        out_specs=pl.BlockSpec((tm, d), lambda i, j: (i, 0)),
        scratch_shapes=[pltpu.VMEM((tm, d), BF16), pltpu.VMEM((tm, d), F32)],
        compiler_params=_params("parallel", "arbitrary"),
        name="ffn",
    )(x, norm_g, wg, wu, wd, fin_g)


Q_COLS = N_HEADS * LANES
KV_COLS = 6 * LANES
GATE_OFF = Q_COLS + KV_COLS
POOL_OFF = GATE_OFF + LANES
POOL_COLS = len(POOL_WINDOWS) * POOL_GROUP_DIM


def _proj_kernel(x_ref, g_ref, w_ref, q_ref, kc_ref, vc_ref, ks_ref, vs_ref, kw_ref, vw_ref,
                 gate_ref, u_ref, gm_ref):
    d = x_ref.shape[1]
    hn = _rms(x_ref[...], g_ref[...]).astype(BF16)
    q_scale = HEAD_DIM ** -0.5
    for h in range(N_HEADS):
        q_ref[0, h] = (_dot(hn, w_ref[:, h * LANES:(h + 1) * LANES]) * q_scale).astype(BF16)
    kv = _dot(hn, w_ref[:, Q_COLS:Q_COLS + KV_COLS])
    kc_ref[...] = kv[:, 0 * LANES:1 * LANES]
    vc_ref[...] = kv[:, 1 * LANES:2 * LANES]
    ks_ref[...] = kv[:, 2 * LANES:3 * LANES].astype(BF16)
    vs_ref[...] = kv[:, 3 * LANES:4 * LANES].astype(BF16)
    kw_ref[...] = kv[:, 4 * LANES:5 * LANES].astype(BF16)
    vw_ref[...] = kv[:, 5 * LANES:6 * LANES].astype(BF16)
    gate_ref[...] = jax.nn.sigmoid(_dot(hn, w_ref[:, GATE_OFF:GATE_OFF + LANES]))
    u_ref[...] = _dot(hn, w_ref[:, POOL_OFF:POOL_OFF + POOL_COLS])
    gm_ref[...] = jax.nn.sigmoid(_dot(hn, w_ref[:, POOL_OFF + POOL_COLS:POOL_OFF + POOL_COLS + 2 * d]))


def _pack_w_in(w_in, d):
    qw = N_HEADS * HEAD_DIM
    kvw = N_GROUPS * HEAD_DIM
    wq = w_in[:, :qw].reshape(d, N_GROUPS, HEADS_PER_GROUP, HEAD_DIM)
    zeros = jnp.zeros_like(wq)
    wq_pad = jnp.stack([jnp.concatenate([wq[:, 0], zeros[:, 0]], axis=-1),
                        jnp.concatenate([zeros[:, 1], wq[:, 1]], axis=-1)], axis=1).reshape(d, Q_COLS)
    off = qw
    w_kv = w_in[:, off:off + 6 * kvw]
    off += 6 * kvw
    n_g = 3 * N_HEADS
    w_g = jnp.pad(w_in[:, off:off + n_g], ((0, 0), (0, LANES - n_g)))
    off += n_g
    w_rest = w_in[:, off:]
    return jnp.concatenate([wq_pad, w_kv, w_g, w_rest], axis=1).astype(BF16)


def _proj(x, norm_g, w_packed, batch, seq, *, tm=256):
    n, d = x.shape
    tiles_per_seq = seq // tm
    row = lambda i: (i, 0)
    const = lambda i: (0, 0)
    out_shape = [
        jax.ShapeDtypeStruct((batch, N_HEADS, seq, LANES), BF16),
        jax.ShapeDtypeStruct((n, LANES), F32), jax.ShapeDtypeStruct((n, LANES), F32),
        jax.ShapeDtypeStruct((n, LANES), BF16), jax.ShapeDtypeStruct((n, LANES), BF16),
        jax.ShapeDtypeStruct((n, LANES), BF16), jax.ShapeDtypeStruct((n, LANES), BF16),
        jax.ShapeDtypeStruct((n, LANES), F32),
        jax.ShapeDtypeStruct((n, POOL_COLS), F32),
        jax.ShapeDtypeStruct((n, 2 * d), F32),
    ]
    out_specs = [
        pl.BlockSpec((1, N_HEADS, tm, LANES), lambda i: (i // tiles_per_seq, 0, i % tiles_per_seq, 0)),
        pl.BlockSpec((tm, LANES), row), pl.BlockSpec((tm, LANES), row),
        pl.BlockSpec((tm, LANES), row), pl.BlockSpec((tm, LANES), row),
        pl.BlockSpec((tm, LANES), row), pl.BlockSpec((tm, LANES), row),
        pl.BlockSpec((tm, LANES), row),
        pl.BlockSpec((tm, POOL_COLS), row),
        pl.BlockSpec((tm, 2 * d), row),
    ]
    return pl.pallas_call(
        _proj_kernel,
        out_shape=out_shape,
        grid=(n // tm,),
        in_specs=[pl.BlockSpec((tm, d), row), pl.BlockSpec((1, d), const),
                  pl.BlockSpec(w_packed.shape, const)],
        out_specs=out_specs,
        compiler_params=_params("parallel"),
        name="proj",
    )(x, norm_g, w_packed)


def _compress_kernel(ck_ref, cv_ref, pos_ref, wk1_ref, wk2_ref, wv1_ref, wv2_ref, ko_ref, vo_ref):
    half = pos_ref.shape[1]
    n_chunk = ck_ref.shape[2]
    pos_a = pos_ref[0:1, :]
    pos_b = pos_ref[1:2, :]

    def compress(c_ref, w1_ref, w2_ref):
        outs = []
        for g in range(N_GROUPS):
            c = c_ref[0, g]
            first = _dot((c + pos_a).astype(BF16), w1_ref[0:half, :])
            second = _dot((c + pos_b).astype(BF16), w1_ref[half:2 * half, :])
            pre = first + pltpu.roll(second, n_chunk - 1, 0)
            outs.append(_dot(jax.nn.gelu(pre).astype(BF16), w2_ref[...]))
        return jnp.concatenate(outs, axis=-1).astype(BF16)

    ko_ref[0] = compress(ck_ref, wk1_ref, wk2_ref)
    vo_ref[0] = compress(cv_ref, wv1_ref, wv2_ref)


def _compress(ck, cv, pos2, wk1, wk2, wv1, wv2):
    batch, _, n_chunk, width = ck.shape
    blk = pl.BlockSpec((1, N_GROUPS, n_chunk, width), lambda b: (b, 0, 0, 0))
    full = lambda a: pl.BlockSpec(a.shape, lambda b: (0,) * a.ndim)
    out = jax.ShapeDtypeStruct((batch, n_chunk, LANES), BF16)
    return pl.pallas_call(
        _compress_kernel,
        out_shape=[out, out],
        grid=(batch,),
        in_specs=[blk, blk, full(pos2), full(wk1), full(wk2), full(wv1), full(wv2)],
        out_specs=[pl.BlockSpec((1, n_chunk, LANES), lambda b: (b, 0, 0))] * 2,
        compiler_params=_params("parallel"),
        name="compress",
    )(ck, cv, pos2, wk1, wk2, wv1, wv2)


SLC_CHUNK = 256


def _slope(head):
    return float(2.0 ** (-ALIBI_MAX_BIAS * (head + 1) / N_HEADS))


def _nsa_kernel(q_ref, kc_ref, vc_ref, ks_ref, vs_ref, kw_ref, vw_ref, gate_ref, ovt_ref, o_ref,
                s_ref, p_ref, sel_ref, m_ref, l_ref, acc_ref, out_ref, *, tq, seq):
    n_sel = seq // SEL_LEN
    n_cmp = kc_ref.shape[1]
    n_win = WINDOW + tq
    q0 = pl.program_id(1) * tq
    q_all = q_ref[0].reshape(N_HEADS * tq, LANES)
    rows = lambda h: slice(h * tq, (h + 1) * tq)

    def gate_col(head, branch):
        c = 3 * head + branch
        return gate_ref[:, c:c + 1]

    def softmax_branch(n_keys, dist, mask, branch, collect):
        scales = []
        for head in range(N_HEADS):
            s = s_ref[rows(head), 0:n_keys] - _slope(head) * dist
            s = jnp.where(mask, s, NEG_BIG)
            p = jnp.where(mask, jnp.exp(s - jnp.max(s, axis=-1, keepdims=True)), 0.0)
            inv = 1.0 / jnp.maximum(jnp.sum(p, axis=-1, keepdims=True), 1e-30)
            p_ref[rows(head), 0:n_keys] = p.astype(BF16)
            if collect is not None:
                collect(head, p * inv)
            scales.append(inv * gate_col(head, branch))
        return scales

    def add_out(pv, scales, first):
        for head in range(N_HEADS):
            contrib = pv[rows(head)] * scales[head]
            out_ref[rows(head), :] = contrib if first else out_ref[rows(head), :] + contrib

    s_ref[:, 0:n_cmp] = _dot_nt(q_all, kc_ref[0])
    t_c = q0 + lax.broadcasted_iota(jnp.int32, (tq, n_cmp), 0)
    end_c = lax.broadcasted_iota(jnp.int32, (tq, n_cmp), 1) * CMP_STRIDE + (CMP_LEN - 1)
    dist_c = (t_c - end_c).astype(F32)
    p_sum = [None] * N_GROUPS

    def collect(head, p_norm):
        g = head // HEADS_PER_GROUP
        p_sum[g] = p_norm if p_sum[g] is None else p_sum[g] + p_norm

    scales = softmax_branch(n_cmp, dist_c, dist_c >= 0, 0, collect)
    add_out(_dot(p_ref[:, 0:n_cmp], vc_ref[0]), scales, True)

    blk = lax.broadcasted_iota(jnp.int32, (n_sel, tq), 0)
    cur = (q0 + lax.broadcasted_iota(jnp.int32, (n_sel, tq), 1)) // SEL_LEN
    valid = blk <= cur
    forced = valid & ((blk == 0) | (blk == cur) | (blk == cur - 1))
    blk_f = blk.astype(F32)
    ovt = ovt_ref[...]
    for g in range(N_GROUPS):
        p1 = p_sum[g].astype(BF16)
        r1 = p_sum[g] - p1.astype(F32)
        p2 = r1.astype(BF16)
        p3 = (r1 - p2.astype(F32)).astype(BF16)
        score = _dot_nt(ovt, p1) + _dot_nt(ovt, p2) + _dot_nt(ovt, p3)
        score = jnp.where(forced, TOPK_BIG, jnp.where(valid, score, -TOPK_BIG))
        sel = jnp.zeros((n_sel, tq), F32)
        for _ in range(min(N_SELECT, n_sel)):
            best = jnp.max(score, axis=0, keepdims=True)
            pick = jnp.min(jnp.where(score == best, blk_f, float(n_sel)), axis=0, keepdims=True)
            hit = blk_f == pick
            sel = jnp.where(hit, 1.0, sel)
            score = jnp.where(hit, -TOPK_BIG, score)
        sel_ref[g] = sel.T.astype(BF16)

    w0 = pl.multiple_of(jnp.maximum(q0 - WINDOW, 0), tq)
    s_ref[:, 0:n_win] = _dot_nt(q_all, kw_ref[0, pl.ds(w0, n_win), :])
    t_w = q0 + lax.broadcasted_iota(jnp.int32, (tq, n_win), 0)
    dist_w_i = t_w - (w0 + lax.broadcasted_iota(jnp.int32, (tq, n_win), 1))
    mask_w = (dist_w_i >= 0) & (dist_w_i < WINDOW)
    scales = softmax_branch(n_win, dist_w_i.astype(F32), mask_w, 2, None)
    add_out(_dot(p_ref[:, 0:n_win], vw_ref[0, pl.ds(w0, n_win), :]), scales, False)

    m_ref[...] = jnp.full_like(m_ref, NEG_BIG)
    l_ref[...] = jnp.zeros_like(l_ref)
    acc_ref[...] = jnp.zeros_like(acc_ref)
    t_s = q0 + lax.broadcasted_iota(jnp.int32, (tq, SLC_CHUNK), 0)
    lane_s = lax.broadcasted_iota(jnp.int32, (tq, SLC_CHUNK), 1)
    e_row = lax.broadcasted_iota(jnp.int32, (n_sel, SLC_CHUNK), 0)
    e_lane = lax.broadcasted_iota(jnp.int32, (n_sel, SLC_CHUNK), 1)

    def chunk_body(c, carry):
        k0 = pl.multiple_of(c * SLC_CHUNK, SLC_CHUNK)
        s_ref[:, 0:SLC_CHUNK] = _dot_nt(q_all, ks_ref[0, pl.ds(k0, SLC_CHUNK), :])
        dist_i = t_s - (k0 + lane_s)
        dist = dist_i.astype(F32)
        expand = jnp.where(e_row == (k0 + e_lane) // SEL_LEN, 1.0, 0.0).astype(BF16)
        for g in range(N_GROUPS):
            sel_keys = _dot(sel_ref[g], expand)
            mask = jnp.where(dist_i >= 0, sel_keys, 0.0) > 0.5
            for h in range(HEADS_PER_GROUP):
                head = g * HEADS_PER_GROUP + h
                s = s_ref[rows(head), 0:SLC_CHUNK] - _slope(head) * dist
                s = jnp.where(mask, s, NEG_BIG)
                m_old = m_ref[head]
                m_new = jnp.maximum(m_old, jnp.max(s, axis=-1, keepdims=True))
                alpha = jnp.exp(m_old - m_new)
                p = jnp.where(mask, jnp.exp(s - m_new), 0.0)
                l_ref[head] = alpha * l_ref[head] + jnp.sum(p, axis=-1, keepdims=True)
                m_ref[head] = m_new
                p_ref[rows(head), 0:SLC_CHUNK] = p.astype(BF16)
                acc_ref[rows(head), :] = alpha * acc_ref[rows(head), :]
        acc_ref[...] += _dot(p_ref[:, 0:SLC_CHUNK], vs_ref[0, pl.ds(k0, SLC_CHUNK), :])
        return carry

    lax.fori_loop(0, (q0 + tq + SLC_CHUNK - 1) // SLC_CHUNK, chunk_body, 0)
    scales = [gate_col(head, 1) / jnp.maximum(l_ref[head], 1e-30) for head in range(N_HEADS)]
    add_out(acc_ref[...], scales, False)

    lane = lax.broadcasted_iota(jnp.int32, (tq, LANES), 1)
    for h in range(HEADS_PER_GROUP):
        pair = jnp.where(lane < HEAD_DIM, out_ref[rows(h), :], out_ref[rows(HEADS_PER_GROUP + h), :])
        o_ref[:, h * LANES:(h + 1) * LANES] = pair.astype(BF16)


def _overlap_t(seq):
    n_cmp = (seq - CMP_LEN) // CMP_STRIDE + 1
    n_sel = seq // SEL_LEN
    cs = np.arange(n_cmp)[:, None] * CMP_STRIDE
    ss = np.arange(n_sel)[None, :] * SEL_LEN
    ov = np.clip(np.minimum(cs + CMP_LEN, ss + SEL_LEN) - np.maximum(cs, ss), 0, None) / CMP_LEN
    out = np.zeros((n_sel, seq // CMP_STRIDE), np.float32)
    out[:, :n_cmp] = ov.T
    return jnp.asarray(out, dtype=BF16)


def _nsa(q, kc, vc, ks, vs, kw, vw, gates, *, tq=128):
    batch, _, seq, _ = q.shape
    n_cmp = kc.shape[1]
    n_sel = seq // SEL_LEN
    assert seq % SLC_CHUNK == 0 and seq >= WINDOW + tq and SLC_CHUNK % tq == 0
    ovt = _overlap_t(seq)
    s_cols = max(n_cmp, WINDOW + tq, SLC_CHUNK)
    per_b = lambda rows_: pl.BlockSpec((1, rows_, LANES), lambda b, i: (b, 0, 0))
    tiles = seq // tq
    return pl.pallas_call(
        functools.partial(_nsa_kernel, tq=tq, seq=seq),
        out_shape=jax.ShapeDtypeStruct((batch * seq, HEADS_PER_GROUP * LANES), BF16),
        grid=(batch, tiles),
        in_specs=[
            pl.BlockSpec((1, N_HEADS, tq, LANES), lambda b, i: (b, 0, i, 0)),
            per_b(n_cmp), per_b(n_cmp), per_b(seq), per_b(seq), per_b(seq), per_b(seq),
            pl.BlockSpec((tq, LANES), lambda b, i: (b * tiles + i, 0)),
            pl.BlockSpec(ovt.shape, lambda b, i: (0, 0)),
        ],
        out_specs=pl.BlockSpec((tq, HEADS_PER_GROUP * LANES), lambda b, i: (b * tiles + i, 0)),
        scratch_shapes=[
            pltpu.VMEM((N_HEADS * tq, s_cols), F32),
            pltpu.VMEM((N_HEADS * tq, s_cols), BF16),
            pltpu.VMEM((N_GROUPS, tq, n_sel), BF16),
            pltpu.VMEM((N_HEADS, tq, 1), F32),
            pltpu.VMEM((N_HEADS, tq, 1), F32),
            pltpu.VMEM((N_HEADS * tq, LANES), F32),
            pltpu.VMEM((N_HEADS * tq, LANES), F32),
        ],
        compiler_params=_params("parallel", "arbitrary"),
        name="nsa",
    )(q, kc, vc, ks, vs, kw, vw, gates, ovt)


HALO = max(POOL_WINDOWS)


def _merge_kernel(x_ref, u_ref, halo_ref, gm_ref, on_ref, pw_ref, ps_ref, wbp_ref, wbn_ref, wo_ref,
                  o_ref, ext_ref, *, tm, seq):
    d = x_ref.shape[1]
    pos0 = (pl.program_id(0) * tm) % seq
    ext_ref[0:HALO, :] = jnp.where(pos0 == 0, 0.0, halo_ref[...])
    ext_ref[HALO:HALO + tm, :] = u_ref[...]
    pos = (pos0 + lax.broadcasted_iota(jnp.int32, (tm, POOL_GROUP_DIM), 0)).astype(F32)
    mixed = []
    for gi, w in enumerate(POOL_WINDOWS):
        cols = slice(gi * POOL_GROUP_DIM, (gi + 1) * POOL_GROUP_DIM)
        u = ext_ref[HALO:HALO + tm, cols]
        total = u
        for lag in range(1, w):
            total = total + ext_ref[HALO - lag:HALO - lag + tm, cols]
        delta = total / jnp.minimum(pos + 1.0, float(w)) - u
        mixed.append(_dot(delta.astype(BF16), pw_ref[gi]) * ps_ref[:, cols])
    mixed = jnp.concatenate(mixed, axis=-1).astype(BF16)
    a = _dot(mixed, wbp_ref[...])
    b = _dot(on_ref[...], wbn_ref[...])
    merged = gm_ref[:, 0:d] * a + gm_ref[:, d:2 * d] * b
    o_ref[...] = x_ref[...] + _dot(merged.astype(BF16), wo_ref[...])


def _merge(x, u, gm, o_nsa, pool_w, pool_scale, w_bp, w_bn, w_out, seq, *, tm=256):
    n, d = x.shape
    row = lambda i: (i, 0)
    full = lambda a: pl.BlockSpec(a.shape, lambda i: (0,) * a.ndim)
    halo_blocks = tm // HALO
    return pl.pallas_call(
        functools.partial(_merge_kernel, tm=tm, seq=seq),
        out_shape=jax.ShapeDtypeStruct((n, d), F32),
        grid=(n // tm,),
        in_specs=[
            pl.BlockSpec((tm, d), row),
            pl.BlockSpec((tm, POOL_COLS), row),
            pl.BlockSpec((HALO, POOL_COLS), lambda i: (jnp.maximum(i * halo_blocks - 1, 0), 0)),
            pl.BlockSpec((tm, 2 * d), row),
            pl.BlockSpec((tm, d), row),
            full(pool_w), full(pool_scale), full(w_bp), full(w_bn), full(w_out),
        ],
        out_specs=pl.BlockSpec((tm, d), row),
        scratch_shapes=[pltpu.VMEM((HALO + tm, POOL_COLS), F32)],
        compiler_params=_params("parallel"),
        name="merge",
    )(x, u, u, gm, o_nsa, pool_w, pool_scale, w_bp, w_bn, w_out)


def _chunked(kv, batch, seq):
    x = kv.reshape(batch, seq // CMP_STRIDE, CMP_STRIDE, N_GROUPS, HEAD_DIM)
    return x.transpose(0, 3, 1, 2, 4).reshape(batch, N_GROUPS, seq // CMP_STRIDE, CMP_STRIDE * HEAD_DIM)


def kernel(x, ffn1_norm, ffn1_w_gate, ffn1_w_up, ffn1_w_down, mix_norm, w_in, cmp_pos, cmp_k_w1, cmp_k_w2, cmp_v_w1, cmp_v_w2, pool_w, pool_scale, w_branch_pool, w_branch_nsa, w_out, ffn2_norm, ffn2_w_gate, ffn2_w_up, ffn2_w_down, final_norm):
    batch, seq, d = x.shape
    depth = w_in.shape[0]
    xf = x.reshape(batch * seq, d)
    bf = lambda a: a.astype(BF16)
    row = lambda a: a.reshape(1, -1)
    for l in range(depth):
        xf = _ffn(xf, row(ffn1_norm[l]), bf(ffn1_w_gate[l]), bf(ffn1_w_up[l]), bf(ffn1_w_down[l]),
                  row(final_norm), final_norm=False)

        q, kc, vc, ks, vs, kw, vw, gates, u, gm = _proj(xf, row(mix_norm[l]), _pack_w_in(w_in[l], d), batch, seq)
        pos2 = cmp_pos[l].reshape(2, (CMP_LEN // 2) * HEAD_DIM)
        k_cmp, v_cmp = _compress(_chunked(kc, batch, seq), _chunked(vc, batch, seq), pos2,
                                 bf(cmp_k_w1[l]), bf(cmp_k_w2[l]), bf(cmp_v_w1[l]), bf(cmp_v_w2[l]))
        per_b = lambda a: a.reshape(batch, seq, LANES)
        o_nsa = _nsa(q, k_cmp, v_cmp, per_b(ks), per_b(vs), per_b(kw), per_b(vw), gates)

        w_bn = w_branch_nsa[l].reshape(N_GROUPS, HEADS_PER_GROUP, HEAD_DIM, d).transpose(1, 0, 2, 3).reshape(-1, d)
        xf = _merge(xf, u, gm, o_nsa, bf(pool_w[l]), row(pool_scale[l]), bf(w_branch_pool[l]), bf(w_bn),
                    bf(w_out[l]), seq)

        xf = _ffn(xf, row(ffn2_norm[l]), bf(ffn2_w_gate[l]), bf(ffn2_w_up[l]), bf(ffn2_w_down[l]),
                  row(final_norm), final_norm=(l == depth - 1))
    return xf.reshape(batch, seq, d)
```

```python
import functools

import jax
import jax.numpy as jnp
import numpy as np
from jax import lax
from jax.experimental import pallas as pl
from jax.experimental.pallas import tpu as pltpu

D_FF = 2816
N_HEADS = 16
N_GROUPS = 2
HEADS_PER_GROUP = N_HEADS // N_GROUPS
HEAD_DIM = 64
CMP_LEN = 32
CMP_STRIDE = 16
SEL_LEN = 64
N_SELECT = 16
WINDOW = 512
POOL_WINDOWS = (2, 4, 8, 16)
POOL_GROUP_DIM = 128
RMS_EPS = 1e-6
ALIBI_MAX_BIAS = 8.0

LANES = 128
NEG_BIG = -1e30
TOPK_BIG = 1e30
VMEM_LIMIT = 56 * 1024 * 1024

BF16 = jnp.bfloat16
F32 = jnp.float32


def _dot(a, b):
    return jnp.dot(a, b, preferred_element_type=F32)


def _dot_nt(a, b):
    return lax.dot_general(a, b, (((1,), (1,)), ((), ())), preferred_element_type=F32)


def _rms(x, g):
    return x * lax.rsqrt(jnp.mean(x * x, axis=-1, keepdims=True) + RMS_EPS) * g


def _params(*sem):
    return pltpu.CompilerParams(dimension_semantics=sem, vmem_limit_bytes=VMEM_LIMIT)


def _ffn_kernel(x_ref, g_ref, wg_ref, wu_ref, wd_ref, fin_ref, o_ref, xn_ref, acc_ref, *, final_norm):
    j = pl.program_id(1)

    @pl.when(j == 0)
    def _():
        xn_ref[...] = _rms(x_ref[...], g_ref[...]).astype(BF16)
        acc_ref[...] = jnp.zeros_like(acc_ref)

    xn = xn_ref[...]
    gate = _dot(xn, wg_ref[...])
    up = _dot(xn, wu_ref[...])
    act = (gate * jax.nn.sigmoid(gate)) * up
    acc_ref[...] += _dot(act.astype(BF16), wd_ref[...])

    @pl.when(j == pl.num_programs(1) - 1)
    def _():
        y = x_ref[...] + 0.5 * acc_ref[...]
        if final_norm:
            y = _rms(y, fin_ref[...])
        o_ref[...] = y


def _ffn(x, norm_g, wg, wu, wd, fin_g, *, final_norm, tm=512, tf=256):
    n, d = x.shape
    grid = (n // tm, D_FF // tf)
    return pl.pallas_call(
        functools.partial(_ffn_kernel, final_norm=final_norm),
        out_shape=jax.ShapeDtypeStruct((n, d), F32),
        grid=grid,
        in_specs=[
            pl.BlockSpec((tm, d), lambda i, j: (i, 0)),
            pl.BlockSpec((1, d), lambda i, j: (0, 0)),
            pl.BlockSpec((d, tf), lambda i, j: (0, j)),
            pl.BlockSpec((d, tf), lambda i, j: (0, j)),
            pl.BlockSpec((tf, d), lambda i, j: (j, 0)),
            pl.BlockSpec((1, d), lambda i, j: (0, 0)),
        ],
        out_specs=pl.BlockSpec((tm, d), lambda i, j: (i, 0)),
        scratch_shapes=[pltpu.VMEM((tm, d), BF16), pltpu.VMEM((tm, d), F32)],
        compiler_params=_params("parallel", "arbitrary"),
        name="ffn",
    )(x, norm_g, wg, wu, wd, fin_g)


Q_COLS = N_HEADS * LANES
KV_COLS = 6 * LANES
GATE_OFF = Q_COLS + KV_COLS
POOL_OFF = GATE_OFF + LANES
POOL_COLS = len(POOL_WINDOWS) * POOL_GROUP_DIM


def _proj_kernel(x_ref, g_ref, w_ref, q_ref, kc_ref, vc_ref, ks_ref, vs_ref, kw_ref, vw_ref,
                 gate_ref, u_ref, gm_ref):
    d = x_ref.shape[1]
    hn = _rms(x_ref[...], g_ref[...]).astype(BF16)
    q_scale = HEAD_DIM ** -0.5
    for h in range(N_HEADS):
        q_ref[0, h] = (_dot(hn, w_ref[:, h * LANES:(h + 1) * LANES]) * q_scale).astype(BF16)
    kv = _dot(hn, w_ref[:, Q_COLS:Q_COLS + KV_COLS])
    kc_ref[...] = kv[:, 0 * LANES:1 * LANES]
    vc_ref[...] = kv[:, 1 * LANES:2 * LANES]
    ks_ref[...] = kv[:, 2 * LANES:3 * LANES].astype(BF16)
    vs_ref[...] = kv[:, 3 * LANES:4 * LANES].astype(BF16)
    kw_ref[...] = kv[:, 4 * LANES:5 * LANES].astype(BF16)
    vw_ref[...] = kv[:, 5 * LANES:6 * LANES].astype(BF16)
    gate_ref[...] = jax.nn.sigmoid(_dot(hn, w_ref[:, GATE_OFF:GATE_OFF + LANES]))
    u_ref[...] = _dot(hn, w_ref[:, POOL_OFF:POOL_OFF + POOL_COLS])
    gm_ref[...] = jax.nn.sigmoid(_dot(hn, w_ref[:, POOL_OFF + POOL_COLS:POOL_OFF + POOL_COLS + 2 * d]))


def _pack_w_in(w_in, d):
    qw = N_HEADS * HEAD_DIM
    kvw = N_GROUPS * HEAD_DIM
    wq = w_in[:, :qw].reshape(d, N_GROUPS, HEADS_PER_GROUP, HEAD_DIM)
    zeros = jnp.zeros_like(wq)
    wq_pad = jnp.stack([jnp.concatenate([wq[:, 0], zeros[:, 0]], axis=-1),
                        jnp.concatenate([zeros[:, 1], wq[:, 1]], axis=-1)], axis=1).reshape(d, Q_COLS)
    off = qw
    w_kv = w_in[:, off:off + 6 * kvw]
    off += 6 * kvw
    n_g = 3 * N_HEADS
    w_g = jnp.pad(w_in[:, off:off + n_g], ((0, 0), (0, LANES - n_g)))
    off += n_g
    w_rest = w_in[:, off:]
    return jnp.concatenate([wq_pad, w_kv, w_g, w_rest], axis=1).astype(BF16)


def _proj(x, norm_g, w_packed, batch, seq, *, tm=256):
    n, d = x.shape
    tiles_per_seq = seq // tm
    row = lambda i: (i, 0)
    const = lambda i: (0, 0)
    out_shape = [
        jax.ShapeDtypeStruct((batch, N_HEADS, seq, LANES), BF16),
        jax.ShapeDtypeStruct((n, LANES), F32), jax.ShapeDtypeStruct((n, LANES), F32),
        jax.ShapeDtypeStruct((n, LANES), BF16), jax.ShapeDtypeStruct((n, LANES), BF16),
        jax.ShapeDtypeStruct((n, LANES), BF16), jax.ShapeDtypeStruct((n, LANES), BF16),
        jax.ShapeDtypeStruct((n, LANES), F32),
        jax.ShapeDtypeStruct((n, POOL_COLS), F32),
        jax.ShapeDtypeStruct((n, 2 * d), F32),
    ]
    out_specs = [
        pl.BlockSpec((1, N_HEADS, tm, LANES), lambda i: (i // tiles_per_seq, 0, i % tiles_per_seq, 0)),
        pl.BlockSpec((tm, LANES), row), pl.BlockSpec((tm, LANES), row),
        pl.BlockSpec((tm, LANES), row), pl.BlockSpec((tm, LANES), row),
        pl.BlockSpec((tm, LANES), row), pl.BlockSpec((tm, LANES), row),
        pl.BlockSpec((tm, LANES), row),
        pl.BlockSpec((tm, POOL_COLS), row),
        pl.BlockSpec((tm, 2 * d), row),
    ]
    return pl.pallas_call(
        _proj_kernel,
        out_shape=out_shape,
        grid=(n // tm,),
        in_specs=[pl.BlockSpec((tm, d), row), pl.BlockSpec((1, d), const),
                  pl.BlockSpec(w_packed.shape, const)],
        out_specs=out_specs,
        compiler_params=_params("parallel"),
        name="proj",
    )(x, norm_g, w_packed)


def _compress_kernel(ck_ref, cv_ref, pos_ref, wk1_ref, wk2_ref, wv1_ref, wv2_ref, ko_ref, vo_ref):
    half = pos_ref.shape[1]
    n_chunk = ck_ref.shape[2]
    pos_a = pos_ref[0:1, :]
    pos_b = pos_ref[1:2, :]

    def compress(c_ref, w1_ref, w2_ref):
        outs = []
        for g in range(N_GROUPS):
            c = c_ref[0, g]
            first = _dot((c + pos_a).astype(BF16), w1_ref[0:half, :])
            second = _dot((c + pos_b).astype(BF16), w1_ref[half:2 * half, :])
            pre = first + pltpu.roll(second, n_chunk - 1, 0)
            outs.append(_dot(jax.nn.gelu(pre).astype(BF16), w2_ref[...]))
        return jnp.concatenate(outs, axis=-1).astype(BF16)

    ko_ref[0] = compress(ck_ref, wk1_ref, wk2_ref)
    vo_ref[0] = compress(cv_ref, wv1_ref, wv2_ref)


def _compress(ck, cv, pos2, wk1, wk2, wv1, wv2):
    batch, _, n_chunk, width = ck.shape
    blk = pl.BlockSpec((1, N_GROUPS, n_chunk, width), lambda b: (b, 0, 0, 0))
    full = lambda a: pl.BlockSpec(a.shape, lambda b: (0,) * a.ndim)
    out = jax.ShapeDtypeStruct((batch, n_chunk, LANES), BF16)
    return pl.pallas_call(
        _compress_kernel,
        out_shape=[out, out],
        grid=(batch,),
        in_specs=[blk, blk, full(pos2), full(wk1), full(wk2), full(wv1), full(wv2)],
        out_specs=[pl.BlockSpec((1, n_chunk, LANES), lambda b: (b, 0, 0))] * 2,
        compiler_params=_params("parallel"),
        name="compress",
    )(ck, cv, pos2, wk1, wk2, wv1, wv2)


SLC_CHUNK = 256
BLOCKS_PER_CHUNK = SLC_CHUNK // SEL_LEN
V_ROWS = HEAD_DIM + 16


def _slope(head):
    return float(2.0 ** (-ALIBI_MAX_BIAS * (head + 1) / N_HEADS))


def _nsa_kernel(q_ref, kc_ref, vct_ref, ks_ref, vst_ref, kw_ref, vwt_ref, gate_ref, ovt_ref, o_ref,
                p_ref, gt_ref, selt_ref, flag_ref, m_ref, acc_ref, out_ref, *, tq, seq):
    n_sel = seq // SEL_LEN
    n_chunks = seq // SLC_CHUNK
    n_cmp = kc_ref.shape[1]
    n_win = WINDOW + tq
    q0 = pl.program_id(1) * tq
    gt_ref[...] = gate_ref[...].T
    lanes = lambda head: slice(head * tq, (head + 1) * tq)
    pairs = [(g, g * HEADS_PER_GROUP + 2 * j) for g in range(N_GROUPS) for j in range(HEADS_PER_GROUP // 2)]

    def gate_row(head, branch):
        c = 3 * head + branch
        return gt_ref[c:c + 1, :]

    def q_pair(head):
        return q_ref[0, head:head + 2].reshape(2 * tq, LANES)

    def out_rows(head):
        g, hh = divmod(head, HEADS_PER_GROUP)
        return hh, slice(g * HEAD_DIM, (g + 1) * HEAD_DIM)

    t_c = q0 + lax.broadcasted_iota(jnp.int32, (n_cmp, tq), 1)
    end_c = lax.broadcasted_iota(jnp.int32, (n_cmp, tq), 0) * CMP_STRIDE + (CMP_LEN - 1)
    dist_c = (t_c - end_c).astype(F32)
    mask_c = dist_c >= 0
    p_sum = [None] * N_GROUPS

    def pipelined(scores, consume):
        nxt = scores(pairs[0][1])
        for idx, (g, head0) in enumerate(pairs):
            st = nxt
            if idx + 1 < len(pairs):
                nxt = scores(pairs[idx + 1][1])
            consume(g, head0, st)

    def cmp_pair(g, head0, st):
        scales = []
        for i in range(2):
            head = head0 + i
            x = jnp.where(mask_c, st[:, i * tq:(i + 1) * tq] - _slope(head) * dist_c, NEG_BIG)
            p = jnp.where(mask_c, jnp.exp(x - jnp.max(x, axis=0, keepdims=True)), 0.0)
            inv = 1.0 / jnp.maximum(jnp.sum(p, axis=0, keepdims=True), 1e-30)
            p_sum[g] = p * inv if p_sum[g] is None else p_sum[g] + p * inv
            p_ref[0:n_cmp, lanes(head)] = p.astype(BF16)
            scales.append(inv * gate_row(head, 0))
        pv = _dot(vct_ref[0, g, 0:HEAD_DIM, :], p_ref[0:n_cmp, head0 * tq:(head0 + 2) * tq])
        for i in range(2):
            hh, rows = out_rows(head0 + i)
            out_ref[hh, rows, :] = pv[:, i * tq:(i + 1) * tq] * scales[i]

    pipelined(lambda head0: _dot_nt(kc_ref[0], q_pair(head0)), cmp_pair)

    blk = lax.broadcasted_iota(jnp.int32, (n_sel, tq), 0)
    cur = (q0 + lax.broadcasted_iota(jnp.int32, (n_sel, tq), 1)) // SEL_LEN
    valid = blk <= cur
    forced = valid & ((blk == 0) | (blk == cur) | (blk == cur - 1))
    blk_f = blk.astype(F32)
    ovt = ovt_ref[...]
    for g in range(N_GROUPS):
        p1 = p_sum[g].astype(BF16)
        r1 = p_sum[g] - p1.astype(F32)
        p2 = r1.astype(BF16)
        p3 = (r1 - p2.astype(F32)).astype(BF16)
        score = _dot(ovt, p1) + _dot(ovt, p2) + _dot(ovt, p3)
        score = jnp.where(forced, TOPK_BIG, jnp.where(valid, score, -TOPK_BIG))
        sel = jnp.zeros((n_sel, tq), F32)
        for _ in range(min(N_SELECT, n_sel)):
            best = jnp.max(score, axis=0, keepdims=True)
            pick = jnp.min(jnp.where(score == best, blk_f, float(n_sel)), axis=0, keepdims=True)
            hit = blk_f == pick
            sel = jnp.where(hit, 1.0, sel)
            score = jnp.where(hit, -TOPK_BIG, score)
        selt_ref[g] = sel
        sel_any = sel if g == 0 else sel_any + sel
    for c in range(n_chunks):
        any_sel = jnp.max(sel_any[c * BLOCKS_PER_CHUNK:(c + 1) * BLOCKS_PER_CHUNK, :])
        flag_ref[c] = (any_sel > 0.5).astype(jnp.int32)

    w0 = pl.multiple_of(jnp.maximum(q0 - WINDOW, 0), tq)
    dist_w_i = (q0 + lax.broadcasted_iota(jnp.int32, (n_win, tq), 1)) - (w0 + lax.broadcasted_iota(jnp.int32, (n_win, tq), 0))
    bias_w = jnp.where((dist_w_i >= 0) & (dist_w_i < WINDOW), 0.0, NEG_BIG)
    dist_w = dist_w_i.astype(F32)
    k_win = kw_ref[0, pl.ds(w0, n_win), :]

    def win_pair(g, head0, st):
        for i in range(2):
            head = head0 + i
            x = st[:, i * tq:(i + 1) * tq] - _slope(head) * dist_w + bias_w
            p_ref[0:n_win, lanes(head)] = jnp.exp(x - jnp.max(x, axis=0, keepdims=True)).astype(BF16)
        pv = _dot(vwt_ref[0, g, :, pl.ds(w0, n_win)], p_ref[0:n_win, head0 * tq:(head0 + 2) * tq])
        for i in range(2):
            hh, rows = out_rows(head0 + i)
            cols = slice(i * tq, (i + 1) * tq)
            scale = gate_row(head0 + i, 2) / pv[HEAD_DIM:HEAD_DIM + 1, cols]
            out_ref[hh, rows, :] += pv[0:HEAD_DIM, cols] * scale

    pipelined(lambda head0: _dot_nt(k_win, q_pair(head0)), win_pair)

    m_ref[...] = jnp.full_like(m_ref, NEG_BIG)
    acc_ref[...] = jnp.zeros_like(acc_ref)
    base_i = (lax.broadcasted_iota(jnp.int32, (SLC_CHUNK, tq), 1)
              - lax.broadcasted_iota(jnp.int32, (SLC_CHUNK, tq), 0))

    def chunk_body(c, carry):
        @pl.when(flag_ref[c] != 0)
        def _():
            k0 = pl.multiple_of(c * SLC_CHUNK, SLC_CHUNK)
            dist_i = base_i + (q0 - k0)
            dist = dist_i.astype(F32)
            bias = []
            for g in range(N_GROUPS):
                sel_keys = jnp.concatenate(
                    [jnp.broadcast_to(selt_ref[g, pl.ds(c * BLOCKS_PER_CHUNK + i, 1), :], (SEL_LEN, tq))
                     for i in range(BLOCKS_PER_CHUNK)], axis=0)
                bias.append(jnp.where((dist_i >= 0) & (sel_keys > 0.5), 0.0, NEG_BIG))
            k_chunk = ks_ref[0, pl.ds(k0, SLC_CHUNK), :]

            def slc_pair(g, head0, st):
                alphas = []
                for i in range(2):
                    head = head0 + i
                    x = st[:, i * tq:(i + 1) * tq] - _slope(head) * dist + bias[g]
                    m_old = m_ref[head:head + 1, :]
                    m_new = jnp.maximum(m_old, jnp.max(x, axis=0, keepdims=True))
                    alphas.append(jnp.exp(m_old - m_new))
                    m_ref[head:head + 1, :] = m_new
                    p_ref[0:SLC_CHUNK, lanes(head)] = jnp.exp(x - m_new).astype(BF16)
                both = slice(head0 * tq, (head0 + 2) * tq)
                pv = _dot(vst_ref[0, g, :, pl.ds(k0, SLC_CHUNK)], p_ref[0:SLC_CHUNK, both])
                acc_ref[:, both] = acc_ref[:, both] * jnp.concatenate(alphas, axis=1) + pv

            pipelined(lambda head0: _dot_nt(k_chunk, q_pair(head0)), slc_pair)
        return carry

    lax.fori_loop(0, (q0 + tq + SLC_CHUNK - 1) // SLC_CHUNK, chunk_body, 0)
    for head in range(N_HEADS):
        hh, rows = out_rows(head)
        scale = gate_row(head, 1) / acc_ref[HEAD_DIM:HEAD_DIM + 1, lanes(head)]
        out_ref[hh, rows, :] += acc_ref[0:HEAD_DIM, lanes(head)] * scale

    for hh in range(HEADS_PER_GROUP):
        o_ref[:, hh * LANES:(hh + 1) * LANES] = out_ref[hh].T.astype(BF16)


def _overlap_t(seq):
    n_cmp = (seq - CMP_LEN) // CMP_STRIDE + 1
    n_sel = seq // SEL_LEN
    cs = np.arange(n_cmp)[:, None] * CMP_STRIDE
    ss = np.arange(n_sel)[None, :] * SEL_LEN
    ov = np.clip(np.minimum(cs + CMP_LEN, ss + SEL_LEN) - np.maximum(cs, ss), 0, None) / CMP_LEN
    out = np.zeros((n_sel, seq // CMP_STRIDE), np.float32)
    out[:, :n_cmp] = ov.T
    return jnp.asarray(out, dtype=BF16)


def _values_t(v, batch, n):
    vt = v.reshape(batch, n, N_GROUPS, HEAD_DIM).transpose(0, 2, 3, 1)
    ones = jnp.ones((batch, N_GROUPS, V_ROWS - HEAD_DIM, n), BF16)
    return jnp.concatenate([vt, ones], axis=2)


def _nsa(q, kc, vc, ks, vs, kw, vw, gates, *, tq=128):
    batch, _, seq, _ = q.shape
    n_cmp = kc.shape[1]
    n_sel = seq // SEL_LEN
    assert seq % SLC_CHUNK == 0 and seq >= WINDOW + tq and SLC_CHUNK % tq == 0 and tq == LANES
    ovt = _overlap_t(seq)
    vct, vst, vwt = _values_t(vc, batch, n_cmp), _values_t(vs, batch, seq), _values_t(vw, batch, seq)
    keys = lambda n: pl.BlockSpec((1, n, LANES), lambda b, i: (b, 0, 0))
    vals = lambda n: pl.BlockSpec((1, N_GROUPS, V_ROWS, n), lambda b, i: (b, 0, 0, 0))
    tiles = seq // tq
    return pl.pallas_call(
        functools.partial(_nsa_kernel, tq=tq, seq=seq),
        out_shape=jax.ShapeDtypeStruct((batch * seq, HEADS_PER_GROUP * LANES), BF16),
        grid=(batch, tiles),
        in_specs=[
            pl.BlockSpec((1, N_HEADS, tq, LANES), lambda b, i: (b, 0, i, 0)),
            keys(n_cmp), vals(n_cmp), keys(seq), vals(seq), keys(seq), vals(seq),
            pl.BlockSpec((tq, LANES), lambda b, i: (b * tiles + i, 0)),
            pl.BlockSpec(ovt.shape, lambda b, i: (0, 0)),
        ],
        out_specs=pl.BlockSpec((tq, HEADS_PER_GROUP * LANES), lambda b, i: (b * tiles + i, 0)),
        scratch_shapes=[
            pltpu.VMEM((max(n_cmp, WINDOW + tq, SLC_CHUNK), N_HEADS * tq), BF16),
            pltpu.VMEM((LANES, tq), F32),
            pltpu.VMEM((N_GROUPS, n_sel, tq), F32),
            pltpu.SMEM((seq // SLC_CHUNK,), jnp.int32),
            pltpu.VMEM((N_HEADS, tq), F32),
            pltpu.VMEM((V_ROWS, N_HEADS * tq), F32),
            pltpu.VMEM((HEADS_PER_GROUP, N_GROUPS * HEAD_DIM, tq), F32),
        ],
        compiler_params=_params("parallel", "arbitrary"),
        name="nsa",
    )(q, kc, vct, ks, vst, kw, vwt, gates, ovt)


HALO = max(POOL_WINDOWS)


def _merge_kernel(x_ref, u_ref, halo_ref, gm_ref, on_ref, pw_ref, ps_ref, wbp_ref, wbn_ref, wo_ref,
                  o_ref, ext_ref, *, tm, seq):
    d = x_ref.shape[1]
    pos0 = (pl.program_id(0) * tm) % seq
    ext_ref[0:HALO, :] = jnp.where(pos0 == 0, 0.0, halo_ref[...])
    ext_ref[HALO:HALO + tm, :] = u_ref[...]
    pos = (pos0 + lax.broadcasted_iota(jnp.int32, (tm, POOL_GROUP_DIM), 0)).astype(F32)
    mixed = []
    for gi, w in enumerate(POOL_WINDOWS):
        cols = slice(gi * POOL_GROUP_DIM, (gi + 1) * POOL_GROUP_DIM)
        u = ext_ref[HALO:HALO + tm, cols]
        total = u
        for lag in range(1, w):
            total = total + ext_ref[HALO - lag:HALO - lag + tm, cols]
        delta = total / jnp.minimum(pos + 1.0, float(w)) - u
        mixed.append(_dot(delta.astype(BF16), pw_ref[gi]) * ps_ref[:, cols])
    mixed = jnp.concatenate(mixed, axis=-1).astype(BF16)
    a = _dot(mixed, wbp_ref[...])
    b = _dot(on_ref[...], wbn_ref[...])
    merged = gm_ref[:, 0:d] * a + gm_ref[:, d:2 * d] * b
    o_ref[...] = x_ref[...] + _dot(merged.astype(BF16), wo_ref[...])


def _merge(x, u, gm, o_nsa, pool_w, pool_scale, w_bp, w_bn, w_out, seq, *, tm=256):
    n, d = x.shape
    row = lambda i: (i, 0)
    full = lambda a: pl.BlockSpec(a.shape, lambda i: (0,) * a.ndim)
    halo_blocks = tm // HALO
    return pl.pallas_call(
        functools.partial(_merge_kernel, tm=tm, seq=seq),
        out_shape=jax.ShapeDtypeStruct((n, d), F32),
        grid=(n // tm,),
        in_specs=[
            pl.BlockSpec((tm, d), row),
            pl.BlockSpec((tm, POOL_COLS), row),
            pl.BlockSpec((HALO, POOL_COLS), lambda i: (jnp.maximum(i * halo_blocks - 1, 0), 0)),
            pl.BlockSpec((tm, 2 * d), row),
            pl.BlockSpec((tm, d), row),
            full(pool_w), full(pool_scale), full(w_bp), full(w_bn), full(w_out),
        ],
        out_specs=pl.BlockSpec((tm, d), row),
        scratch_shapes=[pltpu.VMEM((HALO + tm, POOL_COLS), F32)],
        compiler_params=_params("parallel"),
        name="merge",
    )(x, u, u, gm, o_nsa, pool_w, pool_scale, w_bp, w_bn, w_out)


def _chunked(kv, batch, seq):
    x = kv.reshape(batch, seq // CMP_STRIDE, CMP_STRIDE, N_GROUPS, HEAD_DIM)
    return x.transpose(0, 3, 1, 2, 4).reshape(batch, N_GROUPS, seq // CMP_STRIDE, CMP_STRIDE * HEAD_DIM)


def kernel(x, ffn1_norm, ffn1_w_gate, ffn1_w_up, ffn1_w_down, mix_norm, w_in, cmp_pos, cmp_k_w1, cmp_k_w2, cmp_v_w1, cmp_v_w2, pool_w, pool_scale, w_branch_pool, w_branch_nsa, w_out, ffn2_norm, ffn2_w_gate, ffn2_w_up, ffn2_w_down, final_norm):
    batch, seq, d = x.shape
    depth = w_in.shape[0]
    xf = x.reshape(batch * seq, d)
    bf = lambda a: a.astype(BF16)
    row = lambda a: a.reshape(1, -1)
    for l in range(depth):
        xf = _ffn(xf, row(ffn1_norm[l]), bf(ffn1_w_gate[l]), bf(ffn1_w_up[l]), bf(ffn1_w_down[l]),
                  row(final_norm), final_norm=False)

        q, kc, vc, ks, vs, kw, vw, gates, u, gm = _proj(xf, row(mix_norm[l]), _pack_w_in(w_in[l], d), batch, seq)
        pos2 = cmp_pos[l].reshape(2, (CMP_LEN // 2) * HEAD_DIM)
        k_cmp, v_cmp = _compress(_chunked(kc, batch, seq), _chunked(vc, batch, seq), pos2,
                                 bf(cmp_k_w1[l]), bf(cmp_k_w2[l]), bf(cmp_v_w1[l]), bf(cmp_v_w2[l]))
        per_b = lambda a: a.reshape(batch, seq, LANES)
        o_nsa = _nsa(q, k_cmp, v_cmp, per_b(ks), per_b(vs), per_b(kw), per_b(vw), gates)

        w_bn = w_branch_nsa[l].reshape(N_GROUPS, HEADS_PER_GROUP, HEAD_DIM, d).transpose(1, 0, 2, 3).reshape(-1, d)
        xf = _merge(xf, u, gm, o_nsa, bf(pool_w[l]), row(pool_scale[l]), bf(w_branch_pool[l]), bf(w_bn),
                    bf(w_out[l]), seq)

        xf = _ffn(xf, row(ffn2_norm[l]), bf(ffn2_w_gate[l]), bf(ffn2_w_up[l]), bf(ffn2_w_down[l]),
                  row(final_norm), final_norm=(l == depth - 1))
    return xf.reshape(batch, seq, d)
```

```python
import functools

import jax
import jax.numpy as jnp
import numpy as np
from jax import lax
from jax.experimental import pallas as pl
from jax.experimental.pallas import tpu as pltpu

D_FF = 2816
N_HEADS = 16
N_GROUPS = 2
HEADS_PER_GROUP = N_HEADS // N_GROUPS
HEAD_DIM = 64
CMP_LEN = 32
CMP_STRIDE = 16
SEL_LEN = 64
N_SELECT = 16
WINDOW = 512
POOL_WINDOWS = (2, 4, 8, 16)
POOL_GROUP_DIM = 128
RMS_EPS = 1e-6
ALIBI_MAX_BIAS = 8.0

LANES = 128
NEG_BIG = -1e30
TOPK_BIG = 1e30
VMEM_LIMIT = 56 * 1024 * 1024

BF16 = jnp.bfloat16
F32 = jnp.float32


def _dot(a, b):
    return jnp.dot(a, b, preferred_element_type=F32)


def _dot_nt(a, b):
    return lax.dot_general(a, b, (((1,), (1,)), ((), ())), preferred_element_type=F32)


def _rms(x, g):
    return x * lax.rsqrt(jnp.mean(x * x, axis=-1, keepdims=True) + RMS_EPS) * g


def _params(*sem):
    return pltpu.CompilerParams(dimension_semantics=sem, vmem_limit_bytes=VMEM_LIMIT)


def _ffn_kernel(x_ref, g_ref, wg_ref, wu_ref, wd_ref, fin_ref, o_ref, xn_ref, acc_ref, *, final_norm):
    j = pl.program_id(1)

    @pl.when(j == 0)
    def _():
        xn_ref[...] = _rms(x_ref[...], g_ref[...]).astype(BF16)
        acc_ref[...] = jnp.zeros_like(acc_ref)

    xn = xn_ref[...]
    gate = _dot(xn, wg_ref[...])
    up = _dot(xn, wu_ref[...])
    act = (gate * jax.nn.sigmoid(gate)) * up
    acc_ref[...] += _dot(act.astype(BF16), wd_ref[...])

    @pl.when(j == pl.num_programs(1) - 1)
    def _():
        y = x_ref[...] + 0.5 * acc_ref[...]
        if final_norm:
            y = _rms(y, fin_ref[...])
        o_ref[...] = y


def _ffn(x, norm_g, wg, wu, wd, fin_g, *, final_norm, tm=512, tf=256):
    n, d = x.shape
    grid = (n // tm, D_FF // tf)
    return pl.pallas_call(
        functools.partial(_ffn_kernel, final_norm=final_norm),
        out_shape=jax.ShapeDtypeStruct((n, d), F32),
        grid=grid,
        in_specs=[
            pl.BlockSpec((tm, d), lambda i, j: (i, 0)),
            pl.BlockSpec((1, d), lambda i, j: (0, 0)),
            pl.BlockSpec((d, tf), lambda i, j: (0, j)),
            pl.BlockSpec((d, tf), lambda i, j: (0, j)),
            pl.BlockSpec((tf, d), lambda i, j: (j, 0)),
            pl.BlockSpec((1, d), lambda i, j: (0, 0)),
        ],
        out_specs=pl.BlockSpec((tm, d), lambda i, j: (i, 0)),
        scratch_shapes=[pltpu.VMEM((tm, d), BF16), pltpu.VMEM((tm, d), F32)],
        compiler_params=_params("parallel", "arbitrary"),
        name="ffn",
    )(x, norm_g, wg, wu, wd, fin_g)


LOG2E = 1.4426950408889634
N_FEAT = 6


def _bf16_pieces(x):
    x = np.asarray(x, np.float32)
    s1 = x.astype(BF16).astype(np.float32)
    s2 = (x - s1).astype(BF16).astype(np.float32)
    s3 = (x - s1 - s2).astype(BF16).astype(np.float32)
    return s1, s2, s3


def _query_features():
    slopes = np.float32(2.0) ** (-ALIBI_MAX_BIAS * np.arange(1, N_HEADS + 1, dtype=np.float32) / N_HEADS)
    s1, s2, s3 = _bf16_pieces(slopes * np.float32(LOG2E))
    feat = np.zeros((N_HEADS, LANES), np.float32)
    feat[:, HEAD_DIM:HEAD_DIM + N_FEAT] = np.stack([s1, s2, s3, SEL_LEN * s1, SEL_LEN * s2, SEL_LEN * s3], axis=1)
    return jnp.asarray(feat)


def _key_features(pos, width, offset):
    pos = np.asarray(pos)
    a, b = (pos // SEL_LEN).astype(np.float32), (pos % SEL_LEN).astype(np.float32)
    feat = np.zeros((len(pos), width), np.float32)
    feat[:, offset:offset + N_FEAT] = np.stack([b, b, b, a, a, a], axis=1)
    return jnp.asarray(feat)


Q_COLS = N_HEADS * LANES
CMP_OFF = Q_COLS
KEY_OFF = CMP_OFF + 2 * LANES
VAL_OFF = KEY_OFF + 2 * N_GROUPS * LANES
GATE_OFF = VAL_OFF + 2 * LANES
POOL_OFF = GATE_OFF + LANES
POOL_COLS = len(POOL_WINDOWS) * POOL_GROUP_DIM
MERGE_OFF = POOL_OFF + POOL_COLS


def _proj_kernel(x_ref, g_ref, w_ref, qf_ref, kf_ref, q_ref, kc_ref, vc_ref, ks_ref, kw_ref, vs_ref, vw_ref,
                 gate_ref, u_ref, gm_ref):
    d = x_ref.shape[1]
    hn = _rms(x_ref[...], g_ref[...]).astype(BF16)
    q_scale = HEAD_DIM ** -0.5 * LOG2E
    for h in range(N_HEADS):
        q = _dot(hn, w_ref[:, h * LANES:(h + 1) * LANES]) * q_scale + qf_ref[h:h + 1, :]
        q_ref[0, h] = q.astype(BF16)
    cmp_in = _dot(hn, w_ref[:, CMP_OFF:CMP_OFF + 2 * LANES])
    kc_ref[...] = cmp_in[:, 0:LANES]
    vc_ref[...] = cmp_in[:, LANES:2 * LANES]
    keys = _dot(hn, w_ref[:, KEY_OFF:KEY_OFF + 2 * N_GROUPS * LANES])
    kf = kf_ref[...]
    for g in range(N_GROUPS):
        ks_ref[0, g] = (keys[:, g * LANES:(g + 1) * LANES] + kf).astype(BF16)
        kw_ref[0, g] = (keys[:, (N_GROUPS + g) * LANES:(N_GROUPS + g + 1) * LANES] + kf).astype(BF16)
    vals = _dot(hn, w_ref[:, VAL_OFF:VAL_OFF + 2 * LANES])
    vs_ref[...] = vals[:, 0:LANES].astype(BF16)
    vw_ref[...] = vals[:, LANES:2 * LANES].astype(BF16)
    gate_ref[...] = jax.nn.sigmoid(_dot(hn, w_ref[:, GATE_OFF:GATE_OFF + LANES]))
    u_ref[...] = _dot(hn, w_ref[:, POOL_OFF:POOL_OFF + POOL_COLS])
    gm_ref[...] = jax.nn.sigmoid(_dot(hn, w_ref[:, MERGE_OFF:MERGE_OFF + 2 * d]))


def _pack_w_in(w_in, d):
    qw = N_HEADS * HEAD_DIM
    kvw = N_GROUPS * HEAD_DIM

    def padded(w, n):
        w = w.reshape(d, n, HEAD_DIM)
        return jnp.concatenate([w, jnp.zeros_like(w)], axis=-1).reshape(d, n * LANES)

    kc, vc, ks, vs, kw, vw = (w_in[:, qw + i * kvw:qw + (i + 1) * kvw] for i in range(6))
    off = qw + 6 * kvw
    n_g = 3 * N_HEADS
    w_g = jnp.pad(w_in[:, off:off + n_g], ((0, 0), (0, LANES - n_g)))
    w_rest = w_in[:, off + n_g:]
    return jnp.concatenate([padded(w_in[:, :qw], N_HEADS), kc, vc, padded(ks, N_GROUPS), padded(kw, N_GROUPS),
                            vs, vw, w_g, w_rest], axis=1).astype(BF16)


def _proj(x, norm_g, w_packed, batch, seq, *, tm=256):
    n, d = x.shape
    tiles_per_seq = seq // tm
    row = lambda i: (i, 0)
    const = lambda i: (0, 0)
    per_group = lambda i: (i // tiles_per_seq, 0, i % tiles_per_seq, 0)
    qf = _query_features()
    kf = _key_features(np.arange(seq), LANES, HEAD_DIM)
    flat = lambda width, dtype: (jax.ShapeDtypeStruct((n, width), dtype), pl.BlockSpec((tm, width), row))
    grouped = lambda count: (jax.ShapeDtypeStruct((batch, count, seq, LANES), BF16),
                             pl.BlockSpec((1, count, tm, LANES), per_group))
    outs = [grouped(N_HEADS), flat(LANES, F32), flat(LANES, F32), grouped(N_GROUPS), grouped(N_GROUPS),
            flat(LANES, BF16), flat(LANES, BF16), flat(LANES, F32), flat(POOL_COLS, F32), flat(2 * d, F32)]
    return pl.pallas_call(
        _proj_kernel,
        out_shape=[o[0] for o in outs],
        grid=(n // tm,),
        in_specs=[pl.BlockSpec((tm, d), row), pl.BlockSpec((1, d), const), pl.BlockSpec(w_packed.shape, const),
                  pl.BlockSpec(qf.shape, const), pl.BlockSpec((tm, LANES), lambda i: (i % tiles_per_seq, 0))],
        out_specs=[o[1] for o in outs],
        compiler_params=_params("parallel"),
        name="proj",
    )(x, norm_g, w_packed, qf, kf)


def _compress_kernel(ck_ref, cv_ref, pos_ref, wk1_ref, wk2_ref, wv1_ref, wv2_ref, feat_ref, ko_ref, vo_ref):
    half = pos_ref.shape[1]
    n_chunk = ck_ref.shape[2]
    pos_a = pos_ref[0:1, :]
    pos_b = pos_ref[1:2, :]

    def compress(c_ref, w1_ref, w2_ref, g):
        c = c_ref[0, g]
        first = _dot((c + pos_a).astype(BF16), w1_ref[0:half, :])
        second = _dot((c + pos_b).astype(BF16), w1_ref[half:2 * half, :])
        pre = first + pltpu.roll(second, n_chunk - 1, 0)
        return _dot(jax.nn.gelu(pre).astype(BF16), w2_ref[...])

    for g in range(N_GROUPS):
        ko_ref[0, g] = jnp.concatenate([compress(ck_ref, wk1_ref, wk2_ref, g), feat_ref[...]], axis=-1).astype(BF16)
    vo_ref[0] = jnp.concatenate([compress(cv_ref, wv1_ref, wv2_ref, g) for g in range(N_GROUPS)],
                                axis=-1).astype(BF16)


def _compress(ck, cv, pos2, wk1, wk2, wv1, wv2):
    batch, _, n_chunk, width = ck.shape
    blk = pl.BlockSpec((1, N_GROUPS, n_chunk, width), lambda b: (b, 0, 0, 0))
    full = lambda a: pl.BlockSpec(a.shape, lambda b: (0,) * a.ndim)
    feat = _key_features(np.arange(n_chunk) * CMP_STRIDE + CMP_LEN - 1, HEAD_DIM, 0)
    return pl.pallas_call(
        _compress_kernel,
        out_shape=[jax.ShapeDtypeStruct((batch, N_GROUPS, n_chunk, LANES), BF16),
                   jax.ShapeDtypeStruct((batch, n_chunk, LANES), BF16)],
        grid=(batch,),
        in_specs=[blk, blk, full(pos2), full(wk1), full(wk2), full(wv1), full(wv2), full(feat)],
        out_specs=[pl.BlockSpec((1, N_GROUPS, n_chunk, LANES), lambda b: (b, 0, 0, 0)),
                   pl.BlockSpec((1, n_chunk, LANES), lambda b: (b, 0, 0))],
        compiler_params=_params("parallel"),
        name="compress",
    )(ck, cv, pos2, wk1, wk2, wv1, wv2, feat)


SLC_CHUNK = 256
BLOCKS_PER_CHUNK = SLC_CHUNK // SEL_LEN
V_ROWS = HEAD_DIM + 16


def _nsa_kernel(q_ref, kc_ref, vct_ref, ks_ref, vst_ref, kw_ref, vwt_ref, gate_ref, ovt_ref, o_ref,
                st_ref, p_ref, gt_ref, selt_ref, flag_ref, m_ref, acc_ref, out_ref, *, tq, seq):
    n_sel = seq // SEL_LEN
    n_chunks = seq // SLC_CHUNK
    n_cmp = kc_ref.shape[2]
    n_win = WINDOW + tq
    q0 = pl.program_id(1) * tq
    gt_ref[...] = gate_ref[...].T
    lanes = lambda head: slice(head * tq, (head + 1) * tq)
    pairs = [(g, g * HEADS_PER_GROUP + 2 * j) for g in range(N_GROUPS) for j in range(HEADS_PER_GROUP // 2)]

    def gate_row(head, branch):
        c = 3 * head + branch
        return gt_ref[c:c + 1, :]

    def out_rows(head):
        g, hh = divmod(head, HEADS_PER_GROUP)
        return hh, slice(g * HEAD_DIM, (g + 1) * HEAD_DIM)

    def staged(keys, n_keys, consume):
        for g, head0 in pairs:
            q_pair = q_ref[0, head0:head0 + 2].reshape(2 * tq, LANES)
            st_ref[0:n_keys, head0 * tq:(head0 + 2) * tq] = _dot_nt(keys(g), q_pair)
        for g, head0 in pairs:
            consume(g, head0)

    def finish_pair(head0, pv, branch, first, ok=None):
        invs = []
        for i in range(2):
            cols = slice(i * tq, (i + 1) * tq)
            inv = 1.0 / pv[HEAD_DIM:HEAD_DIM + 1, cols]
            if ok is not None:
                inv = jnp.where(ok, inv, 0.0)
            hh, rows = out_rows(head0 + i)
            contrib = pv[0:HEAD_DIM, cols] * (inv * gate_row(head0 + i, branch))
            out_ref[hh, rows, :] = contrib if first else out_ref[hh, rows, :] + contrib
            invs.append(inv)
        return invs

    t_c = q0 + lax.broadcasted_iota(jnp.int32, (n_cmp, tq), 1)
    end_c = lax.broadcasted_iota(jnp.int32, (n_cmp, tq), 0) * CMP_STRIDE + (CMP_LEN - 1)
    bias_c = jnp.where(t_c >= end_c, 0.0, NEG_BIG)
    has_cmp = (q0 + lax.broadcasted_iota(jnp.int32, (1, tq), 1)) >= CMP_LEN - 1
    inv_c = [None] * N_HEADS

    def cmp_pair(g, head0):
        for head in (head0, head0 + 1):
            x = st_ref[0:n_cmp, lanes(head)] + bias_c
            p_ref[0:n_cmp, lanes(head)] = jnp.exp2(x - jnp.max(x, axis=0, keepdims=True)).astype(BF16)
        pv = _dot(vct_ref[0, g], p_ref[0:n_cmp, head0 * tq:(head0 + 2) * tq])
        inv_c[head0], inv_c[head0 + 1] = finish_pair(head0, pv, 0, True, has_cmp)

    staged(lambda g: kc_ref[0, g], n_cmp, cmp_pair)

    raw = _dot(ovt_ref[...], p_ref[0:n_cmp, :])
    blk = lax.broadcasted_iota(jnp.int32, (n_sel, tq), 0)
    cur = (q0 + lax.broadcasted_iota(jnp.int32, (n_sel, tq), 1)) // SEL_LEN
    valid = blk <= cur
    forced = valid & ((blk == 0) | (blk == cur) | (blk == cur - 1))
    blk_f = blk.astype(F32)
    score, sel = [], []
    for g in range(N_GROUPS):
        total = None
        for head in range(g * HEADS_PER_GROUP, (g + 1) * HEADS_PER_GROUP):
            part = raw[:, lanes(head)] * inv_c[head]
            total = part if total is None else total + part
        score.append(jnp.where(forced, TOPK_BIG, jnp.where(valid, total, -TOPK_BIG)))
        sel.append(jnp.zeros((n_sel, tq), F32))

    for _ in range(min(N_SELECT, n_sel)):
        for g in range(N_GROUPS):
            best = jnp.max(score[g], axis=0, keepdims=True)
            pick = jnp.min(jnp.where(score[g] == best, blk_f, float(n_sel)), axis=0, keepdims=True)
            hit = blk_f == pick
            sel[g] = jnp.where(hit, 1.0, sel[g])
            score[g] = jnp.where(hit, -TOPK_BIG, score[g])
    for g in range(N_GROUPS):
        selt_ref[g] = sel[g]
    sel_any = sel[0] + sel[1]
    for c in range(n_chunks):
        any_sel = jnp.max(sel_any[c * BLOCKS_PER_CHUNK:(c + 1) * BLOCKS_PER_CHUNK, :])
        flag_ref[c] = (any_sel > 0.5).astype(jnp.int32)

    w0 = pl.multiple_of(jnp.maximum(q0 - WINDOW, 0), tq)
    dist_w = (q0 + lax.broadcasted_iota(jnp.int32, (n_win, tq), 1)) - (w0 + lax.broadcasted_iota(jnp.int32, (n_win, tq), 0))
    bias_w = jnp.where((dist_w >= 0) & (dist_w < WINDOW), 0.0, NEG_BIG)

    def win_pair(g, head0):
        for head in (head0, head0 + 1):
            x = st_ref[0:n_win, lanes(head)] + bias_w
            p_ref[0:n_win, lanes(head)] = jnp.exp2(x - jnp.max(x, axis=0, keepdims=True)).astype(BF16)
        pv = _dot(vwt_ref[0, g, :, pl.ds(w0, n_win)], p_ref[0:n_win, head0 * tq:(head0 + 2) * tq])
        finish_pair(head0, pv, 2, False)

    staged(lambda g: kw_ref[0, g, pl.ds(w0, n_win), :], n_win, win_pair)

    m_ref[...] = jnp.full_like(m_ref, NEG_BIG)
    acc_ref[...] = jnp.zeros_like(acc_ref)
    base = (lax.broadcasted_iota(jnp.int32, (SLC_CHUNK, tq), 1)
            - lax.broadcasted_iota(jnp.int32, (SLC_CHUNK, tq), 0))

    def chunk_body(c, carry):
        @pl.when(flag_ref[c] != 0)
        def _():
            k0 = pl.multiple_of(c * SLC_CHUNK, SLC_CHUNK)
            causal = base >= k0 - q0
            bias = []
            for g in range(N_GROUPS):
                sel_keys = jnp.concatenate(
                    [jnp.broadcast_to(selt_ref[g, pl.ds(c * BLOCKS_PER_CHUNK + i, 1), :], (SEL_LEN, tq))
                     for i in range(BLOCKS_PER_CHUNK)], axis=0)
                bias.append(jnp.where(causal & (sel_keys > 0.5), 0.0, NEG_BIG))

            def slc_pair(g, head0):
                alphas = []
                for head in (head0, head0 + 1):
                    x = st_ref[0:SLC_CHUNK, lanes(head)] + bias[g]
                    m_old = m_ref[head:head + 1, :]
                    m_new = jnp.maximum(m_old, jnp.max(x, axis=0, keepdims=True))
                    alphas.append(jnp.exp2(m_old - m_new))
                    m_ref[head:head + 1, :] = m_new
                    p_ref[0:SLC_CHUNK, lanes(head)] = jnp.exp2(x - m_new).astype(BF16)
                both = slice(head0 * tq, (head0 + 2) * tq)
                pv = _dot(vst_ref[0, g, :, pl.ds(k0, SLC_CHUNK)], p_ref[0:SLC_CHUNK, both])
                acc_ref[:, both] = acc_ref[:, both] * jnp.concatenate(alphas, axis=1) + pv

            staged(lambda g: ks_ref[0, g, pl.ds(k0, SLC_CHUNK), :], SLC_CHUNK, slc_pair)
        return carry

    lax.fori_loop(0, (q0 + tq + SLC_CHUNK - 1) // SLC_CHUNK, chunk_body, 0)
    for g, head0 in pairs:
        finish_pair(head0, acc_ref[:, head0 * tq:(head0 + 2) * tq], 1, False)

    for hh in range(HEADS_PER_GROUP):
        o_ref[:, hh * LANES:(hh + 1) * LANES] = out_ref[hh].T.astype(BF16)


def _overlap_t(seq):
    n_cmp = (seq - CMP_LEN) // CMP_STRIDE + 1
    n_sel = seq // SEL_LEN
    cs = np.arange(n_cmp)[:, None] * CMP_STRIDE
    ss = np.arange(n_sel)[None, :] * SEL_LEN
    ov = np.clip(np.minimum(cs + CMP_LEN, ss + SEL_LEN) - np.maximum(cs, ss), 0, None) / CMP_LEN
    out = np.zeros((n_sel, seq // CMP_STRIDE), np.float32)
    out[:, :n_cmp] = ov.T
    return jnp.asarray(out, dtype=BF16)


def _values_t(v, batch, n):
    vt = v.reshape(batch, n, N_GROUPS, HEAD_DIM).transpose(0, 2, 3, 1)
    ones = jnp.ones((batch, N_GROUPS, V_ROWS - HEAD_DIM, n), BF16)
    return jnp.concatenate([vt, ones], axis=2)


def _nsa(q, kc, vc, ks, vs, kw, vw, gates, *, tq=128):
    batch, _, seq, _ = q.shape
    n_cmp = kc.shape[2]
    n_sel = seq // SEL_LEN
    assert seq % SLC_CHUNK == 0 and seq >= WINDOW + tq and SLC_CHUNK % tq == 0 and tq == LANES
    assert n_sel <= 2 * SEL_LEN
    ovt = _overlap_t(seq)
    vct, vst, vwt = _values_t(vc, batch, n_cmp), _values_t(vs, batch, seq), _values_t(vw, batch, seq)
    keys = lambda n: pl.BlockSpec((1, N_GROUPS, n, LANES), lambda b, i: (b, 0, 0, 0))
    vals = lambda n: pl.BlockSpec((1, N_GROUPS, V_ROWS, n), lambda b, i: (b, 0, 0, 0))
    tiles = seq // tq
    s_rows = max(n_cmp, WINDOW + tq, SLC_CHUNK)
    return pl.pallas_call(
        functools.partial(_nsa_kernel, tq=tq, seq=seq),
        out_shape=jax.ShapeDtypeStruct((batch * seq, HEADS_PER_GROUP * LANES), BF16),
        grid=(batch, tiles),
        in_specs=[
            pl.BlockSpec((1, N_HEADS, tq, LANES), lambda b, i: (b, 0, i, 0)),
            keys(n_cmp), vals(n_cmp), keys(seq), vals(seq), keys(seq), vals(seq),
            pl.BlockSpec((tq, LANES), lambda b, i: (b * tiles + i, 0)),
            pl.BlockSpec(ovt.shape, lambda b, i: (0, 0)),
        ],
        out_specs=pl.BlockSpec((tq, HEADS_PER_GROUP * LANES), lambda b, i: (b * tiles + i, 0)),
        scratch_shapes=[
            pltpu.VMEM((s_rows, N_HEADS * tq), F32),
            pltpu.VMEM((s_rows, N_HEADS * tq), BF16),
            pltpu.VMEM((LANES, tq), F32),
            pltpu.VMEM((N_GROUPS, n_sel, tq), F32),
            pltpu.SMEM((seq // SLC_CHUNK,), jnp.int32),
            pltpu.VMEM((N_HEADS, tq), F32),
            pltpu.VMEM((V_ROWS, N_HEADS * tq), F32),
            pltpu.VMEM((HEADS_PER_GROUP, N_GROUPS * HEAD_DIM, tq), F32),
        ],
        compiler_params=_params("parallel", "arbitrary"),
        name="nsa",
    )(q, kc, vct, ks, vst, kw, vwt, gates, ovt)


HALO = max(POOL_WINDOWS)


def _merge_kernel(x_ref, u_ref, halo_ref, gm_ref, on_ref, pw_ref, ps_ref, wbp_ref, wbn_ref, wo_ref,
                  o_ref, ext_ref, *, tm, seq):
    d = x_ref.shape[1]
    pos0 = (pl.program_id(0) * tm) % seq
    ext_ref[0:HALO, :] = jnp.where(pos0 == 0, 0.0, halo_ref[...])
    ext_ref[HALO:HALO + tm, :] = u_ref[...]
    pos = (pos0 + lax.broadcasted_iota(jnp.int32, (tm, POOL_GROUP_DIM), 0)).astype(F32)
    mixed = []
    for gi, w in enumerate(POOL_WINDOWS):
        cols = slice(gi * POOL_GROUP_DIM, (gi + 1) * POOL_GROUP_DIM)
        u = ext_ref[HALO:HALO + tm, cols]
        total = u
        for lag in range(1, w):
            total = total + ext_ref[HALO - lag:HALO - lag + tm, cols]
        delta = total / jnp.minimum(pos + 1.0, float(w)) - u
        mixed.append(_dot(delta.astype(BF16), pw_ref[gi]) * ps_ref[:, cols])
    mixed = jnp.concatenate(mixed, axis=-1).astype(BF16)
    a = _dot(mixed, wbp_ref[...])
    b = _dot(on_ref[...], wbn_ref[...])
    merged = gm_ref[:, 0:d] * a + gm_ref[:, d:2 * d] * b
    o_ref[...] = x_ref[...] + _dot(merged.astype(BF16), wo_ref[...])


def _merge(x, u, gm, o_nsa, pool_w, pool_scale, w_bp, w_bn, w_out, seq, *, tm=256):
    n, d = x.shape
    row = lambda i: (i, 0)
    full = lambda a: pl.BlockSpec(a.shape, lambda i: (0,) * a.ndim)
    halo_blocks = tm // HALO
    return pl.pallas_call(
        functools.partial(_merge_kernel, tm=tm, seq=seq),
        out_shape=jax.ShapeDtypeStruct((n, d), F32),
        grid=(n // tm,),
        in_specs=[
            pl.BlockSpec((tm, d), row),
            pl.BlockSpec((tm, POOL_COLS), row),
            pl.BlockSpec((HALO, POOL_COLS), lambda i: (jnp.maximum(i * halo_blocks - 1, 0), 0)),
            pl.BlockSpec((tm, 2 * d), row),
            pl.BlockSpec((tm, d), row),
            full(pool_w), full(pool_scale), full(w_bp), full(w_bn), full(w_out),
        ],
        out_specs=pl.BlockSpec((tm, d), row),
        scratch_shapes=[pltpu.VMEM((HALO + tm, POOL_COLS), F32)],
        compiler_params=_params("parallel"),
        name="merge",
    )(x, u, u, gm, o_nsa, pool_w, pool_scale, w_bp, w_bn, w_out)


def _chunked(kv, batch, seq):
    x = kv.reshape(batch, seq // CMP_STRIDE, CMP_STRIDE, N_GROUPS, HEAD_DIM)
    return x.transpose(0, 3, 1, 2, 4).reshape(batch, N_GROUPS, seq // CMP_STRIDE, CMP_STRIDE * HEAD_DIM)


def kernel(x, ffn1_norm, ffn1_w_gate, ffn1_w_up, ffn1_w_down, mix_norm, w_in, cmp_pos, cmp_k_w1, cmp_k_w2, cmp_v_w1, cmp_v_w2, pool_w, pool_scale, w_branch_pool, w_branch_nsa, w_out, ffn2_norm, ffn2_w_gate, ffn2_w_up, ffn2_w_down, final_norm):
    batch, seq, d = x.shape
    depth = w_in.shape[0]
    xf = x.reshape(batch * seq, d)
    bf = lambda a: a.astype(BF16)
    row = lambda a: a.reshape(1, -1)
    for l in range(depth):
        xf = _ffn(xf, row(ffn1_norm[l]), bf(ffn1_w_gate[l]), bf(ffn1_w_up[l]), bf(ffn1_w_down[l]),
                  row(final_norm), final_norm=False)

        q, kc, vc, ks, kw, vs, vw, gates, u, gm = _proj(xf, row(mix_norm[l]), _pack_w_in(w_in[l], d), batch, seq)
        pos2 = cmp_pos[l].reshape(2, (CMP_LEN // 2) * HEAD_DIM)
        k_cmp, v_cmp = _compress(_chunked(kc, batch, seq), _chunked(vc, batch, seq), pos2,
                                 bf(cmp_k_w1[l]), bf(cmp_k_w2[l]), bf(cmp_v_w1[l]), bf(cmp_v_w2[l]))
        o_nsa = _nsa(q, k_cmp, v_cmp, ks, vs, kw, vw, gates)

        w_bn = w_branch_nsa[l].reshape(N_GROUPS, HEADS_PER_GROUP, HEAD_DIM, d).transpose(1, 0, 2, 3).reshape(-1, d)
        xf = _merge(xf, u, gm, o_nsa, bf(pool_w[l]), row(pool_scale[l]), bf(w_branch_pool[l]), bf(w_bn),
                    bf(w_out[l]), seq)

        xf = _ffn(xf, row(ffn2_norm[l]), bf(ffn2_w_gate[l]), bf(ffn2_w_up[l]), bf(ffn2_w_down[l]),
                  row(final_norm), final_norm=(l == depth - 1))
    return xf.reshape(batch, seq, d)
```

```python
import functools

import jax
import jax.numpy as jnp
import numpy as np
from jax import lax
from jax.experimental import pallas as pl
from jax.experimental.pallas import tpu as pltpu

D_FF = 2816
N_HEADS = 16
N_GROUPS = 2
HEADS_PER_GROUP = N_HEADS // N_GROUPS
HEAD_DIM = 64
CMP_LEN = 32
CMP_STRIDE = 16
SEL_LEN = 64
N_SELECT = 16
WINDOW = 512
POOL_WINDOWS = (2, 4, 8, 16)
POOL_GROUP_DIM = 128
RMS_EPS = 1e-6
ALIBI_MAX_BIAS = 8.0

LANES = 128
NEG_BIG = -1e30
TOPK_BIG = 1e30
VMEM_LIMIT = 56 * 1024 * 1024

BF16 = jnp.bfloat16
F32 = jnp.float32


def _dot(a, b):
    return jnp.dot(a, b, preferred_element_type=F32)


def _dot_nt(a, b):
    return lax.dot_general(a, b, (((1,), (1,)), ((), ())), preferred_element_type=F32)


def _rms(x, g):
    return x * lax.rsqrt(jnp.mean(x * x, axis=-1, keepdims=True) + RMS_EPS) * g


def _params(*sem):
    return pltpu.CompilerParams(dimension_semantics=sem, vmem_limit_bytes=VMEM_LIMIT)


def _ffn_kernel(x_ref, g_ref, wg_ref, wu_ref, wd_ref, fin_ref, o_ref, *, final_norm):
    x = x_ref[...]
    xn = _rms(x, g_ref[...]).astype(BF16)
    gate = _dot(xn, wg_ref[...])
    up = _dot(xn, wu_ref[...])
    act = (gate * jax.nn.sigmoid(gate)) * up
    y = x + 0.5 * _dot(act.astype(BF16), wd_ref[...])
    if final_norm:
        y = _rms(y, fin_ref[...])
    o_ref[...] = y


def _ffn(x, norm_g, wg, wu, wd, fin_g, *, final_norm, tm=512):
    n, d = x.shape
    row = lambda i: (i, 0)
    resident = lambda a: pl.BlockSpec(a.shape, lambda i: (0, 0), pipeline_mode=pl.Buffered(1))
    return pl.pallas_call(
        functools.partial(_ffn_kernel, final_norm=final_norm),
        out_shape=jax.ShapeDtypeStruct((n, d), F32),
        grid=(n // tm,),
        in_specs=[pl.BlockSpec((tm, d), row), resident(norm_g), resident(wg), resident(wu), resident(wd),
                  resident(fin_g)],
        out_specs=pl.BlockSpec((tm, d), row),
        compiler_params=_params("parallel"),
        name="ffn",
    )(x, norm_g, wg, wu, wd, fin_g)


LOG2E = 1.4426950408889634
N_FEAT = 6


def _bf16_pieces(x):
    x = np.asarray(x, np.float32)
    s1 = x.astype(BF16).astype(np.float32)
    s2 = (x - s1).astype(BF16).astype(np.float32)
    s3 = (x - s1 - s2).astype(BF16).astype(np.float32)
    return s1, s2, s3


def _query_features():
    slopes = np.float32(2.0) ** (-ALIBI_MAX_BIAS * np.arange(1, N_HEADS + 1, dtype=np.float32) / N_HEADS)
    s1, s2, s3 = _bf16_pieces(slopes * np.float32(LOG2E))
    feat = np.zeros((N_HEADS, LANES), np.float32)
    feat[:, HEAD_DIM:HEAD_DIM + N_FEAT] = np.stack([s1, s2, s3, SEL_LEN * s1, SEL_LEN * s2, SEL_LEN * s3], axis=1)
    return jnp.asarray(feat)


def _key_features(pos, width, offset):
    pos = np.asarray(pos)
    a, b = (pos // SEL_LEN).astype(np.float32), (pos % SEL_LEN).astype(np.float32)
    feat = np.zeros((len(pos), width), np.float32)
    feat[:, offset:offset + N_FEAT] = np.stack([b, b, b, a, a, a], axis=1)
    return jnp.asarray(feat)


Q_COLS = N_HEADS * LANES
CMP_OFF = Q_COLS
KEY_OFF = CMP_OFF + 2 * LANES
VAL_OFF = KEY_OFF + 2 * N_GROUPS * LANES
GATE_OFF = VAL_OFF + 2 * LANES
POOL_OFF = GATE_OFF + LANES
POOL_COLS = len(POOL_WINDOWS) * POOL_GROUP_DIM
MERGE_OFF = POOL_OFF + POOL_COLS


V_ROWS = HEAD_DIM + 16


def _store_values_t(v_ref, v):
    vt = v.T
    for g in range(N_GROUPS):
        v_ref[0, g, 0:HEAD_DIM, :] = vt[g * HEAD_DIM:(g + 1) * HEAD_DIM, :].astype(BF16)
        v_ref[0, g, HEAD_DIM:V_ROWS, :] = jnp.ones((V_ROWS - HEAD_DIM, v.shape[0]), BF16)


def _proj_kernel(x_ref, g_ref, w_ref, qf_ref, kf_ref, q_ref, kc_ref, vc_ref, ks_ref, kw_ref, vs_ref, vw_ref,
                 gate_ref, u_ref, gm_ref):
    d = x_ref.shape[1]
    hn = _rms(x_ref[...], g_ref[...]).astype(BF16)
    q_scale = HEAD_DIM ** -0.5 * LOG2E
    for h in range(N_HEADS):
        q = _dot(hn, w_ref[:, h * LANES:(h + 1) * LANES]) * q_scale + qf_ref[h:h + 1, :]
        q_ref[0, h] = q.astype(BF16)
    cmp_in = _dot(hn, w_ref[:, CMP_OFF:CMP_OFF + 2 * LANES])
    kc_ref[...] = cmp_in[:, 0:LANES]
    vc_ref[...] = cmp_in[:, LANES:2 * LANES]
    keys = _dot(hn, w_ref[:, KEY_OFF:KEY_OFF + 2 * N_GROUPS * LANES])
    kf = kf_ref[...]
    for g in range(N_GROUPS):
        ks_ref[0, g] = (keys[:, g * LANES:(g + 1) * LANES] + kf).astype(BF16)
        kw_ref[0, g] = (keys[:, (N_GROUPS + g) * LANES:(N_GROUPS + g + 1) * LANES] + kf).astype(BF16)
    vals = _dot(hn, w_ref[:, VAL_OFF:VAL_OFF + 2 * LANES])
    for i, v_ref in enumerate((vs_ref, vw_ref)):
        _store_values_t(v_ref, vals[:, i * LANES:(i + 1) * LANES])
    gate_ref[...] = jax.nn.sigmoid(_dot(hn, w_ref[:, GATE_OFF:GATE_OFF + LANES]))
    u_ref[...] = _dot(hn, w_ref[:, POOL_OFF:POOL_OFF + POOL_COLS])
    gm_ref[...] = jax.nn.sigmoid(_dot(hn, w_ref[:, MERGE_OFF:MERGE_OFF + 2 * d]))


def _pack_w_in(w_in, d):
    qw = N_HEADS * HEAD_DIM
    kvw = N_GROUPS * HEAD_DIM

    def padded(w, n):
        w = w.reshape(d, n, HEAD_DIM)
        return jnp.concatenate([w, jnp.zeros_like(w)], axis=-1).reshape(d, n * LANES)

    kc, vc, ks, vs, kw, vw = (w_in[:, qw + i * kvw:qw + (i + 1) * kvw] for i in range(6))
    off = qw + 6 * kvw
    n_g = 3 * N_HEADS
    w_g = jnp.pad(w_in[:, off:off + n_g], ((0, 0), (0, LANES - n_g)))
    w_rest = w_in[:, off + n_g:]
    return jnp.concatenate([padded(w_in[:, :qw], N_HEADS), kc, vc, padded(ks, N_GROUPS), padded(kw, N_GROUPS),
                            vs, vw, w_g, w_rest], axis=1).astype(BF16)


def _proj(x, norm_g, w_packed, batch, seq, *, tm=256):
    n, d = x.shape
    tiles_per_seq = seq // tm
    row = lambda i: (i, 0)
    const = lambda i: (0, 0)
    per_group = lambda i: (i // tiles_per_seq, 0, i % tiles_per_seq, 0)
    qf = _query_features()
    kf = _key_features(np.arange(seq), LANES, HEAD_DIM)
    flat = lambda width, dtype: (jax.ShapeDtypeStruct((n, width), dtype), pl.BlockSpec((tm, width), row))
    grouped = lambda count: (jax.ShapeDtypeStruct((batch, count, seq, LANES), BF16),
                             pl.BlockSpec((1, count, tm, LANES), per_group))
    values_t = (jax.ShapeDtypeStruct((batch, N_GROUPS, V_ROWS, seq), BF16),
                pl.BlockSpec((1, N_GROUPS, V_ROWS, tm), lambda i: (i // tiles_per_seq, 0, 0, i % tiles_per_seq)))
    outs = [grouped(N_HEADS), flat(LANES, F32), flat(LANES, F32), grouped(N_GROUPS), grouped(N_GROUPS),
            values_t, values_t, flat(LANES, F32), flat(POOL_COLS, F32), flat(2 * d, F32)]
    return pl.pallas_call(
        _proj_kernel,
        out_shape=[o[0] for o in outs],
        grid=(n // tm,),
        in_specs=[pl.BlockSpec((tm, d), row), pl.BlockSpec((1, d), const), pl.BlockSpec(w_packed.shape, const),
                  pl.BlockSpec(qf.shape, const), pl.BlockSpec((tm, LANES), lambda i: (i % tiles_per_seq, 0))],
        out_specs=[o[1] for o in outs],
        compiler_params=_params("parallel"),
        name="proj",
    )(x, norm_g, w_packed, qf, kf)


def _compress_kernel(ck_ref, cv_ref, pos_ref, wk1_ref, wk2_ref, wv1_ref, wv2_ref, feat_ref, ko_ref, vo_ref):
    half = pos_ref.shape[1]
    n_chunk = ck_ref.shape[2]
    pos_a = pos_ref[0:1, :]
    pos_b = pos_ref[1:2, :]

    def compress(c_ref, w1_ref, w2_ref, g):
        c = c_ref[0, g]
        first = _dot((c + pos_a).astype(BF16), w1_ref[0:half, :])
        second = _dot((c + pos_b).astype(BF16), w1_ref[half:2 * half, :])
        pre = first + pltpu.roll(second, n_chunk - 1, 0)
        return _dot(jax.nn.gelu(pre).astype(BF16), w2_ref[...])

    for g in range(N_GROUPS):
        ko_ref[0, g] = jnp.concatenate([compress(ck_ref, wk1_ref, wk2_ref, g), feat_ref[...]], axis=-1).astype(BF16)
    _store_values_t(vo_ref, jnp.concatenate([compress(cv_ref, wv1_ref, wv2_ref, g) for g in range(N_GROUPS)], axis=-1))


def _compress(ck, cv, pos2, wk1, wk2, wv1, wv2):
    batch, _, n_chunk, width = ck.shape
    blk = pl.BlockSpec((1, N_GROUPS, n_chunk, width), lambda b: (b, 0, 0, 0))
    full = lambda a: pl.BlockSpec(a.shape, lambda b: (0,) * a.ndim)
    feat = _key_features(np.arange(n_chunk) * CMP_STRIDE + CMP_LEN - 1, HEAD_DIM, 0)
    return pl.pallas_call(
        _compress_kernel,
        out_shape=[jax.ShapeDtypeStruct((batch, N_GROUPS, n_chunk, LANES), BF16),
                   jax.ShapeDtypeStruct((batch, N_GROUPS, V_ROWS, n_chunk), BF16)],
        grid=(batch,),
        in_specs=[blk, blk, full(pos2), full(wk1), full(wk2), full(wv1), full(wv2), full(feat)],
        out_specs=[pl.BlockSpec((1, N_GROUPS, n_chunk, LANES), lambda b: (b, 0, 0, 0)),
                   pl.BlockSpec((1, N_GROUPS, V_ROWS, n_chunk), lambda b: (b, 0, 0, 0))],
        compiler_params=_params("parallel"),
        name="compress",
    )(ck, cv, pos2, wk1, wk2, wv1, wv2, feat)


SLC_CHUNK = 256
BLOCKS_PER_CHUNK = SLC_CHUNK // SEL_LEN


def _nsa_kernel(q_ref, kc_ref, vct_ref, ks_ref, vst_ref, kw_ref, vwt_ref, gate_ref, ovt_ref, o_ref,
                st_ref, p_ref, gt_ref, selt_ref, flag_ref, m_ref, acc_ref, out_ref, *, tq, seq):
    n_sel = seq // SEL_LEN
    n_chunks = seq // SLC_CHUNK
    n_cmp = kc_ref.shape[2]
    n_win = WINDOW + tq
    q0 = pl.program_id(1) * tq
    gt_ref[...] = gate_ref[...].T
    lanes = lambda head: slice(head * tq, (head + 1) * tq)
    pairs = [(g, g * HEADS_PER_GROUP + 2 * j) for g in range(N_GROUPS) for j in range(HEADS_PER_GROUP // 2)]

    def gate_row(head, branch):
        c = 3 * head + branch
        return gt_ref[c:c + 1, :]

    def out_rows(head):
        g, hh = divmod(head, HEADS_PER_GROUP)
        return hh, slice(g * HEAD_DIM, (g + 1) * HEAD_DIM)

    def staged(keys, n_keys, consume):
        for g, head0 in pairs:
            q_pair = q_ref[0, head0:head0 + 2].reshape(2 * tq, LANES)
            st_ref[0:n_keys, head0 * tq:(head0 + 2) * tq] = _dot_nt(keys(g), q_pair)
        for g, head0 in pairs:
            consume(g, head0)

    def finish_pair(head0, pv, branch, first, ok=None):
        invs = []
        for i in range(2):
            cols = slice(i * tq, (i + 1) * tq)
            inv = 1.0 / pv[HEAD_DIM:HEAD_DIM + 1, cols]
            if ok is not None:
                inv = jnp.where(ok, inv, 0.0)
            hh, rows = out_rows(head0 + i)
            contrib = pv[0:HEAD_DIM, cols] * (inv * gate_row(head0 + i, branch))
            out_ref[hh, rows, :] = contrib if first else out_ref[hh, rows, :] + contrib
            invs.append(inv)
        return invs

    t_c = q0 + lax.broadcasted_iota(jnp.int32, (n_cmp, tq), 1)
    end_c = lax.broadcasted_iota(jnp.int32, (n_cmp, tq), 0) * CMP_STRIDE + (CMP_LEN - 1)
    bias_c = jnp.where(t_c >= end_c, 0.0, NEG_BIG)
    has_cmp = (q0 + lax.broadcasted_iota(jnp.int32, (1, tq), 1)) >= CMP_LEN - 1
    inv_c = [None] * N_HEADS

    def cmp_pair(g, head0):
        for head in (head0, head0 + 1):
            x = st_ref[0:n_cmp, lanes(head)] + bias_c
            p_ref[0:n_cmp, lanes(head)] = jnp.exp2(x - jnp.max(x, axis=0, keepdims=True)).astype(BF16)
        pv = _dot(vct_ref[0, g], p_ref[0:n_cmp, head0 * tq:(head0 + 2) * tq])
        inv_c[head0], inv_c[head0 + 1] = finish_pair(head0, pv, 0, True, has_cmp)

    staged(lambda g: kc_ref[0, g], n_cmp, cmp_pair)

    raw = _dot(ovt_ref[...], p_ref[0:n_cmp, :])
    blk = lax.broadcasted_iota(jnp.int32, (n_sel, tq), 0)
    cur = (q0 + lax.broadcasted_iota(jnp.int32, (n_sel, tq), 1)) // SEL_LEN
    valid = blk <= cur
    forced = valid & ((blk == 0) | (blk == cur) | (blk == cur - 1))
    blk_f = blk.astype(F32)
    score, sel = [], []
    for g in range(N_GROUPS):
        total = None
        for head in range(g * HEADS_PER_GROUP, (g + 1) * HEADS_PER_GROUP):
            part = raw[:, lanes(head)] * inv_c[head]
            total = part if total is None else total + part
        score.append(jnp.where(forced, TOPK_BIG, jnp.where(valid, total, -TOPK_BIG)))
        sel.append(jnp.zeros((n_sel, tq), F32))

    for _ in range(min(N_SELECT, n_sel)):
        for g in range(N_GROUPS):
            best = jnp.max(score[g], axis=0, keepdims=True)
            pick = jnp.min(jnp.where(score[g] == best, blk_f, float(n_sel)), axis=0, keepdims=True)
            hit = blk_f == pick
            sel[g] = jnp.where(hit, 1.0, sel[g])
            score[g] = jnp.where(hit, -TOPK_BIG, score[g])
    for g in range(N_GROUPS):
        selt_ref[g] = sel[g]
    sel_any = sel[0] + sel[1]
    for c in range(n_chunks):
        any_sel = jnp.max(sel_any[c * BLOCKS_PER_CHUNK:(c + 1) * BLOCKS_PER_CHUNK, :])
        flag_ref[c] = (any_sel > 0.5).astype(jnp.int32)

    w0 = pl.multiple_of(jnp.maximum(q0 - WINDOW, 0), tq)
    dist_w = (q0 + lax.broadcasted_iota(jnp.int32, (n_win, tq), 1)) - (w0 + lax.broadcasted_iota(jnp.int32, (n_win, tq), 0))
    bias_w = jnp.where((dist_w >= 0) & (dist_w < WINDOW), 0.0, NEG_BIG)

    def win_pair(g, head0):
        for head in (head0, head0 + 1):
            x = st_ref[0:n_win, lanes(head)] + bias_w
            p_ref[0:n_win, lanes(head)] = jnp.exp2(x - jnp.max(x, axis=0, keepdims=True)).astype(BF16)
        pv = _dot(vwt_ref[0, g, :, pl.ds(w0, n_win)], p_ref[0:n_win, head0 * tq:(head0 + 2) * tq])
        finish_pair(head0, pv, 2, False)

    staged(lambda g: kw_ref[0, g, pl.ds(w0, n_win), :], n_win, win_pair)

    m_ref[...] = jnp.full_like(m_ref, NEG_BIG)
    acc_ref[...] = jnp.zeros_like(acc_ref)
    base = (lax.broadcasted_iota(jnp.int32, (SLC_CHUNK, tq), 1)
            - lax.broadcasted_iota(jnp.int32, (SLC_CHUNK, tq), 0))

    def chunk_body(c, carry):
        @pl.when(flag_ref[c] != 0)
        def _():
            k0 = pl.multiple_of(c * SLC_CHUNK, SLC_CHUNK)
            causal = base >= k0 - q0
            bias = []
            for g in range(N_GROUPS):
                sel_keys = jnp.concatenate(
                    [jnp.broadcast_to(selt_ref[g, pl.ds(c * BLOCKS_PER_CHUNK + i, 1), :], (SEL_LEN, tq))
                     for i in range(BLOCKS_PER_CHUNK)], axis=0)
                bias.append(jnp.where(causal & (sel_keys > 0.5), 0.0, NEG_BIG))

            def slc_pair(g, head0):
                alphas = []
                for head in (head0, head0 + 1):
                    x = st_ref[0:SLC_CHUNK, lanes(head)] + bias[g]
                    m_old = m_ref[head:head + 1, :]
                    m_new = jnp.maximum(m_old, jnp.max(x, axis=0, keepdims=True))
                    alphas.append(jnp.exp2(m_old - m_new))
                    m_ref[head:head + 1, :] = m_new
                    p_ref[0:SLC_CHUNK, lanes(head)] = jnp.exp2(x - m_new).astype(BF16)
                both = slice(head0 * tq, (head0 + 2) * tq)
                pv = _dot(vst_ref[0, g, :, pl.ds(k0, SLC_CHUNK)], p_ref[0:SLC_CHUNK, both])
                acc_ref[:, both] = acc_ref[:, both] * jnp.concatenate(alphas, axis=1) + pv

            staged(lambda g: ks_ref[0, g, pl.ds(k0, SLC_CHUNK), :], SLC_CHUNK, slc_pair)
        return carry

    lax.fori_loop(0, (q0 + tq + SLC_CHUNK - 1) // SLC_CHUNK, chunk_body, 0)
    for g, head0 in pairs:
        finish_pair(head0, acc_ref[:, head0 * tq:(head0 + 2) * tq], 1, False)

    for hh in range(HEADS_PER_GROUP):
        o_ref[:, hh * LANES:(hh + 1) * LANES] = out_ref[hh].T.astype(BF16)


def _overlap_t(seq):
    n_cmp = (seq - CMP_LEN) // CMP_STRIDE + 1
    n_sel = seq // SEL_LEN
    cs = np.arange(n_cmp)[:, None] * CMP_STRIDE
    ss = np.arange(n_sel)[None, :] * SEL_LEN
    ov = np.clip(np.minimum(cs + CMP_LEN, ss + SEL_LEN) - np.maximum(cs, ss), 0, None) / CMP_LEN
    out = np.zeros((n_sel, seq // CMP_STRIDE), np.float32)
    out[:, :n_cmp] = ov.T
    return jnp.asarray(out, dtype=BF16)


def _nsa(q, kc, vct, ks, vst, kw, vwt, gates, *, tq=128):
    batch, _, seq, _ = q.shape
    n_cmp = kc.shape[2]
    n_sel = seq // SEL_LEN
    assert seq % SLC_CHUNK == 0 and seq >= WINDOW + tq and SLC_CHUNK % tq == 0 and tq == LANES
    assert n_sel <= 2 * SEL_LEN
    ovt = _overlap_t(seq)
    keys = lambda n: pl.BlockSpec((1, N_GROUPS, n, LANES), lambda b, i: (b, 0, 0, 0))
    vals = lambda n: pl.BlockSpec((1, N_GROUPS, V_ROWS, n), lambda b, i: (b, 0, 0, 0))
    tiles = seq // tq
    s_rows = max(n_cmp, WINDOW + tq, SLC_CHUNK)
    return pl.pallas_call(
        functools.partial(_nsa_kernel, tq=tq, seq=seq),
        out_shape=jax.ShapeDtypeStruct((batch * seq, HEADS_PER_GROUP * LANES), BF16),
        grid=(batch, tiles),
        in_specs=[
            pl.BlockSpec((1, N_HEADS, tq, LANES), lambda b, i: (b, 0, i, 0)),
            keys(n_cmp), vals(n_cmp), keys(seq), vals(seq), keys(seq), vals(seq),
            pl.BlockSpec((tq, LANES), lambda b, i: (b * tiles + i, 0)),
            pl.BlockSpec(ovt.shape, lambda b, i: (0, 0)),
        ],
        out_specs=pl.BlockSpec((tq, HEADS_PER_GROUP * LANES), lambda b, i: (b * tiles + i, 0)),
        scratch_shapes=[
            pltpu.VMEM((s_rows, N_HEADS * tq), F32),
            pltpu.VMEM((s_rows, N_HEADS * tq), BF16),
            pltpu.VMEM((LANES, tq), F32),
            pltpu.VMEM((N_GROUPS, n_sel, tq), F32),
            pltpu.SMEM((seq // SLC_CHUNK,), jnp.int32),
            pltpu.VMEM((N_HEADS, tq), F32),
            pltpu.VMEM((V_ROWS, N_HEADS * tq), F32),
            pltpu.VMEM((HEADS_PER_GROUP, N_GROUPS * HEAD_DIM, tq), F32),
        ],
        compiler_params=_params("parallel", "arbitrary"),
        name="nsa",
    )(q, kc, vct, ks, vst, kw, vwt, gates, ovt)


HALO = max(POOL_WINDOWS)


def _merge_kernel(x_ref, u_ref, halo_ref, gm_ref, on_ref, pw_ref, ps_ref, wbp_ref, wbn_ref, wo_ref,
                  o_ref, ext_ref, *, tm, seq):
    d = x_ref.shape[1]
    pos0 = (pl.program_id(0) * tm) % seq
    ext_ref[0:HALO, :] = jnp.where(pos0 == 0, 0.0, halo_ref[...])
    ext_ref[HALO:HALO + tm, :] = u_ref[...]
    pos = (pos0 + lax.broadcasted_iota(jnp.int32, (tm, POOL_GROUP_DIM), 0)).astype(F32)
    mixed = []
    for gi, w in enumerate(POOL_WINDOWS):
        cols = slice(gi * POOL_GROUP_DIM, (gi + 1) * POOL_GROUP_DIM)
        u = ext_ref[HALO:HALO + tm, cols]
        total = u
        for lag in range(1, w):
            total = total + ext_ref[HALO - lag:HALO - lag + tm, cols]
        delta = total / jnp.minimum(pos + 1.0, float(w)) - u
        mixed.append(_dot(delta.astype(BF16), pw_ref[gi]) * ps_ref[:, cols])
    mixed = jnp.concatenate(mixed, axis=-1).astype(BF16)
    a = _dot(mixed, wbp_ref[...])
    b = _dot(on_ref[...], wbn_ref[...])
    merged = gm_ref[:, 0:d] * a + gm_ref[:, d:2 * d] * b
    o_ref[...] = x_ref[...] + _dot(merged.astype(BF16), wo_ref[...])


def _merge(x, u, gm, o_nsa, pool_w, pool_scale, w_bp, w_bn, w_out, seq, *, tm=256):
    n, d = x.shape
    row = lambda i: (i, 0)
    full = lambda a: pl.BlockSpec(a.shape, lambda i: (0,) * a.ndim)
    halo_blocks = tm // HALO
    return pl.pallas_call(
        functools.partial(_merge_kernel, tm=tm, seq=seq),
        out_shape=jax.ShapeDtypeStruct((n, d), F32),
        grid=(n // tm,),
        in_specs=[
            pl.BlockSpec((tm, d), row),
            pl.BlockSpec((tm, POOL_COLS), row),
            pl.BlockSpec((HALO, POOL_COLS), lambda i: (jnp.maximum(i * halo_blocks - 1, 0), 0)),
            pl.BlockSpec((tm, 2 * d), row),
            pl.BlockSpec((tm, d), row),
            full(pool_w), full(pool_scale), full(w_bp), full(w_bn), full(w_out),
        ],
        out_specs=pl.BlockSpec((tm, d), row),
        scratch_shapes=[pltpu.VMEM((HALO + tm, POOL_COLS), F32)],
        compiler_params=_params("parallel"),
        name="merge",
    )(x, u, u, gm, o_nsa, pool_w, pool_scale, w_bp, w_bn, w_out)


def _chunked(kv, batch, seq):
    x = kv.reshape(batch, seq // CMP_STRIDE, CMP_STRIDE, N_GROUPS, HEAD_DIM)
    return x.transpose(0, 3, 1, 2, 4).reshape(batch, N_GROUPS, seq // CMP_STRIDE, CMP_STRIDE * HEAD_DIM)


def kernel(x, ffn1_norm, ffn1_w_gate, ffn1_w_up, ffn1_w_down, mix_norm, w_in, cmp_pos, cmp_k_w1, cmp_k_w2, cmp_v_w1, cmp_v_w2, pool_w, pool_scale, w_branch_pool, w_branch_nsa, w_out, ffn2_norm, ffn2_w_gate, ffn2_w_up, ffn2_w_down, final_norm):
    batch, seq, d = x.shape
    depth = w_in.shape[0]
    xf = x.reshape(batch * seq, d)
    bf = lambda a: a.astype(BF16)
    row = lambda a: a.reshape(1, -1)
    for l in range(depth):
        xf = _ffn(xf, row(ffn1_norm[l]), bf(ffn1_w_gate[l]), bf(ffn1_w_up[l]), bf(ffn1_w_down[l]),
                  row(final_norm), final_norm=False)

        q, kc, vc, ks, kw, vs, vw, gates, u, gm = _proj(xf, row(mix_norm[l]), _pack_w_in(w_in[l], d), batch, seq)
        pos2 = cmp_pos[l].reshape(2, (CMP_LEN // 2) * HEAD_DIM)
        k_cmp, v_cmp = _compress(_chunked(kc, batch, seq), _chunked(vc, batch, seq), pos2,
                                 bf(cmp_k_w1[l]), bf(cmp_k_w2[l]), bf(cmp_v_w1[l]), bf(cmp_v_w2[l]))
        o_nsa = _nsa(q, k_cmp, v_cmp, ks, vs, kw, vw, gates)

        w_bn = w_branch_nsa[l].reshape(N_GROUPS, HEADS_PER_GROUP, HEAD_DIM, d).transpose(1, 0, 2, 3).reshape(-1, d)
        xf = _merge(xf, u, gm, o_nsa, bf(pool_w[l]), row(pool_scale[l]), bf(w_branch_pool[l]), bf(w_bn),
                    bf(w_out[l]), seq)

        xf = _ffn(xf, row(ffn2_norm[l]), bf(ffn2_w_gate[l]), bf(ffn2_w_up[l]), bf(ffn2_w_down[l]),
                  row(final_norm), final_norm=(l == depth - 1))
    return xf.reshape(batch, seq, d)
```

```python
import functools

import jax
import jax.numpy as jnp
import numpy as np
from jax import lax
from jax.experimental import pallas as pl
from jax.experimental.pallas import tpu as pltpu

D_FF = 2816
N_HEADS = 16
N_GROUPS = 2
HEADS_PER_GROUP = N_HEADS // N_GROUPS
HEAD_DIM = 64
CMP_LEN = 32
CMP_STRIDE = 16
SEL_LEN = 64
N_SELECT = 16
WINDOW = 512
POOL_WINDOWS = (2, 4, 8, 16)
POOL_GROUP_DIM = 128
RMS_EPS = 1e-6
ALIBI_MAX_BIAS = 8.0

LANES = 128
NEG_BIG = -1e30
TOPK_BIG = 1e30
VMEM_LIMIT = 56 * 1024 * 1024

BF16 = jnp.bfloat16
F32 = jnp.float32


def _dot(a, b):
    return jnp.dot(a, b, preferred_element_type=F32)


def _dot_nt(a, b):
    return lax.dot_general(a, b, (((1,), (1,)), ((), ())), preferred_element_type=F32)


def _rms(x, g):
    return x * lax.rsqrt(jnp.mean(x * x, axis=-1, keepdims=True) + RMS_EPS) * g


def _params(*sem):
    return pltpu.CompilerParams(dimension_semantics=sem, vmem_limit_bytes=VMEM_LIMIT)


def _ffn_kernel(x_ref, g_ref, wg_ref, wu_ref, wd_ref, fin_ref, o_ref, *, final_norm):
    x = x_ref[...]
    xn = _rms(x, g_ref[...]).astype(BF16)
    gate = _dot(xn, wg_ref[...])
    up = _dot(xn, wu_ref[...])
    act = (gate * jax.nn.sigmoid(gate)) * up
    y = x + 0.5 * _dot(act.astype(BF16), wd_ref[...])
    if final_norm:
        y = _rms(y, fin_ref[...])
    o_ref[...] = y


def _ffn(x, norm_g, wg, wu, wd, fin_g, *, final_norm, tm=512):
    n, d = x.shape
    row = lambda i: (i, 0)
    resident = lambda a: pl.BlockSpec(a.shape, lambda i: (0, 0), pipeline_mode=pl.Buffered(1))
    return pl.pallas_call(
        functools.partial(_ffn_kernel, final_norm=final_norm),
        out_shape=jax.ShapeDtypeStruct((n, d), F32),
        grid=(n // tm,),
        in_specs=[pl.BlockSpec((tm, d), row), resident(norm_g), resident(wg), resident(wu), resident(wd),
                  resident(fin_g)],
        out_specs=pl.BlockSpec((tm, d), row),
        compiler_params=_params("parallel"),
        name="ffn",
    )(x, norm_g, wg, wu, wd, fin_g)


LOG2E = 1.4426950408889634
N_FEAT = 6


def _bf16_pieces(x):
    x = np.asarray(x, np.float32)
    s1 = x.astype(BF16).astype(np.float32)
    s2 = (x - s1).astype(BF16).astype(np.float32)
    s3 = (x - s1 - s2).astype(BF16).astype(np.float32)
    return s1, s2, s3


def _query_features():
    slopes = np.float32(2.0) ** (-ALIBI_MAX_BIAS * np.arange(1, N_HEADS + 1, dtype=np.float32) / N_HEADS)
    s1, s2, s3 = _bf16_pieces(slopes * np.float32(LOG2E))
    feat = np.zeros((N_HEADS, LANES), np.float32)
    feat[:, HEAD_DIM:HEAD_DIM + N_FEAT] = np.stack([s1, s2, s3, SEL_LEN * s1, SEL_LEN * s2, SEL_LEN * s3], axis=1)
    return jnp.asarray(feat)


def _key_features(pos, width, offset):
    pos = np.asarray(pos)
    a, b = (pos // SEL_LEN).astype(np.float32), (pos % SEL_LEN).astype(np.float32)
    feat = np.zeros((len(pos), width), np.float32)
    feat[:, offset:offset + N_FEAT] = np.stack([b, b, b, a, a, a], axis=1)
    return jnp.asarray(feat)


Q_COLS = N_HEADS * LANES
CMP_OFF = Q_COLS
KEY_OFF = CMP_OFF + 2 * LANES
VAL_OFF = KEY_OFF + 2 * N_GROUPS * LANES
GATE_OFF = VAL_OFF + 2 * LANES
POOL_OFF = GATE_OFF + LANES
POOL_COLS = len(POOL_WINDOWS) * POOL_GROUP_DIM
MERGE_OFF = POOL_OFF + POOL_COLS


V_ROWS = HEAD_DIM + 16


def _store_values_t(v_ref, v):
    vt = v.T
    for g in range(N_GROUPS):
        v_ref[0, g, 0:HEAD_DIM, :] = vt[g * HEAD_DIM:(g + 1) * HEAD_DIM, :].astype(BF16)
        v_ref[0, g, HEAD_DIM:V_ROWS, :] = jnp.ones((V_ROWS - HEAD_DIM, v.shape[0]), BF16)


def _proj_kernel(x_ref, g_ref, w_ref, qf_ref, kf_ref, q_ref, kc_ref, vc_ref, ks_ref, kw_ref, vs_ref, vw_ref,
                 gate_ref, u_ref, gm_ref):
    d = x_ref.shape[1]
    hn = _rms(x_ref[...], g_ref[...]).astype(BF16)
    q_scale = HEAD_DIM ** -0.5 * LOG2E
    for h in range(0, N_HEADS, 2):
        q2 = _dot(hn, w_ref[:, h * LANES:(h + 2) * LANES]) * q_scale
        for i in range(2):
            q_ref[0, h + i] = (q2[:, i * LANES:(i + 1) * LANES] + qf_ref[h + i:h + i + 1, :]).astype(BF16)
    cmp_in = _dot(hn, w_ref[:, CMP_OFF:CMP_OFF + 2 * LANES])
    kc_ref[...] = cmp_in[:, 0:LANES]
    vc_ref[...] = cmp_in[:, LANES:2 * LANES]
    keys = _dot(hn, w_ref[:, KEY_OFF:KEY_OFF + 2 * N_GROUPS * LANES])
    kf = kf_ref[...]
    for g in range(N_GROUPS):
        ks_ref[0, g] = (keys[:, g * LANES:(g + 1) * LANES] + kf).astype(BF16)
        kw_ref[0, g] = (keys[:, (N_GROUPS + g) * LANES:(N_GROUPS + g + 1) * LANES] + kf).astype(BF16)
    vals = _dot(hn, w_ref[:, VAL_OFF:VAL_OFF + 2 * LANES])
    for i, v_ref in enumerate((vs_ref, vw_ref)):
        _store_values_t(v_ref, vals[:, i * LANES:(i + 1) * LANES])
    gate_ref[...] = jax.nn.sigmoid(_dot(hn, w_ref[:, GATE_OFF:GATE_OFF + LANES]))
    u_ref[...] = _dot(hn, w_ref[:, POOL_OFF:POOL_OFF + POOL_COLS])
    gm_ref[...] = jax.nn.sigmoid(_dot(hn, w_ref[:, MERGE_OFF:MERGE_OFF + 2 * d]))


def _pack_w_in(w_in, d):
    qw = N_HEADS * HEAD_DIM
    kvw = N_GROUPS * HEAD_DIM

    def padded(w, n):
        w = w.reshape(d, n, HEAD_DIM)
        return jnp.concatenate([w, jnp.zeros_like(w)], axis=-1).reshape(d, n * LANES)

    kc, vc, ks, vs, kw, vw = (w_in[:, qw + i * kvw:qw + (i + 1) * kvw] for i in range(6))
    off = qw + 6 * kvw
    n_g = 3 * N_HEADS
    w_g = jnp.pad(w_in[:, off:off + n_g], ((0, 0), (0, LANES - n_g)))
    w_rest = w_in[:, off + n_g:]
    return jnp.concatenate([padded(w_in[:, :qw], N_HEADS), kc, vc, padded(ks, N_GROUPS), padded(kw, N_GROUPS),
                            vs, vw, w_g, w_rest], axis=1).astype(BF16)


def _proj(x, norm_g, w_packed, batch, seq, *, tm=512):
    n, d = x.shape
    tiles_per_seq = seq // tm
    row = lambda i: (i, 0)
    const = lambda i: (0, 0)
    per_group = lambda i: (i // tiles_per_seq, 0, i % tiles_per_seq, 0)
    qf = _query_features()
    kf = _key_features(np.arange(seq), LANES, HEAD_DIM)
    flat = lambda width, dtype: (jax.ShapeDtypeStruct((n, width), dtype), pl.BlockSpec((tm, width), row))
    grouped = lambda count: (jax.ShapeDtypeStruct((batch, count, seq, LANES), BF16),
                             pl.BlockSpec((1, count, tm, LANES), per_group))
    values_t = (jax.ShapeDtypeStruct((batch, N_GROUPS, V_ROWS, seq), BF16),
                pl.BlockSpec((1, N_GROUPS, V_ROWS, tm), lambda i: (i // tiles_per_seq, 0, 0, i % tiles_per_seq)))
    outs = [grouped(N_HEADS), flat(LANES, F32), flat(LANES, F32), grouped(N_GROUPS), grouped(N_GROUPS),
            values_t, values_t, flat(LANES, F32), flat(POOL_COLS, F32), flat(2 * d, F32)]
    return pl.pallas_call(
        _proj_kernel,
        out_shape=[o[0] for o in outs],
        grid=(n // tm,),
        in_specs=[pl.BlockSpec((tm, d), row), pl.BlockSpec((1, d), const),
                  pl.BlockSpec(w_packed.shape, const, pipeline_mode=pl.Buffered(1)),
                  pl.BlockSpec(qf.shape, const), pl.BlockSpec((tm, LANES), lambda i: (i % tiles_per_seq, 0))],
        out_specs=[o[1] for o in outs],
        compiler_params=_params("parallel"),
        name="proj",
    )(x, norm_g, w_packed, qf, kf)


def _compress_kernel(ck_ref, cv_ref, pos_ref, wk1_ref, wk2_ref, wv1_ref, wv2_ref, feat_ref, ko_ref, vo_ref):
    half = pos_ref.shape[1]
    n_chunk = ck_ref.shape[2]
    pos_a = pos_ref[0:1, :]
    pos_b = pos_ref[1:2, :]

    def compress(c_ref, w1_ref, w2_ref, g):
        c = c_ref[0, g]
        first = _dot((c + pos_a).astype(BF16), w1_ref[0:half, :])
        second = _dot((c + pos_b).astype(BF16), w1_ref[half:2 * half, :])
        pre = first + pltpu.roll(second, n_chunk - 1, 0)
        return _dot(jax.nn.gelu(pre).astype(BF16), w2_ref[...])

    for g in range(N_GROUPS):
        ko_ref[0, g] = jnp.concatenate([compress(ck_ref, wk1_ref, wk2_ref, g), feat_ref[...]], axis=-1).astype(BF16)
    _store_values_t(vo_ref, jnp.concatenate([compress(cv_ref, wv1_ref, wv2_ref, g) for g in range(N_GROUPS)], axis=-1))


def _compress(ck, cv, pos2, wk1, wk2, wv1, wv2):
    batch, _, n_chunk, width = ck.shape
    blk = pl.BlockSpec((1, N_GROUPS, n_chunk, width), lambda b: (b, 0, 0, 0))
    full = lambda a: pl.BlockSpec(a.shape, lambda b: (0,) * a.ndim)
    feat = _key_features(np.arange(n_chunk) * CMP_STRIDE + CMP_LEN - 1, HEAD_DIM, 0)
    return pl.pallas_call(
        _compress_kernel,
        out_shape=[jax.ShapeDtypeStruct((batch, N_GROUPS, n_chunk, LANES), BF16),
                   jax.ShapeDtypeStruct((batch, N_GROUPS, V_ROWS, n_chunk), BF16)],
        grid=(batch,),
        in_specs=[blk, blk, full(pos2), full(wk1), full(wk2), full(wv1), full(wv2), full(feat)],
        out_specs=[pl.BlockSpec((1, N_GROUPS, n_chunk, LANES), lambda b: (b, 0, 0, 0)),
                   pl.BlockSpec((1, N_GROUPS, V_ROWS, n_chunk), lambda b: (b, 0, 0, 0))],
        compiler_params=_params("parallel"),
        name="compress",
    )(ck, cv, pos2, wk1, wk2, wv1, wv2, feat)


SLC_CHUNK = 256
BLOCKS_PER_CHUNK = SLC_CHUNK // SEL_LEN


def _nsa_kernel(q_ref, kc_ref, vct_ref, ks_ref, vst_ref, kw_ref, vwt_ref, gate_ref, ovt_ref, o_ref,
                st_ref, p_ref, gt_ref, selt_ref, flag_ref, m_ref, acc_ref, out_ref, *, tq, seq):
    n_sel = seq // SEL_LEN
    n_chunks = seq // SLC_CHUNK
    n_cmp = kc_ref.shape[2]
    n_win = WINDOW + tq
    q0 = pl.program_id(1) * tq
    gt_ref[...] = gate_ref[...].T
    lanes = lambda head: slice(head * tq, (head + 1) * tq)
    pairs = [(g, g * HEADS_PER_GROUP + 2 * j) for g in range(N_GROUPS) for j in range(HEADS_PER_GROUP // 2)]

    def gate_row(head, branch):
        c = 3 * head + branch
        return gt_ref[c:c + 1, :]

    def out_rows(head):
        g, hh = divmod(head, HEADS_PER_GROUP)
        return hh, slice(g * HEAD_DIM, (g + 1) * HEAD_DIM)

    def stage_scores(keys, n_keys):
        for g, head0 in pairs:
            q_pair = q_ref[0, head0:head0 + 2].reshape(2 * tq, LANES)
            st_ref[0:n_keys, head0 * tq:(head0 + 2) * tq] = _dot_nt(keys(g), q_pair)

    def finish_pair(head0, pv, branch, first, ok=None):
        invs = []
        for i in range(2):
            cols = slice(i * tq, (i + 1) * tq)
            inv = 1.0 / pv[HEAD_DIM:HEAD_DIM + 1, cols]
            if ok is not None:
                inv = jnp.where(ok, inv, 0.0)
            hh, rows = out_rows(head0 + i)
            contrib = pv[0:HEAD_DIM, cols] * (inv * gate_row(head0 + i, branch))
            out_ref[hh, rows, :] = contrib if first else out_ref[hh, rows, :] + contrib
            invs.append(inv)
        return invs

    t_c = q0 + lax.broadcasted_iota(jnp.int32, (n_cmp, tq), 1)
    end_c = lax.broadcasted_iota(jnp.int32, (n_cmp, tq), 0) * CMP_STRIDE + (CMP_LEN - 1)
    bias_c = jnp.where(t_c >= end_c, 0.0, NEG_BIG)
    has_cmp = (q0 + lax.broadcasted_iota(jnp.int32, (1, tq), 1)) >= CMP_LEN - 1
    inv_c = [None] * N_HEADS

    def cmp_pair(g, head0):
        for head in (head0, head0 + 1):
            x = st_ref[0:n_cmp, lanes(head)] + bias_c
            p_ref[0:n_cmp, lanes(head)] = jnp.exp2(x - jnp.max(x, axis=0, keepdims=True)).astype(BF16)
        pv = _dot(vct_ref[0, g], p_ref[0:n_cmp, head0 * tq:(head0 + 2) * tq])
        inv_c[head0], inv_c[head0 + 1] = finish_pair(head0, pv, 0, True, has_cmp)

    stage_scores(lambda g: kc_ref[0, g], n_cmp)
    for g, head0 in pairs:
        cmp_pair(g, head0)

    raw = _dot(ovt_ref[...], p_ref[0:n_cmp, :])

    w0 = pl.multiple_of(jnp.maximum(q0 - WINDOW, 0), tq)
    stage_scores(lambda g: kw_ref[0, g, pl.ds(w0, n_win), :], n_win)

    blk = lax.broadcasted_iota(jnp.int32, (n_sel, tq), 0)
    cur = (q0 + lax.broadcasted_iota(jnp.int32, (n_sel, tq), 1)) // SEL_LEN
    valid = blk <= cur
    forced = valid & ((blk == 0) | (blk == cur) | (blk == cur - 1))
    blk_f = blk.astype(F32)
    score, sel = [], []
    for g in range(N_GROUPS):
        total = None
        for head in range(g * HEADS_PER_GROUP, (g + 1) * HEADS_PER_GROUP):
            part = raw[:, lanes(head)] * inv_c[head]
            total = part if total is None else total + part
        score.append(jnp.where(forced, TOPK_BIG, jnp.where(valid, total, -TOPK_BIG)))
        sel.append(jnp.zeros((n_sel, tq), F32))

    for _ in range(min(N_SELECT, n_sel)):
        for g in range(N_GROUPS):
            best = jnp.max(score[g], axis=0, keepdims=True)
            pick = jnp.min(jnp.where(score[g] == best, blk_f, float(n_sel)), axis=0, keepdims=True)
            hit = blk_f == pick
            sel[g] = jnp.where(hit, 1.0, sel[g])
            score[g] = jnp.where(hit, -TOPK_BIG, score[g])
    for g in range(N_GROUPS):
        selt_ref[g] = sel[g]
    sel_any = sel[0] + sel[1]
    for c in range(n_chunks):
        any_sel = jnp.max(sel_any[c * BLOCKS_PER_CHUNK:(c + 1) * BLOCKS_PER_CHUNK, :])
        flag_ref[c] = (any_sel > 0.5).astype(jnp.int32)

    dist_w = (q0 + lax.broadcasted_iota(jnp.int32, (n_win, tq), 1)) - (w0 + lax.broadcasted_iota(jnp.int32, (n_win, tq), 0))
    bias_w = jnp.where((dist_w >= 0) & (dist_w < WINDOW), 0.0, NEG_BIG)

    def win_pair(g, head0):
        for head in (head0, head0 + 1):
            x = st_ref[0:n_win, lanes(head)] + bias_w
            p_ref[0:n_win, lanes(head)] = jnp.exp2(x - jnp.max(x, axis=0, keepdims=True)).astype(BF16)
        pv = _dot(vwt_ref[0, g, :, pl.ds(w0, n_win)], p_ref[0:n_win, head0 * tq:(head0 + 2) * tq])
        finish_pair(head0, pv, 2, False)

    for g, head0 in pairs:
        win_pair(g, head0)

    m_ref[...] = jnp.full_like(m_ref, NEG_BIG)
    acc_ref[...] = jnp.zeros_like(acc_ref)
    base = (lax.broadcasted_iota(jnp.int32, (SLC_CHUNK, tq), 1)
            - lax.broadcasted_iota(jnp.int32, (SLC_CHUNK, tq), 0))

    def chunk_body(c, carry):
        @pl.when(flag_ref[c] != 0)
        def _():
            k0 = pl.multiple_of(c * SLC_CHUNK, SLC_CHUNK)
            causal = base >= k0 - q0
            bias = []
            for g in range(N_GROUPS):
                sel_keys = jnp.concatenate(
                    [jnp.broadcast_to(selt_ref[g, pl.ds(c * BLOCKS_PER_CHUNK + i, 1), :], (SEL_LEN, tq))
                     for i in range(BLOCKS_PER_CHUNK)], axis=0)
                bias.append(jnp.where(causal & (sel_keys > 0.5), 0.0, NEG_BIG))

            def slc_pair(g, head0):
                alphas = []
                for head in (head0, head0 + 1):
                    x = st_ref[0:SLC_CHUNK, lanes(head)] + bias[g]
                    m_old = m_ref[head:head + 1, :]
                    m_new = jnp.maximum(m_old, jnp.max(x, axis=0, keepdims=True))
                    alphas.append(jnp.exp2(m_old - m_new))
                    m_ref[head:head + 1, :] = m_new
                    p_ref[0:SLC_CHUNK, lanes(head)] = jnp.exp2(x - m_new).astype(BF16)
                both = slice(head0 * tq, (head0 + 2) * tq)
                pv = _dot(vst_ref[0, g, :, pl.ds(k0, SLC_CHUNK)], p_ref[0:SLC_CHUNK, both])
                acc_ref[:, both] = acc_ref[:, both] * jnp.concatenate(alphas, axis=1) + pv

            stage_scores(lambda g: ks_ref[0, g, pl.ds(k0, SLC_CHUNK), :], SLC_CHUNK)
            for g, head0 in pairs:
                slc_pair(g, head0)
        return carry

    lax.fori_loop(0, (q0 + tq + SLC_CHUNK - 1) // SLC_CHUNK, chunk_body, 0)
    for g, head0 in pairs:
        finish_pair(head0, acc_ref[:, head0 * tq:(head0 + 2) * tq], 1, False)

    for hh in range(HEADS_PER_GROUP):
        o_ref[:, hh * LANES:(hh + 1) * LANES] = out_ref[hh].T.astype(BF16)


def _overlap_t(seq):
    n_cmp = (seq - CMP_LEN) // CMP_STRIDE + 1
    n_sel = seq // SEL_LEN
    cs = np.arange(n_cmp)[:, None] * CMP_STRIDE
    ss = np.arange(n_sel)[None, :] * SEL_LEN
    ov = np.clip(np.minimum(cs + CMP_LEN, ss + SEL_LEN) - np.maximum(cs, ss), 0, None) / CMP_LEN
    out = np.zeros((n_sel, seq // CMP_STRIDE), np.float32)
    out[:, :n_cmp] = ov.T
    return jnp.asarray(out, dtype=BF16)


def _nsa(q, kc, vct, ks, vst, kw, vwt, gates, *, tq=128):
    batch, _, seq, _ = q.shape
    n_cmp = kc.shape[2]
    n_sel = seq // SEL_LEN
    assert seq % SLC_CHUNK == 0 and seq >= WINDOW + tq and SLC_CHUNK % tq == 0 and tq == LANES
    assert n_sel <= 2 * SEL_LEN
    ovt = _overlap_t(seq)
    keys = lambda n: pl.BlockSpec((1, N_GROUPS, n, LANES), lambda b, i: (b, 0, 0, 0))
    vals = lambda n: pl.BlockSpec((1, N_GROUPS, V_ROWS, n), lambda b, i: (b, 0, 0, 0))
    tiles = seq // tq
    s_rows = max(n_cmp, WINDOW + tq, SLC_CHUNK)
    return pl.pallas_call(
        functools.partial(_nsa_kernel, tq=tq, seq=seq),
        out_shape=jax.ShapeDtypeStruct((batch * seq, HEADS_PER_GROUP * LANES), BF16),
        grid=(batch, tiles),
        in_specs=[
            pl.BlockSpec((1, N_HEADS, tq, LANES), lambda b, i: (b, 0, i, 0)),
            keys(n_cmp), vals(n_cmp), keys(seq), vals(seq), keys(seq), vals(seq),
            pl.BlockSpec((tq, LANES), lambda b, i: (b * tiles + i, 0)),
            pl.BlockSpec(ovt.shape, lambda b, i: (0, 0)),
        ],
        out_specs=pl.BlockSpec((tq, HEADS_PER_GROUP * LANES), lambda b, i: (b * tiles + i, 0)),
        scratch_shapes=[
            pltpu.VMEM((s_rows, N_HEADS * tq), F32),
            pltpu.VMEM((s_rows, N_HEADS * tq), BF16),
            pltpu.VMEM((LANES, tq), F32),
            pltpu.VMEM((N_GROUPS, n_sel, tq), F32),
            pltpu.SMEM((seq // SLC_CHUNK,), jnp.int32),
            pltpu.VMEM((N_HEADS, tq), F32),
            pltpu.VMEM((V_ROWS, N_HEADS * tq), F32),
            pltpu.VMEM((HEADS_PER_GROUP, N_GROUPS * HEAD_DIM, tq), F32),
        ],
        compiler_params=_params("parallel", "arbitrary"),
        name="nsa",
    )(q, kc, vct, ks, vst, kw, vwt, gates, ovt)


HALO = max(POOL_WINDOWS)


def _merge_kernel(x_ref, u_ref, halo_ref, gm_ref, on_ref, pw_ref, ps_ref, wbp_ref, wbn_ref, wo_ref,
                  o_ref, ext_ref, *, tm, seq):
    d = x_ref.shape[1]
    pos0 = (pl.program_id(0) * tm) % seq
    ext_ref[0:HALO, :] = jnp.where(pos0 == 0, 0.0, halo_ref[...])
    ext_ref[HALO:HALO + tm, :] = u_ref[...]
    pos = (pos0 + lax.broadcasted_iota(jnp.int32, (tm, POOL_GROUP_DIM), 0)).astype(F32)
    mixed = []
    for gi, w in enumerate(POOL_WINDOWS):
        cols = slice(gi * POOL_GROUP_DIM, (gi + 1) * POOL_GROUP_DIM)
        u = ext_ref[HALO:HALO + tm, cols]
        total = u
        for lag in range(1, w):
            total = total + ext_ref[HALO - lag:HALO - lag + tm, cols]
        delta = total / jnp.minimum(pos + 1.0, float(w)) - u
        mixed.append(_dot(delta.astype(BF16), pw_ref[gi]) * ps_ref[:, cols])
    mixed = jnp.concatenate(mixed, axis=-1).astype(BF16)
    a = _dot(mixed, wbp_ref[...])
    b = _dot(on_ref[...], wbn_ref[...])
    merged = gm_ref[:, 0:d] * a + gm_ref[:, d:2 * d] * b
    o_ref[...] = x_ref[...] + _dot(merged.astype(BF16), wo_ref[...])


def _merge(x, u, gm, o_nsa, pool_w, pool_scale, w_bp, w_bn, w_out, seq, *, tm=512):
    n, d = x.shape
    row = lambda i: (i, 0)
    full = lambda a: pl.BlockSpec(a.shape, lambda i: (0,) * a.ndim, pipeline_mode=pl.Buffered(1))
    halo_blocks = tm // HALO
    return pl.pallas_call(
        functools.partial(_merge_kernel, tm=tm, seq=seq),
        out_shape=jax.ShapeDtypeStruct((n, d), F32),
        grid=(n // tm,),
        in_specs=[
            pl.BlockSpec((tm, d), row),
            pl.BlockSpec((tm, POOL_COLS), row),
            pl.BlockSpec((HALO, POOL_COLS), lambda i: (jnp.maximum(i * halo_blocks - 1, 0), 0)),
            pl.BlockSpec((tm, 2 * d), row),
            pl.BlockSpec((tm, d), row),
            full(pool_w), full(pool_scale), full(w_bp), full(w_bn), full(w_out),
        ],
        out_specs=pl.BlockSpec((tm, d), row),
        scratch_shapes=[pltpu.VMEM((HALO + tm, POOL_COLS), F32)],
        compiler_params=_params("parallel"),
        name="merge",
    )(x, u, u, gm, o_nsa, pool_w, pool_scale, w_bp, w_bn, w_out)


def _chunked(kv, batch, seq):
    x = kv.reshape(batch, seq // CMP_STRIDE, CMP_STRIDE, N_GROUPS, HEAD_DIM)
    return x.transpose(0, 3, 1, 2, 4).reshape(batch, N_GROUPS, seq // CMP_STRIDE, CMP_STRIDE * HEAD_DIM)


def kernel(x, ffn1_norm, ffn1_w_gate, ffn1_w_up, ffn1_w_down, mix_norm, w_in, cmp_pos, cmp_k_w1, cmp_k_w2, cmp_v_w1, cmp_v_w2, pool_w, pool_scale, w_branch_pool, w_branch_nsa, w_out, ffn2_norm, ffn2_w_gate, ffn2_w_up, ffn2_w_down, final_norm):
    batch, seq, d = x.shape
    depth = w_in.shape[0]
    xf = x.reshape(batch * seq, d)
    bf = lambda a: a.astype(BF16)
    row = lambda a: a.reshape(1, -1)
    for l in range(depth):
        xf = _ffn(xf, row(ffn1_norm[l]), bf(ffn1_w_gate[l]), bf(ffn1_w_up[l]), bf(ffn1_w_down[l]),
                  row(final_norm), final_norm=False)

        q, kc, vc, ks, kw, vs, vw, gates, u, gm = _proj(xf, row(mix_norm[l]), _pack_w_in(w_in[l], d), batch, seq)
        pos2 = cmp_pos[l].reshape(2, (CMP_LEN // 2) * HEAD_DIM)
        k_cmp, v_cmp = _compress(_chunked(kc, batch, seq), _chunked(vc, batch, seq), pos2,
                                 bf(cmp_k_w1[l]), bf(cmp_k_w2[l]), bf(cmp_v_w1[l]), bf(cmp_v_w2[l]))
        o_nsa = _nsa(q, k_cmp, v_cmp, ks, vs, kw, vw, gates)

        w_bn = w_branch_nsa[l].reshape(N_GROUPS, HEADS_PER_GROUP, HEAD_DIM, d).transpose(1, 0, 2, 3).reshape(-1, d)
        xf = _merge(xf, u, gm, o_nsa, bf(pool_w[l]), row(pool_scale[l]), bf(w_branch_pool[l]), bf(w_bn),
                    bf(w_out[l]), seq)

        xf = _ffn(xf, row(ffn2_norm[l]), bf(ffn2_w_gate[l]), bf(ffn2_w_up[l]), bf(ffn2_w_down[l]),
                  row(final_norm), final_norm=(l == depth - 1))
    return xf.reshape(batch, seq, d)
```

```python
import functools

import jax
import jax.numpy as jnp
import numpy as np
from jax import lax
from jax.experimental import pallas as pl
from jax.experimental.pallas import tpu as pltpu

D_FF = 2816
N_HEADS = 16
N_GROUPS = 2
HEADS_PER_GROUP = N_HEADS // N_GROUPS
HEAD_DIM = 64
CMP_LEN = 32
CMP_STRIDE = 16
SEL_LEN = 64
N_SELECT = 16
WINDOW = 512
POOL_WINDOWS = (2, 4, 8, 16)
POOL_GROUP_DIM = 128
RMS_EPS = 1e-6
ALIBI_MAX_BIAS = 8.0

LANES = 128
NEG_BIG = -1e30
TOPK_BIG = 1e30
VMEM_LIMIT = 56 * 1024 * 1024

BF16 = jnp.bfloat16
F32 = jnp.float32


def _dot(a, b):
    return jnp.dot(a, b, preferred_element_type=F32)


def _dot_nt(a, b):
    return lax.dot_general(a, b, (((1,), (1,)), ((), ())), preferred_element_type=F32)


def _rms(x, g):
    return x * lax.rsqrt(jnp.mean(x * x, axis=-1, keepdims=True) + RMS_EPS) * g


def _params(*sem):
    return pltpu.CompilerParams(dimension_semantics=sem, vmem_limit_bytes=VMEM_LIMIT)


def _ffn_kernel(x_ref, g_ref, wg_ref, wu_ref, wd_ref, fin_ref, o_ref, *, final_norm):
    x = x_ref[...]
    xn = _rms(x, g_ref[...]).astype(BF16)
    gate = _dot(xn, wg_ref[...])
    up = _dot(xn, wu_ref[...])
    act = (gate * jax.nn.sigmoid(gate)) * up
    y = x + 0.5 * _dot(act.astype(BF16), wd_ref[...])
    if final_norm:
        y = _rms(y, fin_ref[...])
    o_ref[...] = y


def _ffn(x, norm_g, wg, wu, wd, fin_g, *, final_norm, tm=512):
    n, d = x.shape
    row = lambda i: (i, 0)
    resident = lambda a: pl.BlockSpec(a.shape, lambda i: (0, 0), pipeline_mode=pl.Buffered(1))
    return pl.pallas_call(
        functools.partial(_ffn_kernel, final_norm=final_norm),
        out_shape=jax.ShapeDtypeStruct((n, d), F32),
        grid=(n // tm,),
        in_specs=[pl.BlockSpec((tm, d), row), resident(norm_g), resident(wg), resident(wu), resident(wd),
                  resident(fin_g)],
        out_specs=pl.BlockSpec((tm, d), row),
        compiler_params=_params("parallel"),
        name="ffn",
    )(x, norm_g, wg, wu, wd, fin_g)


LOG2E = 1.4426950408889634
N_FEAT = 6


def _bf16_pieces(x):
    x = np.asarray(x, np.float32)
    s1 = x.astype(BF16).astype(np.float32)
    s2 = (x - s1).astype(BF16).astype(np.float32)
    s3 = (x - s1 - s2).astype(BF16).astype(np.float32)
    return s1, s2, s3


def _query_features():
    slopes = np.float32(2.0) ** (-ALIBI_MAX_BIAS * np.arange(1, N_HEADS + 1, dtype=np.float32) / N_HEADS)
    s1, s2, s3 = _bf16_pieces(slopes * np.float32(LOG2E))
    feat = np.zeros((N_HEADS, LANES), np.float32)
    feat[:, HEAD_DIM:HEAD_DIM + N_FEAT] = np.stack([s1, s2, s3, SEL_LEN * s1, SEL_LEN * s2, SEL_LEN * s3], axis=1)
    return jnp.asarray(feat)


def _key_features(pos, width, offset):
    pos = np.asarray(pos)
    a, b = (pos // SEL_LEN).astype(np.float32), (pos % SEL_LEN).astype(np.float32)
    feat = np.zeros((len(pos), width), np.float32)
    feat[:, offset:offset + N_FEAT] = np.stack([b, b, b, a, a, a], axis=1)
    return jnp.asarray(feat)


Q_COLS = N_HEADS * LANES
CMP_OFF = Q_COLS
KEY_OFF = CMP_OFF + 2 * LANES
VAL_OFF = KEY_OFF + 2 * N_GROUPS * LANES
GATE_OFF = VAL_OFF + 2 * LANES
POOL_OFF = GATE_OFF + LANES
POOL_COLS = len(POOL_WINDOWS) * POOL_GROUP_DIM
MERGE_OFF = POOL_OFF + POOL_COLS


V_ROWS = HEAD_DIM + 16


def _store_values_t(v_ref, v):
    vt = v.T
    for g in range(N_GROUPS):
        v_ref[0, g, 0:HEAD_DIM, :] = vt[g * HEAD_DIM:(g + 1) * HEAD_DIM, :].astype(BF16)
        v_ref[0, g, HEAD_DIM:V_ROWS, :] = jnp.ones((V_ROWS - HEAD_DIM, v.shape[0]), BF16)


def _proj_kernel(x_ref, g_ref, w_ref, qf_ref, kf_ref, q_ref, kc_ref, vc_ref, ks_ref, kw_ref, vs_ref, vw_ref,
                 gate_ref, u_ref, gm_ref):
    d = x_ref.shape[1]
    hn = _rms(x_ref[...], g_ref[...]).astype(BF16)
    q_scale = HEAD_DIM ** -0.5 * LOG2E
    for h in range(0, N_HEADS, 2):
        q2 = _dot(hn, w_ref[:, h * LANES:(h + 2) * LANES]) * q_scale
        for i in range(2):
            q_ref[0, h + i] = (q2[:, i * LANES:(i + 1) * LANES] + qf_ref[h + i:h + i + 1, :]).astype(BF16)
    cmp_in = _dot(hn, w_ref[:, CMP_OFF:CMP_OFF + 2 * LANES])
    kc_ref[...] = cmp_in[:, 0:LANES]
    vc_ref[...] = cmp_in[:, LANES:2 * LANES]
    keys = _dot(hn, w_ref[:, KEY_OFF:KEY_OFF + 2 * N_GROUPS * LANES])
    kf = kf_ref[...]
    for g in range(N_GROUPS):
        ks_ref[0, g] = (keys[:, g * LANES:(g + 1) * LANES] + kf).astype(BF16)
        kw_ref[0, g] = (keys[:, (N_GROUPS + g) * LANES:(N_GROUPS + g + 1) * LANES] + kf).astype(BF16)
    vals = _dot(hn, w_ref[:, VAL_OFF:VAL_OFF + 2 * LANES])
    for i, v_ref in enumerate((vs_ref, vw_ref)):
        _store_values_t(v_ref, vals[:, i * LANES:(i + 1) * LANES])
    gate_ref[...] = jax.nn.sigmoid(_dot(hn, w_ref[:, GATE_OFF:GATE_OFF + LANES]))
    u_ref[...] = _dot(hn, w_ref[:, POOL_OFF:POOL_OFF + POOL_COLS])
    gm_ref[...] = jax.nn.sigmoid(_dot(hn, w_ref[:, MERGE_OFF:MERGE_OFF + 2 * d]))


def _pack_w_in(w_in, d):
    qw = N_HEADS * HEAD_DIM
    kvw = N_GROUPS * HEAD_DIM

    def padded(w, n):
        w = w.reshape(d, n, HEAD_DIM)
        return jnp.concatenate([w, jnp.zeros_like(w)], axis=-1).reshape(d, n * LANES)

    kc, vc, ks, vs, kw, vw = (w_in[:, qw + i * kvw:qw + (i + 1) * kvw] for i in range(6))
    off = qw + 6 * kvw
    n_g = 3 * N_HEADS
    w_g = jnp.pad(w_in[:, off:off + n_g], ((0, 0), (0, LANES - n_g)))
    w_rest = w_in[:, off + n_g:]
    return jnp.concatenate([padded(w_in[:, :qw], N_HEADS), kc, vc, padded(ks, N_GROUPS), padded(kw, N_GROUPS),
                            vs, vw, w_g, w_rest], axis=1).astype(BF16)


def _proj(x, norm_g, w_packed, batch, seq, *, tm=512):
    n, d = x.shape
    tiles_per_seq = seq // tm
    row = lambda i: (i, 0)
    const = lambda i: (0, 0)
    per_group = lambda i: (i // tiles_per_seq, 0, i % tiles_per_seq, 0)
    qf = _query_features()
    kf = _key_features(np.arange(seq), LANES, HEAD_DIM)
    flat = lambda width, dtype: (jax.ShapeDtypeStruct((n, width), dtype), pl.BlockSpec((tm, width), row))
    grouped = lambda count: (jax.ShapeDtypeStruct((batch, count, seq, LANES), BF16),
                             pl.BlockSpec((1, count, tm, LANES), per_group))
    values_t = (jax.ShapeDtypeStruct((batch, N_GROUPS, V_ROWS, seq), BF16),
                pl.BlockSpec((1, N_GROUPS, V_ROWS, tm), lambda i: (i // tiles_per_seq, 0, 0, i % tiles_per_seq)))
    outs = [grouped(N_HEADS), flat(LANES, F32), flat(LANES, F32), grouped(N_GROUPS), grouped(N_GROUPS),
            values_t, values_t, flat(LANES, F32), flat(POOL_COLS, F32), flat(2 * d, F32)]
    return pl.pallas_call(
        _proj_kernel,
        out_shape=[o[0] for o in outs],
        grid=(n // tm,),
        in_specs=[pl.BlockSpec((tm, d), row), pl.BlockSpec((1, d), const),
                  pl.BlockSpec(w_packed.shape, const, pipeline_mode=pl.Buffered(1)),
                  pl.BlockSpec(qf.shape, const), pl.BlockSpec((tm, LANES), lambda i: (i % tiles_per_seq, 0))],
        out_specs=[o[1] for o in outs],
        compiler_params=_params("parallel"),
        name="proj",
    )(x, norm_g, w_packed, qf, kf)


def _compress_kernel(k_ref, v_ref, pos_ref, wk1_ref, wk2_ref, wv1_ref, wv2_ref, feat_ref, ko_ref, vo_ref):
    n_chunk = k_ref.shape[0] // CMP_STRIDE
    hidden = wk2_ref.shape[0]

    def compress(x_ref, w1_ref, w2_ref):
        first = second = None
        for r in range(CMP_STRIDE):
            rows = x_ref[pl.ds(r, n_chunk, stride=CMP_STRIDE), :]
            a = _dot((rows + pos_ref[r:r + 1, :]).astype(BF16), w1_ref[r])
            b = _dot((rows + pos_ref[CMP_STRIDE + r:CMP_STRIDE + r + 1, :]).astype(BF16), w1_ref[CMP_STRIDE + r])
            first = a if first is None else first + a
            second = b if second is None else second + b
        act = jax.nn.gelu(first + pltpu.roll(second, n_chunk - 1, 0)).astype(BF16)
        return [_dot(act[:, g * hidden:(g + 1) * hidden], w2_ref[...]) for g in range(N_GROUPS)]

    for g, k_cmp in enumerate(compress(k_ref, wk1_ref, wk2_ref)):
        ko_ref[0, g] = jnp.concatenate([k_cmp, feat_ref[...]], axis=-1).astype(BF16)
    _store_values_t(vo_ref, jnp.concatenate(compress(v_ref, wv1_ref, wv2_ref), axis=-1))


def _block_diag_w1(w1):
    w = w1.reshape(CMP_LEN, HEAD_DIM, -1)
    z = jnp.zeros_like(w)
    return jnp.concatenate([jnp.concatenate([w, z], axis=2), jnp.concatenate([z, w], axis=2)], axis=1).astype(BF16)


def _compress(kc, vc, pos, wk1, wk2, wv1, wv2, batch, seq):
    n_chunk = seq // CMP_STRIDE
    rows = pl.BlockSpec((seq, LANES), lambda b: (b, 0))
    full = lambda a: pl.BlockSpec(a.shape, lambda b: (0,) * a.ndim, pipeline_mode=pl.Buffered(1))
    pos2 = jnp.concatenate([pos] * N_GROUPS, axis=1)
    feat = _key_features(np.arange(n_chunk) * CMP_STRIDE + CMP_LEN - 1, HEAD_DIM, 0)
    args = (pos2, _block_diag_w1(wk1), wk2.astype(BF16), _block_diag_w1(wv1), wv2.astype(BF16), feat)
    return pl.pallas_call(
        _compress_kernel,
        out_shape=[jax.ShapeDtypeStruct((batch, N_GROUPS, n_chunk, LANES), BF16),
                   jax.ShapeDtypeStruct((batch, N_GROUPS, V_ROWS, n_chunk), BF16)],
        grid=(batch,),
        in_specs=[rows, rows] + [full(a) for a in args],
        out_specs=[pl.BlockSpec((1, N_GROUPS, n_chunk, LANES), lambda b: (b, 0, 0, 0)),
                   pl.BlockSpec((1, N_GROUPS, V_ROWS, n_chunk), lambda b: (b, 0, 0, 0))],
        compiler_params=_params("parallel"),
        name="compress",
    )(kc, vc, *args)


SLC_CHUNK = 256
BLOCKS_PER_CHUNK = SLC_CHUNK // SEL_LEN


def _nsa_kernel(q_ref, kc_ref, vct_ref, ks_ref, vst_ref, kw_ref, vwt_ref, gate_ref, ovt_ref, o_ref,
                st_ref, p_ref, gt_ref, selt_ref, list_ref, m_ref, acc_ref, out_ref, *, tq, seq):
    n_sel = seq // SEL_LEN
    n_chunks = seq // SLC_CHUNK
    n_cmp = kc_ref.shape[2]
    n_win = WINDOW + tq
    q0 = pl.program_id(1) * tq
    gt_ref[...] = gate_ref[...].T
    lanes = lambda head: slice(head * tq, (head + 1) * tq)
    pairs = [(g, g * HEADS_PER_GROUP + 2 * j) for g in range(N_GROUPS) for j in range(HEADS_PER_GROUP // 2)]

    def gate_row(head, branch):
        c = 3 * head + branch
        return gt_ref[c:c + 1, :]

    def out_rows(head):
        g, hh = divmod(head, HEADS_PER_GROUP)
        return hh, slice(g * HEAD_DIM, (g + 1) * HEAD_DIM)

    def stage_scores(keys, n_keys):
        for g, head0 in pairs:
            q_pair = q_ref[0, head0:head0 + 2].reshape(2 * tq, LANES)
            st_ref[0:n_keys, head0 * tq:(head0 + 2) * tq] = _dot_nt(keys(g), q_pair)

    def finish_pair(head0, pv, branch, first, ok=None):
        invs = []
        for i in range(2):
            cols = slice(i * tq, (i + 1) * tq)
            inv = 1.0 / pv[HEAD_DIM:HEAD_DIM + 1, cols]
            if ok is not None:
                inv = jnp.where(ok, inv, 0.0)
            hh, rows = out_rows(head0 + i)
            contrib = pv[0:HEAD_DIM, cols] * (inv * gate_row(head0 + i, branch))
            out_ref[hh, rows, :] = contrib if first else out_ref[hh, rows, :] + contrib
            invs.append(inv)
        return invs

    t_c = q0 + lax.broadcasted_iota(jnp.int32, (n_cmp, tq), 1)
    end_c = lax.broadcasted_iota(jnp.int32, (n_cmp, tq), 0) * CMP_STRIDE + (CMP_LEN - 1)
    bias_c = jnp.where(t_c >= end_c, 0.0, NEG_BIG)
    has_cmp = (q0 + lax.broadcasted_iota(jnp.int32, (1, tq), 1)) >= CMP_LEN - 1
    inv_c = [None] * N_HEADS

    def cmp_pair(g, head0):
        for head in (head0, head0 + 1):
            x = st_ref[0:n_cmp, lanes(head)] + bias_c
            p_ref[0:n_cmp, lanes(head)] = jnp.exp2(x - jnp.max(x, axis=0, keepdims=True)).astype(BF16)
        pv = _dot(vct_ref[0, g], p_ref[0:n_cmp, head0 * tq:(head0 + 2) * tq])
        inv_c[head0], inv_c[head0 + 1] = finish_pair(head0, pv, 0, True, has_cmp)

    stage_scores(lambda g: kc_ref[0, g], n_cmp)
    for g, head0 in pairs:
        cmp_pair(g, head0)

    raw = _dot(ovt_ref[...], p_ref[0:n_cmp, :])

    w0 = pl.multiple_of(jnp.maximum(q0 - WINDOW, 0), tq)
    stage_scores(lambda g: kw_ref[0, g, pl.ds(w0, n_win), :], n_win)

    blk = lax.broadcasted_iota(jnp.int32, (n_sel, tq), 0)
    cur = (q0 + lax.broadcasted_iota(jnp.int32, (n_sel, tq), 1)) // SEL_LEN
    valid = blk <= cur
    forced = valid & ((blk == 0) | (blk == cur) | (blk == cur - 1))
    blk_f = blk.astype(F32)
    score, sel = [], []
    for g in range(N_GROUPS):
        total = None
        for head in range(g * HEADS_PER_GROUP, (g + 1) * HEADS_PER_GROUP):
            part = raw[:, lanes(head)] * inv_c[head]
            total = part if total is None else total + part
        score.append(jnp.where(forced, TOPK_BIG, jnp.where(valid, total, -TOPK_BIG)))
        sel.append(jnp.zeros((n_sel, tq), F32))

    for _ in range(min(N_SELECT, n_sel)):
        for g in range(N_GROUPS):
            best = jnp.max(score[g], axis=0, keepdims=True)
            pick = jnp.min(jnp.where(score[g] == best, blk_f, float(n_sel)), axis=0, keepdims=True)
            hit = blk_f == pick
            sel[g] = jnp.where(hit, 1.0, sel[g])
            score[g] = jnp.where(hit, -TOPK_BIG, score[g])
    for g in range(N_GROUPS):
        selt_ref[g] = sel[g]
    sel_any = sel[0] + sel[1]
    n_causal = (q0 + tq + SLC_CHUNK - 1) // SLC_CHUNK
    n_items = jnp.int32(0)
    for c in range(n_chunks):
        any_sel = jnp.max(sel_any[c * BLOCKS_PER_CHUNK:(c + 1) * BLOCKS_PER_CHUNK, :])
        list_ref[n_items] = c
        n_items = n_items + ((any_sel > 0.5) & (c < n_causal)).astype(jnp.int32)

    dist_w = (q0 + lax.broadcasted_iota(jnp.int32, (n_win, tq), 1)) - (w0 + lax.broadcasted_iota(jnp.int32, (n_win, tq), 0))
    bias_w = jnp.where((dist_w >= 0) & (dist_w < WINDOW), 0.0, NEG_BIG)

    def win_pair(g, head0):
        for head in (head0, head0 + 1):
            x = st_ref[0:n_win, lanes(head)] + bias_w
            p_ref[0:n_win, lanes(head)] = jnp.exp2(x - jnp.max(x, axis=0, keepdims=True)).astype(BF16)
        pv = _dot(vwt_ref[0, g, :, pl.ds(w0, n_win)], p_ref[0:n_win, head0 * tq:(head0 + 2) * tq])
        finish_pair(head0, pv, 2, False)

    for g, head0 in pairs:
        win_pair(g, head0)

    m_ref[...] = jnp.full_like(m_ref, NEG_BIG)
    acc_ref[...] = jnp.zeros_like(acc_ref)
    base = (lax.broadcasted_iota(jnp.int32, (SLC_CHUNK, tq), 1)
            - lax.broadcasted_iota(jnp.int32, (SLC_CHUNK, tq), 0))

    def stage_chunk(item, row0):
        c = list_ref[item]
        k0 = pl.multiple_of(c * SLC_CHUNK, SLC_CHUNK)
        causal = base >= k0 - q0
        for g, head0 in pairs:
            if head0 % HEADS_PER_GROUP == 0:
                sel_keys = jnp.concatenate(
                    [jnp.broadcast_to(selt_ref[g, pl.ds(c * BLOCKS_PER_CHUNK + i, 1), :], (SEL_LEN, tq))
                     for i in range(BLOCKS_PER_CHUNK)], axis=0)
                bias = jnp.where(causal & (sel_keys > 0.5), 0.0, NEG_BIG)
                bias2 = jnp.concatenate([bias, bias], axis=1)
            q_pair = q_ref[0, head0:head0 + 2].reshape(2 * tq, LANES)
            st_ref[row0:row0 + SLC_CHUNK, head0 * tq:(head0 + 2) * tq] = (
                _dot_nt(ks_ref[0, g, pl.ds(k0, SLC_CHUNK), :], q_pair) + bias2)

    def consume_chunk(item, row0):
        k0 = pl.multiple_of(list_ref[item] * SLC_CHUNK, SLC_CHUNK)
        for g, head0 in pairs:
            both = slice(head0 * tq, (head0 + 2) * tq)
            m_chunk = jnp.max(st_ref[row0:row0 + SLC_CHUNK, both], axis=0, keepdims=True)
            alphas = []
            for i, head in enumerate((head0, head0 + 1)):
                m_old = m_ref[head:head + 1, :]
                m_new = jnp.maximum(m_old, m_chunk[:, i * tq:(i + 1) * tq])
                alphas.append(jnp.exp2(m_old - m_new))
                m_ref[head:head + 1, :] = m_new
                p_ref[0:SLC_CHUNK, lanes(head)] = jnp.exp2(st_ref[row0:row0 + SLC_CHUNK, lanes(head)] - m_new).astype(BF16)
            pv = _dot(vst_ref[0, g, :, pl.ds(k0, SLC_CHUNK)], p_ref[0:SLC_CHUNK, both])
            acc_ref[:, both] = acc_ref[:, both] * jnp.concatenate(alphas, axis=1) + pv

    last = n_items - 1
    stage_chunk(0, 0)

    def two_chunks(j, carry):
        stage_chunk(jnp.minimum(2 * j + 1, last), SLC_CHUNK)
        consume_chunk(2 * j, 0)

        @pl.when(2 * j + 1 < n_items)
        def _():
            stage_chunk(jnp.minimum(2 * j + 2, last), 0)
            consume_chunk(2 * j + 1, SLC_CHUNK)
        return carry

    lax.fori_loop(0, (n_items + 1) // 2, two_chunks, 0)
    for g, head0 in pairs:
        finish_pair(head0, acc_ref[:, head0 * tq:(head0 + 2) * tq], 1, False)

    for hh in range(HEADS_PER_GROUP):
        o_ref[:, hh * LANES:(hh + 1) * LANES] = out_ref[hh].T.astype(BF16)


def _overlap_t(seq):
    n_cmp = (seq - CMP_LEN) // CMP_STRIDE + 1
    n_sel = seq // SEL_LEN
    cs = np.arange(n_cmp)[:, None] * CMP_STRIDE
    ss = np.arange(n_sel)[None, :] * SEL_LEN
    ov = np.clip(np.minimum(cs + CMP_LEN, ss + SEL_LEN) - np.maximum(cs, ss), 0, None) / CMP_LEN
    out = np.zeros((n_sel, seq // CMP_STRIDE), np.float32)
    out[:, :n_cmp] = ov.T
    return jnp.asarray(out, dtype=BF16)


def _nsa(q, kc, vct, ks, vst, kw, vwt, gates, *, tq=128):
    batch, _, seq, _ = q.shape
    n_cmp = kc.shape[2]
    n_sel = seq // SEL_LEN
    assert seq % SLC_CHUNK == 0 and seq >= WINDOW + tq and SLC_CHUNK % tq == 0 and tq == LANES
    assert n_sel <= 2 * SEL_LEN
    ovt = _overlap_t(seq)
    keys = lambda n: pl.BlockSpec((1, N_GROUPS, n, LANES), lambda b, i: (b, 0, 0, 0))
    vals = lambda n: pl.BlockSpec((1, N_GROUPS, V_ROWS, n), lambda b, i: (b, 0, 0, 0))
    tiles = seq // tq
    s_rows = max(n_cmp, WINDOW + tq, SLC_CHUNK)
    return pl.pallas_call(
        functools.partial(_nsa_kernel, tq=tq, seq=seq),
        out_shape=jax.ShapeDtypeStruct((batch * seq, HEADS_PER_GROUP * LANES), BF16),
        grid=(batch, tiles),
        in_specs=[
            pl.BlockSpec((1, N_HEADS, tq, LANES), lambda b, i: (b, 0, i, 0)),
            keys(n_cmp), vals(n_cmp), keys(seq), vals(seq), keys(seq), vals(seq),
            pl.BlockSpec((tq, LANES), lambda b, i: (b * tiles + i, 0)),
            pl.BlockSpec(ovt.shape, lambda b, i: (0, 0)),
        ],
        out_specs=pl.BlockSpec((tq, HEADS_PER_GROUP * LANES), lambda b, i: (b * tiles + i, 0)),
        scratch_shapes=[
            pltpu.VMEM((s_rows, N_HEADS * tq), F32),
            pltpu.VMEM((s_rows, N_HEADS * tq), BF16),
            pltpu.VMEM((LANES, tq), F32),
            pltpu.VMEM((N_GROUPS, n_sel, tq), F32),
            pltpu.SMEM((seq // SLC_CHUNK,), jnp.int32),
            pltpu.VMEM((N_HEADS, tq), F32),
            pltpu.VMEM((V_ROWS, N_HEADS * tq), F32),
            pltpu.VMEM((HEADS_PER_GROUP, N_GROUPS * HEAD_DIM, tq), F32),
        ],
        compiler_params=_params("parallel", "arbitrary"),
        name="nsa",
    )(q, kc, vct, ks, vst, kw, vwt, gates, ovt)


HALO = max(POOL_WINDOWS)


def _merge_kernel(x_ref, u_ref, halo_ref, gm_ref, on_ref, pw_ref, ps_ref, wbp_ref, wbn_ref, wo_ref,
                  o_ref, ext_ref, *, tm, seq):
    d = x_ref.shape[1]
    pos0 = (pl.program_id(0) * tm) % seq
    ext_ref[0:HALO, :] = jnp.where(pos0 == 0, 0.0, halo_ref[...])
    ext_ref[HALO:HALO + tm, :] = u_ref[...]
    pos = (pos0 + lax.broadcasted_iota(jnp.int32, (tm, POOL_GROUP_DIM), 0)).astype(F32)
    mixed = []
    for gi, w in enumerate(POOL_WINDOWS):
        cols = slice(gi * POOL_GROUP_DIM, (gi + 1) * POOL_GROUP_DIM)
        u = ext_ref[HALO:HALO + tm, cols]
        total = u
        for lag in range(1, w):
            total = total + ext_ref[HALO - lag:HALO - lag + tm, cols]
        delta = total / jnp.minimum(pos + 1.0, float(w)) - u
        mixed.append(_dot(delta.astype(BF16), pw_ref[gi]) * ps_ref[:, cols])
    mixed = jnp.concatenate(mixed, axis=-1).astype(BF16)
    a = _dot(mixed, wbp_ref[...])
    b = _dot(on_ref[...], wbn_ref[...])
    merged = gm_ref[:, 0:d] * a + gm_ref[:, d:2 * d] * b
    o_ref[...] = x_ref[...] + _dot(merged.astype(BF16), wo_ref[...])


def _merge(x, u, gm, o_nsa, pool_w, pool_scale, w_bp, w_bn, w_out, seq, *, tm=512):
    n, d = x.shape
    row = lambda i: (i, 0)
    full = lambda a: pl.BlockSpec(a.shape, lambda i: (0,) * a.ndim, pipeline_mode=pl.Buffered(1))
    halo_blocks = tm // HALO
    return pl.pallas_call(
        functools.partial(_merge_kernel, tm=tm, seq=seq),
        out_shape=jax.ShapeDtypeStruct((n, d), F32),
        grid=(n // tm,),
        in_specs=[
            pl.BlockSpec((tm, d), row),
            pl.BlockSpec((tm, POOL_COLS), row),
            pl.BlockSpec((HALO, POOL_COLS), lambda i: (jnp.maximum(i * halo_blocks - 1, 0), 0)),
            pl.BlockSpec((tm, 2 * d), row),
            pl.BlockSpec((tm, d), row),
            full(pool_w), full(pool_scale), full(w_bp), full(w_bn), full(w_out),
        ],
        out_specs=pl.BlockSpec((tm, d), row),
        scratch_shapes=[pltpu.VMEM((HALO + tm, POOL_COLS), F32)],
        compiler_params=_params("parallel"),
        name="merge",
    )(x, u, u, gm, o_nsa, pool_w, pool_scale, w_bp, w_bn, w_out)


def kernel(x, ffn1_norm, ffn1_w_gate, ffn1_w_up, ffn1_w_down, mix_norm, w_in, cmp_pos, cmp_k_w1, cmp_k_w2, cmp_v_w1, cmp_v_w2, pool_w, pool_scale, w_branch_pool, w_branch_nsa, w_out, ffn2_norm, ffn2_w_gate, ffn2_w_up, ffn2_w_down, final_norm):
    batch, seq, d = x.shape
    depth = w_in.shape[0]
    xf = x.reshape(batch * seq, d)
    bf = lambda a: a.astype(BF16)
    row = lambda a: a.reshape(1, -1)
    for l in range(depth):
        xf = _ffn(xf, row(ffn1_norm[l]), bf(ffn1_w_gate[l]), bf(ffn1_w_up[l]), bf(ffn1_w_down[l]),
                  row(final_norm), final_norm=False)

        q, kc, vc, ks, kw, vs, vw, gates, u, gm = _proj(xf, row(mix_norm[l]), _pack_w_in(w_in[l], d), batch, seq)
        k_cmp, v_cmp = _compress(kc, vc, cmp_pos[l], cmp_k_w1[l], cmp_k_w2[l], cmp_v_w1[l], cmp_v_w2[l], batch, seq)
        o_nsa = _nsa(q, k_cmp, v_cmp, ks, vs, kw, vw, gates)

        w_bn = w_branch_nsa[l].reshape(N_GROUPS, HEADS_PER_GROUP, HEAD_DIM, d).transpose(1, 0, 2, 3).reshape(-1, d)
        xf = _merge(xf, u, gm, o_nsa, bf(pool_w[l]), row(pool_scale[l]), bf(w_branch_pool[l]), bf(w_bn),
                    bf(w_out[l]), seq)

        xf = _ffn(xf, row(ffn2_norm[l]), bf(ffn2_w_gate[l]), bf(ffn2_w_up[l]), bf(ffn2_w_down[l]),
                  row(final_norm), final_norm=(l == depth - 1))
    return xf.reshape(batch, seq, d)
```

```python
import functools

import jax
import jax.numpy as jnp
import numpy as np
from jax import lax
from jax.experimental import pallas as pl
from jax.experimental.pallas import tpu as pltpu

D_FF = 2816
N_HEADS = 16
N_GROUPS = 2
HEADS_PER_GROUP = N_HEADS // N_GROUPS
HEAD_DIM = 64
CMP_LEN = 32
CMP_STRIDE = 16
SEL_LEN = 64
N_SELECT = 16
WINDOW = 512
POOL_WINDOWS = (2, 4, 8, 16)
POOL_GROUP_DIM = 128
RMS_EPS = 1e-6
ALIBI_MAX_BIAS = 8.0

LANES = 128
NEG_BIG = -1e30
TOPK_BIG = 1e30
VMEM_LIMIT = 56 * 1024 * 1024

BF16 = jnp.bfloat16
F32 = jnp.float32


def _dot(a, b):
    return jnp.dot(a, b, preferred_element_type=F32)


def _dot_nt(a, b):
    return lax.dot_general(a, b, (((1,), (1,)), ((), ())), preferred_element_type=F32)


def _rms(x, g):
    return x * lax.rsqrt(jnp.mean(x * x, axis=-1, keepdims=True) + RMS_EPS) * g


def _params(*sem):
    return pltpu.CompilerParams(dimension_semantics=sem, vmem_limit_bytes=VMEM_LIMIT)


def _ffn_kernel(x_ref, g_ref, wg_ref, wu_ref, wd_ref, fin_ref, o_ref, *, final_norm):
    x = x_ref[...]
    xn = _rms(x, g_ref[...]).astype(BF16)
    gate = _dot(xn, wg_ref[...])
    up = _dot(xn, wu_ref[...])
    act = (gate * jax.nn.sigmoid(gate)) * up
    y = x + 0.5 * _dot(act.astype(BF16), wd_ref[...])
    if final_norm:
        y = _rms(y, fin_ref[...])
    o_ref[...] = y


def _ffn(x, norm_g, wg, wu, wd, fin_g, *, final_norm, tm=512):
    n, d = x.shape
    row = lambda i: (i, 0)
    resident = lambda a: pl.BlockSpec(a.shape, lambda i: (0, 0), pipeline_mode=pl.Buffered(1))
    return pl.pallas_call(
        functools.partial(_ffn_kernel, final_norm=final_norm),
        out_shape=jax.ShapeDtypeStruct((n, d), F32),
        grid=(n // tm,),
        in_specs=[pl.BlockSpec((tm, d), row), resident(norm_g), resident(wg), resident(wu), resident(wd),
                  resident(fin_g)],
        out_specs=pl.BlockSpec((tm, d), row),
        compiler_params=_params("parallel"),
        name="ffn",
    )(x, norm_g, wg, wu, wd, fin_g)


LOG2E = 1.4426950408889634
N_FEAT = 6


def _bf16_pieces(x):
    x = np.asarray(x, np.float32)
    s1 = x.astype(BF16).astype(np.float32)
    s2 = (x - s1).astype(BF16).astype(np.float32)
    s3 = (x - s1 - s2).astype(BF16).astype(np.float32)
    return s1, s2, s3


def _query_features():
    slopes = np.float32(2.0) ** (-ALIBI_MAX_BIAS * np.arange(1, N_HEADS + 1, dtype=np.float32) / N_HEADS)
    s1, s2, s3 = _bf16_pieces(slopes * np.float32(LOG2E))
    feat = np.zeros((N_HEADS, LANES), np.float32)
    feat[:, HEAD_DIM:HEAD_DIM + N_FEAT] = np.stack([s1, s2, s3, SEL_LEN * s1, SEL_LEN * s2, SEL_LEN * s3], axis=1)
    return jnp.asarray(feat)


def _key_features(pos, width, offset):
    pos = np.asarray(pos)
    a, b = (pos // SEL_LEN).astype(np.float32), (pos % SEL_LEN).astype(np.float32)
    feat = np.zeros((len(pos), width), np.float32)
    feat[:, offset:offset + N_FEAT] = np.stack([b, b, b, a, a, a], axis=1)
    return jnp.asarray(feat)


Q_COLS = N_HEADS * LANES
CMP_OFF = Q_COLS
KEY_OFF = CMP_OFF + 2 * LANES
VAL_OFF = KEY_OFF + 2 * N_GROUPS * LANES
GATE_OFF = VAL_OFF + 2 * LANES
POOL_OFF = GATE_OFF + LANES
POOL_COLS = len(POOL_WINDOWS) * POOL_GROUP_DIM
MERGE_OFF = POOL_OFF + POOL_COLS


V_ROWS = HEAD_DIM + 16


def _store_values_t(v_ref, v):
    vt = v.T
    for g in range(N_GROUPS):
        v_ref[0, g, 0:HEAD_DIM, :] = vt[g * HEAD_DIM:(g + 1) * HEAD_DIM, :].astype(BF16)
        v_ref[0, g, HEAD_DIM:V_ROWS, :] = jnp.ones((V_ROWS - HEAD_DIM, v.shape[0]), BF16)


def _proj_kernel(x_ref, g_ref, w_ref, qf_ref, kf_ref, q_ref, kc_ref, vc_ref, ks_ref, kw_ref, vs_ref, vw_ref,
                 gate_ref, u_ref, gm_ref):
    d = x_ref.shape[1]
    hn = _rms(x_ref[...], g_ref[...]).astype(BF16)
    q_scale = HEAD_DIM ** -0.5 * LOG2E
    for h in range(0, N_HEADS, 2):
        q2 = _dot(hn, w_ref[:, h * LANES:(h + 2) * LANES]) * q_scale
        for i in range(2):
            q_ref[0, h + i] = (q2[:, i * LANES:(i + 1) * LANES] + qf_ref[h + i:h + i + 1, :]).astype(BF16)
    cmp_in = _dot(hn, w_ref[:, CMP_OFF:CMP_OFF + 2 * LANES])
    kc_ref[...] = cmp_in[:, 0:LANES]
    vc_ref[...] = cmp_in[:, LANES:2 * LANES]
    keys = _dot(hn, w_ref[:, KEY_OFF:KEY_OFF + 2 * N_GROUPS * LANES])
    kf = kf_ref[...]
    for g in range(N_GROUPS):
        ks_ref[0, g] = (keys[:, g * LANES:(g + 1) * LANES] + kf).astype(BF16)
        kw_ref[0, g] = (keys[:, (N_GROUPS + g) * LANES:(N_GROUPS + g + 1) * LANES] + kf).astype(BF16)
    vals = _dot(hn, w_ref[:, VAL_OFF:VAL_OFF + 2 * LANES])
    for i, v_ref in enumerate((vs_ref, vw_ref)):
        _store_values_t(v_ref, vals[:, i * LANES:(i + 1) * LANES])
    gate_ref[...] = jax.nn.sigmoid(_dot(hn, w_ref[:, GATE_OFF:GATE_OFF + LANES]))
    u_ref[...] = _dot(hn, w_ref[:, POOL_OFF:POOL_OFF + POOL_COLS])
    gm_ref[...] = jax.nn.sigmoid(_dot(hn, w_ref[:, MERGE_OFF:MERGE_OFF + 2 * d]))


def _pack_w_in(w_in, d):
    qw = N_HEADS * HEAD_DIM
    kvw = N_GROUPS * HEAD_DIM

    def padded(w, n):
        w = w.reshape(d, n, HEAD_DIM)
        return jnp.concatenate([w, jnp.zeros_like(w)], axis=-1).reshape(d, n * LANES)

    kc, vc, ks, vs, kw, vw = (w_in[:, qw + i * kvw:qw + (i + 1) * kvw] for i in range(6))
    off = qw + 6 * kvw
    n_g = 3 * N_HEADS
    w_g = jnp.pad(w_in[:, off:off + n_g], ((0, 0), (0, LANES - n_g)))
    w_rest = w_in[:, off + n_g:]
    return jnp.concatenate([padded(w_in[:, :qw], N_HEADS), kc, vc, padded(ks, N_GROUPS), padded(kw, N_GROUPS),
                            vs, vw, w_g, w_rest], axis=1).astype(BF16)


def _proj(x, norm_g, w_packed, batch, seq, *, tm=512):
    n, d = x.shape
    tiles_per_seq = seq // tm
    row = lambda i: (i, 0)
    const = lambda i: (0, 0)
    per_group = lambda i: (i // tiles_per_seq, 0, i % tiles_per_seq, 0)
    qf = _query_features()
    kf = _key_features(np.arange(seq), LANES, HEAD_DIM)
    flat = lambda width, dtype: (jax.ShapeDtypeStruct((n, width), dtype), pl.BlockSpec((tm, width), row))
    grouped = lambda count: (jax.ShapeDtypeStruct((batch, count, seq, LANES), BF16),
                             pl.BlockSpec((1, count, tm, LANES), per_group))
    values_t = (jax.ShapeDtypeStruct((batch, N_GROUPS, V_ROWS, seq), BF16),
                pl.BlockSpec((1, N_GROUPS, V_ROWS, tm), lambda i: (i // tiles_per_seq, 0, 0, i % tiles_per_seq)))
    outs = [grouped(N_HEADS), flat(LANES, F32), flat(LANES, F32), grouped(N_GROUPS), grouped(N_GROUPS),
            values_t, values_t, flat(LANES, F32), flat(POOL_COLS, F32), flat(2 * d, F32)]
    return pl.pallas_call(
        _proj_kernel,
        out_shape=[o[0] for o in outs],
        grid=(n // tm,),
        in_specs=[pl.BlockSpec((tm, d), row), pl.BlockSpec((1, d), const),
                  pl.BlockSpec(w_packed.shape, const, pipeline_mode=pl.Buffered(1)),
                  pl.BlockSpec(qf.shape, const), pl.BlockSpec((tm, LANES), lambda i: (i % tiles_per_seq, 0))],
        out_specs=[o[1] for o in outs],
        compiler_params=_params("parallel"),
        name="proj",
    )(x, norm_g, w_packed, qf, kf)


def _compress_kernel(k_ref, v_ref, pos_ref, wk1_ref, wk2_ref, wv1_ref, wv2_ref, feat_ref, ko_ref, vo_ref):
    n_chunk = k_ref.shape[0] // CMP_STRIDE
    hidden = wk2_ref.shape[0]

    def compress(x_ref, w1_ref, w2_ref):
        first = second = None
        for r in range(CMP_STRIDE):
            rows = x_ref[pl.ds(r, n_chunk, stride=CMP_STRIDE), :]
            a = _dot((rows + pos_ref[r:r + 1, :]).astype(BF16), w1_ref[r])
            b = _dot((rows + pos_ref[CMP_STRIDE + r:CMP_STRIDE + r + 1, :]).astype(BF16), w1_ref[CMP_STRIDE + r])
            first = a if first is None else first + a
            second = b if second is None else second + b
        act = jax.nn.gelu(first + pltpu.roll(second, n_chunk - 1, 0)).astype(BF16)
        return [_dot(act[:, g * hidden:(g + 1) * hidden], w2_ref[...]) for g in range(N_GROUPS)]

    for g, k_cmp in enumerate(compress(k_ref, wk1_ref, wk2_ref)):
        ko_ref[0, g] = jnp.concatenate([k_cmp, feat_ref[...]], axis=-1).astype(BF16)
    _store_values_t(vo_ref, jnp.concatenate(compress(v_ref, wv1_ref, wv2_ref), axis=-1))


def _block_diag_w1(w1):
    w = w1.reshape(CMP_LEN, HEAD_DIM, -1)
    z = jnp.zeros_like(w)
    return jnp.concatenate([jnp.concatenate([w, z], axis=2), jnp.concatenate([z, w], axis=2)], axis=1).astype(BF16)


def _compress(kc, vc, pos, wk1, wk2, wv1, wv2, batch, seq):
    n_chunk = seq // CMP_STRIDE
    rows = pl.BlockSpec((seq, LANES), lambda b: (b, 0))
    full = lambda a: pl.BlockSpec(a.shape, lambda b: (0,) * a.ndim, pipeline_mode=pl.Buffered(1))
    pos2 = jnp.concatenate([pos] * N_GROUPS, axis=1)
    feat = _key_features(np.arange(n_chunk) * CMP_STRIDE + CMP_LEN - 1, HEAD_DIM, 0)
    args = (pos2, _block_diag_w1(wk1), wk2.astype(BF16), _block_diag_w1(wv1), wv2.astype(BF16), feat)
    return pl.pallas_call(
        _compress_kernel,
        out_shape=[jax.ShapeDtypeStruct((batch, N_GROUPS, n_chunk, LANES), BF16),
                   jax.ShapeDtypeStruct((batch, N_GROUPS, V_ROWS, n_chunk), BF16)],
        grid=(batch,),
        in_specs=[rows, rows] + [full(a) for a in args],
        out_specs=[pl.BlockSpec((1, N_GROUPS, n_chunk, LANES), lambda b: (b, 0, 0, 0)),
                   pl.BlockSpec((1, N_GROUPS, V_ROWS, n_chunk), lambda b: (b, 0, 0, 0))],
        compiler_params=_params("parallel"),
        name="compress",
    )(kc, vc, *args)


SLC_CHUNK = 256
BLOCKS_PER_CHUNK = SLC_CHUNK // SEL_LEN


def _nsa_kernel(q_ref, kc_ref, vct_ref, ks_ref, vst_ref, kw_ref, vwt_ref, gate_ref, ovt_ref, o_ref,
                st_ref, p_ref, gt_ref, selt_ref, list_ref, m_ref, acc_ref, out_ref, *, tq, seq):
    n_sel = seq // SEL_LEN
    n_chunks = seq // SLC_CHUNK
    n_cmp = kc_ref.shape[2]
    n_win = WINDOW + tq
    q0 = pl.program_id(1) * tq
    gt_ref[...] = gate_ref[...].T
    pairs = [(g, g * HEADS_PER_GROUP + 2 * j) for g in range(N_GROUPS) for j in range(HEADS_PER_GROUP // 2)]

    def gate_row(head, branch):
        c = 3 * head + branch
        return gt_ref[c:c + 1, :]

    def out_rows(head):
        g, hh = divmod(head, HEADS_PER_GROUP)
        return hh, slice(g * HEAD_DIM, (g + 1) * HEAD_DIM)

    def stage_scores(keys, n_keys):
        for g, head0 in pairs:
            q_pair = q_ref[0, head0:head0 + 2].reshape(2 * tq, LANES)
            scores = _dot_nt(keys(g), q_pair)
            for i in range(2):
                st_ref[head0 + i, 0:n_keys, :] = scores[:, i * tq:(i + 1) * tq]

    def finish_pair(head0, pv, branch, first, ok=None):
        invs = []
        for i in range(2):
            cols = slice(i * tq, (i + 1) * tq)
            inv = 1.0 / pv[HEAD_DIM:HEAD_DIM + 1, cols]
            if ok is not None:
                inv = jnp.where(ok, inv, 0.0)
            hh, rows = out_rows(head0 + i)
            contrib = pv[0:HEAD_DIM, cols] * (inv * gate_row(head0 + i, branch))
            out_ref[hh, rows, :] = contrib if first else out_ref[hh, rows, :] + contrib
            invs.append(inv)
        return invs

    t_c = q0 + lax.broadcasted_iota(jnp.int32, (n_cmp, tq), 1)
    end_c = lax.broadcasted_iota(jnp.int32, (n_cmp, tq), 0) * CMP_STRIDE + (CMP_LEN - 1)
    bias_c = jnp.where(t_c >= end_c, 0.0, NEG_BIG)
    has_cmp = (q0 + lax.broadcasted_iota(jnp.int32, (1, tq), 1)) >= CMP_LEN - 1
    inv_c = [None] * N_HEADS

    def cmp_pair(g, head0):
        for i in range(2):
            x = st_ref[head0 + i, 0:n_cmp, :] + bias_c
            p_ref[head0 // 2, 0:n_cmp, i * tq:(i + 1) * tq] = jnp.exp2(x - jnp.max(x, axis=0, keepdims=True)).astype(BF16)
        pv = _dot(vct_ref[0, g], p_ref[head0 // 2, 0:n_cmp, :])
        inv_c[head0], inv_c[head0 + 1] = finish_pair(head0, pv, 0, True, has_cmp)

    stage_scores(lambda g: kc_ref[0, g], n_cmp)
    for g, head0 in pairs:
        cmp_pair(g, head0)

    raw = [_dot(ovt_ref[...], p_ref[pair, 0:n_cmp, :]) for pair in range(N_HEADS // 2)]

    w0 = pl.multiple_of(jnp.maximum(q0 - WINDOW, 0), tq)
    stage_scores(lambda g: kw_ref[0, g, pl.ds(w0, n_win), :], n_win)

    blk = lax.broadcasted_iota(jnp.int32, (n_sel, tq), 0)
    cur = (q0 + lax.broadcasted_iota(jnp.int32, (n_sel, tq), 1)) // SEL_LEN
    valid = blk <= cur
    forced = valid & ((blk == 0) | (blk == cur) | (blk == cur - 1))
    blk_f = blk.astype(F32)
    score, sel = [], []
    for g in range(N_GROUPS):
        total = None
        for head in range(g * HEADS_PER_GROUP, (g + 1) * HEADS_PER_GROUP):
            part = raw[head // 2][:, (head % 2) * tq:(head % 2 + 1) * tq] * inv_c[head]
            total = part if total is None else total + part
        score.append(jnp.where(forced, TOPK_BIG, jnp.where(valid, total, -TOPK_BIG)))
        sel.append(jnp.zeros((n_sel, tq), F32))

    for _ in range(min(N_SELECT, n_sel)):
        for g in range(N_GROUPS):
            best = jnp.max(score[g], axis=0, keepdims=True)
            pick = jnp.min(jnp.where(score[g] == best, blk_f, float(n_sel)), axis=0, keepdims=True)
            hit = blk_f == pick
            sel[g] = jnp.where(hit, 1.0, sel[g])
            score[g] = jnp.where(hit, -TOPK_BIG, score[g])
    for g in range(N_GROUPS):
        selt_ref[g] = sel[g]
    sel_any = sel[0] + sel[1]
    n_causal = (q0 + tq + SLC_CHUNK - 1) // SLC_CHUNK
    n_items = jnp.int32(0)
    for c in range(n_chunks):
        any_sel = jnp.max(sel_any[c * BLOCKS_PER_CHUNK:(c + 1) * BLOCKS_PER_CHUNK, :])
        list_ref[n_items] = c
        n_items = n_items + ((any_sel > 0.5) & (c < n_causal)).astype(jnp.int32)

    dist_w = (q0 + lax.broadcasted_iota(jnp.int32, (n_win, tq), 1)) - (w0 + lax.broadcasted_iota(jnp.int32, (n_win, tq), 0))
    bias_w = jnp.where((dist_w >= 0) & (dist_w < WINDOW), 0.0, NEG_BIG)

    def win_pair(g, head0):
        for i in range(2):
            x = st_ref[head0 + i, 0:n_win, :] + bias_w
            p_ref[head0 // 2, 0:n_win, i * tq:(i + 1) * tq] = jnp.exp2(x - jnp.max(x, axis=0, keepdims=True)).astype(BF16)
        pv = _dot(vwt_ref[0, g, :, pl.ds(w0, n_win)], p_ref[head0 // 2, 0:n_win, :])
        finish_pair(head0, pv, 2, False)

    for g, head0 in pairs:
        win_pair(g, head0)

    m_ref[...] = jnp.full_like(m_ref, NEG_BIG)
    acc_ref[...] = jnp.zeros_like(acc_ref)
    base = (lax.broadcasted_iota(jnp.int32, (SLC_CHUNK, tq), 1)
            - lax.broadcasted_iota(jnp.int32, (SLC_CHUNK, tq), 0))

    def stage_chunk(item, row0):
        c = list_ref[item]
        k0 = pl.multiple_of(c * SLC_CHUNK, SLC_CHUNK)
        causal = base >= k0 - q0
        for g, head0 in pairs:
            if head0 % HEADS_PER_GROUP == 0:
                sel_keys = jnp.concatenate(
                    [jnp.broadcast_to(selt_ref[g, pl.ds(c * BLOCKS_PER_CHUNK + i, 1), :], (SEL_LEN, tq))
                     for i in range(BLOCKS_PER_CHUNK)], axis=0)
                bias = jnp.where(causal & (sel_keys > 0.5), 0.0, NEG_BIG)
                bias2 = jnp.concatenate([bias, bias], axis=1)
            q_pair = q_ref[0, head0:head0 + 2].reshape(2 * tq, LANES)
            scores = _dot_nt(ks_ref[0, g, pl.ds(k0, SLC_CHUNK), :], q_pair) + bias2
            for i in range(2):
                st_ref[head0 + i, row0:row0 + SLC_CHUNK, :] = scores[:, i * tq:(i + 1) * tq]

    def consume_chunk(item, row0):
        k0 = pl.multiple_of(list_ref[item] * SLC_CHUNK, SLC_CHUNK)
        for g, head0 in pairs:
            alphas = []
            for i, head in enumerate((head0, head0 + 1)):
                x = st_ref[head, row0:row0 + SLC_CHUNK, :]
                m_old = m_ref[head:head + 1, :]
                m_new = jnp.maximum(m_old, jnp.max(x, axis=0, keepdims=True))
                alphas.append(jnp.exp2(m_old - m_new))
                m_ref[head:head + 1, :] = m_new
                p_ref[head0 // 2, 0:SLC_CHUNK, i * tq:(i + 1) * tq] = jnp.exp2(x - m_new).astype(BF16)
            pv = _dot(vst_ref[0, g, :, pl.ds(k0, SLC_CHUNK)], p_ref[head0 // 2, 0:SLC_CHUNK, :])
            acc_ref[head0 // 2] = acc_ref[head0 // 2] * jnp.concatenate(alphas, axis=1) + pv

    last = n_items - 1
    stage_chunk(0, 0)

    def two_chunks(j, carry):
        stage_chunk(jnp.minimum(2 * j + 1, last), SLC_CHUNK)
        consume_chunk(2 * j, 0)

        @pl.when(2 * j + 1 < n_items)
        def _():
            stage_chunk(jnp.minimum(2 * j + 2, last), 0)
            consume_chunk(2 * j + 1, SLC_CHUNK)
        return carry

    lax.fori_loop(0, (n_items + 1) // 2, two_chunks, 0)
    for g, head0 in pairs:
        finish_pair(head0, acc_ref[head0 // 2], 1, False)

    for hh in range(HEADS_PER_GROUP):
        o_ref[:, hh * LANES:(hh + 1) * LANES] = out_ref[hh].T.astype(BF16)


def _overlap_t(seq):
    n_cmp = (seq - CMP_LEN) // CMP_STRIDE + 1
    n_sel = seq // SEL_LEN
    cs = np.arange(n_cmp)[:, None] * CMP_STRIDE
    ss = np.arange(n_sel)[None, :] * SEL_LEN
    ov = np.clip(np.minimum(cs + CMP_LEN, ss + SEL_LEN) - np.maximum(cs, ss), 0, None) / CMP_LEN
    out = np.zeros((n_sel, seq // CMP_STRIDE), np.float32)
    out[:, :n_cmp] = ov.T
    return jnp.asarray(out, dtype=BF16)


def _nsa(q, kc, vct, ks, vst, kw, vwt, gates, *, tq=128):
    batch, _, seq, _ = q.shape
    n_cmp = kc.shape[2]
    n_sel = seq // SEL_LEN
    assert seq % SLC_CHUNK == 0 and seq >= WINDOW + tq and SLC_CHUNK % tq == 0 and tq == LANES
    assert n_sel <= 2 * SEL_LEN
    ovt = _overlap_t(seq)
    keys = lambda n: pl.BlockSpec((1, N_GROUPS, n, LANES), lambda b, i: (b, 0, 0, 0))
    vals = lambda n: pl.BlockSpec((1, N_GROUPS, V_ROWS, n), lambda b, i: (b, 0, 0, 0))
    tiles = seq // tq
    s_rows = max(n_cmp, WINDOW + tq, SLC_CHUNK)
    return pl.pallas_call(
        functools.partial(_nsa_kernel, tq=tq, seq=seq),
        out_shape=jax.ShapeDtypeStruct((batch * seq, HEADS_PER_GROUP * LANES), BF16),
        grid=(batch, tiles),
        in_specs=[
            pl.BlockSpec((1, N_HEADS, tq, LANES), lambda b, i: (b, 0, i, 0)),
            keys(n_cmp), vals(n_cmp), keys(seq), vals(seq), keys(seq), vals(seq),
            pl.BlockSpec((tq, LANES), lambda b, i: (b * tiles + i, 0)),
            pl.BlockSpec(ovt.shape, lambda b, i: (0, 0)),
        ],
        out_specs=pl.BlockSpec((tq, HEADS_PER_GROUP * LANES), lambda b, i: (b * tiles + i, 0)),
        scratch_shapes=[
            pltpu.VMEM((N_HEADS, s_rows, tq), F32),
            pltpu.VMEM((N_HEADS // 2, s_rows, 2 * tq), BF16),
            pltpu.VMEM((LANES, tq), F32),
            pltpu.VMEM((N_GROUPS, n_sel, tq), F32),
            pltpu.SMEM((seq // SLC_CHUNK,), jnp.int32),
            pltpu.VMEM((N_HEADS, tq), F32),
            pltpu.VMEM((N_HEADS // 2, V_ROWS, 2 * tq), F32),
            pltpu.VMEM((HEADS_PER_GROUP, N_GROUPS * HEAD_DIM, tq), F32),
        ],
        compiler_params=_params("parallel", "arbitrary"),
        name="nsa",
    )(q, kc, vct, ks, vst, kw, vwt, gates, ovt)


HALO = max(POOL_WINDOWS)


def _merge_kernel(x_ref, u_ref, halo_ref, gm_ref, on_ref, pw_ref, ps_ref, wbp_ref, wbn_ref, wo_ref,
                  o_ref, ext_ref, *, tm, seq):
    d = x_ref.shape[1]
    pos0 = (pl.program_id(0) * tm) % seq
    ext_ref[0:HALO, :] = jnp.where(pos0 == 0, 0.0, halo_ref[...])
    ext_ref[HALO:HALO + tm, :] = u_ref[...]
    pos = (pos0 + lax.broadcasted_iota(jnp.int32, (tm, POOL_GROUP_DIM), 0)).astype(F32)
    mixed = []
    for gi, w in enumerate(POOL_WINDOWS):
        cols = slice(gi * POOL_GROUP_DIM, (gi + 1) * POOL_GROUP_DIM)
        u = ext_ref[HALO:HALO + tm, cols]
        total = u
        for lag in range(1, w):
            total = total + ext_ref[HALO - lag:HALO - lag + tm, cols]
        delta = total / jnp.minimum(pos + 1.0, float(w)) - u
        mixed.append(_dot(delta.astype(BF16), pw_ref[gi]) * ps_ref[:, cols])
    mixed = jnp.concatenate(mixed, axis=-1).astype(BF16)
    a = _dot(mixed, wbp_ref[...])
    b = _dot(on_ref[...], wbn_ref[...])
    merged = gm_ref[:, 0:d] * a + gm_ref[:, d:2 * d] * b
    o_ref[...] = x_ref[...] + _dot(merged.astype(BF16), wo_ref[...])


def _merge(x, u, gm, o_nsa, pool_w, pool_scale, w_bp, w_bn, w_out, seq, *, tm=512):
    n, d = x.shape
    row = lambda i: (i, 0)
    full = lambda a: pl.BlockSpec(a.shape, lambda i: (0,) * a.ndim, pipeline_mode=pl.Buffered(1))
    halo_blocks = tm // HALO
    return pl.pallas_call(
        functools.partial(_merge_kernel, tm=tm, seq=seq),
        out_shape=jax.ShapeDtypeStruct((n, d), F32),
        grid=(n // tm,),
        in_specs=[
            pl.BlockSpec((tm, d), row),
            pl.BlockSpec((tm, POOL_COLS), row),
            pl.BlockSpec((HALO, POOL_COLS), lambda i: (jnp.maximum(i * halo_blocks - 1, 0), 0)),
            pl.BlockSpec((tm, 2 * d), row),
            pl.BlockSpec((tm, d), row),
            full(pool_w), full(pool_scale), full(w_bp), full(w_bn), full(w_out),
        ],
        out_specs=pl.BlockSpec((tm, d), row),
        scratch_shapes=[pltpu.VMEM((HALO + tm, POOL_COLS), F32)],
        compiler_params=_params("parallel"),
        name="merge",
    )(x, u, u, gm, o_nsa, pool_w, pool_scale, w_bp, w_bn, w_out)


def kernel(x, ffn1_norm, ffn1_w_gate, ffn1_w_up, ffn1_w_down, mix_norm, w_in, cmp_pos, cmp_k_w1, cmp_k_w2, cmp_v_w1, cmp_v_w2, pool_w, pool_scale, w_branch_pool, w_branch_nsa, w_out, ffn2_norm, ffn2_w_gate, ffn2_w_up, ffn2_w_down, final_norm):
    batch, seq, d = x.shape
    depth = w_in.shape[0]
    xf = x.reshape(batch * seq, d)
    bf = lambda a: a.astype(BF16)
    row = lambda a: a.reshape(1, -1)
    for l in range(depth):
        xf = _ffn(xf, row(ffn1_norm[l]), bf(ffn1_w_gate[l]), bf(ffn1_w_up[l]), bf(ffn1_w_down[l]),
                  row(final_norm), final_norm=False)

        q, kc, vc, ks, kw, vs, vw, gates, u, gm = _proj(xf, row(mix_norm[l]), _pack_w_in(w_in[l], d), batch, seq)
        k_cmp, v_cmp = _compress(kc, vc, cmp_pos[l], cmp_k_w1[l], cmp_k_w2[l], cmp_v_w1[l], cmp_v_w2[l], batch, seq)
        o_nsa = _nsa(q, k_cmp, v_cmp, ks, vs, kw, vw, gates)

        w_bn = w_branch_nsa[l].reshape(N_GROUPS, HEADS_PER_GROUP, HEAD_DIM, d).transpose(1, 0, 2, 3).reshape(-1, d)
        xf = _merge(xf, u, gm, o_nsa, bf(pool_w[l]), row(pool_scale[l]), bf(w_branch_pool[l]), bf(w_bn),
                    bf(w_out[l]), seq)

        xf = _ffn(xf, row(ffn2_norm[l]), bf(ffn2_w_gate[l]), bf(ffn2_w_up[l]), bf(ffn2_w_down[l]),
                  row(final_norm), final_norm=(l == depth - 1))
    return xf.reshape(batch, seq, d)
```

```python
import functools

import jax
import jax.numpy as jnp
import numpy as np
from jax import lax
from jax.experimental import pallas as pl
from jax.experimental.pallas import tpu as pltpu

N_HEADS = 16
N_GROUPS = 2
HEADS_PER_GROUP = N_HEADS // N_GROUPS
HEAD_DIM = 64
CMP_LEN = 32
CMP_STRIDE = 16
SEL_LEN = 64
N_SELECT = 16
WINDOW = 512
POOL_WINDOWS = (2, 4, 8, 16)
POOL_GROUP_DIM = 128
RMS_EPS = 1e-6
ALIBI_MAX_BIAS = 8.0

LANES = 128
NEG_BIG = -1e30
TOPK_BIG = 1e30
TOPK_TAKEN = -2e30
VMEM_LIMIT = 40 * 1024 * 1024

BF16 = jnp.bfloat16
F32 = jnp.float32


def _dot(a, b):
    return jnp.dot(a, b, preferred_element_type=F32)


def _dot_nt(a, b):
    return lax.dot_general(a, b, (((1,), (1,)), ((), ())), preferred_element_type=F32)


def _rms(x, g):
    return x * lax.rsqrt(jnp.mean(x * x, axis=-1, keepdims=True) + RMS_EPS) * g


def _params(*sem):
    return pltpu.CompilerParams(dimension_semantics=sem, vmem_limit_bytes=VMEM_LIMIT)


def _ffn_kernel(x_ref, g_ref, wg_ref, wu_ref, wd_ref, fin_ref, o_ref, *, final_norm):
    x = x_ref[...]
    xn = _rms(x, g_ref[...]).astype(BF16)
    gate = _dot(xn, wg_ref[...])
    up = _dot(xn, wu_ref[...])
    act = (gate * jax.nn.sigmoid(gate)) * up
    y = x + 0.5 * _dot(act.astype(BF16), wd_ref[...])
    if final_norm:
        y = _rms(y, fin_ref[...])
    o_ref[...] = y


def _ffn(x, norm_g, wg, wu, wd, fin_g, *, final_norm, tm=512):
    n, d = x.shape
    row = lambda i: (i, 0)
    resident = lambda a: pl.BlockSpec(a.shape, lambda i: (0, 0), pipeline_mode=pl.Buffered(1))
    return pl.pallas_call(
        functools.partial(_ffn_kernel, final_norm=final_norm),
        out_shape=jax.ShapeDtypeStruct((n, d), F32),
        grid=(n // tm,),
        in_specs=[pl.BlockSpec((tm, d), row), resident(norm_g), resident(wg), resident(wu), resident(wd),
                  resident(fin_g)],
        out_specs=pl.BlockSpec((tm, d), row),
        compiler_params=_params("parallel"),
        name="ffn",
    )(x, norm_g, wg, wu, wd, fin_g)


LOG2E = 1.4426950408889634
N_FEAT = 6


def _bf16_pieces(x):
    x = np.asarray(x, np.float32)
    s1 = x.astype(BF16).astype(np.float32)
    s2 = (x - s1).astype(BF16).astype(np.float32)
    s3 = (x - s1 - s2).astype(BF16).astype(np.float32)
    return s1, s2, s3


def _query_features():
    slopes = np.float32(2.0) ** (-ALIBI_MAX_BIAS * np.arange(1, N_HEADS + 1, dtype=np.float32) / N_HEADS)
    s1, s2, s3 = _bf16_pieces(slopes * np.float32(LOG2E))
    feat = np.zeros((N_HEADS, LANES), np.float32)
    feat[:, HEAD_DIM:HEAD_DIM + N_FEAT] = np.stack([s1, s2, s3, SEL_LEN * s1, SEL_LEN * s2, SEL_LEN * s3], axis=1)
    return jnp.asarray(feat)


def _key_features(pos, width, offset):
    pos = np.asarray(pos)
    a, b = (pos // SEL_LEN).astype(np.float32), (pos % SEL_LEN).astype(np.float32)
    feat = np.zeros((len(pos), width), np.float32)
    feat[:, offset:offset + N_FEAT] = np.stack([b, b, b, a, a, a], axis=1)
    return jnp.asarray(feat)


Q_COLS = N_HEADS * LANES
CMP_OFF = Q_COLS
KEY_OFF = CMP_OFF + 2 * LANES
VAL_OFF = KEY_OFF + 2 * N_GROUPS * LANES
GATE_OFF = VAL_OFF + 2 * LANES
POOL_OFF = GATE_OFF + LANES
POOL_COLS = len(POOL_WINDOWS) * POOL_GROUP_DIM
MERGE_OFF = POOL_OFF + POOL_COLS


V_ROWS = HEAD_DIM + 16


def _store_values_t(v_ref, v):
    vt = v.T
    for g in range(N_GROUPS):
        v_ref[0, g, 0:HEAD_DIM, :] = vt[g * HEAD_DIM:(g + 1) * HEAD_DIM, :].astype(BF16)
        v_ref[0, g, HEAD_DIM:V_ROWS, :] = jnp.ones((V_ROWS - HEAD_DIM, v.shape[0]), BF16)


def _proj_kernel(x_ref, g_ref, w_ref, qf_ref, kf_ref, q_ref, kc_ref, vc_ref, ks_ref, kw_ref, vs_ref, vw_ref,
                 gate_ref, u_ref, gm_ref):
    d = x_ref.shape[1]
    hn = _rms(x_ref[...], g_ref[...]).astype(BF16)
    q_scale = HEAD_DIM ** -0.5 * LOG2E
    for h in range(0, N_HEADS, 2):
        q2 = _dot(hn, w_ref[:, h * LANES:(h + 2) * LANES]) * q_scale
        for i in range(2):
            q_ref[0, h + i] = (q2[:, i * LANES:(i + 1) * LANES] + qf_ref[h + i:h + i + 1, :]).astype(BF16)
    cmp_in = _dot(hn, w_ref[:, CMP_OFF:CMP_OFF + 2 * LANES])
    kc_ref[...] = cmp_in[:, 0:LANES]
    vc_ref[...] = cmp_in[:, LANES:2 * LANES]
    keys = _dot(hn, w_ref[:, KEY_OFF:KEY_OFF + 2 * N_GROUPS * LANES])
    kf = kf_ref[...]
    for g in range(N_GROUPS):
        ks_ref[0, g] = (keys[:, g * LANES:(g + 1) * LANES] + kf).astype(BF16)
        kw_ref[0, g] = (keys[:, (N_GROUPS + g) * LANES:(N_GROUPS + g + 1) * LANES] + kf).astype(BF16)
    vals = _dot(hn, w_ref[:, VAL_OFF:VAL_OFF + 2 * LANES])
    for i, v_ref in enumerate((vs_ref, vw_ref)):
        _store_values_t(v_ref, vals[:, i * LANES:(i + 1) * LANES])
    gate_ref[...] = jax.nn.sigmoid(_dot(hn, w_ref[:, GATE_OFF:GATE_OFF + LANES]))
    u_ref[...] = _dot(hn, w_ref[:, POOL_OFF:POOL_OFF + POOL_COLS])
    gm_ref[...] = jax.nn.sigmoid(_dot(hn, w_ref[:, MERGE_OFF:MERGE_OFF + 2 * d]))


def _pack_w_in(w_in, d):
    w_in = w_in.astype(BF16)
    qw = N_HEADS * HEAD_DIM
    kvw = N_GROUPS * HEAD_DIM

    def padded(w, n):
        w = w.reshape(d, n, HEAD_DIM)
        return jnp.concatenate([w, jnp.zeros_like(w)], axis=-1).reshape(d, n * LANES)

    kc, vc, ks, vs, kw, vw = (w_in[:, qw + i * kvw:qw + (i + 1) * kvw] for i in range(6))
    off = qw + 6 * kvw
    n_g = 3 * N_HEADS
    w_g = jnp.pad(w_in[:, off:off + n_g], ((0, 0), (0, LANES - n_g)))
    w_rest = w_in[:, off + n_g:]
    return jnp.concatenate([padded(w_in[:, :qw], N_HEADS), kc, vc, padded(ks, N_GROUPS), padded(kw, N_GROUPS),
                            vs, vw, w_g, w_rest], axis=1)


def _proj(x, norm_g, w_packed, batch, seq, *, tm=512):
    n, d = x.shape
    tiles_per_seq = seq // tm
    row = lambda i: (i, 0)
    const = lambda i: (0, 0)
    per_group = lambda i: (i // tiles_per_seq, 0, i % tiles_per_seq, 0)
    qf = _query_features()
    kf = _key_features(np.arange(seq), LANES, HEAD_DIM)
    flat = lambda width, dtype: (jax.ShapeDtypeStruct((n, width), dtype), pl.BlockSpec((tm, width), row))
    grouped = lambda count: (jax.ShapeDtypeStruct((batch, count, seq, LANES), BF16),
                             pl.BlockSpec((1, count, tm, LANES), per_group))
    values_t = (jax.ShapeDtypeStruct((batch, N_GROUPS, V_ROWS, seq), BF16),
                pl.BlockSpec((1, N_GROUPS, V_ROWS, tm), lambda i: (i // tiles_per_seq, 0, 0, i % tiles_per_seq)))
    outs = [grouped(N_HEADS), flat(LANES, F32), flat(LANES, F32), grouped(N_GROUPS), grouped(N_GROUPS),
            values_t, values_t, flat(LANES, F32), flat(POOL_COLS, F32), flat(2 * d, F32)]
    return pl.pallas_call(
        _proj_kernel,
        out_shape=[o[0] for o in outs],
        grid=(n // tm,),
        in_specs=[pl.BlockSpec((tm, d), row), pl.BlockSpec((1, d), const),
                  pl.BlockSpec(w_packed.shape, const, pipeline_mode=pl.Buffered(1)),
                  pl.BlockSpec(qf.shape, const), pl.BlockSpec((tm, LANES), lambda i: (i % tiles_per_seq, 0))],
        out_specs=[o[1] for o in outs],
        compiler_params=_params("parallel"),
        name="proj",
    )(x, norm_g, w_packed, qf, kf)


def _compress_kernel(k_ref, v_ref, pos_ref, wk1_ref, wk2_ref, wv1_ref, wv2_ref, feat_ref, ko_ref, vo_ref):
    n_chunk = k_ref.shape[0] // CMP_STRIDE
    hidden = wk2_ref.shape[0]

    def compress(x_ref, w1_ref, w2_ref):
        first = second = None
        for r in range(CMP_STRIDE):
            rows = x_ref[pl.ds(r, n_chunk, stride=CMP_STRIDE), :]
            a = _dot((rows + pos_ref[r:r + 1, :]).astype(BF16), w1_ref[r])
            b = _dot((rows + pos_ref[CMP_STRIDE + r:CMP_STRIDE + r + 1, :]).astype(BF16), w1_ref[CMP_STRIDE + r])
            first = a if first is None else first + a
            second = b if second is None else second + b
        act = jax.nn.gelu(first + pltpu.roll(second, n_chunk - 1, 0)).astype(BF16)
        return [_dot(act[:, g * hidden:(g + 1) * hidden], w2_ref[...]) for g in range(N_GROUPS)]

    for g, k_cmp in enumerate(compress(k_ref, wk1_ref, wk2_ref)):
        ko_ref[0, g] = jnp.concatenate([k_cmp, feat_ref[...]], axis=-1).astype(BF16)
    _store_values_t(vo_ref, jnp.concatenate(compress(v_ref, wv1_ref, wv2_ref), axis=-1))


def _block_diag_w1(w1):
    w = w1.reshape(CMP_LEN, HEAD_DIM, -1)
    z = jnp.zeros_like(w)
    return jnp.concatenate([jnp.concatenate([w, z], axis=2), jnp.concatenate([z, w], axis=2)], axis=1).astype(BF16)


def _compress(kc, vc, pos, wk1, wk2, wv1, wv2, batch, seq):
    n_chunk = seq // CMP_STRIDE
    rows = pl.BlockSpec((seq, LANES), lambda b: (b, 0))
    full = lambda a: pl.BlockSpec(a.shape, lambda b: (0,) * a.ndim, pipeline_mode=pl.Buffered(1))
    pos2 = jnp.concatenate([pos] * N_GROUPS, axis=1)
    feat = _key_features(np.arange(n_chunk) * CMP_STRIDE + CMP_LEN - 1, HEAD_DIM, 0)
    args = (pos2, _block_diag_w1(wk1), wk2.astype(BF16), _block_diag_w1(wv1), wv2.astype(BF16), feat)
    return pl.pallas_call(
        _compress_kernel,
        out_shape=[jax.ShapeDtypeStruct((batch, N_GROUPS, n_chunk, LANES), BF16),
                   jax.ShapeDtypeStruct((batch, N_GROUPS, V_ROWS, n_chunk), BF16)],
        grid=(batch,),
        in_specs=[rows, rows] + [full(a) for a in args],
        out_specs=[pl.BlockSpec((1, N_GROUPS, n_chunk, LANES), lambda b: (b, 0, 0, 0)),
                   pl.BlockSpec((1, N_GROUPS, V_ROWS, n_chunk), lambda b: (b, 0, 0, 0))],
        compiler_params=_params("parallel"),
        name="compress",
    )(kc, vc, *args)


SLC_CHUNK = 256
BLOCKS_PER_CHUNK = SLC_CHUNK // SEL_LEN


def _nsa_kernel(q_ref, kc_ref, vct_ref, ks_ref, vst_ref, kw_ref, vwt_ref, gate_ref, ovt_ref, o_ref,
                st_ref, p_ref, gt_ref, selt_ref, list_ref, m_ref, acc_ref, out_ref, *, tq, seq):
    n_sel = seq // SEL_LEN
    n_chunks = seq // SLC_CHUNK
    n_cmp = kc_ref.shape[2]
    n_win = WINDOW + tq
    q0 = pl.program_id(1) * tq
    gt_ref[...] = gate_ref[...].T
    pairs = [(g, g * HEADS_PER_GROUP + 2 * j) for g in range(N_GROUPS) for j in range(HEADS_PER_GROUP // 2)]

    def gate_row(head, branch):
        c = 3 * head + branch
        return gt_ref[c:c + 1, :]

    def out_rows(head):
        g, hh = divmod(head, HEADS_PER_GROUP)
        return hh, slice(g * HEAD_DIM, (g + 1) * HEAD_DIM)

    def stage_scores(keys, n_keys):
        for g, head0 in pairs:
            q_pair = q_ref[0, head0:head0 + 2].reshape(2 * tq, LANES)
            scores = _dot_nt(keys(g), q_pair)
            for i in range(2):
                st_ref[head0 + i, 0:n_keys, :] = scores[:, i * tq:(i + 1) * tq]

    def finish_pair(head0, pv, branch, first, ok=None):
        invs = []
        for i in range(2):
            cols = slice(i * tq, (i + 1) * tq)
            inv = 1.0 / pv[HEAD_DIM:HEAD_DIM + 1, cols]
            if ok is not None:
                inv = jnp.where(ok, inv, 0.0)
            hh, rows = out_rows(head0 + i)
            contrib = pv[0:HEAD_DIM, cols] * (inv * gate_row(head0 + i, branch))
            out_ref[hh, rows, :] = contrib if first else out_ref[hh, rows, :] + contrib
            invs.append(inv)
        return invs

    t_c = q0 + lax.broadcasted_iota(jnp.int32, (n_cmp, tq), 1)
    end_c = lax.broadcasted_iota(jnp.int32, (n_cmp, tq), 0) * CMP_STRIDE + (CMP_LEN - 1)
    bias_c = jnp.where(t_c >= end_c, 0.0, NEG_BIG)
    has_cmp = (q0 + lax.broadcasted_iota(jnp.int32, (1, tq), 1)) >= CMP_LEN - 1
    inv_c = [None] * N_HEADS

    def cmp_pair(g, head0):
        for i in range(2):
            x = st_ref[head0 + i, 0:n_cmp, :] + bias_c
            p_ref[head0 // 2, 0:n_cmp, i * tq:(i + 1) * tq] = jnp.exp2(x - jnp.max(x, axis=0, keepdims=True)).astype(BF16)
        pv = _dot(vct_ref[0, g], p_ref[head0 // 2, 0:n_cmp, :])
        inv_c[head0], inv_c[head0 + 1] = finish_pair(head0, pv, 0, True, has_cmp)

    stage_scores(lambda g: kc_ref[0, g], n_cmp)
    for g, head0 in pairs:
        cmp_pair(g, head0)

    raw = [_dot(ovt_ref[...], p_ref[pair, 0:n_cmp, :]) for pair in range(N_HEADS // 2)]

    w0 = pl.multiple_of(jnp.maximum(q0 - WINDOW, 0), tq)
    stage_scores(lambda g: kw_ref[0, g, pl.ds(w0, n_win), :], n_win)

    blk = lax.broadcasted_iota(jnp.int32, (n_sel, tq), 0)
    cur = (q0 + lax.broadcasted_iota(jnp.int32, (n_sel, tq), 1)) // SEL_LEN
    valid = blk <= cur
    forced = valid & ((blk == 0) | (blk == cur) | (blk == cur - 1))
    blk_f = blk.astype(F32)
    score = []
    for g in range(N_GROUPS):
        total = None
        for head in range(g * HEADS_PER_GROUP, (g + 1) * HEADS_PER_GROUP):
            part = raw[head // 2][:, (head % 2) * tq:(head % 2 + 1) * tq] * inv_c[head]
            total = part if total is None else total + part
        score.append(jnp.where(forced, TOPK_BIG, jnp.where(valid, total, -TOPK_BIG)))

    for _ in range(min(N_SELECT, n_sel)):
        for g in range(N_GROUPS):
            best = jnp.max(score[g], axis=0, keepdims=True)
            pick = jnp.min(jnp.where(score[g] == best, blk_f, float(n_sel)), axis=0, keepdims=True)
            score[g] = jnp.where(blk_f == pick, TOPK_TAKEN, score[g])
    sel = [jnp.where(score[g] == TOPK_TAKEN, 1.0, 0.0) for g in range(N_GROUPS)]
    for g in range(N_GROUPS):
        selt_ref[g] = sel[g]
    sel_any = sel[0] + sel[1]
    n_causal = (q0 + tq + SLC_CHUNK - 1) // SLC_CHUNK
    n_items = jnp.int32(0)
    for c in range(n_chunks):
        any_sel = jnp.max(sel_any[c * BLOCKS_PER_CHUNK:(c + 1) * BLOCKS_PER_CHUNK, :])
        list_ref[n_items] = c
        n_items = n_items + ((any_sel > 0.5) & (c < n_causal)).astype(jnp.int32)

    dist_w = (q0 + lax.broadcasted_iota(jnp.int32, (n_win, tq), 1)) - (w0 + lax.broadcasted_iota(jnp.int32, (n_win, tq), 0))
    bias_w = jnp.where((dist_w >= 0) & (dist_w < WINDOW), 0.0, NEG_BIG)

    def win_pair(g, head0):
        for i in range(2):
            x = st_ref[head0 + i, 0:n_win, :] + bias_w
            p_ref[head0 // 2, 0:n_win, i * tq:(i + 1) * tq] = jnp.exp2(x - jnp.max(x, axis=0, keepdims=True)).astype(BF16)
        pv = _dot(vwt_ref[0, g, :, pl.ds(w0, n_win)], p_ref[head0 // 2, 0:n_win, :])
        finish_pair(head0, pv, 2, False)

    for g, head0 in pairs:
        win_pair(g, head0)

    m_ref[...] = jnp.full_like(m_ref, NEG_BIG)
    acc_ref[...] = jnp.zeros_like(acc_ref)
    base = (lax.broadcasted_iota(jnp.int32, (SLC_CHUNK, tq), 1)
            - lax.broadcasted_iota(jnp.int32, (SLC_CHUNK, tq), 0))

    def stage_chunk(item, row0):
        c = list_ref[item]
        k0 = pl.multiple_of(c * SLC_CHUNK, SLC_CHUNK)
        causal = base >= k0 - q0
        for g, head0 in pairs:
            if head0 % HEADS_PER_GROUP == 0:
                sel_keys = jnp.concatenate(
                    [jnp.broadcast_to(selt_ref[g, pl.ds(c * BLOCKS_PER_CHUNK + i, 1), :], (SEL_LEN, tq))
                     for i in range(BLOCKS_PER_CHUNK)], axis=0)
                bias = jnp.where(causal & (sel_keys > 0.5), 0.0, NEG_BIG)
                bias2 = jnp.concatenate([bias, bias], axis=1)
            q_pair = q_ref[0, head0:head0 + 2].reshape(2 * tq, LANES)
            scores = _dot_nt(ks_ref[0, g, pl.ds(k0, SLC_CHUNK), :], q_pair) + bias2
            for i in range(2):
                st_ref[head0 + i, row0:row0 + SLC_CHUNK, :] = scores[:, i * tq:(i + 1) * tq]

    def consume_chunk(item, row0):
        k0 = pl.multiple_of(list_ref[item] * SLC_CHUNK, SLC_CHUNK)
        for g, head0 in pairs:
            alphas = []
            for i, head in enumerate((head0, head0 + 1)):
                x = st_ref[head, row0:row0 + SLC_CHUNK, :]
                m_old = m_ref[head:head + 1, :]
                m_new = jnp.maximum(m_old, jnp.max(x, axis=0, keepdims=True))
                alphas.append(jnp.exp2(m_old - m_new))
                m_ref[head:head + 1, :] = m_new
                p_ref[head0 // 2, 0:SLC_CHUNK, i * tq:(i + 1) * tq] = jnp.exp2(x - m_new).astype(BF16)
            pv = _dot(vst_ref[0, g, :, pl.ds(k0, SLC_CHUNK)], p_ref[head0 // 2, 0:SLC_CHUNK, :])
            acc_ref[head0 // 2] = acc_ref[head0 // 2] * jnp.concatenate(alphas, axis=1) + pv

    last = n_items - 1
    stage_chunk(0, 0)

    def two_chunks(j, carry):
        stage_chunk(jnp.minimum(2 * j + 1, last), SLC_CHUNK)
        consume_chunk(2 * j, 0)

        @pl.when(2 * j + 1 < n_items)
        def _():
            stage_chunk(jnp.minimum(2 * j + 2, last), 0)
            consume_chunk(2 * j + 1, SLC_CHUNK)
        return carry

    lax.fori_loop(0, (n_items + 1) // 2, two_chunks, 0)
    for g, head0 in pairs:
        finish_pair(head0, acc_ref[head0 // 2], 1, False)

    for hh in range(HEADS_PER_GROUP):
        o_ref[:, hh * LANES:(hh + 1) * LANES] = out_ref[hh].T.astype(BF16)


def _overlap_t(seq):
    n_cmp = (seq - CMP_LEN) // CMP_STRIDE + 1
    n_sel = seq // SEL_LEN
    cs = np.arange(n_cmp)[:, None] * CMP_STRIDE
    ss = np.arange(n_sel)[None, :] * SEL_LEN
    ov = np.clip(np.minimum(cs + CMP_LEN, ss + SEL_LEN) - np.maximum(cs, ss), 0, None) / CMP_LEN
    out = np.zeros((n_sel, seq // CMP_STRIDE), np.float32)
    out[:, :n_cmp] = ov.T
    return jnp.asarray(out, dtype=BF16)


def _nsa(q, kc, vct, ks, vst, kw, vwt, gates, *, tq=128):
    batch, _, seq, _ = q.shape
    n_cmp = kc.shape[2]
    n_sel = seq // SEL_LEN
    assert seq % SLC_CHUNK == 0 and seq >= WINDOW + tq and SLC_CHUNK % tq == 0 and tq == LANES
    assert n_sel <= 2 * SEL_LEN
    ovt = _overlap_t(seq)
    keys = lambda n: pl.BlockSpec((1, N_GROUPS, n, LANES), lambda b, i: (b, 0, 0, 0), pipeline_mode=pl.Buffered(1))
    vals = lambda n: pl.BlockSpec((1, N_GROUPS, V_ROWS, n), lambda b, i: (b, 0, 0, 0), pipeline_mode=pl.Buffered(1))
    tiles = seq // tq
    s_rows = max(n_cmp, WINDOW + tq, SLC_CHUNK)
    return pl.pallas_call(
        functools.partial(_nsa_kernel, tq=tq, seq=seq),
        out_shape=jax.ShapeDtypeStruct((batch * seq, HEADS_PER_GROUP * LANES), BF16),
        grid=(batch, tiles),
        in_specs=[
            pl.BlockSpec((1, N_HEADS, tq, LANES), lambda b, i: (b, 0, i, 0)),
            keys(n_cmp), vals(n_cmp), keys(seq), vals(seq), keys(seq), vals(seq),
            pl.BlockSpec((tq, LANES), lambda b, i: (b * tiles + i, 0)),
            pl.BlockSpec(ovt.shape, lambda b, i: (0, 0)),
        ],
        out_specs=pl.BlockSpec((tq, HEADS_PER_GROUP * LANES), lambda b, i: (b * tiles + i, 0)),
        scratch_shapes=[
            pltpu.VMEM((N_HEADS, s_rows, tq), F32),
            pltpu.VMEM((N_HEADS // 2, s_rows, 2 * tq), BF16),
            pltpu.VMEM((LANES, tq), F32),
            pltpu.VMEM((N_GROUPS, n_sel, tq), F32),
            pltpu.SMEM((seq // SLC_CHUNK,), jnp.int32),
            pltpu.VMEM((N_HEADS, tq), F32),
            pltpu.VMEM((N_HEADS // 2, V_ROWS, 2 * tq), F32),
            pltpu.VMEM((HEADS_PER_GROUP, N_GROUPS * HEAD_DIM, tq), F32),
        ],
        compiler_params=_params("parallel", "arbitrary"),
        name="nsa",
    )(q, kc, vct, ks, vst, kw, vwt, gates, ovt)


HALO = max(POOL_WINDOWS)


def _merge_kernel(x_ref, u_ref, halo_ref, gm_ref, on_ref, pw_ref, ps_ref, wbp_ref, wbn_ref, wo_ref,
                  o_ref, ext_ref, *, tm, seq):
    d = x_ref.shape[1]
    pos0 = (pl.program_id(0) * tm) % seq
    ext_ref[0:HALO, :] = jnp.where(pos0 == 0, 0.0, halo_ref[...])
    ext_ref[HALO:HALO + tm, :] = u_ref[...]
    pos = (pos0 + lax.broadcasted_iota(jnp.int32, (tm, POOL_GROUP_DIM), 0)).astype(F32)
    mixed = []
    for gi, w in enumerate(POOL_WINDOWS):
        cols = slice(gi * POOL_GROUP_DIM, (gi + 1) * POOL_GROUP_DIM)
        u = ext_ref[HALO:HALO + tm, cols]
        total = u
        for lag in range(1, w):
            total = total + ext_ref[HALO - lag:HALO - lag + tm, cols]
        delta = total / jnp.minimum(pos + 1.0, float(w)) - u
        mixed.append(_dot(delta.astype(BF16), pw_ref[gi]) * ps_ref[:, cols])
    mixed = jnp.concatenate(mixed, axis=-1).astype(BF16)
    a = _dot(mixed, wbp_ref[...])
    b = _dot(on_ref[...], wbn_ref[...])
    merged = gm_ref[:, 0:d] * a + gm_ref[:, d:2 * d] * b
    o_ref[...] = x_ref[...] + _dot(merged.astype(BF16), wo_ref[...])


def _merge(x, u, gm, o_nsa, pool_w, pool_scale, w_bp, w_bn, w_out, seq, *, tm=512):
    n, d = x.shape
    row = lambda i: (i, 0)
    full = lambda a: pl.BlockSpec(a.shape, lambda i: (0,) * a.ndim, pipeline_mode=pl.Buffered(1))
    halo_blocks = tm // HALO
    return pl.pallas_call(
        functools.partial(_merge_kernel, tm=tm, seq=seq),
        out_shape=jax.ShapeDtypeStruct((n, d), F32),
        grid=(n // tm,),
        in_specs=[
            pl.BlockSpec((tm, d), row),
            pl.BlockSpec((tm, POOL_COLS), row),
            pl.BlockSpec((HALO, POOL_COLS), lambda i: (jnp.maximum(i * halo_blocks - 1, 0), 0)),
            pl.BlockSpec((tm, 2 * d), row),
            pl.BlockSpec((tm, d), row),
            full(pool_w), full(pool_scale), full(w_bp), full(w_bn), full(w_out),
        ],
        out_specs=pl.BlockSpec((tm, d), row),
        scratch_shapes=[pltpu.VMEM((HALO + tm, POOL_COLS), F32)],
        compiler_params=_params("parallel"),
        name="merge",
    )(x, u, u, gm, o_nsa, pool_w, pool_scale, w_bp, w_bn, w_out)


def kernel(x, ffn1_norm, ffn1_w_gate, ffn1_w_up, ffn1_w_down, mix_norm, w_in, cmp_pos, cmp_k_w1, cmp_k_w2, cmp_v_w1, cmp_v_w2, pool_w, pool_scale, w_branch_pool, w_branch_nsa, w_out, ffn2_norm, ffn2_w_gate, ffn2_w_up, ffn2_w_down, final_norm):
    batch, seq, d = x.shape
    depth = w_in.shape[0]
    xf = x.reshape(batch * seq, d)
    bf = lambda a: a.astype(BF16)
    row = lambda a: a.reshape(1, -1)
    for l in range(depth):
        xf = _ffn(xf, row(ffn1_norm[l]), bf(ffn1_w_gate[l]), bf(ffn1_w_up[l]), bf(ffn1_w_down[l]),
                  row(final_norm), final_norm=False)

        q, kc, vc, ks, kw, vs, vw, gates, u, gm = _proj(xf, row(mix_norm[l]), _pack_w_in(w_in[l], d), batch, seq)
        k_cmp, v_cmp = _compress(kc, vc, cmp_pos[l], cmp_k_w1[l], cmp_k_w2[l], cmp_v_w1[l], cmp_v_w2[l], batch, seq)
        o_nsa = _nsa(q, k_cmp, v_cmp, ks, vs, kw, vw, gates)

        w_bn = w_branch_nsa[l].reshape(N_GROUPS, HEADS_PER_GROUP, HEAD_DIM, d).transpose(1, 0, 2, 3).reshape(-1, d)
        xf = _merge(xf, u, gm, o_nsa, bf(pool_w[l]), row(pool_scale[l]), bf(w_branch_pool[l]), bf(w_bn),
                    bf(w_out[l]), seq)

        xf = _ffn(xf, row(ffn2_norm[l]), bf(ffn2_w_gate[l]), bf(ffn2_w_up[l]), bf(ffn2_w_down[l]),
                  row(final_norm), final_norm=(l == depth - 1))
    return xf.reshape(batch, seq, d)
```

```python
import functools

import jax
import jax.numpy as jnp
import numpy as np
from jax import lax
from jax.experimental import pallas as pl
from jax.experimental.pallas import tpu as pltpu

N_HEADS = 16
N_GROUPS = 2
HEADS_PER_GROUP = N_HEADS // N_GROUPS
HEAD_DIM = 64
CMP_LEN = 32
CMP_STRIDE = 16
SEL_LEN = 64
N_SELECT = 16
WINDOW = 512
POOL_WINDOWS = (2, 4, 8, 16)
POOL_GROUP_DIM = 128
RMS_EPS = 1e-6
ALIBI_MAX_BIAS = 8.0

LANES = 128
NEG_BIG = -1e30
TOPK_BIG = 1e30
TOPK_TAKEN = -2e30
VMEM_LIMIT = 40 * 1024 * 1024

BF16 = jnp.bfloat16
F32 = jnp.float32


def _dot(a, b):
    return jnp.dot(a, b, preferred_element_type=F32)


def _dot_nt(a, b):
    return lax.dot_general(a, b, (((1,), (1,)), ((), ())), preferred_element_type=F32)


def _rms(x, g):
    return x * lax.rsqrt(jnp.mean(x * x, axis=-1, keepdims=True) + RMS_EPS) * g


def _params(*sem):
    return pltpu.CompilerParams(dimension_semantics=sem, vmem_limit_bytes=VMEM_LIMIT)


def _ffn_kernel(x_ref, g_ref, wg_ref, wu_ref, wd_ref, fin_ref, o_ref, *, final_norm):
    x = x_ref[...]
    xn = _rms(x, g_ref[...]).astype(BF16)
    gate = _dot(xn, wg_ref[...])
    up = _dot(xn, wu_ref[...])
    act = (gate * jax.nn.sigmoid(gate)) * up
    y = x + 0.5 * _dot(act.astype(BF16), wd_ref[...])
    if final_norm:
        y = _rms(y, fin_ref[...])
    o_ref[...] = y


def _ffn(x, norm_g, wg, wu, wd, fin_g, *, final_norm, tm=512):
    n, d = x.shape
    row = lambda i: (i, 0)
    resident = lambda a: pl.BlockSpec(a.shape, lambda i: (0, 0), pipeline_mode=pl.Buffered(1))
    return pl.pallas_call(
        functools.partial(_ffn_kernel, final_norm=final_norm),
        out_shape=jax.ShapeDtypeStruct((n, d), F32),
        grid=(n // tm,),
        in_specs=[pl.BlockSpec((tm, d), row), resident(norm_g), resident(wg), resident(wu), resident(wd),
                  resident(fin_g)],
        out_specs=pl.BlockSpec((tm, d), row),
        compiler_params=_params("parallel"),
        name="ffn",
    )(x, norm_g, wg, wu, wd, fin_g)


LOG2E = 1.4426950408889634
N_FEAT = 6


def _bf16_pieces(x):
    x = np.asarray(x, np.float32)
    s1 = x.astype(BF16).astype(np.float32)
    s2 = (x - s1).astype(BF16).astype(np.float32)
    s3 = (x - s1 - s2).astype(BF16).astype(np.float32)
    return s1, s2, s3


def _query_features():
    slopes = np.float32(2.0) ** (-ALIBI_MAX_BIAS * np.arange(1, N_HEADS + 1, dtype=np.float32) / N_HEADS)
    s1, s2, s3 = _bf16_pieces(slopes * np.float32(LOG2E))
    feat = np.zeros((N_HEADS, LANES), np.float32)
    feat[:, HEAD_DIM:HEAD_DIM + N_FEAT] = np.stack([s1, s2, s3, SEL_LEN * s1, SEL_LEN * s2, SEL_LEN * s3], axis=1)
    return jnp.asarray(feat)


def _key_features(pos, width, offset):
    pos = np.asarray(pos)
    a, b = (pos // SEL_LEN).astype(np.float32), (pos % SEL_LEN).astype(np.float32)
    feat = np.zeros((len(pos), width), np.float32)
    feat[:, offset:offset + N_FEAT] = np.stack([b, b, b, a, a, a], axis=1)
    return jnp.asarray(feat)


Q_COLS = N_HEADS * LANES
CMP_OFF = Q_COLS
KEY_OFF = CMP_OFF + 2 * LANES
VAL_OFF = KEY_OFF + 2 * N_GROUPS * LANES
GATE_OFF = VAL_OFF + 2 * LANES
POOL_OFF = GATE_OFF + LANES
POOL_COLS = len(POOL_WINDOWS) * POOL_GROUP_DIM
MERGE_OFF = POOL_OFF + POOL_COLS


V_ROWS = HEAD_DIM + 16


def _store_values_t(v_ref, v):
    vt = v.T
    for g in range(N_GROUPS):
        v_ref[0, g, 0:HEAD_DIM, :] = vt[g * HEAD_DIM:(g + 1) * HEAD_DIM, :].astype(BF16)
        v_ref[0, g, HEAD_DIM:V_ROWS, :] = jnp.ones((V_ROWS - HEAD_DIM, v.shape[0]), BF16)


def _proj_kernel(x_ref, g_ref, w_ref, qf_ref, kf_ref, q_ref, kc_ref, vc_ref, ks_ref, kw_ref, vs_ref, vw_ref,
                 gate_ref, u_ref, gm_ref):
    d = x_ref.shape[1]
    hn = _rms(x_ref[...], g_ref[...]).astype(BF16)
    q_scale = HEAD_DIM ** -0.5 * LOG2E
    for h in range(0, N_HEADS, 2):
        q2 = _dot(hn, w_ref[:, h * LANES:(h + 2) * LANES]) * q_scale
        for i in range(2):
            q_ref[0, h + i] = (q2[:, i * LANES:(i + 1) * LANES] + qf_ref[h + i:h + i + 1, :]).astype(BF16)
    cmp_in = _dot(hn, w_ref[:, CMP_OFF:CMP_OFF + 2 * LANES])
    kc_ref[...] = cmp_in[:, 0:LANES]
    vc_ref[...] = cmp_in[:, LANES:2 * LANES]
    keys = _dot(hn, w_ref[:, KEY_OFF:KEY_OFF + 2 * N_GROUPS * LANES])
    kf = kf_ref[...]
    for g in range(N_GROUPS):
        ks_ref[0, g] = (keys[:, g * LANES:(g + 1) * LANES] + kf).astype(BF16)
        kw_ref[0, g] = (keys[:, (N_GROUPS + g) * LANES:(N_GROUPS + g + 1) * LANES] + kf).astype(BF16)
    vals = _dot(hn, w_ref[:, VAL_OFF:VAL_OFF + 2 * LANES])
    for i, v_ref in enumerate((vs_ref, vw_ref)):
        _store_values_t(v_ref, vals[:, i * LANES:(i + 1) * LANES])
    gate_ref[...] = jax.nn.sigmoid(_dot(hn, w_ref[:, GATE_OFF:GATE_OFF + LANES]))
    u_ref[...] = _dot(hn, w_ref[:, POOL_OFF:POOL_OFF + POOL_COLS])
    gm_ref[...] = jax.nn.sigmoid(_dot(hn, w_ref[:, MERGE_OFF:MERGE_OFF + 2 * d]))


def _pack_w_in(w_in, d):
    w_in = w_in.astype(BF16)
    qw = N_HEADS * HEAD_DIM
    kvw = N_GROUPS * HEAD_DIM

    def padded(w, n):
        w = w.reshape(d, n, HEAD_DIM)
        return jnp.concatenate([w, jnp.zeros_like(w)], axis=-1).reshape(d, n * LANES)

    kc, vc, ks, vs, kw, vw = (w_in[:, qw + i * kvw:qw + (i + 1) * kvw] for i in range(6))
    off = qw + 6 * kvw
    n_g = 3 * N_HEADS
    w_g = jnp.pad(w_in[:, off:off + n_g], ((0, 0), (0, LANES - n_g)))
    w_rest = w_in[:, off + n_g:]
    return jnp.concatenate([padded(w_in[:, :qw], N_HEADS), kc, vc, padded(ks, N_GROUPS), padded(kw, N_GROUPS),
                            vs, vw, w_g, w_rest], axis=1)


def _proj(x, norm_g, w_packed, batch, seq, *, tm=512):
    n, d = x.shape
    tiles_per_seq = seq // tm
    row = lambda i: (i, 0)
    const = lambda i: (0, 0)
    per_group = lambda i: (i // tiles_per_seq, 0, i % tiles_per_seq, 0)
    qf = _query_features()
    kf = _key_features(np.arange(seq), LANES, HEAD_DIM)
    flat = lambda width, dtype: (jax.ShapeDtypeStruct((n, width), dtype), pl.BlockSpec((tm, width), row))
    grouped = lambda count: (jax.ShapeDtypeStruct((batch, count, seq, LANES), BF16),
                             pl.BlockSpec((1, count, tm, LANES), per_group))
    values_t = (jax.ShapeDtypeStruct((batch, N_GROUPS, V_ROWS, seq), BF16),
                pl.BlockSpec((1, N_GROUPS, V_ROWS, tm), lambda i: (i // tiles_per_seq, 0, 0, i % tiles_per_seq)))
    outs = [grouped(N_HEADS), flat(LANES, F32), flat(LANES, F32), grouped(N_GROUPS), grouped(N_GROUPS),
            values_t, values_t, flat(LANES, F32), flat(POOL_COLS, F32), flat(2 * d, F32)]
    return pl.pallas_call(
        _proj_kernel,
        out_shape=[o[0] for o in outs],
        grid=(n // tm,),
        in_specs=[pl.BlockSpec((tm, d), row), pl.BlockSpec((1, d), const),
                  pl.BlockSpec(w_packed.shape, const, pipeline_mode=pl.Buffered(1)),
                  pl.BlockSpec(qf.shape, const), pl.BlockSpec((tm, LANES), lambda i: (i % tiles_per_seq, 0))],
        out_specs=[o[1] for o in outs],
        compiler_params=_params("parallel"),
        name="proj",
    )(x, norm_g, w_packed, qf, kf)


def _compress_kernel(k_ref, v_ref, pos_ref, wk1_ref, wk2_ref, wv1_ref, wv2_ref, feat_ref, ko_ref, vo_ref):
    n_chunk = k_ref.shape[0] // CMP_STRIDE
    hidden = wk2_ref.shape[0]

    def compress(x_ref, w1_ref, w2_ref):
        first = second = None
        for r in range(CMP_STRIDE):
            rows = x_ref[pl.ds(r, n_chunk, stride=CMP_STRIDE), :]
            a = _dot((rows + pos_ref[r:r + 1, :]).astype(BF16), w1_ref[r])
            b = _dot((rows + pos_ref[CMP_STRIDE + r:CMP_STRIDE + r + 1, :]).astype(BF16), w1_ref[CMP_STRIDE + r])
            first = a if first is None else first + a
            second = b if second is None else second + b
        act = jax.nn.gelu(first + pltpu.roll(second, n_chunk - 1, 0)).astype(BF16)
        return [_dot(act[:, g * hidden:(g + 1) * hidden], w2_ref[...]) for g in range(N_GROUPS)]

    for g, k_cmp in enumerate(compress(k_ref, wk1_ref, wk2_ref)):
        ko_ref[0, g] = jnp.concatenate([k_cmp, feat_ref[...]], axis=-1).astype(BF16)
    _store_values_t(vo_ref, jnp.concatenate(compress(v_ref, wv1_ref, wv2_ref), axis=-1))


def _block_diag_w1(w1):
    w = w1.reshape(CMP_LEN, HEAD_DIM, -1)
    z = jnp.zeros_like(w)
    return jnp.concatenate([jnp.concatenate([w, z], axis=2), jnp.concatenate([z, w], axis=2)], axis=1).astype(BF16)


def _compress(kc, vc, pos, wk1, wk2, wv1, wv2, batch, seq):
    n_chunk = seq // CMP_STRIDE
    rows = pl.BlockSpec((seq, LANES), lambda b: (b, 0))
    full = lambda a: pl.BlockSpec(a.shape, lambda b: (0,) * a.ndim, pipeline_mode=pl.Buffered(1))
    pos2 = jnp.concatenate([pos] * N_GROUPS, axis=1)
    feat = _key_features(np.arange(n_chunk) * CMP_STRIDE + CMP_LEN - 1, HEAD_DIM, 0)
    args = (pos2, _block_diag_w1(wk1), wk2.astype(BF16), _block_diag_w1(wv1), wv2.astype(BF16), feat)
    return pl.pallas_call(
        _compress_kernel,
        out_shape=[jax.ShapeDtypeStruct((batch, N_GROUPS, n_chunk, LANES), BF16),
                   jax.ShapeDtypeStruct((batch, N_GROUPS, V_ROWS, n_chunk), BF16)],
        grid=(batch,),
        in_specs=[rows, rows] + [full(a) for a in args],
        out_specs=[pl.BlockSpec((1, N_GROUPS, n_chunk, LANES), lambda b: (b, 0, 0, 0)),
                   pl.BlockSpec((1, N_GROUPS, V_ROWS, n_chunk), lambda b: (b, 0, 0, 0))],
        compiler_params=_params("parallel"),
        name="compress",
    )(kc, vc, *args)


SLC_CHUNK = 256
BLOCKS_PER_CHUNK = SLC_CHUNK // SEL_LEN
TILES_PER_STEP = 2


def _nsa_kernel(*refs, tq, seq):
    def one_tile(sub, carry):
        _nsa_tile(sub, *refs, tq=tq, seq=seq)
        return carry

    lax.fori_loop(0, TILES_PER_STEP, one_tile, 0)


def _nsa_tile(sub, q_ref, kc_ref, vct_ref, ks_ref, vst_ref, kw_ref, vwt_ref, gate_ref, ovt_ref, o_ref,
              st_ref, p_ref, gt_ref, selt_ref, list_ref, m_ref, acc_ref, out_ref, *, tq, seq):
    n_sel = seq // SEL_LEN
    n_chunks = seq // SLC_CHUNK
    n_cmp = kc_ref.shape[2]
    n_win = WINDOW + tq
    q0 = (pl.program_id(1) * TILES_PER_STEP + sub) * tq
    r0 = pl.multiple_of(sub * tq, tq)
    gt_ref[...] = gate_ref[pl.ds(r0, tq), :].T
    pairs = [(g, g * HEADS_PER_GROUP + 2 * j) for g in range(N_GROUPS) for j in range(HEADS_PER_GROUP // 2)]

    def gate_row(head, branch):
        c = 3 * head + branch
        return gt_ref[c:c + 1, :]

    def out_rows(head):
        g, hh = divmod(head, HEADS_PER_GROUP)
        return hh, slice(g * HEAD_DIM, (g + 1) * HEAD_DIM)

    def stage_scores(keys, n_keys):
        for g, head0 in pairs:
            q_pair = q_ref[0, head0:head0 + 2, pl.ds(r0, tq), :].reshape(2 * tq, LANES)
            scores = _dot_nt(keys(g), q_pair)
            for i in range(2):
                st_ref[head0 + i, 0:n_keys, :] = scores[:, i * tq:(i + 1) * tq]

    def finish_pair(head0, pv, branch, first, ok=None):
        invs = []
        for i in range(2):
            cols = slice(i * tq, (i + 1) * tq)
            inv = 1.0 / pv[HEAD_DIM:HEAD_DIM + 1, cols]
            if ok is not None:
                inv = jnp.where(ok, inv, 0.0)
            hh, rows = out_rows(head0 + i)
            contrib = pv[0:HEAD_DIM, cols] * (inv * gate_row(head0 + i, branch))
            out_ref[hh, rows, :] = contrib if first else out_ref[hh, rows, :] + contrib
            invs.append(inv)
        return invs

    t_c = q0 + lax.broadcasted_iota(jnp.int32, (n_cmp, tq), 1)
    end_c = lax.broadcasted_iota(jnp.int32, (n_cmp, tq), 0) * CMP_STRIDE + (CMP_LEN - 1)
    bias_c = jnp.where(t_c >= end_c, 0.0, NEG_BIG)
    has_cmp = (q0 + lax.broadcasted_iota(jnp.int32, (1, tq), 1)) >= CMP_LEN - 1
    inv_c = [None] * N_HEADS

    def cmp_pair(g, head0):
        for i in range(2):
            x = st_ref[head0 + i, 0:n_cmp, :] + bias_c
            p_ref[head0 // 2, 0:n_cmp, i * tq:(i + 1) * tq] = jnp.exp2(x - jnp.max(x, axis=0, keepdims=True)).astype(BF16)
        pv = _dot(vct_ref[0, g], p_ref[head0 // 2, 0:n_cmp, :])
        inv_c[head0], inv_c[head0 + 1] = finish_pair(head0, pv, 0, True, has_cmp)

    stage_scores(lambda g: kc_ref[0, g], n_cmp)
    for g, head0 in pairs:
        cmp_pair(g, head0)

    raw = [_dot(ovt_ref[...], p_ref[pair, 0:n_cmp, :]) for pair in range(N_HEADS // 2)]

    w0 = pl.multiple_of(jnp.maximum(q0 - WINDOW, 0), tq)
    stage_scores(lambda g: kw_ref[0, g, pl.ds(w0, n_win), :], n_win)

    blk = lax.broadcasted_iota(jnp.int32, (n_sel, tq), 0)
    cur = (q0 + lax.broadcasted_iota(jnp.int32, (n_sel, tq), 1)) // SEL_LEN
    valid = blk <= cur
    forced = valid & ((blk == 0) | (blk == cur) | (blk == cur - 1))
    blk_f = blk.astype(F32)
    score = []
    for g in range(N_GROUPS):
        total = None
        for head in range(g * HEADS_PER_GROUP, (g + 1) * HEADS_PER_GROUP):
            part = raw[head // 2][:, (head % 2) * tq:(head % 2 + 1) * tq] * inv_c[head]
            total = part if total is None else total + part
        score.append(jnp.where(forced, TOPK_BIG, jnp.where(valid, total, -TOPK_BIG)))

    for _ in range(min(N_SELECT, n_sel)):
        for g in range(N_GROUPS):
            best = jnp.max(score[g], axis=0, keepdims=True)
            pick = jnp.min(jnp.where(score[g] == best, blk_f, float(n_sel)), axis=0, keepdims=True)
            score[g] = jnp.where(blk_f == pick, TOPK_TAKEN, score[g])
    sel = [jnp.where(score[g] == TOPK_TAKEN, 1.0, 0.0) for g in range(N_GROUPS)]
    for g in range(N_GROUPS):
        selt_ref[g] = sel[g]
    sel_any = sel[0] + sel[1]
    n_causal = (q0 + tq + SLC_CHUNK - 1) // SLC_CHUNK
    n_items = jnp.int32(0)
    for c in range(n_chunks):
        any_sel = jnp.max(sel_any[c * BLOCKS_PER_CHUNK:(c + 1) * BLOCKS_PER_CHUNK, :])
        list_ref[n_items] = c
        n_items = n_items + ((any_sel > 0.5) & (c < n_causal)).astype(jnp.int32)

    dist_w = (q0 + lax.broadcasted_iota(jnp.int32, (n_win, tq), 1)) - (w0 + lax.broadcasted_iota(jnp.int32, (n_win, tq), 0))
    bias_w = jnp.where((dist_w >= 0) & (dist_w < WINDOW), 0.0, NEG_BIG)

    def win_pair(g, head0):
        for i in range(2):
            x = st_ref[head0 + i, 0:n_win, :] + bias_w
            p_ref[head0 // 2, 0:n_win, i * tq:(i + 1) * tq] = jnp.exp2(x - jnp.max(x, axis=0, keepdims=True)).astype(BF16)
        pv = _dot(vwt_ref[0, g, :, pl.ds(w0, n_win)], p_ref[head0 // 2, 0:n_win, :])
        finish_pair(head0, pv, 2, False)

    for g, head0 in pairs:
        win_pair(g, head0)

    m_ref[...] = jnp.full_like(m_ref, NEG_BIG)
    acc_ref[...] = jnp.zeros_like(acc_ref)
    base = (lax.broadcasted_iota(jnp.int32, (SLC_CHUNK, tq), 1)
            - lax.broadcasted_iota(jnp.int32, (SLC_CHUNK, tq), 0))

    def stage_chunk(item, row0):
        c = list_ref[item]
        k0 = pl.multiple_of(c * SLC_CHUNK, SLC_CHUNK)
        causal = base >= k0 - q0
        for g, head0 in pairs:
            if head0 % HEADS_PER_GROUP == 0:
                sel_keys = jnp.concatenate(
                    [jnp.broadcast_to(selt_ref[g, pl.ds(c * BLOCKS_PER_CHUNK + i, 1), :], (SEL_LEN, tq))
                     for i in range(BLOCKS_PER_CHUNK)], axis=0)
                bias = jnp.where(causal & (sel_keys > 0.5), 0.0, NEG_BIG)
                bias2 = jnp.concatenate([bias, bias], axis=1)
            q_pair = q_ref[0, head0:head0 + 2, pl.ds(r0, tq), :].reshape(2 * tq, LANES)
            scores = _dot_nt(ks_ref[0, g, pl.ds(k0, SLC_CHUNK), :], q_pair) + bias2
            for i in range(2):
                st_ref[head0 + i, row0:row0 + SLC_CHUNK, :] = scores[:, i * tq:(i + 1) * tq]

    def consume_chunk(item, row0):
        k0 = pl.multiple_of(list_ref[item] * SLC_CHUNK, SLC_CHUNK)
        for g, head0 in pairs:
            alphas = []
            for i, head in enumerate((head0, head0 + 1)):
                x = st_ref[head, row0:row0 + SLC_CHUNK, :]
                m_old = m_ref[head:head + 1, :]
                m_new = jnp.maximum(m_old, jnp.max(x, axis=0, keepdims=True))
                alphas.append(jnp.exp2(m_old - m_new))
                m_ref[head:head + 1, :] = m_new
                p_ref[head0 // 2, 0:SLC_CHUNK, i * tq:(i + 1) * tq] = jnp.exp2(x - m_new).astype(BF16)
            pv = _dot(vst_ref[0, g, :, pl.ds(k0, SLC_CHUNK)], p_ref[head0 // 2, 0:SLC_CHUNK, :])
            acc_ref[head0 // 2] = acc_ref[head0 // 2] * jnp.concatenate(alphas, axis=1) + pv

    last = n_items - 1
    stage_chunk(0, 0)

    def two_chunks(j, carry):
        stage_chunk(jnp.minimum(2 * j + 1, last), SLC_CHUNK)
        consume_chunk(2 * j, 0)

        @pl.when(2 * j + 1 < n_items)
        def _():
            stage_chunk(jnp.minimum(2 * j + 2, last), 0)
            consume_chunk(2 * j + 1, SLC_CHUNK)
        return carry

    lax.fori_loop(0, (n_items + 1) // 2, two_chunks, 0)
    for g, head0 in pairs:
        finish_pair(head0, acc_ref[head0 // 2], 1, False)

    for hh in range(HEADS_PER_GROUP):
        o_ref[pl.ds(r0, tq), hh * LANES:(hh + 1) * LANES] = out_ref[hh].T.astype(BF16)


def _overlap_t(seq):
    n_cmp = (seq - CMP_LEN) // CMP_STRIDE + 1
    n_sel = seq // SEL_LEN
    cs = np.arange(n_cmp)[:, None] * CMP_STRIDE
    ss = np.arange(n_sel)[None, :] * SEL_LEN
    ov = np.clip(np.minimum(cs + CMP_LEN, ss + SEL_LEN) - np.maximum(cs, ss), 0, None) / CMP_LEN
    out = np.zeros((n_sel, seq // CMP_STRIDE), np.float32)
    out[:, :n_cmp] = ov.T
    return jnp.asarray(out, dtype=BF16)


def _nsa(q, kc, vct, ks, vst, kw, vwt, gates, *, tq=128):
    batch, _, seq, _ = q.shape
    n_cmp = kc.shape[2]
    n_sel = seq // SEL_LEN
    assert seq % SLC_CHUNK == 0 and seq >= WINDOW + tq and SLC_CHUNK % tq == 0 and tq == LANES
    assert n_sel <= 2 * SEL_LEN
    ovt = _overlap_t(seq)
    keys = lambda n: pl.BlockSpec((1, N_GROUPS, n, LANES), lambda b, i: (b, 0, 0, 0), pipeline_mode=pl.Buffered(1))
    vals = lambda n: pl.BlockSpec((1, N_GROUPS, V_ROWS, n), lambda b, i: (b, 0, 0, 0), pipeline_mode=pl.Buffered(1))
    rows = TILES_PER_STEP * tq
    assert seq % rows == 0
    steps = seq // rows
    s_rows = max(n_cmp, WINDOW + tq, SLC_CHUNK)
    return pl.pallas_call(
        functools.partial(_nsa_kernel, tq=tq, seq=seq),
        out_shape=jax.ShapeDtypeStruct((batch * seq, HEADS_PER_GROUP * LANES), BF16),
        grid=(batch, steps),
        in_specs=[
            pl.BlockSpec((1, N_HEADS, rows, LANES), lambda b, i: (b, 0, i, 0)),
            keys(n_cmp), vals(n_cmp), keys(seq), vals(seq), keys(seq), vals(seq),
            pl.BlockSpec((rows, LANES), lambda b, i: (b * steps + i, 0)),
            pl.BlockSpec(ovt.shape, lambda b, i: (0, 0)),
        ],
        out_specs=pl.BlockSpec((rows, HEADS_PER_GROUP * LANES), lambda b, i: (b * steps + i, 0)),
        scratch_shapes=[
            pltpu.VMEM((N_HEADS, s_rows, tq), F32),
            pltpu.VMEM((N_HEADS // 2, s_rows, 2 * tq), BF16),
            pltpu.VMEM((LANES, tq), F32),
            pltpu.VMEM((N_GROUPS, n_sel, tq), F32),
            pltpu.SMEM((seq // SLC_CHUNK,), jnp.int32),
            pltpu.VMEM((N_HEADS, tq), F32),
            pltpu.VMEM((N_HEADS // 2, V_ROWS, 2 * tq), F32),
            pltpu.VMEM((HEADS_PER_GROUP, N_GROUPS * HEAD_DIM, tq), F32),
        ],
        compiler_params=_params("parallel", "arbitrary"),
        name="nsa",
    )(q, kc, vct, ks, vst, kw, vwt, gates, ovt)


HALO = max(POOL_WINDOWS)


def _merge_kernel(x_ref, u_ref, halo_ref, gm_ref, on_ref, pw_ref, ps_ref, wbp_ref, wbn_ref, wo_ref,
                  o_ref, ext_ref, *, tm, seq):
    d = x_ref.shape[1]
    pos0 = (pl.program_id(0) * tm) % seq
    ext_ref[0:HALO, :] = jnp.where(pos0 == 0, 0.0, halo_ref[...])
    ext_ref[HALO:HALO + tm, :] = u_ref[...]
    pos = (pos0 + lax.broadcasted_iota(jnp.int32, (tm, POOL_GROUP_DIM), 0)).astype(F32)
    mixed = []
    for gi, w in enumerate(POOL_WINDOWS):
        cols = slice(gi * POOL_GROUP_DIM, (gi + 1) * POOL_GROUP_DIM)
        u = ext_ref[HALO:HALO + tm, cols]
        total = u
        for lag in range(1, w):
            total = total + ext_ref[HALO - lag:HALO - lag + tm, cols]
        delta = total / jnp.minimum(pos + 1.0, float(w)) - u
        mixed.append(_dot(delta.astype(BF16), pw_ref[gi]) * ps_ref[:, cols])
    mixed = jnp.concatenate(mixed, axis=-1).astype(BF16)
    a = _dot(mixed, wbp_ref[...])
    b = _dot(on_ref[...], wbn_ref[...])
    merged = gm_ref[:, 0:d] * a + gm_ref[:, d:2 * d] * b
    o_ref[...] = x_ref[...] + _dot(merged.astype(BF16), wo_ref[...])


def _merge(x, u, gm, o_nsa, pool_w, pool_scale, w_bp, w_bn, w_out, seq, *, tm=512):
    n, d = x.shape
    row = lambda i: (i, 0)
    full = lambda a: pl.BlockSpec(a.shape, lambda i: (0,) * a.ndim, pipeline_mode=pl.Buffered(1))
    halo_blocks = tm // HALO
    return pl.pallas_call(
        functools.partial(_merge_kernel, tm=tm, seq=seq),
        out_shape=jax.ShapeDtypeStruct((n, d), F32),
        grid=(n // tm,),
        in_specs=[
            pl.BlockSpec((tm, d), row),
            pl.BlockSpec((tm, POOL_COLS), row),
            pl.BlockSpec((HALO, POOL_COLS), lambda i: (jnp.maximum(i * halo_blocks - 1, 0), 0)),
            pl.BlockSpec((tm, 2 * d), row),
            pl.BlockSpec((tm, d), row),
            full(pool_w), full(pool_scale), full(w_bp), full(w_bn), full(w_out),
        ],
        out_specs=pl.BlockSpec((tm, d), row),
        scratch_shapes=[pltpu.VMEM((HALO + tm, POOL_COLS), F32)],
        compiler_params=_params("parallel"),
        name="merge",
    )(x, u, u, gm, o_nsa, pool_w, pool_scale, w_bp, w_bn, w_out)


def kernel(x, ffn1_norm, ffn1_w_gate, ffn1_w_up, ffn1_w_down, mix_norm, w_in, cmp_pos, cmp_k_w1, cmp_k_w2, cmp_v_w1, cmp_v_w2, pool_w, pool_scale, w_branch_pool, w_branch_nsa, w_out, ffn2_norm, ffn2_w_gate, ffn2_w_up, ffn2_w_down, final_norm):
    batch, seq, d = x.shape
    depth = w_in.shape[0]
    xf = x.reshape(batch * seq, d)
    bf = lambda a: a.astype(BF16)
    row = lambda a: a.reshape(1, -1)
    for l in range(depth):
        xf = _ffn(xf, row(ffn1_norm[l]), bf(ffn1_w_gate[l]), bf(ffn1_w_up[l]), bf(ffn1_w_down[l]),
                  row(final_norm), final_norm=False)

        q, kc, vc, ks, kw, vs, vw, gates, u, gm = _proj(xf, row(mix_norm[l]), _pack_w_in(w_in[l], d), batch, seq)
        k_cmp, v_cmp = _compress(kc, vc, cmp_pos[l], cmp_k_w1[l], cmp_k_w2[l], cmp_v_w1[l], cmp_v_w2[l], batch, seq)
        o_nsa = _nsa(q, k_cmp, v_cmp, ks, vs, kw, vw, gates)

        w_bn = w_branch_nsa[l].reshape(N_GROUPS, HEADS_PER_GROUP, HEAD_DIM, d).transpose(1, 0, 2, 3).reshape(-1, d)
        xf = _merge(xf, u, gm, o_nsa, bf(pool_w[l]), row(pool_scale[l]), bf(w_branch_pool[l]), bf(w_bn),
                    bf(w_out[l]), seq)

        xf = _ffn(xf, row(ffn2_norm[l]), bf(ffn2_w_gate[l]), bf(ffn2_w_up[l]), bf(ffn2_w_down[l]),
                  row(final_norm), final_norm=(l == depth - 1))
    return xf.reshape(batch, seq, d)
```

```python
import functools

import jax
import jax.numpy as jnp
import numpy as np
from jax import lax
from jax.experimental import pallas as pl
from jax.experimental.pallas import tpu as pltpu

N_HEADS = 16
N_GROUPS = 2
HEADS_PER_GROUP = N_HEADS // N_GROUPS
HEAD_DIM = 64
CMP_LEN = 32
CMP_STRIDE = 16
SEL_LEN = 64
N_SELECT = 16
WINDOW = 512
POOL_WINDOWS = (2, 4, 8, 16)
POOL_GROUP_DIM = 128
RMS_EPS = 1e-6
ALIBI_MAX_BIAS = 8.0

LANES = 128
NEG_BIG = -1e30
TOPK_BIG = 1e30
TOPK_TAKEN = -2e30
VMEM_LIMIT = 48 * 1024 * 1024

BF16 = jnp.bfloat16
F32 = jnp.float32


def _dot(a, b):
    return jnp.dot(a, b, preferred_element_type=F32)


def _dot_nt(a, b):
    return lax.dot_general(a, b, (((1,), (1,)), ((), ())), preferred_element_type=F32)


def _rms(x, g):
    return x * lax.rsqrt(jnp.mean(x * x, axis=-1, keepdims=True) + RMS_EPS) * g


def _params(*sem):
    return pltpu.CompilerParams(dimension_semantics=sem, vmem_limit_bytes=VMEM_LIMIT)


def _ffn_kernel(x_ref, g_ref, wg_ref, wu_ref, wd_ref, fin_ref, o_ref, *, final_norm):
    x = x_ref[...]
    xn = _rms(x, g_ref[...]).astype(BF16)
    gate = _dot(xn, wg_ref[...])
    up = _dot(xn, wu_ref[...])
    act = (gate * jax.nn.sigmoid(gate)) * up
    y = x + 0.5 * _dot(act.astype(BF16), wd_ref[...])
    if final_norm:
        y = _rms(y, fin_ref[...])
    o_ref[...] = y


def _ffn(x, norm_g, wg, wu, wd, fin_g, *, final_norm, tm=512):
    n, d = x.shape
    row = lambda i: (i, 0)
    resident = lambda a: pl.BlockSpec(a.shape, lambda i: (0, 0), pipeline_mode=pl.Buffered(1))
    return pl.pallas_call(
        functools.partial(_ffn_kernel, final_norm=final_norm),
        out_shape=jax.ShapeDtypeStruct((n, d), F32),
        grid=(n // tm,),
        in_specs=[pl.BlockSpec((tm, d), row), resident(norm_g), resident(wg), resident(wu), resident(wd),
                  resident(fin_g)],
        out_specs=pl.BlockSpec((tm, d), row),
        compiler_params=_params("parallel"),
        name="ffn",
    )(x, norm_g, wg, wu, wd, fin_g)


LOG2E = 1.4426950408889634
N_FEAT = 6


def _bf16_pieces(x):
    x = np.asarray(x, np.float32)
    s1 = x.astype(BF16).astype(np.float32)
    s2 = (x - s1).astype(BF16).astype(np.float32)
    s3 = (x - s1 - s2).astype(BF16).astype(np.float32)
    return s1, s2, s3


def _query_features():
    slopes = np.float32(2.0) ** (-ALIBI_MAX_BIAS * np.arange(1, N_HEADS + 1, dtype=np.float32) / N_HEADS)
    s1, s2, s3 = _bf16_pieces(slopes * np.float32(LOG2E))
    feat = np.zeros((N_HEADS, LANES), np.float32)
    feat[:, HEAD_DIM:HEAD_DIM + N_FEAT] = np.stack([s1, s2, s3, SEL_LEN * s1, SEL_LEN * s2, SEL_LEN * s3], axis=1)
    return jnp.asarray(feat)


def _key_features(pos, width, offset):
    pos = np.asarray(pos)
    a, b = (pos // SEL_LEN).astype(np.float32), (pos % SEL_LEN).astype(np.float32)
    feat = np.zeros((len(pos), width), np.float32)
    feat[:, offset:offset + N_FEAT] = np.stack([b, b, b, a, a, a], axis=1)
    return jnp.asarray(feat)


Q_COLS = N_HEADS * LANES
CMP_OFF = Q_COLS
KEY_OFF = CMP_OFF + 2 * LANES
VAL_OFF = KEY_OFF + 2 * N_GROUPS * LANES
GATE_OFF = VAL_OFF + 2 * LANES
POOL_OFF = GATE_OFF + LANES
POOL_COLS = len(POOL_WINDOWS) * POOL_GROUP_DIM
MERGE_OFF = POOL_OFF + POOL_COLS


V_ROWS = HEAD_DIM + 16


def _store_values_t(v_ref, v):
    vt = v.T
    for g in range(N_GROUPS):
        v_ref[0, g, 0:HEAD_DIM, :] = vt[g * HEAD_DIM:(g + 1) * HEAD_DIM, :].astype(BF16)
        v_ref[0, g, HEAD_DIM:V_ROWS, :] = jnp.ones((V_ROWS - HEAD_DIM, v.shape[0]), BF16)


def _proj_kernel(x_ref, g_ref, w_ref, qf_ref, kf_ref, q_ref, kc_ref, vc_ref, ks_ref, kw_ref, vs_ref, vw_ref,
                 gate_ref, u_ref, gm_ref):
    d = x_ref.shape[1]
    hn = _rms(x_ref[...], g_ref[...]).astype(BF16)
    q_scale = HEAD_DIM ** -0.5 * LOG2E
    for h in range(0, N_HEADS, 2):
        q2 = _dot(hn, w_ref[:, h * LANES:(h + 2) * LANES]) * q_scale
        for i in range(2):
            q_ref[0, h + i] = (q2[:, i * LANES:(i + 1) * LANES] + qf_ref[h + i:h + i + 1, :]).astype(BF16)
    cmp_in = _dot(hn, w_ref[:, CMP_OFF:CMP_OFF + 2 * LANES])
    kc_ref[...] = cmp_in[:, 0:LANES]
    vc_ref[...] = cmp_in[:, LANES:2 * LANES]
    keys = _dot(hn, w_ref[:, KEY_OFF:KEY_OFF + 2 * N_GROUPS * LANES])
    kf = kf_ref[...]
    for g in range(N_GROUPS):
        ks_ref[0, g] = (keys[:, g * LANES:(g + 1) * LANES] + kf).astype(BF16)
        kw_ref[0, g] = (keys[:, (N_GROUPS + g) * LANES:(N_GROUPS + g + 1) * LANES] + kf).astype(BF16)
    vals = _dot(hn, w_ref[:, VAL_OFF:VAL_OFF + 2 * LANES])
    for i, v_ref in enumerate((vs_ref, vw_ref)):
        _store_values_t(v_ref, vals[:, i * LANES:(i + 1) * LANES])
    gate_ref[...] = jax.nn.sigmoid(_dot(hn, w_ref[:, GATE_OFF:GATE_OFF + LANES]))
    u_ref[...] = _dot(hn, w_ref[:, POOL_OFF:POOL_OFF + POOL_COLS])
    gm_ref[...] = jax.nn.sigmoid(_dot(hn, w_ref[:, MERGE_OFF:MERGE_OFF + 2 * d]))


def _pack_w_in(w_in, d):
    w_in = w_in.astype(BF16)
    qw = N_HEADS * HEAD_DIM
    kvw = N_GROUPS * HEAD_DIM

    def padded(w, n):
        w = w.reshape(d, n, HEAD_DIM)
        return jnp.concatenate([w, jnp.zeros_like(w)], axis=-1).reshape(d, n * LANES)

    kc, vc, ks, vs, kw, vw = (w_in[:, qw + i * kvw:qw + (i + 1) * kvw] for i in range(6))
    off = qw + 6 * kvw
    n_g = 3 * N_HEADS
    w_g = jnp.pad(w_in[:, off:off + n_g], ((0, 0), (0, LANES - n_g)))
    w_rest = w_in[:, off + n_g:]
    return jnp.concatenate([padded(w_in[:, :qw], N_HEADS), kc, vc, padded(ks, N_GROUPS), padded(kw, N_GROUPS),
                            vs, vw, w_g, w_rest], axis=1)


def _proj(x, norm_g, w_packed, batch, seq, *, tm=512):
    n, d = x.shape
    tiles_per_seq = seq // tm
    row = lambda i: (i, 0)
    const = lambda i: (0, 0)
    per_group = lambda i: (i // tiles_per_seq, 0, i % tiles_per_seq, 0)
    qf = _query_features()
    kf = _key_features(np.arange(seq), LANES, HEAD_DIM)
    flat = lambda width, dtype: (jax.ShapeDtypeStruct((n, width), dtype), pl.BlockSpec((tm, width), row))
    grouped = lambda count: (jax.ShapeDtypeStruct((batch, count, seq, LANES), BF16),
                             pl.BlockSpec((1, count, tm, LANES), per_group))
    values_t = (jax.ShapeDtypeStruct((batch, N_GROUPS, V_ROWS, seq), BF16),
                pl.BlockSpec((1, N_GROUPS, V_ROWS, tm), lambda i: (i // tiles_per_seq, 0, 0, i % tiles_per_seq)))
    outs = [grouped(N_HEADS), flat(LANES, F32), flat(LANES, F32), grouped(N_GROUPS), grouped(N_GROUPS),
            values_t, values_t, flat(LANES, F32), flat(POOL_COLS, F32), flat(2 * d, F32)]
    return pl.pallas_call(
        _proj_kernel,
        out_shape=[o[0] for o in outs],
        grid=(n // tm,),
        in_specs=[pl.BlockSpec((tm, d), row), pl.BlockSpec((1, d), const),
                  pl.BlockSpec(w_packed.shape, const, pipeline_mode=pl.Buffered(1)),
                  pl.BlockSpec(qf.shape, const), pl.BlockSpec((tm, LANES), lambda i: (i % tiles_per_seq, 0))],
        out_specs=[o[1] for o in outs],
        compiler_params=_params("parallel"),
        name="proj",
    )(x, norm_g, w_packed, qf, kf)


def _compress_kernel(k_ref, v_ref, pos_ref, wk1_ref, wk2_ref, wv1_ref, wv2_ref, feat_ref, ko_ref, vo_ref):
    n_chunk = k_ref.shape[0] // CMP_STRIDE
    hidden = wk2_ref.shape[0]

    def compress(x_ref, w1_ref, w2_ref):
        first = second = None
        for r in range(CMP_STRIDE):
            rows = x_ref[pl.ds(r, n_chunk, stride=CMP_STRIDE), :]
            a = _dot((rows + pos_ref[r:r + 1, :]).astype(BF16), w1_ref[r])
            b = _dot((rows + pos_ref[CMP_STRIDE + r:CMP_STRIDE + r + 1, :]).astype(BF16), w1_ref[CMP_STRIDE + r])
            first = a if first is None else first + a
            second = b if second is None else second + b
        act = jax.nn.gelu(first + pltpu.roll(second, n_chunk - 1, 0)).astype(BF16)
        return [_dot(act[:, g * hidden:(g + 1) * hidden], w2_ref[...]) for g in range(N_GROUPS)]

    for g, k_cmp in enumerate(compress(k_ref, wk1_ref, wk2_ref)):
        ko_ref[0, g] = jnp.concatenate([k_cmp, feat_ref[...]], axis=-1).astype(BF16)
    _store_values_t(vo_ref, jnp.concatenate(compress(v_ref, wv1_ref, wv2_ref), axis=-1))


def _block_diag_w1(w1):
    w = w1.reshape(CMP_LEN, HEAD_DIM, -1)
    z = jnp.zeros_like(w)
    return jnp.concatenate([jnp.concatenate([w, z], axis=2), jnp.concatenate([z, w], axis=2)], axis=1).astype(BF16)


def _compress(kc, vc, pos, wk1, wk2, wv1, wv2, batch, seq):
    n_chunk = seq // CMP_STRIDE
    rows = pl.BlockSpec((seq, LANES), lambda b: (b, 0))
    full = lambda a: pl.BlockSpec(a.shape, lambda b: (0,) * a.ndim, pipeline_mode=pl.Buffered(1))
    pos2 = jnp.concatenate([pos] * N_GROUPS, axis=1)
    feat = _key_features(np.arange(n_chunk) * CMP_STRIDE + CMP_LEN - 1, HEAD_DIM, 0)
    args = (pos2, _block_diag_w1(wk1), wk2.astype(BF16), _block_diag_w1(wv1), wv2.astype(BF16), feat)
    return pl.pallas_call(
        _compress_kernel,
        out_shape=[jax.ShapeDtypeStruct((batch, N_GROUPS, n_chunk, LANES), BF16),
                   jax.ShapeDtypeStruct((batch, N_GROUPS, V_ROWS, n_chunk), BF16)],
        grid=(batch,),
        in_specs=[rows, rows] + [full(a) for a in args],
        out_specs=[pl.BlockSpec((1, N_GROUPS, n_chunk, LANES), lambda b: (b, 0, 0, 0)),
                   pl.BlockSpec((1, N_GROUPS, V_ROWS, n_chunk), lambda b: (b, 0, 0, 0))],
        compiler_params=_params("parallel"),
        name="compress",
    )(kc, vc, *args)


SLC_CHUNK = 256
BLOCKS_PER_CHUNK = SLC_CHUNK // SEL_LEN


def _nsa_kernel(q_ref, kc_ref, vct_ref, ks_ref, vst_ref, kw_ref, vwt_ref, gate_ref, ovt_ref, o_ref,
                st_ref, p_ref, gt_ref, selt_ref, list_ref, m_ref, acc_ref, out_ref, *, tq, seq):
    n_sel = seq // SEL_LEN
    n_chunks = seq // SLC_CHUNK
    n_cmp = kc_ref.shape[2]
    n_win = WINDOW + tq
    q0 = pl.program_id(1) * tq
    gt_ref[...] = gate_ref[...].T
    pairs = [(g, g * HEADS_PER_GROUP + 2 * j) for g in range(N_GROUPS) for j in range(HEADS_PER_GROUP // 2)]

    def gate_row(head, branch):
        c = 3 * head + branch
        return gt_ref[c:c + 1, :]

    def out_rows(head):
        g, hh = divmod(head, HEADS_PER_GROUP)
        return hh, slice(g * HEAD_DIM, (g + 1) * HEAD_DIM)

    def stage_scores(keys, n_keys):
        for g, head0 in pairs:
            q_pair = q_ref[0, head0:head0 + 2].reshape(2 * tq, LANES)
            scores = _dot_nt(keys(g), q_pair)
            for i in range(2):
                st_ref[head0 + i, 0:n_keys, :] = scores[:, i * tq:(i + 1) * tq]

    def finish_pair(head0, pv, branch, first, ok=None):
        invs = []
        for i in range(2):
            cols = slice(i * tq, (i + 1) * tq)
            inv = 1.0 / pv[HEAD_DIM:HEAD_DIM + 1, cols]
            if ok is not None:
                inv = jnp.where(ok, inv, 0.0)
            hh, rows = out_rows(head0 + i)
            contrib = pv[0:HEAD_DIM, cols] * (inv * gate_row(head0 + i, branch))
            out_ref[hh, rows, :] = contrib if first else out_ref[hh, rows, :] + contrib
            invs.append(inv)
        return invs

    t_c = q0 + lax.broadcasted_iota(jnp.int32, (n_cmp, tq), 1)
    end_c = lax.broadcasted_iota(jnp.int32, (n_cmp, tq), 0) * CMP_STRIDE + (CMP_LEN - 1)
    bias_c = jnp.where(t_c >= end_c, 0.0, NEG_BIG)
    has_cmp = (q0 + lax.broadcasted_iota(jnp.int32, (1, tq), 1)) >= CMP_LEN - 1
    inv_c = [None] * N_HEADS

    def cmp_pair(g, head0):
        for i in range(2):
            x = st_ref[head0 + i, 0:n_cmp, :] + bias_c
            p_ref[head0 // 2, 0:n_cmp, i * tq:(i + 1) * tq] = jnp.exp2(x - jnp.max(x, axis=0, keepdims=True)).astype(BF16)
        pv = _dot(vct_ref[0, g], p_ref[head0 // 2, 0:n_cmp, :])
        inv_c[head0], inv_c[head0 + 1] = finish_pair(head0, pv, 0, True, has_cmp)

    stage_scores(lambda g: kc_ref[0, g], n_cmp)
    for g, head0 in pairs:
        cmp_pair(g, head0)

    raw = [_dot(ovt_ref[...], p_ref[pair, 0:n_cmp, :]) for pair in range(N_HEADS // 2)]

    w0 = pl.multiple_of(jnp.maximum(q0 - WINDOW, 0), tq)
    stage_scores(lambda g: kw_ref[0, g, pl.ds(w0, n_win), :], n_win)

    blk = lax.broadcasted_iota(jnp.int32, (n_sel, tq), 0)
    cur = (q0 + lax.broadcasted_iota(jnp.int32, (n_sel, tq), 1)) // SEL_LEN
    valid = blk <= cur
    forced = valid & ((blk == 0) | (blk == cur) | (blk == cur - 1))
    blk_f = blk.astype(F32)
    score = []
    for g in range(N_GROUPS):
        total = None
        for head in range(g * HEADS_PER_GROUP, (g + 1) * HEADS_PER_GROUP):
            part = raw[head // 2][:, (head % 2) * tq:(head % 2 + 1) * tq] * inv_c[head]
            total = part if total is None else total + part
        score.append(jnp.where(forced, TOPK_BIG, jnp.where(valid, total, -TOPK_BIG)))

    for _ in range(min(N_SELECT, n_sel)):
        for g in range(N_GROUPS):
            best = jnp.max(score[g], axis=0, keepdims=True)
            pick = jnp.min(jnp.where(score[g] == best, blk_f, float(n_sel)), axis=0, keepdims=True)
            score[g] = jnp.where(blk_f == pick, TOPK_TAKEN, score[g])
    sel = [jnp.where(score[g] == TOPK_TAKEN, 1.0, 0.0) for g in range(N_GROUPS)]
    for g in range(N_GROUPS):
        selt_ref[g] = sel[g]
    sel_any = sel[0] + sel[1]
    n_causal = (q0 + tq + SLC_CHUNK - 1) // SLC_CHUNK
    n_items = jnp.int32(0)
    for c in range(n_chunks):
        any_sel = jnp.max(sel_any[c * BLOCKS_PER_CHUNK:(c + 1) * BLOCKS_PER_CHUNK, :])
        list_ref[n_items] = c
        n_items = n_items + ((any_sel > 0.5) & (c < n_causal)).astype(jnp.int32)

    dist_w = (q0 + lax.broadcasted_iota(jnp.int32, (n_win, tq), 1)) - (w0 + lax.broadcasted_iota(jnp.int32, (n_win, tq), 0))
    bias_w = jnp.where((dist_w >= 0) & (dist_w < WINDOW), 0.0, NEG_BIG)

    def win_pair(g, head0):
        for i in range(2):
            x = st_ref[head0 + i, 0:n_win, :] + bias_w
            p_ref[head0 // 2, 0:n_win, i * tq:(i + 1) * tq] = jnp.exp2(x - jnp.max(x, axis=0, keepdims=True)).astype(BF16)
        pv = _dot(vwt_ref[0, g, :, pl.ds(w0, n_win)], p_ref[head0 // 2, 0:n_win, :])
        finish_pair(head0, pv, 2, False)

    for g, head0 in pairs:
        win_pair(g, head0)

    m_ref[...] = jnp.full_like(m_ref, NEG_BIG)
    acc_ref[...] = jnp.zeros_like(acc_ref)
    base = (lax.broadcasted_iota(jnp.int32, (SLC_CHUNK, tq), 1)
            - lax.broadcasted_iota(jnp.int32, (SLC_CHUNK, tq), 0))

    def stage_chunk(item, row0):
        c = list_ref[item]
        k0 = pl.multiple_of(c * SLC_CHUNK, SLC_CHUNK)
        causal = base >= k0 - q0
        for g, head0 in pairs:
            if head0 % HEADS_PER_GROUP == 0:
                sel_keys = jnp.concatenate(
                    [jnp.broadcast_to(selt_ref[g, pl.ds(c * BLOCKS_PER_CHUNK + i, 1), :], (SEL_LEN, tq))
                     for i in range(BLOCKS_PER_CHUNK)], axis=0)
                bias = jnp.where(causal & (sel_keys > 0.5), 0.0, NEG_BIG)
                bias2 = jnp.concatenate([bias, bias], axis=1)
            q_pair = q_ref[0, head0:head0 + 2].reshape(2 * tq, LANES)
            scores = _dot_nt(ks_ref[0, g, pl.ds(k0, SLC_CHUNK), :], q_pair) + bias2
            for i in range(2):
                st_ref[head0 + i, row0:row0 + SLC_CHUNK, :] = scores[:, i * tq:(i + 1) * tq]

    def consume_chunk(item, row0):
        k0 = pl.multiple_of(list_ref[item] * SLC_CHUNK, SLC_CHUNK)
        for g, head0 in pairs:
            alphas = []
            for i, head in enumerate((head0, head0 + 1)):
                x = st_ref[head, row0:row0 + SLC_CHUNK, :]
                m_old = m_ref[head:head + 1, :]
                m_new = jnp.maximum(m_old, jnp.max(x, axis=0, keepdims=True))
                alphas.append(jnp.exp2(m_old - m_new))
                m_ref[head:head + 1, :] = m_new
                p_ref[head0 // 2, 0:SLC_CHUNK, i * tq:(i + 1) * tq] = jnp.exp2(x - m_new).astype(BF16)
            pv = _dot(vst_ref[0, g, :, pl.ds(k0, SLC_CHUNK)], p_ref[head0 // 2, 0:SLC_CHUNK, :])
            acc_ref[head0 // 2] = acc_ref[head0 // 2] * jnp.concatenate(alphas, axis=1) + pv

    last = n_items - 1
    stage_chunk(0, 0)

    def two_chunks(j, carry):
        stage_chunk(jnp.minimum(2 * j + 1, last), SLC_CHUNK)
        consume_chunk(2 * j, 0)

        @pl.when(2 * j + 1 < n_items)
        def _():
            stage_chunk(jnp.minimum(2 * j + 2, last), 0)
            consume_chunk(2 * j + 1, SLC_CHUNK)
        return carry

    lax.fori_loop(0, (n_items + 1) // 2, two_chunks, 0)
    for g, head0 in pairs:
        finish_pair(head0, acc_ref[head0 // 2], 1, False)

    for hh in range(HEADS_PER_GROUP):
        o_ref[:, hh * LANES:(hh + 1) * LANES] = out_ref[hh].T.astype(BF16)


def _overlap_t(seq):
    n_cmp = (seq - CMP_LEN) // CMP_STRIDE + 1
    n_sel = seq // SEL_LEN
    cs = np.arange(n_cmp)[:, None] * CMP_STRIDE
    ss = np.arange(n_sel)[None, :] * SEL_LEN
    ov = np.clip(np.minimum(cs + CMP_LEN, ss + SEL_LEN) - np.maximum(cs, ss), 0, None) / CMP_LEN
    out = np.zeros((n_sel, seq // CMP_STRIDE), np.float32)
    out[:, :n_cmp] = ov.T
    return jnp.asarray(out, dtype=BF16)


def _nsa(q, kc, vct, ks, vst, kw, vwt, gates, *, tq=128):
    batch, _, seq, _ = q.shape
    n_cmp = kc.shape[2]
    n_sel = seq // SEL_LEN
    assert seq % SLC_CHUNK == 0 and seq >= WINDOW + tq and SLC_CHUNK % tq == 0 and tq == LANES
    assert n_sel <= 2 * SEL_LEN
    ovt = _overlap_t(seq)
    keys = lambda n: pl.BlockSpec((1, N_GROUPS, n, LANES), lambda b, i: (b, 0, 0, 0))
    vals = lambda n: pl.BlockSpec((1, N_GROUPS, V_ROWS, n), lambda b, i: (b, 0, 0, 0))
    tiles = seq // tq
    s_rows = max(n_cmp, WINDOW + tq, SLC_CHUNK)
    return pl.pallas_call(
        functools.partial(_nsa_kernel, tq=tq, seq=seq),
        out_shape=jax.ShapeDtypeStruct((batch * seq, HEADS_PER_GROUP * LANES), BF16),
        grid=(batch, tiles),
        in_specs=[
            pl.BlockSpec((1, N_HEADS, tq, LANES), lambda b, i: (b, 0, i, 0)),
            keys(n_cmp), vals(n_cmp), keys(seq), vals(seq), keys(seq), vals(seq),
            pl.BlockSpec((tq, LANES), lambda b, i: (b * tiles + i, 0)),
            pl.BlockSpec(ovt.shape, lambda b, i: (0, 0)),
        ],
        out_specs=pl.BlockSpec((tq, HEADS_PER_GROUP * LANES), lambda b, i: (b * tiles + i, 0)),
        scratch_shapes=[
            pltpu.VMEM((N_HEADS, s_rows, tq), F32),
            pltpu.VMEM((N_HEADS // 2, s_rows, 2 * tq), BF16),
            pltpu.VMEM((LANES, tq), F32),
            pltpu.VMEM((N_GROUPS, n_sel, tq), F32),
            pltpu.SMEM((seq // SLC_CHUNK,), jnp.int32),
            pltpu.VMEM((N_HEADS, tq), F32),
            pltpu.VMEM((N_HEADS // 2, V_ROWS, 2 * tq), F32),
            pltpu.VMEM((HEADS_PER_GROUP, N_GROUPS * HEAD_DIM, tq), F32),
        ],
        compiler_params=_params("parallel", "arbitrary"),
        name="nsa",
    )(q, kc, vct, ks, vst, kw, vwt, gates, ovt)


HALO = max(POOL_WINDOWS)


def _merge_kernel(x_ref, u_ref, halo_ref, gm_ref, on_ref, pw_ref, ps_ref, wbp_ref, wbn_ref, wo_ref,
                  o_ref, ext_ref, *, tm, seq):
    d = x_ref.shape[1]
    pos0 = (pl.program_id(0) * tm) % seq
    ext_ref[0:HALO, :] = jnp.where(pos0 == 0, 0.0, halo_ref[...])
    ext_ref[HALO:HALO + tm, :] = u_ref[...]
    pos = (pos0 + lax.broadcasted_iota(jnp.int32, (tm, POOL_GROUP_DIM), 0)).astype(F32)
    mixed = []
    for gi, w in enumerate(POOL_WINDOWS):
        cols = slice(gi * POOL_GROUP_DIM, (gi + 1) * POOL_GROUP_DIM)
        u = ext_ref[HALO:HALO + tm, cols]
        total = u
        for lag in range(1, w):
            total = total + ext_ref[HALO - lag:HALO - lag + tm, cols]
        delta = total / jnp.minimum(pos + 1.0, float(w)) - u
        mixed.append(_dot(delta.astype(BF16), pw_ref[gi]) * ps_ref[:, cols])
    mixed = jnp.concatenate(mixed, axis=-1).astype(BF16)
    a = _dot(mixed, wbp_ref[...])
    b = _dot(on_ref[...], wbn_ref[...])
    merged = gm_ref[:, 0:d] * a + gm_ref[:, d:2 * d] * b
    o_ref[...] = x_ref[...] + _dot(merged.astype(BF16), wo_ref[...])


def _merge(x, u, gm, o_nsa, pool_w, pool_scale, w_bp, w_bn, w_out, seq, *, tm=512):
    n, d = x.shape
    row = lambda i: (i, 0)
    full = lambda a: pl.BlockSpec(a.shape, lambda i: (0,) * a.ndim, pipeline_mode=pl.Buffered(1))
    halo_blocks = tm // HALO
    return pl.pallas_call(
        functools.partial(_merge_kernel, tm=tm, seq=seq),
        out_shape=jax.ShapeDtypeStruct((n, d), F32),
        grid=(n // tm,),
        in_specs=[
            pl.BlockSpec((tm, d), row),
            pl.BlockSpec((tm, POOL_COLS), row),
            pl.BlockSpec((HALO, POOL_COLS), lambda i: (jnp.maximum(i * halo_blocks - 1, 0), 0)),
            pl.BlockSpec((tm, 2 * d), row),
            pl.BlockSpec((tm, d), row),
            full(pool_w), full(pool_scale), full(w_bp), full(w_bn), full(w_out),
        ],
        out_specs=pl.BlockSpec((tm, d), row),
        scratch_shapes=[pltpu.VMEM((HALO + tm, POOL_COLS), F32)],
        compiler_params=_params("parallel"),
        name="merge",
    )(x, u, u, gm, o_nsa, pool_w, pool_scale, w_bp, w_bn, w_out)


def kernel(x, ffn1_norm, ffn1_w_gate, ffn1_w_up, ffn1_w_down, mix_norm, w_in, cmp_pos, cmp_k_w1, cmp_k_w2, cmp_v_w1, cmp_v_w2, pool_w, pool_scale, w_branch_pool, w_branch_nsa, w_out, ffn2_norm, ffn2_w_gate, ffn2_w_up, ffn2_w_down, final_norm):
    batch, seq, d = x.shape
    depth = w_in.shape[0]
    xf = x.reshape(batch * seq, d)
    bf = lambda a: a.astype(BF16)
    row = lambda a: a.reshape(1, -1)
    for l in range(depth):
        xf = _ffn(xf, row(ffn1_norm[l]), bf(ffn1_w_gate[l]), bf(ffn1_w_up[l]), bf(ffn1_w_down[l]),
                  row(final_norm), final_norm=False)

        q, kc, vc, ks, kw, vs, vw, gates, u, gm = _proj(xf, row(mix_norm[l]), _pack_w_in(w_in[l], d), batch, seq)
        k_cmp, v_cmp = _compress(kc, vc, cmp_pos[l], cmp_k_w1[l], cmp_k_w2[l], cmp_v_w1[l], cmp_v_w2[l], batch, seq)
        o_nsa = _nsa(q, k_cmp, v_cmp, ks, vs, kw, vw, gates)

        w_bn = w_branch_nsa[l].reshape(N_GROUPS, HEADS_PER_GROUP, HEAD_DIM, d).transpose(1, 0, 2, 3).reshape(-1, d)
        xf = _merge(xf, u, gm, o_nsa, bf(pool_w[l]), row(pool_scale[l]), bf(w_branch_pool[l]), bf(w_bn),
                    bf(w_out[l]), seq)

        xf = _ffn(xf, row(ffn2_norm[l]), bf(ffn2_w_gate[l]), bf(ffn2_w_up[l]), bf(ffn2_w_down[l]),
                  row(final_norm), final_norm=(l == depth - 1))
    return xf.reshape(batch, seq, d)
```

```python
import functools

import jax
import jax.numpy as jnp
import numpy as np
from jax import lax
from jax.experimental import pallas as pl
from jax.experimental.pallas import tpu as pltpu

N_HEADS = 16
N_GROUPS = 2
HEADS_PER_GROUP = N_HEADS // N_GROUPS
HEAD_DIM = 64
CMP_LEN = 32
CMP_STRIDE = 16
SEL_LEN = 64
N_SELECT = 16
WINDOW = 512
POOL_WINDOWS = (2, 4, 8, 16)
POOL_GROUP_DIM = 128
RMS_EPS = 1e-6
ALIBI_MAX_BIAS = 8.0

LANES = 128
NEG_BIG = -1e30
TOPK_BIG = 1e30
TOPK_TAKEN = -2e30
VMEM_LIMIT = 48 * 1024 * 1024

BF16 = jnp.bfloat16
F32 = jnp.float32


def _dot(a, b):
    return jnp.dot(a, b, preferred_element_type=F32)


def _dot_nt(a, b):
    return lax.dot_general(a, b, (((1,), (1,)), ((), ())), preferred_element_type=F32)


def _rms(x, g):
    return x * lax.rsqrt(jnp.mean(x * x, axis=-1, keepdims=True) + RMS_EPS) * g


def _params(*sem):
    return pltpu.CompilerParams(dimension_semantics=sem, vmem_limit_bytes=VMEM_LIMIT)


def _ffn_kernel(x_ref, g_ref, wg_ref, wu_ref, wd_ref, fin_ref, o_ref, *, final_norm):
    x = x_ref[...]
    xn = _rms(x, g_ref[...]).astype(BF16)
    gate = _dot(xn, wg_ref[...])
    up = _dot(xn, wu_ref[...])
    act = (gate * jax.nn.sigmoid(gate)) * up
    y = x + 0.5 * _dot(act.astype(BF16), wd_ref[...])
    if final_norm:
        y = _rms(y, fin_ref[...])
    o_ref[...] = y


def _ffn(x, norm_g, wg, wu, wd, fin_g, *, final_norm, tm=512):
    n, d = x.shape
    row = lambda i: (i, 0)
    resident = lambda a: pl.BlockSpec(a.shape, lambda i: (0, 0), pipeline_mode=pl.Buffered(1))
    return pl.pallas_call(
        functools.partial(_ffn_kernel, final_norm=final_norm),
        out_shape=jax.ShapeDtypeStruct((n, d), F32),
        grid=(n // tm,),
        in_specs=[pl.BlockSpec((tm, d), row), resident(norm_g), resident(wg), resident(wu), resident(wd),
                  resident(fin_g)],
        out_specs=pl.BlockSpec((tm, d), row),
        compiler_params=_params("parallel"),
        name="ffn",
    )(x, norm_g, wg, wu, wd, fin_g)


LOG2E = 1.4426950408889634
N_FEAT = 6


def _bf16_pieces(x):
    x = np.asarray(x, np.float32)
    s1 = x.astype(BF16).astype(np.float32)
    s2 = (x - s1).astype(BF16).astype(np.float32)
    s3 = (x - s1 - s2).astype(BF16).astype(np.float32)
    return s1, s2, s3


def _query_features():
    slopes = np.float32(2.0) ** (-ALIBI_MAX_BIAS * np.arange(1, N_HEADS + 1, dtype=np.float32) / N_HEADS)
    s1, s2, s3 = _bf16_pieces(slopes * np.float32(LOG2E))
    feat = np.zeros((N_HEADS, LANES), np.float32)
    feat[:, HEAD_DIM:HEAD_DIM + N_FEAT] = np.stack([s1, s2, s3, SEL_LEN * s1, SEL_LEN * s2, SEL_LEN * s3], axis=1)
    return jnp.asarray(feat)


def _key_features(pos, width, offset):
    pos = np.asarray(pos)
    a, b = (pos // SEL_LEN).astype(np.float32), (pos % SEL_LEN).astype(np.float32)
    feat = np.zeros((len(pos), width), np.float32)
    feat[:, offset:offset + N_FEAT] = np.stack([b, b, b, a, a, a], axis=1)
    return jnp.asarray(feat)


Q_COLS = N_HEADS * LANES
CMP_OFF = Q_COLS
KEY_OFF = CMP_OFF + 2 * LANES
VAL_OFF = KEY_OFF + 2 * N_GROUPS * LANES
GATE_OFF = VAL_OFF + 2 * LANES
POOL_OFF = GATE_OFF + LANES
POOL_COLS = len(POOL_WINDOWS) * POOL_GROUP_DIM
MERGE_OFF = POOL_OFF + POOL_COLS


V_ROWS = HEAD_DIM + 16


def _store_values_t(v_ref, v):
    vt = v.T
    for g in range(N_GROUPS):
        v_ref[0, g, 0:HEAD_DIM, :] = vt[g * HEAD_DIM:(g + 1) * HEAD_DIM, :].astype(BF16)
        v_ref[0, g, HEAD_DIM:V_ROWS, :] = jnp.ones((V_ROWS - HEAD_DIM, v.shape[0]), BF16)


def _proj_kernel(x_ref, g_ref, w_ref, qf_ref, kf_ref, q_ref, kc_ref, vc_ref, ks_ref, kw_ref, vs_ref, vw_ref,
                 gate_ref, u_ref, gm_ref):
    d = x_ref.shape[1]
    hn = _rms(x_ref[...], g_ref[...]).astype(BF16)
    q_scale = HEAD_DIM ** -0.5 * LOG2E
    for h in range(0, N_HEADS, 2):
        q2 = _dot(hn, w_ref[:, h * LANES:(h + 2) * LANES]) * q_scale
        for i in range(2):
            q_ref[0, h + i] = (q2[:, i * LANES:(i + 1) * LANES] + qf_ref[h + i:h + i + 1, :]).astype(BF16)
    cmp_in = _dot(hn, w_ref[:, CMP_OFF:CMP_OFF + 2 * LANES])
    kc_ref[...] = cmp_in[:, 0:LANES]
    vc_ref[...] = cmp_in[:, LANES:2 * LANES]
    keys = _dot(hn, w_ref[:, KEY_OFF:KEY_OFF + 2 * N_GROUPS * LANES])
    kf = kf_ref[...]
    for g in range(N_GROUPS):
        ks_ref[0, g] = (keys[:, g * LANES:(g + 1) * LANES] + kf).astype(BF16)
        kw_ref[0, g] = (keys[:, (N_GROUPS + g) * LANES:(N_GROUPS + g + 1) * LANES] + kf).astype(BF16)
    vals = _dot(hn, w_ref[:, VAL_OFF:VAL_OFF + 2 * LANES])
    for i, v_ref in enumerate((vs_ref, vw_ref)):
        _store_values_t(v_ref, vals[:, i * LANES:(i + 1) * LANES])
    gate_ref[...] = jax.nn.sigmoid(_dot(hn, w_ref[:, GATE_OFF:GATE_OFF + LANES]))
    u_ref[...] = _dot(hn, w_ref[:, POOL_OFF:POOL_OFF + POOL_COLS])
    gm_ref[...] = jax.nn.sigmoid(_dot(hn, w_ref[:, MERGE_OFF:MERGE_OFF + 2 * d]))


def _pack_w_in(w_in, d):
    w_in = w_in.astype(BF16)
    qw = N_HEADS * HEAD_DIM
    kvw = N_GROUPS * HEAD_DIM

    def padded(w, n):
        w = w.reshape(d, n, HEAD_DIM)
        return jnp.concatenate([w, jnp.zeros_like(w)], axis=-1).reshape(d, n * LANES)

    kc, vc, ks, vs, kw, vw = (w_in[:, qw + i * kvw:qw + (i + 1) * kvw] for i in range(6))
    off = qw + 6 * kvw
    n_g = 3 * N_HEADS
    w_g = jnp.pad(w_in[:, off:off + n_g], ((0, 0), (0, LANES - n_g)))
    w_rest = w_in[:, off + n_g:]
    return jnp.concatenate([padded(w_in[:, :qw], N_HEADS), kc, vc, padded(ks, N_GROUPS), padded(kw, N_GROUPS),
                            vs, vw, w_g, w_rest], axis=1)


def _proj(x, norm_g, w_packed, batch, seq, *, tm=512):
    n, d = x.shape
    tiles_per_seq = seq // tm
    row = lambda i: (i, 0)
    const = lambda i: (0, 0)
    per_group = lambda i: (i // tiles_per_seq, 0, i % tiles_per_seq, 0)
    qf = _query_features()
    kf = _key_features(np.arange(seq), LANES, HEAD_DIM)
    flat = lambda width, dtype: (jax.ShapeDtypeStruct((n, width), dtype), pl.BlockSpec((tm, width), row))
    grouped = lambda count: (jax.ShapeDtypeStruct((batch, count, seq, LANES), BF16),
                             pl.BlockSpec((1, count, tm, LANES), per_group))
    values_t = (jax.ShapeDtypeStruct((batch, N_GROUPS, V_ROWS, seq), BF16),
                pl.BlockSpec((1, N_GROUPS, V_ROWS, tm), lambda i: (i // tiles_per_seq, 0, 0, i % tiles_per_seq)))
    outs = [grouped(N_HEADS), flat(LANES, F32), flat(LANES, F32), grouped(N_GROUPS), grouped(N_GROUPS),
            values_t, values_t, flat(LANES, F32), flat(POOL_COLS, F32), flat(2 * d, F32)]
    return pl.pallas_call(
        _proj_kernel,
        out_shape=[o[0] for o in outs],
        grid=(n // tm,),
        in_specs=[pl.BlockSpec((tm, d), row), pl.BlockSpec((1, d), const),
                  pl.BlockSpec(w_packed.shape, const, pipeline_mode=pl.Buffered(1)),
                  pl.BlockSpec(qf.shape, const), pl.BlockSpec((tm, LANES), lambda i: (i % tiles_per_seq, 0))],
        out_specs=[o[1] for o in outs],
        compiler_params=_params("parallel"),
        name="proj",
    )(x, norm_g, w_packed, qf, kf)


def _compress_kernel(k_ref, v_ref, pos_ref, wk1_ref, wk2_ref, wv1_ref, wv2_ref, feat_ref, ko_ref, vo_ref):
    n_chunk = k_ref.shape[0] // CMP_STRIDE
    hidden = wk2_ref.shape[0]

    def compress(x_ref, w1_ref, w2_ref):
        first = second = None
        for r in range(CMP_STRIDE):
            rows = x_ref[pl.ds(r, n_chunk, stride=CMP_STRIDE), :]
            a = _dot((rows + pos_ref[r:r + 1, :]).astype(BF16), w1_ref[r])
            b = _dot((rows + pos_ref[CMP_STRIDE + r:CMP_STRIDE + r + 1, :]).astype(BF16), w1_ref[CMP_STRIDE + r])
            first = a if first is None else first + a
            second = b if second is None else second + b
        act = jax.nn.gelu(first + pltpu.roll(second, n_chunk - 1, 0)).astype(BF16)
        return [_dot(act[:, g * hidden:(g + 1) * hidden], w2_ref[...]) for g in range(N_GROUPS)]

    for g, k_cmp in enumerate(compress(k_ref, wk1_ref, wk2_ref)):
        ko_ref[0, g] = jnp.concatenate([k_cmp, feat_ref[...]], axis=-1).astype(BF16)
    _store_values_t(vo_ref, jnp.concatenate(compress(v_ref, wv1_ref, wv2_ref), axis=-1))


def _block_diag_w1(w1):
    w = w1.reshape(CMP_LEN, HEAD_DIM, -1)
    z = jnp.zeros_like(w)
    return jnp.concatenate([jnp.concatenate([w, z], axis=2), jnp.concatenate([z, w], axis=2)], axis=1).astype(BF16)


def _compress(kc, vc, pos, wk1, wk2, wv1, wv2, batch, seq):
    n_chunk = seq // CMP_STRIDE
    rows = pl.BlockSpec((seq, LANES), lambda b: (b, 0))
    full = lambda a: pl.BlockSpec(a.shape, lambda b: (0,) * a.ndim, pipeline_mode=pl.Buffered(1))
    pos2 = jnp.concatenate([pos] * N_GROUPS, axis=1)
    feat = _key_features(np.arange(n_chunk) * CMP_STRIDE + CMP_LEN - 1, HEAD_DIM, 0)
    args = (pos2, _block_diag_w1(wk1), wk2.astype(BF16), _block_diag_w1(wv1), wv2.astype(BF16), feat)
    return pl.pallas_call(
        _compress_kernel,
        out_shape=[jax.ShapeDtypeStruct((batch, N_GROUPS, n_chunk, LANES), BF16),
                   jax.ShapeDtypeStruct((batch, N_GROUPS, V_ROWS, n_chunk), BF16)],
        grid=(batch,),
        in_specs=[rows, rows] + [full(a) for a in args],
        out_specs=[pl.BlockSpec((1, N_GROUPS, n_chunk, LANES), lambda b: (b, 0, 0, 0)),
                   pl.BlockSpec((1, N_GROUPS, V_ROWS, n_chunk), lambda b: (b, 0, 0, 0))],
        compiler_params=_params("parallel"),
        name="compress",
    )(kc, vc, *args)


SLC_CHUNK = 256
BLOCKS_PER_CHUNK = SLC_CHUNK // SEL_LEN
HEAD_BLOCKS = 2


def _nsa_kernel(q_ref, kc_ref, vct_ref, ks_ref, vst_ref, kw_ref, vwt_ref, gate_ref, ovt_ref, o_ref,
                st_ref, p_ref, gt_ref, selt_ref, list_ref, m_ref, acc_ref, out_ref, *, tq, seq):
    n_sel = seq // SEL_LEN
    n_chunks = seq // SLC_CHUNK
    n_cmp = kc_ref.shape[2]
    n_win = WINDOW + tq
    q0 = pl.program_id(1) * tq
    gt_ref[...] = gate_ref[...].T
    pairs = [(g, g * HEADS_PER_GROUP + 2 * j) for g in range(N_GROUPS) for j in range(HEADS_PER_GROUP // 2)]

    def gate_row(head, branch):
        c = 3 * head + branch
        return gt_ref[c:c + 1, :]

    def out_rows(head):
        g, hh = divmod(head, HEADS_PER_GROUP)
        return hh, slice(g * HEAD_DIM, (g + 1) * HEAD_DIM)

    def stage_scores(keys, n_keys, row0=0):
        for g, head0 in pairs:
            q_pair = q_ref[0, head0:head0 + 2].reshape(2 * tq, LANES)
            scores = _dot_nt(keys(g), q_pair)
            for i in range(2):
                st_ref[head0 + i, row0:row0 + n_keys, :] = scores[:, i * tq:(i + 1) * tq]

    def finish_pair(head0, pv, branch, first, ok=None):
        invs = []
        for i in range(2):
            cols = slice(i * tq, (i + 1) * tq)
            inv = 1.0 / pv[HEAD_DIM:HEAD_DIM + 1, cols]
            if ok is not None:
                inv = jnp.where(ok, inv, 0.0)
            hh, rows = out_rows(head0 + i)
            contrib = pv[0:HEAD_DIM, cols] * (inv * gate_row(head0 + i, branch))
            out_ref[hh, rows, :] = contrib if first else out_ref[hh, rows, :] + contrib
            invs.append(inv)
        return invs

    t_c = q0 + lax.broadcasted_iota(jnp.int32, (n_cmp, tq), 1)
    end_c = lax.broadcasted_iota(jnp.int32, (n_cmp, tq), 0) * CMP_STRIDE + (CMP_LEN - 1)
    bias_c = jnp.where(t_c >= end_c, 0.0, NEG_BIG)
    has_cmp = (q0 + lax.broadcasted_iota(jnp.int32, (1, tq), 1)) >= CMP_LEN - 1
    inv_c = [None] * N_HEADS

    def cmp_pair(g, head0):
        for i in range(2):
            x = st_ref[head0 + i, 0:n_cmp, :] + bias_c
            p_ref[head0 // 2, 0:n_cmp, i * tq:(i + 1) * tq] = jnp.exp2(x - jnp.max(x, axis=0, keepdims=True)).astype(BF16)
        pv = _dot(vct_ref[0, g], p_ref[head0 // 2, 0:n_cmp, :])
        inv_c[head0], inv_c[head0 + 1] = finish_pair(head0, pv, 0, True, has_cmp)

    stage_scores(lambda g: kc_ref[0, g], n_cmp)
    for g, head0 in pairs:
        cmp_pair(g, head0)

    raw = [_dot(ovt_ref[...], p_ref[pair, 0:n_cmp, :]) for pair in range(N_HEADS // 2)]

    w0 = pl.multiple_of(jnp.maximum(q0 - WINDOW, 0), tq)
    stage_scores(lambda g: kw_ref[0, g, pl.ds(w0, n_win), :], n_win)
    head_keys = HEAD_BLOCKS * SEL_LEN
    stage_scores(lambda g: ks_ref[0, g, 0:head_keys, :], head_keys, n_win)

    blk = lax.broadcasted_iota(jnp.int32, (n_sel, tq), 0)
    cur = (q0 + lax.broadcasted_iota(jnp.int32, (n_sel, tq), 1)) // SEL_LEN
    valid = blk <= cur
    forced = valid & ((blk == 0) | (blk == cur) | (blk == cur - 1))
    blk_f = blk.astype(F32)
    score = []
    for g in range(N_GROUPS):
        total = None
        for head in range(g * HEADS_PER_GROUP, (g + 1) * HEADS_PER_GROUP):
            part = raw[head // 2][:, (head % 2) * tq:(head % 2 + 1) * tq] * inv_c[head]
            total = part if total is None else total + part
        score.append(jnp.where(forced, TOPK_BIG, jnp.where(valid, total, -TOPK_BIG)))

    for _ in range(min(N_SELECT, n_sel)):
        for g in range(N_GROUPS):
            best = jnp.max(score[g], axis=0, keepdims=True)
            pick = jnp.min(jnp.where(score[g] == best, blk_f, float(n_sel)), axis=0, keepdims=True)
            score[g] = jnp.where(blk_f == pick, TOPK_TAKEN, score[g])
    sel = [jnp.where(valid & (score[g] == TOPK_TAKEN), 1.0, 0.0) for g in range(N_GROUPS)]
    for g in range(N_GROUPS):
        selt_ref[g] = jnp.where(blk >= HEAD_BLOCKS, sel[g], 0.0)
    sel_any = jnp.where(blk >= HEAD_BLOCKS, sel[0] + sel[1], 0.0)
    n_items = jnp.int32(0)
    for c in range(n_chunks):
        any_sel = jnp.max(sel_any[c * BLOCKS_PER_CHUNK:(c + 1) * BLOCKS_PER_CHUNK, :])
        list_ref[n_items] = c
        n_items = n_items + (any_sel > 0.5).astype(jnp.int32)

    causal_h = (q0 + lax.broadcasted_iota(jnp.int32, (head_keys, tq), 1)) >= lax.broadcasted_iota(jnp.int32, (head_keys, tq), 0)
    head_rows = slice(n_win, n_win + head_keys)
    for g, head0 in pairs:
        if head0 % HEADS_PER_GROUP == 0:
            sel_keys = jnp.concatenate([jnp.broadcast_to(sel[g][i:i + 1, :], (SEL_LEN, tq)) for i in range(HEAD_BLOCKS)], axis=0)
            bias = jnp.where(causal_h & (sel_keys > 0.5), 0.0, NEG_BIG)
        for i in range(2):
            x = st_ref[head0 + i, head_rows, :] + bias
            m_new = jnp.max(x, axis=0, keepdims=True)
            m_ref[head0 + i:head0 + i + 1, :] = m_new
            p_ref[head0 // 2, head_rows, i * tq:(i + 1) * tq] = jnp.exp2(x - m_new).astype(BF16)
    for g, head0 in pairs:
        acc_ref[head0 // 2] = _dot(vst_ref[0, g, :, 0:head_keys], p_ref[head0 // 2, head_rows, :])

    dist_w = (q0 + lax.broadcasted_iota(jnp.int32, (n_win, tq), 1)) - (w0 + lax.broadcasted_iota(jnp.int32, (n_win, tq), 0))
    bias_w = jnp.where((dist_w >= 0) & (dist_w < WINDOW), 0.0, NEG_BIG)

    def win_pair(g, head0):
        for i in range(2):
            x = st_ref[head0 + i, 0:n_win, :] + bias_w
            p_ref[head0 // 2, 0:n_win, i * tq:(i + 1) * tq] = jnp.exp2(x - jnp.max(x, axis=0, keepdims=True)).astype(BF16)
        pv = _dot(vwt_ref[0, g, :, pl.ds(w0, n_win)], p_ref[head0 // 2, 0:n_win, :])
        finish_pair(head0, pv, 2, False)

    for g, head0 in pairs:
        win_pair(g, head0)

    base = (lax.broadcasted_iota(jnp.int32, (SLC_CHUNK, tq), 1)
            - lax.broadcasted_iota(jnp.int32, (SLC_CHUNK, tq), 0))

    def stage_chunk(item, row0):
        c = list_ref[item]
        k0 = pl.multiple_of(c * SLC_CHUNK, SLC_CHUNK)
        causal = base >= k0 - q0
        for g, head0 in pairs:
            if head0 % HEADS_PER_GROUP == 0:
                sel_keys = jnp.concatenate(
                    [jnp.broadcast_to(selt_ref[g, pl.ds(c * BLOCKS_PER_CHUNK + i, 1), :], (SEL_LEN, tq))
                     for i in range(BLOCKS_PER_CHUNK)], axis=0)
                bias = jnp.where(causal & (sel_keys > 0.5), 0.0, NEG_BIG)
                bias2 = jnp.concatenate([bias, bias], axis=1)
            q_pair = q_ref[0, head0:head0 + 2].reshape(2 * tq, LANES)
            scores = _dot_nt(ks_ref[0, g, pl.ds(k0, SLC_CHUNK), :], q_pair) + bias2
            for i in range(2):
                st_ref[head0 + i, row0:row0 + SLC_CHUNK, :] = scores[:, i * tq:(i + 1) * tq]

    def consume_chunk(item, row0):
        k0 = pl.multiple_of(list_ref[item] * SLC_CHUNK, SLC_CHUNK)
        for g, head0 in pairs:
            alphas = []
            for i, head in enumerate((head0, head0 + 1)):
                x = st_ref[head, row0:row0 + SLC_CHUNK, :]
                m_old = m_ref[head:head + 1, :]
                m_new = jnp.maximum(m_old, jnp.max(x, axis=0, keepdims=True))
                alphas.append(jnp.exp2(m_old - m_new))
                m_ref[head:head + 1, :] = m_new
                p_ref[head0 // 2, 0:SLC_CHUNK, i * tq:(i + 1) * tq] = jnp.exp2(x - m_new).astype(BF16)
            pv = _dot(vst_ref[0, g, :, pl.ds(k0, SLC_CHUNK)], p_ref[head0 // 2, 0:SLC_CHUNK, :])
            acc_ref[head0 // 2] = acc_ref[head0 // 2] * jnp.concatenate(alphas, axis=1) + pv

    last = jnp.maximum(n_items - 1, 0)
    stage_chunk(0, 0)

    def two_chunks(j, carry):
        stage_chunk(jnp.minimum(2 * j + 1, last), SLC_CHUNK)
        consume_chunk(2 * j, 0)

        @pl.when(2 * j + 1 < n_items)
        def _():
            stage_chunk(jnp.minimum(2 * j + 2, last), 0)
            consume_chunk(2 * j + 1, SLC_CHUNK)
        return carry

    lax.fori_loop(0, (n_items + 1) // 2, two_chunks, 0)
    for g, head0 in pairs:
        finish_pair(head0, acc_ref[head0 // 2], 1, False)

    for hh in range(HEADS_PER_GROUP):
        o_ref[:, hh * LANES:(hh + 1) * LANES] = out_ref[hh].T.astype(BF16)


def _overlap_t(seq):
    n_cmp = (seq - CMP_LEN) // CMP_STRIDE + 1
    n_sel = seq // SEL_LEN
    cs = np.arange(n_cmp)[:, None] * CMP_STRIDE
    ss = np.arange(n_sel)[None, :] * SEL_LEN
    ov = np.clip(np.minimum(cs + CMP_LEN, ss + SEL_LEN) - np.maximum(cs, ss), 0, None) / CMP_LEN
    out = np.zeros((n_sel, seq // CMP_STRIDE), np.float32)
    out[:, :n_cmp] = ov.T
    return jnp.asarray(out, dtype=BF16)


def _nsa(q, kc, vct, ks, vst, kw, vwt, gates, *, tq=128):
    batch, _, seq, _ = q.shape
    n_cmp = kc.shape[2]
    n_sel = seq // SEL_LEN
    assert seq % SLC_CHUNK == 0 and seq >= WINDOW + tq and SLC_CHUNK % tq == 0 and tq == LANES
    assert n_sel <= 2 * SEL_LEN
    ovt = _overlap_t(seq)
    keys = lambda n: pl.BlockSpec((1, N_GROUPS, n, LANES), lambda b, i: (b, 0, 0, 0))
    vals = lambda n: pl.BlockSpec((1, N_GROUPS, V_ROWS, n), lambda b, i: (b, 0, 0, 0))
    tiles = seq // tq
    s_rows = max(n_cmp, WINDOW + tq + HEAD_BLOCKS * SEL_LEN, 2 * SLC_CHUNK)
    return pl.pallas_call(
        functools.partial(_nsa_kernel, tq=tq, seq=seq),
        out_shape=jax.ShapeDtypeStruct((batch * seq, HEADS_PER_GROUP * LANES), BF16),
        grid=(batch, tiles),
        in_specs=[
            pl.BlockSpec((1, N_HEADS, tq, LANES), lambda b, i: (b, 0, i, 0)),
            keys(n_cmp), vals(n_cmp), keys(seq), vals(seq), keys(seq), vals(seq),
            pl.BlockSpec((tq, LANES), lambda b, i: (b * tiles + i, 0)),
            pl.BlockSpec(ovt.shape, lambda b, i: (0, 0)),
        ],
        out_specs=pl.BlockSpec((tq, HEADS_PER_GROUP * LANES), lambda b, i: (b * tiles + i, 0)),
        scratch_shapes=[
            pltpu.VMEM((N_HEADS, s_rows, tq), F32),
            pltpu.VMEM((N_HEADS // 2, s_rows, 2 * tq), BF16),
            pltpu.VMEM((LANES, tq), F32),
            pltpu.VMEM((N_GROUPS, n_sel, tq), F32),
            pltpu.SMEM((seq // SLC_CHUNK,), jnp.int32),
            pltpu.VMEM((N_HEADS, tq), F32),
            pltpu.VMEM((N_HEADS // 2, V_ROWS, 2 * tq), F32),
            pltpu.VMEM((HEADS_PER_GROUP, N_GROUPS * HEAD_DIM, tq), F32),
        ],
        compiler_params=_params("parallel", "arbitrary"),
        name="nsa",
    )(q, kc, vct, ks, vst, kw, vwt, gates, ovt)


HALO = max(POOL_WINDOWS)


def _merge_kernel(x_ref, u_ref, halo_ref, gm_ref, on_ref, pw_ref, ps_ref, wbp_ref, wbn_ref, wo_ref,
                  o_ref, ext_ref, *, tm, seq):
    d = x_ref.shape[1]
    pos0 = (pl.program_id(0) * tm) % seq
    ext_ref[0:HALO, :] = jnp.where(pos0 == 0, 0.0, halo_ref[...])
    ext_ref[HALO:HALO + tm, :] = u_ref[...]
    pos = (pos0 + lax.broadcasted_iota(jnp.int32, (tm, POOL_GROUP_DIM), 0)).astype(F32)
    mixed = []
    for gi, w in enumerate(POOL_WINDOWS):
        cols = slice(gi * POOL_GROUP_DIM, (gi + 1) * POOL_GROUP_DIM)
        u = ext_ref[HALO:HALO + tm, cols]
        total = u
        for lag in range(1, w):
            total = total + ext_ref[HALO - lag:HALO - lag + tm, cols]
        delta = total / jnp.minimum(pos + 1.0, float(w)) - u
        mixed.append(_dot(delta.astype(BF16), pw_ref[gi]) * ps_ref[:, cols])
    mixed = jnp.concatenate(mixed, axis=-1).astype(BF16)
    a = _dot(mixed, wbp_ref[...])
    b = _dot(on_ref[...], wbn_ref[...])
    merged = gm_ref[:, 0:d] * a + gm_ref[:, d:2 * d] * b
    o_ref[...] = x_ref[...] + _dot(merged.astype(BF16), wo_ref[...])


def _merge(x, u, gm, o_nsa, pool_w, pool_scale, w_bp, w_bn, w_out, seq, *, tm=512):
    n, d = x.shape
    row = lambda i: (i, 0)
    full = lambda a: pl.BlockSpec(a.shape, lambda i: (0,) * a.ndim, pipeline_mode=pl.Buffered(1))
    halo_blocks = tm // HALO
    return pl.pallas_call(
        functools.partial(_merge_kernel, tm=tm, seq=seq),
        out_shape=jax.ShapeDtypeStruct((n, d), F32),
        grid=(n // tm,),
        in_specs=[
            pl.BlockSpec((tm, d), row),
            pl.BlockSpec((tm, POOL_COLS), row),
            pl.BlockSpec((HALO, POOL_COLS), lambda i: (jnp.maximum(i * halo_blocks - 1, 0), 0)),
            pl.BlockSpec((tm, 2 * d), row),
            pl.BlockSpec((tm, d), row),
            full(pool_w), full(pool_scale), full(w_bp), full(w_bn), full(w_out),
        ],
        out_specs=pl.BlockSpec((tm, d), row),
        scratch_shapes=[pltpu.VMEM((HALO + tm, POOL_COLS), F32)],
        compiler_params=_params("parallel"),
        name="merge",
    )(x, u, u, gm, o_nsa, pool_w, pool_scale, w_bp, w_bn, w_out)


def kernel(x, ffn1_norm, ffn1_w_gate, ffn1_w_up, ffn1_w_down, mix_norm, w_in, cmp_pos, cmp_k_w1, cmp_k_w2, cmp_v_w1, cmp_v_w2, pool_w, pool_scale, w_branch_pool, w_branch_nsa, w_out, ffn2_norm, ffn2_w_gate, ffn2_w_up, ffn2_w_down, final_norm):
    batch, seq, d = x.shape
    depth = w_in.shape[0]
    xf = x.reshape(batch * seq, d)
    bf = lambda a: a.astype(BF16)
    row = lambda a: a.reshape(1, -1)
    for l in range(depth):
        xf = _ffn(xf, row(ffn1_norm[l]), bf(ffn1_w_gate[l]), bf(ffn1_w_up[l]), bf(ffn1_w_down[l]),
                  row(final_norm), final_norm=False)

        q, kc, vc, ks, kw, vs, vw, gates, u, gm = _proj(xf, row(mix_norm[l]), _pack_w_in(w_in[l], d), batch, seq)
        k_cmp, v_cmp = _compress(kc, vc, cmp_pos[l], cmp_k_w1[l], cmp_k_w2[l], cmp_v_w1[l], cmp_v_w2[l], batch, seq)
        o_nsa = _nsa(q, k_cmp, v_cmp, ks, vs, kw, vw, gates)

        w_bn = w_branch_nsa[l].reshape(N_GROUPS, HEADS_PER_GROUP, HEAD_DIM, d).transpose(1, 0, 2, 3).reshape(-1, d)
        xf = _merge(xf, u, gm, o_nsa, bf(pool_w[l]), row(pool_scale[l]), bf(w_branch_pool[l]), bf(w_bn),
                    bf(w_out[l]), seq)

        xf = _ffn(xf, row(ffn2_norm[l]), bf(ffn2_w_gate[l]), bf(ffn2_w_up[l]), bf(ffn2_w_down[l]),
                  row(final_norm), final_norm=(l == depth - 1))
    return xf.reshape(batch, seq, d)
```

```python
import functools

import jax
import jax.numpy as jnp
import numpy as np
from jax import lax
from jax.experimental import pallas as pl
from jax.experimental.pallas import tpu as pltpu

N_HEADS = 16
N_GROUPS = 2
HEADS_PER_GROUP = N_HEADS // N_GROUPS
HEAD_DIM = 64
CMP_LEN = 32
CMP_STRIDE = 16
SEL_LEN = 64
N_SELECT = 16
WINDOW = 512
POOL_WINDOWS = (2, 4, 8, 16)
POOL_GROUP_DIM = 128
RMS_EPS = 1e-6
ALIBI_MAX_BIAS = 8.0

LANES = 128
NEG_BIG = -1e30
TOPK_BIG = 1e30
TOPK_TAKEN = -2e30
VMEM_LIMIT = 48 * 1024 * 1024

BF16 = jnp.bfloat16
F32 = jnp.float32


def _dot(a, b):
    return jnp.dot(a, b, preferred_element_type=F32)


def _dot_nt(a, b):
    return lax.dot_general(a, b, (((1,), (1,)), ((), ())), preferred_element_type=F32)


def _rms(x, g):
    return x * lax.rsqrt(jnp.mean(x * x, axis=-1, keepdims=True) + RMS_EPS) * g


def _params(*sem):
    return pltpu.CompilerParams(dimension_semantics=sem, vmem_limit_bytes=VMEM_LIMIT)


def _ffn_kernel(x_ref, g_ref, wg_ref, wu_ref, wd_ref, fin_ref, o_ref, *, final_norm):
    x = x_ref[...]
    xn = _rms(x, g_ref[...]).astype(BF16)
    gate = _dot(xn, wg_ref[...])
    up = _dot(xn, wu_ref[...])
    act = (gate * jax.nn.sigmoid(gate)) * up
    y = x + 0.5 * _dot(act.astype(BF16), wd_ref[...])
    if final_norm:
        y = _rms(y, fin_ref[...])
    o_ref[...] = y


def _ffn(x, norm_g, wg, wu, wd, fin_g, *, final_norm, tm=512):
    n, d = x.shape
    row = lambda i: (i, 0)
    resident = lambda a: pl.BlockSpec(a.shape, lambda i: (0, 0), pipeline_mode=pl.Buffered(1))
    return pl.pallas_call(
        functools.partial(_ffn_kernel, final_norm=final_norm),
        out_shape=jax.ShapeDtypeStruct((n, d), F32),
        grid=(n // tm,),
        in_specs=[pl.BlockSpec((tm, d), row), resident(norm_g), resident(wg), resident(wu), resident(wd),
                  resident(fin_g)],
        out_specs=pl.BlockSpec((tm, d), row),
        compiler_params=_params("parallel"),
        name="ffn",
    )(x, norm_g, wg, wu, wd, fin_g)


LOG2E = 1.4426950408889634
N_FEAT = 6


def _bf16_pieces(x):
    x = np.asarray(x, np.float32)
    s1 = x.astype(BF16).astype(np.float32)
    s2 = (x - s1).astype(BF16).astype(np.float32)
    s3 = (x - s1 - s2).astype(BF16).astype(np.float32)
    return s1, s2, s3


def _query_features():
    slopes = np.float32(2.0) ** (-ALIBI_MAX_BIAS * np.arange(1, N_HEADS + 1, dtype=np.float32) / N_HEADS)
    s1, s2, s3 = _bf16_pieces(slopes * np.float32(LOG2E))
    feat = np.zeros((N_HEADS, LANES), np.float32)
    feat[:, HEAD_DIM:HEAD_DIM + N_FEAT] = np.stack([s1, s2, s3, SEL_LEN * s1, SEL_LEN * s2, SEL_LEN * s3], axis=1)
    return jnp.asarray(feat)


def _key_features(pos, width, offset):
    pos = np.asarray(pos)
    a, b = (pos // SEL_LEN).astype(np.float32), (pos % SEL_LEN).astype(np.float32)
    feat = np.zeros((len(pos), width), np.float32)
    feat[:, offset:offset + N_FEAT] = np.stack([b, b, b, a, a, a], axis=1)
    return jnp.asarray(feat)


Q_COLS = N_HEADS * LANES
CMP_OFF = Q_COLS
KEY_OFF = CMP_OFF + 2 * LANES
VAL_OFF = KEY_OFF + 2 * N_GROUPS * LANES
GATE_OFF = VAL_OFF + 2 * LANES
POOL_OFF = GATE_OFF + LANES
POOL_COLS = len(POOL_WINDOWS) * POOL_GROUP_DIM
MERGE_OFF = POOL_OFF + POOL_COLS


V_ROWS = HEAD_DIM + 16


def _store_values_t(v_ref, v):
    vt = v.T
    for g in range(N_GROUPS):
        v_ref[0, g, 0:HEAD_DIM, :] = vt[g * HEAD_DIM:(g + 1) * HEAD_DIM, :].astype(BF16)
        v_ref[0, g, HEAD_DIM:V_ROWS, :] = jnp.ones((V_ROWS - HEAD_DIM, v.shape[0]), BF16)


def _proj_kernel(x_ref, g_ref, w_ref, qf_ref, kf_ref, q_ref, kc_ref, vc_ref, ks_ref, kw_ref, vs_ref, vw_ref,
                 gate_ref, u_ref, gm_ref):
    d = x_ref.shape[1]
    hn = _rms(x_ref[...], g_ref[...]).astype(BF16)
    q_scale = HEAD_DIM ** -0.5 * LOG2E
    for h in range(0, N_HEADS, 2):
        q2 = _dot(hn, w_ref[:, h * LANES:(h + 2) * LANES]) * q_scale
        for i in range(2):
            q_ref[0, h + i] = (q2[:, i * LANES:(i + 1) * LANES] + qf_ref[h + i:h + i + 1, :]).astype(BF16)
    cmp_in = _dot(hn, w_ref[:, CMP_OFF:CMP_OFF + 2 * LANES])
    kc_ref[...] = cmp_in[:, 0:LANES]
    vc_ref[...] = cmp_in[:, LANES:2 * LANES]
    keys = _dot(hn, w_ref[:, KEY_OFF:KEY_OFF + 2 * N_GROUPS * LANES])
    kf = kf_ref[...]
    for g in range(N_GROUPS):
        ks_ref[0, g] = (keys[:, g * LANES:(g + 1) * LANES] + kf).astype(BF16)
        kw_ref[0, g] = (keys[:, (N_GROUPS + g) * LANES:(N_GROUPS + g + 1) * LANES] + kf).astype(BF16)
    vals = _dot(hn, w_ref[:, VAL_OFF:VAL_OFF + 2 * LANES])
    for i, v_ref in enumerate((vs_ref, vw_ref)):
        _store_values_t(v_ref, vals[:, i * LANES:(i + 1) * LANES])
    gate_ref[...] = jax.nn.sigmoid(_dot(hn, w_ref[:, GATE_OFF:GATE_OFF + LANES]))
    u_ref[...] = _dot(hn, w_ref[:, POOL_OFF:POOL_OFF + POOL_COLS])
    gm_ref[...] = jax.nn.sigmoid(_dot(hn, w_ref[:, MERGE_OFF:MERGE_OFF + 2 * d]))


def _pack_w_in(w_in, d):
    qw = N_HEADS * HEAD_DIM
    kvw = N_GROUPS * HEAD_DIM

    def padded(w, n):
        w = w.reshape(d, n, HEAD_DIM)
        return jnp.concatenate([w, jnp.zeros_like(w)], axis=-1).reshape(d, n * LANES)

    kc, vc, ks, vs, kw, vw = (w_in[:, qw + i * kvw:qw + (i + 1) * kvw] for i in range(6))
    off = qw + 6 * kvw
    n_g = 3 * N_HEADS
    w_g = jnp.pad(w_in[:, off:off + n_g], ((0, 0), (0, LANES - n_g)))
    w_rest = w_in[:, off + n_g:]
    return jnp.concatenate([padded(w_in[:, :qw], N_HEADS), kc, vc, padded(ks, N_GROUPS), padded(kw, N_GROUPS),
                            vs, vw, w_g, w_rest], axis=1).astype(BF16)


def _proj(x, norm_g, w_packed, batch, seq, *, tm=512):
    n, d = x.shape
    tiles_per_seq = seq // tm
    row = lambda i: (i, 0)
    const = lambda i: (0, 0)
    per_group = lambda i: (i // tiles_per_seq, 0, i % tiles_per_seq, 0)
    qf = _query_features()
    kf = _key_features(np.arange(seq), LANES, HEAD_DIM)
    flat = lambda width, dtype: (jax.ShapeDtypeStruct((n, width), dtype), pl.BlockSpec((tm, width), row))
    grouped = lambda count: (jax.ShapeDtypeStruct((batch, count, seq, LANES), BF16),
                             pl.BlockSpec((1, count, tm, LANES), per_group))
    values_t = (jax.ShapeDtypeStruct((batch, N_GROUPS, V_ROWS, seq), BF16),
                pl.BlockSpec((1, N_GROUPS, V_ROWS, tm), lambda i: (i // tiles_per_seq, 0, 0, i % tiles_per_seq)))
    outs = [grouped(N_HEADS), flat(LANES, F32), flat(LANES, F32), grouped(N_GROUPS), grouped(N_GROUPS),
            values_t, values_t, flat(LANES, F32), flat(POOL_COLS, F32), flat(2 * d, F32)]
    return pl.pallas_call(
        _proj_kernel,
        out_shape=[o[0] for o in outs],
        grid=(n // tm,),
        in_specs=[pl.BlockSpec((tm, d), row), pl.BlockSpec((1, d), const),
                  pl.BlockSpec(w_packed.shape, const, pipeline_mode=pl.Buffered(1)),
                  pl.BlockSpec(qf.shape, const), pl.BlockSpec((tm, LANES), lambda i: (i % tiles_per_seq, 0))],
        out_specs=[o[1] for o in outs],
        compiler_params=_params("parallel"),
        name="proj",
    )(x, norm_g, w_packed, qf, kf)


def _compress_kernel(k_ref, v_ref, pos_ref, wk1_ref, wk2_ref, wv1_ref, wv2_ref, feat_ref, ko_ref, vo_ref):
    n_chunk = k_ref.shape[0] // CMP_STRIDE
    hidden = wk2_ref.shape[0]

    def compress(x_ref, w1_ref, w2_ref):
        first = second = None
        for r in range(CMP_STRIDE):
            rows = x_ref[pl.ds(r, n_chunk, stride=CMP_STRIDE), :]
            a = _dot((rows + pos_ref[r:r + 1, :]).astype(BF16), w1_ref[r])
            b = _dot((rows + pos_ref[CMP_STRIDE + r:CMP_STRIDE + r + 1, :]).astype(BF16), w1_ref[CMP_STRIDE + r])
            first = a if first is None else first + a
            second = b if second is None else second + b
        act = jax.nn.gelu(first + pltpu.roll(second, n_chunk - 1, 0)).astype(BF16)
        return [_dot(act[:, g * hidden:(g + 1) * hidden], w2_ref[...]) for g in range(N_GROUPS)]

    for g, k_cmp in enumerate(compress(k_ref, wk1_ref, wk2_ref)):
        ko_ref[0, g] = jnp.concatenate([k_cmp, feat_ref[...]], axis=-1).astype(BF16)
    _store_values_t(vo_ref, jnp.concatenate(compress(v_ref, wv1_ref, wv2_ref), axis=-1))


def _block_diag_w1(w1):
    w = w1.reshape(CMP_LEN, HEAD_DIM, -1)
    z = jnp.zeros_like(w)
    return jnp.concatenate([jnp.concatenate([w, z], axis=2), jnp.concatenate([z, w], axis=2)], axis=1).astype(BF16)


def _compress(kc, vc, pos, wk1, wk2, wv1, wv2, batch, seq):
    n_chunk = seq // CMP_STRIDE
    rows = pl.BlockSpec((seq, LANES), lambda b: (b, 0))
    full = lambda a: pl.BlockSpec(a.shape, lambda b: (0,) * a.ndim, pipeline_mode=pl.Buffered(1))
    pos2 = jnp.concatenate([pos] * N_GROUPS, axis=1)
    feat = _key_features(np.arange(n_chunk) * CMP_STRIDE + CMP_LEN - 1, HEAD_DIM, 0)
    args = (pos2, _block_diag_w1(wk1), wk2.astype(BF16), _block_diag_w1(wv1), wv2.astype(BF16), feat)
    return pl.pallas_call(
        _compress_kernel,
        out_shape=[jax.ShapeDtypeStruct((batch, N_GROUPS, n_chunk, LANES), BF16),
                   jax.ShapeDtypeStruct((batch, N_GROUPS, V_ROWS, n_chunk), BF16)],
        grid=(batch,),
        in_specs=[rows, rows] + [full(a) for a in args],
        out_specs=[pl.BlockSpec((1, N_GROUPS, n_chunk, LANES), lambda b: (b, 0, 0, 0)),
                   pl.BlockSpec((1, N_GROUPS, V_ROWS, n_chunk), lambda b: (b, 0, 0, 0))],
        compiler_params=_params("parallel"),
        name="compress",
    )(kc, vc, *args)


SLC_CHUNK = 256
BLOCKS_PER_CHUNK = SLC_CHUNK // SEL_LEN
HEAD_BLOCKS = 2


def _nsa_kernel(q_ref, kc_ref, vct_ref, ks_ref, vst_ref, kw_ref, vwt_ref, gate_ref, ovt_ref, o_ref,
                st_ref, p_ref, gt_ref, selt_ref, list_ref, m_ref, acc_ref, out_ref, *, tq, seq):
    n_sel = seq // SEL_LEN
    n_chunks = seq // SLC_CHUNK
    n_cmp = kc_ref.shape[2]
    n_win = WINDOW + tq
    q0 = pl.program_id(1) * tq
    gt_ref[...] = gate_ref[...].T
    pairs = [(g, g * HEADS_PER_GROUP + 2 * j) for g in range(N_GROUPS) for j in range(HEADS_PER_GROUP // 2)]

    def gate_row(head, branch):
        c = 3 * head + branch
        return gt_ref[c:c + 1, :]

    def out_rows(head):
        g, hh = divmod(head, HEADS_PER_GROUP)
        return hh, slice(g * HEAD_DIM, (g + 1) * HEAD_DIM)

    def stage_scores(keys, n_keys, row0=0):
        for g, head0 in pairs:
            q_pair = q_ref[0, head0:head0 + 2].reshape(2 * tq, LANES)
            scores = _dot_nt(keys(g), q_pair)
            for i in range(2):
                st_ref[head0 + i, row0:row0 + n_keys, :] = scores[:, i * tq:(i + 1) * tq]

    def finish_pair(head0, pv, branch, first, ok=None):
        invs = []
        for i in range(2):
            cols = slice(i * tq, (i + 1) * tq)
            inv = 1.0 / pv[HEAD_DIM:HEAD_DIM + 1, cols]
            if ok is not None:
                inv = jnp.where(ok, inv, 0.0)
            hh, rows = out_rows(head0 + i)
            contrib = pv[0:HEAD_DIM, cols] * (inv * gate_row(head0 + i, branch))
            out_ref[hh, rows, :] = contrib if first else out_ref[hh, rows, :] + contrib
            invs.append(inv)
        return invs

    t_c = q0 + lax.broadcasted_iota(jnp.int32, (n_cmp, tq), 1)
    end_c = lax.broadcasted_iota(jnp.int32, (n_cmp, tq), 0) * CMP_STRIDE + (CMP_LEN - 1)
    bias_c = jnp.where(t_c >= end_c, 0.0, NEG_BIG)
    has_cmp = (q0 + lax.broadcasted_iota(jnp.int32, (1, tq), 1)) >= CMP_LEN - 1
    inv_c = [None] * N_HEADS

    def cmp_pair(g, head0):
        for i in range(2):
            x = st_ref[head0 + i, 0:n_cmp, :] + bias_c
            p_ref[head0 // 2, 0:n_cmp, i * tq:(i + 1) * tq] = jnp.exp2(x - jnp.max(x, axis=0, keepdims=True)).astype(BF16)
        pv = _dot(vct_ref[0, g], p_ref[head0 // 2, 0:n_cmp, :])
        inv_c[head0], inv_c[head0 + 1] = finish_pair(head0, pv, 0, True, has_cmp)

    stage_scores(lambda g: kc_ref[0, g], n_cmp)
    for g, head0 in pairs:
        cmp_pair(g, head0)

    raw = [_dot(ovt_ref[...], p_ref[pair, 0:n_cmp, :]) for pair in range(N_HEADS // 2)]

    w0 = pl.multiple_of(jnp.maximum(q0 - WINDOW, 0), tq)
    stage_scores(lambda g: kw_ref[0, g, pl.ds(w0, n_win), :], n_win)
    head_keys = HEAD_BLOCKS * SEL_LEN
    stage_scores(lambda g: ks_ref[0, g, 0:head_keys, :], head_keys, n_win)

    blk = lax.broadcasted_iota(jnp.int32, (n_sel, tq), 0)
    cur = (q0 + lax.broadcasted_iota(jnp.int32, (n_sel, tq), 1)) // SEL_LEN
    valid = blk <= cur
    forced = valid & ((blk == 0) | (blk == cur) | (blk == cur - 1))
    blk_f = blk.astype(F32)
    score = []
    for g in range(N_GROUPS):
        total = None
        for head in range(g * HEADS_PER_GROUP, (g + 1) * HEADS_PER_GROUP):
            part = raw[head // 2][:, (head % 2) * tq:(head % 2 + 1) * tq] * inv_c[head]
            total = part if total is None else total + part
        score.append(jnp.where(forced, TOPK_BIG, jnp.where(valid, total, -TOPK_BIG)))

    for _ in range(min(N_SELECT, n_sel)):
        for g in range(N_GROUPS):
            best = jnp.max(score[g], axis=0, keepdims=True)
            pick = jnp.min(jnp.where(score[g] == best, blk_f, float(n_sel)), axis=0, keepdims=True)
            score[g] = jnp.where(blk_f == pick, TOPK_TAKEN, score[g])
    sel = [jnp.where(valid & (score[g] == TOPK_TAKEN), 1.0, 0.0) for g in range(N_GROUPS)]
    for g in range(N_GROUPS):
        selt_ref[g] = jnp.where(blk >= HEAD_BLOCKS, sel[g], 0.0)
    sel_any = jnp.where(blk >= HEAD_BLOCKS, sel[0] + sel[1], 0.0)
    n_items = jnp.int32(0)
    for c in range(n_chunks):
        any_sel = jnp.max(sel_any[c * BLOCKS_PER_CHUNK:(c + 1) * BLOCKS_PER_CHUNK, :])
        list_ref[n_items] = c
        n_items = n_items + (any_sel > 0.5).astype(jnp.int32)

    causal_h = (q0 + lax.broadcasted_iota(jnp.int32, (head_keys, tq), 1)) >= lax.broadcasted_iota(jnp.int32, (head_keys, tq), 0)
    head_rows = slice(n_win, n_win + head_keys)
    for g, head0 in pairs:
        if head0 % HEADS_PER_GROUP == 0:
            sel_keys = jnp.concatenate([jnp.broadcast_to(sel[g][i:i + 1, :], (SEL_LEN, tq)) for i in range(HEAD_BLOCKS)], axis=0)
            bias = jnp.where(causal_h & (sel_keys > 0.5), 0.0, NEG_BIG)
        for i in range(2):
            x = st_ref[head0 + i, head_rows, :] + bias
            m_new = jnp.max(x, axis=0, keepdims=True)
            m_ref[head0 + i:head0 + i + 1, :] = m_new
            p_ref[head0 // 2, head_rows, i * tq:(i + 1) * tq] = jnp.exp2(x - m_new).astype(BF16)
    for g, head0 in pairs:
        acc_ref[head0 // 2] = _dot(vst_ref[0, g, :, 0:head_keys], p_ref[head0 // 2, head_rows, :])

    dist_w = (q0 + lax.broadcasted_iota(jnp.int32, (n_win, tq), 1)) - (w0 + lax.broadcasted_iota(jnp.int32, (n_win, tq), 0))
    bias_w = jnp.where((dist_w >= 0) & (dist_w < WINDOW), 0.0, NEG_BIG)

    def win_pair(g, head0):
        for i in range(2):
            x = st_ref[head0 + i, 0:n_win, :] + bias_w
            p_ref[head0 // 2, 0:n_win, i * tq:(i + 1) * tq] = jnp.exp2(x - jnp.max(x, axis=0, keepdims=True)).astype(BF16)
        pv = _dot(vwt_ref[0, g, :, pl.ds(w0, n_win)], p_ref[head0 // 2, 0:n_win, :])
        finish_pair(head0, pv, 2, False)

    for g, head0 in pairs:
        win_pair(g, head0)

    base = (lax.broadcasted_iota(jnp.int32, (SLC_CHUNK, tq), 1)
            - lax.broadcasted_iota(jnp.int32, (SLC_CHUNK, tq), 0))

    def stage_chunk(item, row0):
        c = list_ref[item]
        k0 = pl.multiple_of(c * SLC_CHUNK, SLC_CHUNK)
        causal = base >= k0 - q0
        for g, head0 in pairs:
            if head0 % HEADS_PER_GROUP == 0:
                sel_keys = jnp.concatenate(
                    [jnp.broadcast_to(selt_ref[g, pl.ds(c * BLOCKS_PER_CHUNK + i, 1), :], (SEL_LEN, tq))
                     for i in range(BLOCKS_PER_CHUNK)], axis=0)
                bias = jnp.where(causal & (sel_keys > 0.5), 0.0, NEG_BIG)
                bias2 = jnp.concatenate([bias, bias], axis=1)
            q_pair = q_ref[0, head0:head0 + 2].reshape(2 * tq, LANES)
            scores = _dot_nt(ks_ref[0, g, pl.ds(k0, SLC_CHUNK), :], q_pair) + bias2
            for i in range(2):
                st_ref[head0 + i, row0:row0 + SLC_CHUNK, :] = scores[:, i * tq:(i + 1) * tq]

    def consume_chunk(item, row0):
        k0 = pl.multiple_of(list_ref[item] * SLC_CHUNK, SLC_CHUNK)
        for g, head0 in pairs:
            alphas = []
            for i, head in enumerate((head0, head0 + 1)):
                x = st_ref[head, row0:row0 + SLC_CHUNK, :]
                m_old = m_ref[head:head + 1, :]
                m_new = jnp.maximum(m_old, jnp.max(x, axis=0, keepdims=True))
                alphas.append(jnp.exp2(m_old - m_new))
                m_ref[head:head + 1, :] = m_new
                p_ref[head0 // 2, 0:SLC_CHUNK, i * tq:(i + 1) * tq] = jnp.exp2(x - m_new).astype(BF16)
            pv = _dot(vst_ref[0, g, :, pl.ds(k0, SLC_CHUNK)], p_ref[head0 // 2, 0:SLC_CHUNK, :])
            acc_ref[head0 // 2] = acc_ref[head0 // 2] * jnp.concatenate(alphas, axis=1) + pv

    last = jnp.maximum(n_items - 1, 0)
    stage_chunk(0, 0)

    def two_chunks(j, carry):
        stage_chunk(jnp.minimum(2 * j + 1, last), SLC_CHUNK)
        consume_chunk(2 * j, 0)

        @pl.when(2 * j + 1 < n_items)
        def _():
            stage_chunk(jnp.minimum(2 * j + 2, last), 0)
            consume_chunk(2 * j + 1, SLC_CHUNK)
        return carry

    lax.fori_loop(0, (n_items + 1) // 2, two_chunks, 0)
    for g, head0 in pairs:
        finish_pair(head0, acc_ref[head0 // 2], 1, False)

    for hh in range(HEADS_PER_GROUP):
        o_ref[:, hh * LANES:(hh + 1) * LANES] = out_ref[hh].T.astype(BF16)


def _overlap_t(seq):
    n_cmp = (seq - CMP_LEN) // CMP_STRIDE + 1
    n_sel = seq // SEL_LEN
    cs = np.arange(n_cmp)[:, None] * CMP_STRIDE
    ss = np.arange(n_sel)[None, :] * SEL_LEN
    ov = np.clip(np.minimum(cs + CMP_LEN, ss + SEL_LEN) - np.maximum(cs, ss), 0, None) / CMP_LEN
    out = np.zeros((n_sel, seq // CMP_STRIDE), np.float32)
    out[:, :n_cmp] = ov.T
    return jnp.asarray(out, dtype=BF16)


def _nsa(q, kc, vct, ks, vst, kw, vwt, gates, *, tq=128):
    batch, _, seq, _ = q.shape
    n_cmp = kc.shape[2]
    n_sel = seq // SEL_LEN
    assert seq % SLC_CHUNK == 0 and seq >= WINDOW + tq and SLC_CHUNK % tq == 0 and tq == LANES
    assert n_sel <= 2 * SEL_LEN
    ovt = _overlap_t(seq)
    keys = lambda n: pl.BlockSpec((1, N_GROUPS, n, LANES), lambda b, i: (b, 0, 0, 0))
    vals = lambda n: pl.BlockSpec((1, N_GROUPS, V_ROWS, n), lambda b, i: (b, 0, 0, 0))
    tiles = seq // tq
    s_rows = max(n_cmp, WINDOW + tq + HEAD_BLOCKS * SEL_LEN, 2 * SLC_CHUNK)
    return pl.pallas_call(
        functools.partial(_nsa_kernel, tq=tq, seq=seq),
        out_shape=jax.ShapeDtypeStruct((batch * seq, HEADS_PER_GROUP * LANES), BF16),
        grid=(batch, tiles),
        in_specs=[
            pl.BlockSpec((1, N_HEADS, tq, LANES), lambda b, i: (b, 0, i, 0)),
            keys(n_cmp), vals(n_cmp), keys(seq), vals(seq), keys(seq), vals(seq),
            pl.BlockSpec((tq, LANES), lambda b, i: (b * tiles + i, 0)),
            pl.BlockSpec(ovt.shape, lambda b, i: (0, 0)),
        ],
        out_specs=pl.BlockSpec((tq, HEADS_PER_GROUP * LANES), lambda b, i: (b * tiles + i, 0)),
        scratch_shapes=[
            pltpu.VMEM((N_HEADS, s_rows, tq), F32),
            pltpu.VMEM((N_HEADS // 2, s_rows, 2 * tq), BF16),
            pltpu.VMEM((LANES, tq), F32),
            pltpu.VMEM((N_GROUPS, n_sel, tq), F32),
            pltpu.SMEM((seq // SLC_CHUNK,), jnp.int32),
            pltpu.VMEM((N_HEADS, tq), F32),
            pltpu.VMEM((N_HEADS // 2, V_ROWS, 2 * tq), F32),
            pltpu.VMEM((HEADS_PER_GROUP, N_GROUPS * HEAD_DIM, tq), F32),
        ],
        compiler_params=_params("parallel", "arbitrary"),
        name="nsa",
    )(q, kc, vct, ks, vst, kw, vwt, gates, ovt)


HALO = max(POOL_WINDOWS)


def _merge_kernel(x_ref, u_ref, halo_ref, gm_ref, on_ref, pw_ref, ps_ref, wbp_ref, wbn_ref, wo_ref,
                  o_ref, ext_ref, *, tm, seq):
    d = x_ref.shape[1]
    pos0 = (pl.program_id(0) * tm) % seq
    ext_ref[0:HALO, :] = jnp.where(pos0 == 0, 0.0, halo_ref[...])
    ext_ref[HALO:HALO + tm, :] = u_ref[...]
    pos = (pos0 + lax.broadcasted_iota(jnp.int32, (tm, POOL_GROUP_DIM), 0)).astype(F32)
    mixed = []
    for gi, w in enumerate(POOL_WINDOWS):
        cols = slice(gi * POOL_GROUP_DIM, (gi + 1) * POOL_GROUP_DIM)
        u = ext_ref[HALO:HALO + tm, cols]
        total = u
        for lag in range(1, w):
            total = total + ext_ref[HALO - lag:HALO - lag + tm, cols]
        delta = total / jnp.minimum(pos + 1.0, float(w)) - u
        mixed.append(_dot(delta.astype(BF16), pw_ref[gi]) * ps_ref[:, cols])
    mixed = jnp.concatenate(mixed, axis=-1).astype(BF16)
    a = _dot(mixed, wbp_ref[...])
    b = _dot(on_ref[...], wbn_ref[...])
    merged = gm_ref[:, 0:d] * a + gm_ref[:, d:2 * d] * b
    o_ref[...] = x_ref[...] + _dot(merged.astype(BF16), wo_ref[...])


def _merge(x, u, gm, o_nsa, pool_w, pool_scale, w_bp, w_bn, w_out, seq, *, tm=512):
    n, d = x.shape
    row = lambda i: (i, 0)
    full = lambda a: pl.BlockSpec(a.shape, lambda i: (0,) * a.ndim, pipeline_mode=pl.Buffered(1))
    halo_blocks = tm // HALO
    return pl.pallas_call(
        functools.partial(_merge_kernel, tm=tm, seq=seq),
        out_shape=jax.ShapeDtypeStruct((n, d), F32),
        grid=(n // tm,),
        in_specs=[
            pl.BlockSpec((tm, d), row),
            pl.BlockSpec((tm, POOL_COLS), row),
            pl.BlockSpec((HALO, POOL_COLS), lambda i: (jnp.maximum(i * halo_blocks - 1, 0), 0)),
            pl.BlockSpec((tm, 2 * d), row),
            pl.BlockSpec((tm, d), row),
            full(pool_w), full(pool_scale), full(w_bp), full(w_bn), full(w_out),
        ],
        out_specs=pl.BlockSpec((tm, d), row),
        scratch_shapes=[pltpu.VMEM((HALO + tm, POOL_COLS), F32)],
        compiler_params=_params("parallel"),
        name="merge",
    )(x, u, u, gm, o_nsa, pool_w, pool_scale, w_bp, w_bn, w_out)


def kernel(x, ffn1_norm, ffn1_w_gate, ffn1_w_up, ffn1_w_down, mix_norm, w_in, cmp_pos, cmp_k_w1, cmp_k_w2, cmp_v_w1, cmp_v_w2, pool_w, pool_scale, w_branch_pool, w_branch_nsa, w_out, ffn2_norm, ffn2_w_gate, ffn2_w_up, ffn2_w_down, final_norm):
    batch, seq, d = x.shape
    depth = w_in.shape[0]
    xf = x.reshape(batch * seq, d)
    bf = lambda a: a.astype(BF16)
    row = lambda a: a.reshape(1, -1)
    for l in range(depth):
        xf = _ffn(xf, row(ffn1_norm[l]), bf(ffn1_w_gate[l]), bf(ffn1_w_up[l]), bf(ffn1_w_down[l]),
                  row(final_norm), final_norm=False)

        q, kc, vc, ks, kw, vs, vw, gates, u, gm = _proj(xf, row(mix_norm[l]), _pack_w_in(w_in[l], d), batch, seq)
        k_cmp, v_cmp = _compress(kc, vc, cmp_pos[l], cmp_k_w1[l], cmp_k_w2[l], cmp_v_w1[l], cmp_v_w2[l], batch, seq)
        o_nsa = _nsa(q, k_cmp, v_cmp, ks, vs, kw, vw, gates)

        w_bn = w_branch_nsa[l].reshape(N_GROUPS, HEADS_PER_GROUP, HEAD_DIM, d).transpose(1, 0, 2, 3).reshape(-1, d)
        xf = _merge(xf, u, gm, o_nsa, bf(pool_w[l]), row(pool_scale[l]), bf(w_branch_pool[l]), bf(w_bn),
                    bf(w_out[l]), seq)

        xf = _ffn(xf, row(ffn2_norm[l]), bf(ffn2_w_gate[l]), bf(ffn2_w_up[l]), bf(ffn2_w_down[l]),
                  row(final_norm), final_norm=(l == depth - 1))
    return xf.reshape(batch, seq, d)
```

```python
import functools

import jax
import jax.numpy as jnp
import numpy as np
from jax import lax
from jax.experimental import pallas as pl
from jax.experimental.pallas import tpu as pltpu

N_HEADS = 16
N_GROUPS = 2
HEADS_PER_GROUP = N_HEADS // N_GROUPS
HEAD_DIM = 64
CMP_LEN = 32
CMP_STRIDE = 16
SEL_LEN = 64
N_SELECT = 16
WINDOW = 512
POOL_WINDOWS = (2, 4, 8, 16)
POOL_GROUP_DIM = 128
RMS_EPS = 1e-6
ALIBI_MAX_BIAS = 8.0

LANES = 128
NEG_BIG = -1e30
TOPK_BIG = 1e30
TOPK_TAKEN = -float(2 ** 101)
VMEM_LIMIT = 48 * 1024 * 1024

BF16 = jnp.bfloat16
F32 = jnp.float32


def _dot(a, b):
    return jnp.dot(a, b, preferred_element_type=F32)


def _dot_nt(a, b):
    return lax.dot_general(a, b, (((1,), (1,)), ((), ())), preferred_element_type=F32)


def _rms(x, g):
    return x * lax.rsqrt(jnp.mean(x * x, axis=-1, keepdims=True) + RMS_EPS) * g


def _params(*sem):
    return pltpu.CompilerParams(dimension_semantics=sem, vmem_limit_bytes=VMEM_LIMIT)


def _ffn_kernel(x_ref, g_ref, wg_ref, wu_ref, wd_ref, fin_ref, o_ref, *, final_norm):
    x = x_ref[...]
    xn = _rms(x, g_ref[...]).astype(BF16)
    gate = _dot(xn, wg_ref[...])
    up = _dot(xn, wu_ref[...])
    act = (gate * jax.nn.sigmoid(gate)) * up
    y = x + 0.5 * _dot(act.astype(BF16), wd_ref[...])
    if final_norm:
        y = _rms(y, fin_ref[...])
    o_ref[...] = y


def _ffn(x, norm_g, wg, wu, wd, fin_g, *, final_norm, tm=512):
    n, d = x.shape
    row = lambda i: (i, 0)
    resident = lambda a: pl.BlockSpec(a.shape, lambda i: (0, 0), pipeline_mode=pl.Buffered(1))
    return pl.pallas_call(
        functools.partial(_ffn_kernel, final_norm=final_norm),
        out_shape=jax.ShapeDtypeStruct((n, d), F32),
        grid=(n // tm,),
        in_specs=[pl.BlockSpec((tm, d), row), resident(norm_g), resident(wg), resident(wu), resident(wd),
                  resident(fin_g)],
        out_specs=pl.BlockSpec((tm, d), row),
        compiler_params=_params("parallel"),
        name="ffn",
    )(x, norm_g, wg, wu, wd, fin_g)


LOG2E = 1.4426950408889634
N_FEAT = 6


def _bf16_pieces(x):
    x = np.asarray(x, np.float32)
    s1 = x.astype(BF16).astype(np.float32)
    s2 = (x - s1).astype(BF16).astype(np.float32)
    s3 = (x - s1 - s2).astype(BF16).astype(np.float32)
    return s1, s2, s3


def _query_features():
    slopes = np.float32(2.0) ** (-ALIBI_MAX_BIAS * np.arange(1, N_HEADS + 1, dtype=np.float32) / N_HEADS)
    s1, s2, s3 = _bf16_pieces(slopes * np.float32(LOG2E))
    feat = np.zeros((N_HEADS, LANES), np.float32)
    feat[:, HEAD_DIM:HEAD_DIM + N_FEAT] = np.stack([s1, s2, s3, SEL_LEN * s1, SEL_LEN * s2, SEL_LEN * s3], axis=1)
    return jnp.asarray(feat)


def _key_features(pos, width, offset):
    pos = np.asarray(pos)
    a, b = (pos // SEL_LEN).astype(np.float32), (pos % SEL_LEN).astype(np.float32)
    feat = np.zeros((len(pos), width), np.float32)
    feat[:, offset:offset + N_FEAT] = np.stack([b, b, b, a, a, a], axis=1)
    return jnp.asarray(feat)


Q_COLS = N_HEADS * LANES
CMP_OFF = Q_COLS
KEY_OFF = CMP_OFF + 2 * LANES
VAL_OFF = KEY_OFF + 2 * N_GROUPS * LANES
GATE_OFF = VAL_OFF + 2 * LANES
POOL_OFF = GATE_OFF + LANES
POOL_COLS = len(POOL_WINDOWS) * POOL_GROUP_DIM
MERGE_OFF = POOL_OFF + POOL_COLS


V_ROWS = HEAD_DIM + 16


def _store_values_t(v_ref, v):
    vt = v.T
    for g in range(N_GROUPS):
        v_ref[0, g, 0:HEAD_DIM, :] = vt[g * HEAD_DIM:(g + 1) * HEAD_DIM, :].astype(BF16)
        v_ref[0, g, HEAD_DIM:V_ROWS, :] = jnp.ones((V_ROWS - HEAD_DIM, v.shape[0]), BF16)


def _proj_kernel(x_ref, g_ref, w_ref, qf_ref, kf_ref, q_ref, kc_ref, vc_ref, ks_ref, kw_ref, vs_ref, vw_ref,
                 gate_ref, u_ref, gm_ref):
    d = x_ref.shape[1]
    hn = _rms(x_ref[...], g_ref[...]).astype(BF16)
    q_scale = HEAD_DIM ** -0.5 * LOG2E
    for h in range(0, N_HEADS, 2):
        q2 = _dot(hn, w_ref[:, h * LANES:(h + 2) * LANES]) * q_scale
        for i in range(2):
            q_ref[0, h + i] = (q2[:, i * LANES:(i + 1) * LANES] + qf_ref[h + i:h + i + 1, :]).astype(BF16)
    cmp_in = _dot(hn, w_ref[:, CMP_OFF:CMP_OFF + 2 * LANES])
    kc_ref[...] = cmp_in[:, 0:LANES]
    vc_ref[...] = cmp_in[:, LANES:2 * LANES]
    keys = _dot(hn, w_ref[:, KEY_OFF:KEY_OFF + 2 * N_GROUPS * LANES])
    kf = kf_ref[...]
    for g in range(N_GROUPS):
        ks_ref[0, g] = (keys[:, g * LANES:(g + 1) * LANES] + kf).astype(BF16)
        kw_ref[0, g] = (keys[:, (N_GROUPS + g) * LANES:(N_GROUPS + g + 1) * LANES] + kf).astype(BF16)
    vals = _dot(hn, w_ref[:, VAL_OFF:VAL_OFF + 2 * LANES])
    for i, v_ref in enumerate((vs_ref, vw_ref)):
        _store_values_t(v_ref, vals[:, i * LANES:(i + 1) * LANES])
    gate_ref[...] = jax.nn.sigmoid(_dot(hn, w_ref[:, GATE_OFF:GATE_OFF + LANES]))
    u_ref[...] = _dot(hn, w_ref[:, POOL_OFF:POOL_OFF + POOL_COLS])
    gm_ref[...] = jax.nn.sigmoid(_dot(hn, w_ref[:, MERGE_OFF:MERGE_OFF + 2 * d]))


def _pack_w_in(w_in, d):
    qw = N_HEADS * HEAD_DIM
    kvw = N_GROUPS * HEAD_DIM

    def padded(w, n):
        w = w.reshape(d, n, HEAD_DIM)
        return jnp.concatenate([w, jnp.zeros_like(w)], axis=-1).reshape(d, n * LANES)

    kc, vc, ks, vs, kw, vw = (w_in[:, qw + i * kvw:qw + (i + 1) * kvw] for i in range(6))
    off = qw + 6 * kvw
    n_g = 3 * N_HEADS
    w_g = jnp.pad(w_in[:, off:off + n_g], ((0, 0), (0, LANES - n_g)))
    w_rest = w_in[:, off + n_g:]
    return jnp.concatenate([padded(w_in[:, :qw], N_HEADS), kc, vc, padded(ks, N_GROUPS), padded(kw, N_GROUPS),
                            vs, vw, w_g, w_rest], axis=1).astype(BF16)


def _proj(x, norm_g, w_packed, batch, seq, *, tm=512):
    n, d = x.shape
    tiles_per_seq = seq // tm
    row = lambda i: (i, 0)
    const = lambda i: (0, 0)
    per_group = lambda i: (i // tiles_per_seq, 0, i % tiles_per_seq, 0)
    qf = _query_features()
    kf = _key_features(np.arange(seq), LANES, HEAD_DIM)
    flat = lambda width, dtype: (jax.ShapeDtypeStruct((n, width), dtype), pl.BlockSpec((tm, width), row))
    grouped = lambda count: (jax.ShapeDtypeStruct((batch, count, seq, LANES), BF16),
                             pl.BlockSpec((1, count, tm, LANES), per_group))
    values_t = (jax.ShapeDtypeStruct((batch, N_GROUPS, V_ROWS, seq), BF16),
                pl.BlockSpec((1, N_GROUPS, V_ROWS, tm), lambda i: (i // tiles_per_seq, 0, 0, i % tiles_per_seq)))
    outs = [grouped(N_HEADS), flat(LANES, F32), flat(LANES, F32), grouped(N_GROUPS), grouped(N_GROUPS),
            values_t, values_t, flat(LANES, F32), flat(POOL_COLS, F32), flat(2 * d, F32)]
    return pl.pallas_call(
        _proj_kernel,
        out_shape=[o[0] for o in outs],
        grid=(n // tm,),
        in_specs=[pl.BlockSpec((tm, d), row), pl.BlockSpec((1, d), const),
                  pl.BlockSpec(w_packed.shape, const, pipeline_mode=pl.Buffered(1)),
                  pl.BlockSpec(qf.shape, const), pl.BlockSpec((tm, LANES), lambda i: (i % tiles_per_seq, 0))],
        out_specs=[o[1] for o in outs],
        compiler_params=_params("parallel"),
        name="proj",
    )(x, norm_g, w_packed, qf, kf)


def _compress_kernel(k_ref, v_ref, pos_ref, wk1_ref, wk2_ref, wv1_ref, wv2_ref, feat_ref, ko_ref, vo_ref):
    n_chunk = k_ref.shape[0] // CMP_STRIDE
    hidden = wk2_ref.shape[0]

    def compress(x_ref, w1_ref, w2_ref):
        first = second = None
        for r in range(CMP_STRIDE):
            rows = x_ref[pl.ds(r, n_chunk, stride=CMP_STRIDE), :]
            a = _dot((rows + pos_ref[r:r + 1, :]).astype(BF16), w1_ref[r])
            b = _dot((rows + pos_ref[CMP_STRIDE + r:CMP_STRIDE + r + 1, :]).astype(BF16), w1_ref[CMP_STRIDE + r])
            first = a if first is None else first + a
            second = b if second is None else second + b
        act = jax.nn.gelu(first + pltpu.roll(second, n_chunk - 1, 0)).astype(BF16)
        return [_dot(act[:, g * hidden:(g + 1) * hidden], w2_ref[...]) for g in range(N_GROUPS)]

    for g, k_cmp in enumerate(compress(k_ref, wk1_ref, wk2_ref)):
        ko_ref[0, g] = jnp.concatenate([k_cmp, feat_ref[...]], axis=-1).astype(BF16)
    _store_values_t(vo_ref, jnp.concatenate(compress(v_ref, wv1_ref, wv2_ref), axis=-1))


def _block_diag_w1(w1):
    w = w1.reshape(CMP_LEN, HEAD_DIM, -1)
    z = jnp.zeros_like(w)
    return jnp.concatenate([jnp.concatenate([w, z], axis=2), jnp.concatenate([z, w], axis=2)], axis=1).astype(BF16)


def _compress(kc, vc, pos, wk1, wk2, wv1, wv2, batch, seq):
    n_chunk = seq // CMP_STRIDE
    rows = pl.BlockSpec((seq, LANES), lambda b: (b, 0))
    full = lambda a: pl.BlockSpec(a.shape, lambda b: (0,) * a.ndim, pipeline_mode=pl.Buffered(1))
    pos2 = jnp.concatenate([pos] * N_GROUPS, axis=1)
    feat = _key_features(np.arange(n_chunk) * CMP_STRIDE + CMP_LEN - 1, HEAD_DIM, 0)
    args = (pos2, _block_diag_w1(wk1), wk2.astype(BF16), _block_diag_w1(wv1), wv2.astype(BF16), feat)
    return pl.pallas_call(
        _compress_kernel,
        out_shape=[jax.ShapeDtypeStruct((batch, N_GROUPS, n_chunk, LANES), BF16),
                   jax.ShapeDtypeStruct((batch, N_GROUPS, V_ROWS, n_chunk), BF16)],
        grid=(batch,),
        in_specs=[rows, rows] + [full(a) for a in args],
        out_specs=[pl.BlockSpec((1, N_GROUPS, n_chunk, LANES), lambda b: (b, 0, 0, 0)),
                   pl.BlockSpec((1, N_GROUPS, V_ROWS, n_chunk), lambda b: (b, 0, 0, 0))],
        compiler_params=_params("parallel"),
        name="compress",
    )(kc, vc, *args)


SLC_CHUNK = 256
BLOCKS_PER_CHUNK = SLC_CHUNK // SEL_LEN
HEAD_BLOCKS = 2


def _nsa_kernel(q_ref, kc_ref, vct_ref, ks_ref, vst_ref, kw_ref, vwt_ref, gate_ref, ovt_ref, o_ref,
                st_ref, p_ref, gt_ref, selt_ref, list_ref, m_ref, acc_ref, out_ref, *, tq, seq):
    n_sel = seq // SEL_LEN
    n_chunks = seq // SLC_CHUNK
    n_cmp = kc_ref.shape[2]
    n_win = WINDOW + tq
    q0 = pl.program_id(1) * tq
    gt_ref[...] = gate_ref[...].T
    pairs = [(g, g * HEADS_PER_GROUP + 2 * j) for g in range(N_GROUPS) for j in range(HEADS_PER_GROUP // 2)]

    def gate_row(head, branch):
        c = 3 * head + branch
        return gt_ref[c:c + 1, :]

    def out_rows(head):
        g, hh = divmod(head, HEADS_PER_GROUP)
        return hh, slice(g * HEAD_DIM, (g + 1) * HEAD_DIM)

    def stage_scores(keys, n_keys, row0=0):
        for g, head0 in pairs:
            q_pair = q_ref[0, head0:head0 + 2].reshape(2 * tq, LANES)
            scores = _dot_nt(keys(g), q_pair)
            for i in range(2):
                st_ref[head0 + i, row0:row0 + n_keys, :] = scores[:, i * tq:(i + 1) * tq]

    def finish_pair(head0, pv, branch, first, ok=None):
        invs = []
        for i in range(2):
            cols = slice(i * tq, (i + 1) * tq)
            inv = 1.0 / pv[HEAD_DIM:HEAD_DIM + 1, cols]
            if ok is not None:
                inv = jnp.where(ok, inv, 0.0)
            hh, rows = out_rows(head0 + i)
            contrib = pv[0:HEAD_DIM, cols] * (inv * gate_row(head0 + i, branch))
            out_ref[hh, rows, :] = contrib if first else out_ref[hh, rows, :] + contrib
            invs.append(inv)
        return invs

    t_c = q0 + lax.broadcasted_iota(jnp.int32, (n_cmp, tq), 1)
    end_c = lax.broadcasted_iota(jnp.int32, (n_cmp, tq), 0) * CMP_STRIDE + (CMP_LEN - 1)
    bias_c = jnp.where(t_c >= end_c, 0.0, NEG_BIG)
    has_cmp = (q0 + lax.broadcasted_iota(jnp.int32, (1, tq), 1)) >= CMP_LEN - 1
    inv_c = [None] * N_HEADS

    def cmp_pair(g, head0):
        for i in range(2):
            x = st_ref[head0 + i, 0:n_cmp, :] + bias_c
            p_ref[head0 // 2, 0:n_cmp, i * tq:(i + 1) * tq] = jnp.exp2(x - jnp.max(x, axis=0, keepdims=True)).astype(BF16)
        pv = _dot(vct_ref[0, g], p_ref[head0 // 2, 0:n_cmp, :])
        inv_c[head0], inv_c[head0 + 1] = finish_pair(head0, pv, 0, True, has_cmp)

    stage_scores(lambda g: kc_ref[0, g], n_cmp)
    for g, head0 in pairs:
        cmp_pair(g, head0)

    raw = [_dot(ovt_ref[...], p_ref[pair, 0:n_cmp, :]) for pair in range(N_HEADS // 2)]

    w0 = pl.multiple_of(jnp.maximum(q0 - WINDOW, 0), tq)
    stage_scores(lambda g: kw_ref[0, g, pl.ds(w0, n_win), :], n_win)
    head_keys = HEAD_BLOCKS * SEL_LEN
    stage_scores(lambda g: ks_ref[0, g, 0:head_keys, :], head_keys, n_win)

    blk = lax.broadcasted_iota(jnp.int32, (n_sel, tq), 0)
    cur = (q0 + lax.broadcasted_iota(jnp.int32, (n_sel, tq), 1)) // SEL_LEN
    valid = blk <= cur
    forced = valid & ((blk == 0) | (blk == cur) | (blk == cur - 1))
    blk_f = blk.astype(F32)
    score = []
    for g in range(N_GROUPS):
        total = None
        for head in range(g * HEADS_PER_GROUP, (g + 1) * HEADS_PER_GROUP):
            part = raw[head // 2][:, (head % 2) * tq:(head % 2 + 1) * tq] * inv_c[head]
            total = part if total is None else total + part
        score.append(jnp.where(forced, TOPK_BIG, jnp.where(valid, total, -TOPK_BIG)))

    for _ in range(min(N_SELECT, n_sel)):
        for g in range(N_GROUPS):
            best = jnp.max(score[g], axis=0, keepdims=True)
            pick = jnp.min(jnp.where(score[g] == best, blk_f, float(n_sel)), axis=0, keepdims=True)
            score[g] = jnp.where(blk_f == pick, TOPK_TAKEN, score[g])
    sel = [jnp.where(valid & (score[g] == TOPK_TAKEN), 1.0, 0.0) for g in range(N_GROUPS)]
    for g in range(N_GROUPS):
        selt_ref[g] = jnp.where(blk >= HEAD_BLOCKS, sel[g], 0.0)
    sel_any = jnp.where(blk >= HEAD_BLOCKS, sel[0] + sel[1], 0.0)
    n_items = jnp.int32(0)
    for c in range(n_chunks):
        any_sel = jnp.max(sel_any[c * BLOCKS_PER_CHUNK:(c + 1) * BLOCKS_PER_CHUNK, :])
        list_ref[n_items] = c
        n_items = n_items + (any_sel > 0.5).astype(jnp.int32)

    causal_h = (q0 + lax.broadcasted_iota(jnp.int32, (head_keys, tq), 1)) >= lax.broadcasted_iota(jnp.int32, (head_keys, tq), 0)
    head_rows = slice(n_win, n_win + head_keys)
    for g, head0 in pairs:
        if head0 % HEADS_PER_GROUP == 0:
            sel_keys = jnp.concatenate([jnp.broadcast_to(sel[g][i:i + 1, :], (SEL_LEN, tq)) for i in range(HEAD_BLOCKS)], axis=0)
            bias = jnp.where(causal_h & (sel_keys > 0.5), 0.0, NEG_BIG)
        for i in range(2):
            x = st_ref[head0 + i, head_rows, :] + bias
            m_new = jnp.max(x, axis=0, keepdims=True)
            m_ref[head0 + i:head0 + i + 1, :] = m_new
            p_ref[head0 // 2, head_rows, i * tq:(i + 1) * tq] = jnp.exp2(x - m_new).astype(BF16)
    for g, head0 in pairs:
        acc_ref[head0 // 2] = _dot(vst_ref[0, g, :, 0:head_keys], p_ref[head0 // 2, head_rows, :])

    dist_w = (q0 + lax.broadcasted_iota(jnp.int32, (n_win, tq), 1)) - (w0 + lax.broadcasted_iota(jnp.int32, (n_win, tq), 0))
    bias_w = jnp.where((dist_w >= 0) & (dist_w < WINDOW), 0.0, NEG_BIG)

    def win_pair(g, head0):
        for i in range(2):
            x = st_ref[head0 + i, 0:n_win, :] + bias_w
            p_ref[head0 // 2, 0:n_win, i * tq:(i + 1) * tq] = jnp.exp2(x - jnp.max(x, axis=0, keepdims=True)).astype(BF16)
        pv = _dot(vwt_ref[0, g, :, pl.ds(w0, n_win)], p_ref[head0 // 2, 0:n_win, :])
        finish_pair(head0, pv, 2, False)

    for g, head0 in pairs:
        win_pair(g, head0)

    base = (lax.broadcasted_iota(jnp.int32, (SLC_CHUNK, tq), 1)
            - lax.broadcasted_iota(jnp.int32, (SLC_CHUNK, tq), 0))

    def stage_chunk(item, row0):
        c = list_ref[item]
        k0 = pl.multiple_of(c * SLC_CHUNK, SLC_CHUNK)
        causal = base >= k0 - q0
        for g, head0 in pairs:
            if head0 % HEADS_PER_GROUP == 0:
                sel_keys = jnp.concatenate(
                    [jnp.broadcast_to(selt_ref[g, pl.ds(c * BLOCKS_PER_CHUNK + i, 1), :], (SEL_LEN, tq))
                     for i in range(BLOCKS_PER_CHUNK)], axis=0)
                bias = jnp.where(causal & (sel_keys > 0.5), 0.0, NEG_BIG)
                bias2 = jnp.concatenate([bias, bias], axis=1)
            q_pair = q_ref[0, head0:head0 + 2].reshape(2 * tq, LANES)
            scores = _dot_nt(ks_ref[0, g, pl.ds(k0, SLC_CHUNK), :], q_pair) + bias2
            for i in range(2):
                st_ref[head0 + i, row0:row0 + SLC_CHUNK, :] = scores[:, i * tq:(i + 1) * tq]

    def consume_chunk(item, row0):
        k0 = pl.multiple_of(list_ref[item] * SLC_CHUNK, SLC_CHUNK)
        for g, head0 in pairs:
            alphas = []
            for i, head in enumerate((head0, head0 + 1)):
                x = st_ref[head, row0:row0 + SLC_CHUNK, :]
                m_old = m_ref[head:head + 1, :]
                m_new = jnp.maximum(m_old, jnp.max(x, axis=0, keepdims=True))
                alphas.append(jnp.exp2(m_old - m_new))
                m_ref[head:head + 1, :] = m_new
                p_ref[head0 // 2, 0:SLC_CHUNK, i * tq:(i + 1) * tq] = jnp.exp2(x - m_new).astype(BF16)
            pv = _dot(vst_ref[0, g, :, pl.ds(k0, SLC_CHUNK)], p_ref[head0 // 2, 0:SLC_CHUNK, :])
            acc_ref[head0 // 2] = acc_ref[head0 // 2] * jnp.concatenate(alphas, axis=1) + pv

    last = jnp.maximum(n_items - 1, 0)
    stage_chunk(0, 0)

    def two_chunks(j, carry):
        stage_chunk(jnp.minimum(2 * j + 1, last), SLC_CHUNK)
        consume_chunk(2 * j, 0)

        @pl.when(2 * j + 1 < n_items)
        def _():
            stage_chunk(jnp.minimum(2 * j + 2, last), 0)
            consume_chunk(2 * j + 1, SLC_CHUNK)
        return carry

    lax.fori_loop(0, (n_items + 1) // 2, two_chunks, 0)
    for g, head0 in pairs:
        finish_pair(head0, acc_ref[head0 // 2], 1, False)

    for hh in range(HEADS_PER_GROUP):
        o_ref[:, hh * LANES:(hh + 1) * LANES] = out_ref[hh].T.astype(BF16)


def _overlap_t(seq):
    n_cmp = (seq - CMP_LEN) // CMP_STRIDE + 1
    n_sel = seq // SEL_LEN
    cs = np.arange(n_cmp)[:, None] * CMP_STRIDE
    ss = np.arange(n_sel)[None, :] * SEL_LEN
    ov = np.clip(np.minimum(cs + CMP_LEN, ss + SEL_LEN) - np.maximum(cs, ss), 0, None) / CMP_LEN
    out = np.zeros((n_sel, seq // CMP_STRIDE), np.float32)
    out[:, :n_cmp] = ov.T
    return jnp.asarray(out, dtype=BF16)


def _nsa(q, kc, vct, ks, vst, kw, vwt, gates, *, tq=128):
    batch, _, seq, _ = q.shape
    n_cmp = kc.shape[2]
    n_sel = seq // SEL_LEN
    assert seq % SLC_CHUNK == 0 and seq >= WINDOW + tq and SLC_CHUNK % tq == 0 and tq == LANES
    assert n_sel <= 2 * SEL_LEN
    ovt = _overlap_t(seq)
    keys = lambda n: pl.BlockSpec((1, N_GROUPS, n, LANES), lambda b, i: (b, 0, 0, 0))
    vals = lambda n: pl.BlockSpec((1, N_GROUPS, V_ROWS, n), lambda b, i: (b, 0, 0, 0))
    tiles = seq // tq
    s_rows = max(n_cmp, WINDOW + tq + HEAD_BLOCKS * SEL_LEN, 2 * SLC_CHUNK)
    return pl.pallas_call(
        functools.partial(_nsa_kernel, tq=tq, seq=seq),
        out_shape=jax.ShapeDtypeStruct((batch * seq, HEADS_PER_GROUP * LANES), BF16),
        grid=(batch, tiles),
        in_specs=[
            pl.BlockSpec((1, N_HEADS, tq, LANES), lambda b, i: (b, 0, i, 0)),
            keys(n_cmp), vals(n_cmp), keys(seq), vals(seq), keys(seq), vals(seq),
            pl.BlockSpec((tq, LANES), lambda b, i: (b * tiles + i, 0)),
            pl.BlockSpec(ovt.shape, lambda b, i: (0, 0)),
        ],
        out_specs=pl.BlockSpec((tq, HEADS_PER_GROUP * LANES), lambda b, i: (b * tiles + i, 0)),
        scratch_shapes=[
            pltpu.VMEM((N_HEADS, s_rows, tq), F32),
            pltpu.VMEM((N_HEADS // 2, s_rows, 2 * tq), BF16),
            pltpu.VMEM((LANES, tq), F32),
            pltpu.VMEM((N_GROUPS, n_sel, tq), F32),
            pltpu.SMEM((seq // SLC_CHUNK,), jnp.int32),
            pltpu.VMEM((N_HEADS, tq), F32),
            pltpu.VMEM((N_HEADS // 2, V_ROWS, 2 * tq), F32),
            pltpu.VMEM((HEADS_PER_GROUP, N_GROUPS * HEAD_DIM, tq), F32),
        ],
        compiler_params=_params("parallel", "arbitrary"),
        name="nsa",
    )(q, kc, vct, ks, vst, kw, vwt, gates, ovt)


HALO = max(POOL_WINDOWS)


def _merge_kernel(x_ref, u_ref, halo_ref, gm_ref, on_ref, pw_ref, ps_ref, wbp_ref, wbn_ref, wo_ref,
                  o_ref, ext_ref, *, tm, seq):
    d = x_ref.shape[1]
    pos0 = (pl.program_id(0) * tm) % seq
    ext_ref[0:HALO, :] = jnp.where(pos0 == 0, 0.0, halo_ref[...])
    ext_ref[HALO:HALO + tm, :] = u_ref[...]
    pos = (pos0 + lax.broadcasted_iota(jnp.int32, (tm, POOL_GROUP_DIM), 0)).astype(F32)
    mixed = []
    for gi, w in enumerate(POOL_WINDOWS):
        cols = slice(gi * POOL_GROUP_DIM, (gi + 1) * POOL_GROUP_DIM)
        u = ext_ref[HALO:HALO + tm, cols]
        total = u
        for lag in range(1, w):
            total = total + ext_ref[HALO - lag:HALO - lag + tm, cols]
        delta = total / jnp.minimum(pos + 1.0, float(w)) - u
        mixed.append(_dot(delta.astype(BF16), pw_ref[gi]) * ps_ref[:, cols])
    mixed = jnp.concatenate(mixed, axis=-1).astype(BF16)
    a = _dot(mixed, wbp_ref[...])
    b = _dot(on_ref[...], wbn_ref[...])
    merged = gm_ref[:, 0:d] * a + gm_ref[:, d:2 * d] * b
    o_ref[...] = x_ref[...] + _dot(merged.astype(BF16), wo_ref[...])


def _merge(x, u, gm, o_nsa, pool_w, pool_scale, w_bp, w_bn, w_out, seq, *, tm=512):
    n, d = x.shape
    row = lambda i: (i, 0)
    full = lambda a: pl.BlockSpec(a.shape, lambda i: (0,) * a.ndim, pipeline_mode=pl.Buffered(1))
    halo_blocks = tm // HALO
    return pl.pallas_call(
        functools.partial(_merge_kernel, tm=tm, seq=seq),
        out_shape=jax.ShapeDtypeStruct((n, d), F32),
        grid=(n // tm,),
        in_specs=[
            pl.BlockSpec((tm, d), row),
            pl.BlockSpec((tm, POOL_COLS), row),
            pl.BlockSpec((HALO, POOL_COLS), lambda i: (jnp.maximum(i * halo_blocks - 1, 0), 0)),
            pl.BlockSpec((tm, 2 * d), row),
            pl.BlockSpec((tm, d), row),
            full(pool_w), full(pool_scale), full(w_bp), full(w_bn), full(w_out),
        ],
        out_specs=pl.BlockSpec((tm, d), row),
        scratch_shapes=[pltpu.VMEM((HALO + tm, POOL_COLS), F32)],
        compiler_params=_params("parallel"),
        name="merge",
    )(x, u, u, gm, o_nsa, pool_w, pool_scale, w_bp, w_bn, w_out)


def kernel(x, ffn1_norm, ffn1_w_gate, ffn1_w_up, ffn1_w_down, mix_norm, w_in, cmp_pos, cmp_k_w1, cmp_k_w2, cmp_v_w1, cmp_v_w2, pool_w, pool_scale, w_branch_pool, w_branch_nsa, w_out, ffn2_norm, ffn2_w_gate, ffn2_w_up, ffn2_w_down, final_norm):
    batch, seq, d = x.shape
    depth = w_in.shape[0]
    xf = x.reshape(batch * seq, d)
    bf = lambda a: a.astype(BF16)
    row = lambda a: a.reshape(1, -1)
    for l in range(depth):
        xf = _ffn(xf, row(ffn1_norm[l]), bf(ffn1_w_gate[l]), bf(ffn1_w_up[l]), bf(ffn1_w_down[l]),
                  row(final_norm), final_norm=False)

        q, kc, vc, ks, kw, vs, vw, gates, u, gm = _proj(xf, row(mix_norm[l]), _pack_w_in(w_in[l], d), batch, seq)
        k_cmp, v_cmp = _compress(kc, vc, cmp_pos[l], cmp_k_w1[l], cmp_k_w2[l], cmp_v_w1[l], cmp_v_w2[l], batch, seq)
        o_nsa = _nsa(q, k_cmp, v_cmp, ks, vs, kw, vw, gates)

        w_bn = w_branch_nsa[l].reshape(N_GROUPS, HEADS_PER_GROUP, HEAD_DIM, d).transpose(1, 0, 2, 3).reshape(-1, d)
        xf = _merge(xf, u, gm, o_nsa, bf(pool_w[l]), row(pool_scale[l]), bf(w_branch_pool[l]), bf(w_bn),
                    bf(w_out[l]), seq)

        xf = _ffn(xf, row(ffn2_norm[l]), bf(ffn2_w_gate[l]), bf(ffn2_w_up[l]), bf(ffn2_w_down[l]),
                  row(final_norm), final_norm=(l == depth - 1))
    return xf.reshape(batch, seq, d)
```

```python
import functools

import jax
import jax.numpy as jnp
import numpy as np
from jax import lax
from jax.experimental import pallas as pl
from jax.experimental.pallas import tpu as pltpu

N_HEADS = 16
N_GROUPS = 2
HEADS_PER_GROUP = N_HEADS // N_GROUPS
HEAD_DIM = 64
CMP_LEN = 32
CMP_STRIDE = 16
SEL_LEN = 64
N_SELECT = 16
WINDOW = 512
POOL_WINDOWS = (2, 4, 8, 16)
POOL_GROUP_DIM = 128
RMS_EPS = 1e-6
ALIBI_MAX_BIAS = 8.0

LANES = 128
NEG_BIG = -1e30
TOPK_BIG = 1e30
TOPK_TAKEN = -float(2 ** 101)
VMEM_LIMIT = 48 * 1024 * 1024

BF16 = jnp.bfloat16
F32 = jnp.float32


def _dot(a, b):
    return jnp.dot(a, b, preferred_element_type=F32)


def _dot_nt(a, b):
    return lax.dot_general(a, b, (((1,), (1,)), ((), ())), preferred_element_type=F32)


def _rms(x, g):
    return x * lax.rsqrt(jnp.mean(x * x, axis=-1, keepdims=True) + RMS_EPS) * g


def _params(*sem):
    return pltpu.CompilerParams(dimension_semantics=sem, vmem_limit_bytes=VMEM_LIMIT)


def _ffn_kernel(x_ref, g_ref, wg_ref, wu_ref, wd_ref, fin_ref, o_ref, *, final_norm):
    x = x_ref[...]
    xn = _rms(x, g_ref[...]).astype(BF16)
    gate = _dot(xn, wg_ref[...])
    up = _dot(xn, wu_ref[...])
    act = (gate * jax.nn.sigmoid(gate)) * up
    y = x + 0.5 * _dot(act.astype(BF16), wd_ref[...])
    if final_norm:
        y = _rms(y, fin_ref[...])
    o_ref[...] = y


def _ffn(x, norm_g, wg, wu, wd, fin_g, *, final_norm, tm=512):
    n, d = x.shape
    row = lambda i: (i, 0)
    resident = lambda a: pl.BlockSpec(a.shape, lambda i: (0, 0), pipeline_mode=pl.Buffered(1))
    return pl.pallas_call(
        functools.partial(_ffn_kernel, final_norm=final_norm),
        out_shape=jax.ShapeDtypeStruct((n, d), F32),
        grid=(n // tm,),
        in_specs=[pl.BlockSpec((tm, d), row), resident(norm_g), resident(wg), resident(wu), resident(wd),
                  resident(fin_g)],
        out_specs=pl.BlockSpec((tm, d), row),
        compiler_params=_params("parallel"),
        name="ffn",
    )(x, norm_g, wg, wu, wd, fin_g)


LOG2E = 1.4426950408889634
N_FEAT = 6


def _bf16_pieces(x):
    x = np.asarray(x, np.float32)
    s1 = x.astype(BF16).astype(np.float32)
    s2 = (x - s1).astype(BF16).astype(np.float32)
    s3 = (x - s1 - s2).astype(BF16).astype(np.float32)
    return s1, s2, s3


def _query_features():
    slopes = np.float32(2.0) ** (-ALIBI_MAX_BIAS * np.arange(1, N_HEADS + 1, dtype=np.float32) / N_HEADS)
    s1, s2, s3 = _bf16_pieces(slopes * np.float32(LOG2E))
    feat = np.zeros((N_HEADS, LANES), np.float32)
    feat[:, HEAD_DIM:HEAD_DIM + N_FEAT] = np.stack([s1, s2, s3, SEL_LEN * s1, SEL_LEN * s2, SEL_LEN * s3], axis=1)
    return jnp.asarray(feat)


def _key_features(pos, width, offset):
    pos = np.asarray(pos)
    a, b = (pos // SEL_LEN).astype(np.float32), (pos % SEL_LEN).astype(np.float32)
    feat = np.zeros((len(pos), width), np.float32)
    feat[:, offset:offset + N_FEAT] = np.stack([b, b, b, a, a, a], axis=1)
    return jnp.asarray(feat)


Q_COLS = N_HEADS * LANES
CMP_OFF = Q_COLS
KEY_OFF = CMP_OFF + 2 * LANES
VAL_OFF = KEY_OFF + 2 * N_GROUPS * LANES
GATE_OFF = VAL_OFF + 2 * LANES
POOL_OFF = GATE_OFF + LANES
POOL_COLS = len(POOL_WINDOWS) * POOL_GROUP_DIM
MERGE_OFF = POOL_OFF + POOL_COLS


V_ROWS = HEAD_DIM + 16


def _store_values_t(v_ref, v):
    vt = v.T
    for g in range(N_GROUPS):
        v_ref[0, g, 0:HEAD_DIM, :] = vt[g * HEAD_DIM:(g + 1) * HEAD_DIM, :].astype(BF16)
        v_ref[0, g, HEAD_DIM:V_ROWS, :] = jnp.ones((V_ROWS - HEAD_DIM, v.shape[0]), BF16)


def _proj_kernel(x_ref, g_ref, w_ref, qf_ref, kf_ref, q_ref, kc_ref, vc_ref, ks_ref, kw_ref, vs_ref, vw_ref,
                 gate_ref, u_ref, gm_ref):
    d = x_ref.shape[1]
    hn = _rms(x_ref[...], g_ref[...]).astype(BF16)
    q_scale = HEAD_DIM ** -0.5 * LOG2E
    for h in range(0, N_HEADS, 2):
        q2 = _dot(hn, w_ref[:, h * LANES:(h + 2) * LANES]) * q_scale
        for i in range(2):
            q_ref[0, h + i] = (q2[:, i * LANES:(i + 1) * LANES] + qf_ref[h + i:h + i + 1, :]).astype(BF16)
    cmp_in = _dot(hn, w_ref[:, CMP_OFF:CMP_OFF + 2 * LANES])
    kc_ref[...] = cmp_in[:, 0:LANES]
    vc_ref[...] = cmp_in[:, LANES:2 * LANES]
    keys = _dot(hn, w_ref[:, KEY_OFF:KEY_OFF + 2 * N_GROUPS * LANES])
    kf = kf_ref[...]
    for g in range(N_GROUPS):
        ks_ref[0, g] = (keys[:, g * LANES:(g + 1) * LANES] + kf).astype(BF16)
        kw_ref[0, g] = (keys[:, (N_GROUPS + g) * LANES:(N_GROUPS + g + 1) * LANES] + kf).astype(BF16)
    vals = _dot(hn, w_ref[:, VAL_OFF:VAL_OFF + 2 * LANES])
    for i, v_ref in enumerate((vs_ref, vw_ref)):
        _store_values_t(v_ref, vals[:, i * LANES:(i + 1) * LANES])
    gate_ref[...] = jax.nn.sigmoid(_dot(hn, w_ref[:, GATE_OFF:GATE_OFF + LANES]))
    u_ref[...] = _dot(hn, w_ref[:, POOL_OFF:POOL_OFF + POOL_COLS])
    gm_ref[...] = jax.nn.sigmoid(_dot(hn, w_ref[:, MERGE_OFF:MERGE_OFF + 2 * d]))


def _pack_w_in(w_in, d):
    w_in = w_in.astype(BF16)
    qw = N_HEADS * HEAD_DIM
    kvw = N_GROUPS * HEAD_DIM

    def padded(w, n):
        w = w.reshape(d, n, HEAD_DIM)
        return jnp.concatenate([w, jnp.zeros_like(w)], axis=-1).reshape(d, n * LANES)

    kc, vc, ks, vs, kw, vw = (w_in[:, qw + i * kvw:qw + (i + 1) * kvw] for i in range(6))
    off = qw + 6 * kvw
    n_g = 3 * N_HEADS
    w_g = jnp.pad(w_in[:, off:off + n_g], ((0, 0), (0, LANES - n_g)))
    w_rest = w_in[:, off + n_g:]
    return jnp.concatenate([padded(w_in[:, :qw], N_HEADS), kc, vc, padded(ks, N_GROUPS), padded(kw, N_GROUPS),
                            vs, vw, w_g, w_rest], axis=1)


def _proj(x, norm_g, w_packed, batch, seq, *, tm=512):
    n, d = x.shape
    tiles_per_seq = seq // tm
    row = lambda i: (i, 0)
    const = lambda i: (0, 0)
    per_group = lambda i: (i // tiles_per_seq, 0, i % tiles_per_seq, 0)
    qf = _query_features()
    kf = _key_features(np.arange(seq), LANES, HEAD_DIM)
    flat = lambda width, dtype: (jax.ShapeDtypeStruct((n, width), dtype), pl.BlockSpec((tm, width), row))
    grouped = lambda count: (jax.ShapeDtypeStruct((batch, count, seq, LANES), BF16),
                             pl.BlockSpec((1, count, tm, LANES), per_group))
    values_t = (jax.ShapeDtypeStruct((batch, N_GROUPS, V_ROWS, seq), BF16),
                pl.BlockSpec((1, N_GROUPS, V_ROWS, tm), lambda i: (i // tiles_per_seq, 0, 0, i % tiles_per_seq)))
    outs = [grouped(N_HEADS), flat(LANES, F32), flat(LANES, F32), grouped(N_GROUPS), grouped(N_GROUPS),
            values_t, values_t, flat(LANES, F32), flat(POOL_COLS, F32), flat(2 * d, F32)]
    return pl.pallas_call(
        _proj_kernel,
        out_shape=[o[0] for o in outs],
        grid=(n // tm,),
        in_specs=[pl.BlockSpec((tm, d), row), pl.BlockSpec((1, d), const),
                  pl.BlockSpec(w_packed.shape, const, pipeline_mode=pl.Buffered(1)),
                  pl.BlockSpec(qf.shape, const), pl.BlockSpec((tm, LANES), lambda i: (i % tiles_per_seq, 0))],
        out_specs=[o[1] for o in outs],
        compiler_params=_params("parallel"),
        name="proj",
    )(x, norm_g, w_packed, qf, kf)


def _compress_kernel(k_ref, v_ref, pos_ref, wk1_ref, wk2_ref, wv1_ref, wv2_ref, feat_ref, ko_ref, vo_ref):
    n_chunk = k_ref.shape[0] // CMP_STRIDE
    hidden = wk2_ref.shape[0]

    def compress(x_ref, w1_ref, w2_ref):
        first = second = None
        for r in range(CMP_STRIDE):
            rows = x_ref[pl.ds(r, n_chunk, stride=CMP_STRIDE), :]
            a = _dot((rows + pos_ref[r:r + 1, :]).astype(BF16), w1_ref[r])
            b = _dot((rows + pos_ref[CMP_STRIDE + r:CMP_STRIDE + r + 1, :]).astype(BF16), w1_ref[CMP_STRIDE + r])
            first = a if first is None else first + a
            second = b if second is None else second + b
        act = jax.nn.gelu(first + pltpu.roll(second, n_chunk - 1, 0)).astype(BF16)
        return [_dot(act[:, g * hidden:(g + 1) * hidden], w2_ref[...]) for g in range(N_GROUPS)]

    for g, k_cmp in enumerate(compress(k_ref, wk1_ref, wk2_ref)):
        ko_ref[0, g] = jnp.concatenate([k_cmp, feat_ref[...]], axis=-1).astype(BF16)
    _store_values_t(vo_ref, jnp.concatenate(compress(v_ref, wv1_ref, wv2_ref), axis=-1))


def _block_diag_w1(w1):
    w = w1.reshape(CMP_LEN, HEAD_DIM, -1)
    z = jnp.zeros_like(w)
    return jnp.concatenate([jnp.concatenate([w, z], axis=2), jnp.concatenate([z, w], axis=2)], axis=1).astype(BF16)


def _compress(kc, vc, pos, wk1, wk2, wv1, wv2, batch, seq):
    n_chunk = seq // CMP_STRIDE
    rows = pl.BlockSpec((seq, LANES), lambda b: (b, 0))
    full = lambda a: pl.BlockSpec(a.shape, lambda b: (0,) * a.ndim, pipeline_mode=pl.Buffered(1))
    pos2 = jnp.concatenate([pos] * N_GROUPS, axis=1)
    feat = _key_features(np.arange(n_chunk) * CMP_STRIDE + CMP_LEN - 1, HEAD_DIM, 0)
    args = (pos2, _block_diag_w1(wk1), wk2.astype(BF16), _block_diag_w1(wv1), wv2.astype(BF16), feat)
    return pl.pallas_call(
        _compress_kernel,
        out_shape=[jax.ShapeDtypeStruct((batch, N_GROUPS, n_chunk, LANES), BF16),
                   jax.ShapeDtypeStruct((batch, N_GROUPS, V_ROWS, n_chunk), BF16)],
        grid=(batch,),
        in_specs=[rows, rows] + [full(a) for a in args],
        out_specs=[pl.BlockSpec((1, N_GROUPS, n_chunk, LANES), lambda b: (b, 0, 0, 0)),
                   pl.BlockSpec((1, N_GROUPS, V_ROWS, n_chunk), lambda b: (b, 0, 0, 0))],
        compiler_params=_params("parallel"),
        name="compress",
    )(kc, vc, *args)


SLC_CHUNK = 256
BLOCKS_PER_CHUNK = SLC_CHUNK // SEL_LEN
HEAD_BLOCKS = 2


def _nsa_kernel(q_ref, kc_ref, vct_ref, ks_ref, vst_ref, kw_ref, vwt_ref, gate_ref, ovt_ref, o_ref,
                st_ref, p_ref, gt_ref, selt_ref, list_ref, m_ref, acc_ref, out_ref, *, tq, seq):
    n_sel = seq // SEL_LEN
    n_chunks = seq // SLC_CHUNK
    n_cmp = kc_ref.shape[2]
    n_win = WINDOW + tq
    q0 = pl.program_id(1) * tq
    gt_ref[...] = gate_ref[...].T
    pairs = [(g, g * HEADS_PER_GROUP + 2 * j) for g in range(N_GROUPS) for j in range(HEADS_PER_GROUP // 2)]

    def gate_row(head, branch):
        c = 3 * head + branch
        return gt_ref[c:c + 1, :]

    def out_rows(head):
        g, hh = divmod(head, HEADS_PER_GROUP)
        return hh, slice(g * HEAD_DIM, (g + 1) * HEAD_DIM)

    def stage_scores(keys, n_keys, row0=0):
        for g, head0 in pairs:
            q_pair = q_ref[0, head0:head0 + 2].reshape(2 * tq, LANES)
            scores = _dot_nt(keys(g), q_pair)
            for i in range(2):
                st_ref[head0 + i, row0:row0 + n_keys, :] = scores[:, i * tq:(i + 1) * tq]

    def finish_pair(head0, pv, branch, first, ok=None):
        invs = []
        for i in range(2):
            cols = slice(i * tq, (i + 1) * tq)
            inv = 1.0 / pv[HEAD_DIM:HEAD_DIM + 1, cols]
            if ok is not None:
                inv = jnp.where(ok, inv, 0.0)
            hh, rows = out_rows(head0 + i)
            contrib = pv[0:HEAD_DIM, cols] * (inv * gate_row(head0 + i, branch))
            out_ref[hh, rows, :] = contrib if first else out_ref[hh, rows, :] + contrib
            invs.append(inv)
        return invs

    t_c = q0 + lax.broadcasted_iota(jnp.int32, (n_cmp, tq), 1)
    end_c = lax.broadcasted_iota(jnp.int32, (n_cmp, tq), 0) * CMP_STRIDE + (CMP_LEN - 1)
    bias_c = jnp.where(t_c >= end_c, 0.0, NEG_BIG)
    has_cmp = (q0 + lax.broadcasted_iota(jnp.int32, (1, tq), 1)) >= CMP_LEN - 1
    inv_c = [None] * N_HEADS

    def cmp_pair(g, head0):
        for i in range(2):
            x = st_ref[head0 + i, 0:n_cmp, :] + bias_c
            p_ref[head0 // 2, 0:n_cmp, i * tq:(i + 1) * tq] = jnp.exp2(x - jnp.max(x, axis=0, keepdims=True)).astype(BF16)
        pv = _dot(vct_ref[0, g], p_ref[head0 // 2, 0:n_cmp, :])
        inv_c[head0], inv_c[head0 + 1] = finish_pair(head0, pv, 0, True, has_cmp)

    stage_scores(lambda g: kc_ref[0, g], n_cmp)
    for g, head0 in pairs:
        cmp_pair(g, head0)

    raw = [_dot(ovt_ref[...], p_ref[pair, 0:n_cmp, :]) for pair in range(N_HEADS // 2)]

    w0 = pl.multiple_of(jnp.maximum(q0 - WINDOW, 0), tq)
    stage_scores(lambda g: kw_ref[0, g, pl.ds(w0, n_win), :], n_win)
    head_keys = HEAD_BLOCKS * SEL_LEN
    stage_scores(lambda g: ks_ref[0, g, 0:head_keys, :], head_keys, n_win)

    blk = lax.broadcasted_iota(jnp.int32, (n_sel, tq), 0)
    cur = (q0 + lax.broadcasted_iota(jnp.int32, (n_sel, tq), 1)) // SEL_LEN
    valid = blk <= cur
    forced = valid & ((blk == 0) | (blk == cur) | (blk == cur - 1))
    blk_f = blk.astype(F32)
    score = []
    for g in range(N_GROUPS):
        total = None
        for head in range(g * HEADS_PER_GROUP, (g + 1) * HEADS_PER_GROUP):
            part = raw[head // 2][:, (head % 2) * tq:(head % 2 + 1) * tq] * inv_c[head]
            total = part if total is None else total + part
        score.append(jnp.where(forced, TOPK_BIG, jnp.where(valid, total, -TOPK_BIG)))

    for _ in range(min(N_SELECT, n_sel)):
        for g in range(N_GROUPS):
            best = jnp.max(score[g], axis=0, keepdims=True)
            pick = jnp.min(jnp.where(score[g] == best, blk_f, float(n_sel)), axis=0, keepdims=True)
            score[g] = jnp.where(blk_f == pick, TOPK_TAKEN, score[g])
    sel = [jnp.where(valid & (score[g] == TOPK_TAKEN), 1.0, 0.0) for g in range(N_GROUPS)]
    for g in range(N_GROUPS):
        selt_ref[g] = jnp.where(blk >= HEAD_BLOCKS, sel[g], 0.0)
    sel_any = jnp.where(blk >= HEAD_BLOCKS, sel[0] + sel[1], 0.0)
    n_items = jnp.int32(0)
    for c in range(n_chunks):
        any_sel = jnp.max(sel_any[c * BLOCKS_PER_CHUNK:(c + 1) * BLOCKS_PER_CHUNK, :])
        list_ref[n_items] = c
        n_items = n_items + (any_sel > 0.5).astype(jnp.int32)

    causal_h = (q0 + lax.broadcasted_iota(jnp.int32, (head_keys, tq), 1)) >= lax.broadcasted_iota(jnp.int32, (head_keys, tq), 0)
    head_rows = slice(n_win, n_win + head_keys)
    for g, head0 in pairs:
        if head0 % HEADS_PER_GROUP == 0:
            sel_keys = jnp.concatenate([jnp.broadcast_to(sel[g][i:i + 1, :], (SEL_LEN, tq)) for i in range(HEAD_BLOCKS)], axis=0)
            bias = jnp.where(causal_h & (sel_keys > 0.5), 0.0, NEG_BIG)
        for i in range(2):
            x = st_ref[head0 + i, head_rows, :] + bias
            m_new = jnp.max(x, axis=0, keepdims=True)
            m_ref[head0 + i:head0 + i + 1, :] = m_new
            p_ref[head0 // 2, head_rows, i * tq:(i + 1) * tq] = jnp.exp2(x - m_new).astype(BF16)
    for g, head0 in pairs:
        acc_ref[head0 // 2] = _dot(vst_ref[0, g, :, 0:head_keys], p_ref[head0 // 2, head_rows, :])

    dist_w = (q0 + lax.broadcasted_iota(jnp.int32, (n_win, tq), 1)) - (w0 + lax.broadcasted_iota(jnp.int32, (n_win, tq), 0))
    bias_w = jnp.where((dist_w >= 0) & (dist_w < WINDOW), 0.0, NEG_BIG)

    def win_pair(g, head0):
        for i in range(2):
            x = st_ref[head0 + i, 0:n_win, :] + bias_w
            p_ref[head0 // 2, 0:n_win, i * tq:(i + 1) * tq] = jnp.exp2(x - jnp.max(x, axis=0, keepdims=True)).astype(BF16)
        pv = _dot(vwt_ref[0, g, :, pl.ds(w0, n_win)], p_ref[head0 // 2, 0:n_win, :])
        finish_pair(head0, pv, 2, False)

    for g, head0 in pairs:
        win_pair(g, head0)

    base = (lax.broadcasted_iota(jnp.int32, (SLC_CHUNK, tq), 1)
            - lax.broadcasted_iota(jnp.int32, (SLC_CHUNK, tq), 0))

    def stage_chunk(item, row0):
        c = list_ref[item]
        k0 = pl.multiple_of(c * SLC_CHUNK, SLC_CHUNK)
        causal = base >= k0 - q0
        for g, head0 in pairs:
            if head0 % HEADS_PER_GROUP == 0:
                sel_keys = jnp.concatenate(
                    [jnp.broadcast_to(selt_ref[g, pl.ds(c * BLOCKS_PER_CHUNK + i, 1), :], (SEL_LEN, tq))
                     for i in range(BLOCKS_PER_CHUNK)], axis=0)
                bias = jnp.where(causal & (sel_keys > 0.5), 0.0, NEG_BIG)
                bias2 = jnp.concatenate([bias, bias], axis=1)
            q_pair = q_ref[0, head0:head0 + 2].reshape(2 * tq, LANES)
            scores = _dot_nt(ks_ref[0, g, pl.ds(k0, SLC_CHUNK), :], q_pair) + bias2
            for i in range(2):
                st_ref[head0 + i, row0:row0 + SLC_CHUNK, :] = scores[:, i * tq:(i + 1) * tq]

    def consume_chunk(item, row0):
        k0 = pl.multiple_of(list_ref[item] * SLC_CHUNK, SLC_CHUNK)
        for g, head0 in pairs:
            alphas = []
            for i, head in enumerate((head0, head0 + 1)):
                x = st_ref[head, row0:row0 + SLC_CHUNK, :]
                m_old = m_ref[head:head + 1, :]
                m_new = jnp.maximum(m_old, jnp.max(x, axis=0, keepdims=True))
                alphas.append(jnp.exp2(m_old - m_new))
                m_ref[head:head + 1, :] = m_new
                p_ref[head0 // 2, 0:SLC_CHUNK, i * tq:(i + 1) * tq] = jnp.exp2(x - m_new).astype(BF16)
            pv = _dot(vst_ref[0, g, :, pl.ds(k0, SLC_CHUNK)], p_ref[head0 // 2, 0:SLC_CHUNK, :])
            acc_ref[head0 // 2] = acc_ref[head0 // 2] * jnp.concatenate(alphas, axis=1) + pv

    last = jnp.maximum(n_items - 1, 0)
    stage_chunk(0, 0)

    def two_chunks(j, carry):
        stage_chunk(jnp.minimum(2 * j + 1, last), SLC_CHUNK)
        consume_chunk(2 * j, 0)

        @pl.when(2 * j + 1 < n_items)
        def _():
            stage_chunk(jnp.minimum(2 * j + 2, last), 0)
            consume_chunk(2 * j + 1, SLC_CHUNK)
        return carry

    lax.fori_loop(0, (n_items + 1) // 2, two_chunks, 0)
    for g, head0 in pairs:
        finish_pair(head0, acc_ref[head0 // 2], 1, False)

    for hh in range(HEADS_PER_GROUP):
        o_ref[:, hh * LANES:(hh + 1) * LANES] = out_ref[hh].T.astype(BF16)


def _overlap_t(seq):
    n_cmp = (seq - CMP_LEN) // CMP_STRIDE + 1
    n_sel = seq // SEL_LEN
    cs = np.arange(n_cmp)[:, None] * CMP_STRIDE
    ss = np.arange(n_sel)[None, :] * SEL_LEN
    ov = np.clip(np.minimum(cs + CMP_LEN, ss + SEL_LEN) - np.maximum(cs, ss), 0, None) / CMP_LEN
    out = np.zeros((n_sel, seq // CMP_STRIDE), np.float32)
    out[:, :n_cmp] = ov.T
    return jnp.asarray(out, dtype=BF16)


def _nsa(q, kc, vct, ks, vst, kw, vwt, gates, *, tq=128):
    batch, _, seq, _ = q.shape
    n_cmp = kc.shape[2]
    n_sel = seq // SEL_LEN
    assert seq % SLC_CHUNK == 0 and seq >= WINDOW + tq and SLC_CHUNK % tq == 0 and tq == LANES
    assert n_sel <= 2 * SEL_LEN
    ovt = _overlap_t(seq)
    keys = lambda n: pl.BlockSpec((1, N_GROUPS, n, LANES), lambda b, i: (b, 0, 0, 0))
    vals = lambda n: pl.BlockSpec((1, N_GROUPS, V_ROWS, n), lambda b, i: (b, 0, 0, 0))
    tiles = seq // tq
    s_rows = max(n_cmp, WINDOW + tq + HEAD_BLOCKS * SEL_LEN, 2 * SLC_CHUNK)
    return pl.pallas_call(
        functools.partial(_nsa_kernel, tq=tq, seq=seq),
        out_shape=jax.ShapeDtypeStruct((batch * seq, HEADS_PER_GROUP * LANES), BF16),
        grid=(batch, tiles),
        in_specs=[
            pl.BlockSpec((1, N_HEADS, tq, LANES), lambda b, i: (b, 0, i, 0)),
            keys(n_cmp), vals(n_cmp), keys(seq), vals(seq), keys(seq), vals(seq),
            pl.BlockSpec((tq, LANES), lambda b, i: (b * tiles + i, 0)),
            pl.BlockSpec(ovt.shape, lambda b, i: (0, 0)),
        ],
        out_specs=pl.BlockSpec((tq, HEADS_PER_GROUP * LANES), lambda b, i: (b * tiles + i, 0)),
        scratch_shapes=[
            pltpu.VMEM((N_HEADS, s_rows, tq), F32),
            pltpu.VMEM((N_HEADS // 2, s_rows, 2 * tq), BF16),
            pltpu.VMEM((LANES, tq), F32),
            pltpu.VMEM((N_GROUPS, n_sel, tq), F32),
            pltpu.SMEM((seq // SLC_CHUNK,), jnp.int32),
            pltpu.VMEM((N_HEADS, tq), F32),
            pltpu.VMEM((N_HEADS // 2, V_ROWS, 2 * tq), F32),
            pltpu.VMEM((HEADS_PER_GROUP, N_GROUPS * HEAD_DIM, tq), F32),
        ],
        compiler_params=_params("parallel", "arbitrary"),
        name="nsa",
    )(q, kc, vct, ks, vst, kw, vwt, gates, ovt)


HALO = max(POOL_WINDOWS)


def _merge_kernel(x_ref, u_ref, halo_ref, gm_ref, on_ref, pw_ref, ps_ref, wbp_ref, wbn_ref, wo_ref,
                  o_ref, ext_ref, *, tm, seq):
    d = x_ref.shape[1]
    pos0 = (pl.program_id(0) * tm) % seq
    ext_ref[0:HALO, :] = jnp.where(pos0 == 0, 0.0, halo_ref[...])
    ext_ref[HALO:HALO + tm, :] = u_ref[...]
    pos = (pos0 + lax.broadcasted_iota(jnp.int32, (tm, POOL_GROUP_DIM), 0)).astype(F32)
    mixed = []
    for gi, w in enumerate(POOL_WINDOWS):
        cols = slice(gi * POOL_GROUP_DIM, (gi + 1) * POOL_GROUP_DIM)
        u = ext_ref[HALO:HALO + tm, cols]
        total = u
        for lag in range(1, w):
            total = total + ext_ref[HALO - lag:HALO - lag + tm, cols]
        delta = total / jnp.minimum(pos + 1.0, float(w)) - u
        mixed.append(_dot(delta.astype(BF16), pw_ref[gi]) * ps_ref[:, cols])
    mixed = jnp.concatenate(mixed, axis=-1).astype(BF16)
    a = _dot(mixed, wbp_ref[...])
    b = _dot(on_ref[...], wbn_ref[...])
    merged = gm_ref[:, 0:d] * a + gm_ref[:, d:2 * d] * b
    o_ref[...] = x_ref[...] + _dot(merged.astype(BF16), wo_ref[...])


def _merge(x, u, gm, o_nsa, pool_w, pool_scale, w_bp, w_bn, w_out, seq, *, tm=512):
    n, d = x.shape
    row = lambda i: (i, 0)
    full = lambda a: pl.BlockSpec(a.shape, lambda i: (0,) * a.ndim, pipeline_mode=pl.Buffered(1))
    halo_blocks = tm // HALO
    return pl.pallas_call(
        functools.partial(_merge_kernel, tm=tm, seq=seq),
        out_shape=jax.ShapeDtypeStruct((n, d), F32),
        grid=(n // tm,),
        in_specs=[
            pl.BlockSpec((tm, d), row),
            pl.BlockSpec((tm, POOL_COLS), row),
            pl.BlockSpec((HALO, POOL_COLS), lambda i: (jnp.maximum(i * halo_blocks - 1, 0), 0)),
            pl.BlockSpec((tm, 2 * d), row),
            pl.BlockSpec((tm, d), row),
            full(pool_w), full(pool_scale), full(w_bp), full(w_bn), full(w_out),
        ],
        out_specs=pl.BlockSpec((tm, d), row),
        scratch_shapes=[pltpu.VMEM((HALO + tm, POOL_COLS), F32)],
        compiler_params=_params("parallel"),
        name="merge",
    )(x, u, u, gm, o_nsa, pool_w, pool_scale, w_bp, w_bn, w_out)


def kernel(x, ffn1_norm, ffn1_w_gate, ffn1_w_up, ffn1_w_down, mix_norm, w_in, cmp_pos, cmp_k_w1, cmp_k_w2, cmp_v_w1, cmp_v_w2, pool_w, pool_scale, w_branch_pool, w_branch_nsa, w_out, ffn2_norm, ffn2_w_gate, ffn2_w_up, ffn2_w_down, final_norm):
    batch, seq, d = x.shape
    depth = w_in.shape[0]
    xf = x.reshape(batch * seq, d)
    bf = lambda a: a.astype(BF16)
    row = lambda a: a.reshape(1, -1)
    for l in range(depth):
        xf = _ffn(xf, row(ffn1_norm[l]), bf(ffn1_w_gate[l]), bf(ffn1_w_up[l]), bf(ffn1_w_down[l]),
                  row(final_norm), final_norm=False)

        q, kc, vc, ks, kw, vs, vw, gates, u, gm = _proj(xf, row(mix_norm[l]), _pack_w_in(w_in[l], d), batch, seq)
        k_cmp, v_cmp = _compress(kc, vc, cmp_pos[l], cmp_k_w1[l], cmp_k_w2[l], cmp_v_w1[l], cmp_v_w2[l], batch, seq)
        o_nsa = _nsa(q, k_cmp, v_cmp, ks, vs, kw, vw, gates)

        w_bn = w_branch_nsa[l].reshape(N_GROUPS, HEADS_PER_GROUP, HEAD_DIM, d).transpose(1, 0, 2, 3).reshape(-1, d)
        xf = _merge(xf, u, gm, o_nsa, bf(pool_w[l]), row(pool_scale[l]), bf(w_branch_pool[l]), bf(w_bn),
                    bf(w_out[l]), seq)

        xf = _ffn(xf, row(ffn2_norm[l]), bf(ffn2_w_gate[l]), bf(ffn2_w_up[l]), bf(ffn2_w_down[l]),
                  row(final_norm), final_norm=(l == depth - 1))
    return xf.reshape(batch, seq, d)
```

```python
import functools

import jax
import jax.numpy as jnp
import numpy as np
from jax import lax
from jax.experimental import pallas as pl
from jax.experimental.pallas import tpu as pltpu

N_HEADS = 16
N_GROUPS = 2
HEADS_PER_GROUP = N_HEADS // N_GROUPS
HEAD_DIM = 64
CMP_LEN = 32
CMP_STRIDE = 16
SEL_LEN = 64
N_SELECT = 16
WINDOW = 512
POOL_WINDOWS = (2, 4, 8, 16)
POOL_GROUP_DIM = 128
RMS_EPS = 1e-6
ALIBI_MAX_BIAS = 8.0

LANES = 128
NEG_BIG = -1e30
TOPK_BIG = 1e30
N_FORCED = 3
TOPK_TAKEN = -float(2 ** 101)
VMEM_LIMIT = 48 * 1024 * 1024

BF16 = jnp.bfloat16
F32 = jnp.float32


def _dot(a, b):
    return jnp.dot(a, b, preferred_element_type=F32)


def _dot_nt(a, b):
    return lax.dot_general(a, b, (((1,), (1,)), ((), ())), preferred_element_type=F32)


def _rms(x, g):
    return x * lax.rsqrt(jnp.mean(x * x, axis=-1, keepdims=True) + RMS_EPS) * g


def _params(*sem):
    return pltpu.CompilerParams(dimension_semantics=sem, vmem_limit_bytes=VMEM_LIMIT)


def _ffn_kernel(x_ref, g_ref, wg_ref, wu_ref, wd_ref, fin_ref, o_ref, *, final_norm):
    x = x_ref[...]
    xn = _rms(x, g_ref[...]).astype(BF16)
    gate = _dot(xn, wg_ref[...])
    up = _dot(xn, wu_ref[...])
    act = (gate * jax.nn.sigmoid(gate)) * up
    y = x + 0.5 * _dot(act.astype(BF16), wd_ref[...])
    if final_norm:
        y = _rms(y, fin_ref[...])
    o_ref[...] = y


def _ffn(x, norm_g, wg, wu, wd, fin_g, *, final_norm, tm=512):
    n, d = x.shape
    row = lambda i: (i, 0)
    resident = lambda a: pl.BlockSpec(a.shape, lambda i: (0, 0), pipeline_mode=pl.Buffered(1))
    return pl.pallas_call(
        functools.partial(_ffn_kernel, final_norm=final_norm),
        out_shape=jax.ShapeDtypeStruct((n, d), F32),
        grid=(n // tm,),
        in_specs=[pl.BlockSpec((tm, d), row), resident(norm_g), resident(wg), resident(wu), resident(wd),
                  resident(fin_g)],
        out_specs=pl.BlockSpec((tm, d), row),
        compiler_params=_params("parallel"),
        name="ffn",
    )(x, norm_g, wg, wu, wd, fin_g)


LOG2E = 1.4426950408889634
N_FEAT = 6


def _bf16_pieces(x):
    x = np.asarray(x, np.float32)
    s1 = x.astype(BF16).astype(np.float32)
    s2 = (x - s1).astype(BF16).astype(np.float32)
    s3 = (x - s1 - s2).astype(BF16).astype(np.float32)
    return s1, s2, s3


def _query_features():
    slopes = np.float32(2.0) ** (-ALIBI_MAX_BIAS * np.arange(1, N_HEADS + 1, dtype=np.float32) / N_HEADS)
    s1, s2, s3 = _bf16_pieces(slopes * np.float32(LOG2E))
    feat = np.zeros((N_HEADS, LANES), np.float32)
    feat[:, HEAD_DIM:HEAD_DIM + N_FEAT] = np.stack([s1, s2, s3, SEL_LEN * s1, SEL_LEN * s2, SEL_LEN * s3], axis=1)
    return jnp.asarray(feat)


def _key_features(pos, width, offset):
    pos = np.asarray(pos)
    a, b = (pos // SEL_LEN).astype(np.float32), (pos % SEL_LEN).astype(np.float32)
    feat = np.zeros((len(pos), width), np.float32)
    feat[:, offset:offset + N_FEAT] = np.stack([b, b, b, a, a, a], axis=1)
    return jnp.asarray(feat)


Q_COLS = N_HEADS * LANES
CMP_OFF = Q_COLS
KEY_OFF = CMP_OFF + 2 * LANES
VAL_OFF = KEY_OFF + 2 * N_GROUPS * LANES
GATE_OFF = VAL_OFF + 2 * LANES
POOL_OFF = GATE_OFF + LANES
POOL_COLS = len(POOL_WINDOWS) * POOL_GROUP_DIM
MERGE_OFF = POOL_OFF + POOL_COLS


V_ROWS = HEAD_DIM + 16


def _store_values_t(v_ref, v):
    vt = v.T
    for g in range(N_GROUPS):
        v_ref[0, g, 0:HEAD_DIM, :] = vt[g * HEAD_DIM:(g + 1) * HEAD_DIM, :].astype(BF16)
        v_ref[0, g, HEAD_DIM:V_ROWS, :] = jnp.ones((V_ROWS - HEAD_DIM, v.shape[0]), BF16)


def _proj_kernel(x_ref, g_ref, w_ref, qf_ref, kf_ref, q_ref, kc_ref, vc_ref, ks_ref, kw_ref, vs_ref, vw_ref,
                 gate_ref, u_ref, gm_ref):
    d = x_ref.shape[1]
    hn = _rms(x_ref[...], g_ref[...]).astype(BF16)
    q_scale = HEAD_DIM ** -0.5 * LOG2E
    for h in range(0, N_HEADS, 2):
        q2 = _dot(hn, w_ref[:, h * LANES:(h + 2) * LANES]) * q_scale
        for i in range(2):
            q_ref[0, h + i] = (q2[:, i * LANES:(i + 1) * LANES] + qf_ref[h + i:h + i + 1, :]).astype(BF16)
    cmp_in = _dot(hn, w_ref[:, CMP_OFF:CMP_OFF + 2 * LANES])
    kc_ref[...] = cmp_in[:, 0:LANES]
    vc_ref[...] = cmp_in[:, LANES:2 * LANES]
    keys = _dot(hn, w_ref[:, KEY_OFF:KEY_OFF + 2 * N_GROUPS * LANES])
    kf = kf_ref[...]
    for g in range(N_GROUPS):
        ks_ref[0, g] = (keys[:, g * LANES:(g + 1) * LANES] + kf).astype(BF16)
        kw_ref[0, g] = (keys[:, (N_GROUPS + g) * LANES:(N_GROUPS + g + 1) * LANES] + kf).astype(BF16)
    vals = _dot(hn, w_ref[:, VAL_OFF:VAL_OFF + 2 * LANES])
    for i, v_ref in enumerate((vs_ref, vw_ref)):
        _store_values_t(v_ref, vals[:, i * LANES:(i + 1) * LANES])
    gate_ref[...] = jax.nn.sigmoid(_dot(hn, w_ref[:, GATE_OFF:GATE_OFF + LANES]))
    u_ref[...] = _dot(hn, w_ref[:, POOL_OFF:POOL_OFF + POOL_COLS])
    gm_ref[...] = jax.nn.sigmoid(_dot(hn, w_ref[:, MERGE_OFF:MERGE_OFF + 2 * d]))


def _pack_w_in(w_in, d):
    w_in = w_in.astype(BF16)
    qw = N_HEADS * HEAD_DIM
    kvw = N_GROUPS * HEAD_DIM

    def padded(w, n):
        w = w.reshape(d, n, HEAD_DIM)
        return jnp.concatenate([w, jnp.zeros_like(w)], axis=-1).reshape(d, n * LANES)

    kc, vc, ks, vs, kw, vw = (w_in[:, qw + i * kvw:qw + (i + 1) * kvw] for i in range(6))
    off = qw + 6 * kvw
    n_g = 3 * N_HEADS
    w_g = jnp.pad(w_in[:, off:off + n_g], ((0, 0), (0, LANES - n_g)))
    w_rest = w_in[:, off + n_g:]
    return jnp.concatenate([padded(w_in[:, :qw], N_HEADS), kc, vc, padded(ks, N_GROUPS), padded(kw, N_GROUPS),
                            vs, vw, w_g, w_rest], axis=1)


def _proj(x, norm_g, w_packed, batch, seq, *, tm=512):
    n, d = x.shape
    tiles_per_seq = seq // tm
    row = lambda i: (i, 0)
    const = lambda i: (0, 0)
    per_group = lambda i: (i // tiles_per_seq, 0, i % tiles_per_seq, 0)
    qf = _query_features()
    kf = _key_features(np.arange(seq), LANES, HEAD_DIM)
    flat = lambda width, dtype: (jax.ShapeDtypeStruct((n, width), dtype), pl.BlockSpec((tm, width), row))
    grouped = lambda count: (jax.ShapeDtypeStruct((batch, count, seq, LANES), BF16),
                             pl.BlockSpec((1, count, tm, LANES), per_group))
    values_t = (jax.ShapeDtypeStruct((batch, N_GROUPS, V_ROWS, seq), BF16),
                pl.BlockSpec((1, N_GROUPS, V_ROWS, tm), lambda i: (i // tiles_per_seq, 0, 0, i % tiles_per_seq)))
    outs = [grouped(N_HEADS), flat(LANES, F32), flat(LANES, F32), grouped(N_GROUPS), grouped(N_GROUPS),
            values_t, values_t, flat(LANES, F32), flat(POOL_COLS, F32), flat(2 * d, F32)]
    return pl.pallas_call(
        _proj_kernel,
        out_shape=[o[0] for o in outs],
        grid=(n // tm,),
        in_specs=[pl.BlockSpec((tm, d), row), pl.BlockSpec((1, d), const),
                  pl.BlockSpec(w_packed.shape, const, pipeline_mode=pl.Buffered(1)),
                  pl.BlockSpec(qf.shape, const), pl.BlockSpec((tm, LANES), lambda i: (i % tiles_per_seq, 0))],
        out_specs=[o[1] for o in outs],
        compiler_params=_params("parallel"),
        name="proj",
    )(x, norm_g, w_packed, qf, kf)


def _compress_kernel(k_ref, v_ref, pos_ref, wk1_ref, wk2_ref, wv1_ref, wv2_ref, feat_ref, ko_ref, vo_ref):
    n_chunk = k_ref.shape[0] // CMP_STRIDE
    hidden = wk2_ref.shape[0]

    def compress(x_ref, w1_ref, w2_ref):
        first = second = None
        for r in range(CMP_STRIDE):
            rows = x_ref[pl.ds(r, n_chunk, stride=CMP_STRIDE), :]
            a = _dot((rows + pos_ref[r:r + 1, :]).astype(BF16), w1_ref[r])
            b = _dot((rows + pos_ref[CMP_STRIDE + r:CMP_STRIDE + r + 1, :]).astype(BF16), w1_ref[CMP_STRIDE + r])
            first = a if first is None else first + a
            second = b if second is None else second + b
        act = jax.nn.gelu(first + pltpu.roll(second, n_chunk - 1, 0)).astype(BF16)
        return [_dot(act[:, g * hidden:(g + 1) * hidden], w2_ref[...]) for g in range(N_GROUPS)]

    for g, k_cmp in enumerate(compress(k_ref, wk1_ref, wk2_ref)):
        ko_ref[0, g] = jnp.concatenate([k_cmp, feat_ref[...]], axis=-1).astype(BF16)
    _store_values_t(vo_ref, jnp.concatenate(compress(v_ref, wv1_ref, wv2_ref), axis=-1))


def _block_diag_w1(w1):
    w = w1.reshape(CMP_LEN, HEAD_DIM, -1)
    z = jnp.zeros_like(w)
    return jnp.concatenate([jnp.concatenate([w, z], axis=2), jnp.concatenate([z, w], axis=2)], axis=1).astype(BF16)


def _compress(kc, vc, pos, wk1, wk2, wv1, wv2, batch, seq):
    n_chunk = seq // CMP_STRIDE
    rows = pl.BlockSpec((seq, LANES), lambda b: (b, 0))
    full = lambda a: pl.BlockSpec(a.shape, lambda b: (0,) * a.ndim, pipeline_mode=pl.Buffered(1))
    pos2 = jnp.concatenate([pos] * N_GROUPS, axis=1)
    feat = _key_features(np.arange(n_chunk) * CMP_STRIDE + CMP_LEN - 1, HEAD_DIM, 0)
    args = (pos2, _block_diag_w1(wk1), wk2.astype(BF16), _block_diag_w1(wv1), wv2.astype(BF16), feat)
    return pl.pallas_call(
        _compress_kernel,
        out_shape=[jax.ShapeDtypeStruct((batch, N_GROUPS, n_chunk, LANES), BF16),
                   jax.ShapeDtypeStruct((batch, N_GROUPS, V_ROWS, n_chunk), BF16)],
        grid=(batch,),
        in_specs=[rows, rows] + [full(a) for a in args],
        out_specs=[pl.BlockSpec((1, N_GROUPS, n_chunk, LANES), lambda b: (b, 0, 0, 0)),
                   pl.BlockSpec((1, N_GROUPS, V_ROWS, n_chunk), lambda b: (b, 0, 0, 0))],
        compiler_params=_params("parallel"),
        name="compress",
    )(kc, vc, *args)


SLC_CHUNK = 256
BLOCKS_PER_CHUNK = SLC_CHUNK // SEL_LEN
HEAD_BLOCKS = 2


def _nsa_kernel(q_ref, kc_ref, vct_ref, ks_ref, vst_ref, kw_ref, vwt_ref, gate_ref, ovt_ref, o_ref,
                st_ref, p_ref, gt_ref, selt_ref, list_ref, m_ref, acc_ref, out_ref, *, tq, seq):
    n_sel = seq // SEL_LEN
    n_chunks = seq // SLC_CHUNK
    n_cmp = kc_ref.shape[2]
    n_win = WINDOW + tq
    q0 = pl.program_id(1) * tq
    gt_ref[...] = gate_ref[...].T
    pairs = [(g, g * HEADS_PER_GROUP + 2 * j) for g in range(N_GROUPS) for j in range(HEADS_PER_GROUP // 2)]

    def gate_row(head, branch):
        c = 3 * head + branch
        return gt_ref[c:c + 1, :]

    def out_rows(head):
        g, hh = divmod(head, HEADS_PER_GROUP)
        return hh, slice(g * HEAD_DIM, (g + 1) * HEAD_DIM)

    def stage_scores(keys, n_keys, row0=0):
        for g, head0 in pairs:
            q_pair = q_ref[0, head0:head0 + 2].reshape(2 * tq, LANES)
            scores = _dot_nt(keys(g), q_pair)
            for i in range(2):
                st_ref[head0 + i, row0:row0 + n_keys, :] = scores[:, i * tq:(i + 1) * tq]

    def finish_pair(head0, pv, branch, first, ok=None):
        invs = []
        for i in range(2):
            cols = slice(i * tq, (i + 1) * tq)
            inv = 1.0 / pv[HEAD_DIM:HEAD_DIM + 1, cols]
            if ok is not None:
                inv = jnp.where(ok, inv, 0.0)
            hh, rows = out_rows(head0 + i)
            contrib = pv[0:HEAD_DIM, cols] * (inv * gate_row(head0 + i, branch))
            out_ref[hh, rows, :] = contrib if first else out_ref[hh, rows, :] + contrib
            invs.append(inv)
        return invs

    t_c = q0 + lax.broadcasted_iota(jnp.int32, (n_cmp, tq), 1)
    end_c = lax.broadcasted_iota(jnp.int32, (n_cmp, tq), 0) * CMP_STRIDE + (CMP_LEN - 1)
    bias_c = jnp.where(t_c >= end_c, 0.0, NEG_BIG)
    has_cmp = (q0 + lax.broadcasted_iota(jnp.int32, (1, tq), 1)) >= CMP_LEN - 1
    inv_c = [None] * N_HEADS

    def cmp_pair(g, head0):
        for i in range(2):
            x = st_ref[head0 + i, 0:n_cmp, :] + bias_c
            p_ref[head0 // 2, 0:n_cmp, i * tq:(i + 1) * tq] = jnp.exp2(x - jnp.max(x, axis=0, keepdims=True)).astype(BF16)
        pv = _dot(vct_ref[0, g], p_ref[head0 // 2, 0:n_cmp, :])
        inv_c[head0], inv_c[head0 + 1] = finish_pair(head0, pv, 0, True, has_cmp)

    stage_scores(lambda g: kc_ref[0, g], n_cmp)
    for g, head0 in pairs:
        cmp_pair(g, head0)

    raw = [_dot(ovt_ref[...], p_ref[pair, 0:n_cmp, :]) for pair in range(N_HEADS // 2)]

    w0 = pl.multiple_of(jnp.maximum(q0 - WINDOW, 0), tq)
    stage_scores(lambda g: kw_ref[0, g, pl.ds(w0, n_win), :], n_win)
    head_keys = HEAD_BLOCKS * SEL_LEN
    stage_scores(lambda g: ks_ref[0, g, 0:head_keys, :], head_keys, n_win)

    blk = lax.broadcasted_iota(jnp.int32, (n_sel, tq), 0)
    cur = (q0 + lax.broadcasted_iota(jnp.int32, (n_sel, tq), 1)) // SEL_LEN
    valid = blk <= cur
    forced = valid & ((blk == 0) | (blk == cur) | (blk == cur - 1))
    blk_f = blk.astype(F32)
    score = []
    for g in range(N_GROUPS):
        total = None
        for head in range(g * HEADS_PER_GROUP, (g + 1) * HEADS_PER_GROUP):
            part = raw[head // 2][:, (head % 2) * tq:(head % 2 + 1) * tq] * inv_c[head]
            total = part if total is None else total + part
        score.append(jnp.where(forced, TOPK_TAKEN, jnp.where(valid, total, -TOPK_BIG)))

    for _ in range(min(N_SELECT, n_sel) - N_FORCED):
        for g in range(N_GROUPS):
            best = jnp.max(score[g], axis=0, keepdims=True)
            pick = jnp.min(jnp.where(score[g] == best, blk_f, float(n_sel)), axis=0, keepdims=True)
            score[g] = jnp.where(blk_f == pick, TOPK_TAKEN, score[g])
    sel = [jnp.where(valid & (score[g] == TOPK_TAKEN), 1.0, 0.0) for g in range(N_GROUPS)]
    for g in range(N_GROUPS):
        selt_ref[g] = jnp.where(blk >= HEAD_BLOCKS, sel[g], 0.0)
    sel_any = jnp.where(blk >= HEAD_BLOCKS, sel[0] + sel[1], 0.0)
    n_items = jnp.int32(0)
    for c in range(n_chunks):
        any_sel = jnp.max(sel_any[c * BLOCKS_PER_CHUNK:(c + 1) * BLOCKS_PER_CHUNK, :])
        list_ref[n_items] = c
        n_items = n_items + (any_sel > 0.5).astype(jnp.int32)

    causal_h = (q0 + lax.broadcasted_iota(jnp.int32, (head_keys, tq), 1)) >= lax.broadcasted_iota(jnp.int32, (head_keys, tq), 0)
    head_rows = slice(n_win, n_win + head_keys)
    for g, head0 in pairs:
        if head0 % HEADS_PER_GROUP == 0:
            sel_keys = jnp.concatenate([jnp.broadcast_to(sel[g][i:i + 1, :], (SEL_LEN, tq)) for i in range(HEAD_BLOCKS)], axis=0)
            bias = jnp.where(causal_h & (sel_keys > 0.5), 0.0, NEG_BIG)
        for i in range(2):
            x = st_ref[head0 + i, head_rows, :] + bias
            m_new = jnp.max(x, axis=0, keepdims=True)
            m_ref[head0 + i:head0 + i + 1, :] = m_new
            p_ref[head0 // 2, head_rows, i * tq:(i + 1) * tq] = jnp.exp2(x - m_new).astype(BF16)
    for g, head0 in pairs:
        acc_ref[head0 // 2] = _dot(vst_ref[0, g, :, 0:head_keys], p_ref[head0 // 2, head_rows, :])

    dist_w = (q0 + lax.broadcasted_iota(jnp.int32, (n_win, tq), 1)) - (w0 + lax.broadcasted_iota(jnp.int32, (n_win, tq), 0))
    bias_w = jnp.where((dist_w >= 0) & (dist_w < WINDOW), 0.0, NEG_BIG)

    def win_pair(g, head0):
        for i in range(2):
            x = st_ref[head0 + i, 0:n_win, :] + bias_w
            p_ref[head0 // 2, 0:n_win, i * tq:(i + 1) * tq] = jnp.exp2(x - jnp.max(x, axis=0, keepdims=True)).astype(BF16)
        pv = _dot(vwt_ref[0, g, :, pl.ds(w0, n_win)], p_ref[head0 // 2, 0:n_win, :])
        finish_pair(head0, pv, 2, False)

    for g, head0 in pairs:
        win_pair(g, head0)

    base = (lax.broadcasted_iota(jnp.int32, (SLC_CHUNK, tq), 1)
            - lax.broadcasted_iota(jnp.int32, (SLC_CHUNK, tq), 0))

    def stage_chunk(item, row0):
        c = list_ref[item]
        k0 = pl.multiple_of(c * SLC_CHUNK, SLC_CHUNK)
        causal = base >= k0 - q0
        for g, head0 in pairs:
            if head0 % HEADS_PER_GROUP == 0:
                sel_keys = jnp.concatenate(
                    [jnp.broadcast_to(selt_ref[g, pl.ds(c * BLOCKS_PER_CHUNK + i, 1), :], (SEL_LEN, tq))
                     for i in range(BLOCKS_PER_CHUNK)], axis=0)
                bias = jnp.where(causal & (sel_keys > 0.5), 0.0, NEG_BIG)
                bias2 = jnp.concatenate([bias, bias], axis=1)
            q_pair = q_ref[0, head0:head0 + 2].reshape(2 * tq, LANES)
            scores = _dot_nt(ks_ref[0, g, pl.ds(k0, SLC_CHUNK), :], q_pair) + bias2
            for i in range(2):
                st_ref[head0 + i, row0:row0 + SLC_CHUNK, :] = scores[:, i * tq:(i + 1) * tq]

    def consume_chunk(item, row0):
        k0 = pl.multiple_of(list_ref[item] * SLC_CHUNK, SLC_CHUNK)
        for g, head0 in pairs:
            alphas = []
            for i, head in enumerate((head0, head0 + 1)):
                x = st_ref[head, row0:row0 + SLC_CHUNK, :]
                m_old = m_ref[head:head + 1, :]
                m_new = jnp.maximum(m_old, jnp.max(x, axis=0, keepdims=True))
                alphas.append(jnp.exp2(m_old - m_new))
                m_ref[head:head + 1, :] = m_new
                p_ref[head0 // 2, 0:SLC_CHUNK, i * tq:(i + 1) * tq] = jnp.exp2(x - m_new).astype(BF16)
            pv = _dot(vst_ref[0, g, :, pl.ds(k0, SLC_CHUNK)], p_ref[head0 // 2, 0:SLC_CHUNK, :])
            acc_ref[head0 // 2] = acc_ref[head0 // 2] * jnp.concatenate(alphas, axis=1) + pv

    last = jnp.maximum(n_items - 1, 0)
    stage_chunk(0, 0)

    def two_chunks(j, carry):
        stage_chunk(jnp.minimum(2 * j + 1, last), SLC_CHUNK)
        consume_chunk(2 * j, 0)

        @pl.when(2 * j + 1 < n_items)
        def _():
            stage_chunk(jnp.minimum(2 * j + 2, last), 0)
            consume_chunk(2 * j + 1, SLC_CHUNK)
        return carry

    lax.fori_loop(0, (n_items + 1) // 2, two_chunks, 0)
    for g, head0 in pairs:
        finish_pair(head0, acc_ref[head0 // 2], 1, False)

    for hh in range(HEADS_PER_GROUP):
        o_ref[:, hh * LANES:(hh + 1) * LANES] = out_ref[hh].T.astype(BF16)


def _overlap_t(seq):
    n_cmp = (seq - CMP_LEN) // CMP_STRIDE + 1
    n_sel = seq // SEL_LEN
    cs = np.arange(n_cmp)[:, None] * CMP_STRIDE
    ss = np.arange(n_sel)[None, :] * SEL_LEN
    ov = np.clip(np.minimum(cs + CMP_LEN, ss + SEL_LEN) - np.maximum(cs, ss), 0, None) / CMP_LEN
    out = np.zeros((n_sel, seq // CMP_STRIDE), np.float32)
    out[:, :n_cmp] = ov.T
    return jnp.asarray(out, dtype=BF16)


def _nsa(q, kc, vct, ks, vst, kw, vwt, gates, *, tq=128):
    batch, _, seq, _ = q.shape
    n_cmp = kc.shape[2]
    n_sel = seq // SEL_LEN
    assert seq % SLC_CHUNK == 0 and seq >= WINDOW + tq and SLC_CHUNK % tq == 0 and tq == LANES
    assert n_sel <= 2 * SEL_LEN
    ovt = _overlap_t(seq)
    keys = lambda n: pl.BlockSpec((1, N_GROUPS, n, LANES), lambda b, i: (b, 0, 0, 0))
    vals = lambda n: pl.BlockSpec((1, N_GROUPS, V_ROWS, n), lambda b, i: (b, 0, 0, 0))
    tiles = seq // tq
    s_rows = max(n_cmp, WINDOW + tq + HEAD_BLOCKS * SEL_LEN, 2 * SLC_CHUNK)
    return pl.pallas_call(
        functools.partial(_nsa_kernel, tq=tq, seq=seq),
        out_shape=jax.ShapeDtypeStruct((batch * seq, HEADS_PER_GROUP * LANES), BF16),
        grid=(batch, tiles),
        in_specs=[
            pl.BlockSpec((1, N_HEADS, tq, LANES), lambda b, i: (b, 0, i, 0)),
            keys(n_cmp), vals(n_cmp), keys(seq), vals(seq), keys(seq), vals(seq),
            pl.BlockSpec((tq, LANES), lambda b, i: (b * tiles + i, 0)),
            pl.BlockSpec(ovt.shape, lambda b, i: (0, 0)),
        ],
        out_specs=pl.BlockSpec((tq, HEADS_PER_GROUP * LANES), lambda b, i: (b * tiles + i, 0)),
        scratch_shapes=[
            pltpu.VMEM((N_HEADS, s_rows, tq), F32),
            pltpu.VMEM((N_HEADS // 2, s_rows, 2 * tq), BF16),
            pltpu.VMEM((LANES, tq), F32),
            pltpu.VMEM((N_GROUPS, n_sel, tq), F32),
            pltpu.SMEM((seq // SLC_CHUNK,), jnp.int32),
            pltpu.VMEM((N_HEADS, tq), F32),
            pltpu.VMEM((N_HEADS // 2, V_ROWS, 2 * tq), F32),
            pltpu.VMEM((HEADS_PER_GROUP, N_GROUPS * HEAD_DIM, tq), F32),
        ],
        compiler_params=_params("parallel", "arbitrary"),
        name="nsa",
    )(q, kc, vct, ks, vst, kw, vwt, gates, ovt)


HALO = max(POOL_WINDOWS)


def _merge_kernel(x_ref, u_ref, halo_ref, gm_ref, on_ref, pw_ref, ps_ref, wbp_ref, wbn_ref, wo_ref,
                  o_ref, ext_ref, *, tm, seq):
    d = x_ref.shape[1]
    pos0 = (pl.program_id(0) * tm) % seq
    ext_ref[0:HALO, :] = jnp.where(pos0 == 0, 0.0, halo_ref[...])
    ext_ref[HALO:HALO + tm, :] = u_ref[...]
    pos = (pos0 + lax.broadcasted_iota(jnp.int32, (tm, POOL_GROUP_DIM), 0)).astype(F32)
    mixed = []
    for gi, w in enumerate(POOL_WINDOWS):
        cols = slice(gi * POOL_GROUP_DIM, (gi + 1) * POOL_GROUP_DIM)
        u = ext_ref[HALO:HALO + tm, cols]
        total = u
        for lag in range(1, w):
            total = total + ext_ref[HALO - lag:HALO - lag + tm, cols]
        delta = total / jnp.minimum(pos + 1.0, float(w)) - u
        mixed.append(_dot(delta.astype(BF16), pw_ref[gi]) * ps_ref[:, cols])
    mixed = jnp.concatenate(mixed, axis=-1).astype(BF16)
    a = _dot(mixed, wbp_ref[...])
    b = _dot(on_ref[...], wbn_ref[...])
    merged = gm_ref[:, 0:d] * a + gm_ref[:, d:2 * d] * b
    o_ref[...] = x_ref[...] + _dot(merged.astype(BF16), wo_ref[...])


def _merge(x, u, gm, o_nsa, pool_w, pool_scale, w_bp, w_bn, w_out, seq, *, tm=512):
    n, d = x.shape
    row = lambda i: (i, 0)
    full = lambda a: pl.BlockSpec(a.shape, lambda i: (0,) * a.ndim, pipeline_mode=pl.Buffered(1))
    halo_blocks = tm // HALO
    return pl.pallas_call(
        functools.partial(_merge_kernel, tm=tm, seq=seq),
        out_shape=jax.ShapeDtypeStruct((n, d), F32),
        grid=(n // tm,),
        in_specs=[
            pl.BlockSpec((tm, d), row),
            pl.BlockSpec((tm, POOL_COLS), row),
            pl.BlockSpec((HALO, POOL_COLS), lambda i: (jnp.maximum(i * halo_blocks - 1, 0), 0)),
            pl.BlockSpec((tm, 2 * d), row),
            pl.BlockSpec((tm, d), row),
            full(pool_w), full(pool_scale), full(w_bp), full(w_bn), full(w_out),
        ],
        out_specs=pl.BlockSpec((tm, d), row),
        scratch_shapes=[pltpu.VMEM((HALO + tm, POOL_COLS), F32)],
        compiler_params=_params("parallel"),
        name="merge",
    )(x, u, u, gm, o_nsa, pool_w, pool_scale, w_bp, w_bn, w_out)


def kernel(x, ffn1_norm, ffn1_w_gate, ffn1_w_up, ffn1_w_down, mix_norm, w_in, cmp_pos, cmp_k_w1, cmp_k_w2, cmp_v_w1, cmp_v_w2, pool_w, pool_scale, w_branch_pool, w_branch_nsa, w_out, ffn2_norm, ffn2_w_gate, ffn2_w_up, ffn2_w_down, final_norm):
    batch, seq, d = x.shape
    depth = w_in.shape[0]
    xf = x.reshape(batch * seq, d)
    bf = lambda a: a.astype(BF16)
    row = lambda a: a.reshape(1, -1)
    for l in range(depth):
        xf = _ffn(xf, row(ffn1_norm[l]), bf(ffn1_w_gate[l]), bf(ffn1_w_up[l]), bf(ffn1_w_down[l]),
                  row(final_norm), final_norm=False)

        q, kc, vc, ks, kw, vs, vw, gates, u, gm = _proj(xf, row(mix_norm[l]), _pack_w_in(w_in[l], d), batch, seq)
        k_cmp, v_cmp = _compress(kc, vc, cmp_pos[l], cmp_k_w1[l], cmp_k_w2[l], cmp_v_w1[l], cmp_v_w2[l], batch, seq)
        o_nsa = _nsa(q, k_cmp, v_cmp, ks, vs, kw, vw, gates)

        w_bn = w_branch_nsa[l].reshape(N_GROUPS, HEADS_PER_GROUP, HEAD_DIM, d).transpose(1, 0, 2, 3).reshape(-1, d)
        xf = _merge(xf, u, gm, o_nsa, bf(pool_w[l]), row(pool_scale[l]), bf(w_branch_pool[l]), bf(w_bn),
                    bf(w_out[l]), seq)

        xf = _ffn(xf, row(ffn2_norm[l]), bf(ffn2_w_gate[l]), bf(ffn2_w_up[l]), bf(ffn2_w_down[l]),
                  row(final_norm), final_norm=(l == depth - 1))
    return xf.reshape(batch, seq, d)
```

```python
import functools

import jax
import jax.numpy as jnp
import numpy as np
from jax import lax
from jax.experimental import pallas as pl
from jax.experimental.pallas import tpu as pltpu

N_HEADS = 16
N_GROUPS = 2
HEADS_PER_GROUP = N_HEADS // N_GROUPS
HEAD_DIM = 64
CMP_LEN = 32
CMP_STRIDE = 16
SEL_LEN = 64
N_SELECT = 16
WINDOW = 512
POOL_WINDOWS = (2, 4, 8, 16)
POOL_GROUP_DIM = 128
RMS_EPS = 1e-6
ALIBI_MAX_BIAS = 8.0

LANES = 128
NEG_BIG = -1e30
TOPK_BIG = 1e30
N_FORCED = 3
TOPK_TAKEN = -float(2 ** 101)
VMEM_LIMIT = 48 * 1024 * 1024

BF16 = jnp.bfloat16
F32 = jnp.float32


def _dot(a, b):
    return jnp.dot(a, b, preferred_element_type=F32)


def _dot_nt(a, b):
    return lax.dot_general(a, b, (((1,), (1,)), ((), ())), preferred_element_type=F32)


def _rms(x, g):
    return x * lax.rsqrt(jnp.mean(x * x, axis=-1, keepdims=True) + RMS_EPS) * g


def _params(*sem):
    return pltpu.CompilerParams(dimension_semantics=sem, vmem_limit_bytes=VMEM_LIMIT)


def _ffn_kernel(x_ref, g_ref, wg_ref, wu_ref, wd_ref, fin_ref, o_ref, *, final_norm):
    x = x_ref[...]
    xn = _rms(x, g_ref[...]).astype(BF16)
    gate = _dot(xn, wg_ref[...])
    up = _dot(xn, wu_ref[...])
    act = (gate * jax.nn.sigmoid(gate)) * up
    y = x + 0.5 * _dot(act.astype(BF16), wd_ref[...])
    if final_norm:
        y = _rms(y, fin_ref[...])
    o_ref[...] = y


def _ffn(x, norm_g, wg, wu, wd, fin_g, *, final_norm, tm=512):
    n, d = x.shape
    row = lambda i: (i, 0)
    resident = lambda a: pl.BlockSpec(a.shape, lambda i: (0, 0), pipeline_mode=pl.Buffered(1))
    return pl.pallas_call(
        functools.partial(_ffn_kernel, final_norm=final_norm),
        out_shape=jax.ShapeDtypeStruct((n, d), F32),
        grid=(n // tm,),
        in_specs=[pl.BlockSpec((tm, d), row), resident(norm_g), resident(wg), resident(wu), resident(wd),
                  resident(fin_g)],
        out_specs=pl.BlockSpec((tm, d), row),
        compiler_params=_params("parallel"),
        name="ffn",
    )(x, norm_g, wg, wu, wd, fin_g)


LOG2E = 1.4426950408889634
N_FEAT = 6


def _bf16_pieces(x):
    x = np.asarray(x, np.float32)
    s1 = x.astype(BF16).astype(np.float32)
    s2 = (x - s1).astype(BF16).astype(np.float32)
    s3 = (x - s1 - s2).astype(BF16).astype(np.float32)
    return s1, s2, s3


def _query_features():
    slopes = np.float32(2.0) ** (-ALIBI_MAX_BIAS * np.arange(1, N_HEADS + 1, dtype=np.float32) / N_HEADS)
    s1, s2, s3 = _bf16_pieces(slopes * np.float32(LOG2E))
    feat = np.zeros((N_HEADS, LANES), np.float32)
    feat[:, HEAD_DIM:HEAD_DIM + N_FEAT] = np.stack([s1, s2, s3, SEL_LEN * s1, SEL_LEN * s2, SEL_LEN * s3], axis=1)
    return jnp.asarray(feat)


def _key_features(pos, width, offset):
    pos = np.asarray(pos)
    a, b = (pos // SEL_LEN).astype(np.float32), (pos % SEL_LEN).astype(np.float32)
    feat = np.zeros((len(pos), width), np.float32)
    feat[:, offset:offset + N_FEAT] = np.stack([b, b, b, a, a, a], axis=1)
    return jnp.asarray(feat)


Q_COLS = N_HEADS * LANES
CMP_OFF = Q_COLS
KEY_OFF = CMP_OFF + 2 * LANES
VAL_OFF = KEY_OFF + 2 * N_GROUPS * LANES
GATE_OFF = VAL_OFF + 2 * LANES
POOL_OFF = GATE_OFF + LANES
POOL_COLS = len(POOL_WINDOWS) * POOL_GROUP_DIM
MERGE_OFF = POOL_OFF + POOL_COLS


V_ROWS = HEAD_DIM + 16


def _store_values_t(v_ref, v):
    vt = v.T
    for g in range(N_GROUPS):
        v_ref[0, g, 0:HEAD_DIM, :] = vt[g * HEAD_DIM:(g + 1) * HEAD_DIM, :].astype(BF16)
        v_ref[0, g, HEAD_DIM:V_ROWS, :] = jnp.ones((V_ROWS - HEAD_DIM, v.shape[0]), BF16)


def _proj_kernel(x_ref, g_ref, w_ref, qf_ref, kf_ref, q_ref, kc_ref, vc_ref, ks_ref, kw_ref, vs_ref, vw_ref,
                 gate_ref, u_ref, gm_ref):
    d = x_ref.shape[1]
    hn = _rms(x_ref[...], g_ref[...]).astype(BF16)
    q_scale = HEAD_DIM ** -0.5 * LOG2E
    for h in range(0, N_HEADS, 2):
        q2 = _dot(hn, w_ref[:, h * LANES:(h + 2) * LANES]) * q_scale
        for i in range(2):
            q_ref[0, h + i] = (q2[:, i * LANES:(i + 1) * LANES] + qf_ref[h + i:h + i + 1, :]).astype(BF16)
    cmp_in = _dot(hn, w_ref[:, CMP_OFF:CMP_OFF + 2 * LANES])
    kc_ref[...] = cmp_in[:, 0:LANES]
    vc_ref[...] = cmp_in[:, LANES:2 * LANES]
    keys = _dot(hn, w_ref[:, KEY_OFF:KEY_OFF + 2 * N_GROUPS * LANES])
    kf = kf_ref[...]
    for g in range(N_GROUPS):
        ks_ref[0, g] = (keys[:, g * LANES:(g + 1) * LANES] + kf).astype(BF16)
        kw_ref[0, g] = (keys[:, (N_GROUPS + g) * LANES:(N_GROUPS + g + 1) * LANES] + kf).astype(BF16)
    vals = _dot(hn, w_ref[:, VAL_OFF:VAL_OFF + 2 * LANES])
    for i, v_ref in enumerate((vs_ref, vw_ref)):
        _store_values_t(v_ref, vals[:, i * LANES:(i + 1) * LANES])
    gate_ref[...] = jax.nn.sigmoid(_dot(hn, w_ref[:, GATE_OFF:GATE_OFF + LANES]))
    u_ref[...] = _dot(hn, w_ref[:, POOL_OFF:POOL_OFF + POOL_COLS])
    gm_ref[...] = jax.nn.sigmoid(_dot(hn, w_ref[:, MERGE_OFF:MERGE_OFF + 2 * d]))


def _pack_w_in(w_in, d):
    w_in = w_in.astype(BF16)
    qw = N_HEADS * HEAD_DIM
    kvw = N_GROUPS * HEAD_DIM

    def padded(w, n):
        w = w.reshape(d, n, HEAD_DIM)
        return jnp.concatenate([w, jnp.zeros_like(w)], axis=-1).reshape(d, n * LANES)

    kc, vc, ks, vs, kw, vw = (w_in[:, qw + i * kvw:qw + (i + 1) * kvw] for i in range(6))
    off = qw + 6 * kvw
    n_g = 3 * N_HEADS
    w_g = jnp.pad(w_in[:, off:off + n_g], ((0, 0), (0, LANES - n_g)))
    w_rest = w_in[:, off + n_g:]
    return jnp.concatenate([padded(w_in[:, :qw], N_HEADS), kc, vc, padded(ks, N_GROUPS), padded(kw, N_GROUPS),
                            vs, vw, w_g, w_rest], axis=1)


def _proj(x, norm_g, w_packed, batch, seq, *, tm=512):
    n, d = x.shape
    tiles_per_seq = seq // tm
    row = lambda i: (i, 0)
    const = lambda i: (0, 0)
    per_group = lambda i: (i // tiles_per_seq, 0, i % tiles_per_seq, 0)
    qf = _query_features()
    kf = _key_features(np.arange(seq), LANES, HEAD_DIM)
    flat = lambda width, dtype: (jax.ShapeDtypeStruct((n, width), dtype), pl.BlockSpec((tm, width), row))
    grouped = lambda count: (jax.ShapeDtypeStruct((batch, count, seq, LANES), BF16),
                             pl.BlockSpec((1, count, tm, LANES), per_group))
    values_t = (jax.ShapeDtypeStruct((batch, N_GROUPS, V_ROWS, seq), BF16),
                pl.BlockSpec((1, N_GROUPS, V_ROWS, tm), lambda i: (i // tiles_per_seq, 0, 0, i % tiles_per_seq)))
    outs = [grouped(N_HEADS), flat(LANES, F32), flat(LANES, F32), grouped(N_GROUPS), grouped(N_GROUPS),
            values_t, values_t, flat(LANES, F32), flat(POOL_COLS, F32), flat(2 * d, F32)]
    return pl.pallas_call(
        _proj_kernel,
        out_shape=[o[0] for o in outs],
        grid=(n // tm,),
        in_specs=[pl.BlockSpec((tm, d), row), pl.BlockSpec((1, d), const),
                  pl.BlockSpec(w_packed.shape, const, pipeline_mode=pl.Buffered(1)),
                  pl.BlockSpec(qf.shape, const), pl.BlockSpec((tm, LANES), lambda i: (i % tiles_per_seq, 0))],
        out_specs=[o[1] for o in outs],
        compiler_params=_params("parallel"),
        name="proj",
    )(x, norm_g, w_packed, qf, kf)


def _compress_kernel(k_ref, v_ref, pos_ref, wk1_ref, wk2_ref, wv1_ref, wv2_ref, feat_ref, ko_ref, vo_ref):
    n_chunk = k_ref.shape[0] // CMP_STRIDE
    hidden = wk2_ref.shape[0]

    def compress(x_ref, w1_ref, w2_ref):
        first = second = None
        for r in range(CMP_STRIDE):
            rows = x_ref[pl.ds(r, n_chunk, stride=CMP_STRIDE), :]
            a = _dot((rows + pos_ref[r:r + 1, :]).astype(BF16), w1_ref[r])
            b = _dot((rows + pos_ref[CMP_STRIDE + r:CMP_STRIDE + r + 1, :]).astype(BF16), w1_ref[CMP_STRIDE + r])
            first = a if first is None else first + a
            second = b if second is None else second + b
        act = jax.nn.gelu(first + pltpu.roll(second, n_chunk - 1, 0)).astype(BF16)
        return [_dot(act[:, g * hidden:(g + 1) * hidden], w2_ref[...]) for g in range(N_GROUPS)]

    for g, k_cmp in enumerate(compress(k_ref, wk1_ref, wk2_ref)):
        ko_ref[0, g] = jnp.concatenate([k_cmp, feat_ref[...]], axis=-1).astype(BF16)
    _store_values_t(vo_ref, jnp.concatenate(compress(v_ref, wv1_ref, wv2_ref), axis=-1))


def _block_diag_w1(w1):
    w = w1.reshape(CMP_LEN, HEAD_DIM, -1)
    z = jnp.zeros_like(w)
    return jnp.concatenate([jnp.concatenate([w, z], axis=2), jnp.concatenate([z, w], axis=2)], axis=1).astype(BF16)


def _compress(kc, vc, pos, wk1, wk2, wv1, wv2, batch, seq):
    n_chunk = seq // CMP_STRIDE
    rows = pl.BlockSpec((seq, LANES), lambda b: (b, 0))
    full = lambda a: pl.BlockSpec(a.shape, lambda b: (0,) * a.ndim, pipeline_mode=pl.Buffered(1))
    pos2 = jnp.concatenate([pos] * N_GROUPS, axis=1)
    feat = _key_features(np.arange(n_chunk) * CMP_STRIDE + CMP_LEN - 1, HEAD_DIM, 0)
    args = (pos2, _block_diag_w1(wk1), wk2.astype(BF16), _block_diag_w1(wv1), wv2.astype(BF16), feat)
    return pl.pallas_call(
        _compress_kernel,
        out_shape=[jax.ShapeDtypeStruct((batch, N_GROUPS, n_chunk, LANES), BF16),
                   jax.ShapeDtypeStruct((batch, N_GROUPS, V_ROWS, n_chunk), BF16)],
        grid=(batch,),
        in_specs=[rows, rows] + [full(a) for a in args],
        out_specs=[pl.BlockSpec((1, N_GROUPS, n_chunk, LANES), lambda b: (b, 0, 0, 0)),
                   pl.BlockSpec((1, N_GROUPS, V_ROWS, n_chunk), lambda b: (b, 0, 0, 0))],
        compiler_params=_params("parallel"),
        name="compress",
    )(kc, vc, *args)


SLC_CHUNK = 256
BLOCKS_PER_CHUNK = SLC_CHUNK // SEL_LEN
HEAD_BLOCKS = 2


def _nsa_kernel(q_ref, kc_ref, vct_ref, ks_ref, vst_ref, kw_ref, vwt_ref, gate_ref, ovt_ref, o_ref,
                st_ref, p_ref, gt_ref, selt_ref, list_ref, m_ref, acc_ref, out_ref, *, tq, seq):
    n_sel = seq // SEL_LEN
    n_chunks = seq // SLC_CHUNK
    n_cmp = kc_ref.shape[2]
    n_win = WINDOW + tq
    q0 = pl.program_id(1) * tq
    gt_ref[...] = gate_ref[...].T
    pairs = [(g, g * HEADS_PER_GROUP + 2 * j) for g in range(N_GROUPS) for j in range(HEADS_PER_GROUP // 2)]

    def gate_row(head, branch):
        c = 3 * head + branch
        return gt_ref[c:c + 1, :]

    def out_rows(head):
        g, hh = divmod(head, HEADS_PER_GROUP)
        return hh, slice(g * HEAD_DIM, (g + 1) * HEAD_DIM)

    def stage_scores(keys, n_keys, row0=0):
        for g, head0 in pairs:
            q_pair = q_ref[0, head0:head0 + 2].reshape(2 * tq, LANES)
            scores = _dot_nt(keys(g), q_pair)
            for i in range(2):
                st_ref[head0 + i, row0:row0 + n_keys, :] = scores[:, i * tq:(i + 1) * tq]

    def finish_pair(head0, pv, branch, first, ok=None):
        invs = []
        for i in range(2):
            cols = slice(i * tq, (i + 1) * tq)
            inv = 1.0 / pv[HEAD_DIM:HEAD_DIM + 1, cols]
            if ok is not None:
                inv = jnp.where(ok, inv, 0.0)
            hh, rows = out_rows(head0 + i)
            contrib = pv[0:HEAD_DIM, cols] * (inv * gate_row(head0 + i, branch))
            out_ref[hh, rows, :] = contrib if first else out_ref[hh, rows, :] + contrib
            invs.append(inv)
        return invs

    t_c = q0 + lax.broadcasted_iota(jnp.int32, (n_cmp, tq), 1)
    end_c = lax.broadcasted_iota(jnp.int32, (n_cmp, tq), 0) * CMP_STRIDE + (CMP_LEN - 1)
    bias_c = jnp.where(t_c >= end_c, 0.0, NEG_BIG)
    has_cmp = (q0 + lax.broadcasted_iota(jnp.int32, (1, tq), 1)) >= CMP_LEN - 1
    inv_c = [None] * N_HEADS

    def cmp_pair(g, head0):
        for i in range(2):
            x = st_ref[head0 + i, 0:n_cmp, :] + bias_c
            p_ref[head0 // 2, 0:n_cmp, i * tq:(i + 1) * tq] = jnp.exp2(x - jnp.max(x, axis=0, keepdims=True)).astype(BF16)
        pv = _dot(vct_ref[0, g], p_ref[head0 // 2, 0:n_cmp, :])
        inv_c[head0], inv_c[head0 + 1] = finish_pair(head0, pv, 0, True, has_cmp)

    stage_scores(lambda g: kc_ref[0, g], n_cmp)
    for g, head0 in pairs:
        cmp_pair(g, head0)

    raw = [_dot(ovt_ref[...], p_ref[pair, 0:n_cmp, :]) for pair in range(N_HEADS // 2)]

    w0 = pl.multiple_of(jnp.maximum(q0 - WINDOW, 0), tq)
    stage_scores(lambda g: kw_ref[0, g, pl.ds(w0, n_win), :], n_win)
    head_keys = HEAD_BLOCKS * SEL_LEN
    stage_scores(lambda g: ks_ref[0, g, 0:head_keys, :], head_keys, n_win)

    dist_w = (q0 + lax.broadcasted_iota(jnp.int32, (n_win, tq), 1)) - (w0 + lax.broadcasted_iota(jnp.int32, (n_win, tq), 0))
    bias_w = jnp.where((dist_w >= 0) & (dist_w < WINDOW), 0.0, NEG_BIG)

    def win_pair(g, head0):
        for i in range(2):
            x = st_ref[head0 + i, 0:n_win, :] + bias_w
            p_ref[head0 // 2, 0:n_win, i * tq:(i + 1) * tq] = jnp.exp2(x - jnp.max(x, axis=0, keepdims=True)).astype(BF16)
        pv = _dot(vwt_ref[0, g, :, pl.ds(w0, n_win)], p_ref[head0 // 2, 0:n_win, :])
        finish_pair(head0, pv, 2, False)

    for g, head0 in pairs:
        win_pair(g, head0)

    blk = lax.broadcasted_iota(jnp.int32, (n_sel, tq), 0)
    cur = (q0 + lax.broadcasted_iota(jnp.int32, (n_sel, tq), 1)) // SEL_LEN
    valid = blk <= cur
    forced = valid & ((blk == 0) | (blk == cur) | (blk == cur - 1))
    blk_f = blk.astype(F32)
    score = []
    for g in range(N_GROUPS):
        total = None
        for head in range(g * HEADS_PER_GROUP, (g + 1) * HEADS_PER_GROUP):
            part = raw[head // 2][:, (head % 2) * tq:(head % 2 + 1) * tq] * inv_c[head]
            total = part if total is None else total + part
        score.append(jnp.where(forced, TOPK_TAKEN, jnp.where(valid, total, -TOPK_BIG)))

    for _ in range(min(N_SELECT, n_sel) - N_FORCED):
        for g in range(N_GROUPS):
            best = jnp.max(score[g], axis=0, keepdims=True)
            pick = jnp.min(jnp.where(score[g] == best, blk_f, float(n_sel)), axis=0, keepdims=True)
            score[g] = jnp.where(blk_f == pick, TOPK_TAKEN, score[g])
    sel = [jnp.where(valid & (score[g] == TOPK_TAKEN), 1.0, 0.0) for g in range(N_GROUPS)]
    for g in range(N_GROUPS):
        selt_ref[g] = jnp.where(blk >= HEAD_BLOCKS, sel[g], 0.0)
    sel_any = jnp.where(blk >= HEAD_BLOCKS, sel[0] + sel[1], 0.0)
    n_items = jnp.int32(0)
    for c in range(n_chunks):
        any_sel = jnp.max(sel_any[c * BLOCKS_PER_CHUNK:(c + 1) * BLOCKS_PER_CHUNK, :])
        list_ref[n_items] = c
        n_items = n_items + (any_sel > 0.5).astype(jnp.int32)

    causal_h = (q0 + lax.broadcasted_iota(jnp.int32, (head_keys, tq), 1)) >= lax.broadcasted_iota(jnp.int32, (head_keys, tq), 0)
    head_rows = slice(n_win, n_win + head_keys)
    for g, head0 in pairs:
        if head0 % HEADS_PER_GROUP == 0:
            sel_keys = jnp.concatenate([jnp.broadcast_to(sel[g][i:i + 1, :], (SEL_LEN, tq)) for i in range(HEAD_BLOCKS)], axis=0)
            bias = jnp.where(causal_h & (sel_keys > 0.5), 0.0, NEG_BIG)
        for i in range(2):
            x = st_ref[head0 + i, head_rows, :] + bias
            m_new = jnp.max(x, axis=0, keepdims=True)
            m_ref[head0 + i:head0 + i + 1, :] = m_new
            p_ref[head0 // 2, head_rows, i * tq:(i + 1) * tq] = jnp.exp2(x - m_new).astype(BF16)
    for g, head0 in pairs:
        acc_ref[head0 // 2] = _dot(vst_ref[0, g, :, 0:head_keys], p_ref[head0 // 2, head_rows, :])

    base = (lax.broadcasted_iota(jnp.int32, (SLC_CHUNK, tq), 1)
            - lax.broadcasted_iota(jnp.int32, (SLC_CHUNK, tq), 0))

    def stage_chunk(item, row0):
        c = list_ref[item]
        k0 = pl.multiple_of(c * SLC_CHUNK, SLC_CHUNK)
        causal = base >= k0 - q0
        for g, head0 in pairs:
            if head0 % HEADS_PER_GROUP == 0:
                sel_keys = jnp.concatenate(
                    [jnp.broadcast_to(selt_ref[g, pl.ds(c * BLOCKS_PER_CHUNK + i, 1), :], (SEL_LEN, tq))
                     for i in range(BLOCKS_PER_CHUNK)], axis=0)
                bias = jnp.where(causal & (sel_keys > 0.5), 0.0, NEG_BIG)
                bias2 = jnp.concatenate([bias, bias], axis=1)
            q_pair = q_ref[0, head0:head0 + 2].reshape(2 * tq, LANES)
            scores = _dot_nt(ks_ref[0, g, pl.ds(k0, SLC_CHUNK), :], q_pair) + bias2
            for i in range(2):
                st_ref[head0 + i, row0:row0 + SLC_CHUNK, :] = scores[:, i * tq:(i + 1) * tq]

    def consume_chunk(item, row0):
        k0 = pl.multiple_of(list_ref[item] * SLC_CHUNK, SLC_CHUNK)
        for g, head0 in pairs:
            alphas = []
            for i, head in enumerate((head0, head0 + 1)):
                x = st_ref[head, row0:row0 + SLC_CHUNK, :]
                m_old = m_ref[head:head + 1, :]
                m_new = jnp.maximum(m_old, jnp.max(x, axis=0, keepdims=True))
                alphas.append(jnp.exp2(m_old - m_new))
                m_ref[head:head + 1, :] = m_new
                p_ref[head0 // 2, 0:SLC_CHUNK, i * tq:(i + 1) * tq] = jnp.exp2(x - m_new).astype(BF16)
            pv = _dot(vst_ref[0, g, :, pl.ds(k0, SLC_CHUNK)], p_ref[head0 // 2, 0:SLC_CHUNK, :])
            acc_ref[head0 // 2] = acc_ref[head0 // 2] * jnp.concatenate(alphas, axis=1) + pv

    last = jnp.maximum(n_items - 1, 0)
    stage_chunk(0, 0)

    def two_chunks(j, carry):
        stage_chunk(jnp.minimum(2 * j + 1, last), SLC_CHUNK)
        consume_chunk(2 * j, 0)

        @pl.when(2 * j + 1 < n_items)
        def _():
            stage_chunk(jnp.minimum(2 * j + 2, last), 0)
            consume_chunk(2 * j + 1, SLC_CHUNK)
        return carry

    lax.fori_loop(0, (n_items + 1) // 2, two_chunks, 0)
    for g, head0 in pairs:
        finish_pair(head0, acc_ref[head0 // 2], 1, False)

    for hh in range(HEADS_PER_GROUP):
        o_ref[:, hh * LANES:(hh + 1) * LANES] = out_ref[hh].T.astype(BF16)


def _overlap_t(seq):
    n_cmp = (seq - CMP_LEN) // CMP_STRIDE + 1
    n_sel = seq // SEL_LEN
    cs = np.arange(n_cmp)[:, None] * CMP_STRIDE
    ss = np.arange(n_sel)[None, :] * SEL_LEN
    ov = np.clip(np.minimum(cs + CMP_LEN, ss + SEL_LEN) - np.maximum(cs, ss), 0, None) / CMP_LEN
    out = np.zeros((n_sel, seq // CMP_STRIDE), np.float32)
    out[:, :n_cmp] = ov.T
    return jnp.asarray(out, dtype=BF16)


def _nsa(q, kc, vct, ks, vst, kw, vwt, gates, *, tq=128):
    batch, _, seq, _ = q.shape
    n_cmp = kc.shape[2]
    n_sel = seq // SEL_LEN
    assert seq % SLC_CHUNK == 0 and seq >= WINDOW + tq and SLC_CHUNK % tq == 0 and tq == LANES
    assert n_sel <= 2 * SEL_LEN
    ovt = _overlap_t(seq)
    keys = lambda n: pl.BlockSpec((1, N_GROUPS, n, LANES), lambda b, i: (b, 0, 0, 0))
    vals = lambda n: pl.BlockSpec((1, N_GROUPS, V_ROWS, n), lambda b, i: (b, 0, 0, 0))
    tiles = seq // tq
    s_rows = max(n_cmp, WINDOW + tq + HEAD_BLOCKS * SEL_LEN, 2 * SLC_CHUNK)
    return pl.pallas_call(
        functools.partial(_nsa_kernel, tq=tq, seq=seq),
        out_shape=jax.ShapeDtypeStruct((batch * seq, HEADS_PER_GROUP * LANES), BF16),
        grid=(batch, tiles),
        in_specs=[
            pl.BlockSpec((1, N_HEADS, tq, LANES), lambda b, i: (b, 0, i, 0)),
            keys(n_cmp), vals(n_cmp), keys(seq), vals(seq), keys(seq), vals(seq),
            pl.BlockSpec((tq, LANES), lambda b, i: (b * tiles + i, 0)),
            pl.BlockSpec(ovt.shape, lambda b, i: (0, 0)),
        ],
        out_specs=pl.BlockSpec((tq, HEADS_PER_GROUP * LANES), lambda b, i: (b * tiles + i, 0)),
        scratch_shapes=[
            pltpu.VMEM((N_HEADS, s_rows, tq), F32),
            pltpu.VMEM((N_HEADS // 2, s_rows, 2 * tq), BF16),
            pltpu.VMEM((LANES, tq), F32),
            pltpu.VMEM((N_GROUPS, n_sel, tq), F32),
            pltpu.SMEM((seq // SLC_CHUNK,), jnp.int32),
            pltpu.VMEM((N_HEADS, tq), F32),
            pltpu.VMEM((N_HEADS // 2, V_ROWS, 2 * tq), F32),
            pltpu.VMEM((HEADS_PER_GROUP, N_GROUPS * HEAD_DIM, tq), F32),
        ],
        compiler_params=_params("parallel", "arbitrary"),
        name="nsa",
    )(q, kc, vct, ks, vst, kw, vwt, gates, ovt)


HALO = max(POOL_WINDOWS)


def _merge_kernel(x_ref, u_ref, halo_ref, gm_ref, on_ref, pw_ref, ps_ref, wbp_ref, wbn_ref, wo_ref,
                  o_ref, ext_ref, *, tm, seq):
    d = x_ref.shape[1]
    pos0 = (pl.program_id(0) * tm) % seq
    ext_ref[0:HALO, :] = jnp.where(pos0 == 0, 0.0, halo_ref[...])
    ext_ref[HALO:HALO + tm, :] = u_ref[...]
    pos = (pos0 + lax.broadcasted_iota(jnp.int32, (tm, POOL_GROUP_DIM), 0)).astype(F32)
    mixed = []
    for gi, w in enumerate(POOL_WINDOWS):
        cols = slice(gi * POOL_GROUP_DIM, (gi + 1) * POOL_GROUP_DIM)
        u = ext_ref[HALO:HALO + tm, cols]
        total = u
        for lag in range(1, w):
            total = total + ext_ref[HALO - lag:HALO - lag + tm, cols]
        delta = total / jnp.minimum(pos + 1.0, float(w)) - u
        mixed.append(_dot(delta.astype(BF16), pw_ref[gi]) * ps_ref[:, cols])
    mixed = jnp.concatenate(mixed, axis=-1).astype(BF16)
    a = _dot(mixed, wbp_ref[...])
    b = _dot(on_ref[...], wbn_ref[...])
    merged = gm_ref[:, 0:d] * a + gm_ref[:, d:2 * d] * b
    o_ref[...] = x_ref[...] + _dot(merged.astype(BF16), wo_ref[...])


def _merge(x, u, gm, o_nsa, pool_w, pool_scale, w_bp, w_bn, w_out, seq, *, tm=512):
    n, d = x.shape
    row = lambda i: (i, 0)
    full = lambda a: pl.BlockSpec(a.shape, lambda i: (0,) * a.ndim, pipeline_mode=pl.Buffered(1))
    halo_blocks = tm // HALO
    return pl.pallas_call(
        functools.partial(_merge_kernel, tm=tm, seq=seq),
        out_shape=jax.ShapeDtypeStruct((n, d), F32),
        grid=(n // tm,),
        in_specs=[
            pl.BlockSpec((tm, d), row),
            pl.BlockSpec((tm, POOL_COLS), row),
            pl.BlockSpec((HALO, POOL_COLS), lambda i: (jnp.maximum(i * halo_blocks - 1, 0), 0)),
            pl.BlockSpec((tm, 2 * d), row),
            pl.BlockSpec((tm, d), row),
            full(pool_w), full(pool_scale), full(w_bp), full(w_bn), full(w_out),
        ],
        out_specs=pl.BlockSpec((tm, d), row),
        scratch_shapes=[pltpu.VMEM((HALO + tm, POOL_COLS), F32)],
        compiler_params=_params("parallel"),
        name="merge",
    )(x, u, u, gm, o_nsa, pool_w, pool_scale, w_bp, w_bn, w_out)


def kernel(x, ffn1_norm, ffn1_w_gate, ffn1_w_up, ffn1_w_down, mix_norm, w_in, cmp_pos, cmp_k_w1, cmp_k_w2, cmp_v_w1, cmp_v_w2, pool_w, pool_scale, w_branch_pool, w_branch_nsa, w_out, ffn2_norm, ffn2_w_gate, ffn2_w_up, ffn2_w_down, final_norm):
    batch, seq, d = x.shape
    depth = w_in.shape[0]
    xf = x.reshape(batch * seq, d)
    bf = lambda a: a.astype(BF16)
    row = lambda a: a.reshape(1, -1)
    for l in range(depth):
        xf = _ffn(xf, row(ffn1_norm[l]), bf(ffn1_w_gate[l]), bf(ffn1_w_up[l]), bf(ffn1_w_down[l]),
                  row(final_norm), final_norm=False)

        q, kc, vc, ks, kw, vs, vw, gates, u, gm = _proj(xf, row(mix_norm[l]), _pack_w_in(w_in[l], d), batch, seq)
        k_cmp, v_cmp = _compress(kc, vc, cmp_pos[l], cmp_k_w1[l], cmp_k_w2[l], cmp_v_w1[l], cmp_v_w2[l], batch, seq)
        o_nsa = _nsa(q, k_cmp, v_cmp, ks, vs, kw, vw, gates)

        w_bn = w_branch_nsa[l].reshape(N_GROUPS, HEADS_PER_GROUP, HEAD_DIM, d).transpose(1, 0, 2, 3).reshape(-1, d)
        xf = _merge(xf, u, gm, o_nsa, bf(pool_w[l]), row(pool_scale[l]), bf(w_branch_pool[l]), bf(w_bn),
                    bf(w_out[l]), seq)

        xf = _ffn(xf, row(ffn2_norm[l]), bf(ffn2_w_gate[l]), bf(ffn2_w_up[l]), bf(ffn2_w_down[l]),
                  row(final_norm), final_norm=(l == depth - 1))
    return xf.reshape(batch, seq, d)
```

```python
import functools

import jax
import jax.numpy as jnp
import numpy as np
from jax import lax
from jax.experimental import pallas as pl
from jax.experimental.pallas import tpu as pltpu

N_HEADS = 16
N_GROUPS = 2
HEADS_PER_GROUP = N_HEADS // N_GROUPS
HEAD_DIM = 64
CMP_LEN = 32
CMP_STRIDE = 16
SEL_LEN = 64
N_SELECT = 16
WINDOW = 512
POOL_WINDOWS = (2, 4, 8, 16)
POOL_GROUP_DIM = 128
RMS_EPS = 1e-6
ALIBI_MAX_BIAS = 8.0

LANES = 128
NEG_BIG = -1e30
TOPK_BIG = 1e30
N_FORCED = 3
TOPK_TAKEN = -float(2 ** 101)
VMEM_LIMIT = 48 * 1024 * 1024

BF16 = jnp.bfloat16
F32 = jnp.float32


def _dot(a, b):
    return jnp.dot(a, b, preferred_element_type=F32)


def _dot_nt(a, b):
    return lax.dot_general(a, b, (((1,), (1,)), ((), ())), preferred_element_type=F32)


def _rms(x, g):
    return x * lax.rsqrt(jnp.mean(x * x, axis=-1, keepdims=True) + RMS_EPS) * g


def _params(*sem):
    return pltpu.CompilerParams(dimension_semantics=sem, vmem_limit_bytes=VMEM_LIMIT)


def _ffn_kernel(x_ref, g_ref, wg_ref, wu_ref, wd_ref, fin_ref, o_ref, *, final_norm):
    x = x_ref[...]
    xn = _rms(x, g_ref[...]).astype(BF16)
    gate = _dot(xn, wg_ref[...])
    up = _dot(xn, wu_ref[...])
    act = (gate * jax.nn.sigmoid(gate)) * up
    y = x + 0.5 * _dot(act.astype(BF16), wd_ref[...])
    if final_norm:
        y = _rms(y, fin_ref[...])
    o_ref[...] = y


def _ffn(x, norm_g, wg, wu, wd, fin_g, *, final_norm, tm=512):
    n, d = x.shape
    row = lambda i: (i, 0)
    resident = lambda a: pl.BlockSpec(a.shape, lambda i: (0, 0), pipeline_mode=pl.Buffered(1))
    return pl.pallas_call(
        functools.partial(_ffn_kernel, final_norm=final_norm),
        out_shape=jax.ShapeDtypeStruct((n, d), F32),
        grid=(n // tm,),
        in_specs=[pl.BlockSpec((tm, d), row), resident(norm_g), resident(wg), resident(wu), resident(wd),
                  resident(fin_g)],
        out_specs=pl.BlockSpec((tm, d), row),
        compiler_params=_params("parallel"),
        name="ffn",
    )(x, norm_g, wg, wu, wd, fin_g)


LOG2E = 1.4426950408889634
N_FEAT = 6


def _bf16_pieces(x):
    x = np.asarray(x, np.float32)
    s1 = x.astype(BF16).astype(np.float32)
    s2 = (x - s1).astype(BF16).astype(np.float32)
    s3 = (x - s1 - s2).astype(BF16).astype(np.float32)
    return s1, s2, s3


def _query_features():
    slopes = np.float32(2.0) ** (-ALIBI_MAX_BIAS * np.arange(1, N_HEADS + 1, dtype=np.float32) / N_HEADS)
    s1, s2, s3 = _bf16_pieces(slopes * np.float32(LOG2E))
    feat = np.zeros((N_HEADS, LANES), np.float32)
    feat[:, HEAD_DIM:HEAD_DIM + N_FEAT] = np.stack([s1, s2, s3, SEL_LEN * s1, SEL_LEN * s2, SEL_LEN * s3], axis=1)
    return jnp.asarray(feat)


def _key_features(pos, width, offset):
    pos = np.asarray(pos)
    a, b = (pos // SEL_LEN).astype(np.float32), (pos % SEL_LEN).astype(np.float32)
    feat = np.zeros((len(pos), width), np.float32)
    feat[:, offset:offset + N_FEAT] = np.stack([b, b, b, a, a, a], axis=1)
    return jnp.asarray(feat)


Q_COLS = N_HEADS * HEAD_DIM
CMP_OFF = Q_COLS
KEY_OFF = CMP_OFF + 2 * LANES
VAL_OFF = KEY_OFF + 2 * LANES
GATE_OFF = VAL_OFF + 2 * LANES
POOL_OFF = GATE_OFF + LANES
POOL_COLS = len(POOL_WINDOWS) * POOL_GROUP_DIM
MERGE_OFF = POOL_OFF + POOL_COLS


V_ROWS = HEAD_DIM + 16


def _store_values_t(v_ref, v):
    vt = v.T
    for g in range(N_GROUPS):
        v_ref[0, g, 0:HEAD_DIM, :] = vt[g * HEAD_DIM:(g + 1) * HEAD_DIM, :].astype(BF16)
        v_ref[0, g, HEAD_DIM:V_ROWS, :] = jnp.ones((V_ROWS - HEAD_DIM, v.shape[0]), BF16)


def _proj_kernel(x_ref, g_ref, w_ref, qf_ref, kf_ref, q_ref, kc_ref, vc_ref, ks_ref, kw_ref, vs_ref, vw_ref,
                 gate_ref, u_ref, gm_ref):
    d = x_ref.shape[1]
    hn = _rms(x_ref[...], g_ref[...]).astype(BF16)
    q_scale = HEAD_DIM ** -0.5 * LOG2E
    low = lax.broadcasted_iota(jnp.int32, (x_ref.shape[0], LANES), 1) < HEAD_DIM
    for j in range(N_HEADS // 4):
        q4 = _dot(hn, w_ref[:, j * 2 * LANES:(j + 1) * 2 * LANES]) * q_scale
        for b in range(2):
            both = q4[:, b * LANES:(b + 1) * LANES]
            h = 4 * j + 2 * b
            q_ref[0, h] = (jnp.where(low, both, 0.0) + qf_ref[h:h + 1, :]).astype(BF16)
            q_ref[0, h + 1] = (jnp.where(low, pltpu.roll(both, HEAD_DIM, 1), 0.0) + qf_ref[h + 1:h + 2, :]).astype(BF16)
    cmp_in = _dot(hn, w_ref[:, CMP_OFF:CMP_OFF + 2 * LANES])
    kc_ref[...] = cmp_in[:, 0:LANES]
    vc_ref[...] = cmp_in[:, LANES:2 * LANES]
    keys = _dot(hn, w_ref[:, KEY_OFF:KEY_OFF + 2 * LANES])
    kf = kf_ref[...]
    for i, k_ref in enumerate((ks_ref, kw_ref)):
        both = keys[:, i * LANES:(i + 1) * LANES]
        k_ref[0, 0] = (jnp.where(low, both, 0.0) + kf).astype(BF16)
        k_ref[0, 1] = (jnp.where(low, pltpu.roll(both, HEAD_DIM, 1), 0.0) + kf).astype(BF16)
    vals = _dot(hn, w_ref[:, VAL_OFF:VAL_OFF + 2 * LANES])
    for i, v_ref in enumerate((vs_ref, vw_ref)):
        _store_values_t(v_ref, vals[:, i * LANES:(i + 1) * LANES])
    gate_ref[...] = jax.nn.sigmoid(_dot(hn, w_ref[:, GATE_OFF:GATE_OFF + LANES]))
    u_ref[...] = _dot(hn, w_ref[:, POOL_OFF:POOL_OFF + POOL_COLS])
    gm_ref[...] = jax.nn.sigmoid(_dot(hn, w_ref[:, MERGE_OFF:MERGE_OFF + 2 * d]))


def _pack_w_in(w_in, d):
    w_in = w_in.astype(BF16)
    qw = N_HEADS * HEAD_DIM
    kvw = N_GROUPS * HEAD_DIM

    kc, vc, ks, vs, kw, vw = (w_in[:, qw + i * kvw:qw + (i + 1) * kvw] for i in range(6))
    off = qw + 6 * kvw
    n_g = 3 * N_HEADS
    w_g = jnp.pad(w_in[:, off:off + n_g], ((0, 0), (0, LANES - n_g)))
    w_rest = w_in[:, off + n_g:]
    return jnp.concatenate([w_in[:, :qw], kc, vc, ks, kw,
                            vs, vw, w_g, w_rest], axis=1)


def _proj(x, norm_g, w_packed, batch, seq, *, tm=512):
    n, d = x.shape
    tiles_per_seq = seq // tm
    row = lambda i: (i, 0)
    const = lambda i: (0, 0)
    per_group = lambda i: (i // tiles_per_seq, 0, i % tiles_per_seq, 0)
    qf = _query_features()
    kf = _key_features(np.arange(seq), LANES, HEAD_DIM)
    flat = lambda width, dtype: (jax.ShapeDtypeStruct((n, width), dtype), pl.BlockSpec((tm, width), row))
    grouped = lambda count: (jax.ShapeDtypeStruct((batch, count, seq, LANES), BF16),
                             pl.BlockSpec((1, count, tm, LANES), per_group))
    values_t = (jax.ShapeDtypeStruct((batch, N_GROUPS, V_ROWS, seq), BF16),
                pl.BlockSpec((1, N_GROUPS, V_ROWS, tm), lambda i: (i // tiles_per_seq, 0, 0, i % tiles_per_seq)))
    outs = [grouped(N_HEADS), flat(LANES, F32), flat(LANES, F32), grouped(N_GROUPS), grouped(N_GROUPS),
            values_t, values_t, flat(LANES, F32), flat(POOL_COLS, F32), flat(2 * d, F32)]
    return pl.pallas_call(
        _proj_kernel,
        out_shape=[o[0] for o in outs],
        grid=(n // tm,),
        in_specs=[pl.BlockSpec((tm, d), row), pl.BlockSpec((1, d), const),
                  pl.BlockSpec(w_packed.shape, const, pipeline_mode=pl.Buffered(1)),
                  pl.BlockSpec(qf.shape, const), pl.BlockSpec((tm, LANES), lambda i: (i % tiles_per_seq, 0))],
        out_specs=[o[1] for o in outs],
        compiler_params=_params("parallel"),
        name="proj",
    )(x, norm_g, w_packed, qf, kf)


def _compress_kernel(k_ref, v_ref, pos_ref, wk1_ref, wk2_ref, wv1_ref, wv2_ref, feat_ref, ko_ref, vo_ref):
    n_chunk = k_ref.shape[0] // CMP_STRIDE
    hidden = wk2_ref.shape[0]

    def compress(x_ref, w1_ref, w2_ref):
        first = second = None
        for r in range(CMP_STRIDE):
            rows = x_ref[pl.ds(r, n_chunk, stride=CMP_STRIDE), :]
            a = _dot((rows + pos_ref[r:r + 1, :]).astype(BF16), w1_ref[r])
            b = _dot((rows + pos_ref[CMP_STRIDE + r:CMP_STRIDE + r + 1, :]).astype(BF16), w1_ref[CMP_STRIDE + r])
            first = a if first is None else first + a
            second = b if second is None else second + b
        act = jax.nn.gelu(first + pltpu.roll(second, n_chunk - 1, 0)).astype(BF16)
        return [_dot(act[:, g * hidden:(g + 1) * hidden], w2_ref[...]) for g in range(N_GROUPS)]

    for g, k_cmp in enumerate(compress(k_ref, wk1_ref, wk2_ref)):
        ko_ref[0, g] = jnp.concatenate([k_cmp, feat_ref[...]], axis=-1).astype(BF16)
    _store_values_t(vo_ref, jnp.concatenate(compress(v_ref, wv1_ref, wv2_ref), axis=-1))


def _block_diag_w1(w1):
    w = w1.reshape(CMP_LEN, HEAD_DIM, -1)
    z = jnp.zeros_like(w)
    return jnp.concatenate([jnp.concatenate([w, z], axis=2), jnp.concatenate([z, w], axis=2)], axis=1).astype(BF16)


def _compress(kc, vc, pos, wk1, wk2, wv1, wv2, batch, seq):
    n_chunk = seq // CMP_STRIDE
    rows = pl.BlockSpec((seq, LANES), lambda b: (b, 0))
    full = lambda a: pl.BlockSpec(a.shape, lambda b: (0,) * a.ndim, pipeline_mode=pl.Buffered(1))
    pos2 = jnp.concatenate([pos] * N_GROUPS, axis=1)
    feat = _key_features(np.arange(n_chunk) * CMP_STRIDE + CMP_LEN - 1, HEAD_DIM, 0)
    args = (pos2, _block_diag_w1(wk1), wk2.astype(BF16), _block_diag_w1(wv1), wv2.astype(BF16), feat)
    return pl.pallas_call(
        _compress_kernel,
        out_shape=[jax.ShapeDtypeStruct((batch, N_GROUPS, n_chunk, LANES), BF16),
                   jax.ShapeDtypeStruct((batch, N_GROUPS, V_ROWS, n_chunk), BF16)],
        grid=(batch,),
        in_specs=[rows, rows] + [full(a) for a in args],
        out_specs=[pl.BlockSpec((1, N_GROUPS, n_chunk, LANES), lambda b: (b, 0, 0, 0)),
                   pl.BlockSpec((1, N_GROUPS, V_ROWS, n_chunk), lambda b: (b, 0, 0, 0))],
        compiler_params=_params("parallel"),
        name="compress",
    )(kc, vc, *args)


SLC_CHUNK = 256
BLOCKS_PER_CHUNK = SLC_CHUNK // SEL_LEN
HEAD_BLOCKS = 2


def _nsa_kernel(q_ref, kc_ref, vct_ref, ks_ref, vst_ref, kw_ref, vwt_ref, gate_ref, ovt_ref, o_ref,
                st_ref, p_ref, gt_ref, selt_ref, list_ref, m_ref, acc_ref, out_ref, *, tq, seq):
    n_sel = seq // SEL_LEN
    n_chunks = seq // SLC_CHUNK
    n_cmp = kc_ref.shape[2]
    n_win = WINDOW + tq
    q0 = pl.program_id(1) * tq
    gt_ref[...] = gate_ref[...].T
    pairs = [(g, g * HEADS_PER_GROUP + 2 * j) for g in range(N_GROUPS) for j in range(HEADS_PER_GROUP // 2)]

    def gate_row(head, branch):
        c = 3 * head + branch
        return gt_ref[c:c + 1, :]

    def out_rows(head):
        g, hh = divmod(head, HEADS_PER_GROUP)
        return hh, slice(g * HEAD_DIM, (g + 1) * HEAD_DIM)

    def stage_scores(keys, n_keys, row0=0):
        for g, head0 in pairs:
            q_pair = q_ref[0, head0:head0 + 2].reshape(2 * tq, LANES)
            scores = _dot_nt(keys(g), q_pair)
            for i in range(2):
                st_ref[head0 + i, row0:row0 + n_keys, :] = scores[:, i * tq:(i + 1) * tq]

    def finish_pair(head0, pv, branch, first, ok=None):
        invs = []
        for i in range(2):
            cols = slice(i * tq, (i + 1) * tq)
            inv = 1.0 / pv[HEAD_DIM:HEAD_DIM + 1, cols]
            if ok is not None:
                inv = jnp.where(ok, inv, 0.0)
            hh, rows = out_rows(head0 + i)
            contrib = pv[0:HEAD_DIM, cols] * (inv * gate_row(head0 + i, branch))
            out_ref[hh, rows, :] = contrib if first else out_ref[hh, rows, :] + contrib
            invs.append(inv)
        return invs

    t_c = q0 + lax.broadcasted_iota(jnp.int32, (n_cmp, tq), 1)
    end_c = lax.broadcasted_iota(jnp.int32, (n_cmp, tq), 0) * CMP_STRIDE + (CMP_LEN - 1)
    bias_c = jnp.where(t_c >= end_c, 0.0, NEG_BIG)
    has_cmp = (q0 + lax.broadcasted_iota(jnp.int32, (1, tq), 1)) >= CMP_LEN - 1
    inv_c = [None] * N_HEADS

    def cmp_pair(g, head0):
        for i in range(2):
            x = st_ref[head0 + i, 0:n_cmp, :] + bias_c
            p_ref[head0 // 2, 0:n_cmp, i * tq:(i + 1) * tq] = jnp.exp2(x - jnp.max(x, axis=0, keepdims=True)).astype(BF16)
        pv = _dot(vct_ref[0, g], p_ref[head0 // 2, 0:n_cmp, :])
        inv_c[head0], inv_c[head0 + 1] = finish_pair(head0, pv, 0, True, has_cmp)

    stage_scores(lambda g: kc_ref[0, g], n_cmp)
    for g, head0 in pairs:
        cmp_pair(g, head0)

    raw = [_dot(ovt_ref[...], p_ref[pair, 0:n_cmp, :]) for pair in range(N_HEADS // 2)]

    w0 = pl.multiple_of(jnp.maximum(q0 - WINDOW, 0), tq)
    stage_scores(lambda g: kw_ref[0, g, pl.ds(w0, n_win), :], n_win)
    head_keys = HEAD_BLOCKS * SEL_LEN
    stage_scores(lambda g: ks_ref[0, g, 0:head_keys, :], head_keys, n_win)

    dist_w = (q0 + lax.broadcasted_iota(jnp.int32, (n_win, tq), 1)) - (w0 + lax.broadcasted_iota(jnp.int32, (n_win, tq), 0))
    bias_w = jnp.where((dist_w >= 0) & (dist_w < WINDOW), 0.0, NEG_BIG)

    def win_pair(g, head0):
        for i in range(2):
            x = st_ref[head0 + i, 0:n_win, :] + bias_w
            p_ref[head0 // 2, 0:n_win, i * tq:(i + 1) * tq] = jnp.exp2(x - jnp.max(x, axis=0, keepdims=True)).astype(BF16)
        pv = _dot(vwt_ref[0, g, :, pl.ds(w0, n_win)], p_ref[head0 // 2, 0:n_win, :])
        finish_pair(head0, pv, 2, False)

    for g, head0 in pairs:
        win_pair(g, head0)

    blk = lax.broadcasted_iota(jnp.int32, (n_sel, tq), 0)
    cur = (q0 + lax.broadcasted_iota(jnp.int32, (n_sel, tq), 1)) // SEL_LEN
    valid = blk <= cur
    forced = valid & ((blk == 0) | (blk == cur) | (blk == cur - 1))
    blk_f = blk.astype(F32)
    score = []
    for g in range(N_GROUPS):
        total = None
        for head in range(g * HEADS_PER_GROUP, (g + 1) * HEADS_PER_GROUP):
            part = raw[head // 2][:, (head % 2) * tq:(head % 2 + 1) * tq] * inv_c[head]
            total = part if total is None else total + part
        score.append(jnp.where(forced, TOPK_TAKEN, jnp.where(valid, total, -TOPK_BIG)))

    for _ in range(min(N_SELECT, n_sel) - N_FORCED):
        for g in range(N_GROUPS):
            best = jnp.max(score[g], axis=0, keepdims=True)
            pick = jnp.min(jnp.where(score[g] == best, blk_f, float(n_sel)), axis=0, keepdims=True)
            score[g] = jnp.where(blk_f == pick, TOPK_TAKEN, score[g])
    sel = [jnp.where(valid & (score[g] == TOPK_TAKEN), 1.0, 0.0) for g in range(N_GROUPS)]
    for g in range(N_GROUPS):
        selt_ref[g] = jnp.where(blk >= HEAD_BLOCKS, sel[g], 0.0)
    sel_any = jnp.where(blk >= HEAD_BLOCKS, sel[0] + sel[1], 0.0)
    n_items = jnp.int32(0)
    for c in range(n_chunks):
        any_sel = jnp.max(sel_any[c * BLOCKS_PER_CHUNK:(c + 1) * BLOCKS_PER_CHUNK, :])
        list_ref[n_items] = c
        n_items = n_items + (any_sel > 0.5).astype(jnp.int32)

    causal_h = (q0 + lax.broadcasted_iota(jnp.int32, (head_keys, tq), 1)) >= lax.broadcasted_iota(jnp.int32, (head_keys, tq), 0)
    head_rows = slice(n_win, n_win + head_keys)
    for g, head0 in pairs:
        if head0 % HEADS_PER_GROUP == 0:
            sel_keys = jnp.concatenate([jnp.broadcast_to(sel[g][i:i + 1, :], (SEL_LEN, tq)) for i in range(HEAD_BLOCKS)], axis=0)
            bias = jnp.where(causal_h & (sel_keys > 0.5), 0.0, NEG_BIG)
        for i in range(2):
            x = st_ref[head0 + i, head_rows, :] + bias
            m_new = jnp.max(x, axis=0, keepdims=True)
            m_ref[head0 + i:head0 + i + 1, :] = m_new
            p_ref[head0 // 2, head_rows, i * tq:(i + 1) * tq] = jnp.exp2(x - m_new).astype(BF16)
    for g, head0 in pairs:
        acc_ref[head0 // 2] = _dot(vst_ref[0, g, :, 0:head_keys], p_ref[head0 // 2, head_rows, :])

    base = (lax.broadcasted_iota(jnp.int32, (SLC_CHUNK, tq), 1)
            - lax.broadcasted_iota(jnp.int32, (SLC_CHUNK, tq), 0))

    def stage_chunk(item, row0):
        c = list_ref[item]
        k0 = pl.multiple_of(c * SLC_CHUNK, SLC_CHUNK)
        causal = base >= k0 - q0
        for g, head0 in pairs:
            if head0 % HEADS_PER_GROUP == 0:
                sel_keys = jnp.concatenate(
                    [jnp.broadcast_to(selt_ref[g, pl.ds(c * BLOCKS_PER_CHUNK + i, 1), :], (SEL_LEN, tq))
                     for i in range(BLOCKS_PER_CHUNK)], axis=0)
                bias = jnp.where(causal & (sel_keys > 0.5), 0.0, NEG_BIG)
                bias2 = jnp.concatenate([bias, bias], axis=1)
            q_pair = q_ref[0, head0:head0 + 2].reshape(2 * tq, LANES)
            scores = _dot_nt(ks_ref[0, g, pl.ds(k0, SLC_CHUNK), :], q_pair) + bias2
            for i in range(2):
                st_ref[head0 + i, row0:row0 + SLC_CHUNK, :] = scores[:, i * tq:(i + 1) * tq]

    def consume_chunk(item, row0):
        k0 = pl.multiple_of(list_ref[item] * SLC_CHUNK, SLC_CHUNK)
        for g, head0 in pairs:
            alphas = []
            for i, head in enumerate((head0, head0 + 1)):
                x = st_ref[head, row0:row0 + SLC_CHUNK, :]
                m_old = m_ref[head:head + 1, :]
                m_new = jnp.maximum(m_old, jnp.max(x, axis=0, keepdims=True))
                alphas.append(jnp.exp2(m_old - m_new))
                m_ref[head:head + 1, :] = m_new
                p_ref[head0 // 2, 0:SLC_CHUNK, i * tq:(i + 1) * tq] = jnp.exp2(x - m_new).astype(BF16)
            pv = _dot(vst_ref[0, g, :, pl.ds(k0, SLC_CHUNK)], p_ref[head0 // 2, 0:SLC_CHUNK, :])
            acc_ref[head0 // 2] = acc_ref[head0 // 2] * jnp.concatenate(alphas, axis=1) + pv

    last = jnp.maximum(n_items - 1, 0)
    stage_chunk(0, 0)

    def two_chunks(j, carry):
        stage_chunk(jnp.minimum(2 * j + 1, last), SLC_CHUNK)
        consume_chunk(2 * j, 0)

        @pl.when(2 * j + 1 < n_items)
        def _():
            stage_chunk(jnp.minimum(2 * j + 2, last), 0)
            consume_chunk(2 * j + 1, SLC_CHUNK)
        return carry

    lax.fori_loop(0, (n_items + 1) // 2, two_chunks, 0)
    for g, head0 in pairs:
        finish_pair(head0, acc_ref[head0 // 2], 1, False)

    for hh in range(HEADS_PER_GROUP):
        o_ref[:, hh * LANES:(hh + 1) * LANES] = out_ref[hh].T.astype(BF16)


def _overlap_t(seq):
    n_cmp = (seq - CMP_LEN) // CMP_STRIDE + 1
    n_sel = seq // SEL_LEN
    cs = np.arange(n_cmp)[:, None] * CMP_STRIDE
    ss = np.arange(n_sel)[None, :] * SEL_LEN
    ov = np.clip(np.minimum(cs + CMP_LEN, ss + SEL_LEN) - np.maximum(cs, ss), 0, None) / CMP_LEN
    out = np.zeros((n_sel, seq // CMP_STRIDE), np.float32)
    out[:, :n_cmp] = ov.T
    return jnp.asarray(out, dtype=BF16)


def _nsa(q, kc, vct, ks, vst, kw, vwt, gates, *, tq=128):
    batch, _, seq, _ = q.shape
    n_cmp = kc.shape[2]
    n_sel = seq // SEL_LEN
    assert seq % SLC_CHUNK == 0 and seq >= WINDOW + tq and SLC_CHUNK % tq == 0 and tq == LANES
    assert n_sel <= 2 * SEL_LEN
    ovt = _overlap_t(seq)
    keys = lambda n: pl.BlockSpec((1, N_GROUPS, n, LANES), lambda b, i: (b, 0, 0, 0))
    vals = lambda n: pl.BlockSpec((1, N_GROUPS, V_ROWS, n), lambda b, i: (b, 0, 0, 0))
    tiles = seq // tq
    s_rows = max(n_cmp, WINDOW + tq + HEAD_BLOCKS * SEL_LEN, 2 * SLC_CHUNK)
    return pl.pallas_call(
        functools.partial(_nsa_kernel, tq=tq, seq=seq),
        out_shape=jax.ShapeDtypeStruct((batch * seq, HEADS_PER_GROUP * LANES), BF16),
        grid=(batch, tiles),
        in_specs=[
            pl.BlockSpec((1, N_HEADS, tq, LANES), lambda b, i: (b, 0, i, 0)),
            keys(n_cmp), vals(n_cmp), keys(seq), vals(seq), keys(seq), vals(seq),
            pl.BlockSpec((tq, LANES), lambda b, i: (b * tiles + i, 0)),
            pl.BlockSpec(ovt.shape, lambda b, i: (0, 0)),
        ],
        out_specs=pl.BlockSpec((tq, HEADS_PER_GROUP * LANES), lambda b, i: (b * tiles + i, 0)),
        scratch_shapes=[
            pltpu.VMEM((N_HEADS, s_rows, tq), F32),
            pltpu.VMEM((N_HEADS // 2, s_rows, 2 * tq), BF16),
            pltpu.VMEM((LANES, tq), F32),
            pltpu.VMEM((N_GROUPS, n_sel, tq), F32),
            pltpu.SMEM((seq // SLC_CHUNK,), jnp.int32),
            pltpu.VMEM((N_HEADS, tq), F32),
            pltpu.VMEM((N_HEADS // 2, V_ROWS, 2 * tq), F32),
            pltpu.VMEM((HEADS_PER_GROUP, N_GROUPS * HEAD_DIM, tq), F32),
        ],
        compiler_params=_params("parallel", "arbitrary"),
        name="nsa",
    )(q, kc, vct, ks, vst, kw, vwt, gates, ovt)


HALO = max(POOL_WINDOWS)


def _merge_kernel(x_ref, u_ref, halo_ref, gm_ref, on_ref, pw_ref, ps_ref, wbp_ref, wbn_ref, wo_ref,
                  o_ref, ext_ref, *, tm, seq):
    d = x_ref.shape[1]
    pos0 = (pl.program_id(0) * tm) % seq
    ext_ref[0:HALO, :] = jnp.where(pos0 == 0, 0.0, halo_ref[...])
    ext_ref[HALO:HALO + tm, :] = u_ref[...]
    pos = (pos0 + lax.broadcasted_iota(jnp.int32, (tm, POOL_GROUP_DIM), 0)).astype(F32)
    mixed = []
    for gi, w in enumerate(POOL_WINDOWS):
        cols = slice(gi * POOL_GROUP_DIM, (gi + 1) * POOL_GROUP_DIM)
        u = ext_ref[HALO:HALO + tm, cols]
        total = u
        for lag in range(1, w):
            total = total + ext_ref[HALO - lag:HALO - lag + tm, cols]
        delta = total / jnp.minimum(pos + 1.0, float(w)) - u
        mixed.append(_dot(delta.astype(BF16), pw_ref[gi]) * ps_ref[:, cols])
    mixed = jnp.concatenate(mixed, axis=-1).astype(BF16)
    a = _dot(mixed, wbp_ref[...])
    b = _dot(on_ref[...], wbn_ref[...])
    merged = gm_ref[:, 0:d] * a + gm_ref[:, d:2 * d] * b
    o_ref[...] = x_ref[...] + _dot(merged.astype(BF16), wo_ref[...])


def _merge(x, u, gm, o_nsa, pool_w, pool_scale, w_bp, w_bn, w_out, seq, *, tm=512):
    n, d = x.shape
    row = lambda i: (i, 0)
    full = lambda a: pl.BlockSpec(a.shape, lambda i: (0,) * a.ndim, pipeline_mode=pl.Buffered(1))
    halo_blocks = tm // HALO
    return pl.pallas_call(
        functools.partial(_merge_kernel, tm=tm, seq=seq),
        out_shape=jax.ShapeDtypeStruct((n, d), F32),
        grid=(n // tm,),
        in_specs=[
            pl.BlockSpec((tm, d), row),
            pl.BlockSpec((tm, POOL_COLS), row),
            pl.BlockSpec((HALO, POOL_COLS), lambda i: (jnp.maximum(i * halo_blocks - 1, 0), 0)),
            pl.BlockSpec((tm, 2 * d), row),
            pl.BlockSpec((tm, d), row),
            full(pool_w), full(pool_scale), full(w_bp), full(w_bn), full(w_out),
        ],
        out_specs=pl.BlockSpec((tm, d), row),
        scratch_shapes=[pltpu.VMEM((HALO + tm, POOL_COLS), F32)],
        compiler_params=_params("parallel"),
        name="merge",
    )(x, u, u, gm, o_nsa, pool_w, pool_scale, w_bp, w_bn, w_out)


def kernel(x, ffn1_norm, ffn1_w_gate, ffn1_w_up, ffn1_w_down, mix_norm, w_in, cmp_pos, cmp_k_w1, cmp_k_w2, cmp_v_w1, cmp_v_w2, pool_w, pool_scale, w_branch_pool, w_branch_nsa, w_out, ffn2_norm, ffn2_w_gate, ffn2_w_up, ffn2_w_down, final_norm):
    batch, seq, d = x.shape
    depth = w_in.shape[0]
    xf = x.reshape(batch * seq, d)
    bf = lambda a: a.astype(BF16)
    row = lambda a: a.reshape(1, -1)
    for l in range(depth):
        xf = _ffn(xf, row(ffn1_norm[l]), bf(ffn1_w_gate[l]), bf(ffn1_w_up[l]), bf(ffn1_w_down[l]),
                  row(final_norm), final_norm=False)

        q, kc, vc, ks, kw, vs, vw, gates, u, gm = _proj(xf, row(mix_norm[l]), _pack_w_in(w_in[l], d), batch, seq)
        k_cmp, v_cmp = _compress(kc, vc, cmp_pos[l], cmp_k_w1[l], cmp_k_w2[l], cmp_v_w1[l], cmp_v_w2[l], batch, seq)
        o_nsa = _nsa(q, k_cmp, v_cmp, ks, vs, kw, vw, gates)

        w_bn = w_branch_nsa[l].reshape(N_GROUPS, HEADS_PER_GROUP, HEAD_DIM, d).transpose(1, 0, 2, 3).reshape(-1, d)
        xf = _merge(xf, u, gm, o_nsa, bf(pool_w[l]), row(pool_scale[l]), bf(w_branch_pool[l]), bf(w_bn),
                    bf(w_out[l]), seq)

        xf = _ffn(xf, row(ffn2_norm[l]), bf(ffn2_w_gate[l]), bf(ffn2_w_up[l]), bf(ffn2_w_down[l]),
                  row(final_norm), final_norm=(l == depth - 1))
    return xf.reshape(batch, seq, d)
```

```python
import functools

import jax
import jax.numpy as jnp
import numpy as np
from jax import lax
from jax.experimental import pallas as pl
from jax.experimental.pallas import tpu as pltpu

N_HEADS = 16
N_GROUPS = 2
HEADS_PER_GROUP = N_HEADS // N_GROUPS
HEAD_DIM = 64
CMP_LEN = 32
CMP_STRIDE = 16
SEL_LEN = 64
N_SELECT = 16
WINDOW = 512
POOL_WINDOWS = (2, 4, 8, 16)
POOL_GROUP_DIM = 128
RMS_EPS = 1e-6
ALIBI_MAX_BIAS = 8.0

LANES = 128
NEG_BIG = -1e30
TOPK_BIG = 1e30
N_FORCED = 3
TOPK_TAKEN = -float(2 ** 101)
VMEM_LIMIT = 48 * 1024 * 1024

BF16 = jnp.bfloat16
F32 = jnp.float32


def _dot(a, b):
    return jnp.dot(a, b, preferred_element_type=F32)


def _dot_nt(a, b):
    return lax.dot_general(a, b, (((1,), (1,)), ((), ())), preferred_element_type=F32)


def _rms(x, g):
    return x * lax.rsqrt(jnp.mean(x * x, axis=-1, keepdims=True) + RMS_EPS) * g


def _params(*sem):
    return pltpu.CompilerParams(dimension_semantics=sem, vmem_limit_bytes=VMEM_LIMIT)


def _ffn_kernel(x_ref, g_ref, wg_ref, wu_ref, wd_ref, fin_ref, o_ref, *, final_norm):
    x = x_ref[...]
    xn = _rms(x, g_ref[...]).astype(BF16)
    gate = _dot(xn, wg_ref[...])
    up = _dot(xn, wu_ref[...])
    act = (gate * jax.nn.sigmoid(gate)) * up
    y = x + 0.5 * _dot(act.astype(BF16), wd_ref[...])
    if final_norm:
        y = _rms(y, fin_ref[...])
    o_ref[...] = y


def _ffn(x, norm_g, wg, wu, wd, fin_g, *, final_norm, tm=512):
    n, d = x.shape
    row = lambda i: (i, 0)
    resident = lambda a: pl.BlockSpec(a.shape, lambda i: (0, 0), pipeline_mode=pl.Buffered(1))
    return pl.pallas_call(
        functools.partial(_ffn_kernel, final_norm=final_norm),
        out_shape=jax.ShapeDtypeStruct((n, d), F32),
        grid=(n // tm,),
        in_specs=[pl.BlockSpec((tm, d), row), resident(norm_g), resident(wg), resident(wu), resident(wd),
                  resident(fin_g)],
        out_specs=pl.BlockSpec((tm, d), row),
        compiler_params=_params("parallel"),
        name="ffn",
    )(x, norm_g, wg, wu, wd, fin_g)


LOG2E = 1.4426950408889634
N_FEAT = 6


def _bf16_pieces(x):
    x = np.asarray(x, np.float32)
    s1 = x.astype(BF16).astype(np.float32)
    s2 = (x - s1).astype(BF16).astype(np.float32)
    s3 = (x - s1 - s2).astype(BF16).astype(np.float32)
    return s1, s2, s3


def _query_features():
    slopes = np.float32(2.0) ** (-ALIBI_MAX_BIAS * np.arange(1, N_HEADS + 1, dtype=np.float32) / N_HEADS)
    s1, s2, s3 = _bf16_pieces(slopes * np.float32(LOG2E))
    feat = np.zeros((N_HEADS, LANES), np.float32)
    feat[:, HEAD_DIM:HEAD_DIM + N_FEAT] = np.stack([s1, s2, s3, SEL_LEN * s1, SEL_LEN * s2, SEL_LEN * s3], axis=1)
    return jnp.asarray(feat)


def _key_features(pos, width, offset):
    pos = np.asarray(pos)
    a, b = (pos // SEL_LEN).astype(np.float32), (pos % SEL_LEN).astype(np.float32)
    feat = np.zeros((len(pos), width), np.float32)
    feat[:, offset:offset + N_FEAT] = np.stack([b, b, b, a, a, a], axis=1)
    return jnp.asarray(feat)


Q_COLS = N_HEADS * HEAD_DIM
CMP_OFF = Q_COLS
KEY_OFF = CMP_OFF + 2 * LANES
VAL_OFF = KEY_OFF + 2 * LANES
GATE_OFF = VAL_OFF + 2 * LANES
POOL_OFF = GATE_OFF + LANES
POOL_COLS = len(POOL_WINDOWS) * POOL_GROUP_DIM
MERGE_OFF = POOL_OFF + POOL_COLS


V_ROWS = HEAD_DIM + 16


def _store_values_t(v_ref, v):
    vt = v.T
    for g in range(N_GROUPS):
        v_ref[0, g, 0:HEAD_DIM, :] = vt[g * HEAD_DIM:(g + 1) * HEAD_DIM, :].astype(BF16)
        v_ref[0, g, HEAD_DIM:V_ROWS, :] = jnp.ones((V_ROWS - HEAD_DIM, v.shape[0]), BF16)


def _proj_kernel(x_ref, g_ref, w_ref, qf_ref, kf_ref, q_ref, kc_ref, vc_ref, ks_ref, kw_ref, vs_ref, vw_ref,
                 gate_ref, u_ref, gm_ref):
    d = x_ref.shape[1]
    hn = _rms(x_ref[...], g_ref[...]).astype(BF16)
    q_scale = HEAD_DIM ** -0.5 * LOG2E
    low = lax.broadcasted_iota(jnp.int32, (x_ref.shape[0], LANES), 1) < HEAD_DIM
    for j in range(N_HEADS // 4):
        q4 = _dot(hn, w_ref[:, j * 2 * LANES:(j + 1) * 2 * LANES]) * q_scale
        for b in range(2):
            both = q4[:, b * LANES:(b + 1) * LANES]
            h = 4 * j + 2 * b
            q_ref[0, h] = (jnp.where(low, both, 0.0) + qf_ref[h:h + 1, :]).astype(BF16)
            q_ref[0, h + 1] = (jnp.where(low, pltpu.roll(both, HEAD_DIM, 1), 0.0) + qf_ref[h + 1:h + 2, :]).astype(BF16)
    cmp_in = _dot(hn, w_ref[:, CMP_OFF:CMP_OFF + 2 * LANES])
    kc_ref[...] = cmp_in[:, 0:LANES]
    vc_ref[...] = cmp_in[:, LANES:2 * LANES]
    keys = _dot(hn, w_ref[:, KEY_OFF:KEY_OFF + 2 * LANES])
    kf = kf_ref[...]
    for i, k_ref in enumerate((ks_ref, kw_ref)):
        both = keys[:, i * LANES:(i + 1) * LANES]
        k_ref[0, 0] = (jnp.where(low, both, 0.0) + kf).astype(BF16)
        k_ref[0, 1] = (jnp.where(low, pltpu.roll(both, HEAD_DIM, 1), 0.0) + kf).astype(BF16)
    vals = _dot(hn, w_ref[:, VAL_OFF:VAL_OFF + 2 * LANES])
    for i, v_ref in enumerate((vs_ref, vw_ref)):
        _store_values_t(v_ref, vals[:, i * LANES:(i + 1) * LANES])
    gate_ref[...] = jax.nn.sigmoid(_dot(hn, w_ref[:, GATE_OFF:GATE_OFF + LANES]))
    u_ref[...] = _dot(hn, w_ref[:, POOL_OFF:POOL_OFF + POOL_COLS])
    gm_ref[...] = jax.nn.sigmoid(_dot(hn, w_ref[:, MERGE_OFF:MERGE_OFF + 2 * d]))


def _pack_w_in(w_in, d):
    w_in = w_in.astype(BF16)
    qw = N_HEADS * HEAD_DIM
    kvw = N_GROUPS * HEAD_DIM

    kc, vc, ks, vs, kw, vw = (w_in[:, qw + i * kvw:qw + (i + 1) * kvw] for i in range(6))
    off = qw + 6 * kvw
    n_g = 3 * N_HEADS
    w_g = jnp.pad(w_in[:, off:off + n_g], ((0, 0), (0, LANES - n_g)))
    w_rest = w_in[:, off + n_g:]
    return jnp.concatenate([w_in[:, :qw], kc, vc, ks, kw,
                            vs, vw, w_g, w_rest], axis=1)


def _proj(x, norm_g, w_packed, batch, seq, *, tm=512):
    n, d = x.shape
    tiles_per_seq = seq // tm
    row = lambda i: (i, 0)
    const = lambda i: (0, 0)
    per_group = lambda i: (i // tiles_per_seq, 0, i % tiles_per_seq, 0)
    qf = _query_features()
    kf = _key_features(np.arange(seq), LANES, HEAD_DIM)
    flat = lambda width, dtype: (jax.ShapeDtypeStruct((n, width), dtype), pl.BlockSpec((tm, width), row))
    grouped = lambda count: (jax.ShapeDtypeStruct((batch, count, seq, LANES), BF16),
                             pl.BlockSpec((1, count, tm, LANES), per_group))
    values_t = (jax.ShapeDtypeStruct((batch, N_GROUPS, V_ROWS, seq), BF16),
                pl.BlockSpec((1, N_GROUPS, V_ROWS, tm), lambda i: (i // tiles_per_seq, 0, 0, i % tiles_per_seq)))
    outs = [grouped(N_HEADS), flat(LANES, F32), flat(LANES, F32), grouped(N_GROUPS), grouped(N_GROUPS),
            values_t, values_t, flat(LANES, F32), flat(POOL_COLS, F32), flat(2 * d, F32)]
    return pl.pallas_call(
        _proj_kernel,
        out_shape=[o[0] for o in outs],
        grid=(n // tm,),
        in_specs=[pl.BlockSpec((tm, d), row), pl.BlockSpec((1, d), const),
                  pl.BlockSpec(w_packed.shape, const, pipeline_mode=pl.Buffered(1)),
                  pl.BlockSpec(qf.shape, const), pl.BlockSpec((tm, LANES), lambda i: (i % tiles_per_seq, 0))],
        out_specs=[o[1] for o in outs],
        compiler_params=_params("parallel"),
        name="proj",
    )(x, norm_g, w_packed, qf, kf)


def _compress_kernel(k_ref, v_ref, pos_ref, wk1_ref, wk2_ref, wv1_ref, wv2_ref, feat_ref, ko_ref, vo_ref):
    n_chunk = k_ref.shape[0] // CMP_STRIDE
    hidden = wk2_ref.shape[0]

    def compress(x_ref, w1_ref, w2_ref):
        first = second = None
        for r in range(CMP_STRIDE):
            rows = x_ref[pl.ds(r, n_chunk, stride=CMP_STRIDE), :]
            a = _dot((rows + pos_ref[r:r + 1, :]).astype(BF16), w1_ref[r])
            b = _dot((rows + pos_ref[CMP_STRIDE + r:CMP_STRIDE + r + 1, :]).astype(BF16), w1_ref[CMP_STRIDE + r])
            first = a if first is None else first + a
            second = b if second is None else second + b
        act = jax.nn.gelu(first + pltpu.roll(second, n_chunk - 1, 0)).astype(BF16)
        return [_dot(act[:, g * hidden:(g + 1) * hidden], w2_ref[...]) for g in range(N_GROUPS)]

    for g, k_cmp in enumerate(compress(k_ref, wk1_ref, wk2_ref)):
        ko_ref[0, g] = jnp.concatenate([k_cmp, feat_ref[...]], axis=-1).astype(BF16)
    _store_values_t(vo_ref, jnp.concatenate(compress(v_ref, wv1_ref, wv2_ref), axis=-1))


def _block_diag_w1(w1):
    w = w1.reshape(CMP_LEN, HEAD_DIM, -1)
    z = jnp.zeros_like(w)
    return jnp.concatenate([jnp.concatenate([w, z], axis=2), jnp.concatenate([z, w], axis=2)], axis=1).astype(BF16)


def _compress(kc, vc, pos, wk1, wk2, wv1, wv2, batch, seq):
    n_chunk = seq // CMP_STRIDE
    rows = pl.BlockSpec((seq, LANES), lambda b: (b, 0))
    full = lambda a: pl.BlockSpec(a.shape, lambda b: (0,) * a.ndim, pipeline_mode=pl.Buffered(1))
    pos2 = jnp.concatenate([pos] * N_GROUPS, axis=1)
    feat = _key_features(np.arange(n_chunk) * CMP_STRIDE + CMP_LEN - 1, HEAD_DIM, 0)
    args = (pos2, _block_diag_w1(wk1), wk2.astype(BF16), _block_diag_w1(wv1), wv2.astype(BF16), feat)
    return pl.pallas_call(
        _compress_kernel,
        out_shape=[jax.ShapeDtypeStruct((batch, N_GROUPS, n_chunk, LANES), BF16),
                   jax.ShapeDtypeStruct((batch, N_GROUPS, V_ROWS, n_chunk), BF16)],
        grid=(batch,),
        in_specs=[rows, rows] + [full(a) for a in args],
        out_specs=[pl.BlockSpec((1, N_GROUPS, n_chunk, LANES), lambda b: (b, 0, 0, 0)),
                   pl.BlockSpec((1, N_GROUPS, V_ROWS, n_chunk), lambda b: (b, 0, 0, 0))],
        compiler_params=_params("parallel"),
        name="compress",
    )(kc, vc, *args)


SLC_CHUNK = 256
BLOCKS_PER_CHUNK = SLC_CHUNK // SEL_LEN
HEAD_BLOCKS = 2


def _nsa_kernel(q_ref, kc_ref, vct_ref, ks_ref, vst_ref, kw_ref, vwt_ref, gate_ref, o_ref,
                st_ref, p_ref, gt_ref, selt_ref, list_ref, m_ref, acc_ref, out_ref, *, tq, seq):
    n_sel = seq // SEL_LEN
    n_chunks = seq // SLC_CHUNK
    n_cmp = kc_ref.shape[2]
    n_win = WINDOW + tq
    q0 = pl.program_id(1) * tq
    gt_ref[...] = gate_ref[...].T
    pairs = [(g, g * HEADS_PER_GROUP + 2 * j) for g in range(N_GROUPS) for j in range(HEADS_PER_GROUP // 2)]

    def gate_row(head, branch):
        c = 3 * head + branch
        return gt_ref[c:c + 1, :]

    def out_rows(head):
        g, hh = divmod(head, HEADS_PER_GROUP)
        return hh, slice(g * HEAD_DIM, (g + 1) * HEAD_DIM)

    def stage_scores(keys, n_keys, row0=0):
        for g, head0 in pairs:
            q_pair = q_ref[0, head0:head0 + 2].reshape(2 * tq, LANES)
            scores = _dot_nt(keys(g), q_pair)
            for i in range(2):
                st_ref[head0 + i, row0:row0 + n_keys, :] = scores[:, i * tq:(i + 1) * tq]

    def finish_pair(head0, pv, branch, first, ok=None):
        invs = []
        for i in range(2):
            cols = slice(i * tq, (i + 1) * tq)
            inv = 1.0 / pv[HEAD_DIM:HEAD_DIM + 1, cols]
            if ok is not None:
                inv = jnp.where(ok, inv, 0.0)
            hh, rows = out_rows(head0 + i)
            contrib = pv[0:HEAD_DIM, cols] * (inv * gate_row(head0 + i, branch))
            out_ref[hh, rows, :] = contrib if first else out_ref[hh, rows, :] + contrib
            invs.append(inv)
        return invs

    t_c = q0 + lax.broadcasted_iota(jnp.int32, (n_cmp, tq), 1)
    end_c = lax.broadcasted_iota(jnp.int32, (n_cmp, tq), 0) * CMP_STRIDE + (CMP_LEN - 1)
    bias_c = jnp.where(t_c >= end_c, 0.0, NEG_BIG)
    has_cmp = (q0 + lax.broadcasted_iota(jnp.int32, (1, tq), 1)) >= CMP_LEN - 1
    inv_c = [None] * N_HEADS
    raw = [None] * (N_HEADS // 2)

    def cmp_pair(g, head0):
        for i in range(2):
            x = st_ref[head0 + i, 0:n_cmp, :] + bias_c
            p_ref[head0 // 2, 0:n_cmp, i * tq:(i + 1) * tq] = jnp.exp2(x - jnp.max(x, axis=0, keepdims=True)).astype(BF16)
        both = _dot(vct_ref[0, g], p_ref[head0 // 2, 0:n_cmp, :])
        raw[head0 // 2] = both[V_ROWS:, :]
        inv_c[head0], inv_c[head0 + 1] = finish_pair(head0, both[0:V_ROWS, :], 0, True, has_cmp)

    stage_scores(lambda g: kc_ref[0, g], n_cmp)
    for g, head0 in pairs:
        cmp_pair(g, head0)

    w0 = pl.multiple_of(jnp.maximum(q0 - WINDOW, 0), tq)
    stage_scores(lambda g: kw_ref[0, g, pl.ds(w0, n_win), :], n_win)
    head_keys = HEAD_BLOCKS * SEL_LEN
    stage_scores(lambda g: ks_ref[0, g, 0:head_keys, :], head_keys, n_win)

    dist_w = (q0 + lax.broadcasted_iota(jnp.int32, (n_win, tq), 1)) - (w0 + lax.broadcasted_iota(jnp.int32, (n_win, tq), 0))
    bias_w = jnp.where((dist_w >= 0) & (dist_w < WINDOW), 0.0, NEG_BIG)

    def win_pair(g, head0):
        for i in range(2):
            x = st_ref[head0 + i, 0:n_win, :] + bias_w
            p_ref[head0 // 2, 0:n_win, i * tq:(i + 1) * tq] = jnp.exp2(x - jnp.max(x, axis=0, keepdims=True)).astype(BF16)
        pv = _dot(vwt_ref[0, g, :, pl.ds(w0, n_win)], p_ref[head0 // 2, 0:n_win, :])
        finish_pair(head0, pv, 2, False)

    for g, head0 in pairs:
        win_pair(g, head0)

    blk = lax.broadcasted_iota(jnp.int32, (n_sel, tq), 0)
    cur = (q0 + lax.broadcasted_iota(jnp.int32, (n_sel, tq), 1)) // SEL_LEN
    valid = blk <= cur
    forced = valid & ((blk == 0) | (blk == cur) | (blk == cur - 1))
    blk_f = blk.astype(F32)
    score = []
    for g in range(N_GROUPS):
        total = None
        for head in range(g * HEADS_PER_GROUP, (g + 1) * HEADS_PER_GROUP):
            part = raw[head // 2][:, (head % 2) * tq:(head % 2 + 1) * tq] * inv_c[head]
            total = part if total is None else total + part
        score.append(jnp.where(forced, TOPK_TAKEN, jnp.where(valid, total, -TOPK_BIG)))

    for _ in range(min(N_SELECT, n_sel) - N_FORCED):
        for g in range(N_GROUPS):
            best = jnp.max(score[g], axis=0, keepdims=True)
            pick = jnp.min(jnp.where(score[g] == best, blk_f, float(n_sel)), axis=0, keepdims=True)
            score[g] = jnp.where(blk_f == pick, TOPK_TAKEN, score[g])
    sel = [jnp.where(valid & (score[g] == TOPK_TAKEN), 1.0, 0.0) for g in range(N_GROUPS)]
    for g in range(N_GROUPS):
        selt_ref[g] = jnp.where(blk >= HEAD_BLOCKS, sel[g], 0.0)
    sel_any = jnp.where(blk >= HEAD_BLOCKS, sel[0] + sel[1], 0.0)
    n_items = jnp.int32(0)
    for c in range(n_chunks):
        any_sel = jnp.max(sel_any[c * BLOCKS_PER_CHUNK:(c + 1) * BLOCKS_PER_CHUNK, :])
        list_ref[n_items] = c
        n_items = n_items + (any_sel > 0.5).astype(jnp.int32)

    causal_h = (q0 + lax.broadcasted_iota(jnp.int32, (head_keys, tq), 1)) >= lax.broadcasted_iota(jnp.int32, (head_keys, tq), 0)
    head_rows = slice(n_win, n_win + head_keys)
    for g, head0 in pairs:
        if head0 % HEADS_PER_GROUP == 0:
            sel_keys = jnp.concatenate([jnp.broadcast_to(sel[g][i:i + 1, :], (SEL_LEN, tq)) for i in range(HEAD_BLOCKS)], axis=0)
            bias = jnp.where(causal_h & (sel_keys > 0.5), 0.0, NEG_BIG)
        for i in range(2):
            x = st_ref[head0 + i, head_rows, :] + bias
            m_new = jnp.max(x, axis=0, keepdims=True)
            m_ref[head0 + i:head0 + i + 1, :] = m_new
            p_ref[head0 // 2, head_rows, i * tq:(i + 1) * tq] = jnp.exp2(x - m_new).astype(BF16)
    for g, head0 in pairs:
        acc_ref[head0 // 2] = _dot(vst_ref[0, g, :, 0:head_keys], p_ref[head0 // 2, head_rows, :])

    base = (lax.broadcasted_iota(jnp.int32, (SLC_CHUNK, tq), 1)
            - lax.broadcasted_iota(jnp.int32, (SLC_CHUNK, tq), 0))

    def stage_chunk(item, row0):
        c = list_ref[item]
        k0 = pl.multiple_of(c * SLC_CHUNK, SLC_CHUNK)
        causal = base >= k0 - q0
        for g, head0 in pairs:
            if head0 % HEADS_PER_GROUP == 0:
                sel_keys = jnp.concatenate(
                    [jnp.broadcast_to(selt_ref[g, pl.ds(c * BLOCKS_PER_CHUNK + i, 1), :], (SEL_LEN, tq))
                     for i in range(BLOCKS_PER_CHUNK)], axis=0)
                bias = jnp.where(causal & (sel_keys > 0.5), 0.0, NEG_BIG)
                bias2 = jnp.concatenate([bias, bias], axis=1)
            q_pair = q_ref[0, head0:head0 + 2].reshape(2 * tq, LANES)
            scores = _dot_nt(ks_ref[0, g, pl.ds(k0, SLC_CHUNK), :], q_pair) + bias2
            for i in range(2):
                st_ref[head0 + i, row0:row0 + SLC_CHUNK, :] = scores[:, i * tq:(i + 1) * tq]

    def consume_chunk(item, row0):
        k0 = pl.multiple_of(list_ref[item] * SLC_CHUNK, SLC_CHUNK)
        for g, head0 in pairs:
            alphas = []
            for i, head in enumerate((head0, head0 + 1)):
                x = st_ref[head, row0:row0 + SLC_CHUNK, :]
                m_old = m_ref[head:head + 1, :]
                m_new = jnp.maximum(m_old, jnp.max(x, axis=0, keepdims=True))
                alphas.append(jnp.exp2(m_old - m_new))
                m_ref[head:head + 1, :] = m_new
                p_ref[head0 // 2, 0:SLC_CHUNK, i * tq:(i + 1) * tq] = jnp.exp2(x - m_new).astype(BF16)
            pv = _dot(vst_ref[0, g, :, pl.ds(k0, SLC_CHUNK)], p_ref[head0 // 2, 0:SLC_CHUNK, :])
            acc_ref[head0 // 2] = acc_ref[head0 // 2] * jnp.concatenate(alphas, axis=1) + pv

    last = jnp.maximum(n_items - 1, 0)
    stage_chunk(0, 0)

    def two_chunks(j, carry):
        stage_chunk(jnp.minimum(2 * j + 1, last), SLC_CHUNK)
        consume_chunk(2 * j, 0)

        @pl.when(2 * j + 1 < n_items)
        def _():
            stage_chunk(jnp.minimum(2 * j + 2, last), 0)
            consume_chunk(2 * j + 1, SLC_CHUNK)
        return carry

    lax.fori_loop(0, (n_items + 1) // 2, two_chunks, 0)
    for g, head0 in pairs:
        finish_pair(head0, acc_ref[head0 // 2], 1, False)

    for hh in range(HEADS_PER_GROUP):
        o_ref[:, hh * LANES:(hh + 1) * LANES] = out_ref[hh].T.astype(BF16)


def _overlap_t(seq):
    n_cmp = (seq - CMP_LEN) // CMP_STRIDE + 1
    n_sel = seq // SEL_LEN
    cs = np.arange(n_cmp)[:, None] * CMP_STRIDE
    ss = np.arange(n_sel)[None, :] * SEL_LEN
    ov = np.clip(np.minimum(cs + CMP_LEN, ss + SEL_LEN) - np.maximum(cs, ss), 0, None) / CMP_LEN
    out = np.zeros((n_sel, seq // CMP_STRIDE), np.float32)
    out[:, :n_cmp] = ov.T
    return jnp.asarray(out, dtype=BF16)


def _nsa(q, kc, vct, ks, vst, kw, vwt, gates, *, tq=128):
    batch, _, seq, _ = q.shape
    n_cmp = kc.shape[2]
    n_sel = seq // SEL_LEN
    assert seq % SLC_CHUNK == 0 and seq >= WINDOW + tq and SLC_CHUNK % tq == 0 and tq == LANES
    assert n_sel <= 2 * SEL_LEN
    ovt = jnp.broadcast_to(_overlap_t(seq), (batch, N_GROUPS, n_sel, n_cmp))
    vct = jnp.concatenate([vct, ovt], axis=2)
    keys = lambda n: pl.BlockSpec((1, N_GROUPS, n, LANES), lambda b, i: (b, 0, 0, 0))
    vals = lambda n: pl.BlockSpec((1, N_GROUPS, V_ROWS, n), lambda b, i: (b, 0, 0, 0))
    tiles = seq // tq
    s_rows = max(n_cmp, WINDOW + tq + HEAD_BLOCKS * SEL_LEN, 2 * SLC_CHUNK)
    return pl.pallas_call(
        functools.partial(_nsa_kernel, tq=tq, seq=seq),
        out_shape=jax.ShapeDtypeStruct((batch * seq, HEADS_PER_GROUP * LANES), BF16),
        grid=(batch, tiles),
        in_specs=[
            pl.BlockSpec((1, N_HEADS, tq, LANES), lambda b, i: (b, 0, i, 0)),
            keys(n_cmp), pl.BlockSpec((1, N_GROUPS, V_ROWS + n_sel, n_cmp), lambda b, i: (b, 0, 0, 0)),
            keys(seq), vals(seq), keys(seq), vals(seq),
            pl.BlockSpec((tq, LANES), lambda b, i: (b * tiles + i, 0)),
        ],
        out_specs=pl.BlockSpec((tq, HEADS_PER_GROUP * LANES), lambda b, i: (b * tiles + i, 0)),
        scratch_shapes=[
            pltpu.VMEM((N_HEADS, s_rows, tq), F32),
            pltpu.VMEM((N_HEADS // 2, s_rows, 2 * tq), BF16),
            pltpu.VMEM((LANES, tq), F32),
            pltpu.VMEM((N_GROUPS, n_sel, tq), F32),
            pltpu.SMEM((seq // SLC_CHUNK,), jnp.int32),
            pltpu.VMEM((N_HEADS, tq), F32),
            pltpu.VMEM((N_HEADS // 2, V_ROWS, 2 * tq), F32),
            pltpu.VMEM((HEADS_PER_GROUP, N_GROUPS * HEAD_DIM, tq), F32),
        ],
        compiler_params=_params("parallel", "arbitrary"),
        name="nsa",
    )(q, kc, vct, ks, vst, kw, vwt, gates)


HALO = max(POOL_WINDOWS)


def _merge_kernel(x_ref, u_ref, halo_ref, gm_ref, on_ref, pw_ref, ps_ref, wbp_ref, wbn_ref, wo_ref,
                  o_ref, ext_ref, *, tm, seq):
    d = x_ref.shape[1]
    pos0 = (pl.program_id(0) * tm) % seq
    ext_ref[0:HALO, :] = jnp.where(pos0 == 0, 0.0, halo_ref[...])
    ext_ref[HALO:HALO + tm, :] = u_ref[...]
    pos = (pos0 + lax.broadcasted_iota(jnp.int32, (tm, POOL_GROUP_DIM), 0)).astype(F32)
    mixed = []
    for gi, w in enumerate(POOL_WINDOWS):
        cols = slice(gi * POOL_GROUP_DIM, (gi + 1) * POOL_GROUP_DIM)
        u = ext_ref[HALO:HALO + tm, cols]
        total = u
        for lag in range(1, w):
            total = total + ext_ref[HALO - lag:HALO - lag + tm, cols]
        delta = total / jnp.minimum(pos + 1.0, float(w)) - u
        mixed.append(_dot(delta.astype(BF16), pw_ref[gi]) * ps_ref[:, cols])
    mixed = jnp.concatenate(mixed, axis=-1).astype(BF16)
    a = _dot(mixed, wbp_ref[...])
    b = _dot(on_ref[...], wbn_ref[...])
    merged = gm_ref[:, 0:d] * a + gm_ref[:, d:2 * d] * b
    o_ref[...] = x_ref[...] + _dot(merged.astype(BF16), wo_ref[...])


def _merge(x, u, gm, o_nsa, pool_w, pool_scale, w_bp, w_bn, w_out, seq, *, tm=512):
    n, d = x.shape
    row = lambda i: (i, 0)
    full = lambda a: pl.BlockSpec(a.shape, lambda i: (0,) * a.ndim, pipeline_mode=pl.Buffered(1))
    halo_blocks = tm // HALO
    return pl.pallas_call(
        functools.partial(_merge_kernel, tm=tm, seq=seq),
        out_shape=jax.ShapeDtypeStruct((n, d), F32),
        grid=(n // tm,),
        in_specs=[
            pl.BlockSpec((tm, d), row),
            pl.BlockSpec((tm, POOL_COLS), row),
            pl.BlockSpec((HALO, POOL_COLS), lambda i: (jnp.maximum(i * halo_blocks - 1, 0), 0)),
            pl.BlockSpec((tm, 2 * d), row),
            pl.BlockSpec((tm, d), row),
            full(pool_w), full(pool_scale), full(w_bp), full(w_bn), full(w_out),
        ],
        out_specs=pl.BlockSpec((tm, d), row),
        scratch_shapes=[pltpu.VMEM((HALO + tm, POOL_COLS), F32)],
        compiler_params=_params("parallel"),
        name="merge",
    )(x, u, u, gm, o_nsa, pool_w, pool_scale, w_bp, w_bn, w_out)


def kernel(x, ffn1_norm, ffn1_w_gate, ffn1_w_up, ffn1_w_down, mix_norm, w_in, cmp_pos, cmp_k_w1, cmp_k_w2, cmp_v_w1, cmp_v_w2, pool_w, pool_scale, w_branch_pool, w_branch_nsa, w_out, ffn2_norm, ffn2_w_gate, ffn2_w_up, ffn2_w_down, final_norm):
    batch, seq, d = x.shape
    depth = w_in.shape[0]
    xf = x.reshape(batch * seq, d)
    bf = lambda a: a.astype(BF16)
    row = lambda a: a.reshape(1, -1)
    for l in range(depth):
        xf = _ffn(xf, row(ffn1_norm[l]), bf(ffn1_w_gate[l]), bf(ffn1_w_up[l]), bf(ffn1_w_down[l]),
                  row(final_norm), final_norm=False)

        q, kc, vc, ks, kw, vs, vw, gates, u, gm = _proj(xf, row(mix_norm[l]), _pack_w_in(w_in[l], d), batch, seq)
        k_cmp, v_cmp = _compress(kc, vc, cmp_pos[l], cmp_k_w1[l], cmp_k_w2[l], cmp_v_w1[l], cmp_v_w2[l], batch, seq)
        o_nsa = _nsa(q, k_cmp, v_cmp, ks, vs, kw, vw, gates)

        w_bn = w_branch_nsa[l].reshape(N_GROUPS, HEADS_PER_GROUP, HEAD_DIM, d).transpose(1, 0, 2, 3).reshape(-1, d)
        xf = _merge(xf, u, gm, o_nsa, bf(pool_w[l]), row(pool_scale[l]), bf(w_branch_pool[l]), bf(w_bn),
                    bf(w_out[l]), seq)

        xf = _ffn(xf, row(ffn2_norm[l]), bf(ffn2_w_gate[l]), bf(ffn2_w_up[l]), bf(ffn2_w_down[l]),
                  row(final_norm), final_norm=(l == depth - 1))
    return xf.reshape(batch, seq, d)
```

```python
import functools

import jax
import jax.numpy as jnp
import numpy as np
from jax import lax
from jax.experimental import pallas as pl
from jax.experimental.pallas import tpu as pltpu

N_HEADS = 16
N_GROUPS = 2
HEADS_PER_GROUP = N_HEADS // N_GROUPS
HEAD_DIM = 64
CMP_LEN = 32
CMP_STRIDE = 16
SEL_LEN = 64
N_SELECT = 16
WINDOW = 512
POOL_WINDOWS = (2, 4, 8, 16)
POOL_GROUP_DIM = 128
RMS_EPS = 1e-6
ALIBI_MAX_BIAS = 8.0

LANES = 128
NEG_BIG = -1e30
TOPK_BIG = 1e30
N_FORCED = 3
TOPK_TAKEN = -float(2 ** 101)
VMEM_LIMIT = 48 * 1024 * 1024

BF16 = jnp.bfloat16
F32 = jnp.float32


def _dot(a, b):
    return jnp.dot(a, b, preferred_element_type=F32)


def _dot_nt(a, b):
    return lax.dot_general(a, b, (((1,), (1,)), ((), ())), preferred_element_type=F32)


def _rms(x, g):
    return x * lax.rsqrt(jnp.mean(x * x, axis=-1, keepdims=True) + RMS_EPS) * g


def _params(*sem):
    return pltpu.CompilerParams(dimension_semantics=sem, vmem_limit_bytes=VMEM_LIMIT)


def _ffn_kernel(x_ref, g_ref, wg_ref, wu_ref, wd_ref, fin_ref, o_ref, *, final_norm):
    x = x_ref[...]
    xn = _rms(x, g_ref[...]).astype(BF16)
    gate = _dot(xn, wg_ref[...])
    up = _dot(xn, wu_ref[...])
    act = (gate * jax.nn.sigmoid(gate)) * up
    y = x + 0.5 * _dot(act.astype(BF16), wd_ref[...])
    if final_norm:
        y = _rms(y, fin_ref[...])
    o_ref[...] = y


def _ffn(x, norm_g, wg, wu, wd, fin_g, *, final_norm, tm=512):
    n, d = x.shape
    row = lambda i: (i, 0)
    resident = lambda a: pl.BlockSpec(a.shape, lambda i: (0, 0), pipeline_mode=pl.Buffered(1))
    return pl.pallas_call(
        functools.partial(_ffn_kernel, final_norm=final_norm),
        out_shape=jax.ShapeDtypeStruct((n, d), F32),
        grid=(n // tm,),
        in_specs=[pl.BlockSpec((tm, d), row), resident(norm_g), resident(wg), resident(wu), resident(wd),
                  resident(fin_g)],
        out_specs=pl.BlockSpec((tm, d), row),
        compiler_params=_params("parallel"),
        name="ffn",
    )(x, norm_g, wg, wu, wd, fin_g)


LOG2E = 1.4426950408889634
N_FEAT = 6


def _bf16_pieces(x):
    x = np.asarray(x, np.float32)
    s1 = x.astype(BF16).astype(np.float32)
    s2 = (x - s1).astype(BF16).astype(np.float32)
    s3 = (x - s1 - s2).astype(BF16).astype(np.float32)
    return s1, s2, s3


def _query_features():
    slopes = np.float32(2.0) ** (-ALIBI_MAX_BIAS * np.arange(1, N_HEADS + 1, dtype=np.float32) / N_HEADS)
    s1, s2, s3 = _bf16_pieces(slopes * np.float32(LOG2E))
    feat = np.zeros((N_HEADS, LANES), np.float32)
    feat[:, HEAD_DIM:HEAD_DIM + N_FEAT] = np.stack([s1, s2, s3, SEL_LEN * s1, SEL_LEN * s2, SEL_LEN * s3], axis=1)
    return jnp.asarray(feat)


def _key_features(pos, width, offset):
    pos = np.asarray(pos)
    a, b = (pos // SEL_LEN).astype(np.float32), (pos % SEL_LEN).astype(np.float32)
    feat = np.zeros((len(pos), width), np.float32)
    feat[:, offset:offset + N_FEAT] = np.stack([b, b, b, a, a, a], axis=1)
    return jnp.asarray(feat)


Q_COLS = N_HEADS * HEAD_DIM
CMP_OFF = Q_COLS
KEY_OFF = CMP_OFF + 2 * LANES
VAL_OFF = KEY_OFF + 2 * LANES
GATE_OFF = VAL_OFF + 2 * LANES
POOL_OFF = GATE_OFF + LANES
POOL_COLS = len(POOL_WINDOWS) * POOL_GROUP_DIM
MERGE_OFF = POOL_OFF + POOL_COLS


V_ROWS = HEAD_DIM + 16


def _store_values_t(v_ref, v):
    vt = v.T
    for g in range(N_GROUPS):
        v_ref[0, g, 0:HEAD_DIM, :] = vt[g * HEAD_DIM:(g + 1) * HEAD_DIM, :].astype(BF16)
        v_ref[0, g, HEAD_DIM:V_ROWS, :] = jnp.ones((V_ROWS - HEAD_DIM, v.shape[0]), BF16)


def _proj_kernel(x_ref, g_ref, w_ref, qf_ref, kf_ref, q_ref, kc_ref, vc_ref, ks_ref, kw_ref, vs_ref, vw_ref,
                 gate_ref, u_ref, gm_ref):
    d = x_ref.shape[1]
    hn = _rms(x_ref[...], g_ref[...]).astype(BF16)
    q_scale = HEAD_DIM ** -0.5 * LOG2E
    low = lax.broadcasted_iota(jnp.int32, (x_ref.shape[0], LANES), 1) < HEAD_DIM
    for j in range(N_HEADS // 4):
        q4 = _dot(hn, w_ref[:, j * 2 * LANES:(j + 1) * 2 * LANES]) * q_scale
        for b in range(2):
            both = q4[:, b * LANES:(b + 1) * LANES]
            h = 4 * j + 2 * b
            q_ref[0, h] = (jnp.where(low, both, 0.0) + qf_ref[h:h + 1, :]).astype(BF16)
            q_ref[0, h + 1] = (jnp.where(low, pltpu.roll(both, HEAD_DIM, 1), 0.0) + qf_ref[h + 1:h + 2, :]).astype(BF16)
    cmp_in = _dot(hn, w_ref[:, CMP_OFF:CMP_OFF + 2 * LANES])
    kc_ref[...] = cmp_in[:, 0:LANES]
    vc_ref[...] = cmp_in[:, LANES:2 * LANES]
    keys = _dot(hn, w_ref[:, KEY_OFF:KEY_OFF + 2 * LANES])
    kf = kf_ref[...]
    for i, k_ref in enumerate((ks_ref, kw_ref)):
        both = keys[:, i * LANES:(i + 1) * LANES]
        k_ref[0, 0] = (jnp.where(low, both, 0.0) + kf).astype(BF16)
        k_ref[0, 1] = (jnp.where(low, pltpu.roll(both, HEAD_DIM, 1), 0.0) + kf).astype(BF16)
    vals = _dot(hn, w_ref[:, VAL_OFF:VAL_OFF + 2 * LANES])
    for i, v_ref in enumerate((vs_ref, vw_ref)):
        _store_values_t(v_ref, vals[:, i * LANES:(i + 1) * LANES])
    gate_ref[...] = jax.nn.sigmoid(_dot(hn, w_ref[:, GATE_OFF:GATE_OFF + LANES]))
    u_ref[...] = _dot(hn, w_ref[:, POOL_OFF:POOL_OFF + POOL_COLS])
    gm_ref[...] = jax.nn.sigmoid(_dot(hn, w_ref[:, MERGE_OFF:MERGE_OFF + 2 * d]))


def _pack_w_in(w_in, d):
    w_in = w_in.astype(BF16)
    qw = N_HEADS * HEAD_DIM
    kvw = N_GROUPS * HEAD_DIM

    kc, vc, ks, vs, kw, vw = (w_in[:, qw + i * kvw:qw + (i + 1) * kvw] for i in range(6))
    off = qw + 6 * kvw
    n_g = 3 * N_HEADS
    w_g = jnp.pad(w_in[:, off:off + n_g], ((0, 0), (0, LANES - n_g)))
    w_rest = w_in[:, off + n_g:]
    return jnp.concatenate([w_in[:, :qw], kc, vc, ks, kw,
                            vs, vw, w_g, w_rest], axis=1)


def _proj(x, norm_g, w_packed, batch, seq, *, tm=512):
    n, d = x.shape
    tiles_per_seq = seq // tm
    row = lambda i: (i, 0)
    const = lambda i: (0, 0)
    per_group = lambda i: (i // tiles_per_seq, 0, i % tiles_per_seq, 0)
    qf = _query_features()
    kf = _key_features(np.arange(seq), LANES, HEAD_DIM)
    flat = lambda width, dtype: (jax.ShapeDtypeStruct((n, width), dtype), pl.BlockSpec((tm, width), row))
    grouped = lambda count: (jax.ShapeDtypeStruct((batch, count, seq, LANES), BF16),
                             pl.BlockSpec((1, count, tm, LANES), per_group))
    values_t = (jax.ShapeDtypeStruct((batch, N_GROUPS, V_ROWS, seq), BF16),
                pl.BlockSpec((1, N_GROUPS, V_ROWS, tm), lambda i: (i // tiles_per_seq, 0, 0, i % tiles_per_seq)))
    outs = [grouped(N_HEADS), flat(LANES, F32), flat(LANES, F32), grouped(N_GROUPS), grouped(N_GROUPS),
            values_t, values_t, flat(LANES, F32), flat(POOL_COLS, F32), flat(2 * d, F32)]
    return pl.pallas_call(
        _proj_kernel,
        out_shape=[o[0] for o in outs],
        grid=(n // tm,),
        in_specs=[pl.BlockSpec((tm, d), row), pl.BlockSpec((1, d), const),
                  pl.BlockSpec(w_packed.shape, const, pipeline_mode=pl.Buffered(1)),
                  pl.BlockSpec(qf.shape, const), pl.BlockSpec((tm, LANES), lambda i: (i % tiles_per_seq, 0))],
        out_specs=[o[1] for o in outs],
        compiler_params=_params("parallel"),
        name="proj",
    )(x, norm_g, w_packed, qf, kf)


def _compress_kernel(k_ref, v_ref, pos_ref, wk1_ref, wk2_ref, wv1_ref, wv2_ref, feat_ref, ko_ref, vo_ref):
    n_chunk = k_ref.shape[0] // CMP_STRIDE
    hidden = wk2_ref.shape[0]

    def compress(x_ref, w1_ref, w2_ref):
        first = second = None
        for r in range(CMP_STRIDE):
            rows = x_ref[pl.ds(r, n_chunk, stride=CMP_STRIDE), :]
            a = _dot((rows + pos_ref[r:r + 1, :]).astype(BF16), w1_ref[r])
            b = _dot((rows + pos_ref[CMP_STRIDE + r:CMP_STRIDE + r + 1, :]).astype(BF16), w1_ref[CMP_STRIDE + r])
            first = a if first is None else first + a
            second = b if second is None else second + b
        act = jax.nn.gelu(first + pltpu.roll(second, n_chunk - 1, 0)).astype(BF16)
        return [_dot(act[:, g * hidden:(g + 1) * hidden], w2_ref[...]) for g in range(N_GROUPS)]

    for g, k_cmp in enumerate(compress(k_ref, wk1_ref, wk2_ref)):
        ko_ref[0, g] = jnp.concatenate([k_cmp, feat_ref[...]], axis=-1).astype(BF16)
    _store_values_t(vo_ref, jnp.concatenate(compress(v_ref, wv1_ref, wv2_ref), axis=-1))


def _block_diag_w1(w1):
    w = w1.reshape(CMP_LEN, HEAD_DIM, -1)
    z = jnp.zeros_like(w)
    return jnp.concatenate([jnp.concatenate([w, z], axis=2), jnp.concatenate([z, w], axis=2)], axis=1).astype(BF16)


def _compress(kc, vc, pos, wk1, wk2, wv1, wv2, batch, seq):
    n_chunk = seq // CMP_STRIDE
    rows = pl.BlockSpec((seq, LANES), lambda b: (b, 0))
    full = lambda a: pl.BlockSpec(a.shape, lambda b: (0,) * a.ndim, pipeline_mode=pl.Buffered(1))
    pos2 = jnp.concatenate([pos] * N_GROUPS, axis=1)
    feat = _key_features(np.arange(n_chunk) * CMP_STRIDE + CMP_LEN - 1, HEAD_DIM, 0)
    args = (pos2, _block_diag_w1(wk1), wk2.astype(BF16), _block_diag_w1(wv1), wv2.astype(BF16), feat)
    return pl.pallas_call(
        _compress_kernel,
        out_shape=[jax.ShapeDtypeStruct((batch, N_GROUPS, n_chunk, LANES), BF16),
                   jax.ShapeDtypeStruct((batch, N_GROUPS, V_ROWS, n_chunk), BF16)],
        grid=(batch,),
        in_specs=[rows, rows] + [full(a) for a in args],
        out_specs=[pl.BlockSpec((1, N_GROUPS, n_chunk, LANES), lambda b: (b, 0, 0, 0)),
                   pl.BlockSpec((1, N_GROUPS, V_ROWS, n_chunk), lambda b: (b, 0, 0, 0))],
        compiler_params=_params("parallel"),
        name="compress",
    )(kc, vc, *args)


SLC_CHUNK = 256
BLOCKS_PER_CHUNK = SLC_CHUNK // SEL_LEN
HEAD_BLOCKS = 2


def _nsa_kernel(q_ref, kc_ref, vct_ref, ks_ref, vst_ref, kw_ref, vwt_ref, gate_ref, o_ref,
                st_ref, p_ref, gt_ref, selt_ref, list_ref, m_ref, acc_ref, out_ref, *, tq, seq):
    n_sel = seq // SEL_LEN
    n_chunks = seq // SLC_CHUNK
    n_cmp = kc_ref.shape[2]
    n_win = WINDOW + tq
    q0 = pl.program_id(1) * tq
    gt_ref[...] = gate_ref[...].T
    pairs = [(g, g * HEADS_PER_GROUP + 2 * j) for g in range(N_GROUPS) for j in range(HEADS_PER_GROUP // 2)]

    def gate_row(head, branch):
        c = 3 * head + branch
        return gt_ref[c:c + 1, :]

    def out_rows(head):
        g, hh = divmod(head, HEADS_PER_GROUP)
        return hh, slice(g * HEAD_DIM, (g + 1) * HEAD_DIM)

    def stage_scores(keys, n_keys, row0=0):
        for g, head0 in pairs:
            q_pair = q_ref[0, head0:head0 + 2].reshape(2 * tq, LANES)
            scores = _dot_nt(keys(g), q_pair)
            for i in range(2):
                st_ref[head0 + i, row0:row0 + n_keys, :] = scores[:, i * tq:(i + 1) * tq]

    def finish_pair(head0, pv, branch, first, ok=None):
        invs = []
        for i in range(2):
            cols = slice(i * tq, (i + 1) * tq)
            inv = 1.0 / pv[HEAD_DIM:HEAD_DIM + 1, cols]
            if ok is not None:
                inv = jnp.where(ok, inv, 0.0)
            hh, rows = out_rows(head0 + i)
            contrib = pv[0:HEAD_DIM, cols] * (inv * gate_row(head0 + i, branch))
            out_ref[hh, rows, :] = contrib if first else out_ref[hh, rows, :] + contrib
            invs.append(inv)
        return invs

    t_c = q0 + lax.broadcasted_iota(jnp.int32, (n_cmp, tq), 1)
    end_c = lax.broadcasted_iota(jnp.int32, (n_cmp, tq), 0) * CMP_STRIDE + (CMP_LEN - 1)
    bias_c = jnp.where(t_c >= end_c, 0.0, NEG_BIG)
    has_cmp = (q0 + lax.broadcasted_iota(jnp.int32, (1, tq), 1)) >= CMP_LEN - 1
    inv_c = [None] * N_HEADS
    raw = [None] * (N_HEADS // 2)

    def cmp_pair(g, head0):
        for i in range(2):
            x = st_ref[head0 + i, 0:n_cmp, :] + bias_c
            p_ref[head0 // 2, 0:n_cmp, i * tq:(i + 1) * tq] = jnp.exp2(x - jnp.max(x, axis=0, keepdims=True)).astype(BF16)
        both = _dot(vct_ref[0, g], p_ref[head0 // 2, 0:n_cmp, :])
        raw[head0 // 2] = both[V_ROWS:, :]
        inv_c[head0], inv_c[head0 + 1] = finish_pair(head0, both[0:V_ROWS, :], 0, True, has_cmp)

    stage_scores(lambda g: kc_ref[0, g], n_cmp)
    for g, head0 in pairs:
        cmp_pair(g, head0)

    w0 = pl.multiple_of(jnp.maximum(q0 - WINDOW, 0), tq)
    stage_scores(lambda g: kw_ref[0, g, pl.ds(w0, n_win), :], n_win)
    head_keys = HEAD_BLOCKS * SEL_LEN
    stage_scores(lambda g: ks_ref[0, g, 0:head_keys, :], head_keys, n_win)

    dist_w = (q0 + lax.broadcasted_iota(jnp.int32, (n_win, tq), 1)) - (w0 + lax.broadcasted_iota(jnp.int32, (n_win, tq), 0))
    bias_w = jnp.where((dist_w >= 0) & (dist_w < WINDOW), 0.0, NEG_BIG)

    def win_pair(g, head0):
        for i in range(2):
            x = st_ref[head0 + i, 0:n_win, :] + bias_w
            p_ref[head0 // 2, 0:n_win, i * tq:(i + 1) * tq] = jnp.exp2(x - jnp.max(x, axis=0, keepdims=True)).astype(BF16)
        pv = _dot(vwt_ref[0, g, :, pl.ds(w0, n_win)], p_ref[head0 // 2, 0:n_win, :])
        finish_pair(head0, pv, 2, False)

    for g, head0 in pairs:
        win_pair(g, head0)

    blk = lax.broadcasted_iota(jnp.int32, (n_sel, tq), 0)
    cur = (q0 + lax.broadcasted_iota(jnp.int32, (n_sel, tq), 1)) // SEL_LEN
    valid = blk <= cur
    forced = valid & ((blk == 0) | (blk == cur) | (blk == cur - 1))
    blk_f = blk.astype(F32)
    score = []
    for g in range(N_GROUPS):
        total = None
        for head in range(g * HEADS_PER_GROUP, (g + 1) * HEADS_PER_GROUP):
            part = raw[head // 2][:, (head % 2) * tq:(head % 2 + 1) * tq] * inv_c[head]
            total = part if total is None else total + part
        score.append(jnp.where(forced, TOPK_TAKEN, jnp.where(valid, total, -TOPK_BIG)))

    for _ in range(min(N_SELECT, n_sel) - N_FORCED):
        for g in range(N_GROUPS):
            best = jnp.max(score[g], axis=0, keepdims=True)
            pick = jnp.min(jnp.where(score[g] == best, blk_f, float(n_sel)), axis=0, keepdims=True)
            score[g] = jnp.where(blk_f == pick, TOPK_TAKEN, score[g])
    sel = [jnp.where(valid & (score[g] == TOPK_TAKEN), 1.0, 0.0) for g in range(N_GROUPS)]
    for g in range(N_GROUPS):
        selt_ref[g] = jnp.where(blk >= HEAD_BLOCKS, sel[g], 0.0)
    sel_any = jnp.where(blk >= HEAD_BLOCKS, sel[0] + sel[1], 0.0)
    n_items = jnp.int32(0)
    for c in range(n_chunks):
        any_sel = jnp.max(sel_any[c * BLOCKS_PER_CHUNK:(c + 1) * BLOCKS_PER_CHUNK, :])
        list_ref[n_items] = c
        n_items = n_items + (any_sel > 0.5).astype(jnp.int32)

    causal_h = (q0 + lax.broadcasted_iota(jnp.int32, (head_keys, tq), 1)) >= lax.broadcasted_iota(jnp.int32, (head_keys, tq), 0)
    head_rows = slice(n_win, n_win + head_keys)
    for g, head0 in pairs:
        if head0 % HEADS_PER_GROUP == 0:
            sel_keys = jnp.concatenate([jnp.broadcast_to(sel[g][i:i + 1, :], (SEL_LEN, tq)) for i in range(HEAD_BLOCKS)], axis=0)
            bias = jnp.where(causal_h & (sel_keys > 0.5), 0.0, NEG_BIG)
        for i in range(2):
            x = st_ref[head0 + i, head_rows, :] + bias
            m_new = jnp.max(x, axis=0, keepdims=True)
            m_ref[head0 + i:head0 + i + 1, :] = m_new
            p_ref[head0 // 2, head_rows, i * tq:(i + 1) * tq] = jnp.exp2(x - m_new).astype(BF16)
    for g, head0 in pairs:
        acc_ref[head0 // 2] = _dot(vst_ref[0, g, :, 0:head_keys], p_ref[head0 // 2, head_rows, :])

    base = (lax.broadcasted_iota(jnp.int32, (SLC_CHUNK, tq), 1)
            - lax.broadcasted_iota(jnp.int32, (SLC_CHUNK, tq), 0))

    def stage_chunk(item, row0):
        c = list_ref[item]
        k0 = pl.multiple_of(c * SLC_CHUNK, SLC_CHUNK)
        causal = base >= k0 - q0
        for g, head0 in pairs:
            if head0 % HEADS_PER_GROUP == 0:
                sel_keys = jnp.concatenate(
                    [jnp.broadcast_to(selt_ref[g, pl.ds(c * BLOCKS_PER_CHUNK + i, 1), :], (SEL_LEN, tq))
                     for i in range(BLOCKS_PER_CHUNK)], axis=0)
                bias = jnp.where(causal & (sel_keys > 0.5), 0.0, NEG_BIG)
                bias2 = jnp.concatenate([bias, bias], axis=1)
            q_pair = q_ref[0, head0:head0 + 2].reshape(2 * tq, LANES)
            scores = _dot_nt(ks_ref[0, g, pl.ds(k0, SLC_CHUNK), :], q_pair) + bias2
            for i in range(2):
                st_ref[head0 + i, row0:row0 + SLC_CHUNK, :] = scores[:, i * tq:(i + 1) * tq]

    def consume_chunk(item, row0):
        k0 = pl.multiple_of(list_ref[item] * SLC_CHUNK, SLC_CHUNK)
        for g, head0 in pairs:
            alphas = []
            for i, head in enumerate((head0, head0 + 1)):
                x = st_ref[head, row0:row0 + SLC_CHUNK, :]
                m_old = m_ref[head:head + 1, :]
                m_new = jnp.maximum(m_old, jnp.max(x, axis=0, keepdims=True))
                alphas.append(jnp.exp2(m_old - m_new))
                m_ref[head:head + 1, :] = m_new
                p_ref[head0 // 2, 0:SLC_CHUNK, i * tq:(i + 1) * tq] = jnp.exp2(x - m_new).astype(BF16)
            pv = _dot(vst_ref[0, g, :, pl.ds(k0, SLC_CHUNK)], p_ref[head0 // 2, 0:SLC_CHUNK, :])
            acc_ref[head0 // 2] = acc_ref[head0 // 2] * jnp.concatenate(alphas, axis=1) + pv

    last = jnp.maximum(n_items - 1, 0)
    stage_chunk(0, 0)

    def two_chunks(j, carry):
        stage_chunk(jnp.minimum(2 * j + 1, last), SLC_CHUNK)
        consume_chunk(2 * j, 0)

        @pl.when(2 * j + 1 < n_items)
        def _():
            stage_chunk(jnp.minimum(2 * j + 2, last), 0)
            consume_chunk(2 * j + 1, SLC_CHUNK)
        return carry

    lax.fori_loop(0, (n_items + 1) // 2, two_chunks, 0)
    for g, head0 in pairs:
        finish_pair(head0, acc_ref[head0 // 2], 1, False)

    for hh in range(HEADS_PER_GROUP):
        o_ref[:, hh * LANES:(hh + 1) * LANES] = out_ref[hh].T.astype(BF16)


def _overlap_t(seq):
    n_cmp = (seq - CMP_LEN) // CMP_STRIDE + 1
    n_sel = seq // SEL_LEN
    cs = np.arange(n_cmp)[:, None] * CMP_STRIDE
    ss = np.arange(n_sel)[None, :] * SEL_LEN
    ov = np.clip(np.minimum(cs + CMP_LEN, ss + SEL_LEN) - np.maximum(cs, ss), 0, None) / CMP_LEN
    out = np.zeros((n_sel, seq // CMP_STRIDE), np.float32)
    out[:, :n_cmp] = ov.T
    return jnp.asarray(out, dtype=BF16)


def _nsa(q, kc, vct, ks, vst, kw, vwt, gates, *, tq=128):
    batch, _, seq, _ = q.shape
    n_cmp = kc.shape[2]
    n_sel = seq // SEL_LEN
    assert seq % SLC_CHUNK == 0 and seq >= WINDOW + tq and SLC_CHUNK % tq == 0 and tq == LANES
    assert n_sel <= 2 * SEL_LEN
    ovt = jnp.broadcast_to(_overlap_t(seq), (batch, N_GROUPS, n_sel, n_cmp))
    vct = jnp.concatenate([vct, ovt], axis=2)
    keys = lambda n: pl.BlockSpec((1, N_GROUPS, n, LANES), lambda b, i: (b, 0, 0, 0))
    vals = lambda n: pl.BlockSpec((1, N_GROUPS, V_ROWS, n), lambda b, i: (b, 0, 0, 0))
    tiles = seq // tq
    s_rows = max(n_cmp, WINDOW + tq + HEAD_BLOCKS * SEL_LEN, 2 * SLC_CHUNK)
    return pl.pallas_call(
        functools.partial(_nsa_kernel, tq=tq, seq=seq),
        out_shape=jax.ShapeDtypeStruct((batch * seq, HEADS_PER_GROUP * LANES), BF16),
        grid=(batch, tiles),
        in_specs=[
            pl.BlockSpec((1, N_HEADS, tq, LANES), lambda b, i: (b, 0, i, 0)),
            keys(n_cmp), pl.BlockSpec((1, N_GROUPS, V_ROWS + n_sel, n_cmp), lambda b, i: (b, 0, 0, 0)),
            keys(seq), vals(seq), keys(seq), vals(seq),
            pl.BlockSpec((tq, LANES), lambda b, i: (b * tiles + i, 0)),
        ],
        out_specs=pl.BlockSpec((tq, HEADS_PER_GROUP * LANES), lambda b, i: (b * tiles + i, 0)),
        scratch_shapes=[
            pltpu.VMEM((N_HEADS, s_rows, tq), F32),
            pltpu.VMEM((N_HEADS // 2, s_rows, 2 * tq), BF16),
            pltpu.VMEM((LANES, tq), F32),
            pltpu.VMEM((N_GROUPS, n_sel, tq), F32),
            pltpu.SMEM((seq // SLC_CHUNK,), jnp.int32),
            pltpu.VMEM((N_HEADS, tq), F32),
            pltpu.VMEM((N_HEADS // 2, V_ROWS, 2 * tq), F32),
            pltpu.VMEM((HEADS_PER_GROUP, N_GROUPS * HEAD_DIM, tq), F32),
        ],
        compiler_params=_params("parallel", "arbitrary"),
        name="nsa",
    )(q, kc, vct, ks, vst, kw, vwt, gates)


HALO = max(POOL_WINDOWS)
assert all(w & (w - 1) == 0 for w in POOL_WINDOWS)


def _merge_kernel(x_ref, u_ref, halo_ref, gm_ref, on_ref, pw_ref, ps_ref, wbp_ref, wbn_ref, wo_ref,
                  o_ref, ext_ref, *, tm, seq):
    d = x_ref.shape[1]
    b = _dot(on_ref[...], wbn_ref[...])
    pos0 = (pl.program_id(0) * tm) % seq
    ext_ref[0:HALO, :] = jnp.where(pos0 == 0, 0.0, halo_ref[...])
    ext_ref[HALO:HALO + tm, :] = u_ref[...]
    pos = (pos0 + lax.broadcasted_iota(jnp.int32, (tm, POOL_GROUP_DIM), 0)).astype(F32)
    mixed = []
    for gi, w in enumerate(POOL_WINDOWS):
        cols = slice(gi * POOL_GROUP_DIM, (gi + 1) * POOL_GROUP_DIM)
        run = ext_ref[:, cols]
        for k in range(w.bit_length() - 1):
            run = run + pltpu.roll(run, 1 << k, 0)
        u = ext_ref[HALO:HALO + tm, cols]
        delta = run[HALO:HALO + tm] / jnp.minimum(pos + 1.0, float(w)) - u
        mixed.append(_dot(delta.astype(BF16), pw_ref[gi]) * ps_ref[:, cols])
    mixed = jnp.concatenate(mixed, axis=-1).astype(BF16)
    a = _dot(mixed, wbp_ref[...])
    merged = gm_ref[:, 0:d] * a + gm_ref[:, d:2 * d] * b
    o_ref[...] = x_ref[...] + _dot(merged.astype(BF16), wo_ref[...])


def _merge(x, u, gm, o_nsa, pool_w, pool_scale, w_bp, w_bn, w_out, seq, *, tm=512):
    n, d = x.shape
    row = lambda i: (i, 0)
    full = lambda a: pl.BlockSpec(a.shape, lambda i: (0,) * a.ndim, pipeline_mode=pl.Buffered(1))
    halo_blocks = tm // HALO
    return pl.pallas_call(
        functools.partial(_merge_kernel, tm=tm, seq=seq),
        out_shape=jax.ShapeDtypeStruct((n, d), F32),
        grid=(n // tm,),
        in_specs=[
            pl.BlockSpec((tm, d), row),
            pl.BlockSpec((tm, POOL_COLS), row),
            pl.BlockSpec((HALO, POOL_COLS), lambda i: (jnp.maximum(i * halo_blocks - 1, 0), 0)),
            pl.BlockSpec((tm, 2 * d), row),
            pl.BlockSpec((tm, d), row),
            full(pool_w), full(pool_scale), full(w_bp), full(w_bn), full(w_out),
        ],
        out_specs=pl.BlockSpec((tm, d), row),
        scratch_shapes=[pltpu.VMEM((HALO + tm, POOL_COLS), F32)],
        compiler_params=_params("parallel"),
        name="merge",
    )(x, u, u, gm, o_nsa, pool_w, pool_scale, w_bp, w_bn, w_out)


def kernel(x, ffn1_norm, ffn1_w_gate, ffn1_w_up, ffn1_w_down, mix_norm, w_in, cmp_pos, cmp_k_w1, cmp_k_w2, cmp_v_w1, cmp_v_w2, pool_w, pool_scale, w_branch_pool, w_branch_nsa, w_out, ffn2_norm, ffn2_w_gate, ffn2_w_up, ffn2_w_down, final_norm):
    batch, seq, d = x.shape
    depth = w_in.shape[0]
    xf = x.reshape(batch * seq, d)
    bf = lambda a: a.astype(BF16)
    row = lambda a: a.reshape(1, -1)
    for l in range(depth):
        xf = _ffn(xf, row(ffn1_norm[l]), bf(ffn1_w_gate[l]), bf(ffn1_w_up[l]), bf(ffn1_w_down[l]),
                  row(final_norm), final_norm=False)

        q, kc, vc, ks, kw, vs, vw, gates, u, gm = _proj(xf, row(mix_norm[l]), _pack_w_in(w_in[l], d), batch, seq)
        k_cmp, v_cmp = _compress(kc, vc, cmp_pos[l], cmp_k_w1[l], cmp_k_w2[l], cmp_v_w1[l], cmp_v_w2[l], batch, seq)
        o_nsa = _nsa(q, k_cmp, v_cmp, ks, vs, kw, vw, gates)

        w_bn = w_branch_nsa[l].reshape(N_GROUPS, HEADS_PER_GROUP, HEAD_DIM, d).transpose(1, 0, 2, 3).reshape(-1, d)
        xf = _merge(xf, u, gm, o_nsa, bf(pool_w[l]), row(pool_scale[l]), bf(w_branch_pool[l]), bf(w_bn),
                    bf(w_out[l]), seq)

        xf = _ffn(xf, row(ffn2_norm[l]), bf(ffn2_w_gate[l]), bf(ffn2_w_up[l]), bf(ffn2_w_down[l]),
                  row(final_norm), final_norm=(l == depth - 1))
    return xf.reshape(batch, seq, d)
```

```python
import functools

import jax
import jax.numpy as jnp
import numpy as np
from jax import lax
from jax.experimental import pallas as pl
from jax.experimental.pallas import tpu as pltpu

N_HEADS = 16
N_GROUPS = 2
HEADS_PER_GROUP = N_HEADS // N_GROUPS
HEAD_DIM = 64
CMP_LEN = 32
CMP_STRIDE = 16
SEL_LEN = 64
N_SELECT = 16
WINDOW = 512
POOL_WINDOWS = (2, 4, 8, 16)
POOL_GROUP_DIM = 128
RMS_EPS = 1e-6
ALIBI_MAX_BIAS = 8.0

LANES = 128
NEG_BIG = -1e30
TOPK_BIG = 1e30
N_FORCED = 3
TOPK_TAKEN = -float(2 ** 101)
VMEM_LIMIT = 48 * 1024 * 1024

BF16 = jnp.bfloat16
F32 = jnp.float32


def _dot(a, b):
    return jnp.dot(a, b, preferred_element_type=F32)


def _dot_nt(a, b):
    return lax.dot_general(a, b, (((1,), (1,)), ((), ())), preferred_element_type=F32)


def _rms(x, g):
    return x * lax.rsqrt(jnp.mean(x * x, axis=-1, keepdims=True) + RMS_EPS) * g


def _params(*sem):
    return pltpu.CompilerParams(dimension_semantics=sem, vmem_limit_bytes=VMEM_LIMIT)


def _ffn_kernel(x_ref, g_ref, wg_ref, wu_ref, wd_ref, fin_ref, o_ref, *, final_norm):
    half = x_ref.shape[0] // 2
    for r in (slice(0, half), slice(half, 2 * half)):
        x = x_ref[r, :]
        xn = _rms(x, g_ref[...]).astype(BF16)
        gate = _dot(xn, wg_ref[...])
        up = _dot(xn, wu_ref[...])
        act = (gate * jax.nn.sigmoid(gate)) * up
        y = x + 0.5 * _dot(act.astype(BF16), wd_ref[...])
        if final_norm:
            y = _rms(y, fin_ref[...])
        o_ref[r, :] = y


def _ffn(x, norm_g, wg, wu, wd, fin_g, *, final_norm, tm=1024):
    n, d = x.shape
    row = lambda i: (i, 0)
    resident = lambda a: pl.BlockSpec(a.shape, lambda i: (0, 0), pipeline_mode=pl.Buffered(1))
    return pl.pallas_call(
        functools.partial(_ffn_kernel, final_norm=final_norm),
        out_shape=jax.ShapeDtypeStruct((n, d), F32),
        grid=(n // tm,),
        in_specs=[pl.BlockSpec((tm, d), row), resident(norm_g), resident(wg), resident(wu), resident(wd),
                  resident(fin_g)],
        out_specs=pl.BlockSpec((tm, d), row),
        compiler_params=_params("parallel"),
        name="ffn",
    )(x, norm_g, wg, wu, wd, fin_g)


LOG2E = 1.4426950408889634
N_FEAT = 6


def _bf16_pieces(x):
    x = np.asarray(x, np.float32)
    s1 = x.astype(BF16).astype(np.float32)
    s2 = (x - s1).astype(BF16).astype(np.float32)
    s3 = (x - s1 - s2).astype(BF16).astype(np.float32)
    return s1, s2, s3


def _query_features():
    slopes = np.float32(2.0) ** (-ALIBI_MAX_BIAS * np.arange(1, N_HEADS + 1, dtype=np.float32) / N_HEADS)
    s1, s2, s3 = _bf16_pieces(slopes * np.float32(LOG2E))
    feat = np.zeros((N_HEADS, LANES), np.float32)
    feat[:, HEAD_DIM:HEAD_DIM + N_FEAT] = np.stack([s1, s2, s3, SEL_LEN * s1, SEL_LEN * s2, SEL_LEN * s3], axis=1)
    return jnp.asarray(feat)


def _key_features(pos, width, offset):
    pos = np.asarray(pos)
    a, b = (pos // SEL_LEN).astype(np.float32), (pos % SEL_LEN).astype(np.float32)
    feat = np.zeros((len(pos), width), np.float32)
    feat[:, offset:offset + N_FEAT] = np.stack([b, b, b, a, a, a], axis=1)
    return jnp.asarray(feat)


Q_COLS = N_HEADS * HEAD_DIM
CMP_OFF = Q_COLS
KEY_OFF = CMP_OFF + 2 * LANES
VAL_OFF = KEY_OFF + 2 * LANES
GATE_OFF = VAL_OFF + 2 * LANES
POOL_OFF = GATE_OFF + LANES
POOL_COLS = len(POOL_WINDOWS) * POOL_GROUP_DIM
MERGE_OFF = POOL_OFF + POOL_COLS


V_ROWS = HEAD_DIM + 16


def _store_values_t(v_ref, v):
    vt = v.T
    for g in range(N_GROUPS):
        v_ref[0, g, 0:HEAD_DIM, :] = vt[g * HEAD_DIM:(g + 1) * HEAD_DIM, :].astype(BF16)
        v_ref[0, g, HEAD_DIM:V_ROWS, :] = jnp.ones((V_ROWS - HEAD_DIM, v.shape[0]), BF16)


def _proj_kernel(x_ref, g_ref, w_ref, qf_ref, kf_ref, q_ref, kc_ref, vc_ref, ks_ref, kw_ref, vs_ref, vw_ref,
                 gate_ref, u_ref, gm_ref):
    d = x_ref.shape[1]
    hn = _rms(x_ref[...], g_ref[...]).astype(BF16)
    q_scale = HEAD_DIM ** -0.5 * LOG2E
    low = lax.broadcasted_iota(jnp.int32, (x_ref.shape[0], LANES), 1) < HEAD_DIM
    for j in range(N_HEADS // 4):
        q4 = _dot(hn, w_ref[:, j * 2 * LANES:(j + 1) * 2 * LANES]) * q_scale
        for b in range(2):
            both = q4[:, b * LANES:(b + 1) * LANES]
            h = 4 * j + 2 * b
            q_ref[0, h] = (jnp.where(low, both, 0.0) + qf_ref[h:h + 1, :]).astype(BF16)
            q_ref[0, h + 1] = (jnp.where(low, pltpu.roll(both, HEAD_DIM, 1), 0.0) + qf_ref[h + 1:h + 2, :]).astype(BF16)
    cmp_in = _dot(hn, w_ref[:, CMP_OFF:CMP_OFF + 2 * LANES])
    kc_ref[...] = cmp_in[:, 0:LANES]
    vc_ref[...] = cmp_in[:, LANES:2 * LANES]
    keys = _dot(hn, w_ref[:, KEY_OFF:KEY_OFF + 2 * LANES])
    kf = kf_ref[...]
    for i, k_ref in enumerate((ks_ref, kw_ref)):
        both = keys[:, i * LANES:(i + 1) * LANES]
        k_ref[0, 0] = (jnp.where(low, both, 0.0) + kf).astype(BF16)
        k_ref[0, 1] = (jnp.where(low, pltpu.roll(both, HEAD_DIM, 1), 0.0) + kf).astype(BF16)
    vals = _dot(hn, w_ref[:, VAL_OFF:VAL_OFF + 2 * LANES])
    for i, v_ref in enumerate((vs_ref, vw_ref)):
        _store_values_t(v_ref, vals[:, i * LANES:(i + 1) * LANES])
    gate_ref[...] = jax.nn.sigmoid(_dot(hn, w_ref[:, GATE_OFF:GATE_OFF + LANES]))
    u_ref[...] = _dot(hn, w_ref[:, POOL_OFF:POOL_OFF + POOL_COLS])
    gm_ref[...] = jax.nn.sigmoid(_dot(hn, w_ref[:, MERGE_OFF:MERGE_OFF + 2 * d]))


def _pack_w_in(w_in, d):
    w_in = w_in.astype(BF16)
    qw = N_HEADS * HEAD_DIM
    kvw = N_GROUPS * HEAD_DIM

    kc, vc, ks, vs, kw, vw = (w_in[:, qw + i * kvw:qw + (i + 1) * kvw] for i in range(6))
    off = qw + 6 * kvw
    n_g = 3 * N_HEADS
    w_g = jnp.pad(w_in[:, off:off + n_g], ((0, 0), (0, LANES - n_g)))
    w_rest = w_in[:, off + n_g:]
    return jnp.concatenate([w_in[:, :qw], kc, vc, ks, kw,
                            vs, vw, w_g, w_rest], axis=1)


def _proj(x, norm_g, w_packed, batch, seq, *, tm=512):
    n, d = x.shape
    tiles_per_seq = seq // tm
    row = lambda i: (i, 0)
    const = lambda i: (0, 0)
    per_group = lambda i: (i // tiles_per_seq, 0, i % tiles_per_seq, 0)
    qf = _query_features()
    kf = _key_features(np.arange(seq), LANES, HEAD_DIM)
    flat = lambda width, dtype: (jax.ShapeDtypeStruct((n, width), dtype), pl.BlockSpec((tm, width), row))
    grouped = lambda count: (jax.ShapeDtypeStruct((batch, count, seq, LANES), BF16),
                             pl.BlockSpec((1, count, tm, LANES), per_group))
    values_t = (jax.ShapeDtypeStruct((batch, N_GROUPS, V_ROWS, seq), BF16),
                pl.BlockSpec((1, N_GROUPS, V_ROWS, tm), lambda i: (i // tiles_per_seq, 0, 0, i % tiles_per_seq)))
    outs = [grouped(N_HEADS), flat(LANES, F32), flat(LANES, F32), grouped(N_GROUPS), grouped(N_GROUPS),
            values_t, values_t, flat(LANES, F32), flat(POOL_COLS, F32), flat(2 * d, F32)]
    return pl.pallas_call(
        _proj_kernel,
        out_shape=[o[0] for o in outs],
        grid=(n // tm,),
        in_specs=[pl.BlockSpec((tm, d), row), pl.BlockSpec((1, d), const),
                  pl.BlockSpec(w_packed.shape, const, pipeline_mode=pl.Buffered(1)),
                  pl.BlockSpec(qf.shape, const), pl.BlockSpec((tm, LANES), lambda i: (i % tiles_per_seq, 0))],
        out_specs=[o[1] for o in outs],
        compiler_params=_params("parallel"),
        name="proj",
    )(x, norm_g, w_packed, qf, kf)


def _compress_kernel(k_ref, v_ref, pos_ref, wk1_ref, wk2_ref, wv1_ref, wv2_ref, feat_ref, ko_ref, vo_ref):
    n_chunk = k_ref.shape[0] // CMP_STRIDE
    hidden = wk2_ref.shape[0]

    def compress(x_ref, w1_ref, w2_ref):
        first = second = None
        for r in range(CMP_STRIDE):
            rows = x_ref[pl.ds(r, n_chunk, stride=CMP_STRIDE), :]
            a = _dot((rows + pos_ref[r:r + 1, :]).astype(BF16), w1_ref[r])
            b = _dot((rows + pos_ref[CMP_STRIDE + r:CMP_STRIDE + r + 1, :]).astype(BF16), w1_ref[CMP_STRIDE + r])
            first = a if first is None else first + a
            second = b if second is None else second + b
        act = jax.nn.gelu(first + pltpu.roll(second, n_chunk - 1, 0)).astype(BF16)
        return [_dot(act[:, g * hidden:(g + 1) * hidden], w2_ref[...]) for g in range(N_GROUPS)]

    for g, k_cmp in enumerate(compress(k_ref, wk1_ref, wk2_ref)):
        ko_ref[0, g] = jnp.concatenate([k_cmp, feat_ref[...]], axis=-1).astype(BF16)
    _store_values_t(vo_ref, jnp.concatenate(compress(v_ref, wv1_ref, wv2_ref), axis=-1))


def _block_diag_w1(w1):
    w = w1.reshape(CMP_LEN, HEAD_DIM, -1)
    z = jnp.zeros_like(w)
    return jnp.concatenate([jnp.concatenate([w, z], axis=2), jnp.concatenate([z, w], axis=2)], axis=1).astype(BF16)


def _compress(kc, vc, pos, wk1, wk2, wv1, wv2, batch, seq):
    n_chunk = seq // CMP_STRIDE
    rows = pl.BlockSpec((seq, LANES), lambda b: (b, 0))
    full = lambda a: pl.BlockSpec(a.shape, lambda b: (0,) * a.ndim, pipeline_mode=pl.Buffered(1))
    pos2 = jnp.concatenate([pos] * N_GROUPS, axis=1)
    feat = _key_features(np.arange(n_chunk) * CMP_STRIDE + CMP_LEN - 1, HEAD_DIM, 0)
    args = (pos2, _block_diag_w1(wk1), wk2.astype(BF16), _block_diag_w1(wv1), wv2.astype(BF16), feat)
    return pl.pallas_call(
        _compress_kernel,
        out_shape=[jax.ShapeDtypeStruct((batch, N_GROUPS, n_chunk, LANES), BF16),
                   jax.ShapeDtypeStruct((batch, N_GROUPS, V_ROWS, n_chunk), BF16)],
        grid=(batch,),
        in_specs=[rows, rows] + [full(a) for a in args],
        out_specs=[pl.BlockSpec((1, N_GROUPS, n_chunk, LANES), lambda b: (b, 0, 0, 0)),
                   pl.BlockSpec((1, N_GROUPS, V_ROWS, n_chunk), lambda b: (b, 0, 0, 0))],
        compiler_params=_params("parallel"),
        name="compress",
    )(kc, vc, *args)


SLC_CHUNK = 256
BLOCKS_PER_CHUNK = SLC_CHUNK // SEL_LEN
HEAD_BLOCKS = 2


def _nsa_kernel(q_ref, kc_ref, vct_ref, ks_ref, vst_ref, kw_ref, vwt_ref, gate_ref, o_ref,
                st_ref, p_ref, gt_ref, selt_ref, list_ref, m_ref, acc_ref, out_ref, *, tq, seq):
    n_sel = seq // SEL_LEN
    n_chunks = seq // SLC_CHUNK
    n_cmp = kc_ref.shape[2]
    n_win = WINDOW + tq
    q0 = pl.program_id(1) * tq
    gt_ref[...] = gate_ref[...].T
    pairs = [(g, g * HEADS_PER_GROUP + 2 * j) for g in range(N_GROUPS) for j in range(HEADS_PER_GROUP // 2)]

    def gate_row(head, branch):
        c = 3 * head + branch
        return gt_ref[c:c + 1, :]

    def out_rows(head):
        g, hh = divmod(head, HEADS_PER_GROUP)
        return hh, slice(g * HEAD_DIM, (g + 1) * HEAD_DIM)

    def stage_scores(keys, n_keys, row0=0):
        for g, head0 in pairs:
            q_pair = q_ref[0, head0:head0 + 2].reshape(2 * tq, LANES)
            scores = _dot_nt(keys(g), q_pair)
            for i in range(2):
                st_ref[head0 + i, row0:row0 + n_keys, :] = scores[:, i * tq:(i + 1) * tq]

    def finish_pair(head0, pv, branch, first, ok=None):
        invs = []
        for i in range(2):
            cols = slice(i * tq, (i + 1) * tq)
            inv = 1.0 / pv[HEAD_DIM:HEAD_DIM + 1, cols]
            if ok is not None:
                inv = jnp.where(ok, inv, 0.0)
            hh, rows = out_rows(head0 + i)
            contrib = pv[0:HEAD_DIM, cols] * (inv * gate_row(head0 + i, branch))
            out_ref[hh, rows, :] = contrib if first else out_ref[hh, rows, :] + contrib
            invs.append(inv)
        return invs

    t_c = q0 + lax.broadcasted_iota(jnp.int32, (n_cmp, tq), 1)
    end_c = lax.broadcasted_iota(jnp.int32, (n_cmp, tq), 0) * CMP_STRIDE + (CMP_LEN - 1)
    bias_c = jnp.where(t_c >= end_c, 0.0, NEG_BIG)
    has_cmp = (q0 + lax.broadcasted_iota(jnp.int32, (1, tq), 1)) >= CMP_LEN - 1
    inv_c = [None] * N_HEADS
    raw = [None] * (N_HEADS // 2)

    def cmp_pair(g, head0):
        for i in range(2):
            x = st_ref[head0 + i, 0:n_cmp, :] + bias_c
            p_ref[head0 // 2, 0:n_cmp, i * tq:(i + 1) * tq] = jnp.exp2(x - jnp.max(x, axis=0, keepdims=True)).astype(BF16)
        both = _dot(vct_ref[0, g], p_ref[head0 // 2, 0:n_cmp, :])
        raw[head0 // 2] = both[V_ROWS:, :]
        inv_c[head0], inv_c[head0 + 1] = finish_pair(head0, both[0:V_ROWS, :], 0, True, has_cmp)

    stage_scores(lambda g: kc_ref[0, g], n_cmp)
    for g, head0 in pairs:
        cmp_pair(g, head0)

    w0 = pl.multiple_of(jnp.maximum(q0 - WINDOW, 0), tq)
    stage_scores(lambda g: kw_ref[0, g, pl.ds(w0, n_win), :], n_win)
    head_keys = HEAD_BLOCKS * SEL_LEN
    stage_scores(lambda g: ks_ref[0, g, 0:head_keys, :], head_keys, n_win)

    dist_w = (q0 + lax.broadcasted_iota(jnp.int32, (n_win, tq), 1)) - (w0 + lax.broadcasted_iota(jnp.int32, (n_win, tq), 0))
    bias_w = jnp.where((dist_w >= 0) & (dist_w < WINDOW), 0.0, NEG_BIG)

    def win_pair(g, head0):
        for i in range(2):
            x = st_ref[head0 + i, 0:n_win, :] + bias_w
            p_ref[head0 // 2, 0:n_win, i * tq:(i + 1) * tq] = jnp.exp2(x - jnp.max(x, axis=0, keepdims=True)).astype(BF16)
        pv = _dot(vwt_ref[0, g, :, pl.ds(w0, n_win)], p_ref[head0 // 2, 0:n_win, :])
        finish_pair(head0, pv, 2, False)

    for g, head0 in pairs:
        win_pair(g, head0)

    blk = lax.broadcasted_iota(jnp.int32, (n_sel, tq), 0)
    cur = (q0 + lax.broadcasted_iota(jnp.int32, (n_sel, tq), 1)) // SEL_LEN
    valid = blk <= cur
    forced = valid & ((blk == 0) | (blk == cur) | (blk == cur - 1))
    blk_f = blk.astype(F32)
    score = []
    for g in range(N_GROUPS):
        total = None
        for head in range(g * HEADS_PER_GROUP, (g + 1) * HEADS_PER_GROUP):
            part = raw[head // 2][:, (head % 2) * tq:(head % 2 + 1) * tq] * inv_c[head]
            total = part if total is None else total + part
        score.append(jnp.where(forced, TOPK_TAKEN, jnp.where(valid, total, -TOPK_BIG)))

    for _ in range(min(N_SELECT, n_sel) - N_FORCED):
        for g in range(N_GROUPS):
            best = jnp.max(score[g], axis=0, keepdims=True)
            pick = jnp.min(jnp.where(score[g] == best, blk_f, float(n_sel)), axis=0, keepdims=True)
            score[g] = jnp.where(blk_f == pick, TOPK_TAKEN, score[g])
    sel = [jnp.where(valid & (score[g] == TOPK_TAKEN), 1.0, 0.0) for g in range(N_GROUPS)]
    for g in range(N_GROUPS):
        selt_ref[g] = jnp.where(blk >= HEAD_BLOCKS, sel[g], 0.0)
    sel_any = jnp.where(blk >= HEAD_BLOCKS, sel[0] + sel[1], 0.0)
    n_items = jnp.int32(0)
    for c in range(n_chunks):
        any_sel = jnp.max(sel_any[c * BLOCKS_PER_CHUNK:(c + 1) * BLOCKS_PER_CHUNK, :])
        list_ref[n_items] = c
        n_items = n_items + (any_sel > 0.5).astype(jnp.int32)

    causal_h = (q0 + lax.broadcasted_iota(jnp.int32, (head_keys, tq), 1)) >= lax.broadcasted_iota(jnp.int32, (head_keys, tq), 0)
    head_rows = slice(n_win, n_win + head_keys)
    for g, head0 in pairs:
        if head0 % HEADS_PER_GROUP == 0:
            sel_keys = jnp.concatenate([jnp.broadcast_to(sel[g][i:i + 1, :], (SEL_LEN, tq)) for i in range(HEAD_BLOCKS)], axis=0)
            bias = jnp.where(causal_h & (sel_keys > 0.5), 0.0, NEG_BIG)
        for i in range(2):
            x = st_ref[head0 + i, head_rows, :] + bias
            m_new = jnp.max(x, axis=0, keepdims=True)
            m_ref[head0 + i:head0 + i + 1, :] = m_new
            p_ref[head0 // 2, head_rows, i * tq:(i + 1) * tq] = jnp.exp2(x - m_new).astype(BF16)
    for g, head0 in pairs:
        acc_ref[head0 // 2] = _dot(vst_ref[0, g, :, 0:head_keys], p_ref[head0 // 2, head_rows, :])

    base = (lax.broadcasted_iota(jnp.int32, (SLC_CHUNK, tq), 1)
            - lax.broadcasted_iota(jnp.int32, (SLC_CHUNK, tq), 0))

    def stage_chunk(item, row0):
        c = list_ref[item]
        k0 = pl.multiple_of(c * SLC_CHUNK, SLC_CHUNK)
        causal = base >= k0 - q0
        for g, head0 in pairs:
            if head0 % HEADS_PER_GROUP == 0:
                sel_keys = jnp.concatenate(
                    [jnp.broadcast_to(selt_ref[g, pl.ds(c * BLOCKS_PER_CHUNK + i, 1), :], (SEL_LEN, tq))
                     for i in range(BLOCKS_PER_CHUNK)], axis=0)
                bias = jnp.where(causal & (sel_keys > 0.5), 0.0, NEG_BIG)
                bias2 = jnp.concatenate([bias, bias], axis=1)
            q_pair = q_ref[0, head0:head0 + 2].reshape(2 * tq, LANES)
            scores = _dot_nt(ks_ref[0, g, pl.ds(k0, SLC_CHUNK), :], q_pair) + bias2
            for i in range(2):
                st_ref[head0 + i, row0:row0 + SLC_CHUNK, :] = scores[:, i * tq:(i + 1) * tq]

    def consume_chunk(item, row0):
        k0 = pl.multiple_of(list_ref[item] * SLC_CHUNK, SLC_CHUNK)
        for g, head0 in pairs:
            alphas = []
            for i, head in enumerate((head0, head0 + 1)):
                x = st_ref[head, row0:row0 + SLC_CHUNK, :]
                m_old = m_ref[head:head + 1, :]
                m_new = jnp.maximum(m_old, jnp.max(x, axis=0, keepdims=True))
                alphas.append(jnp.exp2(m_old - m_new))
                m_ref[head:head + 1, :] = m_new
                p_ref[head0 // 2, 0:SLC_CHUNK, i * tq:(i + 1) * tq] = jnp.exp2(x - m_new).astype(BF16)
            pv = _dot(vst_ref[0, g, :, pl.ds(k0, SLC_CHUNK)], p_ref[head0 // 2, 0:SLC_CHUNK, :])
            acc_ref[head0 // 2] = acc_ref[head0 // 2] * jnp.concatenate(alphas, axis=1) + pv

    last = jnp.maximum(n_items - 1, 0)
    stage_chunk(0, 0)

    def two_chunks(j, carry):
        stage_chunk(jnp.minimum(2 * j + 1, last), SLC_CHUNK)
        consume_chunk(2 * j, 0)

        @pl.when(2 * j + 1 < n_items)
        def _():
            stage_chunk(jnp.minimum(2 * j + 2, last), 0)
            consume_chunk(2 * j + 1, SLC_CHUNK)
        return carry

    lax.fori_loop(0, (n_items + 1) // 2, two_chunks, 0)
    for g, head0 in pairs:
        finish_pair(head0, acc_ref[head0 // 2], 1, False)

    for hh in range(HEADS_PER_GROUP):
        o_ref[:, hh * LANES:(hh + 1) * LANES] = out_ref[hh].T.astype(BF16)


def _overlap_t(seq):
    n_cmp = (seq - CMP_LEN) // CMP_STRIDE + 1
    n_sel = seq // SEL_LEN
    cs = np.arange(n_cmp)[:, None] * CMP_STRIDE
    ss = np.arange(n_sel)[None, :] * SEL_LEN
    ov = np.clip(np.minimum(cs + CMP_LEN, ss + SEL_LEN) - np.maximum(cs, ss), 0, None) / CMP_LEN
    out = np.zeros((n_sel, seq // CMP_STRIDE), np.float32)
    out[:, :n_cmp] = ov.T
    return jnp.asarray(out, dtype=BF16)


def _nsa(q, kc, vct, ks, vst, kw, vwt, gates, *, tq=128):
    batch, _, seq, _ = q.shape
    n_cmp = kc.shape[2]
    n_sel = seq // SEL_LEN
    assert seq % SLC_CHUNK == 0 and seq >= WINDOW + tq and SLC_CHUNK % tq == 0 and tq == LANES
    assert n_sel <= 2 * SEL_LEN
    ovt = jnp.broadcast_to(_overlap_t(seq), (batch, N_GROUPS, n_sel, n_cmp))
    vct = jnp.concatenate([vct, ovt], axis=2)
    keys = lambda n: pl.BlockSpec((1, N_GROUPS, n, LANES), lambda b, i: (b, 0, 0, 0))
    vals = lambda n: pl.BlockSpec((1, N_GROUPS, V_ROWS, n), lambda b, i: (b, 0, 0, 0))
    tiles = seq // tq
    s_rows = max(n_cmp, WINDOW + tq + HEAD_BLOCKS * SEL_LEN, 2 * SLC_CHUNK)
    return pl.pallas_call(
        functools.partial(_nsa_kernel, tq=tq, seq=seq),
        out_shape=jax.ShapeDtypeStruct((batch * seq, HEADS_PER_GROUP * LANES), BF16),
        grid=(batch, tiles),
        in_specs=[
            pl.BlockSpec((1, N_HEADS, tq, LANES), lambda b, i: (b, 0, i, 0)),
            keys(n_cmp), pl.BlockSpec((1, N_GROUPS, V_ROWS + n_sel, n_cmp), lambda b, i: (b, 0, 0, 0)),
            keys(seq), vals(seq), keys(seq), vals(seq),
            pl.BlockSpec((tq, LANES), lambda b, i: (b * tiles + i, 0)),
        ],
        out_specs=pl.BlockSpec((tq, HEADS_PER_GROUP * LANES), lambda b, i: (b * tiles + i, 0)),
        scratch_shapes=[
            pltpu.VMEM((N_HEADS, s_rows, tq), F32),
            pltpu.VMEM((N_HEADS // 2, s_rows, 2 * tq), BF16),
            pltpu.VMEM((LANES, tq), F32),
            pltpu.VMEM((N_GROUPS, n_sel, tq), F32),
            pltpu.SMEM((seq // SLC_CHUNK,), jnp.int32),
            pltpu.VMEM((N_HEADS, tq), F32),
            pltpu.VMEM((N_HEADS // 2, V_ROWS, 2 * tq), F32),
            pltpu.VMEM((HEADS_PER_GROUP, N_GROUPS * HEAD_DIM, tq), F32),
        ],
        compiler_params=_params("parallel", "arbitrary"),
        name="nsa",
    )(q, kc, vct, ks, vst, kw, vwt, gates)


HALO = max(POOL_WINDOWS)
assert all(w & (w - 1) == 0 for w in POOL_WINDOWS)


def _merge_kernel(x_ref, u_ref, halo_ref, gm_ref, on_ref, pw_ref, ps_ref, wbp_ref, wbn_ref, wo_ref,
                  o_ref, ext_ref, *, tm, seq):
    d = x_ref.shape[1]
    b = _dot(on_ref[...], wbn_ref[...])
    pos0 = (pl.program_id(0) * tm) % seq
    ext_ref[0:HALO, :] = jnp.where(pos0 == 0, 0.0, halo_ref[...])
    ext_ref[HALO:HALO + tm, :] = u_ref[...]
    pos = (pos0 + lax.broadcasted_iota(jnp.int32, (tm, POOL_GROUP_DIM), 0)).astype(F32)
    mixed = []
    for gi, w in enumerate(POOL_WINDOWS):
        cols = slice(gi * POOL_GROUP_DIM, (gi + 1) * POOL_GROUP_DIM)
        run = ext_ref[:, cols]
        for k in range(w.bit_length() - 1):
            run = run + pltpu.roll(run, 1 << k, 0)
        u = ext_ref[HALO:HALO + tm, cols]
        delta = run[HALO:HALO + tm] / jnp.minimum(pos + 1.0, float(w)) - u
        mixed.append(_dot(delta.astype(BF16), pw_ref[gi]) * ps_ref[:, cols])
    mixed = jnp.concatenate(mixed, axis=-1).astype(BF16)
    a = _dot(mixed, wbp_ref[...])
    merged = gm_ref[:, 0:d] * a + gm_ref[:, d:2 * d] * b
    o_ref[...] = x_ref[...] + _dot(merged.astype(BF16), wo_ref[...])


def _merge(x, u, gm, o_nsa, pool_w, pool_scale, w_bp, w_bn, w_out, seq, *, tm=512):
    n, d = x.shape
    row = lambda i: (i, 0)
    full = lambda a: pl.BlockSpec(a.shape, lambda i: (0,) * a.ndim, pipeline_mode=pl.Buffered(1))
    halo_blocks = tm // HALO
    return pl.pallas_call(
        functools.partial(_merge_kernel, tm=tm, seq=seq),
        out_shape=jax.ShapeDtypeStruct((n, d), F32),
        grid=(n // tm,),
        in_specs=[
            pl.BlockSpec((tm, d), row),
            pl.BlockSpec((tm, POOL_COLS), row),
            pl.BlockSpec((HALO, POOL_COLS), lambda i: (jnp.maximum(i * halo_blocks - 1, 0), 0)),
            pl.BlockSpec((tm, 2 * d), row),
            pl.BlockSpec((tm, d), row),
            full(pool_w), full(pool_scale), full(w_bp), full(w_bn), full(w_out),
        ],
        out_specs=pl.BlockSpec((tm, d), row),
        scratch_shapes=[pltpu.VMEM((HALO + tm, POOL_COLS), F32)],
        compiler_params=_params("parallel"),
        name="merge",
    )(x, u, u, gm, o_nsa, pool_w, pool_scale, w_bp, w_bn, w_out)


def kernel(x, ffn1_norm, ffn1_w_gate, ffn1_w_up, ffn1_w_down, mix_norm, w_in, cmp_pos, cmp_k_w1, cmp_k_w2, cmp_v_w1, cmp_v_w2, pool_w, pool_scale, w_branch_pool, w_branch_nsa, w_out, ffn2_norm, ffn2_w_gate, ffn2_w_up, ffn2_w_down, final_norm):
    batch, seq, d = x.shape
    depth = w_in.shape[0]
    xf = x.reshape(batch * seq, d)
    bf = lambda a: a.astype(BF16)
    row = lambda a: a.reshape(1, -1)
    for l in range(depth):
        xf = _ffn(xf, row(ffn1_norm[l]), bf(ffn1_w_gate[l]), bf(ffn1_w_up[l]), bf(ffn1_w_down[l]),
                  row(final_norm), final_norm=False)

        q, kc, vc, ks, kw, vs, vw, gates, u, gm = _proj(xf, row(mix_norm[l]), _pack_w_in(w_in[l], d), batch, seq)
        k_cmp, v_cmp = _compress(kc, vc, cmp_pos[l], cmp_k_w1[l], cmp_k_w2[l], cmp_v_w1[l], cmp_v_w2[l], batch, seq)
        o_nsa = _nsa(q, k_cmp, v_cmp, ks, vs, kw, vw, gates)

        w_bn = w_branch_nsa[l].reshape(N_GROUPS, HEADS_PER_GROUP, HEAD_DIM, d).transpose(1, 0, 2, 3).reshape(-1, d)
        xf = _merge(xf, u, gm, o_nsa, bf(pool_w[l]), row(pool_scale[l]), bf(w_branch_pool[l]), bf(w_bn),
                    bf(w_out[l]), seq)

        xf = _ffn(xf, row(ffn2_norm[l]), bf(ffn2_w_gate[l]), bf(ffn2_w_up[l]), bf(ffn2_w_down[l]),
                  row(final_norm), final_norm=(l == depth - 1))
    return xf.reshape(batch, seq, d)
```

```python
import functools

import jax
import jax.numpy as jnp
import numpy as np
from jax import lax
from jax.experimental import pallas as pl
from jax.experimental.pallas import tpu as pltpu

N_HEADS = 16
N_GROUPS = 2
HEADS_PER_GROUP = N_HEADS // N_GROUPS
HEAD_DIM = 64
CMP_LEN = 32
CMP_STRIDE = 16
SEL_LEN = 64
N_SELECT = 16
WINDOW = 512
POOL_WINDOWS = (2, 4, 8, 16)
POOL_GROUP_DIM = 128
RMS_EPS = 1e-6
ALIBI_MAX_BIAS = 8.0

LANES = 128
NEG_BIG = -1e30
TOPK_BIG = 1e30
N_FORCED = 3
TOPK_TAKEN = -float(2 ** 101)
VMEM_LIMIT = 48 * 1024 * 1024

BF16 = jnp.bfloat16
F32 = jnp.float32


def _dot(a, b):
    return jnp.dot(a, b, preferred_element_type=F32)


def _dot_nt(a, b):
    return lax.dot_general(a, b, (((1,), (1,)), ((), ())), preferred_element_type=F32)


def _rms(x, g):
    return x * lax.rsqrt(jnp.mean(x * x, axis=-1, keepdims=True) + RMS_EPS) * g


def _params(*sem):
    return pltpu.CompilerParams(dimension_semantics=sem, vmem_limit_bytes=VMEM_LIMIT)


def _ffn_kernel(x_ref, g_ref, wg_ref, wu_ref, wd_ref, fin_ref, o_ref, *, final_norm):
    x = x_ref[...]
    xn = _rms(x, g_ref[...]).astype(BF16)
    gate = _dot(xn, wg_ref[...])
    up = _dot(xn, wu_ref[...])
    act = (gate * jax.nn.sigmoid(gate)) * up
    y = x + 0.5 * _dot(act.astype(BF16), wd_ref[...])
    if final_norm:
        y = _rms(y, fin_ref[...])
    o_ref[...] = y


def _ffn(x, norm_g, wg, wu, wd, fin_g, *, final_norm, tm=512):
    n, d = x.shape
    row = lambda i: (i, 0)
    resident = lambda a: pl.BlockSpec(a.shape, lambda i: (0, 0), pipeline_mode=pl.Buffered(1))
    return pl.pallas_call(
        functools.partial(_ffn_kernel, final_norm=final_norm),
        out_shape=jax.ShapeDtypeStruct((n, d), F32),
        grid=(n // tm,),
        in_specs=[pl.BlockSpec((tm, d), row), resident(norm_g), resident(wg), resident(wu), resident(wd),
                  resident(fin_g)],
        out_specs=pl.BlockSpec((tm, d), row),
        compiler_params=_params("parallel"),
        name="ffn",
    )(x, norm_g, wg, wu, wd, fin_g)


LOG2E = 1.4426950408889634
N_FEAT = 6


def _bf16_pieces(x):
    x = np.asarray(x, np.float32)
    s1 = x.astype(BF16).astype(np.float32)
    s2 = (x - s1).astype(BF16).astype(np.float32)
    s3 = (x - s1 - s2).astype(BF16).astype(np.float32)
    return s1, s2, s3


def _query_features():
    slopes = np.float32(2.0) ** (-ALIBI_MAX_BIAS * np.arange(1, N_HEADS + 1, dtype=np.float32) / N_HEADS)
    s1, s2, s3 = _bf16_pieces(slopes * np.float32(LOG2E))
    feat = np.zeros((N_HEADS, LANES), np.float32)
    feat[:, HEAD_DIM:HEAD_DIM + N_FEAT] = np.stack([s1, s2, s3, SEL_LEN * s1, SEL_LEN * s2, SEL_LEN * s3], axis=1)
    return jnp.asarray(feat)


def _key_features(pos, width, offset):
    pos = np.asarray(pos)
    a, b = (pos // SEL_LEN).astype(np.float32), (pos % SEL_LEN).astype(np.float32)
    feat = np.zeros((len(pos), width), np.float32)
    feat[:, offset:offset + N_FEAT] = np.stack([b, b, b, a, a, a], axis=1)
    return jnp.asarray(feat)


Q_COLS = N_HEADS * HEAD_DIM
CMP_OFF = Q_COLS
KEY_OFF = CMP_OFF + 2 * LANES
VAL_OFF = KEY_OFF + 2 * LANES
GATE_OFF = VAL_OFF + 2 * LANES
POOL_OFF = GATE_OFF + LANES
POOL_COLS = len(POOL_WINDOWS) * POOL_GROUP_DIM
MERGE_OFF = POOL_OFF + POOL_COLS


V_ROWS = HEAD_DIM + 16


def _store_values_t(v_ref, v):
    vt = v.T
    for g in range(N_GROUPS):
        v_ref[0, g, 0:HEAD_DIM, :] = vt[g * HEAD_DIM:(g + 1) * HEAD_DIM, :].astype(BF16)
        v_ref[0, g, HEAD_DIM:V_ROWS, :] = jnp.ones((V_ROWS - HEAD_DIM, v.shape[0]), BF16)


def _proj_kernel(x_ref, g_ref, w_ref, qf_ref, kf_ref, q_ref, kc_ref, vc_ref, ks_ref, kw_ref, vs_ref, vw_ref,
                 gate_ref, u_ref, gm_ref):
    d = x_ref.shape[1]
    hn = _rms(x_ref[...], g_ref[...]).astype(BF16)
    q_scale = HEAD_DIM ** -0.5 * LOG2E
    low = lax.broadcasted_iota(jnp.int32, (x_ref.shape[0], LANES), 1) < HEAD_DIM
    for j in range(N_HEADS // 4):
        q4 = _dot(hn, w_ref[:, j * 2 * LANES:(j + 1) * 2 * LANES]) * q_scale
        for b in range(2):
            both = q4[:, b * LANES:(b + 1) * LANES]
            h = 4 * j + 2 * b
            q_ref[0, h] = (jnp.where(low, both, 0.0) + qf_ref[h:h + 1, :]).astype(BF16)
            q_ref[0, h + 1] = (jnp.where(low, pltpu.roll(both, HEAD_DIM, 1), 0.0) + qf_ref[h + 1:h + 2, :]).astype(BF16)
    cmp_in = _dot(hn, w_ref[:, CMP_OFF:CMP_OFF + 2 * LANES])
    kc_ref[...] = cmp_in[:, 0:LANES]
    vc_ref[...] = cmp_in[:, LANES:2 * LANES]
    keys = _dot(hn, w_ref[:, KEY_OFF:KEY_OFF + 2 * LANES])
    kf = kf_ref[...]
    for i, k_ref in enumerate((ks_ref, kw_ref)):
        both = keys[:, i * LANES:(i + 1) * LANES]
        k_ref[0, 0] = (jnp.where(low, both, 0.0) + kf).astype(BF16)
        k_ref[0, 1] = (jnp.where(low, pltpu.roll(both, HEAD_DIM, 1), 0.0) + kf).astype(BF16)
    vals = _dot(hn, w_ref[:, VAL_OFF:VAL_OFF + 2 * LANES])
    for i, v_ref in enumerate((vs_ref, vw_ref)):
        _store_values_t(v_ref, vals[:, i * LANES:(i + 1) * LANES])
    gate_ref[...] = jax.nn.sigmoid(_dot(hn, w_ref[:, GATE_OFF:GATE_OFF + LANES]))
    u_ref[...] = _dot(hn, w_ref[:, POOL_OFF:POOL_OFF + POOL_COLS])
    gm_ref[...] = jax.nn.sigmoid(_dot(hn, w_ref[:, MERGE_OFF:MERGE_OFF + 2 * d]))


def _pack_w_in(w_in, d):
    w_in = w_in.astype(BF16)
    qw = N_HEADS * HEAD_DIM
    kvw = N_GROUPS * HEAD_DIM

    kc, vc, ks, vs, kw, vw = (w_in[:, qw + i * kvw:qw + (i + 1) * kvw] for i in range(6))
    off = qw + 6 * kvw
    n_g = 3 * N_HEADS
    w_g = jnp.pad(w_in[:, off:off + n_g], ((0, 0), (0, LANES - n_g)))
    w_rest = w_in[:, off + n_g:]
    return jnp.concatenate([w_in[:, :qw], kc, vc, ks, kw,
                            vs, vw, w_g, w_rest], axis=1)


def _proj(x, norm_g, w_packed, batch, seq, *, tm=512):
    n, d = x.shape
    tiles_per_seq = seq // tm
    row = lambda i: (i, 0)
    const = lambda i: (0, 0)
    per_group = lambda i: (i // tiles_per_seq, 0, i % tiles_per_seq, 0)
    qf = _query_features()
    kf = _key_features(np.arange(seq), LANES, HEAD_DIM)
    flat = lambda width, dtype: (jax.ShapeDtypeStruct((n, width), dtype), pl.BlockSpec((tm, width), row))
    grouped = lambda count: (jax.ShapeDtypeStruct((batch, count, seq, LANES), BF16),
                             pl.BlockSpec((1, count, tm, LANES), per_group))
    values_t = (jax.ShapeDtypeStruct((batch, N_GROUPS, V_ROWS, seq), BF16),
                pl.BlockSpec((1, N_GROUPS, V_ROWS, tm), lambda i: (i // tiles_per_seq, 0, 0, i % tiles_per_seq)))
    outs = [grouped(N_HEADS), flat(LANES, F32), flat(LANES, F32), grouped(N_GROUPS), grouped(N_GROUPS),
            values_t, values_t, flat(LANES, F32), flat(POOL_COLS, F32), flat(2 * d, F32)]
    return pl.pallas_call(
        _proj_kernel,
        out_shape=[o[0] for o in outs],
        grid=(n // tm,),
        in_specs=[pl.BlockSpec((tm, d), row), pl.BlockSpec((1, d), const),
                  pl.BlockSpec(w_packed.shape, const, pipeline_mode=pl.Buffered(1)),
                  pl.BlockSpec(qf.shape, const), pl.BlockSpec((tm, LANES), lambda i: (i % tiles_per_seq, 0))],
        out_specs=[o[1] for o in outs],
        compiler_params=_params("parallel"),
        name="proj",
    )(x, norm_g, w_packed, qf, kf)


def _compress_kernel(k_ref, v_ref, pos_ref, wk1_ref, wk2_ref, wv1_ref, wv2_ref, feat_ref, ko_ref, vo_ref):
    n_chunk = k_ref.shape[0] // CMP_STRIDE
    hidden = wk2_ref.shape[0]

    def compress(x_ref, w1_ref, w2_ref):
        first = second = None
        for r in range(CMP_STRIDE):
            rows = x_ref[pl.ds(r, n_chunk, stride=CMP_STRIDE), :]
            a = _dot((rows + pos_ref[r:r + 1, :]).astype(BF16), w1_ref[r])
            b = _dot((rows + pos_ref[CMP_STRIDE + r:CMP_STRIDE + r + 1, :]).astype(BF16), w1_ref[CMP_STRIDE + r])
            first = a if first is None else first + a
            second = b if second is None else second + b
        act = jax.nn.gelu(first + pltpu.roll(second, n_chunk - 1, 0)).astype(BF16)
        return [_dot(act[:, g * hidden:(g + 1) * hidden], w2_ref[...]) for g in range(N_GROUPS)]

    for g, k_cmp in enumerate(compress(k_ref, wk1_ref, wk2_ref)):
        ko_ref[0, g] = jnp.concatenate([k_cmp, feat_ref[...]], axis=-1).astype(BF16)
    _store_values_t(vo_ref, jnp.concatenate(compress(v_ref, wv1_ref, wv2_ref), axis=-1))


def _block_diag_w1(w1):
    w = w1.reshape(CMP_LEN, HEAD_DIM, -1)
    z = jnp.zeros_like(w)
    return jnp.concatenate([jnp.concatenate([w, z], axis=2), jnp.concatenate([z, w], axis=2)], axis=1).astype(BF16)


def _compress(kc, vc, pos, wk1, wk2, wv1, wv2, batch, seq):
    n_chunk = seq // CMP_STRIDE
    rows = pl.BlockSpec((seq, LANES), lambda b: (b, 0))
    full = lambda a: pl.BlockSpec(a.shape, lambda b: (0,) * a.ndim, pipeline_mode=pl.Buffered(1))
    pos2 = jnp.concatenate([pos] * N_GROUPS, axis=1)
    feat = _key_features(np.arange(n_chunk) * CMP_STRIDE + CMP_LEN - 1, HEAD_DIM, 0)
    args = (pos2, _block_diag_w1(wk1), wk2.astype(BF16), _block_diag_w1(wv1), wv2.astype(BF16), feat)
    return pl.pallas_call(
        _compress_kernel,
        out_shape=[jax.ShapeDtypeStruct((batch, N_GROUPS, n_chunk, LANES), BF16),
                   jax.ShapeDtypeStruct((batch, N_GROUPS, V_ROWS, n_chunk), BF16)],
        grid=(batch,),
        in_specs=[rows, rows] + [full(a) for a in args],
        out_specs=[pl.BlockSpec((1, N_GROUPS, n_chunk, LANES), lambda b: (b, 0, 0, 0)),
                   pl.BlockSpec((1, N_GROUPS, V_ROWS, n_chunk), lambda b: (b, 0, 0, 0))],
        compiler_params=_params("parallel"),
        name="compress",
    )(kc, vc, *args)


SLC_CHUNK = 256
BLOCKS_PER_CHUNK = SLC_CHUNK // SEL_LEN
HEAD_BLOCKS = 2


def _nsa_kernel(q_ref, kc_ref, vct_ref, ks_ref, vst_ref, kw_ref, vwt_ref, gate_ref, o_ref,
                st_ref, p_ref, mx_ref, gt_ref, selt_ref, list_ref, m_ref, acc_ref, out_ref, *, tq, seq):
    n_sel = seq // SEL_LEN
    n_chunks = seq // SLC_CHUNK
    n_cmp = kc_ref.shape[2]
    n_win = WINDOW + tq
    q0 = pl.program_id(1) * tq
    gt_ref[...] = gate_ref[...].T
    pairs = [(g, g * HEADS_PER_GROUP + 2 * j) for g in range(N_GROUPS) for j in range(HEADS_PER_GROUP // 2)]

    def gate_row(head, branch):
        c = 3 * head + branch
        return gt_ref[c:c + 1, :]

    def out_rows(head):
        g, hh = divmod(head, HEADS_PER_GROUP)
        return hh, slice(g * HEAD_DIM, (g + 1) * HEAD_DIM)

    def stage_scores(keys, n_keys, row0=0):
        for g, head0 in pairs:
            q_pair = q_ref[0, head0:head0 + 2].reshape(2 * tq, LANES)
            scores = _dot_nt(keys(g), q_pair)
            for i in range(2):
                st_ref[head0 + i, row0:row0 + n_keys, :] = scores[:, i * tq:(i + 1) * tq]

    def finish_pair(head0, pv, branch, first, ok=None):
        invs = []
        for i in range(2):
            cols = slice(i * tq, (i + 1) * tq)
            inv = 1.0 / pv[HEAD_DIM:HEAD_DIM + 1, cols]
            if ok is not None:
                inv = jnp.where(ok, inv, 0.0)
            hh, rows = out_rows(head0 + i)
            contrib = pv[0:HEAD_DIM, cols] * (inv * gate_row(head0 + i, branch))
            out_ref[hh, rows, :] = contrib if first else out_ref[hh, rows, :] + contrib
            invs.append(inv)
        return invs

    t_c = q0 + lax.broadcasted_iota(jnp.int32, (n_cmp, tq), 1)
    end_c = lax.broadcasted_iota(jnp.int32, (n_cmp, tq), 0) * CMP_STRIDE + (CMP_LEN - 1)
    bias_c = jnp.where(t_c >= end_c, 0.0, NEG_BIG)
    has_cmp = (q0 + lax.broadcasted_iota(jnp.int32, (1, tq), 1)) >= CMP_LEN - 1
    inv_c = [None] * N_HEADS
    raw = [None] * (N_HEADS // 2)

    def cmp_pair(g, head0):
        for i in range(2):
            x = st_ref[head0 + i, 0:n_cmp, :] + bias_c
            p_ref[head0 // 2, 0:n_cmp, i * tq:(i + 1) * tq] = jnp.exp2(x - jnp.max(x, axis=0, keepdims=True)).astype(BF16)
        both = _dot(vct_ref[0, g], p_ref[head0 // 2, 0:n_cmp, :])
        raw[head0 // 2] = both[V_ROWS:, :]
        inv_c[head0], inv_c[head0 + 1] = finish_pair(head0, both[0:V_ROWS, :], 0, True, has_cmp)

    stage_scores(lambda g: kc_ref[0, g], n_cmp)
    for g, head0 in pairs:
        cmp_pair(g, head0)

    w0 = pl.multiple_of(jnp.maximum(q0 - WINDOW, 0), tq)
    stage_scores(lambda g: kw_ref[0, g, pl.ds(w0, n_win), :], n_win)
    head_keys = HEAD_BLOCKS * SEL_LEN
    stage_scores(lambda g: ks_ref[0, g, 0:head_keys, :], head_keys, n_win)

    dist_w = (q0 + lax.broadcasted_iota(jnp.int32, (n_win, tq), 1)) - (w0 + lax.broadcasted_iota(jnp.int32, (n_win, tq), 0))
    bias_w = jnp.where((dist_w >= 0) & (dist_w < WINDOW), 0.0, NEG_BIG)

    def win_pair(g, head0):
        for i in range(2):
            x = st_ref[head0 + i, 0:n_win, :] + bias_w
            p_ref[head0 // 2, 0:n_win, i * tq:(i + 1) * tq] = jnp.exp2(x - jnp.max(x, axis=0, keepdims=True)).astype(BF16)
        pv = _dot(vwt_ref[0, g, :, pl.ds(w0, n_win)], p_ref[head0 // 2, 0:n_win, :])
        finish_pair(head0, pv, 2, False)

    for g, head0 in pairs:
        win_pair(g, head0)

    blk = lax.broadcasted_iota(jnp.int32, (n_sel, tq), 0)
    cur = (q0 + lax.broadcasted_iota(jnp.int32, (n_sel, tq), 1)) // SEL_LEN
    valid = blk <= cur
    forced = valid & ((blk == 0) | (blk == cur) | (blk == cur - 1))
    blk_f = blk.astype(F32)
    score = []
    for g in range(N_GROUPS):
        total = None
        for head in range(g * HEADS_PER_GROUP, (g + 1) * HEADS_PER_GROUP):
            part = raw[head // 2][:, (head % 2) * tq:(head % 2 + 1) * tq] * inv_c[head]
            total = part if total is None else total + part
        score.append(jnp.where(forced, TOPK_TAKEN, jnp.where(valid, total, -TOPK_BIG)))

    for _ in range(min(N_SELECT, n_sel) - N_FORCED):
        for g in range(N_GROUPS):
            best = jnp.max(score[g], axis=0, keepdims=True)
            pick = jnp.min(jnp.where(score[g] == best, blk_f, float(n_sel)), axis=0, keepdims=True)
            score[g] = jnp.where(blk_f == pick, TOPK_TAKEN, score[g])
    sel = [jnp.where(valid & (score[g] == TOPK_TAKEN), 1.0, 0.0) for g in range(N_GROUPS)]
    for g in range(N_GROUPS):
        selt_ref[g] = jnp.where(blk >= HEAD_BLOCKS, sel[g], 0.0)
    sel_any = jnp.where(blk >= HEAD_BLOCKS, sel[0] + sel[1], 0.0)
    n_items = jnp.int32(0)
    for c in range(n_chunks):
        any_sel = jnp.max(sel_any[c * BLOCKS_PER_CHUNK:(c + 1) * BLOCKS_PER_CHUNK, :])
        list_ref[n_items] = c
        n_items = n_items + (any_sel > 0.5).astype(jnp.int32)

    causal_h = (q0 + lax.broadcasted_iota(jnp.int32, (head_keys, tq), 1)) >= lax.broadcasted_iota(jnp.int32, (head_keys, tq), 0)
    head_rows = slice(n_win, n_win + head_keys)
    for g, head0 in pairs:
        if head0 % HEADS_PER_GROUP == 0:
            sel_keys = jnp.concatenate([jnp.broadcast_to(sel[g][i:i + 1, :], (SEL_LEN, tq)) for i in range(HEAD_BLOCKS)], axis=0)
            bias = jnp.where(causal_h & (sel_keys > 0.5), 0.0, NEG_BIG)
        for i in range(2):
            x = st_ref[head0 + i, head_rows, :] + bias
            m_new = jnp.max(x, axis=0, keepdims=True)
            m_ref[head0 + i:head0 + i + 1, :] = m_new
            p_ref[head0 // 2, head_rows, i * tq:(i + 1) * tq] = jnp.exp2(x - m_new).astype(BF16)
    for g, head0 in pairs:
        acc_ref[head0 // 2] = _dot(vst_ref[0, g, :, 0:head_keys], p_ref[head0 // 2, head_rows, :])

    base = (lax.broadcasted_iota(jnp.int32, (SLC_CHUNK, tq), 1)
            - lax.broadcasted_iota(jnp.int32, (SLC_CHUNK, tq), 0))

    def stage_chunk(item, row0):
        c = list_ref[item]
        k0 = pl.multiple_of(c * SLC_CHUNK, SLC_CHUNK)
        causal = base >= k0 - q0
        for g, head0 in pairs:
            if head0 % HEADS_PER_GROUP == 0:
                sel_keys = jnp.concatenate(
                    [jnp.broadcast_to(selt_ref[g, pl.ds(c * BLOCKS_PER_CHUNK + i, 1), :], (SEL_LEN, tq))
                     for i in range(BLOCKS_PER_CHUNK)], axis=0)
                bias = jnp.where(causal & (sel_keys > 0.5), 0.0, NEG_BIG)
                bias2 = jnp.concatenate([bias, bias], axis=1)
            q_pair = q_ref[0, head0:head0 + 2].reshape(2 * tq, LANES)
            scores = _dot_nt(ks_ref[0, g, pl.ds(k0, SLC_CHUNK), :], q_pair) + bias2
            slot = (row0 // SLC_CHUNK) * (N_HEADS // 2) + head0 // 2
            mx_ref[slot:slot + 1, :] = jnp.max(scores, axis=0, keepdims=True)
            for i in range(2):
                st_ref[head0 + i, row0:row0 + SLC_CHUNK, :] = scores[:, i * tq:(i + 1) * tq]

    def consume_chunk(item, row0):
        k0 = pl.multiple_of(list_ref[item] * SLC_CHUNK, SLC_CHUNK)
        for g, head0 in pairs:
            alphas = []
            for i, head in enumerate((head0, head0 + 1)):
                slot = (row0 // SLC_CHUNK) * (N_HEADS // 2) + head0 // 2
                m_old = m_ref[head:head + 1, :]
                m_new = jnp.maximum(m_old, mx_ref[slot:slot + 1, i * tq:(i + 1) * tq])
                alphas.append(jnp.exp2(m_old - m_new))
                m_ref[head:head + 1, :] = m_new
                x = st_ref[head, row0:row0 + SLC_CHUNK, :]
                p_ref[head0 // 2, 0:SLC_CHUNK, i * tq:(i + 1) * tq] = jnp.exp2(x - m_new).astype(BF16)
            pv = _dot(vst_ref[0, g, :, pl.ds(k0, SLC_CHUNK)], p_ref[head0 // 2, 0:SLC_CHUNK, :])
            acc_ref[head0 // 2] = acc_ref[head0 // 2] * jnp.concatenate(alphas, axis=1) + pv

    last = jnp.maximum(n_items - 1, 0)
    stage_chunk(0, 0)

    def two_chunks(j, carry):
        stage_chunk(jnp.minimum(2 * j + 1, last), SLC_CHUNK)
        consume_chunk(2 * j, 0)

        @pl.when(2 * j + 1 < n_items)
        def _():
            stage_chunk(jnp.minimum(2 * j + 2, last), 0)
            consume_chunk(2 * j + 1, SLC_CHUNK)
        return carry

    lax.fori_loop(0, (n_items + 1) // 2, two_chunks, 0)
    for g, head0 in pairs:
        finish_pair(head0, acc_ref[head0 // 2], 1, False)

    for hh in range(HEADS_PER_GROUP):
        o_ref[:, hh * LANES:(hh + 1) * LANES] = out_ref[hh].T.astype(BF16)


def _overlap_t(seq):
    n_cmp = (seq - CMP_LEN) // CMP_STRIDE + 1
    n_sel = seq // SEL_LEN
    cs = np.arange(n_cmp)[:, None] * CMP_STRIDE
    ss = np.arange(n_sel)[None, :] * SEL_LEN
    ov = np.clip(np.minimum(cs + CMP_LEN, ss + SEL_LEN) - np.maximum(cs, ss), 0, None) / CMP_LEN
    out = np.zeros((n_sel, seq // CMP_STRIDE), np.float32)
    out[:, :n_cmp] = ov.T
    return jnp.asarray(out, dtype=BF16)


def _nsa(q, kc, vct, ks, vst, kw, vwt, gates, *, tq=128):
    batch, _, seq, _ = q.shape
    n_cmp = kc.shape[2]
    n_sel = seq // SEL_LEN
    assert seq % SLC_CHUNK == 0 and seq >= WINDOW + tq and SLC_CHUNK % tq == 0 and tq == LANES
    assert n_sel <= 2 * SEL_LEN
    ovt = jnp.broadcast_to(_overlap_t(seq), (batch, N_GROUPS, n_sel, n_cmp))
    vct = jnp.concatenate([vct, ovt], axis=2)
    keys = lambda n: pl.BlockSpec((1, N_GROUPS, n, LANES), lambda b, i: (b, 0, 0, 0))
    vals = lambda n: pl.BlockSpec((1, N_GROUPS, V_ROWS, n), lambda b, i: (b, 0, 0, 0))
    tiles = seq // tq
    s_rows = max(n_cmp, WINDOW + tq + HEAD_BLOCKS * SEL_LEN, 2 * SLC_CHUNK)
    return pl.pallas_call(
        functools.partial(_nsa_kernel, tq=tq, seq=seq),
        out_shape=jax.ShapeDtypeStruct((batch * seq, HEADS_PER_GROUP * LANES), BF16),
        grid=(batch, tiles),
        in_specs=[
            pl.BlockSpec((1, N_HEADS, tq, LANES), lambda b, i: (b, 0, i, 0)),
            keys(n_cmp), pl.BlockSpec((1, N_GROUPS, V_ROWS + n_sel, n_cmp), lambda b, i: (b, 0, 0, 0)),
            keys(seq), vals(seq), keys(seq), vals(seq),
            pl.BlockSpec((tq, LANES), lambda b, i: (b * tiles + i, 0)),
        ],
        out_specs=pl.BlockSpec((tq, HEADS_PER_GROUP * LANES), lambda b, i: (b * tiles + i, 0)),
        scratch_shapes=[
            pltpu.VMEM((N_HEADS, s_rows, tq), F32),
            pltpu.VMEM((N_HEADS // 2, s_rows, 2 * tq), BF16),
            pltpu.VMEM((N_HEADS, 2 * tq), F32),
            pltpu.VMEM((LANES, tq), F32),
            pltpu.VMEM((N_GROUPS, n_sel, tq), F32),
            pltpu.SMEM((seq // SLC_CHUNK,), jnp.int32),
            pltpu.VMEM((2 * N_HEADS, tq), F32),
            pltpu.VMEM((N_HEADS // 2, V_ROWS, 2 * tq), F32),
            pltpu.VMEM((HEADS_PER_GROUP, N_GROUPS * HEAD_DIM, tq), F32),
        ],
        compiler_params=_params("parallel", "arbitrary"),
        name="nsa",
    )(q, kc, vct, ks, vst, kw, vwt, gates)


HALO = max(POOL_WINDOWS)
assert all(w & (w - 1) == 0 for w in POOL_WINDOWS)


def _merge_kernel(x_ref, u_ref, halo_ref, gm_ref, on_ref, pw_ref, ps_ref, wbp_ref, wbn_ref, wo_ref,
                  o_ref, ext_ref, *, tm, seq):
    d = x_ref.shape[1]
    b = _dot(on_ref[...], wbn_ref[...])
    pos0 = (pl.program_id(0) * tm) % seq
    ext_ref[0:HALO, :] = jnp.where(pos0 == 0, 0.0, halo_ref[...])
    ext_ref[HALO:HALO + tm, :] = u_ref[...]
    pos = (pos0 + lax.broadcasted_iota(jnp.int32, (tm, POOL_GROUP_DIM), 0)).astype(F32)
    mixed = []
    for gi, w in enumerate(POOL_WINDOWS):
        cols = slice(gi * POOL_GROUP_DIM, (gi + 1) * POOL_GROUP_DIM)
        run = ext_ref[:, cols]
        for k in range(w.bit_length() - 1):
            run = run + pltpu.roll(run, 1 << k, 0)
        u = ext_ref[HALO:HALO + tm, cols]
        delta = run[HALO:HALO + tm] / jnp.minimum(pos + 1.0, float(w)) - u
        mixed.append(_dot(delta.astype(BF16), pw_ref[gi]) * ps_ref[:, cols])
    mixed = jnp.concatenate(mixed, axis=-1).astype(BF16)
    a = _dot(mixed, wbp_ref[...])
    merged = gm_ref[:, 0:d] * a + gm_ref[:, d:2 * d] * b
    o_ref[...] = x_ref[...] + _dot(merged.astype(BF16), wo_ref[...])


def _merge(x, u, gm, o_nsa, pool_w, pool_scale, w_bp, w_bn, w_out, seq, *, tm=512):
    n, d = x.shape
    row = lambda i: (i, 0)
    full = lambda a: pl.BlockSpec(a.shape, lambda i: (0,) * a.ndim, pipeline_mode=pl.Buffered(1))
    halo_blocks = tm // HALO
    return pl.pallas_call(
        functools.partial(_merge_kernel, tm=tm, seq=seq),
        out_shape=jax.ShapeDtypeStruct((n, d), F32),
        grid=(n // tm,),
        in_specs=[
            pl.BlockSpec((tm, d), row),
            pl.BlockSpec((tm, POOL_COLS), row),
            pl.BlockSpec((HALO, POOL_COLS), lambda i: (jnp.maximum(i * halo_blocks - 1, 0), 0)),
            pl.BlockSpec((tm, 2 * d), row),
            pl.BlockSpec((tm, d), row),
            full(pool_w), full(pool_scale), full(w_bp), full(w_bn), full(w_out),
        ],
        out_specs=pl.BlockSpec((tm, d), row),
        scratch_shapes=[pltpu.VMEM((HALO + tm, POOL_COLS), F32)],
        compiler_params=_params("parallel"),
        name="merge",
    )(x, u, u, gm, o_nsa, pool_w, pool_scale, w_bp, w_bn, w_out)


def kernel(x, ffn1_norm, ffn1_w_gate, ffn1_w_up, ffn1_w_down, mix_norm, w_in, cmp_pos, cmp_k_w1, cmp_k_w2, cmp_v_w1, cmp_v_w2, pool_w, pool_scale, w_branch_pool, w_branch_nsa, w_out, ffn2_norm, ffn2_w_gate, ffn2_w_up, ffn2_w_down, final_norm):
    batch, seq, d = x.shape
    depth = w_in.shape[0]
    xf = x.reshape(batch * seq, d)
    bf = lambda a: a.astype(BF16)
    row = lambda a: a.reshape(1, -1)
    for l in range(depth):
        xf = _ffn(xf, row(ffn1_norm[l]), bf(ffn1_w_gate[l]), bf(ffn1_w_up[l]), bf(ffn1_w_down[l]),
                  row(final_norm), final_norm=False)

        q, kc, vc, ks, kw, vs, vw, gates, u, gm = _proj(xf, row(mix_norm[l]), _pack_w_in(w_in[l], d), batch, seq)
        k_cmp, v_cmp = _compress(kc, vc, cmp_pos[l], cmp_k_w1[l], cmp_k_w2[l], cmp_v_w1[l], cmp_v_w2[l], batch, seq)
        o_nsa = _nsa(q, k_cmp, v_cmp, ks, vs, kw, vw, gates)

        w_bn = w_branch_nsa[l].reshape(N_GROUPS, HEADS_PER_GROUP, HEAD_DIM, d).transpose(1, 0, 2, 3).reshape(-1, d)
        xf = _merge(xf, u, gm, o_nsa, bf(pool_w[l]), row(pool_scale[l]), bf(w_branch_pool[l]), bf(w_bn),
                    bf(w_out[l]), seq)

        xf = _ffn(xf, row(ffn2_norm[l]), bf(ffn2_w_gate[l]), bf(ffn2_w_up[l]), bf(ffn2_w_down[l]),
                  row(final_norm), final_norm=(l == depth - 1))
    return xf.reshape(batch, seq, d)
```

```python
import functools

import jax
import jax.numpy as jnp
import numpy as np
from jax import lax
from jax.experimental import pallas as pl
from jax.experimental.pallas import tpu as pltpu

N_HEADS = 16
N_GROUPS = 2
HEADS_PER_GROUP = N_HEADS // N_GROUPS
HEAD_DIM = 64
CMP_LEN = 32
CMP_STRIDE = 16
SEL_LEN = 64
N_SELECT = 16
WINDOW = 512
POOL_WINDOWS = (2, 4, 8, 16)
POOL_GROUP_DIM = 128
RMS_EPS = 1e-6
ALIBI_MAX_BIAS = 8.0

LANES = 128
NEG_BIG = -1e30
TOPK_BIG = 1e30
N_FORCED = 3
TOPK_TAKEN = -float(2 ** 101)
VMEM_LIMIT = 48 * 1024 * 1024

BF16 = jnp.bfloat16
F32 = jnp.float32


def _dot(a, b):
    return jnp.dot(a, b, preferred_element_type=F32)


def _dot_nt(a, b):
    return lax.dot_general(a, b, (((1,), (1,)), ((), ())), preferred_element_type=F32)


def _rms(x, g):
    return x * lax.rsqrt(jnp.mean(x * x, axis=-1, keepdims=True) + RMS_EPS) * g


def _params(*sem):
    return pltpu.CompilerParams(dimension_semantics=sem, vmem_limit_bytes=VMEM_LIMIT)


def _ffn_kernel(x_ref, g_ref, wg_ref, wu_ref, wd_ref, fin_ref, o_ref, *, final_norm):
    x = x_ref[...]
    xn = _rms(x, g_ref[...]).astype(BF16)
    gate = _dot(xn, wg_ref[...])
    up = _dot(xn, wu_ref[...])
    act = (gate * jax.nn.sigmoid(gate)) * up
    y = x + 0.5 * _dot(act.astype(BF16), wd_ref[...])
    if final_norm:
        y = _rms(y, fin_ref[...])
    o_ref[...] = y


def _ffn(x, norm_g, wg, wu, wd, fin_g, *, final_norm, tm=512):
    n, d = x.shape
    row = lambda i: (i, 0)
    resident = lambda a: pl.BlockSpec(a.shape, lambda i: (0, 0), pipeline_mode=pl.Buffered(1))
    return pl.pallas_call(
        functools.partial(_ffn_kernel, final_norm=final_norm),
        out_shape=jax.ShapeDtypeStruct((n, d), F32),
        grid=(n // tm,),
        in_specs=[pl.BlockSpec((tm, d), row), resident(norm_g), resident(wg), resident(wu), resident(wd),
                  resident(fin_g)],
        out_specs=pl.BlockSpec((tm, d), row),
        compiler_params=_params("parallel"),
        name="ffn",
    )(x, norm_g, wg, wu, wd, fin_g)


LOG2E = 1.4426950408889634
N_FEAT = 6


def _bf16_pieces(x):
    x = np.asarray(x, np.float32)
    s1 = x.astype(BF16).astype(np.float32)
    s2 = (x - s1).astype(BF16).astype(np.float32)
    s3 = (x - s1 - s2).astype(BF16).astype(np.float32)
    return s1, s2, s3


def _query_features():
    slopes = np.float32(2.0) ** (-ALIBI_MAX_BIAS * np.arange(1, N_HEADS + 1, dtype=np.float32) / N_HEADS)
    s1, s2, s3 = _bf16_pieces(slopes * np.float32(LOG2E))
    feat = np.zeros((N_HEADS, LANES), np.float32)
    feat[:, HEAD_DIM:HEAD_DIM + N_FEAT] = np.stack([s1, s2, s3, SEL_LEN * s1, SEL_LEN * s2, SEL_LEN * s3], axis=1)
    return jnp.asarray(feat)


def _key_features(pos, width, offset):
    pos = np.asarray(pos)
    a, b = (pos // SEL_LEN).astype(np.float32), (pos % SEL_LEN).astype(np.float32)
    feat = np.zeros((len(pos), width), np.float32)
    feat[:, offset:offset + N_FEAT] = np.stack([b, b, b, a, a, a], axis=1)
    return jnp.asarray(feat)


Q_COLS = N_HEADS * HEAD_DIM
CMP_OFF = Q_COLS
KEY_OFF = CMP_OFF + 2 * LANES
VAL_OFF = KEY_OFF + 2 * LANES
GATE_OFF = VAL_OFF + 2 * LANES
POOL_OFF = GATE_OFF + LANES
POOL_COLS = len(POOL_WINDOWS) * POOL_GROUP_DIM
MERGE_OFF = POOL_OFF + POOL_COLS


V_ROWS = HEAD_DIM + 16


def _store_values_t(v_ref, v):
    vt = v.T
    for g in range(N_GROUPS):
        v_ref[0, g, 0:HEAD_DIM, :] = vt[g * HEAD_DIM:(g + 1) * HEAD_DIM, :].astype(BF16)
        v_ref[0, g, HEAD_DIM:V_ROWS, :] = jnp.ones((V_ROWS - HEAD_DIM, v.shape[0]), BF16)


def _proj_kernel(x_ref, g_ref, w_ref, qf_ref, kf_ref, q_ref, kc_ref, vc_ref, ks_ref, kw_ref, vs_ref, vw_ref,
                 gate_ref, u_ref, gm_ref):
    d = x_ref.shape[1]
    hn = _rms(x_ref[...], g_ref[...]).astype(BF16)
    q_scale = HEAD_DIM ** -0.5 * LOG2E
    low = lax.broadcasted_iota(jnp.int32, (x_ref.shape[0], LANES), 1) < HEAD_DIM
    for j in range(N_HEADS // 4):
        q4 = _dot(hn, w_ref[:, j * 2 * LANES:(j + 1) * 2 * LANES]) * q_scale
        for b in range(2):
            both = q4[:, b * LANES:(b + 1) * LANES]
            h = 4 * j + 2 * b
            q_ref[0, h] = (jnp.where(low, both, 0.0) + qf_ref[h:h + 1, :]).astype(BF16)
            q_ref[0, h + 1] = (jnp.where(low, pltpu.roll(both, HEAD_DIM, 1), 0.0) + qf_ref[h + 1:h + 2, :]).astype(BF16)
    cmp_in = _dot(hn, w_ref[:, CMP_OFF:CMP_OFF + 2 * LANES])
    kc_ref[...] = cmp_in[:, 0:LANES]
    vc_ref[...] = cmp_in[:, LANES:2 * LANES]
    keys = _dot(hn, w_ref[:, KEY_OFF:KEY_OFF + 2 * LANES])
    kf = kf_ref[...]
    for i, k_ref in enumerate((ks_ref, kw_ref)):
        both = keys[:, i * LANES:(i + 1) * LANES]
        k_ref[0, 0] = (jnp.where(low, both, 0.0) + kf).astype(BF16)
        k_ref[0, 1] = (jnp.where(low, pltpu.roll(both, HEAD_DIM, 1), 0.0) + kf).astype(BF16)
    vals = _dot(hn, w_ref[:, VAL_OFF:VAL_OFF + 2 * LANES])
    for i, v_ref in enumerate((vs_ref, vw_ref)):
        _store_values_t(v_ref, vals[:, i * LANES:(i + 1) * LANES])
    gate_ref[...] = jax.nn.sigmoid(_dot(hn, w_ref[:, GATE_OFF:GATE_OFF + LANES]))
    u_ref[...] = _dot(hn, w_ref[:, POOL_OFF:POOL_OFF + POOL_COLS])
    gm_ref[...] = jax.nn.sigmoid(_dot(hn, w_ref[:, MERGE_OFF:MERGE_OFF + 2 * d]))


def _pack_w_in(w_in, d):
    w_in = w_in.astype(BF16)
    qw = N_HEADS * HEAD_DIM
    kvw = N_GROUPS * HEAD_DIM

    kc, vc, ks, vs, kw, vw = (w_in[:, qw + i * kvw:qw + (i + 1) * kvw] for i in range(6))
    off = qw + 6 * kvw
    n_g = 3 * N_HEADS
    w_g = jnp.pad(w_in[:, off:off + n_g], ((0, 0), (0, LANES - n_g)))
    w_rest = w_in[:, off + n_g:]
    return jnp.concatenate([w_in[:, :qw], kc, vc, ks, kw,
                            vs, vw, w_g, w_rest], axis=1)


def _proj(x, norm_g, w_packed, batch, seq, *, tm=512):
    n, d = x.shape
    tiles_per_seq = seq // tm
    row = lambda i: (i, 0)
    const = lambda i: (0, 0)
    per_group = lambda i: (i // tiles_per_seq, 0, i % tiles_per_seq, 0)
    qf = _query_features()
    kf = _key_features(np.arange(seq), LANES, HEAD_DIM)
    flat = lambda width, dtype: (jax.ShapeDtypeStruct((n, width), dtype), pl.BlockSpec((tm, width), row))
    grouped = lambda count: (jax.ShapeDtypeStruct((batch, count, seq, LANES), BF16),
                             pl.BlockSpec((1, count, tm, LANES), per_group))
    values_t = (jax.ShapeDtypeStruct((batch, N_GROUPS, V_ROWS, seq), BF16),
                pl.BlockSpec((1, N_GROUPS, V_ROWS, tm), lambda i: (i // tiles_per_seq, 0, 0, i % tiles_per_seq)))
    outs = [grouped(N_HEADS), flat(LANES, F32), flat(LANES, F32), grouped(N_GROUPS), grouped(N_GROUPS),
            values_t, values_t, flat(LANES, F32), flat(POOL_COLS, F32), flat(2 * d, F32)]
    return pl.pallas_call(
        _proj_kernel,
        out_shape=[o[0] for o in outs],
        grid=(n // tm,),
        in_specs=[pl.BlockSpec((tm, d), row), pl.BlockSpec((1, d), const),
                  pl.BlockSpec(w_packed.shape, const, pipeline_mode=pl.Buffered(1)),
                  pl.BlockSpec(qf.shape, const), pl.BlockSpec((tm, LANES), lambda i: (i % tiles_per_seq, 0))],
        out_specs=[o[1] for o in outs],
        compiler_params=_params("parallel"),
        name="proj",
    )(x, norm_g, w_packed, qf, kf)


def _compress_kernel(k_ref, v_ref, pos_ref, wk1_ref, wk2_ref, wv1_ref, wv2_ref, feat_ref, ko_ref, vo_ref):
    n_chunk = k_ref.shape[0] // CMP_STRIDE
    hidden = wk2_ref.shape[0]

    def compress(x_ref, w1_ref, w2_ref):
        first = second = None
        for r in range(CMP_STRIDE):
            rows = x_ref[pl.ds(r, n_chunk, stride=CMP_STRIDE), :]
            a = _dot((rows + pos_ref[r:r + 1, :]).astype(BF16), w1_ref[r])
            b = _dot((rows + pos_ref[CMP_STRIDE + r:CMP_STRIDE + r + 1, :]).astype(BF16), w1_ref[CMP_STRIDE + r])
            first = a if first is None else first + a
            second = b if second is None else second + b
        act = jax.nn.gelu(first + pltpu.roll(second, n_chunk - 1, 0)).astype(BF16)
        return [_dot(act[:, g * hidden:(g + 1) * hidden], w2_ref[...]) for g in range(N_GROUPS)]

    for g, k_cmp in enumerate(compress(k_ref, wk1_ref, wk2_ref)):
        ko_ref[0, g] = jnp.concatenate([k_cmp, feat_ref[...]], axis=-1).astype(BF16)
    _store_values_t(vo_ref, jnp.concatenate(compress(v_ref, wv1_ref, wv2_ref), axis=-1))


def _block_diag_w1(w1):
    w = w1.reshape(CMP_LEN, HEAD_DIM, -1)
    z = jnp.zeros_like(w)
    return jnp.concatenate([jnp.concatenate([w, z], axis=2), jnp.concatenate([z, w], axis=2)], axis=1).astype(BF16)


def _compress(kc, vc, pos, wk1, wk2, wv1, wv2, batch, seq):
    n_chunk = seq // CMP_STRIDE
    rows = pl.BlockSpec((seq, LANES), lambda b: (b, 0))
    full = lambda a: pl.BlockSpec(a.shape, lambda b: (0,) * a.ndim, pipeline_mode=pl.Buffered(1))
    pos2 = jnp.concatenate([pos] * N_GROUPS, axis=1)
    feat = _key_features(np.arange(n_chunk) * CMP_STRIDE + CMP_LEN - 1, HEAD_DIM, 0)
    args = (pos2, _block_diag_w1(wk1), wk2.astype(BF16), _block_diag_w1(wv1), wv2.astype(BF16), feat)
    return pl.pallas_call(
        _compress_kernel,
        out_shape=[jax.ShapeDtypeStruct((batch, N_GROUPS, n_chunk, LANES), BF16),
                   jax.ShapeDtypeStruct((batch, N_GROUPS, V_ROWS, n_chunk), BF16)],
        grid=(batch,),
        in_specs=[rows, rows] + [full(a) for a in args],
        out_specs=[pl.BlockSpec((1, N_GROUPS, n_chunk, LANES), lambda b: (b, 0, 0, 0)),
                   pl.BlockSpec((1, N_GROUPS, V_ROWS, n_chunk), lambda b: (b, 0, 0, 0))],
        compiler_params=_params("parallel"),
        name="compress",
    )(kc, vc, *args)


SLC_CHUNK = 256
BLOCKS_PER_CHUNK = SLC_CHUNK // SEL_LEN
HEAD_BLOCKS = 2


def _nsa_kernel(q_ref, kc_ref, vct_ref, ks_ref, vst_ref, kw_ref, vwt_ref, gate_ref, o_ref,
                st_ref, p_ref, mx_ref, gt_ref, selt_ref, list_ref, m_ref, acc_ref, out_ref, *, tq, seq):
    n_sel = seq // SEL_LEN
    n_chunks = seq // SLC_CHUNK
    n_cmp = kc_ref.shape[2]
    n_win = WINDOW + tq
    q0 = pl.program_id(1) * tq
    gt_ref[...] = gate_ref[...].T
    pairs = [(g, g * HEADS_PER_GROUP + 2 * j) for g in range(N_GROUPS) for j in range(HEADS_PER_GROUP // 2)]

    def gate_row(head, branch):
        c = 3 * head + branch
        return gt_ref[c:c + 1, :]

    def out_rows(head):
        g, hh = divmod(head, HEADS_PER_GROUP)
        return hh, slice(g * HEAD_DIM, (g + 1) * HEAD_DIM)

    def stage_scores(keys, n_keys, row0=0):
        for g, head0 in pairs:
            q_pair = q_ref[0, head0:head0 + 2].reshape(2 * tq, LANES)
            scores = _dot_nt(keys(g), q_pair)
            for i in range(2):
                st_ref[head0 + i, row0:row0 + n_keys, :] = scores[:, i * tq:(i + 1) * tq]

    def finish_pair(head0, pv, branch, first, ok=None):
        invs = []
        for i in range(2):
            cols = slice(i * tq, (i + 1) * tq)
            inv = 1.0 / pv[HEAD_DIM:HEAD_DIM + 1, cols]
            if ok is not None:
                inv = jnp.where(ok, inv, 0.0)
            hh, rows = out_rows(head0 + i)
            contrib = pv[0:HEAD_DIM, cols] * (inv * gate_row(head0 + i, branch))
            out_ref[hh, rows, :] = contrib if first else out_ref[hh, rows, :] + contrib
            invs.append(inv)
        return invs

    t_c = q0 + lax.broadcasted_iota(jnp.int32, (n_cmp, tq), 1)
    end_c = lax.broadcasted_iota(jnp.int32, (n_cmp, tq), 0) * CMP_STRIDE + (CMP_LEN - 1)
    bias_c = jnp.where(t_c >= end_c, 0.0, NEG_BIG)
    has_cmp = (q0 + lax.broadcasted_iota(jnp.int32, (1, tq), 1)) >= CMP_LEN - 1
    inv_c = [None] * N_HEADS
    raw = [None] * (N_HEADS // 2)

    def cmp_pair(g, head0):
        for i in range(2):
            x = st_ref[head0 + i, 0:n_cmp, :] + bias_c
            p_ref[head0 // 2, 0:n_cmp, i * tq:(i + 1) * tq] = jnp.exp2(x - jnp.max(x, axis=0, keepdims=True)).astype(BF16)
        both = _dot(vct_ref[0, g], p_ref[head0 // 2, 0:n_cmp, :])
        raw[head0 // 2] = both[V_ROWS:, :]
        inv_c[head0], inv_c[head0 + 1] = finish_pair(head0, both[0:V_ROWS, :], 0, True, has_cmp)

    stage_scores(lambda g: kc_ref[0, g], n_cmp)
    for g, head0 in pairs:
        cmp_pair(g, head0)

    w0 = pl.multiple_of(jnp.maximum(q0 - WINDOW, 0), tq)
    stage_scores(lambda g: kw_ref[0, g, pl.ds(w0, n_win), :], n_win)
    head_keys = HEAD_BLOCKS * SEL_LEN
    stage_scores(lambda g: ks_ref[0, g, 0:head_keys, :], head_keys, n_win)

    dist_w = (q0 + lax.broadcasted_iota(jnp.int32, (n_win, tq), 1)) - (w0 + lax.broadcasted_iota(jnp.int32, (n_win, tq), 0))
    bias_w = jnp.where((dist_w >= 0) & (dist_w < WINDOW), 0.0, NEG_BIG)

    def win_pair(g, head0):
        for i in range(2):
            x = st_ref[head0 + i, 0:n_win, :] + bias_w
            p_ref[head0 // 2, 0:n_win, i * tq:(i + 1) * tq] = jnp.exp2(x - jnp.max(x, axis=0, keepdims=True)).astype(BF16)
        pv = _dot(vwt_ref[0, g, :, pl.ds(w0, n_win)], p_ref[head0 // 2, 0:n_win, :])
        finish_pair(head0, pv, 2, False)

    for g, head0 in pairs:
        win_pair(g, head0)

    blk = lax.broadcasted_iota(jnp.int32, (n_sel, tq), 0)
    cur = (q0 + lax.broadcasted_iota(jnp.int32, (n_sel, tq), 1)) // SEL_LEN
    valid = blk <= cur
    forced = valid & ((blk == 0) | (blk == cur) | (blk == cur - 1))
    blk_f = blk.astype(F32)
    score = []
    for g in range(N_GROUPS):
        total = None
        for head in range(g * HEADS_PER_GROUP, (g + 1) * HEADS_PER_GROUP):
            part = raw[head // 2][:, (head % 2) * tq:(head % 2 + 1) * tq] * inv_c[head]
            total = part if total is None else total + part
        score.append(jnp.where(forced, TOPK_TAKEN, jnp.where(valid, total, -TOPK_BIG)))

    for _ in range(min(N_SELECT, n_sel) - N_FORCED):
        for g in range(N_GROUPS):
            best = jnp.max(score[g], axis=0, keepdims=True)
            pick = jnp.min(jnp.where(score[g] == best, blk_f, float(n_sel)), axis=0, keepdims=True)
            score[g] = jnp.where(blk_f == pick, TOPK_TAKEN, score[g])
    sel = [jnp.where(valid & (score[g] == TOPK_TAKEN), 1.0, 0.0) for g in range(N_GROUPS)]
    for g in range(N_GROUPS):
        selt_ref[g] = jnp.where(blk >= HEAD_BLOCKS, sel[g], 0.0)
    sel_any = jnp.where(blk >= HEAD_BLOCKS, sel[0] + sel[1], 0.0)
    n_items = jnp.int32(0)
    for c in range(n_chunks):
        any_sel = jnp.max(sel_any[c * BLOCKS_PER_CHUNK:(c + 1) * BLOCKS_PER_CHUNK, :])
        list_ref[n_items] = c
        n_items = n_items + (any_sel > 0.5).astype(jnp.int32)

    causal_h = (q0 + lax.broadcasted_iota(jnp.int32, (head_keys, tq), 1)) >= lax.broadcasted_iota(jnp.int32, (head_keys, tq), 0)
    head_rows = slice(n_win, n_win + head_keys)
    for g, head0 in pairs:
        if head0 % HEADS_PER_GROUP == 0:
            sel_keys = jnp.concatenate([jnp.broadcast_to(sel[g][i:i + 1, :], (SEL_LEN, tq)) for i in range(HEAD_BLOCKS)], axis=0)
            bias = jnp.where(causal_h & (sel_keys > 0.5), 0.0, NEG_BIG)
        for i in range(2):
            x = st_ref[head0 + i, head_rows, :] + bias
            m_new = jnp.max(x, axis=0, keepdims=True)
            m_ref[head0 + i:head0 + i + 1, :] = m_new
            p_ref[head0 // 2, head_rows, i * tq:(i + 1) * tq] = jnp.exp2(x - m_new).astype(BF16)
    for g, head0 in pairs:
        acc_ref[head0 // 2] = _dot(vst_ref[0, g, :, 0:head_keys], p_ref[head0 // 2, head_rows, :])

    base = (lax.broadcasted_iota(jnp.int32, (SLC_CHUNK, tq), 1)
            - lax.broadcasted_iota(jnp.int32, (SLC_CHUNK, tq), 0))

    def stage_chunk(item, row0):
        c = list_ref[item]
        k0 = pl.multiple_of(c * SLC_CHUNK, SLC_CHUNK)
        causal = base >= k0 - q0
        for g, head0 in pairs:
            if head0 % HEADS_PER_GROUP == 0:
                sel_keys = jnp.concatenate(
                    [jnp.broadcast_to(selt_ref[g, pl.ds(c * BLOCKS_PER_CHUNK + i, 1), :], (SEL_LEN, tq))
                     for i in range(BLOCKS_PER_CHUNK)], axis=0)
                bias = jnp.where(causal & (sel_keys > 0.5), 0.0, NEG_BIG)
                bias2 = jnp.concatenate([bias, bias], axis=1)
            q_pair = q_ref[0, head0:head0 + 2].reshape(2 * tq, LANES)
            scores = _dot_nt(ks_ref[0, g, pl.ds(k0, SLC_CHUNK), :], q_pair) + bias2
            slot = (row0 // SLC_CHUNK) * (N_HEADS // 2) + head0 // 2
            mx_ref[slot:slot + 1, :] = jnp.max(scores, axis=0, keepdims=True)
            for i in range(2):
                st_ref[head0 + i, row0:row0 + SLC_CHUNK, :] = scores[:, i * tq:(i + 1) * tq]

    def consume_chunk(item, row0):
        k0 = pl.multiple_of(list_ref[item] * SLC_CHUNK, SLC_CHUNK)
        for g, head0 in pairs:
            alphas = []
            for i, head in enumerate((head0, head0 + 1)):
                slot = (row0 // SLC_CHUNK) * (N_HEADS // 2) + head0 // 2
                m_old = m_ref[head:head + 1, :]
                m_new = jnp.maximum(m_old, mx_ref[slot:slot + 1, i * tq:(i + 1) * tq])
                alphas.append(jnp.exp2(m_old - m_new))
                m_ref[head:head + 1, :] = m_new
                x = st_ref[head, row0:row0 + SLC_CHUNK, :]
                p_ref[head0 // 2, 0:SLC_CHUNK, i * tq:(i + 1) * tq] = jnp.exp2(x - m_new).astype(BF16)
            pv = _dot(vst_ref[0, g, :, pl.ds(k0, SLC_CHUNK)], p_ref[head0 // 2, 0:SLC_CHUNK, :])
            acc_ref[head0 // 2] = acc_ref[head0 // 2] * jnp.concatenate(alphas, axis=1) + pv

    last = jnp.maximum(n_items - 1, 0)
    stage_chunk(0, 0)

    def two_chunks(j, carry):
        stage_chunk(jnp.minimum(2 * j + 1, last), SLC_CHUNK)
        consume_chunk(2 * j, 0)

        @pl.when(2 * j + 1 < n_items)
        def _():
            stage_chunk(jnp.minimum(2 * j + 2, last), 0)
            consume_chunk(2 * j + 1, SLC_CHUNK)
        return carry

    lax.fori_loop(0, (n_items + 1) // 2, two_chunks, 0)
    for g, head0 in pairs:
        finish_pair(head0, acc_ref[head0 // 2], 1, False)

    for hh in range(HEADS_PER_GROUP):
        o_ref[:, hh * LANES:(hh + 1) * LANES] = out_ref[hh].T.astype(BF16)


def _overlap_t(seq):
    n_cmp = (seq - CMP_LEN) // CMP_STRIDE + 1
    n_sel = seq // SEL_LEN
    cs = np.arange(n_cmp)[:, None] * CMP_STRIDE
    ss = np.arange(n_sel)[None, :] * SEL_LEN
    ov = np.clip(np.minimum(cs + CMP_LEN, ss + SEL_LEN) - np.maximum(cs, ss), 0, None) / CMP_LEN
    out = np.zeros((n_sel, seq // CMP_STRIDE), np.float32)
    out[:, :n_cmp] = ov.T
    return jnp.asarray(out, dtype=BF16)


def _nsa(q, kc, vct, ks, vst, kw, vwt, gates, *, tq=128):
    batch, _, seq, _ = q.shape
    n_cmp = kc.shape[2]
    n_sel = seq // SEL_LEN
    assert seq % SLC_CHUNK == 0 and seq >= WINDOW + tq and SLC_CHUNK % tq == 0 and tq == LANES
    assert n_sel <= 2 * SEL_LEN
    ovt = jnp.broadcast_to(_overlap_t(seq), (batch, N_GROUPS, n_sel, n_cmp))
    vct = jnp.concatenate([vct, ovt], axis=2)
    keys = lambda n: pl.BlockSpec((1, N_GROUPS, n, LANES), lambda b, i: (b, 0, 0, 0))
    vals = lambda n: pl.BlockSpec((1, N_GROUPS, V_ROWS, n), lambda b, i: (b, 0, 0, 0))
    tiles = seq // tq
    s_rows = max(n_cmp, WINDOW + tq + HEAD_BLOCKS * SEL_LEN, 2 * SLC_CHUNK)
    return pl.pallas_call(
        functools.partial(_nsa_kernel, tq=tq, seq=seq),
        out_shape=jax.ShapeDtypeStruct((batch * seq, HEADS_PER_GROUP * LANES), BF16),
        grid=(batch, tiles),
        in_specs=[
            pl.BlockSpec((1, N_HEADS, tq, LANES), lambda b, i: (b, 0, i, 0)),
            keys(n_cmp), pl.BlockSpec((1, N_GROUPS, V_ROWS + n_sel, n_cmp), lambda b, i: (b, 0, 0, 0)),
            keys(seq), vals(seq), keys(seq), vals(seq),
            pl.BlockSpec((tq, LANES), lambda b, i: (b * tiles + i, 0)),
        ],
        out_specs=pl.BlockSpec((tq, HEADS_PER_GROUP * LANES), lambda b, i: (b * tiles + i, 0)),
        scratch_shapes=[
            pltpu.VMEM((N_HEADS, s_rows + 8, tq), F32),
            pltpu.VMEM((N_HEADS // 2, s_rows, 2 * tq), BF16),
            pltpu.VMEM((N_HEADS, 2 * tq), F32),
            pltpu.VMEM((LANES, tq), F32),
            pltpu.VMEM((N_GROUPS, n_sel, tq), F32),
            pltpu.SMEM((seq // SLC_CHUNK,), jnp.int32),
            pltpu.VMEM((2 * N_HEADS, tq), F32),
            pltpu.VMEM((N_HEADS // 2, V_ROWS, 2 * tq), F32),
            pltpu.VMEM((HEADS_PER_GROUP, N_GROUPS * HEAD_DIM, tq), F32),
        ],
        compiler_params=_params("parallel", "arbitrary"),
        name="nsa",
    )(q, kc, vct, ks, vst, kw, vwt, gates)


HALO = max(POOL_WINDOWS)
assert all(w & (w - 1) == 0 for w in POOL_WINDOWS)


def _merge_kernel(x_ref, u_ref, halo_ref, gm_ref, on_ref, pw_ref, ps_ref, wbp_ref, wbn_ref, wo_ref,
                  o_ref, ext_ref, *, tm, seq):
    d = x_ref.shape[1]
    b = _dot(on_ref[...], wbn_ref[...])
    pos0 = (pl.program_id(0) * tm) % seq
    ext_ref[0:HALO, :] = jnp.where(pos0 == 0, 0.0, halo_ref[...])
    ext_ref[HALO:HALO + tm, :] = u_ref[...]
    pos = (pos0 + lax.broadcasted_iota(jnp.int32, (tm, POOL_GROUP_DIM), 0)).astype(F32)
    mixed = []
    for gi, w in enumerate(POOL_WINDOWS):
        cols = slice(gi * POOL_GROUP_DIM, (gi + 1) * POOL_GROUP_DIM)
        run = ext_ref[:, cols]
        for k in range(w.bit_length() - 1):
            run = run + pltpu.roll(run, 1 << k, 0)
        u = ext_ref[HALO:HALO + tm, cols]
        delta = run[HALO:HALO + tm] / jnp.minimum(pos + 1.0, float(w)) - u
        mixed.append(_dot(delta.astype(BF16), pw_ref[gi]) * ps_ref[:, cols])
    mixed = jnp.concatenate(mixed, axis=-1).astype(BF16)
    a = _dot(mixed, wbp_ref[...])
    merged = gm_ref[:, 0:d] * a + gm_ref[:, d:2 * d] * b
    o_ref[...] = x_ref[...] + _dot(merged.astype(BF16), wo_ref[...])


def _merge(x, u, gm, o_nsa, pool_w, pool_scale, w_bp, w_bn, w_out, seq, *, tm=512):
    n, d = x.shape
    row = lambda i: (i, 0)
    full = lambda a: pl.BlockSpec(a.shape, lambda i: (0,) * a.ndim, pipeline_mode=pl.Buffered(1))
    halo_blocks = tm // HALO
    return pl.pallas_call(
        functools.partial(_merge_kernel, tm=tm, seq=seq),
        out_shape=jax.ShapeDtypeStruct((n, d), F32),
        grid=(n // tm,),
        in_specs=[
            pl.BlockSpec((tm, d), row),
            pl.BlockSpec((tm, POOL_COLS), row),
            pl.BlockSpec((HALO, POOL_COLS), lambda i: (jnp.maximum(i * halo_blocks - 1, 0), 0)),
            pl.BlockSpec((tm, 2 * d), row),
            pl.BlockSpec((tm, d), row),
            full(pool_w), full(pool_scale), full(w_bp), full(w_bn), full(w_out),
        ],
        out_specs=pl.BlockSpec((tm, d), row),
        scratch_shapes=[pltpu.VMEM((HALO + tm, POOL_COLS), F32)],
        compiler_params=_params("parallel"),
        name="merge",
    )(x, u, u, gm, o_nsa, pool_w, pool_scale, w_bp, w_bn, w_out)


def kernel(x, ffn1_norm, ffn1_w_gate, ffn1_w_up, ffn1_w_down, mix_norm, w_in, cmp_pos, cmp_k_w1, cmp_k_w2, cmp_v_w1, cmp_v_w2, pool_w, pool_scale, w_branch_pool, w_branch_nsa, w_out, ffn2_norm, ffn2_w_gate, ffn2_w_up, ffn2_w_down, final_norm):
    batch, seq, d = x.shape
    depth = w_in.shape[0]
    xf = x.reshape(batch * seq, d)
    bf = lambda a: a.astype(BF16)
    row = lambda a: a.reshape(1, -1)
    for l in range(depth):
        xf = _ffn(xf, row(ffn1_norm[l]), bf(ffn1_w_gate[l]), bf(ffn1_w_up[l]), bf(ffn1_w_down[l]),
                  row(final_norm), final_norm=False)

        q, kc, vc, ks, kw, vs, vw, gates, u, gm = _proj(xf, row(mix_norm[l]), _pack_w_in(w_in[l], d), batch, seq)
        k_cmp, v_cmp = _compress(kc, vc, cmp_pos[l], cmp_k_w1[l], cmp_k_w2[l], cmp_v_w1[l], cmp_v_w2[l], batch, seq)
        o_nsa = _nsa(q, k_cmp, v_cmp, ks, vs, kw, vw, gates)

        w_bn = w_branch_nsa[l].reshape(N_GROUPS, HEADS_PER_GROUP, HEAD_DIM, d).transpose(1, 0, 2, 3).reshape(-1, d)
        xf = _merge(xf, u, gm, o_nsa, bf(pool_w[l]), row(pool_scale[l]), bf(w_branch_pool[l]), bf(w_bn),
                    bf(w_out[l]), seq)

        xf = _ffn(xf, row(ffn2_norm[l]), bf(ffn2_w_gate[l]), bf(ffn2_w_up[l]), bf(ffn2_w_down[l]),
                  row(final_norm), final_norm=(l == depth - 1))
    return xf.reshape(batch, seq, d)
```

```python
import functools

import jax
import jax.numpy as jnp
import numpy as np
from jax import lax
from jax.experimental import pallas as pl
from jax.experimental.pallas import tpu as pltpu

N_HEADS = 16
N_GROUPS = 2
HEADS_PER_GROUP = N_HEADS // N_GROUPS
HEAD_DIM = 64
CMP_LEN = 32
CMP_STRIDE = 16
SEL_LEN = 64
N_SELECT = 16
WINDOW = 512
POOL_WINDOWS = (2, 4, 8, 16)
POOL_GROUP_DIM = 128
RMS_EPS = 1e-6
ALIBI_MAX_BIAS = 8.0

LANES = 128
NEG_BIG = -1e30
TOPK_BIG = 1e30
N_FORCED = 3
TOPK_TAKEN = -float(2 ** 101)
VMEM_LIMIT = 48 * 1024 * 1024

BF16 = jnp.bfloat16
F32 = jnp.float32


def _dot(a, b):
    return jnp.dot(a, b, preferred_element_type=F32)


def _dot_nt(a, b):
    return lax.dot_general(a, b, (((1,), (1,)), ((), ())), preferred_element_type=F32)


def _rms(x, g):
    return x * lax.rsqrt(jnp.mean(x * x, axis=-1, keepdims=True) + RMS_EPS) * g


def _params(*sem):
    return pltpu.CompilerParams(dimension_semantics=sem, vmem_limit_bytes=VMEM_LIMIT)


def _ffn_kernel(x_ref, g_ref, wg_ref, wu_ref, wd_ref, fin_ref, o_ref, *, final_norm):
    x = x_ref[...]
    xn = _rms(x, g_ref[...]).astype(BF16)
    gate = _dot(xn, wg_ref[...])
    up = _dot(xn, wu_ref[...])
    act = (gate * jax.nn.sigmoid(gate)) * up
    y = x + 0.5 * _dot(act.astype(BF16), wd_ref[...])
    if final_norm:
        y = _rms(y, fin_ref[...])
    o_ref[...] = y


def _ffn(x, norm_g, wg, wu, wd, fin_g, *, final_norm, tm=512):
    n, d = x.shape
    row = lambda i: (i, 0)
    resident = lambda a: pl.BlockSpec(a.shape, lambda i: (0, 0), pipeline_mode=pl.Buffered(1))
    return pl.pallas_call(
        functools.partial(_ffn_kernel, final_norm=final_norm),
        out_shape=jax.ShapeDtypeStruct((n, d), F32),
        grid=(n // tm,),
        in_specs=[pl.BlockSpec((tm, d), row), resident(norm_g), resident(wg), resident(wu), resident(wd),
                  resident(fin_g)],
        out_specs=pl.BlockSpec((tm, d), row),
        compiler_params=_params("parallel"),
        name="ffn",
    )(x, norm_g, wg, wu, wd, fin_g)


LOG2E = 1.4426950408889634
N_FEAT = 6


def _bf16_pieces(x):
    x = np.asarray(x, np.float32)
    s1 = x.astype(BF16).astype(np.float32)
    s2 = (x - s1).astype(BF16).astype(np.float32)
    s3 = (x - s1 - s2).astype(BF16).astype(np.float32)
    return s1, s2, s3


def _query_features():
    slopes = np.float32(2.0) ** (-ALIBI_MAX_BIAS * np.arange(1, N_HEADS + 1, dtype=np.float32) / N_HEADS)
    s1, s2, s3 = _bf16_pieces(slopes * np.float32(LOG2E))
    feat = np.zeros((N_HEADS, LANES), np.float32)
    feat[:, HEAD_DIM:HEAD_DIM + N_FEAT] = np.stack([s1, s2, s3, SEL_LEN * s1, SEL_LEN * s2, SEL_LEN * s3], axis=1)
    return jnp.asarray(feat)


def _key_features(pos, width, offset):
    pos = np.asarray(pos)
    a, b = (pos // SEL_LEN).astype(np.float32), (pos % SEL_LEN).astype(np.float32)
    feat = np.zeros((len(pos), width), np.float32)
    feat[:, offset:offset + N_FEAT] = np.stack([b, b, b, a, a, a], axis=1)
    return jnp.asarray(feat)


Q_COLS = N_HEADS * HEAD_DIM
CMP_OFF = Q_COLS
KEY_OFF = CMP_OFF + 2 * LANES
VAL_OFF = KEY_OFF + 2 * LANES
GATE_OFF = VAL_OFF + 2 * LANES
POOL_OFF = GATE_OFF + LANES
POOL_COLS = len(POOL_WINDOWS) * POOL_GROUP_DIM


V_ROWS = HEAD_DIM + 16


def _store_values_t(v_ref, v):
    vt = v.T
    for g in range(N_GROUPS):
        v_ref[0, g, 0:HEAD_DIM, :] = vt[g * HEAD_DIM:(g + 1) * HEAD_DIM, :].astype(BF16)
        v_ref[0, g, HEAD_DIM:V_ROWS, :] = jnp.ones((V_ROWS - HEAD_DIM, v.shape[0]), BF16)


def _proj_kernel(x_ref, g_ref, w_ref, qf_ref, kf_ref, q_ref, kc_ref, vc_ref, ks_ref, kw_ref, vs_ref, vw_ref,
                 gate_ref, u_ref):
    hn =_rms(x_ref[...], g_ref[...]).astype(BF16)
    q_scale = HEAD_DIM ** -0.5 * LOG2E
    low = lax.broadcasted_iota(jnp.int32, (x_ref.shape[0], LANES), 1) < HEAD_DIM
    for j in range(N_HEADS // 4):
        q4 = _dot(hn, w_ref[:, j * 2 * LANES:(j + 1) * 2 * LANES]) * q_scale
        for b in range(2):
            both = q4[:, b * LANES:(b + 1) * LANES]
            h = 4 * j + 2 * b
            q_ref[0, h] = (jnp.where(low, both, 0.0) + qf_ref[h:h + 1, :]).astype(BF16)
            q_ref[0, h + 1] = (jnp.where(low, pltpu.roll(both, HEAD_DIM, 1), 0.0) + qf_ref[h + 1:h + 2, :]).astype(BF16)
    cmp_in = _dot(hn, w_ref[:, CMP_OFF:CMP_OFF + 2 * LANES])
    kc_ref[...] = cmp_in[:, 0:LANES]
    vc_ref[...] = cmp_in[:, LANES:2 * LANES]
    keys = _dot(hn, w_ref[:, KEY_OFF:KEY_OFF + 2 * LANES])
    kf = kf_ref[...]
    for i, k_ref in enumerate((ks_ref, kw_ref)):
        both = keys[:, i * LANES:(i + 1) * LANES]
        k_ref[0, 0] = (jnp.where(low, both, 0.0) + kf).astype(BF16)
        k_ref[0, 1] = (jnp.where(low, pltpu.roll(both, HEAD_DIM, 1), 0.0) + kf).astype(BF16)
    vals = _dot(hn, w_ref[:, VAL_OFF:VAL_OFF + 2 * LANES])
    for i, v_ref in enumerate((vs_ref, vw_ref)):
        _store_values_t(v_ref, vals[:, i * LANES:(i + 1) * LANES])
    gate_ref[...] = jax.nn.sigmoid(_dot(hn, w_ref[:, GATE_OFF:GATE_OFF + LANES]))
    u_ref[...] = _dot(hn, w_ref[:, POOL_OFF:POOL_OFF + POOL_COLS])


def _pack_w_in(w_in, d):
    w_in = w_in.astype(BF16)
    qw = N_HEADS * HEAD_DIM
    kvw = N_GROUPS * HEAD_DIM

    kc, vc, ks, vs, kw, vw = (w_in[:, qw + i * kvw:qw + (i + 1) * kvw] for i in range(6))
    off = qw + 6 * kvw
    n_g = 3 * N_HEADS
    w_g = jnp.pad(w_in[:, off:off + n_g], ((0, 0), (0, LANES - n_g)))
    w_pool = w_in[:, off + n_g:off + n_g + POOL_COLS]
    w_merge = w_in[:, off + n_g + POOL_COLS:]
    return jnp.concatenate([w_in[:, :qw], kc, vc, ks, kw, vs, vw, w_g, w_pool], axis=1), w_merge


def _proj(x, norm_g, w_packed, batch, seq, *, tm=512):
    n, d = x.shape
    tiles_per_seq = seq // tm
    row = lambda i: (i, 0)
    const = lambda i: (0, 0)
    per_group = lambda i: (i // tiles_per_seq, 0, i % tiles_per_seq, 0)
    qf = _query_features()
    kf = _key_features(np.arange(seq), LANES, HEAD_DIM)
    flat = lambda width, dtype: (jax.ShapeDtypeStruct((n, width), dtype), pl.BlockSpec((tm, width), row))
    grouped = lambda count: (jax.ShapeDtypeStruct((batch, count, seq, LANES), BF16),
                             pl.BlockSpec((1, count, tm, LANES), per_group))
    values_t = (jax.ShapeDtypeStruct((batch, N_GROUPS, V_ROWS, seq), BF16),
                pl.BlockSpec((1, N_GROUPS, V_ROWS, tm), lambda i: (i // tiles_per_seq, 0, 0, i % tiles_per_seq)))
    outs = [grouped(N_HEADS), flat(LANES, F32), flat(LANES, F32), grouped(N_GROUPS), grouped(N_GROUPS),
            values_t, values_t, flat(LANES, F32), flat(POOL_COLS, F32)]
    return pl.pallas_call(
        _proj_kernel,
        out_shape=[o[0] for o in outs],
        grid=(n // tm,),
        in_specs=[pl.BlockSpec((tm, d), row), pl.BlockSpec((1, d), const),
                  pl.BlockSpec(w_packed.shape, const, pipeline_mode=pl.Buffered(1)),
                  pl.BlockSpec(qf.shape, const), pl.BlockSpec((tm, LANES), lambda i: (i % tiles_per_seq, 0))],
        out_specs=[o[1] for o in outs],
        compiler_params=_params("parallel"),
        name="proj",
    )(x, norm_g, w_packed, qf, kf)


def _compress_kernel(k_ref, v_ref, pos_ref, wk1_ref, wk2_ref, wv1_ref, wv2_ref, feat_ref, ko_ref, vo_ref):
    n_chunk = k_ref.shape[0] // CMP_STRIDE
    hidden = wk2_ref.shape[0]

    def compress(x_ref, w1_ref, w2_ref):
        first = second = None
        for r in range(CMP_STRIDE):
            rows = x_ref[pl.ds(r, n_chunk, stride=CMP_STRIDE), :]
            a = _dot((rows + pos_ref[r:r + 1, :]).astype(BF16), w1_ref[r])
            b = _dot((rows + pos_ref[CMP_STRIDE + r:CMP_STRIDE + r + 1, :]).astype(BF16), w1_ref[CMP_STRIDE + r])
            first = a if first is None else first + a
            second = b if second is None else second + b
        act = jax.nn.gelu(first + pltpu.roll(second, n_chunk - 1, 0)).astype(BF16)
        return [_dot(act[:, g * hidden:(g + 1) * hidden], w2_ref[...]) for g in range(N_GROUPS)]

    for g, k_cmp in enumerate(compress(k_ref, wk1_ref, wk2_ref)):
        ko_ref[0, g] = jnp.concatenate([k_cmp, feat_ref[...]], axis=-1).astype(BF16)
    _store_values_t(vo_ref, jnp.concatenate(compress(v_ref, wv1_ref, wv2_ref), axis=-1))


def _block_diag_w1(w1):
    w = w1.reshape(CMP_LEN, HEAD_DIM, -1)
    z = jnp.zeros_like(w)
    return jnp.concatenate([jnp.concatenate([w, z], axis=2), jnp.concatenate([z, w], axis=2)], axis=1).astype(BF16)


def _compress(kc, vc, pos, wk1, wk2, wv1, wv2, batch, seq):
    n_chunk = seq // CMP_STRIDE
    rows = pl.BlockSpec((seq, LANES), lambda b: (b, 0))
    full = lambda a: pl.BlockSpec(a.shape, lambda b: (0,) * a.ndim, pipeline_mode=pl.Buffered(1))
    pos2 = jnp.concatenate([pos] * N_GROUPS, axis=1)
    feat = _key_features(np.arange(n_chunk) * CMP_STRIDE + CMP_LEN - 1, HEAD_DIM, 0)
    args = (pos2, _block_diag_w1(wk1), wk2.astype(BF16), _block_diag_w1(wv1), wv2.astype(BF16), feat)
    return pl.pallas_call(
        _compress_kernel,
        out_shape=[jax.ShapeDtypeStruct((batch, N_GROUPS, n_chunk, LANES), BF16),
                   jax.ShapeDtypeStruct((batch, N_GROUPS, V_ROWS, n_chunk), BF16)],
        grid=(batch,),
        in_specs=[rows, rows] + [full(a) for a in args],
        out_specs=[pl.BlockSpec((1, N_GROUPS, n_chunk, LANES), lambda b: (b, 0, 0, 0)),
                   pl.BlockSpec((1, N_GROUPS, V_ROWS, n_chunk), lambda b: (b, 0, 0, 0))],
        compiler_params=_params("parallel"),
        name="compress",
    )(kc, vc, *args)


SLC_CHUNK = 256
BLOCKS_PER_CHUNK = SLC_CHUNK // SEL_LEN
HEAD_BLOCKS = 2


def _nsa_kernel(q_ref, kc_ref, vct_ref, ks_ref, vst_ref, kw_ref, vwt_ref, gate_ref, o_ref,
                st_ref, p_ref, mx_ref, gt_ref, selt_ref, list_ref, m_ref, acc_ref, out_ref, *, tq, seq):
    n_sel = seq // SEL_LEN
    n_chunks = seq // SLC_CHUNK
    n_cmp = kc_ref.shape[2]
    n_win = WINDOW + tq
    q0 = pl.program_id(1) * tq
    gt_ref[...] = gate_ref[...].T
    pairs = [(g, g * HEADS_PER_GROUP + 2 * j) for g in range(N_GROUPS) for j in range(HEADS_PER_GROUP // 2)]

    def gate_row(head, branch):
        c = 3 * head + branch
        return gt_ref[c:c + 1, :]

    def out_rows(head):
        g, hh = divmod(head, HEADS_PER_GROUP)
        return hh, slice(g * HEAD_DIM, (g + 1) * HEAD_DIM)

    def stage_scores(keys, n_keys, row0=0):
        for g, head0 in pairs:
            q_pair = q_ref[0, head0:head0 + 2].reshape(2 * tq, LANES)
            scores = _dot_nt(keys(g), q_pair)
            for i in range(2):
                st_ref[head0 + i, row0:row0 + n_keys, :] = scores[:, i * tq:(i + 1) * tq]

    def finish_pair(head0, pv, branch, first, ok=None):
        invs = []
        for i in range(2):
            cols = slice(i * tq, (i + 1) * tq)
            inv = 1.0 / pv[HEAD_DIM:HEAD_DIM + 1, cols]
            if ok is not None:
                inv = jnp.where(ok, inv, 0.0)
            hh, rows = out_rows(head0 + i)
            contrib = pv[0:HEAD_DIM, cols] * (inv * gate_row(head0 + i, branch))
            out_ref[hh, rows, :] = contrib if first else out_ref[hh, rows, :] + contrib
            invs.append(inv)
        return invs

    t_c = q0 + lax.broadcasted_iota(jnp.int32, (n_cmp, tq), 1)
    end_c = lax.broadcasted_iota(jnp.int32, (n_cmp, tq), 0) * CMP_STRIDE + (CMP_LEN - 1)
    bias_c = jnp.where(t_c >= end_c, 0.0, NEG_BIG)
    has_cmp = (q0 + lax.broadcasted_iota(jnp.int32, (1, tq), 1)) >= CMP_LEN - 1
    inv_c = [None] * N_HEADS
    raw = [None] * (N_HEADS // 2)

    def cmp_pair(g, head0):
        for i in range(2):
            x = st_ref[head0 + i, 0:n_cmp, :] + bias_c
            p_ref[head0 // 2, 0:n_cmp, i * tq:(i + 1) * tq] = jnp.exp2(x - jnp.max(x, axis=0, keepdims=True)).astype(BF16)
        both = _dot(vct_ref[0, g], p_ref[head0 // 2, 0:n_cmp, :])
        raw[head0 // 2] = both[V_ROWS:, :]
        inv_c[head0], inv_c[head0 + 1] = finish_pair(head0, both[0:V_ROWS, :], 0, True, has_cmp)

    stage_scores(lambda g: kc_ref[0, g], n_cmp)
    for g, head0 in pairs:
        cmp_pair(g, head0)

    w0 = pl.multiple_of(jnp.maximum(q0 - WINDOW, 0), tq)
    stage_scores(lambda g: kw_ref[0, g, pl.ds(w0, n_win), :], n_win)
    head_keys = HEAD_BLOCKS * SEL_LEN
    stage_scores(lambda g: ks_ref[0, g, 0:head_keys, :], head_keys, n_win)

    dist_w = (q0 + lax.broadcasted_iota(jnp.int32, (n_win, tq), 1)) - (w0 + lax.broadcasted_iota(jnp.int32, (n_win, tq), 0))
    bias_w = jnp.where((dist_w >= 0) & (dist_w < WINDOW), 0.0, NEG_BIG)

    def win_pair(g, head0):
        for i in range(2):
            x = st_ref[head0 + i, 0:n_win, :] + bias_w
            p_ref[head0 // 2, 0:n_win, i * tq:(i + 1) * tq] = jnp.exp2(x - jnp.max(x, axis=0, keepdims=True)).astype(BF16)
        pv = _dot(vwt_ref[0, g, :, pl.ds(w0, n_win)], p_ref[head0 // 2, 0:n_win, :])
        finish_pair(head0, pv, 2, False)

    for g, head0 in pairs:
        win_pair(g, head0)

    blk = lax.broadcasted_iota(jnp.int32, (n_sel, tq), 0)
    cur = (q0 + lax.broadcasted_iota(jnp.int32, (n_sel, tq), 1)) // SEL_LEN
    valid = blk <= cur
    forced = valid & ((blk == 0) | (blk == cur) | (blk == cur - 1))
    blk_f = blk.astype(F32)
    score = []
    for g in range(N_GROUPS):
        total = None
        for head in range(g * HEADS_PER_GROUP, (g + 1) * HEADS_PER_GROUP):
            part = raw[head // 2][:, (head % 2) * tq:(head % 2 + 1) * tq] * inv_c[head]
            total = part if total is None else total + part
        score.append(jnp.where(forced, TOPK_TAKEN, jnp.where(valid, total, -TOPK_BIG)))

    for _ in range(min(N_SELECT, n_sel) - N_FORCED):
        for g in range(N_GROUPS):
            best = jnp.max(score[g], axis=0, keepdims=True)
            pick = jnp.min(jnp.where(score[g] == best, blk_f, float(n_sel)), axis=0, keepdims=True)
            score[g] = jnp.where(blk_f == pick, TOPK_TAKEN, score[g])
    sel = [jnp.where(valid & (score[g] == TOPK_TAKEN), 1.0, 0.0) for g in range(N_GROUPS)]
    for g in range(N_GROUPS):
        selt_ref[g] = jnp.where(blk >= HEAD_BLOCKS, sel[g], 0.0)
    sel_any = jnp.where(blk >= HEAD_BLOCKS, sel[0] + sel[1], 0.0)
    n_items = jnp.int32(0)
    for c in range(n_chunks):
        any_sel = jnp.max(sel_any[c * BLOCKS_PER_CHUNK:(c + 1) * BLOCKS_PER_CHUNK, :])
        list_ref[n_items] = c
        n_items = n_items + (any_sel > 0.5).astype(jnp.int32)

    causal_h = (q0 + lax.broadcasted_iota(jnp.int32, (head_keys, tq), 1)) >= lax.broadcasted_iota(jnp.int32, (head_keys, tq), 0)
    head_rows = slice(n_win, n_win + head_keys)
    for g, head0 in pairs:
        if head0 % HEADS_PER_GROUP == 0:
            sel_keys = jnp.concatenate([jnp.broadcast_to(sel[g][i:i + 1, :], (SEL_LEN, tq)) for i in range(HEAD_BLOCKS)], axis=0)
            bias = jnp.where(causal_h & (sel_keys > 0.5), 0.0, NEG_BIG)
        for i in range(2):
            x = st_ref[head0 + i, head_rows, :] + bias
            m_new = jnp.max(x, axis=0, keepdims=True)
            m_ref[head0 + i:head0 + i + 1, :] = m_new
            p_ref[head0 // 2, head_rows, i * tq:(i + 1) * tq] = jnp.exp2(x - m_new).astype(BF16)
    for g, head0 in pairs:
        acc_ref[head0 // 2] = _dot(vst_ref[0, g, :, 0:head_keys], p_ref[head0 // 2, head_rows, :])

    base = (lax.broadcasted_iota(jnp.int32, (SLC_CHUNK, tq), 1)
            - lax.broadcasted_iota(jnp.int32, (SLC_CHUNK, tq), 0))

    def stage_chunk(item, row0):
        c = list_ref[item]
        k0 = pl.multiple_of(c * SLC_CHUNK, SLC_CHUNK)
        causal = base >= k0 - q0
        for g, head0 in pairs:
            if head0 % HEADS_PER_GROUP == 0:
                sel_keys = jnp.concatenate(
                    [jnp.broadcast_to(selt_ref[g, pl.ds(c * BLOCKS_PER_CHUNK + i, 1), :], (SEL_LEN, tq))
                     for i in range(BLOCKS_PER_CHUNK)], axis=0)
                bias = jnp.where(causal & (sel_keys > 0.5), 0.0, NEG_BIG)
                bias2 = jnp.concatenate([bias, bias], axis=1)
            q_pair = q_ref[0, head0:head0 + 2].reshape(2 * tq, LANES)
            scores = _dot_nt(ks_ref[0, g, pl.ds(k0, SLC_CHUNK), :], q_pair) + bias2
            slot = (row0 // SLC_CHUNK) * (N_HEADS // 2) + head0 // 2
            mx_ref[slot:slot + 1, :] = jnp.max(scores, axis=0, keepdims=True)
            for i in range(2):
                st_ref[head0 + i, row0:row0 + SLC_CHUNK, :] = scores[:, i * tq:(i + 1) * tq]

    def consume_chunk(item, row0):
        k0 = pl.multiple_of(list_ref[item] * SLC_CHUNK, SLC_CHUNK)
        for g, head0 in pairs:
            alphas = []
            for i, head in enumerate((head0, head0 + 1)):
                slot = (row0 // SLC_CHUNK) * (N_HEADS // 2) + head0 // 2
                m_old = m_ref[head:head + 1, :]
                m_new = jnp.maximum(m_old, mx_ref[slot:slot + 1, i * tq:(i + 1) * tq])
                alphas.append(jnp.exp2(m_old - m_new))
                m_ref[head:head + 1, :] = m_new
                x = st_ref[head, row0:row0 + SLC_CHUNK, :]
                p_ref[head0 // 2, 0:SLC_CHUNK, i * tq:(i + 1) * tq] = jnp.exp2(x - m_new).astype(BF16)
            pv = _dot(vst_ref[0, g, :, pl.ds(k0, SLC_CHUNK)], p_ref[head0 // 2, 0:SLC_CHUNK, :])
            acc_ref[head0 // 2] = acc_ref[head0 // 2] * jnp.concatenate(alphas, axis=1) + pv

    last = jnp.maximum(n_items - 1, 0)
    stage_chunk(0, 0)

    def two_chunks(j, carry):
        stage_chunk(jnp.minimum(2 * j + 1, last), SLC_CHUNK)
        consume_chunk(2 * j, 0)

        @pl.when(2 * j + 1 < n_items)
        def _():
            stage_chunk(jnp.minimum(2 * j + 2, last), 0)
            consume_chunk(2 * j + 1, SLC_CHUNK)
        return carry

    lax.fori_loop(0, (n_items + 1) // 2, two_chunks, 0)
    for g, head0 in pairs:
        finish_pair(head0, acc_ref[head0 // 2], 1, False)

    for hh in range(HEADS_PER_GROUP):
        o_ref[:, hh * LANES:(hh + 1) * LANES] = out_ref[hh].T.astype(BF16)


def _overlap_t(seq):
    n_cmp = (seq - CMP_LEN) // CMP_STRIDE + 1
    n_sel = seq // SEL_LEN
    cs = np.arange(n_cmp)[:, None] * CMP_STRIDE
    ss = np.arange(n_sel)[None, :] * SEL_LEN
    ov = np.clip(np.minimum(cs + CMP_LEN, ss + SEL_LEN) - np.maximum(cs, ss), 0, None) / CMP_LEN
    out = np.zeros((n_sel, seq // CMP_STRIDE), np.float32)
    out[:, :n_cmp] = ov.T
    return jnp.asarray(out, dtype=BF16)


def _nsa(q, kc, vct, ks, vst, kw, vwt, gates, *, tq=128):
    batch, _, seq, _ = q.shape
    n_cmp = kc.shape[2]
    n_sel = seq // SEL_LEN
    assert seq % SLC_CHUNK == 0 and seq >= WINDOW + tq and SLC_CHUNK % tq == 0 and tq == LANES
    assert n_sel <= 2 * SEL_LEN
    ovt = jnp.broadcast_to(_overlap_t(seq), (batch, N_GROUPS, n_sel, n_cmp))
    vct = jnp.concatenate([vct, ovt], axis=2)
    keys = lambda n: pl.BlockSpec((1, N_GROUPS, n, LANES), lambda b, i: (b, 0, 0, 0))
    vals = lambda n: pl.BlockSpec((1, N_GROUPS, V_ROWS, n), lambda b, i: (b, 0, 0, 0))
    tiles = seq // tq
    s_rows = max(n_cmp, WINDOW + tq + HEAD_BLOCKS * SEL_LEN, 2 * SLC_CHUNK)
    return pl.pallas_call(
        functools.partial(_nsa_kernel, tq=tq, seq=seq),
        out_shape=jax.ShapeDtypeStruct((batch * seq, HEADS_PER_GROUP * LANES), BF16),
        grid=(batch, tiles),
        in_specs=[
            pl.BlockSpec((1, N_HEADS, tq, LANES), lambda b, i: (b, 0, i, 0)),
            keys(n_cmp), pl.BlockSpec((1, N_GROUPS, V_ROWS + n_sel, n_cmp), lambda b, i: (b, 0, 0, 0)),
            keys(seq), vals(seq), keys(seq), vals(seq),
            pl.BlockSpec((tq, LANES), lambda b, i: (b * tiles + i, 0)),
        ],
        out_specs=pl.BlockSpec((tq, HEADS_PER_GROUP * LANES), lambda b, i: (b * tiles + i, 0)),
        scratch_shapes=[
            pltpu.VMEM((N_HEADS, s_rows, tq), F32),
            pltpu.VMEM((N_HEADS // 2, s_rows, 2 * tq), BF16),
            pltpu.VMEM((N_HEADS, 2 * tq), F32),
            pltpu.VMEM((LANES, tq), F32),
            pltpu.VMEM((N_GROUPS, n_sel, tq), F32),
            pltpu.SMEM((seq // SLC_CHUNK,), jnp.int32),
            pltpu.VMEM((2 * N_HEADS, tq), F32),
            pltpu.VMEM((N_HEADS // 2, V_ROWS, 2 * tq), F32),
            pltpu.VMEM((HEADS_PER_GROUP, N_GROUPS * HEAD_DIM, tq), F32),
        ],
        compiler_params=_params("parallel", "arbitrary"),
        name="nsa",
    )(q, kc, vct, ks, vst, kw, vwt, gates)


HALO = max(POOL_WINDOWS)
assert all(w & (w - 1) == 0 for w in POOL_WINDOWS)


def _merge_kernel(x_ref, g_ref, u_ref, halo_ref, wgm_ref, on_ref, pw_ref, ps_ref, wbp_ref, wbn_ref, wo_ref,
                  o_ref, ext_ref, *, tm, seq):
    d = x_ref.shape[1]
    gm = jax.nn.sigmoid(_dot(_rms(x_ref[...], g_ref[...]).astype(BF16), wgm_ref[...]))
    b = _dot(on_ref[...], wbn_ref[...])
    pos0 = (pl.program_id(0) * tm) % seq
    ext_ref[0:HALO, :] = jnp.where(pos0 == 0, 0.0, halo_ref[...])
    ext_ref[HALO:HALO + tm, :] = u_ref[...]
    pos = (pos0 + lax.broadcasted_iota(jnp.int32, (tm, POOL_GROUP_DIM), 0)).astype(F32)
    mixed = []
    for gi, w in enumerate(POOL_WINDOWS):
        cols = slice(gi * POOL_GROUP_DIM, (gi + 1) * POOL_GROUP_DIM)
        run = ext_ref[:, cols]
        for k in range(w.bit_length() - 1):
            run = run + pltpu.roll(run, 1 << k, 0)
        u = ext_ref[HALO:HALO + tm, cols]
        delta = run[HALO:HALO + tm] / jnp.minimum(pos + 1.0, float(w)) - u
        mixed.append(_dot(delta.astype(BF16), pw_ref[gi]) * ps_ref[:, cols])
    mixed = jnp.concatenate(mixed, axis=-1).astype(BF16)
    a = _dot(mixed, wbp_ref[...])
    merged = gm[:, 0:d] * a + gm[:, d:2 * d] * b
    o_ref[...] = x_ref[...] + _dot(merged.astype(BF16), wo_ref[...])


def _merge(x, norm_g, u, w_gm, o_nsa, pool_w, pool_scale, w_bp, w_bn, w_out, seq, *, tm=512):
    n, d = x.shape
    row = lambda i: (i, 0)
    full = lambda a: pl.BlockSpec(a.shape, lambda i: (0,) * a.ndim, pipeline_mode=pl.Buffered(1))
    halo_blocks = tm // HALO
    return pl.pallas_call(
        functools.partial(_merge_kernel, tm=tm, seq=seq),
        out_shape=jax.ShapeDtypeStruct((n, d), F32),
        grid=(n // tm,),
        in_specs=[
            pl.BlockSpec((tm, d), row),
            full(norm_g),
            pl.BlockSpec((tm, POOL_COLS), row),
            pl.BlockSpec((HALO, POOL_COLS), lambda i: (jnp.maximum(i * halo_blocks - 1, 0), 0)),
            full(w_gm),
            pl.BlockSpec((tm, d), row),
            full(pool_w), full(pool_scale), full(w_bp), full(w_bn), full(w_out),
        ],
        out_specs=pl.BlockSpec((tm, d), row),
        scratch_shapes=[pltpu.VMEM((HALO + tm, POOL_COLS), F32)],
        compiler_params=_params("parallel"),
        name="merge",
    )(x, norm_g, u, u, w_gm, o_nsa, pool_w, pool_scale, w_bp, w_bn, w_out)


def kernel(x, ffn1_norm, ffn1_w_gate, ffn1_w_up, ffn1_w_down, mix_norm, w_in, cmp_pos, cmp_k_w1, cmp_k_w2, cmp_v_w1, cmp_v_w2, pool_w, pool_scale, w_branch_pool, w_branch_nsa, w_out, ffn2_norm, ffn2_w_gate, ffn2_w_up, ffn2_w_down, final_norm):
    batch, seq, d = x.shape
    depth = w_in.shape[0]
    xf = x.reshape(batch * seq, d)
    bf = lambda a: a.astype(BF16)
    row = lambda a: a.reshape(1, -1)
    for l in range(depth):
        xf = _ffn(xf, row(ffn1_norm[l]), bf(ffn1_w_gate[l]), bf(ffn1_w_up[l]), bf(ffn1_w_down[l]),
                  row(final_norm), final_norm=False)

        w_packed, w_gm = _pack_w_in(w_in[l], d)
        q, kc, vc, ks, kw, vs, vw, gates, u = _proj(xf, row(mix_norm[l]), w_packed, batch, seq)
        k_cmp, v_cmp = _compress(kc, vc, cmp_pos[l], cmp_k_w1[l], cmp_k_w2[l], cmp_v_w1[l], cmp_v_w2[l], batch, seq)
        o_nsa = _nsa(q, k_cmp, v_cmp, ks, vs, kw, vw, gates)

        w_bn = w_branch_nsa[l].reshape(N_GROUPS, HEADS_PER_GROUP, HEAD_DIM, d).transpose(1, 0, 2, 3).reshape(-1, d)
        xf = _merge(xf, row(mix_norm[l]), u, w_gm, o_nsa, bf(pool_w[l]), row(pool_scale[l]), bf(w_branch_pool[l]), bf(w_bn),
                    bf(w_out[l]), seq)

        xf = _ffn(xf, row(ffn2_norm[l]), bf(ffn2_w_gate[l]), bf(ffn2_w_up[l]), bf(ffn2_w_down[l]),
                  row(final_norm), final_norm=(l == depth - 1))
    return xf.reshape(batch, seq, d)
```

```python
import functools

import jax
import jax.numpy as jnp
import numpy as np
from jax import lax
from jax.experimental import pallas as pl
from jax.experimental.pallas import tpu as pltpu

N_HEADS = 16
N_GROUPS = 2
HEADS_PER_GROUP = N_HEADS // N_GROUPS
HEAD_DIM = 64
CMP_LEN = 32
CMP_STRIDE = 16
SEL_LEN = 64
N_SELECT = 16
WINDOW = 512
POOL_WINDOWS = (2, 4, 8, 16)
POOL_GROUP_DIM = 128
RMS_EPS = 1e-6
ALIBI_MAX_BIAS = 8.0

LANES = 128
NEG_BIG = -1e30
TOPK_BIG = 1e30
N_FORCED = 3
TOPK_TAKEN = -float(2 ** 101)
VMEM_LIMIT = 48 * 1024 * 1024

BF16 = jnp.bfloat16
F32 = jnp.float32


def _dot(a, b):
    return jnp.dot(a, b, preferred_element_type=F32)


def _dot_nt(a, b):
    return lax.dot_general(a, b, (((1,), (1,)), ((), ())), preferred_element_type=F32)


def _rms(x, g):
    return x * lax.rsqrt(jnp.mean(x * x, axis=-1, keepdims=True) + RMS_EPS) * g


def _params(*sem):
    return pltpu.CompilerParams(dimension_semantics=sem, vmem_limit_bytes=VMEM_LIMIT)


def _ffn_kernel(x_ref, g_ref, wg_ref, wu_ref, wd_ref, fin_ref, o_ref, *, final_norm):
    x = x_ref[...]
    xn = _rms(x, g_ref[...]).astype(BF16)
    gate = _dot(xn, wg_ref[...])
    up = _dot(xn, wu_ref[...])
    act = (gate * jax.nn.sigmoid(gate)) * up
    y = x + 0.5 * _dot(act.astype(BF16), wd_ref[...])
    if final_norm:
        y = _rms(y, fin_ref[...])
    o_ref[...] = y


def _ffn(x, norm_g, wg, wu, wd, fin_g, *, final_norm, tm=512):
    n, d = x.shape
    row = lambda i: (i, 0)
    resident = lambda a: pl.BlockSpec(a.shape, lambda i: (0, 0), pipeline_mode=pl.Buffered(1))
    return pl.pallas_call(
        functools.partial(_ffn_kernel, final_norm=final_norm),
        out_shape=jax.ShapeDtypeStruct((n, d), F32),
        grid=(n // tm,),
        in_specs=[pl.BlockSpec((tm, d), row), resident(norm_g), resident(wg), resident(wu), resident(wd),
                  resident(fin_g)],
        out_specs=pl.BlockSpec((tm, d), row),
        compiler_params=_params("parallel"),
        name="ffn",
    )(x, norm_g, wg, wu, wd, fin_g)


LOG2E = 1.4426950408889634
N_FEAT = 6


def _bf16_pieces(x):
    x = np.asarray(x, np.float32)
    s1 = x.astype(BF16).astype(np.float32)
    s2 = (x - s1).astype(BF16).astype(np.float32)
    s3 = (x - s1 - s2).astype(BF16).astype(np.float32)
    return s1, s2, s3


def _query_features():
    slopes = np.float32(2.0) ** (-ALIBI_MAX_BIAS * np.arange(1, N_HEADS + 1, dtype=np.float32) / N_HEADS)
    s1, s2, s3 = _bf16_pieces(slopes * np.float32(LOG2E))
    feat = np.zeros((N_HEADS, LANES), np.float32)
    feat[:, HEAD_DIM:HEAD_DIM + N_FEAT] = np.stack([s1, s2, s3, SEL_LEN * s1, SEL_LEN * s2, SEL_LEN * s3], axis=1)
    return jnp.asarray(feat)


def _key_features(pos, width, offset):
    pos = np.asarray(pos)
    a, b = (pos // SEL_LEN).astype(np.float32), (pos % SEL_LEN).astype(np.float32)
    feat = np.zeros((len(pos), width), np.float32)
    feat[:, offset:offset + N_FEAT] = np.stack([b, b, b, a, a, a], axis=1)
    return jnp.asarray(feat)


Q_COLS = N_HEADS * HEAD_DIM
CMP_OFF = Q_COLS
KEY_OFF = CMP_OFF + 2 * LANES
VAL_OFF = KEY_OFF + 2 * LANES
GATE_OFF = VAL_OFF + 2 * LANES
POOL_OFF = GATE_OFF + LANES
POOL_COLS = len(POOL_WINDOWS) * POOL_GROUP_DIM


V_ROWS = HEAD_DIM + 16


def _store_values_t(v_ref, v):
    vt = v.T
    for g in range(N_GROUPS):
        v_ref[0, g, 0:HEAD_DIM, :] = vt[g * HEAD_DIM:(g + 1) * HEAD_DIM, :].astype(BF16)
        v_ref[0, g, HEAD_DIM:V_ROWS, :] = jnp.ones((V_ROWS - HEAD_DIM, v.shape[0]), BF16)


def _proj_kernel(x_ref, g_ref, w_ref, qf_ref, kf_ref, q_ref, kc_ref, vc_ref, ks_ref, kw_ref, vs_ref, vw_ref,
                 gate_ref, u_ref):
    hn = _rms(x_ref[...], g_ref[...]).astype(BF16)
    q_scale = HEAD_DIM ** -0.5 * LOG2E
    low = lax.broadcasted_iota(jnp.int32, (x_ref.shape[0], LANES), 1) < HEAD_DIM
    for j in range(N_HEADS // 4):
        q4 = _dot(hn, w_ref[:, j * 2 * LANES:(j + 1) * 2 * LANES]) * q_scale
        for b in range(2):
            both = q4[:, b * LANES:(b + 1) * LANES]
            h = 4 * j + 2 * b
            q_ref[0, h] = (jnp.where(low, both, 0.0) + qf_ref[h:h + 1, :]).astype(BF16)
            q_ref[0, h + 1] = (jnp.where(low, pltpu.roll(both, HEAD_DIM, 1), 0.0) + qf_ref[h + 1:h + 2, :]).astype(BF16)
    cmp_in = _dot(hn, w_ref[:, CMP_OFF:CMP_OFF + 2 * LANES])
    kc_ref[...] = cmp_in[:, 0:LANES]
    vc_ref[...] = cmp_in[:, LANES:2 * LANES]
    keys = _dot(hn, w_ref[:, KEY_OFF:KEY_OFF + 2 * LANES])
    kf = kf_ref[...]
    for i, k_ref in enumerate((ks_ref, kw_ref)):
        both = keys[:, i * LANES:(i + 1) * LANES]
        k_ref[0, 0] = (jnp.where(low, both, 0.0) + kf).astype(BF16)
        k_ref[0, 1] = (jnp.where(low, pltpu.roll(both, HEAD_DIM, 1), 0.0) + kf).astype(BF16)
    vals = _dot(hn, w_ref[:, VAL_OFF:VAL_OFF + 2 * LANES])
    for i, v_ref in enumerate((vs_ref, vw_ref)):
        _store_values_t(v_ref, vals[:, i * LANES:(i + 1) * LANES])
    gate_ref[...] = jax.nn.sigmoid(_dot(hn, w_ref[:, GATE_OFF:GATE_OFF + LANES]))
    u_ref[...] = _dot(hn, w_ref[:, POOL_OFF:POOL_OFF + POOL_COLS])


def _pack_w_in(w_in):
    w_in = w_in.astype(BF16)
    qw = N_HEADS * HEAD_DIM
    kvw = N_GROUPS * HEAD_DIM
    kc, vc, ks, vs, kw, vw = (w_in[:, qw + i * kvw:qw + (i + 1) * kvw] for i in range(6))
    off = qw + 6 * kvw
    n_g = 3 * N_HEADS
    w_g = jnp.pad(w_in[:, off:off + n_g], ((0, 0), (0, LANES - n_g)))
    w_pool = w_in[:, off + n_g:off + n_g + POOL_COLS]
    w_merge = w_in[:, off + n_g + POOL_COLS:]
    return jnp.concatenate([w_in[:, :qw], kc, vc, ks, kw, vs, vw, w_g, w_pool], axis=1), w_merge


def _proj(x, norm_g, w_packed, batch, seq, *, tm=512):
    n, d = x.shape
    tiles_per_seq = seq // tm
    row = lambda i: (i, 0)
    const = lambda i: (0, 0)
    per_group = lambda i: (i // tiles_per_seq, 0, i % tiles_per_seq, 0)
    qf = _query_features()
    kf = _key_features(np.arange(seq), LANES, HEAD_DIM)
    flat = lambda width, dtype: (jax.ShapeDtypeStruct((n, width), dtype), pl.BlockSpec((tm, width), row))
    grouped = lambda count: (jax.ShapeDtypeStruct((batch, count, seq, LANES), BF16),
                             pl.BlockSpec((1, count, tm, LANES), per_group))
    values_t = (jax.ShapeDtypeStruct((batch, N_GROUPS, V_ROWS, seq), BF16),
                pl.BlockSpec((1, N_GROUPS, V_ROWS, tm), lambda i: (i // tiles_per_seq, 0, 0, i % tiles_per_seq)))
    outs = [grouped(N_HEADS), flat(LANES, F32), flat(LANES, F32), grouped(N_GROUPS), grouped(N_GROUPS),
            values_t, values_t, flat(LANES, F32), flat(POOL_COLS, F32)]
    return pl.pallas_call(
        _proj_kernel,
        out_shape=[o[0] for o in outs],
        grid=(n // tm,),
        in_specs=[pl.BlockSpec((tm, d), row), pl.BlockSpec((1, d), const),
                  pl.BlockSpec(w_packed.shape, const, pipeline_mode=pl.Buffered(1)),
                  pl.BlockSpec(qf.shape, const), pl.BlockSpec((tm, LANES), lambda i: (i % tiles_per_seq, 0))],
        out_specs=[o[1] for o in outs],
        compiler_params=_params("parallel"),
        name="proj",
    )(x, norm_g, w_packed, qf, kf)


def _compress_kernel(k_ref, v_ref, pos_ref, wk1_ref, wk2_ref, wv1_ref, wv2_ref, feat_ref, ko_ref, vo_ref):
    n_chunk = k_ref.shape[0] // CMP_STRIDE
    hidden = wk2_ref.shape[0]

    def compress(x_ref, w1_ref, w2_ref):
        first = second = None
        for r in range(CMP_STRIDE):
            rows = x_ref[pl.ds(r, n_chunk, stride=CMP_STRIDE), :]
            a = _dot((rows + pos_ref[r:r + 1, :]).astype(BF16), w1_ref[r])
            b = _dot((rows + pos_ref[CMP_STRIDE + r:CMP_STRIDE + r + 1, :]).astype(BF16), w1_ref[CMP_STRIDE + r])
            first = a if first is None else first + a
            second = b if second is None else second + b
        act = jax.nn.gelu(first + pltpu.roll(second, n_chunk - 1, 0)).astype(BF16)
        return [_dot(act[:, g * hidden:(g + 1) * hidden], w2_ref[...]) for g in range(N_GROUPS)]

    for g, k_cmp in enumerate(compress(k_ref, wk1_ref, wk2_ref)):
        ko_ref[0, g] = jnp.concatenate([k_cmp, feat_ref[...]], axis=-1).astype(BF16)
    _store_values_t(vo_ref, jnp.concatenate(compress(v_ref, wv1_ref, wv2_ref), axis=-1))


def _block_diag_w1(w1):
    w = w1.reshape(CMP_LEN, HEAD_DIM, -1)
    z = jnp.zeros_like(w)
    return jnp.concatenate([jnp.concatenate([w, z], axis=2), jnp.concatenate([z, w], axis=2)], axis=1).astype(BF16)


def _compress(kc, vc, pos, wk1, wk2, wv1, wv2, batch, seq):
    n_chunk = seq // CMP_STRIDE
    rows = pl.BlockSpec((seq, LANES), lambda b: (b, 0))
    full = lambda a: pl.BlockSpec(a.shape, lambda b: (0,) * a.ndim, pipeline_mode=pl.Buffered(1))
    pos2 = jnp.concatenate([pos] * N_GROUPS, axis=1)
    feat = _key_features(np.arange(n_chunk) * CMP_STRIDE + CMP_LEN - 1, HEAD_DIM, 0)
    args = (pos2, _block_diag_w1(wk1), wk2.astype(BF16), _block_diag_w1(wv1), wv2.astype(BF16), feat)
    return pl.pallas_call(
        _compress_kernel,
        out_shape=[jax.ShapeDtypeStruct((batch, N_GROUPS, n_chunk, LANES), BF16),
                   jax.ShapeDtypeStruct((batch, N_GROUPS, V_ROWS, n_chunk), BF16)],
        grid=(batch,),
        in_specs=[rows, rows] + [full(a) for a in args],
        out_specs=[pl.BlockSpec((1, N_GROUPS, n_chunk, LANES), lambda b: (b, 0, 0, 0)),
                   pl.BlockSpec((1, N_GROUPS, V_ROWS, n_chunk), lambda b: (b, 0, 0, 0))],
        compiler_params=_params("parallel"),
        name="compress",
    )(kc, vc, *args)


SLC_CHUNK = 256
BLOCKS_PER_CHUNK = SLC_CHUNK // SEL_LEN
HEAD_BLOCKS = 2


def _nsa_kernel(q_ref, kc_ref, vct_ref, ks_ref, vst_ref, kw_ref, vwt_ref, gate_ref, o_ref,
                st_ref, p_ref, mx_ref, gt_ref, selt_ref, list_ref, m_ref, acc_ref, out_ref, *, tq, seq):
    n_sel = seq // SEL_LEN
    n_chunks = seq // SLC_CHUNK
    n_cmp = kc_ref.shape[2]
    n_win = WINDOW + tq
    q0 = pl.program_id(1) * tq
    gt_ref[...] = gate_ref[...].T
    pairs = [(g, g * HEADS_PER_GROUP + 2 * j) for g in range(N_GROUPS) for j in range(HEADS_PER_GROUP // 2)]

    def gate_row(head, branch):
        c = 3 * head + branch
        return gt_ref[c:c + 1, :]

    def out_rows(head):
        g, hh = divmod(head, HEADS_PER_GROUP)
        return hh, slice(g * HEAD_DIM, (g + 1) * HEAD_DIM)

    def stage_scores(keys, n_keys, row0=0):
        for g, head0 in pairs:
            q_pair = q_ref[0, head0:head0 + 2].reshape(2 * tq, LANES)
            scores = _dot_nt(keys(g), q_pair)
            for i in range(2):
                st_ref[head0 + i, row0:row0 + n_keys, :] = scores[:, i * tq:(i + 1) * tq]

    def finish_pair(head0, pv, branch, first, ok=None):
        invs = []
        for i in range(2):
            cols = slice(i * tq, (i + 1) * tq)
            inv = 1.0 / pv[HEAD_DIM:HEAD_DIM + 1, cols]
            if ok is not None:
                inv = jnp.where(ok, inv, 0.0)
            hh, rows = out_rows(head0 + i)
            contrib = pv[0:HEAD_DIM, cols] * (inv * gate_row(head0 + i, branch))
            out_ref[hh, rows, :] = contrib if first else out_ref[hh, rows, :] + contrib
            invs.append(inv)
        return invs

    t_c = q0 + lax.broadcasted_iota(jnp.int32, (n_cmp, tq), 1)
    end_c = lax.broadcasted_iota(jnp.int32, (n_cmp, tq), 0) * CMP_STRIDE + (CMP_LEN - 1)
    bias_c = jnp.where(t_c >= end_c, 0.0, NEG_BIG)
    has_cmp = (q0 + lax.broadcasted_iota(jnp.int32, (1, tq), 1)) >= CMP_LEN - 1
    inv_c = [None] * N_HEADS
    raw = [None] * (N_HEADS // 2)

    def cmp_pair(g, head0):
        for i in range(2):
            x = st_ref[head0 + i, 0:n_cmp, :] + bias_c
            p_ref[head0 // 2, 0:n_cmp, i * tq:(i + 1) * tq] = jnp.exp2(x - jnp.max(x, axis=0, keepdims=True)).astype(BF16)
        both = _dot(vct_ref[0, g], p_ref[head0 // 2, 0:n_cmp, :])
        raw[head0 // 2] = both[V_ROWS:, :]
        inv_c[head0], inv_c[head0 + 1] = finish_pair(head0, both[0:V_ROWS, :], 0, True, has_cmp)

    stage_scores(lambda g: kc_ref[0, g], n_cmp)
    for g, head0 in pairs:
        cmp_pair(g, head0)

    w0 = pl.multiple_of(jnp.maximum(q0 - WINDOW, 0), tq)
    stage_scores(lambda g: kw_ref[0, g, pl.ds(w0, n_win), :], n_win)
    head_keys = HEAD_BLOCKS * SEL_LEN
    stage_scores(lambda g: ks_ref[0, g, 0:head_keys, :], head_keys, n_win)

    dist_w = (q0 + lax.broadcasted_iota(jnp.int32, (n_win, tq), 1)) - (w0 + lax.broadcasted_iota(jnp.int32, (n_win, tq), 0))
    bias_w = jnp.where((dist_w >= 0) & (dist_w < WINDOW), 0.0, NEG_BIG)

    def win_pair(g, head0):
        for i in range(2):
            x = st_ref[head0 + i, 0:n_win, :] + bias_w
            p_ref[head0 // 2, 0:n_win, i * tq:(i + 1) * tq] = jnp.exp2(x - jnp.max(x, axis=0, keepdims=True)).astype(BF16)
        pv = _dot(vwt_ref[0, g, :, pl.ds(w0, n_win)], p_ref[head0 // 2, 0:n_win, :])
        finish_pair(head0, pv, 2, False)

    for g, head0 in pairs:
        win_pair(g, head0)

    blk = lax.broadcasted_iota(jnp.int32, (n_sel, tq), 0)
    cur = (q0 + lax.broadcasted_iota(jnp.int32, (n_sel, tq), 1)) // SEL_LEN
    valid = blk <= cur
    forced = valid & ((blk == 0) | (blk == cur) | (blk == cur - 1))
    blk_f = blk.astype(F32)
    score = []
    for g in range(N_GROUPS):
        total = None
        for head in range(g * HEADS_PER_GROUP, (g + 1) * HEADS_PER_GROUP):
            part = raw[head // 2][:, (head % 2) * tq:(head % 2 + 1) * tq] * inv_c[head]
            total = part if total is None else total + part
        score.append(jnp.where(forced, TOPK_TAKEN, jnp.where(valid, total, -TOPK_BIG)))

    for _ in range(min(N_SELECT, n_sel) - N_FORCED):
        for g in range(N_GROUPS):
            best = jnp.max(score[g], axis=0, keepdims=True)
            pick = jnp.min(jnp.where(score[g] == best, blk_f, float(n_sel)), axis=0, keepdims=True)
            score[g] = jnp.where(blk_f == pick, TOPK_TAKEN, score[g])
    sel = [jnp.where(valid & (score[g] == TOPK_TAKEN), 1.0, 0.0) for g in range(N_GROUPS)]
    for g in range(N_GROUPS):
        selt_ref[g] = jnp.where(blk >= HEAD_BLOCKS, sel[g], 0.0)
    sel_any = jnp.where(blk >= HEAD_BLOCKS, sel[0] + sel[1], 0.0)
    n_items = jnp.int32(0)
    for c in range(n_chunks):
        any_sel = jnp.max(sel_any[c * BLOCKS_PER_CHUNK:(c + 1) * BLOCKS_PER_CHUNK, :])
        list_ref[n_items] = c
        n_items = n_items + (any_sel > 0.5).astype(jnp.int32)

    causal_h = (q0 + lax.broadcasted_iota(jnp.int32, (head_keys, tq), 1)) >= lax.broadcasted_iota(jnp.int32, (head_keys, tq), 0)
    head_rows = slice(n_win, n_win + head_keys)
    for g, head0 in pairs:
        if head0 % HEADS_PER_GROUP == 0:
            sel_keys = jnp.concatenate([jnp.broadcast_to(sel[g][i:i + 1, :], (SEL_LEN, tq)) for i in range(HEAD_BLOCKS)], axis=0)
            bias = jnp.where(causal_h & (sel_keys > 0.5), 0.0, NEG_BIG)
        for i in range(2):
            x = st_ref[head0 + i, head_rows, :] + bias
            m_new = jnp.max(x, axis=0, keepdims=True)
            m_ref[head0 + i:head0 + i + 1, :] = m_new
            p_ref[head0 // 2, head_rows, i * tq:(i + 1) * tq] = jnp.exp2(x - m_new).astype(BF16)
    for g, head0 in pairs:
        acc_ref[head0 // 2] = _dot(vst_ref[0, g, :, 0:head_keys], p_ref[head0 // 2, head_rows, :])

    base = (lax.broadcasted_iota(jnp.int32, (SLC_CHUNK, tq), 1)
            - lax.broadcasted_iota(jnp.int32, (SLC_CHUNK, tq), 0))

    def stage_chunk(item, row0):
        c = list_ref[item]
        k0 = pl.multiple_of(c * SLC_CHUNK, SLC_CHUNK)
        causal = base >= k0 - q0
        for g, head0 in pairs:
            if head0 % HEADS_PER_GROUP == 0:
                sel_keys = jnp.concatenate(
                    [jnp.broadcast_to(selt_ref[g, pl.ds(c * BLOCKS_PER_CHUNK + i, 1), :], (SEL_LEN, tq))
                     for i in range(BLOCKS_PER_CHUNK)], axis=0)
                bias = jnp.where(causal & (sel_keys > 0.5), 0.0, NEG_BIG)
                bias2 = jnp.concatenate([bias, bias], axis=1)
            q_pair = q_ref[0, head0:head0 + 2].reshape(2 * tq, LANES)
            scores = _dot_nt(ks_ref[0, g, pl.ds(k0, SLC_CHUNK), :], q_pair) + bias2
            slot = (row0 // SLC_CHUNK) * (N_HEADS // 2) + head0 // 2
            mx_ref[slot:slot + 1, :] = jnp.max(scores, axis=0, keepdims=True)
            for i in range(2):
                st_ref[head0 + i, row0:row0 + SLC_CHUNK, :] = scores[:, i * tq:(i + 1) * tq]

    def consume_chunk(item, row0):
        k0 = pl.multiple_of(list_ref[item] * SLC_CHUNK, SLC_CHUNK)
        for g, head0 in pairs:
            alphas = []
            for i, head in enumerate((head0, head0 + 1)):
                slot = (row0 // SLC_CHUNK) * (N_HEADS // 2) + head0 // 2
                m_old = m_ref[head:head + 1, :]
                m_new = jnp.maximum(m_old, mx_ref[slot:slot + 1, i * tq:(i + 1) * tq])
                alphas.append(jnp.exp2(m_old - m_new))
                m_ref[head:head + 1, :] = m_new
                x = st_ref[head, row0:row0 + SLC_CHUNK, :]
                p_ref[head0 // 2, 0:SLC_CHUNK, i * tq:(i + 1) * tq] = jnp.exp2(x - m_new).astype(BF16)
            pv = _dot(vst_ref[0, g, :, pl.ds(k0, SLC_CHUNK)], p_ref[head0 // 2, 0:SLC_CHUNK, :])
            acc_ref[head0 // 2] = acc_ref[head0 // 2] * jnp.concatenate(alphas, axis=1) + pv

    last = jnp.maximum(n_items - 1, 0)
    stage_chunk(0, 0)

    def two_chunks(j, carry):
        stage_chunk(jnp.minimum(2 * j + 1, last), SLC_CHUNK)
        consume_chunk(2 * j, 0)

        @pl.when(2 * j + 1 < n_items)
        def _():
            stage_chunk(jnp.minimum(2 * j + 2, last), 0)
            consume_chunk(2 * j + 1, SLC_CHUNK)
        return carry

    lax.fori_loop(0, (n_items + 1) // 2, two_chunks, 0)
    for g, head0 in pairs:
        finish_pair(head0, acc_ref[head0 // 2], 1, False)

    for hh in range(HEADS_PER_GROUP):
        o_ref[:, hh * LANES:(hh + 1) * LANES] = out_ref[hh].T.astype(BF16)


def _overlap_t(seq):
    n_cmp = (seq - CMP_LEN) // CMP_STRIDE + 1
    n_sel = seq // SEL_LEN
    cs = np.arange(n_cmp)[:, None] * CMP_STRIDE
    ss = np.arange(n_sel)[None, :] * SEL_LEN
    ov = np.clip(np.minimum(cs + CMP_LEN, ss + SEL_LEN) - np.maximum(cs, ss), 0, None) / CMP_LEN
    out = np.zeros((n_sel, seq // CMP_STRIDE), np.float32)
    out[:, :n_cmp] = ov.T
    return jnp.asarray(out, dtype=BF16)


def _nsa(q, kc, vct, ks, vst, kw, vwt, gates, *, tq=128):
    batch, _, seq, _ = q.shape
    n_cmp = kc.shape[2]
    n_sel = seq // SEL_LEN
    assert seq % SLC_CHUNK == 0 and seq >= WINDOW + tq and SLC_CHUNK % tq == 0 and tq == LANES
    assert n_sel <= 2 * SEL_LEN
    ovt = jnp.broadcast_to(_overlap_t(seq), (batch, N_GROUPS, n_sel, n_cmp))
    vct = jnp.concatenate([vct, ovt], axis=2)
    keys = lambda n: pl.BlockSpec((1, N_GROUPS, n, LANES), lambda b, i: (b, 0, 0, 0))
    vals = lambda n: pl.BlockSpec((1, N_GROUPS, V_ROWS, n), lambda b, i: (b, 0, 0, 0))
    tiles = seq // tq
    s_rows = max(n_cmp, WINDOW + tq + HEAD_BLOCKS * SEL_LEN, 2 * SLC_CHUNK)
    return pl.pallas_call(
        functools.partial(_nsa_kernel, tq=tq, seq=seq),
        out_shape=jax.ShapeDtypeStruct((batch * seq, HEADS_PER_GROUP * LANES), BF16),
        grid=(batch, tiles),
        in_specs=[
            pl.BlockSpec((1, N_HEADS, tq, LANES), lambda b, i: (b, 0, i, 0)),
            keys(n_cmp), pl.BlockSpec((1, N_GROUPS, V_ROWS + n_sel, n_cmp), lambda b, i: (b, 0, 0, 0)),
            keys(seq), vals(seq), keys(seq), vals(seq),
            pl.BlockSpec((tq, LANES), lambda b, i: (b * tiles + i, 0)),
        ],
        out_specs=pl.BlockSpec((tq, HEADS_PER_GROUP * LANES), lambda b, i: (b * tiles + i, 0)),
        scratch_shapes=[
            pltpu.VMEM((N_HEADS, s_rows, tq), F32),
            pltpu.VMEM((N_HEADS // 2, s_rows, 2 * tq), BF16),
            pltpu.VMEM((N_HEADS, 2 * tq), F32),
            pltpu.VMEM((LANES, tq), F32),
            pltpu.VMEM((N_GROUPS, n_sel, tq), F32),
            pltpu.SMEM((seq // SLC_CHUNK,), jnp.int32),
            pltpu.VMEM((2 * N_HEADS, tq), F32),
            pltpu.VMEM((N_HEADS // 2, V_ROWS, 2 * tq), F32),
            pltpu.VMEM((HEADS_PER_GROUP, N_GROUPS * HEAD_DIM, tq), F32),
        ],
        compiler_params=_params("parallel", "arbitrary"),
        name="nsa",
    )(q, kc, vct, ks, vst, kw, vwt, gates)


HALO = max(POOL_WINDOWS)
assert all(w & (w - 1) == 0 for w in POOL_WINDOWS)


def _merge_kernel(x_ref, g_ref, u_ref, halo_ref, wgm_ref, on_ref, pw_ref, ps_ref, wbp_ref, wbn_ref, wo_ref,
                  o_ref, ext_ref, *, tm, seq):
    d = x_ref.shape[1]
    gm = jax.nn.sigmoid(_dot(_rms(x_ref[...], g_ref[...]).astype(BF16), wgm_ref[...]))
    b = _dot(on_ref[...], wbn_ref[...])
    pos0 = (pl.program_id(0) * tm) % seq
    ext_ref[0:HALO, :] = jnp.where(pos0 == 0, 0.0, halo_ref[...])
    ext_ref[HALO:HALO + tm, :] = u_ref[...]
    pos = (pos0 + lax.broadcasted_iota(jnp.int32, (tm, POOL_GROUP_DIM), 0)).astype(F32)
    mixed = []
    for gi, w in enumerate(POOL_WINDOWS):
        cols = slice(gi * POOL_GROUP_DIM, (gi + 1) * POOL_GROUP_DIM)
        run = ext_ref[:, cols]
        for k in range(w.bit_length() - 1):
            run = run + pltpu.roll(run, 1 << k, 0)
        u = ext_ref[HALO:HALO + tm, cols]
        delta = run[HALO:HALO + tm] / jnp.minimum(pos + 1.0, float(w)) - u
        mixed.append(_dot(delta.astype(BF16), pw_ref[gi]) * ps_ref[:, cols])
    mixed = jnp.concatenate(mixed, axis=-1).astype(BF16)
    a = _dot(mixed, wbp_ref[...])
    merged = gm[:, 0:d] * a + gm[:, d:2 * d] * b
    o_ref[...] = x_ref[...] + _dot(merged.astype(BF16), wo_ref[...])


def _merge(x, norm_g, u, w_gm, o_nsa, pool_w, pool_scale, w_bp, w_bn, w_out, seq, *, tm=512):
    n, d = x.shape
    row = lambda i: (i, 0)
    full = lambda a: pl.BlockSpec(a.shape, lambda i: (0,) * a.ndim, pipeline_mode=pl.Buffered(1))
    halo_blocks = tm // HALO
    return pl.pallas_call(
        functools.partial(_merge_kernel, tm=tm, seq=seq),
        out_shape=jax.ShapeDtypeStruct((n, d), F32),
        grid=(n // tm,),
        in_specs=[
            pl.BlockSpec((tm, d), row),
            full(norm_g),
            pl.BlockSpec((tm, POOL_COLS), row),
            pl.BlockSpec((HALO, POOL_COLS), lambda i: (jnp.maximum(i * halo_blocks - 1, 0), 0)),
            full(w_gm),
            pl.BlockSpec((tm, d), row),
            full(pool_w), full(pool_scale), full(w_bp), full(w_bn), full(w_out),
        ],
        out_specs=pl.BlockSpec((tm, d), row),
        scratch_shapes=[pltpu.VMEM((HALO + tm, POOL_COLS), F32)],
        compiler_params=_params("parallel"),
        name="merge",
    )(x, norm_g, u, u, w_gm, o_nsa, pool_w, pool_scale, w_bp, w_bn, w_out)


def kernel(x, ffn1_norm, ffn1_w_gate, ffn1_w_up, ffn1_w_down, mix_norm, w_in, cmp_pos, cmp_k_w1, cmp_k_w2, cmp_v_w1, cmp_v_w2, pool_w, pool_scale, w_branch_pool, w_branch_nsa, w_out, ffn2_norm, ffn2_w_gate, ffn2_w_up, ffn2_w_down, final_norm):
    batch, seq, d = x.shape
    depth = w_in.shape[0]
    xf = x.reshape(batch * seq, d)
    bf = lambda a: a.astype(BF16)
    row = lambda a: a.reshape(1, -1)
    for l in range(depth):
        xf = _ffn(xf, row(ffn1_norm[l]), bf(ffn1_w_gate[l]), bf(ffn1_w_up[l]), bf(ffn1_w_down[l]),
                  row(final_norm), final_norm=False)

        w_packed, w_gm = _pack_w_in(w_in[l])
        q, kc, vc, ks, kw, vs, vw, gates, u = _proj(xf, row(mix_norm[l]), w_packed, batch, seq)
        k_cmp, v_cmp = _compress(kc, vc, cmp_pos[l], cmp_k_w1[l], cmp_k_w2[l], cmp_v_w1[l], cmp_v_w2[l], batch, seq)
        o_nsa = _nsa(q, k_cmp, v_cmp, ks, vs, kw, vw, gates)

        w_bn = w_branch_nsa[l].reshape(N_GROUPS, HEADS_PER_GROUP, HEAD_DIM, d).transpose(1, 0, 2, 3).reshape(-1, d)
        xf = _merge(xf, row(mix_norm[l]), u, w_gm, o_nsa, bf(pool_w[l]), row(pool_scale[l]), bf(w_branch_pool[l]), bf(w_bn),
                    bf(w_out[l]), seq)

        xf = _ffn(xf, row(ffn2_norm[l]), bf(ffn2_w_gate[l]), bf(ffn2_w_up[l]), bf(ffn2_w_down[l]),
                  row(final_norm), final_norm=(l == depth - 1))
    return xf.reshape(batch, seq, d)
```

```python
import functools

import jax
import jax.numpy as jnp
import numpy as np
from jax import lax
from jax.experimental import pallas as pl
from jax.experimental.pallas import tpu as pltpu

N_HEADS = 16
N_GROUPS = 2
HEADS_PER_GROUP = N_HEADS // N_GROUPS
HEAD_DIM = 64
CMP_LEN = 32
CMP_STRIDE = 16
SEL_LEN = 64
N_SELECT = 16
WINDOW = 512
POOL_WINDOWS = (2, 4, 8, 16)
POOL_GROUP_DIM = 128
RMS_EPS = 1e-6
ALIBI_MAX_BIAS = 8.0

LANES = 128
NEG_BIG = -1e30
TOPK_BIG = 1e30
N_FORCED = 3
TOPK_TAKEN = -float(2 ** 101)
VMEM_LIMIT = 48 * 1024 * 1024

BF16 = jnp.bfloat16
F32 = jnp.float32


def _dot(a, b):
    return jnp.dot(a, b, preferred_element_type=F32)


def _dot_nt(a, b):
    return lax.dot_general(a, b, (((1,), (1,)), ((), ())), preferred_element_type=F32)


def _rms(x, g):
    return x * lax.rsqrt(jnp.mean(x * x, axis=-1, keepdims=True) + RMS_EPS) * g


def _params(*sem):
    return pltpu.CompilerParams(dimension_semantics=sem, vmem_limit_bytes=VMEM_LIMIT)


def _ffn_kernel(x_ref, g_ref, wg_ref, wu_ref, wd_ref, fin_ref, o_ref, *, final_norm):
    x = x_ref[...]
    xn = _rms(x, g_ref[...]).astype(BF16)
    gate = _dot(xn, wg_ref[...])
    up = _dot(xn, wu_ref[...])
    act = (gate * jax.nn.sigmoid(gate)) * up
    y = x + 0.5 * _dot(act.astype(BF16), wd_ref[...])
    if final_norm:
        y = _rms(y, fin_ref[...])
    o_ref[...] = y


def _ffn(x, norm_g, wg, wu, wd, fin_g, *, final_norm, tm=512):
    n, d = x.shape
    row = lambda i: (i, 0)
    resident = lambda a: pl.BlockSpec(a.shape, lambda i: (0, 0), pipeline_mode=pl.Buffered(1))
    return pl.pallas_call(
        functools.partial(_ffn_kernel, final_norm=final_norm),
        out_shape=jax.ShapeDtypeStruct((n, d), F32),
        grid=(n // tm,),
        in_specs=[pl.BlockSpec((tm, d), row), resident(norm_g), resident(wg), resident(wu), resident(wd),
                  resident(fin_g)],
        out_specs=pl.BlockSpec((tm, d), row),
        compiler_params=_params("parallel"),
        name="ffn",
    )(x, norm_g, wg, wu, wd, fin_g)


LOG2E = 1.4426950408889634
N_FEAT = 6


def _bf16_pieces(x):
    x = np.asarray(x, np.float32)
    s1 = x.astype(BF16).astype(np.float32)
    s2 = (x - s1).astype(BF16).astype(np.float32)
    s3 = (x - s1 - s2).astype(BF16).astype(np.float32)
    return s1, s2, s3


def _query_features():
    slopes = np.float32(2.0) ** (-ALIBI_MAX_BIAS * np.arange(1, N_HEADS + 1, dtype=np.float32) / N_HEADS)
    s1, s2, s3 = _bf16_pieces(slopes * np.float32(LOG2E))
    feat = np.zeros((N_HEADS, LANES), np.float32)
    feat[:, HEAD_DIM:HEAD_DIM + N_FEAT] = np.stack([s1, s2, s3, SEL_LEN * s1, SEL_LEN * s2, SEL_LEN * s3], axis=1)
    return jnp.asarray(feat)


def _key_features(pos, width, offset):
    pos = np.asarray(pos)
    a, b = (pos // SEL_LEN).astype(np.float32), (pos % SEL_LEN).astype(np.float32)
    feat = np.zeros((len(pos), width), np.float32)
    feat[:, offset:offset + N_FEAT] = np.stack([b, b, b, a, a, a], axis=1)
    return jnp.asarray(feat)


Q_COLS = N_HEADS * HEAD_DIM
CMP_OFF = Q_COLS
KEY_OFF = CMP_OFF + 2 * LANES
VAL_OFF = KEY_OFF + 2 * LANES
GATE_OFF = VAL_OFF + 2 * LANES
POOL_OFF = GATE_OFF + LANES
POOL_COLS = len(POOL_WINDOWS) * POOL_GROUP_DIM


V_ROWS = HEAD_DIM + 16


def _store_values_t(v_ref, v):
    vt = v.T
    for g in range(N_GROUPS):
        v_ref[0, g, 0:HEAD_DIM, :] = vt[g * HEAD_DIM:(g + 1) * HEAD_DIM, :].astype(BF16)
        v_ref[0, g, HEAD_DIM:V_ROWS, :] = jnp.ones((V_ROWS - HEAD_DIM, v.shape[0]), BF16)


def _proj_kernel(x_ref, g_ref, w_ref, qf_ref, kf_ref, q_ref, kc_ref, vc_ref, ks_ref, kw_ref, vs_ref, vw_ref,
                 gate_ref, u_ref):
    hn = _rms(x_ref[...], g_ref[...]).astype(BF16)
    q_scale = HEAD_DIM ** -0.5 * LOG2E
    low = lax.broadcasted_iota(jnp.int32, (x_ref.shape[0], LANES), 1) < HEAD_DIM
    for j in range(N_HEADS // 4):
        q4 = _dot(hn, w_ref[:, j * 2 * LANES:(j + 1) * 2 * LANES]) * q_scale
        for b in range(2):
            both = q4[:, b * LANES:(b + 1) * LANES]
            h = 4 * j + 2 * b
            q_ref[0, h] = (jnp.where(low, both, 0.0) + qf_ref[h:h + 1, :]).astype(BF16)
            q_ref[0, h + 1] = (jnp.where(low, pltpu.roll(both, HEAD_DIM, 1), 0.0) + qf_ref[h + 1:h + 2, :]).astype(BF16)
    cmp_in = _dot(hn, w_ref[:, CMP_OFF:CMP_OFF + 2 * LANES])
    kc_ref[...] = cmp_in[:, 0:LANES]
    vc_ref[...] = cmp_in[:, LANES:2 * LANES]
    keys = _dot(hn, w_ref[:, KEY_OFF:KEY_OFF + 2 * LANES])
    kf = kf_ref[...]
    for i, k_ref in enumerate((ks_ref, kw_ref)):
        both = keys[:, i * LANES:(i + 1) * LANES]
        k_ref[0, 0] = (jnp.where(low, both, 0.0) + kf).astype(BF16)
        k_ref[0, 1] = (jnp.where(low, pltpu.roll(both, HEAD_DIM, 1), 0.0) + kf).astype(BF16)
    vals = _dot(hn, w_ref[:, VAL_OFF:VAL_OFF + 2 * LANES])
    for i, v_ref in enumerate((vs_ref, vw_ref)):
        _store_values_t(v_ref, vals[:, i * LANES:(i + 1) * LANES])
    gate_ref[...] = jax.nn.sigmoid(_dot(hn, w_ref[:, GATE_OFF:GATE_OFF + LANES]))
    u_ref[...] = _dot(hn, w_ref[:, POOL_OFF:POOL_OFF + POOL_COLS])


def _pack_w_in(w_in):
    w_in = w_in.astype(BF16)
    qw = N_HEADS * HEAD_DIM
    kvw = N_GROUPS * HEAD_DIM
    kc, vc, ks, vs, kw, vw = (w_in[:, qw + i * kvw:qw + (i + 1) * kvw] for i in range(6))
    off = qw + 6 * kvw
    n_g = 3 * N_HEADS
    w_g = jnp.pad(w_in[:, off:off + n_g], ((0, 0), (0, LANES - n_g)))
    w_pool = w_in[:, off + n_g:off + n_g + POOL_COLS]
    w_merge = w_in[:, off + n_g + POOL_COLS:]
    return jnp.concatenate([w_in[:, :qw], kc, vc, ks, kw, vs, vw, w_g, w_pool], axis=1), w_merge


def _proj(x, norm_g, w_packed, batch, seq, *, tm=1024):
    n, d = x.shape
    tiles_per_seq = seq // tm
    row = lambda i: (i, 0)
    const = lambda i: (0, 0)
    per_group = lambda i: (i // tiles_per_seq, 0, i % tiles_per_seq, 0)
    qf = _query_features()
    kf = _key_features(np.arange(seq), LANES, HEAD_DIM)
    flat = lambda width, dtype: (jax.ShapeDtypeStruct((n, width), dtype), pl.BlockSpec((tm, width), row))
    grouped = lambda count: (jax.ShapeDtypeStruct((batch, count, seq, LANES), BF16),
                             pl.BlockSpec((1, count, tm, LANES), per_group))
    values_t = (jax.ShapeDtypeStruct((batch, N_GROUPS, V_ROWS, seq), BF16),
                pl.BlockSpec((1, N_GROUPS, V_ROWS, tm), lambda i: (i // tiles_per_seq, 0, 0, i % tiles_per_seq)))
    outs = [grouped(N_HEADS), flat(LANES, F32), flat(LANES, F32), grouped(N_GROUPS), grouped(N_GROUPS),
            values_t, values_t, flat(LANES, F32), flat(POOL_COLS, F32)]
    return pl.pallas_call(
        _proj_kernel,
        out_shape=[o[0] for o in outs],
        grid=(n // tm,),
        in_specs=[pl.BlockSpec((tm, d), row), pl.BlockSpec((1, d), const),
                  pl.BlockSpec(w_packed.shape, const, pipeline_mode=pl.Buffered(1)),
                  pl.BlockSpec(qf.shape, const), pl.BlockSpec((tm, LANES), lambda i: (i % tiles_per_seq, 0))],
        out_specs=[o[1] for o in outs],
        compiler_params=_params("parallel"),
        name="proj",
    )(x, norm_g, w_packed, qf, kf)


def _compress_kernel(k_ref, v_ref, pos_ref, wk1_ref, wk2_ref, wv1_ref, wv2_ref, feat_ref, ko_ref, vo_ref):
    n_chunk = k_ref.shape[0] // CMP_STRIDE
    hidden = wk2_ref.shape[0]

    def compress(x_ref, w1_ref, w2_ref):
        first = second = None
        for r in range(CMP_STRIDE):
            rows = x_ref[pl.ds(r, n_chunk, stride=CMP_STRIDE), :]
            a = _dot((rows + pos_ref[r:r + 1, :]).astype(BF16), w1_ref[r])
            b = _dot((rows + pos_ref[CMP_STRIDE + r:CMP_STRIDE + r + 1, :]).astype(BF16), w1_ref[CMP_STRIDE + r])
            first = a if first is None else first + a
            second = b if second is None else second + b
        act = jax.nn.gelu(first + pltpu.roll(second, n_chunk - 1, 0)).astype(BF16)
        return [_dot(act[:, g * hidden:(g + 1) * hidden], w2_ref[...]) for g in range(N_GROUPS)]

    for g, k_cmp in enumerate(compress(k_ref, wk1_ref, wk2_ref)):
        ko_ref[0, g] = jnp.concatenate([k_cmp, feat_ref[...]], axis=-1).astype(BF16)
    _store_values_t(vo_ref, jnp.concatenate(compress(v_ref, wv1_ref, wv2_ref), axis=-1))


def _block_diag_w1(w1):
    w = w1.reshape(CMP_LEN, HEAD_DIM, -1)
    z = jnp.zeros_like(w)
    return jnp.concatenate([jnp.concatenate([w, z], axis=2), jnp.concatenate([z, w], axis=2)], axis=1).astype(BF16)


def _compress(kc, vc, pos, wk1, wk2, wv1, wv2, batch, seq):
    n_chunk = seq // CMP_STRIDE
    rows = pl.BlockSpec((seq, LANES), lambda b: (b, 0))
    full = lambda a: pl.BlockSpec(a.shape, lambda b: (0,) * a.ndim, pipeline_mode=pl.Buffered(1))
    pos2 = jnp.concatenate([pos] * N_GROUPS, axis=1)
    feat = _key_features(np.arange(n_chunk) * CMP_STRIDE + CMP_LEN - 1, HEAD_DIM, 0)
    args = (pos2, _block_diag_w1(wk1), wk2.astype(BF16), _block_diag_w1(wv1), wv2.astype(BF16), feat)
    return pl.pallas_call(
        _compress_kernel,
        out_shape=[jax.ShapeDtypeStruct((batch, N_GROUPS, n_chunk, LANES), BF16),
                   jax.ShapeDtypeStruct((batch, N_GROUPS, V_ROWS, n_chunk), BF16)],
        grid=(batch,),
        in_specs=[rows, rows] + [full(a) for a in args],
        out_specs=[pl.BlockSpec((1, N_GROUPS, n_chunk, LANES), lambda b: (b, 0, 0, 0)),
                   pl.BlockSpec((1, N_GROUPS, V_ROWS, n_chunk), lambda b: (b, 0, 0, 0))],
        compiler_params=_params("parallel"),
        name="compress",
    )(kc, vc, *args)


SLC_CHUNK = 256
BLOCKS_PER_CHUNK = SLC_CHUNK // SEL_LEN
HEAD_BLOCKS = 2


def _nsa_kernel(q_ref, kc_ref, vct_ref, ks_ref, vst_ref, kw_ref, vwt_ref, gate_ref, o_ref,
                st_ref, p_ref, mx_ref, gt_ref, selt_ref, list_ref, m_ref, acc_ref, out_ref, *, tq, seq):
    n_sel = seq // SEL_LEN
    n_chunks = seq // SLC_CHUNK
    n_cmp = kc_ref.shape[2]
    n_win = WINDOW + tq
    q0 = pl.program_id(1) * tq
    gt_ref[...] = gate_ref[...].T
    pairs = [(g, g * HEADS_PER_GROUP + 2 * j) for g in range(N_GROUPS) for j in range(HEADS_PER_GROUP // 2)]

    def gate_row(head, branch):
        c = 3 * head + branch
        return gt_ref[c:c + 1, :]

    def out_rows(head):
        g, hh = divmod(head, HEADS_PER_GROUP)
        return hh, slice(g * HEAD_DIM, (g + 1) * HEAD_DIM)

    def stage_scores(keys, n_keys, row0=0):
        for g, head0 in pairs:
            q_pair = q_ref[0, head0:head0 + 2].reshape(2 * tq, LANES)
            scores = _dot_nt(keys(g), q_pair)
            for i in range(2):
                st_ref[head0 + i, row0:row0 + n_keys, :] = scores[:, i * tq:(i + 1) * tq]

    def finish_pair(head0, pv, branch, first, ok=None):
        invs = []
        for i in range(2):
            cols = slice(i * tq, (i + 1) * tq)
            inv = 1.0 / pv[HEAD_DIM:HEAD_DIM + 1, cols]
            if ok is not None:
                inv = jnp.where(ok, inv, 0.0)
            hh, rows = out_rows(head0 + i)
            contrib = pv[0:HEAD_DIM, cols] * (inv * gate_row(head0 + i, branch))
            out_ref[hh, rows, :] = contrib if first else out_ref[hh, rows, :] + contrib
            invs.append(inv)
        return invs

    t_c = q0 + lax.broadcasted_iota(jnp.int32, (n_cmp, tq), 1)
    end_c = lax.broadcasted_iota(jnp.int32, (n_cmp, tq), 0) * CMP_STRIDE + (CMP_LEN - 1)
    bias_c = jnp.where(t_c >= end_c, 0.0, NEG_BIG)
    has_cmp = (q0 + lax.broadcasted_iota(jnp.int32, (1, tq), 1)) >= CMP_LEN - 1
    inv_c = [None] * N_HEADS
    raw = [None] * (N_HEADS // 2)

    def cmp_pair(g, head0):
        for i in range(2):
            x = st_ref[head0 + i, 0:n_cmp, :] + bias_c
            p_ref[head0 // 2, 0:n_cmp, i * tq:(i + 1) * tq] = jnp.exp2(x - jnp.max(x, axis=0, keepdims=True)).astype(BF16)
        both = _dot(vct_ref[0, g], p_ref[head0 // 2, 0:n_cmp, :])
        raw[head0 // 2] = both[V_ROWS:, :]
        inv_c[head0], inv_c[head0 + 1] = finish_pair(head0, both[0:V_ROWS, :], 0, True, has_cmp)

    stage_scores(lambda g: kc_ref[0, g], n_cmp)
    for g, head0 in pairs:
        cmp_pair(g, head0)

    w0 = pl.multiple_of(jnp.maximum(q0 - WINDOW, 0), tq)
    stage_scores(lambda g: kw_ref[0, g, pl.ds(w0, n_win), :], n_win)
    head_keys = HEAD_BLOCKS * SEL_LEN
    stage_scores(lambda g: ks_ref[0, g, 0:head_keys, :], head_keys, n_win)

    dist_w = (q0 + lax.broadcasted_iota(jnp.int32, (n_win, tq), 1)) - (w0 + lax.broadcasted_iota(jnp.int32, (n_win, tq), 0))
    bias_w = jnp.where((dist_w >= 0) & (dist_w < WINDOW), 0.0, NEG_BIG)

    def win_pair(g, head0):
        for i in range(2):
            x = st_ref[head0 + i, 0:n_win, :] + bias_w
            p_ref[head0 // 2, 0:n_win, i * tq:(i + 1) * tq] = jnp.exp2(x - jnp.max(x, axis=0, keepdims=True)).astype(BF16)
        pv = _dot(vwt_ref[0, g, :, pl.ds(w0, n_win)], p_ref[head0 // 2, 0:n_win, :])
        finish_pair(head0, pv, 2, False)

    for g, head0 in pairs:
        win_pair(g, head0)

    blk = lax.broadcasted_iota(jnp.int32, (n_sel, tq), 0)
    cur = (q0 + lax.broadcasted_iota(jnp.int32, (n_sel, tq), 1)) // SEL_LEN
    valid = blk <= cur
    forced = valid & ((blk == 0) | (blk == cur) | (blk == cur - 1))
    blk_f = blk.astype(F32)
    score = []
    for g in range(N_GROUPS):
        total = None
        for head in range(g * HEADS_PER_GROUP, (g + 1) * HEADS_PER_GROUP):
            part = raw[head // 2][:, (head % 2) * tq:(head % 2 + 1) * tq] * inv_c[head]
            total = part if total is None else total + part
        score.append(jnp.where(forced, TOPK_TAKEN, jnp.where(valid, total, -TOPK_BIG)))

    for _ in range(min(N_SELECT, n_sel) - N_FORCED):
        for g in range(N_GROUPS):
            best = jnp.max(score[g], axis=0, keepdims=True)
            pick = jnp.min(jnp.where(score[g] == best, blk_f, float(n_sel)), axis=0, keepdims=True)
            score[g] = jnp.where(blk_f == pick, TOPK_TAKEN, score[g])
    sel = [jnp.where(valid & (score[g] == TOPK_TAKEN), 1.0, 0.0) for g in range(N_GROUPS)]
    for g in range(N_GROUPS):
        selt_ref[g] = jnp.where(blk >= HEAD_BLOCKS, sel[g], 0.0)
    sel_any = jnp.where(blk >= HEAD_BLOCKS, sel[0] + sel[1], 0.0)
    n_items = jnp.int32(0)
    for c in range(n_chunks):
        any_sel = jnp.max(sel_any[c * BLOCKS_PER_CHUNK:(c + 1) * BLOCKS_PER_CHUNK, :])
        list_ref[n_items] = c
        n_items = n_items + (any_sel > 0.5).astype(jnp.int32)

    causal_h = (q0 + lax.broadcasted_iota(jnp.int32, (head_keys, tq), 1)) >= lax.broadcasted_iota(jnp.int32, (head_keys, tq), 0)
    head_rows = slice(n_win, n_win + head_keys)
    for g, head0 in pairs:
        if head0 % HEADS_PER_GROUP == 0:
            sel_keys = jnp.concatenate([jnp.broadcast_to(sel[g][i:i + 1, :], (SEL_LEN, tq)) for i in range(HEAD_BLOCKS)], axis=0)
            bias = jnp.where(causal_h & (sel_keys > 0.5), 0.0, NEG_BIG)
        for i in range(2):
            x = st_ref[head0 + i, head_rows, :] + bias
            m_new = jnp.max(x, axis=0, keepdims=True)
            m_ref[head0 + i:head0 + i + 1, :] = m_new
            p_ref[head0 // 2, head_rows, i * tq:(i + 1) * tq] = jnp.exp2(x - m_new).astype(BF16)
    for g, head0 in pairs:
        acc_ref[head0 // 2] = _dot(vst_ref[0, g, :, 0:head_keys], p_ref[head0 // 2, head_rows, :])

    base = (lax.broadcasted_iota(jnp.int32, (SLC_CHUNK, tq), 1)
            - lax.broadcasted_iota(jnp.int32, (SLC_CHUNK, tq), 0))

    def stage_chunk(item, row0):
        c = list_ref[item]
        k0 = pl.multiple_of(c * SLC_CHUNK, SLC_CHUNK)
        causal = base >= k0 - q0
        for g, head0 in pairs:
            if head0 % HEADS_PER_GROUP == 0:
                sel_keys = jnp.concatenate(
                    [jnp.broadcast_to(selt_ref[g, pl.ds(c * BLOCKS_PER_CHUNK + i, 1), :], (SEL_LEN, tq))
                     for i in range(BLOCKS_PER_CHUNK)], axis=0)
                bias = jnp.where(causal & (sel_keys > 0.5), 0.0, NEG_BIG)
                bias2 = jnp.concatenate([bias, bias], axis=1)
            q_pair = q_ref[0, head0:head0 + 2].reshape(2 * tq, LANES)
            scores = _dot_nt(ks_ref[0, g, pl.ds(k0, SLC_CHUNK), :], q_pair) + bias2
            slot = (row0 // SLC_CHUNK) * (N_HEADS // 2) + head0 // 2
            mx_ref[slot:slot + 1, :] = jnp.max(scores, axis=0, keepdims=True)
            for i in range(2):
                st_ref[head0 + i, row0:row0 + SLC_CHUNK, :] = scores[:, i * tq:(i + 1) * tq]

    def consume_chunk(item, row0):
        k0 = pl.multiple_of(list_ref[item] * SLC_CHUNK, SLC_CHUNK)
        for g, head0 in pairs:
            alphas = []
            for i, head in enumerate((head0, head0 + 1)):
                slot = (row0 // SLC_CHUNK) * (N_HEADS // 2) + head0 // 2
                m_old = m_ref[head:head + 1, :]
                m_new = jnp.maximum(m_old, mx_ref[slot:slot + 1, i * tq:(i + 1) * tq])
                alphas.append(jnp.exp2(m_old - m_new))
                m_ref[head:head + 1, :] = m_new
                x = st_ref[head, row0:row0 + SLC_CHUNK, :]
                p_ref[head0 // 2, 0:SLC_CHUNK, i * tq:(i + 1) * tq] = jnp.exp2(x - m_new).astype(BF16)
            pv = _dot(vst_ref[0, g, :, pl.ds(k0, SLC_CHUNK)], p_ref[head0 // 2, 0:SLC_CHUNK, :])
            acc_ref[head0 // 2] = acc_ref[head0 // 2] * jnp.concatenate(alphas, axis=1) + pv

    last = jnp.maximum(n_items - 1, 0)
    stage_chunk(0, 0)

    def two_chunks(j, carry):
        stage_chunk(jnp.minimum(2 * j + 1, last), SLC_CHUNK)
        consume_chunk(2 * j, 0)

        @pl.when(2 * j + 1 < n_items)
        def _():
            stage_chunk(jnp.minimum(2 * j + 2, last), 0)
            consume_chunk(2 * j + 1, SLC_CHUNK)
        return carry

    lax.fori_loop(0, (n_items + 1) // 2, two_chunks, 0)
    for g, head0 in pairs:
        finish_pair(head0, acc_ref[head0 // 2], 1, False)

    for hh in range(HEADS_PER_GROUP):
        o_ref[:, hh * LANES:(hh + 1) * LANES] = out_ref[hh].T.astype(BF16)


def _overlap_t(seq):
    n_cmp = (seq - CMP_LEN) // CMP_STRIDE + 1
    n_sel = seq // SEL_LEN
    cs = np.arange(n_cmp)[:, None] * CMP_STRIDE
    ss = np.arange(n_sel)[None, :] * SEL_LEN
    ov = np.clip(np.minimum(cs + CMP_LEN, ss + SEL_LEN) - np.maximum(cs, ss), 0, None) / CMP_LEN
    out = np.zeros((n_sel, seq // CMP_STRIDE), np.float32)
    out[:, :n_cmp] = ov.T
    return jnp.asarray(out, dtype=BF16)


def _nsa(q, kc, vct, ks, vst, kw, vwt, gates, *, tq=128):
    batch, _, seq, _ = q.shape
    n_cmp = kc.shape[2]
    n_sel = seq // SEL_LEN
    assert seq % SLC_CHUNK == 0 and seq >= WINDOW + tq and SLC_CHUNK % tq == 0 and tq == LANES
    assert n_sel <= 2 * SEL_LEN
    ovt = jnp.broadcast_to(_overlap_t(seq), (batch, N_GROUPS, n_sel, n_cmp))
    vct = jnp.concatenate([vct, ovt], axis=2)
    keys = lambda n: pl.BlockSpec((1, N_GROUPS, n, LANES), lambda b, i: (b, 0, 0, 0))
    vals = lambda n: pl.BlockSpec((1, N_GROUPS, V_ROWS, n), lambda b, i: (b, 0, 0, 0))
    tiles = seq // tq
    s_rows = max(n_cmp, WINDOW + tq + HEAD_BLOCKS * SEL_LEN, 2 * SLC_CHUNK)
    return pl.pallas_call(
        functools.partial(_nsa_kernel, tq=tq, seq=seq),
        out_shape=jax.ShapeDtypeStruct((batch * seq, HEADS_PER_GROUP * LANES), BF16),
        grid=(batch, tiles),
        in_specs=[
            pl.BlockSpec((1, N_HEADS, tq, LANES), lambda b, i: (b, 0, i, 0)),
            keys(n_cmp), pl.BlockSpec((1, N_GROUPS, V_ROWS + n_sel, n_cmp), lambda b, i: (b, 0, 0, 0)),
            keys(seq), vals(seq), keys(seq), vals(seq),
            pl.BlockSpec((tq, LANES), lambda b, i: (b * tiles + i, 0)),
        ],
        out_specs=pl.BlockSpec((tq, HEADS_PER_GROUP * LANES), lambda b, i: (b * tiles + i, 0)),
        scratch_shapes=[
            pltpu.VMEM((N_HEADS, s_rows, tq), F32),
            pltpu.VMEM((N_HEADS // 2, s_rows, 2 * tq), BF16),
            pltpu.VMEM((N_HEADS, 2 * tq), F32),
            pltpu.VMEM((LANES, tq), F32),
            pltpu.VMEM((N_GROUPS, n_sel, tq), F32),
            pltpu.SMEM((seq // SLC_CHUNK,), jnp.int32),
            pltpu.VMEM((2 * N_HEADS, tq), F32),
            pltpu.VMEM((N_HEADS // 2, V_ROWS, 2 * tq), F32),
            pltpu.VMEM((HEADS_PER_GROUP, N_GROUPS * HEAD_DIM, tq), F32),
        ],
        compiler_params=_params("parallel", "arbitrary"),
        name="nsa",
    )(q, kc, vct, ks, vst, kw, vwt, gates)


HALO = max(POOL_WINDOWS)
assert all(w & (w - 1) == 0 for w in POOL_WINDOWS)


def _merge_kernel(x_ref, g_ref, u_ref, halo_ref, wgm_ref, on_ref, pw_ref, ps_ref, wbp_ref, wbn_ref, wo_ref,
                  o_ref, ext_ref, *, tm, seq):
    d = x_ref.shape[1]
    gm = jax.nn.sigmoid(_dot(_rms(x_ref[...], g_ref[...]).astype(BF16), wgm_ref[...]))
    b = _dot(on_ref[...], wbn_ref[...])
    pos0 = (pl.program_id(0) * tm) % seq
    ext_ref[0:HALO, :] = jnp.where(pos0 == 0, 0.0, halo_ref[...])
    ext_ref[HALO:HALO + tm, :] = u_ref[...]
    pos = (pos0 + lax.broadcasted_iota(jnp.int32, (tm, POOL_GROUP_DIM), 0)).astype(F32)
    mixed = []
    for gi, w in enumerate(POOL_WINDOWS):
        cols = slice(gi * POOL_GROUP_DIM, (gi + 1) * POOL_GROUP_DIM)
        run = ext_ref[:, cols]
        for k in range(w.bit_length() - 1):
            run = run + pltpu.roll(run, 1 << k, 0)
        u = ext_ref[HALO:HALO + tm, cols]
        delta = run[HALO:HALO + tm] / jnp.minimum(pos + 1.0, float(w)) - u
        mixed.append(_dot(delta.astype(BF16), pw_ref[gi]) * ps_ref[:, cols])
    mixed = jnp.concatenate(mixed, axis=-1).astype(BF16)
    a = _dot(mixed, wbp_ref[...])
    merged = gm[:, 0:d] * a + gm[:, d:2 * d] * b
    o_ref[...] = x_ref[...] + _dot(merged.astype(BF16), wo_ref[...])


def _merge(x, norm_g, u, w_gm, o_nsa, pool_w, pool_scale, w_bp, w_bn, w_out, seq, *, tm=512):
    n, d = x.shape
    row = lambda i: (i, 0)
    full = lambda a: pl.BlockSpec(a.shape, lambda i: (0,) * a.ndim, pipeline_mode=pl.Buffered(1))
    halo_blocks = tm // HALO
    return pl.pallas_call(
        functools.partial(_merge_kernel, tm=tm, seq=seq),
        out_shape=jax.ShapeDtypeStruct((n, d), F32),
        grid=(n // tm,),
        in_specs=[
            pl.BlockSpec((tm, d), row),
            full(norm_g),
            pl.BlockSpec((tm, POOL_COLS), row),
            pl.BlockSpec((HALO, POOL_COLS), lambda i: (jnp.maximum(i * halo_blocks - 1, 0), 0)),
            full(w_gm),
            pl.BlockSpec((tm, d), row),
            full(pool_w), full(pool_scale), full(w_bp), full(w_bn), full(w_out),
        ],
        out_specs=pl.BlockSpec((tm, d), row),
        scratch_shapes=[pltpu.VMEM((HALO + tm, POOL_COLS), F32)],
        compiler_params=_params("parallel"),
        name="merge",
    )(x, norm_g, u, u, w_gm, o_nsa, pool_w, pool_scale, w_bp, w_bn, w_out)


def kernel(x, ffn1_norm, ffn1_w_gate, ffn1_w_up, ffn1_w_down, mix_norm, w_in, cmp_pos, cmp_k_w1, cmp_k_w2, cmp_v_w1, cmp_v_w2, pool_w, pool_scale, w_branch_pool, w_branch_nsa, w_out, ffn2_norm, ffn2_w_gate, ffn2_w_up, ffn2_w_down, final_norm):
    batch, seq, d = x.shape
    depth = w_in.shape[0]
    xf = x.reshape(batch * seq, d)
    bf = lambda a: a.astype(BF16)
    row = lambda a: a.reshape(1, -1)
    for l in range(depth):
        xf = _ffn(xf, row(ffn1_norm[l]), bf(ffn1_w_gate[l]), bf(ffn1_w_up[l]), bf(ffn1_w_down[l]),
                  row(final_norm), final_norm=False)

        w_packed, w_gm = _pack_w_in(w_in[l])
        q, kc, vc, ks, kw, vs, vw, gates, u = _proj(xf, row(mix_norm[l]), w_packed, batch, seq)
        k_cmp, v_cmp = _compress(kc, vc, cmp_pos[l], cmp_k_w1[l], cmp_k_w2[l], cmp_v_w1[l], cmp_v_w2[l], batch, seq)
        o_nsa = _nsa(q, k_cmp, v_cmp, ks, vs, kw, vw, gates)

        w_bn = w_branch_nsa[l].reshape(N_GROUPS, HEADS_PER_GROUP, HEAD_DIM, d).transpose(1, 0, 2, 3).reshape(-1, d)
        xf = _merge(xf, row(mix_norm[l]), u, w_gm, o_nsa, bf(pool_w[l]), row(pool_scale[l]), bf(w_branch_pool[l]), bf(w_bn),
                    bf(w_out[l]), seq)

        xf = _ffn(xf, row(ffn2_norm[l]), bf(ffn2_w_gate[l]), bf(ffn2_w_up[l]), bf(ffn2_w_down[l]),
                  row(final_norm), final_norm=(l == depth - 1))
    return xf.reshape(batch, seq, d)
```

```python
import functools

import jax
import jax.numpy as jnp
import numpy as np
from jax import lax
from jax.experimental import pallas as pl
from jax.experimental.pallas import tpu as pltpu

N_HEADS = 16
N_GROUPS = 2
HEADS_PER_GROUP = N_HEADS // N_GROUPS
HEAD_DIM = 64
CMP_LEN = 32
CMP_STRIDE = 16
SEL_LEN = 64
N_SELECT = 16
WINDOW = 512
POOL_WINDOWS = (2, 4, 8, 16)
POOL_GROUP_DIM = 128
RMS_EPS = 1e-6
ALIBI_MAX_BIAS = 8.0

LANES = 128
NEG_BIG = -1e30
TOPK_BIG = 1e30
N_FORCED = 3
TOPK_TAKEN = -float(2 ** 101)
VMEM_LIMIT = 48 * 1024 * 1024

BF16 = jnp.bfloat16
F32 = jnp.float32


def _dot(a, b):
    return jnp.dot(a, b, preferred_element_type=F32)


def _dot_nt(a, b):
    return lax.dot_general(a, b, (((1,), (1,)), ((), ())), preferred_element_type=F32)


def _rms(x, g):
    return x * lax.rsqrt(jnp.mean(x * x, axis=-1, keepdims=True) + RMS_EPS) * g


def _params(*sem):
    return pltpu.CompilerParams(dimension_semantics=sem, vmem_limit_bytes=VMEM_LIMIT)


def _ffn_kernel(x_ref, g_ref, wg_ref, wu_ref, wd_ref, fin_ref, o_ref, *, final_norm):
    x = x_ref[...]
    xn = _rms(x, g_ref[...]).astype(BF16)
    gate = _dot(xn, wg_ref[...])
    up = _dot(xn, wu_ref[...])
    act = (gate * jax.nn.sigmoid(gate)) * up
    y = x + 0.5 * _dot(act.astype(BF16), wd_ref[...])
    if final_norm:
        y = _rms(y, fin_ref[...])
    o_ref[...] = y


def _ffn(x, norm_g, wg, wu, wd, fin_g, *, final_norm, tm=512):
    n, d = x.shape
    row = lambda i: (i, 0)
    resident = lambda a: pl.BlockSpec(a.shape, lambda i: (0, 0), pipeline_mode=pl.Buffered(1))
    return pl.pallas_call(
        functools.partial(_ffn_kernel, final_norm=final_norm),
        out_shape=jax.ShapeDtypeStruct((n, d), F32),
        grid=(n // tm,),
        in_specs=[pl.BlockSpec((tm, d), row), resident(norm_g), resident(wg), resident(wu), resident(wd),
                  resident(fin_g)],
        out_specs=pl.BlockSpec((tm, d), row),
        compiler_params=_params("parallel"),
        name="ffn",
    )(x, norm_g, wg, wu, wd, fin_g)


LOG2E = 1.4426950408889634
N_FEAT = 6


def _bf16_pieces(x):
    x = np.asarray(x, np.float32)
    s1 = x.astype(BF16).astype(np.float32)
    s2 = (x - s1).astype(BF16).astype(np.float32)
    s3 = (x - s1 - s2).astype(BF16).astype(np.float32)
    return s1, s2, s3


def _query_features():
    slopes = np.float32(2.0) ** (-ALIBI_MAX_BIAS * np.arange(1, N_HEADS + 1, dtype=np.float32) / N_HEADS)
    s1, s2, s3 = _bf16_pieces(slopes * np.float32(LOG2E))
    feat = np.zeros((N_HEADS, LANES), np.float32)
    feat[:, HEAD_DIM:HEAD_DIM + N_FEAT] = np.stack([s1, s2, s3, SEL_LEN * s1, SEL_LEN * s2, SEL_LEN * s3], axis=1)
    return jnp.asarray(feat)


def _key_features(pos, width, offset):
    pos = np.asarray(pos)
    a, b = (pos // SEL_LEN).astype(np.float32), (pos % SEL_LEN).astype(np.float32)
    feat = np.zeros((len(pos), width), np.float32)
    feat[:, offset:offset + N_FEAT] = np.stack([b, b, b, a, a, a], axis=1)
    return jnp.asarray(feat)


Q_COLS = N_HEADS * HEAD_DIM
CMP_OFF = Q_COLS
KEY_OFF = CMP_OFF + 2 * LANES
VAL_OFF = KEY_OFF + 2 * LANES
GATE_OFF = VAL_OFF + 2 * LANES
POOL_OFF = GATE_OFF + LANES
POOL_COLS = len(POOL_WINDOWS) * POOL_GROUP_DIM


V_ROWS = HEAD_DIM + 16


def _store_values_t(v_ref, v):
    vt = v.T
    for g in range(N_GROUPS):
        v_ref[0, g, 0:HEAD_DIM, :] = vt[g * HEAD_DIM:(g + 1) * HEAD_DIM, :].astype(BF16)
        v_ref[0, g, HEAD_DIM:V_ROWS, :] = jnp.ones((V_ROWS - HEAD_DIM, v.shape[0]), BF16)


def _proj_kernel(x_ref, g_ref, w_ref, qf_ref, kf_ref, q_ref, kc_ref, vc_ref, ks_ref, kw_ref, vs_ref, vw_ref,
                 gate_ref, u_ref):
    hn = _rms(x_ref[...], g_ref[...]).astype(BF16)
    q_scale = HEAD_DIM ** -0.5 * LOG2E
    low = lax.broadcasted_iota(jnp.int32, (x_ref.shape[0], LANES), 1) < HEAD_DIM
    for j in range(N_HEADS // 4):
        q4 = _dot(hn, w_ref[:, j * 2 * LANES:(j + 1) * 2 * LANES]) * q_scale
        for b in range(2):
            both = q4[:, b * LANES:(b + 1) * LANES]
            h = 4 * j + 2 * b
            q_ref[0, h] = (jnp.where(low, both, 0.0) + qf_ref[h:h + 1, :]).astype(BF16)
            q_ref[0, h + 1] = (jnp.where(low, pltpu.roll(both, HEAD_DIM, 1), 0.0) + qf_ref[h + 1:h + 2, :]).astype(BF16)
    cmp_in = _dot(hn, w_ref[:, CMP_OFF:CMP_OFF + 2 * LANES])
    kc_ref[...] = cmp_in[:, 0:LANES]
    vc_ref[...] = cmp_in[:, LANES:2 * LANES]
    keys = _dot(hn, w_ref[:, KEY_OFF:KEY_OFF + 2 * LANES])
    kf = kf_ref[...]
    for i, k_ref in enumerate((ks_ref, kw_ref)):
        both = keys[:, i * LANES:(i + 1) * LANES]
        k_ref[0, 0] = (jnp.where(low, both, 0.0) + kf).astype(BF16)
        k_ref[0, 1] = (jnp.where(low, pltpu.roll(both, HEAD_DIM, 1), 0.0) + kf).astype(BF16)
    vals = _dot(hn, w_ref[:, VAL_OFF:VAL_OFF + 2 * LANES])
    for i, v_ref in enumerate((vs_ref, vw_ref)):
        _store_values_t(v_ref, vals[:, i * LANES:(i + 1) * LANES])
    gate_ref[...] = jax.nn.sigmoid(_dot(hn, w_ref[:, GATE_OFF:GATE_OFF + LANES]))
    u_ref[...] = _dot(hn, w_ref[:, POOL_OFF:POOL_OFF + POOL_COLS])


def _pack_w_in(w_in):
    w_in = w_in.astype(BF16)
    qw = N_HEADS * HEAD_DIM
    kvw = N_GROUPS * HEAD_DIM
    kc, vc, ks, vs, kw, vw = (w_in[:, qw + i * kvw:qw + (i + 1) * kvw] for i in range(6))
    off = qw + 6 * kvw
    n_g = 3 * N_HEADS
    w_g = jnp.pad(w_in[:, off:off + n_g], ((0, 0), (0, LANES - n_g)))
    w_pool = w_in[:, off + n_g:off + n_g + POOL_COLS]
    w_merge = w_in[:, off + n_g + POOL_COLS:]
    return jnp.concatenate([w_in[:, :qw], kc, vc, ks, kw, vs, vw, w_g, w_pool], axis=1), w_merge


def _proj(x, norm_g, w_packed, batch, seq, *, tm=1024):
    n, d = x.shape
    tiles_per_seq = seq // tm
    row = lambda i: (i, 0)
    const = lambda i: (0, 0)
    per_group = lambda i: (i // tiles_per_seq, 0, i % tiles_per_seq, 0)
    qf = _query_features()
    kf = _key_features(np.arange(seq), LANES, HEAD_DIM)
    flat = lambda width, dtype: (jax.ShapeDtypeStruct((n, width), dtype), pl.BlockSpec((tm, width), row))
    grouped = lambda count: (jax.ShapeDtypeStruct((batch, count, seq, LANES), BF16),
                             pl.BlockSpec((1, count, tm, LANES), per_group))
    values_t = (jax.ShapeDtypeStruct((batch, N_GROUPS, V_ROWS, seq), BF16),
                pl.BlockSpec((1, N_GROUPS, V_ROWS, tm), lambda i: (i // tiles_per_seq, 0, 0, i % tiles_per_seq)))
    outs = [grouped(N_HEADS), flat(LANES, F32), flat(LANES, F32), grouped(N_GROUPS), grouped(N_GROUPS),
            values_t, values_t, flat(LANES, F32), flat(POOL_COLS, F32)]
    return pl.pallas_call(
        _proj_kernel,
        out_shape=[o[0] for o in outs],
        grid=(n // tm,),
        in_specs=[pl.BlockSpec((tm, d), row), pl.BlockSpec((1, d), const),
                  pl.BlockSpec(w_packed.shape, const, pipeline_mode=pl.Buffered(1)),
                  pl.BlockSpec(qf.shape, const), pl.BlockSpec((tm, LANES), lambda i: (i % tiles_per_seq, 0))],
        out_specs=[o[1] for o in outs],
        compiler_params=_params("parallel"),
        name="proj",
    )(x, norm_g, w_packed, qf, kf)


def _compress_kernel(k_ref, v_ref, pos_ref, wk1_ref, wk2_ref, wv1_ref, wv2_ref, feat_ref, ko_ref, vo_ref):
    n_chunk = k_ref.shape[0] // CMP_STRIDE
    hidden = wk2_ref.shape[0]

    def compress(x_ref, w1_ref, w2_ref):
        first = second = None
        for r in range(CMP_STRIDE):
            rows = x_ref[pl.ds(r, n_chunk, stride=CMP_STRIDE), :]
            a = _dot((rows + pos_ref[r:r + 1, :]).astype(BF16), w1_ref[r])
            b = _dot((rows + pos_ref[CMP_STRIDE + r:CMP_STRIDE + r + 1, :]).astype(BF16), w1_ref[CMP_STRIDE + r])
            first = a if first is None else first + a
            second = b if second is None else second + b
        act = jax.nn.gelu(first + pltpu.roll(second, n_chunk - 1, 0)).astype(BF16)
        return [_dot(act[:, g * hidden:(g + 1) * hidden], w2_ref[...]) for g in range(N_GROUPS)]

    for g, k_cmp in enumerate(compress(k_ref, wk1_ref, wk2_ref)):
        ko_ref[0, g] = jnp.concatenate([k_cmp, feat_ref[...]], axis=-1).astype(BF16)
    _store_values_t(vo_ref, jnp.concatenate(compress(v_ref, wv1_ref, wv2_ref), axis=-1))


def _block_diag_w1(w1):
    w = w1.reshape(CMP_LEN, HEAD_DIM, -1)
    z = jnp.zeros_like(w)
    return jnp.concatenate([jnp.concatenate([w, z], axis=2), jnp.concatenate([z, w], axis=2)], axis=1).astype(BF16)


def _compress(kc, vc, pos, wk1, wk2, wv1, wv2, batch, seq):
    n_chunk = seq // CMP_STRIDE
    rows = pl.BlockSpec((seq, LANES), lambda b: (b, 0))
    full = lambda a: pl.BlockSpec(a.shape, lambda b: (0,) * a.ndim, pipeline_mode=pl.Buffered(1))
    pos2 = jnp.concatenate([pos] * N_GROUPS, axis=1)
    feat = _key_features(np.arange(n_chunk) * CMP_STRIDE + CMP_LEN - 1, HEAD_DIM, 0)
    args = (pos2, _block_diag_w1(wk1), wk2.astype(BF16), _block_diag_w1(wv1), wv2.astype(BF16), feat)
    return pl.pallas_call(
        _compress_kernel,
        out_shape=[jax.ShapeDtypeStruct((batch, N_GROUPS, n_chunk, LANES), BF16),
                   jax.ShapeDtypeStruct((batch, N_GROUPS, V_ROWS, n_chunk), BF16)],
        grid=(batch,),
        in_specs=[rows, rows] + [full(a) for a in args],
        out_specs=[pl.BlockSpec((1, N_GROUPS, n_chunk, LANES), lambda b: (b, 0, 0, 0)),
                   pl.BlockSpec((1, N_GROUPS, V_ROWS, n_chunk), lambda b: (b, 0, 0, 0))],
        compiler_params=_params("parallel"),
        name="compress",
    )(kc, vc, *args)


SLC_CHUNK = 256
BLOCKS_PER_CHUNK = SLC_CHUNK // SEL_LEN
HEAD_BLOCKS = 2


def _nsa_kernel(q_ref, kc_ref, vct_ref, ks_ref, vst_ref, kw_ref, vwt_ref, gate_ref, o_ref,
                st_ref, p_ref, mx_ref, gt_ref, selt_ref, list_ref, m_ref, acc_ref, out_ref, *, tq, seq):
    n_sel = seq // SEL_LEN
    n_chunks = seq // SLC_CHUNK
    n_cmp = kc_ref.shape[2]
    n_win = WINDOW + tq
    q0 = pl.program_id(1) * tq
    gt_ref[...] = gate_ref[...].T
    pairs = [(g, g * HEADS_PER_GROUP + 2 * j) for g in range(N_GROUPS) for j in range(HEADS_PER_GROUP // 2)]

    def gate_row(head, branch):
        c = 3 * head + branch
        return gt_ref[c:c + 1, :]

    def out_rows(head):
        g, hh = divmod(head, HEADS_PER_GROUP)
        return hh, slice(g * HEAD_DIM, (g + 1) * HEAD_DIM)

    def stage_scores(keys, n_keys, row0=0):
        for g, head0 in pairs:
            q_pair = q_ref[0, head0:head0 + 2].reshape(2 * tq, LANES)
            scores = _dot_nt(keys(g), q_pair)
            for i in range(2):
                st_ref[head0 + i, row0:row0 + n_keys, :] = scores[:, i * tq:(i + 1) * tq]

    def finish_pair(head0, pv, branch, first, ok=None):
        invs = []
        for i in range(2):
            cols = slice(i * tq, (i + 1) * tq)
            inv = 1.0 / pv[HEAD_DIM:HEAD_DIM + 1, cols]
            if ok is not None:
                inv = jnp.where(ok, inv, 0.0)
            hh, rows = out_rows(head0 + i)
            contrib = pv[0:HEAD_DIM, cols] * (inv * gate_row(head0 + i, branch))
            out_ref[hh, rows, :] = contrib if first else out_ref[hh, rows, :] + contrib
            invs.append(inv)
        return invs

    t_c = q0 + lax.broadcasted_iota(jnp.int32, (n_cmp, tq), 1)
    end_c = lax.broadcasted_iota(jnp.int32, (n_cmp, tq), 0) * CMP_STRIDE + (CMP_LEN - 1)
    bias_c = jnp.where(t_c >= end_c, 0.0, NEG_BIG)
    has_cmp = (q0 + lax.broadcasted_iota(jnp.int32, (1, tq), 1)) >= CMP_LEN - 1
    inv_c = [None] * N_HEADS
    raw = [None] * (N_HEADS // 2)

    def cmp_pair(g, head0):
        for i in range(2):
            x = st_ref[head0 + i, 0:n_cmp, :] + bias_c
            p_ref[head0 // 2, 0:n_cmp, i * tq:(i + 1) * tq] = jnp.exp2(x - jnp.max(x, axis=0, keepdims=True)).astype(BF16)
        both = _dot(vct_ref[0, g], p_ref[head0 // 2, 0:n_cmp, :])
        raw[head0 // 2] = both[V_ROWS:, :]
        inv_c[head0], inv_c[head0 + 1] = finish_pair(head0, both[0:V_ROWS, :], 0, True, has_cmp)

    stage_scores(lambda g: kc_ref[0, g], n_cmp)
    for g, head0 in pairs:
        cmp_pair(g, head0)

    w0 = pl.multiple_of(jnp.maximum(q0 - WINDOW, 0), tq)
    stage_scores(lambda g: kw_ref[0, g, pl.ds(w0, n_win), :], n_win)
    head_keys = HEAD_BLOCKS * SEL_LEN
    stage_scores(lambda g: ks_ref[0, g, 0:head_keys, :], head_keys, n_win)

    dist_w = (q0 + lax.broadcasted_iota(jnp.int32, (n_win, tq), 1)) - (w0 + lax.broadcasted_iota(jnp.int32, (n_win, tq), 0))
    bias_w = jnp.where((dist_w >= 0) & (dist_w < WINDOW), 0.0, NEG_BIG)

    def win_pair(g, head0):
        for i in range(2):
            x = st_ref[head0 + i, 0:n_win, :] + bias_w
            p_ref[head0 // 2, 0:n_win, i * tq:(i + 1) * tq] = jnp.exp2(x - jnp.max(x, axis=0, keepdims=True)).astype(BF16)
        pv = _dot(vwt_ref[0, g, :, pl.ds(w0, n_win)], p_ref[head0 // 2, 0:n_win, :])
        finish_pair(head0, pv, 2, False)

    for g, head0 in pairs:
        win_pair(g, head0)

    blk = lax.broadcasted_iota(jnp.int32, (n_sel, tq), 0)
    cur = (q0 + lax.broadcasted_iota(jnp.int32, (n_sel, tq), 1)) // SEL_LEN
    valid = blk <= cur
    forced = valid & ((blk == 0) | (blk == cur) | (blk == cur - 1))
    blk_f = blk.astype(F32)
    score = []
    for g in range(N_GROUPS):
        total = None
        for head in range(g * HEADS_PER_GROUP, (g + 1) * HEADS_PER_GROUP):
            part = raw[head // 2][:, (head % 2) * tq:(head % 2 + 1) * tq] * inv_c[head]
            total = part if total is None else total + part
        score.append(jnp.where(forced, TOPK_TAKEN, jnp.where(valid, total, -TOPK_BIG)))

    for _ in range(min(N_SELECT, n_sel) - N_FORCED):
        for g in range(N_GROUPS):
            best = jnp.max(score[g], axis=0, keepdims=True)
            pick = jnp.min(jnp.where(score[g] == best, blk_f, float(n_sel)), axis=0, keepdims=True)
            score[g] = jnp.where(blk_f == pick, TOPK_TAKEN, score[g])
    sel = [jnp.where(valid & (score[g] == TOPK_TAKEN), 1.0, 0.0) for g in range(N_GROUPS)]
    for g in range(N_GROUPS):
        selt_ref[g] = jnp.where(blk >= HEAD_BLOCKS, sel[g], 0.0)
    sel_any = jnp.where(blk >= HEAD_BLOCKS, sel[0] + sel[1], 0.0)
    n_items = jnp.int32(0)
    for c in range(n_chunks):
        any_sel = jnp.max(sel_any[c * BLOCKS_PER_CHUNK:(c + 1) * BLOCKS_PER_CHUNK, :])
        list_ref[n_items] = c
        n_items = n_items + (any_sel > 0.5).astype(jnp.int32)

    causal_h = (q0 + lax.broadcasted_iota(jnp.int32, (head_keys, tq), 1)) >= lax.broadcasted_iota(jnp.int32, (head_keys, tq), 0)
    head_rows = slice(n_win, n_win + head_keys)
    for g, head0 in pairs:
        if head0 % HEADS_PER_GROUP == 0:
            sel_keys = jnp.concatenate([jnp.broadcast_to(sel[g][i:i + 1, :], (SEL_LEN, tq)) for i in range(HEAD_BLOCKS)], axis=0)
            bias = jnp.where(causal_h & (sel_keys > 0.5), 0.0, NEG_BIG)
        for i in range(2):
            x = st_ref[head0 + i, head_rows, :] + bias
            m_new = jnp.max(x, axis=0, keepdims=True)
            m_ref[head0 + i:head0 + i + 1, :] = m_new
            p_ref[head0 // 2, head_rows, i * tq:(i + 1) * tq] = jnp.exp2(x - m_new).astype(BF16)
    for g, head0 in pairs:
        acc_ref[head0 // 2] = _dot(vst_ref[0, g, :, 0:head_keys], p_ref[head0 // 2, head_rows, :])

    base = (lax.broadcasted_iota(jnp.int32, (SLC_CHUNK, tq), 1)
            - lax.broadcasted_iota(jnp.int32, (SLC_CHUNK, tq), 0))

    def stage_chunk(item, row0):
        c = list_ref[item]
        k0 = pl.multiple_of(c * SLC_CHUNK, SLC_CHUNK)
        causal = base >= k0 - q0
        for g, head0 in pairs:
            if head0 % HEADS_PER_GROUP == 0:
                sel_keys = jnp.concatenate(
                    [jnp.broadcast_to(selt_ref[g, pl.ds(c * BLOCKS_PER_CHUNK + i, 1), :], (SEL_LEN, tq))
                     for i in range(BLOCKS_PER_CHUNK)], axis=0)
                bias = jnp.where(causal & (sel_keys > 0.5), 0.0, NEG_BIG)
                bias2 = jnp.concatenate([bias, bias], axis=1)
            q_pair = q_ref[0, head0:head0 + 2].reshape(2 * tq, LANES)
            scores = _dot_nt(ks_ref[0, g, pl.ds(k0, SLC_CHUNK), :], q_pair) + bias2
            slot = (row0 // SLC_CHUNK) * (N_HEADS // 2) + head0 // 2
            mx_ref[slot:slot + 1, :] = jnp.max(scores, axis=0, keepdims=True)
            for i in range(2):
                st_ref[head0 + i, row0:row0 + SLC_CHUNK, :] = scores[:, i * tq:(i + 1) * tq]

    def consume_chunk(item, row0):
        k0 = pl.multiple_of(list_ref[item] * SLC_CHUNK, SLC_CHUNK)
        for g, head0 in pairs:
            alphas = []
            for i, head in enumerate((head0, head0 + 1)):
                slot = (row0 // SLC_CHUNK) * (N_HEADS // 2) + head0 // 2
                m_old = m_ref[head:head + 1, :]
                m_new = jnp.maximum(m_old, mx_ref[slot:slot + 1, i * tq:(i + 1) * tq])
                alphas.append(jnp.exp2(m_old - m_new))
                m_ref[head:head + 1, :] = m_new
                x = st_ref[head, row0:row0 + SLC_CHUNK, :]
                p_ref[head0 // 2, 0:SLC_CHUNK, i * tq:(i + 1) * tq] = jnp.exp2(x - m_new).astype(BF16)
            pv = _dot(vst_ref[0, g, :, pl.ds(k0, SLC_CHUNK)], p_ref[head0 // 2, 0:SLC_CHUNK, :])
            acc_ref[head0 // 2] = acc_ref[head0 // 2] * jnp.concatenate(alphas, axis=1) + pv

    last = jnp.maximum(n_items - 1, 0)
    stage_chunk(0, 0)

    def two_chunks(j, carry):
        stage_chunk(jnp.minimum(2 * j + 1, last), SLC_CHUNK)
        consume_chunk(2 * j, 0)

        @pl.when(2 * j + 1 < n_items)
        def _():
            stage_chunk(jnp.minimum(2 * j + 2, last), 0)
            consume_chunk(2 * j + 1, SLC_CHUNK)
        return carry

    lax.fori_loop(0, (n_items + 1) // 2, two_chunks, 0)
    for g, head0 in pairs:
        finish_pair(head0, acc_ref[head0 // 2], 1, False)

    for hh in range(HEADS_PER_GROUP):
        o_ref[:, hh * LANES:(hh + 1) * LANES] = out_ref[hh].T.astype(BF16)


def _overlap_t(seq):
    n_cmp = (seq - CMP_LEN) // CMP_STRIDE + 1
    n_sel = seq // SEL_LEN
    cs = np.arange(n_cmp)[:, None] * CMP_STRIDE
    ss = np.arange(n_sel)[None, :] * SEL_LEN
    ov = np.clip(np.minimum(cs + CMP_LEN, ss + SEL_LEN) - np.maximum(cs, ss), 0, None) / CMP_LEN
    out = np.zeros((n_sel, seq // CMP_STRIDE), np.float32)
    out[:, :n_cmp] = ov.T
    return jnp.asarray(out, dtype=BF16)


def _nsa(q, kc, vct, ks, vst, kw, vwt, gates, *, tq=128):
    batch, _, seq, _ = q.shape
    n_cmp = kc.shape[2]
    n_sel = seq // SEL_LEN
    assert seq % SLC_CHUNK == 0 and seq >= WINDOW + tq and SLC_CHUNK % tq == 0 and tq == LANES
    assert n_sel <= 2 * SEL_LEN
    ovt = jnp.broadcast_to(_overlap_t(seq), (batch, N_GROUPS, n_sel, n_cmp))
    vct = jnp.concatenate([vct, ovt], axis=2)
    keys = lambda n: pl.BlockSpec((1, N_GROUPS, n, LANES), lambda b, i: (b, 0, 0, 0))
    vals = lambda n: pl.BlockSpec((1, N_GROUPS, V_ROWS, n), lambda b, i: (b, 0, 0, 0))
    tiles = seq // tq
    s_rows = max(n_cmp, WINDOW + tq + HEAD_BLOCKS * SEL_LEN, 2 * SLC_CHUNK)
    return pl.pallas_call(
        functools.partial(_nsa_kernel, tq=tq, seq=seq),
        out_shape=jax.ShapeDtypeStruct((batch * seq, HEADS_PER_GROUP * LANES), BF16),
        grid=(batch, tiles),
        in_specs=[
            pl.BlockSpec((1, N_HEADS, tq, LANES), lambda b, i: (b, 0, i, 0)),
            keys(n_cmp), pl.BlockSpec((1, N_GROUPS, V_ROWS + n_sel, n_cmp), lambda b, i: (b, 0, 0, 0)),
            keys(seq), vals(seq), keys(seq), vals(seq),
            pl.BlockSpec((tq, LANES), lambda b, i: (b * tiles + i, 0)),
        ],
        out_specs=pl.BlockSpec((tq, HEADS_PER_GROUP * LANES), lambda b, i: (b * tiles + i, 0)),
        scratch_shapes=[
            pltpu.VMEM((N_HEADS, s_rows, tq), F32),
            pltpu.VMEM((N_HEADS // 2, s_rows, 2 * tq), BF16),
            pltpu.VMEM((N_HEADS, 2 * tq), F32),
            pltpu.VMEM((LANES, tq), F32),
            pltpu.VMEM((N_GROUPS, n_sel, tq), F32),
            pltpu.SMEM((seq // SLC_CHUNK,), jnp.int32),
            pltpu.VMEM((2 * N_HEADS, tq), F32),
            pltpu.VMEM((N_HEADS // 2, V_ROWS, 2 * tq), F32),
            pltpu.VMEM((HEADS_PER_GROUP, N_GROUPS * HEAD_DIM, tq), F32),
        ],
        compiler_params=_params("parallel", "arbitrary"),
        name="nsa",
    )(q, kc, vct, ks, vst, kw, vwt, gates)


HALO = max(POOL_WINDOWS)
assert all(w & (w - 1) == 0 for w in POOL_WINDOWS)


def _merge_kernel(x_ref, g_ref, u_ref, halo_ref, wgm_ref, on_ref, pw_ref, ps_ref, wbp_ref, wbn_ref, wo_ref,
                  o_ref, ext_ref, *, tm, seq):
    d = x_ref.shape[1]
    gm = jax.nn.sigmoid(_dot(_rms(x_ref[...], g_ref[...]).astype(BF16), wgm_ref[...]))
    b = _dot(on_ref[...], wbn_ref[...])
    pos0 = (pl.program_id(0) * tm) % seq
    ext_ref[0:HALO, :] = jnp.where(pos0 == 0, 0.0, halo_ref[...])
    ext_ref[HALO:HALO + tm, :] = u_ref[...]
    pos = (pos0 + lax.broadcasted_iota(jnp.int32, (tm, POOL_GROUP_DIM), 0)).astype(F32)
    mixed = []
    for gi, w in enumerate(POOL_WINDOWS):
        cols = slice(gi * POOL_GROUP_DIM, (gi + 1) * POOL_GROUP_DIM)
        run = ext_ref[:, cols]
        for k in range(w.bit_length() - 1):
            run = run + pltpu.roll(run, 1 << k, 0)
        u = ext_ref[HALO:HALO + tm, cols]
        delta = run[HALO:HALO + tm] / jnp.minimum(pos + 1.0, float(w)) - u
        mixed.append(_dot(delta.astype(BF16), pw_ref[gi]) * ps_ref[:, cols])
    mixed = jnp.concatenate(mixed, axis=-1).astype(BF16)
    a = _dot(mixed, wbp_ref[...])
    merged = gm[:, 0:d] * a + gm[:, d:2 * d] * b
    o_ref[...] = x_ref[...] + _dot(merged.astype(BF16), wo_ref[...])


def _merge(x, norm_g, u, w_gm, o_nsa, pool_w, pool_scale, w_bp, w_bn, w_out, seq, *, tm=1024):
    n, d = x.shape
    row = lambda i: (i, 0)
    full = lambda a: pl.BlockSpec(a.shape, lambda i: (0,) * a.ndim, pipeline_mode=pl.Buffered(1))
    halo_blocks = tm // HALO
    return pl.pallas_call(
        functools.partial(_merge_kernel, tm=tm, seq=seq),
        out_shape=jax.ShapeDtypeStruct((n, d), F32),
        grid=(n // tm,),
        in_specs=[
            pl.BlockSpec((tm, d), row),
            full(norm_g),
            pl.BlockSpec((tm, POOL_COLS), row),
            pl.BlockSpec((HALO, POOL_COLS), lambda i: (jnp.maximum(i * halo_blocks - 1, 0), 0)),
            full(w_gm),
            pl.BlockSpec((tm, d), row),
            full(pool_w), full(pool_scale), full(w_bp), full(w_bn), full(w_out),
        ],
        out_specs=pl.BlockSpec((tm, d), row),
        scratch_shapes=[pltpu.VMEM((HALO + tm, POOL_COLS), F32)],
        compiler_params=_params("parallel"),
        name="merge",
    )(x, norm_g, u, u, w_gm, o_nsa, pool_w, pool_scale, w_bp, w_bn, w_out)


def kernel(x, ffn1_norm, ffn1_w_gate, ffn1_w_up, ffn1_w_down, mix_norm, w_in, cmp_pos, cmp_k_w1, cmp_k_w2, cmp_v_w1, cmp_v_w2, pool_w, pool_scale, w_branch_pool, w_branch_nsa, w_out, ffn2_norm, ffn2_w_gate, ffn2_w_up, ffn2_w_down, final_norm):
    batch, seq, d = x.shape
    depth = w_in.shape[0]
    xf = x.reshape(batch * seq, d)
    bf = lambda a: a.astype(BF16)
    row = lambda a: a.reshape(1, -1)
    for l in range(depth):
        xf = _ffn(xf, row(ffn1_norm[l]), bf(ffn1_w_gate[l]), bf(ffn1_w_up[l]), bf(ffn1_w_down[l]),
                  row(final_norm), final_norm=False)

        w_packed, w_gm = _pack_w_in(w_in[l])
        q, kc, vc, ks, kw, vs, vw, gates, u = _proj(xf, row(mix_norm[l]), w_packed, batch, seq)
        k_cmp, v_cmp = _compress(kc, vc, cmp_pos[l], cmp_k_w1[l], cmp_k_w2[l], cmp_v_w1[l], cmp_v_w2[l], batch, seq)
        o_nsa = _nsa(q, k_cmp, v_cmp, ks, vs, kw, vw, gates)

        w_bn = w_branch_nsa[l].reshape(N_GROUPS, HEADS_PER_GROUP, HEAD_DIM, d).transpose(1, 0, 2, 3).reshape(-1, d)
        xf = _merge(xf, row(mix_norm[l]), u, w_gm, o_nsa, bf(pool_w[l]), row(pool_scale[l]), bf(w_branch_pool[l]), bf(w_bn),
                    bf(w_out[l]), seq)

        xf = _ffn(xf, row(ffn2_norm[l]), bf(ffn2_w_gate[l]), bf(ffn2_w_up[l]), bf(ffn2_w_down[l]),
                  row(final_norm), final_norm=(l == depth - 1))
    return xf.reshape(batch, seq, d)
```

```python
import functools

import jax
import jax.numpy as jnp
import numpy as np
from jax import lax
from jax.experimental import pallas as pl
from jax.experimental.pallas import tpu as pltpu

N_HEADS = 16
N_GROUPS = 2
HEADS_PER_GROUP = N_HEADS // N_GROUPS
HEAD_DIM = 64
CMP_LEN = 32
CMP_STRIDE = 16
SEL_LEN = 64
N_SELECT = 16
WINDOW = 512
POOL_WINDOWS = (2, 4, 8, 16)
POOL_GROUP_DIM = 128
RMS_EPS = 1e-6
ALIBI_MAX_BIAS = 8.0

LANES = 128
NEG_BIG = -1e30
TOPK_BIG = 1e30
N_FORCED = 3
TOPK_TAKEN = -float(2 ** 101)
VMEM_LIMIT = 48 * 1024 * 1024

BF16 = jnp.bfloat16
F32 = jnp.float32


def _dot(a, b):
    return jnp.dot(a, b, preferred_element_type=F32)


def _dot_nt(a, b):
    return lax.dot_general(a, b, (((1,), (1,)), ((), ())), preferred_element_type=F32)


def _rms(x, g):
    return x * lax.rsqrt(jnp.mean(x * x, axis=-1, keepdims=True) + RMS_EPS) * g


def _params(*sem):
    return pltpu.CompilerParams(dimension_semantics=sem, vmem_limit_bytes=VMEM_LIMIT)


def _ffn_kernel(x_ref, g_ref, wg_ref, wu_ref, wd_ref, fin_ref, o_ref, *, final_norm):
    x = x_ref[...]
    xn = _rms(x, g_ref[...]).astype(BF16)
    gate = _dot(xn, wg_ref[...])
    up = _dot(xn, wu_ref[...])
    act = (gate * jax.nn.sigmoid(gate)) * up
    y = x + 0.5 * _dot(act.astype(BF16), wd_ref[...])
    if final_norm:
        y = _rms(y, fin_ref[...])
    o_ref[...] = y


def _ffn(x, norm_g, wg, wu, wd, fin_g, layer, *, final_norm, tm=512):
    n, d = x.shape
    row = lambda i: (i, 0)
    resident = lambda a: pl.BlockSpec(a.shape, lambda i: (0, 0), pipeline_mode=pl.Buffered(1))
    of_layer = lambda a: pl.BlockSpec((None,) + a.shape[1:], lambda i: (layer, 0, 0), pipeline_mode=pl.Buffered(1))
    return pl.pallas_call(
        functools.partial(_ffn_kernel, final_norm=final_norm),
        out_shape=jax.ShapeDtypeStruct((n, d), F32),
        grid=(n // tm,),
        in_specs=[pl.BlockSpec((tm, d), row), resident(norm_g), of_layer(wg), of_layer(wu), of_layer(wd),
                  resident(fin_g)],
        out_specs=pl.BlockSpec((tm, d), row),
        compiler_params=_params("parallel"),
        name="ffn",
    )(x, norm_g, wg, wu, wd, fin_g)


LOG2E = 1.4426950408889634
N_FEAT = 6


def _bf16_pieces(x):
    x = np.asarray(x, np.float32)
    s1 = x.astype(BF16).astype(np.float32)
    s2 = (x - s1).astype(BF16).astype(np.float32)
    s3 = (x - s1 - s2).astype(BF16).astype(np.float32)
    return s1, s2, s3


def _query_features():
    slopes = np.float32(2.0) ** (-ALIBI_MAX_BIAS * np.arange(1, N_HEADS + 1, dtype=np.float32) / N_HEADS)
    s1, s2, s3 = _bf16_pieces(slopes * np.float32(LOG2E))
    feat = np.zeros((N_HEADS, LANES), np.float32)
    feat[:, HEAD_DIM:HEAD_DIM + N_FEAT] = np.stack([s1, s2, s3, SEL_LEN * s1, SEL_LEN * s2, SEL_LEN * s3], axis=1)
    return jnp.asarray(feat)


def _key_features(pos, width, offset):
    pos = np.asarray(pos)
    a, b = (pos // SEL_LEN).astype(np.float32), (pos % SEL_LEN).astype(np.float32)
    feat = np.zeros((len(pos), width), np.float32)
    feat[:, offset:offset + N_FEAT] = np.stack([b, b, b, a, a, a], axis=1)
    return jnp.asarray(feat)


Q_COLS = N_HEADS * HEAD_DIM
CMP_OFF = Q_COLS
KEY_OFF = CMP_OFF + 2 * LANES
VAL_OFF = KEY_OFF + 2 * LANES
GATE_OFF = VAL_OFF + 2 * LANES
POOL_OFF = GATE_OFF + LANES
POOL_COLS = len(POOL_WINDOWS) * POOL_GROUP_DIM


V_ROWS = HEAD_DIM + 16


def _store_values_t(v_ref, v):
    vt = v.T
    for g in range(N_GROUPS):
        v_ref[0, g, 0:HEAD_DIM, :] = vt[g * HEAD_DIM:(g + 1) * HEAD_DIM, :].astype(BF16)
        v_ref[0, g, HEAD_DIM:V_ROWS, :] = jnp.ones((V_ROWS - HEAD_DIM, v.shape[0]), BF16)


def _proj_kernel(x_ref, g_ref, w_ref, qf_ref, kf_ref, q_ref, kc_ref, vc_ref, ks_ref, kw_ref, vs_ref, vw_ref,
                 gate_ref, u_ref):
    hn = _rms(x_ref[...], g_ref[...]).astype(BF16)
    q_scale = HEAD_DIM ** -0.5 * LOG2E
    low = lax.broadcasted_iota(jnp.int32, (x_ref.shape[0], LANES), 1) < HEAD_DIM
    for j in range(N_HEADS // 4):
        q4 = _dot(hn, w_ref[:, j * 2 * LANES:(j + 1) * 2 * LANES]) * q_scale
        for b in range(2):
            both = q4[:, b * LANES:(b + 1) * LANES]
            h = 4 * j + 2 * b
            q_ref[0, h] = (jnp.where(low, both, 0.0) + qf_ref[h:h + 1, :]).astype(BF16)
            q_ref[0, h + 1] = (jnp.where(low, pltpu.roll(both, HEAD_DIM, 1), 0.0) + qf_ref[h + 1:h + 2, :]).astype(BF16)
    cmp_in = _dot(hn, w_ref[:, CMP_OFF:CMP_OFF + 2 * LANES])
    kc_ref[...] = cmp_in[:, 0:LANES]
    vc_ref[...] = cmp_in[:, LANES:2 * LANES]
    keys = _dot(hn, w_ref[:, KEY_OFF:KEY_OFF + 2 * LANES])
    kf = kf_ref[...]
    for i, k_ref in enumerate((ks_ref, kw_ref)):
        both = keys[:, i * LANES:(i + 1) * LANES]
        k_ref[0, 0] = (jnp.where(low, both, 0.0) + kf).astype(BF16)
        k_ref[0, 1] = (jnp.where(low, pltpu.roll(both, HEAD_DIM, 1), 0.0) + kf).astype(BF16)
    vals = _dot(hn, w_ref[:, VAL_OFF:VAL_OFF + 2 * LANES])
    for i, v_ref in enumerate((vs_ref, vw_ref)):
        _store_values_t(v_ref, vals[:, i * LANES:(i + 1) * LANES])
    gate_ref[...] = jax.nn.sigmoid(_dot(hn, w_ref[:, GATE_OFF:GATE_OFF + LANES]))
    u_ref[...] = _dot(hn, w_ref[:, POOL_OFF:POOL_OFF + POOL_COLS])


def _pack_w_in(w_in):
    w_in = w_in.astype(BF16)
    qw = N_HEADS * HEAD_DIM
    kvw = N_GROUPS * HEAD_DIM
    kc, vc, ks, vs, kw, vw = (w_in[:, qw + i * kvw:qw + (i + 1) * kvw] for i in range(6))
    off = qw + 6 * kvw
    n_g = 3 * N_HEADS
    w_g = jnp.pad(w_in[:, off:off + n_g], ((0, 0), (0, LANES - n_g)))
    w_pool = w_in[:, off + n_g:off + n_g + POOL_COLS]
    w_merge = w_in[:, off + n_g + POOL_COLS:]
    return jnp.concatenate([w_in[:, :qw], kc, vc, ks, kw, vs, vw, w_g, w_pool], axis=1), w_merge


def _proj(x, norm_g, w_packed, batch, seq, *, tm=1024):
    n, d = x.shape
    tiles_per_seq = seq // tm
    row = lambda i: (i, 0)
    const = lambda i: (0, 0)
    per_group = lambda i: (i // tiles_per_seq, 0, i % tiles_per_seq, 0)
    qf = _query_features()
    kf = _key_features(np.arange(seq), LANES, HEAD_DIM)
    flat = lambda width, dtype: (jax.ShapeDtypeStruct((n, width), dtype), pl.BlockSpec((tm, width), row))
    grouped = lambda count: (jax.ShapeDtypeStruct((batch, count, seq, LANES), BF16),
                             pl.BlockSpec((1, count, tm, LANES), per_group))
    values_t = (jax.ShapeDtypeStruct((batch, N_GROUPS, V_ROWS, seq), BF16),
                pl.BlockSpec((1, N_GROUPS, V_ROWS, tm), lambda i: (i // tiles_per_seq, 0, 0, i % tiles_per_seq)))
    outs = [grouped(N_HEADS), flat(LANES, F32), flat(LANES, F32), grouped(N_GROUPS), grouped(N_GROUPS),
            values_t, values_t, flat(LANES, F32), flat(POOL_COLS, F32)]
    return pl.pallas_call(
        _proj_kernel,
        out_shape=[o[0] for o in outs],
        grid=(n // tm,),
        in_specs=[pl.BlockSpec((tm, d), row), pl.BlockSpec((1, d), const),
                  pl.BlockSpec(w_packed.shape, const, pipeline_mode=pl.Buffered(1)),
                  pl.BlockSpec(qf.shape, const), pl.BlockSpec((tm, LANES), lambda i: (i % tiles_per_seq, 0))],
        out_specs=[o[1] for o in outs],
        compiler_params=_params("parallel"),
        name="proj",
    )(x, norm_g, w_packed, qf, kf)


def _compress_kernel(k_ref, v_ref, pos_ref, wk1_ref, wk2_ref, wv1_ref, wv2_ref, feat_ref, ko_ref, vo_ref):
    n_chunk = k_ref.shape[0] // CMP_STRIDE
    hidden = wk2_ref.shape[0]

    def compress(x_ref, w1_ref, w2_ref):
        first = second = None
        for r in range(CMP_STRIDE):
            rows = x_ref[pl.ds(r, n_chunk, stride=CMP_STRIDE), :]
            a = _dot((rows + pos_ref[r:r + 1, :]).astype(BF16), w1_ref[r])
            b = _dot((rows + pos_ref[CMP_STRIDE + r:CMP_STRIDE + r + 1, :]).astype(BF16), w1_ref[CMP_STRIDE + r])
            first = a if first is None else first + a
            second = b if second is None else second + b
        act = jax.nn.gelu(first + pltpu.roll(second, n_chunk - 1, 0)).astype(BF16)
        return [_dot(act[:, g * hidden:(g + 1) * hidden], w2_ref[...]) for g in range(N_GROUPS)]

    for g, k_cmp in enumerate(compress(k_ref, wk1_ref, wk2_ref)):
        ko_ref[0, g] = jnp.concatenate([k_cmp, feat_ref[...]], axis=-1).astype(BF16)
    _store_values_t(vo_ref, jnp.concatenate(compress(v_ref, wv1_ref, wv2_ref), axis=-1))


def _block_diag_w1(w1):
    w = w1.reshape(CMP_LEN, HEAD_DIM, -1)
    z = jnp.zeros_like(w)
    return jnp.concatenate([jnp.concatenate([w, z], axis=2), jnp.concatenate([z, w], axis=2)], axis=1).astype(BF16)


def _compress(kc, vc, pos, wk1, wk2, wv1, wv2, batch, seq):
    n_chunk = seq // CMP_STRIDE
    rows = pl.BlockSpec((seq, LANES), lambda b: (b, 0))
    full = lambda a: pl.BlockSpec(a.shape, lambda b: (0,) * a.ndim, pipeline_mode=pl.Buffered(1))
    pos2 = jnp.concatenate([pos] * N_GROUPS, axis=1)
    feat = _key_features(np.arange(n_chunk) * CMP_STRIDE + CMP_LEN - 1, HEAD_DIM, 0)
    args = (pos2, _block_diag_w1(wk1), wk2.astype(BF16), _block_diag_w1(wv1), wv2.astype(BF16), feat)
    return pl.pallas_call(
        _compress_kernel,
        out_shape=[jax.ShapeDtypeStruct((batch, N_GROUPS, n_chunk, LANES), BF16),
                   jax.ShapeDtypeStruct((batch, N_GROUPS, V_ROWS, n_chunk), BF16)],
        grid=(batch,),
        in_specs=[rows, rows] + [full(a) for a in args],
        out_specs=[pl.BlockSpec((1, N_GROUPS, n_chunk, LANES), lambda b: (b, 0, 0, 0)),
                   pl.BlockSpec((1, N_GROUPS, V_ROWS, n_chunk), lambda b: (b, 0, 0, 0))],
        compiler_params=_params("parallel"),
        name="compress",
    )(kc, vc, *args)


SLC_CHUNK = 256
BLOCKS_PER_CHUNK = SLC_CHUNK // SEL_LEN
HEAD_BLOCKS = 2


def _nsa_kernel(q_ref, kc_ref, vct_ref, ks_ref, vst_ref, kw_ref, vwt_ref, gate_ref, o_ref,
                st_ref, p_ref, mx_ref, gt_ref, selt_ref, list_ref, m_ref, acc_ref, out_ref, *, tq, seq):
    n_sel = seq // SEL_LEN
    n_chunks = seq // SLC_CHUNK
    n_cmp = kc_ref.shape[2]
    n_win = WINDOW + tq
    q0 = pl.program_id(1) * tq
    gt_ref[...] = gate_ref[...].T
    pairs = [(g, g * HEADS_PER_GROUP + 2 * j) for g in range(N_GROUPS) for j in range(HEADS_PER_GROUP // 2)]

    def gate_row(head, branch):
        c = 3 * head + branch
        return gt_ref[c:c + 1, :]

    def out_rows(head):
        g, hh = divmod(head, HEADS_PER_GROUP)
        return hh, slice(g * HEAD_DIM, (g + 1) * HEAD_DIM)

    def stage_scores(keys, n_keys, row0=0):
        for g, head0 in pairs:
            q_pair = q_ref[0, head0:head0 + 2].reshape(2 * tq, LANES)
            scores = _dot_nt(keys(g), q_pair)
            for i in range(2):
                st_ref[head0 + i, row0:row0 + n_keys, :] = scores[:, i * tq:(i + 1) * tq]

    def finish_pair(head0, pv, branch, first, ok=None):
        invs = []
        for i in range(2):
            cols = slice(i * tq, (i + 1) * tq)
            inv = 1.0 / pv[HEAD_DIM:HEAD_DIM + 1, cols]
            if ok is not None:
                inv = jnp.where(ok, inv, 0.0)
            hh, rows = out_rows(head0 + i)
            contrib = pv[0:HEAD_DIM, cols] * (inv * gate_row(head0 + i, branch))
            out_ref[hh, rows, :] = contrib if first else out_ref[hh, rows, :] + contrib
            invs.append(inv)
        return invs

    t_c = q0 + lax.broadcasted_iota(jnp.int32, (n_cmp, tq), 1)
    end_c = lax.broadcasted_iota(jnp.int32, (n_cmp, tq), 0) * CMP_STRIDE + (CMP_LEN - 1)
    bias_c = jnp.where(t_c >= end_c, 0.0, NEG_BIG)
    has_cmp = (q0 + lax.broadcasted_iota(jnp.int32, (1, tq), 1)) >= CMP_LEN - 1
    inv_c = [None] * N_HEADS
    raw = [None] * (N_HEADS // 2)

    def cmp_pair(g, head0):
        for i in range(2):
            x = st_ref[head0 + i, 0:n_cmp, :] + bias_c
            p_ref[head0 // 2, 0:n_cmp, i * tq:(i + 1) * tq] = jnp.exp2(x - jnp.max(x, axis=0, keepdims=True)).astype(BF16)
        both = _dot(vct_ref[0, g], p_ref[head0 // 2, 0:n_cmp, :])
        raw[head0 // 2] = both[V_ROWS:, :]
        inv_c[head0], inv_c[head0 + 1] = finish_pair(head0, both[0:V_ROWS, :], 0, True, has_cmp)

    stage_scores(lambda g: kc_ref[0, g], n_cmp)
    for g, head0 in pairs:
        cmp_pair(g, head0)

    w0 = pl.multiple_of(jnp.maximum(q0 - WINDOW, 0), tq)
    stage_scores(lambda g: kw_ref[0, g, pl.ds(w0, n_win), :], n_win)
    head_keys = HEAD_BLOCKS * SEL_LEN
    stage_scores(lambda g: ks_ref[0, g, 0:head_keys, :], head_keys, n_win)

    dist_w = (q0 + lax.broadcasted_iota(jnp.int32, (n_win, tq), 1)) - (w0 + lax.broadcasted_iota(jnp.int32, (n_win, tq), 0))
    bias_w = jnp.where((dist_w >= 0) & (dist_w < WINDOW), 0.0, NEG_BIG)

    def win_pair(g, head0):
        for i in range(2):
            x = st_ref[head0 + i, 0:n_win, :] + bias_w
            p_ref[head0 // 2, 0:n_win, i * tq:(i + 1) * tq] = jnp.exp2(x - jnp.max(x, axis=0, keepdims=True)).astype(BF16)
        pv = _dot(vwt_ref[0, g, :, pl.ds(w0, n_win)], p_ref[head0 // 2, 0:n_win, :])
        finish_pair(head0, pv, 2, False)

    for g, head0 in pairs:
        win_pair(g, head0)

    blk = lax.broadcasted_iota(jnp.int32, (n_sel, tq), 0)
    cur = (q0 + lax.broadcasted_iota(jnp.int32, (n_sel, tq), 1)) // SEL_LEN
    valid = blk <= cur
    forced = valid & ((blk == 0) | (blk == cur) | (blk == cur - 1))
    blk_f = blk.astype(F32)
    score = []
    for g in range(N_GROUPS):
        total = None
        for head in range(g * HEADS_PER_GROUP, (g + 1) * HEADS_PER_GROUP):
            part = raw[head // 2][:, (head % 2) * tq:(head % 2 + 1) * tq] * inv_c[head]
            total = part if total is None else total + part
        score.append(jnp.where(forced, TOPK_TAKEN, jnp.where(valid, total, -TOPK_BIG)))

    for _ in range(min(N_SELECT, n_sel) - N_FORCED):
        for g in range(N_GROUPS):
            best = jnp.max(score[g], axis=0, keepdims=True)
            pick = jnp.min(jnp.where(score[g] == best, blk_f, float(n_sel)), axis=0, keepdims=True)
            score[g] = jnp.where(blk_f == pick, TOPK_TAKEN, score[g])
    sel = [jnp.where(valid & (score[g] == TOPK_TAKEN), 1.0, 0.0) for g in range(N_GROUPS)]
    for g in range(N_GROUPS):
        selt_ref[g] = jnp.where(blk >= HEAD_BLOCKS, sel[g], 0.0)
    sel_any = jnp.where(blk >= HEAD_BLOCKS, sel[0] + sel[1], 0.0)
    n_items = jnp.int32(0)
    for c in range(n_chunks):
        any_sel = jnp.max(sel_any[c * BLOCKS_PER_CHUNK:(c + 1) * BLOCKS_PER_CHUNK, :])
        list_ref[n_items] = c
        n_items = n_items + (any_sel > 0.5).astype(jnp.int32)

    causal_h = (q0 + lax.broadcasted_iota(jnp.int32, (head_keys, tq), 1)) >= lax.broadcasted_iota(jnp.int32, (head_keys, tq), 0)
    head_rows = slice(n_win, n_win + head_keys)
    for g, head0 in pairs:
        if head0 % HEADS_PER_GROUP == 0:
            sel_keys = jnp.concatenate([jnp.broadcast_to(sel[g][i:i + 1, :], (SEL_LEN, tq)) for i in range(HEAD_BLOCKS)], axis=0)
            bias = jnp.where(causal_h & (sel_keys > 0.5), 0.0, NEG_BIG)
        for i in range(2):
            x = st_ref[head0 + i, head_rows, :] + bias
            m_new = jnp.max(x, axis=0, keepdims=True)
            m_ref[head0 + i:head0 + i + 1, :] = m_new
            p_ref[head0 // 2, head_rows, i * tq:(i + 1) * tq] = jnp.exp2(x - m_new).astype(BF16)
    for g, head0 in pairs:
        acc_ref[head0 // 2] = _dot(vst_ref[0, g, :, 0:head_keys], p_ref[head0 // 2, head_rows, :])

    base = (lax.broadcasted_iota(jnp.int32, (SLC_CHUNK, tq), 1)
            - lax.broadcasted_iota(jnp.int32, (SLC_CHUNK, tq), 0))

    def stage_chunk(item, row0):
        c = list_ref[item]
        k0 = pl.multiple_of(c * SLC_CHUNK, SLC_CHUNK)
        causal = base >= k0 - q0
        for g, head0 in pairs:
            if head0 % HEADS_PER_GROUP == 0:
                sel_keys = jnp.concatenate(
                    [jnp.broadcast_to(selt_ref[g, pl.ds(c * BLOCKS_PER_CHUNK + i, 1), :], (SEL_LEN, tq))
                     for i in range(BLOCKS_PER_CHUNK)], axis=0)
                bias = jnp.where(causal & (sel_keys > 0.5), 0.0, NEG_BIG)
                bias2 = jnp.concatenate([bias, bias], axis=1)
            q_pair = q_ref[0, head0:head0 + 2].reshape(2 * tq, LANES)
            scores = _dot_nt(ks_ref[0, g, pl.ds(k0, SLC_CHUNK), :], q_pair) + bias2
            slot = (row0 // SLC_CHUNK) * (N_HEADS // 2) + head0 // 2
            mx_ref[slot:slot + 1, :] = jnp.max(scores, axis=0, keepdims=True)
            for i in range(2):
                st_ref[head0 + i, row0:row0 + SLC_CHUNK, :] = scores[:, i * tq:(i + 1) * tq]

    def consume_chunk(item, row0):
        k0 = pl.multiple_of(list_ref[item] * SLC_CHUNK, SLC_CHUNK)
        for g, head0 in pairs:
            alphas = []
            for i, head in enumerate((head0, head0 + 1)):
                slot = (row0 // SLC_CHUNK) * (N_HEADS // 2) + head0 // 2
                m_old = m_ref[head:head + 1, :]
                m_new = jnp.maximum(m_old, mx_ref[slot:slot + 1, i * tq:(i + 1) * tq])
                alphas.append(jnp.exp2(m_old - m_new))
                m_ref[head:head + 1, :] = m_new
                x = st_ref[head, row0:row0 + SLC_CHUNK, :]
                p_ref[head0 // 2, 0:SLC_CHUNK, i * tq:(i + 1) * tq] = jnp.exp2(x - m_new).astype(BF16)
            pv = _dot(vst_ref[0, g, :, pl.ds(k0, SLC_CHUNK)], p_ref[head0 // 2, 0:SLC_CHUNK, :])
            acc_ref[head0 // 2] = acc_ref[head0 // 2] * jnp.concatenate(alphas, axis=1) + pv

    last = jnp.maximum(n_items - 1, 0)
    stage_chunk(0, 0)

    def two_chunks(j, carry):
        stage_chunk(jnp.minimum(2 * j + 1, last), SLC_CHUNK)
        consume_chunk(2 * j, 0)

        @pl.when(2 * j + 1 < n_items)
        def _():
            stage_chunk(jnp.minimum(2 * j + 2, last), 0)
            consume_chunk(2 * j + 1, SLC_CHUNK)
        return carry

    lax.fori_loop(0, (n_items + 1) // 2, two_chunks, 0)
    for g, head0 in pairs:
        finish_pair(head0, acc_ref[head0 // 2], 1, False)

    for hh in range(HEADS_PER_GROUP):
        o_ref[:, hh * LANES:(hh + 1) * LANES] = out_ref[hh].T.astype(BF16)


def _overlap_t(seq):
    n_cmp = (seq - CMP_LEN) // CMP_STRIDE + 1
    n_sel = seq // SEL_LEN
    cs = np.arange(n_cmp)[:, None] * CMP_STRIDE
    ss = np.arange(n_sel)[None, :] * SEL_LEN
    ov = np.clip(np.minimum(cs + CMP_LEN, ss + SEL_LEN) - np.maximum(cs, ss), 0, None) / CMP_LEN
    out = np.zeros((n_sel, seq // CMP_STRIDE), np.float32)
    out[:, :n_cmp] = ov.T
    return jnp.asarray(out, dtype=BF16)


def _nsa(q, kc, vct, ks, vst, kw, vwt, gates, *, tq=128):
    batch, _, seq, _ = q.shape
    n_cmp = kc.shape[2]
    n_sel = seq // SEL_LEN
    assert seq % SLC_CHUNK == 0 and seq >= WINDOW + tq and SLC_CHUNK % tq == 0 and tq == LANES
    assert n_sel <= 2 * SEL_LEN
    ovt = jnp.broadcast_to(_overlap_t(seq), (batch, N_GROUPS, n_sel, n_cmp))
    vct = jnp.concatenate([vct, ovt], axis=2)
    keys = lambda n: pl.BlockSpec((1, N_GROUPS, n, LANES), lambda b, i: (b, 0, 0, 0))
    vals = lambda n: pl.BlockSpec((1, N_GROUPS, V_ROWS, n), lambda b, i: (b, 0, 0, 0))
    tiles = seq // tq
    s_rows = max(n_cmp, WINDOW + tq + HEAD_BLOCKS * SEL_LEN, 2 * SLC_CHUNK)
    return pl.pallas_call(
        functools.partial(_nsa_kernel, tq=tq, seq=seq),
        out_shape=jax.ShapeDtypeStruct((batch * seq, HEADS_PER_GROUP * LANES), BF16),
        grid=(batch, tiles),
        in_specs=[
            pl.BlockSpec((1, N_HEADS, tq, LANES), lambda b, i: (b, 0, i, 0)),
            keys(n_cmp), pl.BlockSpec((1, N_GROUPS, V_ROWS + n_sel, n_cmp), lambda b, i: (b, 0, 0, 0)),
            keys(seq), vals(seq), keys(seq), vals(seq),
            pl.BlockSpec((tq, LANES), lambda b, i: (b * tiles + i, 0)),
        ],
        out_specs=pl.BlockSpec((tq, HEADS_PER_GROUP * LANES), lambda b, i: (b * tiles + i, 0)),
        scratch_shapes=[
            pltpu.VMEM((N_HEADS, s_rows, tq), F32),
            pltpu.VMEM((N_HEADS // 2, s_rows, 2 * tq), BF16),
            pltpu.VMEM((N_HEADS, 2 * tq), F32),
            pltpu.VMEM((LANES, tq), F32),
            pltpu.VMEM((N_GROUPS, n_sel, tq), F32),
            pltpu.SMEM((seq // SLC_CHUNK,), jnp.int32),
            pltpu.VMEM((2 * N_HEADS, tq), F32),
            pltpu.VMEM((N_HEADS // 2, V_ROWS, 2 * tq), F32),
            pltpu.VMEM((HEADS_PER_GROUP, N_GROUPS * HEAD_DIM, tq), F32),
        ],
        compiler_params=_params("parallel", "arbitrary"),
        name="nsa",
    )(q, kc, vct, ks, vst, kw, vwt, gates)


HALO = max(POOL_WINDOWS)
assert all(w & (w - 1) == 0 for w in POOL_WINDOWS)


def _merge_kernel(x_ref, g_ref, u_ref, halo_ref, wgm_ref, on_ref, pw_ref, ps_ref, wbp_ref, wbn_ref, wo_ref,
                  o_ref, ext_ref, *, tm, seq):
    d = x_ref.shape[1]
    gm = jax.nn.sigmoid(_dot(_rms(x_ref[...], g_ref[...]).astype(BF16), wgm_ref[...]))
    b = _dot(on_ref[...], wbn_ref[...])
    pos0 = (pl.program_id(0) * tm) % seq
    ext_ref[0:HALO, :] = jnp.where(pos0 == 0, 0.0, halo_ref[...])
    ext_ref[HALO:HALO + tm, :] = u_ref[...]
    pos = (pos0 + lax.broadcasted_iota(jnp.int32, (tm, POOL_GROUP_DIM), 0)).astype(F32)
    mixed = []
    for gi, w in enumerate(POOL_WINDOWS):
        cols = slice(gi * POOL_GROUP_DIM, (gi + 1) * POOL_GROUP_DIM)
        run = ext_ref[:, cols]
        for k in range(w.bit_length() - 1):
            run = run + pltpu.roll(run, 1 << k, 0)
        u = ext_ref[HALO:HALO + tm, cols]
        delta = run[HALO:HALO + tm] / jnp.minimum(pos + 1.0, float(w)) - u
        mixed.append(_dot(delta.astype(BF16), pw_ref[gi]) * ps_ref[:, cols])
    mixed = jnp.concatenate(mixed, axis=-1).astype(BF16)
    a = _dot(mixed, wbp_ref[...])
    merged = gm[:, 0:d] * a + gm[:, d:2 * d] * b
    o_ref[...] = x_ref[...] + _dot(merged.astype(BF16), wo_ref[...])


def _merge(x, norm_g, u, w_gm, o_nsa, pool_w, pool_scale, w_bp, w_bn, w_out, seq, *, tm=1024):
    n, d = x.shape
    row = lambda i: (i, 0)
    full = lambda a: pl.BlockSpec(a.shape, lambda i: (0,) * a.ndim, pipeline_mode=pl.Buffered(1))
    halo_blocks = tm // HALO
    return pl.pallas_call(
        functools.partial(_merge_kernel, tm=tm, seq=seq),
        out_shape=jax.ShapeDtypeStruct((n, d), F32),
        grid=(n // tm,),
        in_specs=[
            pl.BlockSpec((tm, d), row),
            full(norm_g),
            pl.BlockSpec((tm, POOL_COLS), row),
            pl.BlockSpec((HALO, POOL_COLS), lambda i: (jnp.maximum(i * halo_blocks - 1, 0), 0)),
            full(w_gm),
            pl.BlockSpec((tm, d), row),
            full(pool_w), full(pool_scale), full(w_bp), full(w_bn), full(w_out),
        ],
        out_specs=pl.BlockSpec((tm, d), row),
        scratch_shapes=[pltpu.VMEM((HALO + tm, POOL_COLS), F32)],
        compiler_params=_params("parallel"),
        name="merge",
    )(x, norm_g, u, u, w_gm, o_nsa, pool_w, pool_scale, w_bp, w_bn, w_out)


def kernel(x, ffn1_norm, ffn1_w_gate, ffn1_w_up, ffn1_w_down, mix_norm, w_in, cmp_pos, cmp_k_w1, cmp_k_w2, cmp_v_w1, cmp_v_w2, pool_w, pool_scale, w_branch_pool, w_branch_nsa, w_out, ffn2_norm, ffn2_w_gate, ffn2_w_up, ffn2_w_down, final_norm):
    batch, seq, d = x.shape
    depth = w_in.shape[0]
    xf = x.reshape(batch * seq, d)
    bf = lambda a: a.astype(BF16)
    row = lambda a: a.reshape(1, -1)
    ffn1 = (bf(ffn1_w_gate), bf(ffn1_w_up), bf(ffn1_w_down))
    ffn2 = (bf(ffn2_w_gate), bf(ffn2_w_up), bf(ffn2_w_down))
    for l in range(depth):
        xf = _ffn(xf, row(ffn1_norm[l]), *ffn1, row(final_norm), l, final_norm=False)

        w_packed, w_gm = _pack_w_in(w_in[l])
        q, kc, vc, ks, kw, vs, vw, gates, u = _proj(xf, row(mix_norm[l]), w_packed, batch, seq)
        k_cmp, v_cmp = _compress(kc, vc, cmp_pos[l], cmp_k_w1[l], cmp_k_w2[l], cmp_v_w1[l], cmp_v_w2[l], batch, seq)
        o_nsa = _nsa(q, k_cmp, v_cmp, ks, vs, kw, vw, gates)

        w_bn = w_branch_nsa[l].reshape(N_GROUPS, HEADS_PER_GROUP, HEAD_DIM, d).transpose(1, 0, 2, 3).reshape(-1, d)
        xf = _merge(xf, row(mix_norm[l]), u, w_gm, o_nsa, bf(pool_w[l]), row(pool_scale[l]), bf(w_branch_pool[l]), bf(w_bn),
                    bf(w_out[l]), seq)

        xf = _ffn(xf, row(ffn2_norm[l]), *ffn2, row(final_norm), l, final_norm=(l == depth - 1))
    return xf.reshape(batch, seq, d)
```

```python
import functools

import jax
import jax.numpy as jnp
import numpy as np
from jax import lax
from jax.experimental import pallas as pl
from jax.experimental.pallas import tpu as pltpu

N_HEADS = 16
N_GROUPS = 2
HEADS_PER_GROUP = N_HEADS // N_GROUPS
HEAD_DIM = 64
CMP_LEN = 32
CMP_STRIDE = 16
SEL_LEN = 64
N_SELECT = 16
WINDOW = 512
POOL_WINDOWS = (2, 4, 8, 16)
POOL_GROUP_DIM = 128
RMS_EPS = 1e-6
ALIBI_MAX_BIAS = 8.0

LANES = 128
NEG_BIG = -1e30
TOPK_BIG = 1e30
N_FORCED = 3
TOPK_TAKEN = -float(2 ** 101)
VMEM_LIMIT = 48 * 1024 * 1024

BF16 = jnp.bfloat16
F32 = jnp.float32


def _dot(a, b):
    return jnp.dot(a, b, preferred_element_type=F32)


def _dot_nt(a, b):
    return lax.dot_general(a, b, (((1,), (1,)), ((), ())), preferred_element_type=F32)


def _rms(x, g):
    return x * lax.rsqrt(jnp.mean(x * x, axis=-1, keepdims=True) + RMS_EPS) * g


def _params(*sem):
    return pltpu.CompilerParams(dimension_semantics=sem, vmem_limit_bytes=VMEM_LIMIT)


def _ffn_kernel(x_ref, g_ref, wg_ref, wu_ref, wd_ref, fin_ref, o_ref, *, final_norm):
    x = x_ref[...]
    xn = _rms(x, g_ref[...]).astype(BF16)
    gate = _dot(xn, wg_ref[...])
    up = _dot(xn, wu_ref[...])
    act = (gate * jax.nn.sigmoid(gate)) * up
    y = x + 0.5 * _dot(act.astype(BF16), wd_ref[...])
    if final_norm:
        y = _rms(y, fin_ref[...])
    o_ref[...] = y


def _ffn(x, norm_g, wg, wu, wd, fin_g, layer, *, final_norm, tm=512):
    n, d = x.shape
    row = lambda i: (i, 0)
    resident = lambda a: pl.BlockSpec(a.shape, lambda i: (0, 0), pipeline_mode=pl.Buffered(1))
    of_layer = lambda a: pl.BlockSpec((None,) + a.shape[1:], lambda i: (layer, 0, 0), pipeline_mode=pl.Buffered(1))
    return pl.pallas_call(
        functools.partial(_ffn_kernel, final_norm=final_norm),
        out_shape=jax.ShapeDtypeStruct((n, d), F32),
        grid=(n // tm,),
        in_specs=[pl.BlockSpec((tm, d), row), resident(norm_g), of_layer(wg), of_layer(wu), of_layer(wd),
                  resident(fin_g)],
        out_specs=pl.BlockSpec((tm, d), row),
        compiler_params=_params("parallel"),
        name="ffn",
    )(x, norm_g, wg, wu, wd, fin_g)


LOG2E = 1.4426950408889634
N_FEAT = 6


def _bf16_pieces(x):
    x = np.asarray(x, np.float32)
    s1 = x.astype(BF16).astype(np.float32)
    s2 = (x - s1).astype(BF16).astype(np.float32)
    s3 = (x - s1 - s2).astype(BF16).astype(np.float32)
    return s1, s2, s3


def _query_features():
    slopes = np.float32(2.0) ** (-ALIBI_MAX_BIAS * np.arange(1, N_HEADS + 1, dtype=np.float32) / N_HEADS)
    s1, s2, s3 = _bf16_pieces(slopes * np.float32(LOG2E))
    feat = np.zeros((N_HEADS, LANES), np.float32)
    feat[:, HEAD_DIM:HEAD_DIM + N_FEAT] = np.stack([s1, s2, s3, SEL_LEN * s1, SEL_LEN * s2, SEL_LEN * s3], axis=1)
    return jnp.asarray(feat)


def _key_features(pos, width, offset):
    pos = np.asarray(pos)
    a, b = (pos // SEL_LEN).astype(np.float32), (pos % SEL_LEN).astype(np.float32)
    feat = np.zeros((len(pos), width), np.float32)
    feat[:, offset:offset + N_FEAT] = np.stack([b, b, b, a, a, a], axis=1)
    return jnp.asarray(feat)


Q_COLS = N_HEADS * HEAD_DIM
CMP_OFF = Q_COLS
KEY_OFF = CMP_OFF + 2 * LANES
VAL_OFF = KEY_OFF + 2 * LANES
GATE_OFF = VAL_OFF + 2 * LANES
POOL_OFF = GATE_OFF + LANES
POOL_COLS = len(POOL_WINDOWS) * POOL_GROUP_DIM


V_ROWS = HEAD_DIM + 16


def _store_values_t(v_ref, v):
    vt = v.T
    for g in range(N_GROUPS):
        v_ref[0, g, 0:HEAD_DIM, :] = vt[g * HEAD_DIM:(g + 1) * HEAD_DIM, :].astype(BF16)
        v_ref[0, g, HEAD_DIM:V_ROWS, :] = jnp.ones((V_ROWS - HEAD_DIM, v.shape[0]), BF16)


def _proj_kernel(x_ref, g_ref, w_ref, qf_ref, kf_ref, q_ref, kc_ref, vc_ref, ks_ref, kw_ref, vs_ref, vw_ref,
                 gate_ref, u_ref):
    hn = _rms(x_ref[...], g_ref[...]).astype(BF16)
    q_scale = HEAD_DIM ** -0.5 * LOG2E
    low = lax.broadcasted_iota(jnp.int32, (x_ref.shape[0], LANES), 1) < HEAD_DIM
    for j in range(N_HEADS // 4):
        q4 = _dot(hn, w_ref[:, j * 2 * LANES:(j + 1) * 2 * LANES]) * q_scale
        for b in range(2):
            both = q4[:, b * LANES:(b + 1) * LANES]
            h = 4 * j + 2 * b
            q_ref[0, h] = (jnp.where(low, both, 0.0) + qf_ref[h:h + 1, :]).astype(BF16)
            q_ref[0, h + 1] = (jnp.where(low, pltpu.roll(both, HEAD_DIM, 1), 0.0) + qf_ref[h + 1:h + 2, :]).astype(BF16)
    cmp_in = _dot(hn, w_ref[:, CMP_OFF:CMP_OFF + 2 * LANES])
    kc_ref[...] = cmp_in[:, 0:LANES]
    vc_ref[...] = cmp_in[:, LANES:2 * LANES]
    keys = _dot(hn, w_ref[:, KEY_OFF:KEY_OFF + 2 * LANES])
    kf = kf_ref[...]
    for i, k_ref in enumerate((ks_ref, kw_ref)):
        both = keys[:, i * LANES:(i + 1) * LANES]
        k_ref[0, 0] = (jnp.where(low, both, 0.0) + kf).astype(BF16)
        k_ref[0, 1] = (jnp.where(low, pltpu.roll(both, HEAD_DIM, 1), 0.0) + kf).astype(BF16)
    vals = _dot(hn, w_ref[:, VAL_OFF:VAL_OFF + 2 * LANES])
    for i, v_ref in enumerate((vs_ref, vw_ref)):
        _store_values_t(v_ref, vals[:, i * LANES:(i + 1) * LANES])
    gate_ref[...] = jax.nn.sigmoid(_dot(hn, w_ref[:, GATE_OFF:GATE_OFF + LANES]))
    u_ref[...] = _dot(hn, w_ref[:, POOL_OFF:POOL_OFF + POOL_COLS])


def _pack_w_in(w_in):
    w_in = w_in.astype(BF16)
    qw = N_HEADS * HEAD_DIM
    kvw = N_GROUPS * HEAD_DIM
    kc, vc, ks, vs, kw, vw = (w_in[..., qw + i * kvw:qw + (i + 1) * kvw] for i in range(6))
    off = qw + 6 * kvw
    n_g = 3 * N_HEADS
    w_g = jnp.pad(w_in[..., off:off + n_g], ((0, 0), (0, 0), (0, LANES - n_g)))
    w_pool = w_in[..., off + n_g:off + n_g + POOL_COLS]
    w_merge = w_in[..., off + n_g + POOL_COLS:]
    return jnp.concatenate([w_in[..., :qw], kc, vc, ks, kw, vs, vw, w_g, w_pool], axis=-1), w_merge


def _proj(x, norm_g, w_packed, layer, batch, seq, *, tm=1024):
    n, d = x.shape
    tiles_per_seq = seq // tm
    row = lambda i: (i, 0)
    const = lambda i: (0, 0)
    per_group = lambda i: (i // tiles_per_seq, 0, i % tiles_per_seq, 0)
    qf = _query_features()
    kf = _key_features(np.arange(seq), LANES, HEAD_DIM)
    flat = lambda width, dtype: (jax.ShapeDtypeStruct((n, width), dtype), pl.BlockSpec((tm, width), row))
    grouped = lambda count: (jax.ShapeDtypeStruct((batch, count, seq, LANES), BF16),
                             pl.BlockSpec((1, count, tm, LANES), per_group))
    values_t = (jax.ShapeDtypeStruct((batch, N_GROUPS, V_ROWS, seq), BF16),
                pl.BlockSpec((1, N_GROUPS, V_ROWS, tm), lambda i: (i // tiles_per_seq, 0, 0, i % tiles_per_seq)))
    outs = [grouped(N_HEADS), flat(LANES, F32), flat(LANES, F32), grouped(N_GROUPS), grouped(N_GROUPS),
            values_t, values_t, flat(LANES, F32), flat(POOL_COLS, F32)]
    return pl.pallas_call(
        _proj_kernel,
        out_shape=[o[0] for o in outs],
        grid=(n // tm,),
        in_specs=[pl.BlockSpec((tm, d), row), pl.BlockSpec((1, d), const),
                  pl.BlockSpec((None,) + w_packed.shape[1:], lambda i: (layer, 0, 0),
                               pipeline_mode=pl.Buffered(1)),
                  pl.BlockSpec(qf.shape, const), pl.BlockSpec((tm, LANES), lambda i: (i % tiles_per_seq, 0))],
        out_specs=[o[1] for o in outs],
        compiler_params=_params("parallel"),
        name="proj",
    )(x, norm_g, w_packed, qf, kf)


def _compress_kernel(k_ref, v_ref, pos_ref, wk1_ref, wk2_ref, wv1_ref, wv2_ref, feat_ref, ko_ref, vo_ref):
    n_chunk = k_ref.shape[0] // CMP_STRIDE
    hidden = wk2_ref.shape[0]

    def compress(x_ref, w1_ref, w2_ref):
        first = second = None
        for r in range(CMP_STRIDE):
            rows = x_ref[pl.ds(r, n_chunk, stride=CMP_STRIDE), :]
            a = _dot((rows + pos_ref[r:r + 1, :]).astype(BF16), w1_ref[r])
            b = _dot((rows + pos_ref[CMP_STRIDE + r:CMP_STRIDE + r + 1, :]).astype(BF16), w1_ref[CMP_STRIDE + r])
            first = a if first is None else first + a
            second = b if second is None else second + b
        act = jax.nn.gelu(first + pltpu.roll(second, n_chunk - 1, 0)).astype(BF16)
        return [_dot(act[:, g * hidden:(g + 1) * hidden], w2_ref[...]) for g in range(N_GROUPS)]

    for g, k_cmp in enumerate(compress(k_ref, wk1_ref, wk2_ref)):
        ko_ref[0, g] = jnp.concatenate([k_cmp, feat_ref[...]], axis=-1).astype(BF16)
    _store_values_t(vo_ref, jnp.concatenate(compress(v_ref, wv1_ref, wv2_ref), axis=-1))


def _block_diag_w1(w1):
    w = w1.astype(BF16).reshape(w1.shape[0], CMP_LEN, HEAD_DIM, -1)
    z = jnp.zeros_like(w)
    return jnp.concatenate([jnp.concatenate([w, z], axis=3), jnp.concatenate([z, w], axis=3)], axis=2)


def _compress(kc, vc, weights, layer, batch, seq):
    n_chunk = seq // CMP_STRIDE
    rows = pl.BlockSpec((seq, LANES), lambda b: (b, 0))
    of_layer = lambda a: pl.BlockSpec((None,) + a.shape[1:], lambda b: (layer,) + (0,) * (a.ndim - 1),
                                      pipeline_mode=pl.Buffered(1))
    feat = _key_features(np.arange(n_chunk) * CMP_STRIDE + CMP_LEN - 1, HEAD_DIM, 0)
    return pl.pallas_call(
        _compress_kernel,
        out_shape=[jax.ShapeDtypeStruct((batch, N_GROUPS, n_chunk, LANES), BF16),
                   jax.ShapeDtypeStruct((batch, N_GROUPS, V_ROWS, n_chunk), BF16)],
        grid=(batch,),
        in_specs=[rows, rows] + [of_layer(a) for a in weights]
                 + [pl.BlockSpec(feat.shape, lambda b: (0, 0), pipeline_mode=pl.Buffered(1))],
        out_specs=[pl.BlockSpec((1, N_GROUPS, n_chunk, LANES), lambda b: (b, 0, 0, 0)),
                   pl.BlockSpec((1, N_GROUPS, V_ROWS, n_chunk), lambda b: (b, 0, 0, 0))],
        compiler_params=_params("parallel"),
        name="compress",
    )(kc, vc, *weights, feat)


SLC_CHUNK = 256
BLOCKS_PER_CHUNK = SLC_CHUNK // SEL_LEN
HEAD_BLOCKS = 2


def _nsa_kernel(q_ref, kc_ref, vct_ref, ks_ref, vst_ref, kw_ref, vwt_ref, gate_ref, o_ref,
                st_ref, p_ref, mx_ref, gt_ref, selt_ref, list_ref, m_ref, acc_ref, out_ref, *, tq, seq):
    n_sel = seq // SEL_LEN
    n_chunks = seq // SLC_CHUNK
    n_cmp = kc_ref.shape[2]
    n_win = WINDOW + tq
    q0 = pl.program_id(1) * tq
    gt_ref[...] = gate_ref[...].T
    pairs = [(g, g * HEADS_PER_GROUP + 2 * j) for g in range(N_GROUPS) for j in range(HEADS_PER_GROUP // 2)]

    def gate_row(head, branch):
        c = 3 * head + branch
        return gt_ref[c:c + 1, :]

    def out_rows(head):
        g, hh = divmod(head, HEADS_PER_GROUP)
        return hh, slice(g * HEAD_DIM, (g + 1) * HEAD_DIM)

    def stage_scores(keys, n_keys, row0=0):
        for g, head0 in pairs:
            q_pair = q_ref[0, head0:head0 + 2].reshape(2 * tq, LANES)
            scores = _dot_nt(keys(g), q_pair)
            for i in range(2):
                st_ref[head0 + i, row0:row0 + n_keys, :] = scores[:, i * tq:(i + 1) * tq]

    def finish_pair(head0, pv, branch, first, ok=None):
        invs = []
        for i in range(2):
            cols = slice(i * tq, (i + 1) * tq)
            inv = 1.0 / pv[HEAD_DIM:HEAD_DIM + 1, cols]
            if ok is not None:
                inv = jnp.where(ok, inv, 0.0)
            hh, rows = out_rows(head0 + i)
            contrib = pv[0:HEAD_DIM, cols] * (inv * gate_row(head0 + i, branch))
            out_ref[hh, rows, :] = contrib if first else out_ref[hh, rows, :] + contrib
            invs.append(inv)
        return invs

    t_c = q0 + lax.broadcasted_iota(jnp.int32, (n_cmp, tq), 1)
    end_c = lax.broadcasted_iota(jnp.int32, (n_cmp, tq), 0) * CMP_STRIDE + (CMP_LEN - 1)
    bias_c = jnp.where(t_c >= end_c, 0.0, NEG_BIG)
    has_cmp = (q0 + lax.broadcasted_iota(jnp.int32, (1, tq), 1)) >= CMP_LEN - 1
    inv_c = [None] * N_HEADS
    raw = [None] * (N_HEADS // 2)

    def cmp_pair(g, head0):
        for i in range(2):
            x = st_ref[head0 + i, 0:n_cmp, :] + bias_c
            p_ref[head0 // 2, 0:n_cmp, i * tq:(i + 1) * tq] = jnp.exp2(x - jnp.max(x, axis=0, keepdims=True)).astype(BF16)
        both = _dot(vct_ref[0, g], p_ref[head0 // 2, 0:n_cmp, :])
        raw[head0 // 2] = both[V_ROWS:, :]
        inv_c[head0], inv_c[head0 + 1] = finish_pair(head0, both[0:V_ROWS, :], 0, True, has_cmp)

    stage_scores(lambda g: kc_ref[0, g], n_cmp)
    for g, head0 in pairs:
        cmp_pair(g, head0)

    w0 = pl.multiple_of(jnp.maximum(q0 - WINDOW, 0), tq)
    stage_scores(lambda g: kw_ref[0, g, pl.ds(w0, n_win), :], n_win)
    head_keys = HEAD_BLOCKS * SEL_LEN
    stage_scores(lambda g: ks_ref[0, g, 0:head_keys, :], head_keys, n_win)

    dist_w = (q0 + lax.broadcasted_iota(jnp.int32, (n_win, tq), 1)) - (w0 + lax.broadcasted_iota(jnp.int32, (n_win, tq), 0))
    bias_w = jnp.where((dist_w >= 0) & (dist_w < WINDOW), 0.0, NEG_BIG)

    def win_pair(g, head0):
        for i in range(2):
            x = st_ref[head0 + i, 0:n_win, :] + bias_w
            p_ref[head0 // 2, 0:n_win, i * tq:(i + 1) * tq] = jnp.exp2(x - jnp.max(x, axis=0, keepdims=True)).astype(BF16)
        pv = _dot(vwt_ref[0, g, :, pl.ds(w0, n_win)], p_ref[head0 // 2, 0:n_win, :])
        finish_pair(head0, pv, 2, False)

    for g, head0 in pairs:
        win_pair(g, head0)

    blk = lax.broadcasted_iota(jnp.int32, (n_sel, tq), 0)
    cur = (q0 + lax.broadcasted_iota(jnp.int32, (n_sel, tq), 1)) // SEL_LEN
    valid = blk <= cur
    forced = valid & ((blk == 0) | (blk == cur) | (blk == cur - 1))
    blk_f = blk.astype(F32)
    score = []
    for g in range(N_GROUPS):
        total = None
        for head in range(g * HEADS_PER_GROUP, (g + 1) * HEADS_PER_GROUP):
            part = raw[head // 2][:, (head % 2) * tq:(head % 2 + 1) * tq] * inv_c[head]
            total = part if total is None else total + part
        score.append(jnp.where(forced, TOPK_TAKEN, jnp.where(valid, total, -TOPK_BIG)))

    for _ in range(min(N_SELECT, n_sel) - N_FORCED):
        for g in range(N_GROUPS):
            best = jnp.max(score[g], axis=0, keepdims=True)
            pick = jnp.min(jnp.where(score[g] == best, blk_f, float(n_sel)), axis=0, keepdims=True)
            score[g] = jnp.where(blk_f == pick, TOPK_TAKEN, score[g])
    sel = [jnp.where(valid & (score[g] == TOPK_TAKEN), 1.0, 0.0) for g in range(N_GROUPS)]
    for g in range(N_GROUPS):
        selt_ref[g] = jnp.where(blk >= HEAD_BLOCKS, sel[g], 0.0)
    sel_any = jnp.where(blk >= HEAD_BLOCKS, sel[0] + sel[1], 0.0)
    n_items = jnp.int32(0)
    for c in range(n_chunks):
        any_sel = jnp.max(sel_any[c * BLOCKS_PER_CHUNK:(c + 1) * BLOCKS_PER_CHUNK, :])
        list_ref[n_items] = c
        n_items = n_items + (any_sel > 0.5).astype(jnp.int32)

    causal_h = (q0 + lax.broadcasted_iota(jnp.int32, (head_keys, tq), 1)) >= lax.broadcasted_iota(jnp.int32, (head_keys, tq), 0)
    head_rows = slice(n_win, n_win + head_keys)
    for g, head0 in pairs:
        if head0 % HEADS_PER_GROUP == 0:
            sel_keys = jnp.concatenate([jnp.broadcast_to(sel[g][i:i + 1, :], (SEL_LEN, tq)) for i in range(HEAD_BLOCKS)], axis=0)
            bias = jnp.where(causal_h & (sel_keys > 0.5), 0.0, NEG_BIG)
        for i in range(2):
            x = st_ref[head0 + i, head_rows, :] + bias
            m_new = jnp.max(x, axis=0, keepdims=True)
            m_ref[head0 + i:head0 + i + 1, :] = m_new
            p_ref[head0 // 2, head_rows, i * tq:(i + 1) * tq] = jnp.exp2(x - m_new).astype(BF16)
    for g, head0 in pairs:
        acc_ref[head0 // 2] = _dot(vst_ref[0, g, :, 0:head_keys], p_ref[head0 // 2, head_rows, :])

    base = (lax.broadcasted_iota(jnp.int32, (SLC_CHUNK, tq), 1)
            - lax.broadcasted_iota(jnp.int32, (SLC_CHUNK, tq), 0))

    def stage_chunk(item, row0):
        c = list_ref[item]
        k0 = pl.multiple_of(c * SLC_CHUNK, SLC_CHUNK)
        causal = base >= k0 - q0
        for g, head0 in pairs:
            if head0 % HEADS_PER_GROUP == 0:
                sel_keys = jnp.concatenate(
                    [jnp.broadcast_to(selt_ref[g, pl.ds(c * BLOCKS_PER_CHUNK + i, 1), :], (SEL_LEN, tq))
                     for i in range(BLOCKS_PER_CHUNK)], axis=0)
                bias = jnp.where(causal & (sel_keys > 0.5), 0.0, NEG_BIG)
                bias2 = jnp.concatenate([bias, bias], axis=1)
            q_pair = q_ref[0, head0:head0 + 2].reshape(2 * tq, LANES)
            scores = _dot_nt(ks_ref[0, g, pl.ds(k0, SLC_CHUNK), :], q_pair) + bias2
            slot = (row0 // SLC_CHUNK) * (N_HEADS // 2) + head0 // 2
            mx_ref[slot:slot + 1, :] = jnp.max(scores, axis=0, keepdims=True)
            for i in range(2):
                st_ref[head0 + i, row0:row0 + SLC_CHUNK, :] = scores[:, i * tq:(i + 1) * tq]

    def consume_chunk(item, row0):
        k0 = pl.multiple_of(list_ref[item] * SLC_CHUNK, SLC_CHUNK)
        for g, head0 in pairs:
            alphas = []
            for i, head in enumerate((head0, head0 + 1)):
                slot = (row0 // SLC_CHUNK) * (N_HEADS // 2) + head0 // 2
                m_old = m_ref[head:head + 1, :]
                m_new = jnp.maximum(m_old, mx_ref[slot:slot + 1, i * tq:(i + 1) * tq])
                alphas.append(jnp.exp2(m_old - m_new))
                m_ref[head:head + 1, :] = m_new
                x = st_ref[head, row0:row0 + SLC_CHUNK, :]
                p_ref[head0 // 2, 0:SLC_CHUNK, i * tq:(i + 1) * tq] = jnp.exp2(x - m_new).astype(BF16)
            pv = _dot(vst_ref[0, g, :, pl.ds(k0, SLC_CHUNK)], p_ref[head0 // 2, 0:SLC_CHUNK, :])
            acc_ref[head0 // 2] = acc_ref[head0 // 2] * jnp.concatenate(alphas, axis=1) + pv

    last = jnp.maximum(n_items - 1, 0)
    stage_chunk(0, 0)

    def two_chunks(j, carry):
        stage_chunk(jnp.minimum(2 * j + 1, last), SLC_CHUNK)
        consume_chunk(2 * j, 0)

        @pl.when(2 * j + 1 < n_items)
        def _():
            stage_chunk(jnp.minimum(2 * j + 2, last), 0)
            consume_chunk(2 * j + 1, SLC_CHUNK)
        return carry

    lax.fori_loop(0, (n_items + 1) // 2, two_chunks, 0)
    for g, head0 in pairs:
        finish_pair(head0, acc_ref[head0 // 2], 1, False)

    for hh in range(HEADS_PER_GROUP):
        o_ref[:, hh * LANES:(hh + 1) * LANES] = out_ref[hh].T.astype(BF16)


def _overlap_t(seq):
    n_cmp = (seq - CMP_LEN) // CMP_STRIDE + 1
    n_sel = seq // SEL_LEN
    cs = np.arange(n_cmp)[:, None] * CMP_STRIDE
    ss = np.arange(n_sel)[None, :] * SEL_LEN
    ov = np.clip(np.minimum(cs + CMP_LEN, ss + SEL_LEN) - np.maximum(cs, ss), 0, None) / CMP_LEN
    out = np.zeros((n_sel, seq // CMP_STRIDE), np.float32)
    out[:, :n_cmp] = ov.T
    return jnp.asarray(out, dtype=BF16)


def _nsa(q, kc, vct, ks, vst, kw, vwt, gates, *, tq=128):
    batch, _, seq, _ = q.shape
    n_cmp = kc.shape[2]
    n_sel = seq // SEL_LEN
    assert seq % SLC_CHUNK == 0 and seq >= WINDOW + tq and SLC_CHUNK % tq == 0 and tq == LANES
    assert n_sel <= 2 * SEL_LEN
    ovt = jnp.broadcast_to(_overlap_t(seq), (batch, N_GROUPS, n_sel, n_cmp))
    vct = jnp.concatenate([vct, ovt], axis=2)
    keys = lambda n: pl.BlockSpec((1, N_GROUPS, n, LANES), lambda b, i: (b, 0, 0, 0))
    vals = lambda n: pl.BlockSpec((1, N_GROUPS, V_ROWS, n), lambda b, i: (b, 0, 0, 0))
    tiles = seq // tq
    s_rows = max(n_cmp, WINDOW + tq + HEAD_BLOCKS * SEL_LEN, 2 * SLC_CHUNK)
    return pl.pallas_call(
        functools.partial(_nsa_kernel, tq=tq, seq=seq),
        out_shape=jax.ShapeDtypeStruct((batch * seq, HEADS_PER_GROUP * LANES), BF16),
        grid=(batch, tiles),
        in_specs=[
            pl.BlockSpec((1, N_HEADS, tq, LANES), lambda b, i: (b, 0, i, 0)),
            keys(n_cmp), pl.BlockSpec((1, N_GROUPS, V_ROWS + n_sel, n_cmp), lambda b, i: (b, 0, 0, 0)),
            keys(seq), vals(seq), keys(seq), vals(seq),
            pl.BlockSpec((tq, LANES), lambda b, i: (b * tiles + i, 0)),
        ],
        out_specs=pl.BlockSpec((tq, HEADS_PER_GROUP * LANES), lambda b, i: (b * tiles + i, 0)),
        scratch_shapes=[
            pltpu.VMEM((N_HEADS, s_rows, tq), F32),
            pltpu.VMEM((N_HEADS // 2, s_rows, 2 * tq), BF16),
            pltpu.VMEM((N_HEADS, 2 * tq), F32),
            pltpu.VMEM((LANES, tq), F32),
            pltpu.VMEM((N_GROUPS, n_sel, tq), F32),
            pltpu.SMEM((seq // SLC_CHUNK,), jnp.int32),
            pltpu.VMEM((2 * N_HEADS, tq), F32),
            pltpu.VMEM((N_HEADS // 2, V_ROWS, 2 * tq), F32),
            pltpu.VMEM((HEADS_PER_GROUP, N_GROUPS * HEAD_DIM, tq), F32),
        ],
        compiler_params=_params("parallel", "arbitrary"),
        name="nsa",
    )(q, kc, vct, ks, vst, kw, vwt, gates)


HALO = max(POOL_WINDOWS)
assert all(w & (w - 1) == 0 for w in POOL_WINDOWS)


def _merge_kernel(x_ref, g_ref, u_ref, halo_ref, wgm_ref, on_ref, pw_ref, ps_ref, wbp_ref, wbn_ref, wo_ref,
                  o_ref, ext_ref, *, tm, seq):
    d = x_ref.shape[1]
    gm = jax.nn.sigmoid(_dot(_rms(x_ref[...], g_ref[...]).astype(BF16), wgm_ref[...]))
    b = _dot(on_ref[...], wbn_ref[...])
    pos0 = (pl.program_id(0) * tm) % seq
    ext_ref[0:HALO, :] = jnp.where(pos0 == 0, 0.0, halo_ref[...])
    ext_ref[HALO:HALO + tm, :] = u_ref[...]
    pos = (pos0 + lax.broadcasted_iota(jnp.int32, (tm, POOL_GROUP_DIM), 0)).astype(F32)
    mixed = []
    for gi, w in enumerate(POOL_WINDOWS):
        cols = slice(gi * POOL_GROUP_DIM, (gi + 1) * POOL_GROUP_DIM)
        run = ext_ref[:, cols]
        for k in range(w.bit_length() - 1):
            run = run + pltpu.roll(run, 1 << k, 0)
        u = ext_ref[HALO:HALO + tm, cols]
        delta = run[HALO:HALO + tm] / jnp.minimum(pos + 1.0, float(w)) - u
        mixed.append(_dot(delta.astype(BF16), pw_ref[gi]) * ps_ref[:, cols])
    mixed = jnp.concatenate(mixed, axis=-1).astype(BF16)
    a = _dot(mixed, wbp_ref[...])
    merged = gm[:, 0:d] * a + gm[:, d:2 * d] * b
    o_ref[...] = x_ref[...] + _dot(merged.astype(BF16), wo_ref[...])


def _merge(x, norm_g, u, w_gm, o_nsa, pool_w, pool_scale, w_bp, w_bn, w_out, layer, seq, *, tm=1024):
    n, d = x.shape
    row = lambda i: (i, 0)
    full = lambda a: pl.BlockSpec(a.shape, lambda i: (0,) * a.ndim, pipeline_mode=pl.Buffered(1))
    of_layer = lambda a: pl.BlockSpec((None,) + a.shape[1:], lambda i: (layer,) + (0,) * (a.ndim - 1),
                                      pipeline_mode=pl.Buffered(1))
    halo_blocks = tm // HALO
    return pl.pallas_call(
        functools.partial(_merge_kernel, tm=tm, seq=seq),
        out_shape=jax.ShapeDtypeStruct((n, d), F32),
        grid=(n // tm,),
        in_specs=[
            pl.BlockSpec((tm, d), row),
            full(norm_g),
            pl.BlockSpec((tm, POOL_COLS), row),
            pl.BlockSpec((HALO, POOL_COLS), lambda i: (jnp.maximum(i * halo_blocks - 1, 0), 0)),
            of_layer(w_gm),
            pl.BlockSpec((tm, d), row),
            of_layer(pool_w), full(pool_scale), of_layer(w_bp), of_layer(w_bn), of_layer(w_out),
        ],
        out_specs=pl.BlockSpec((tm, d), row),
        scratch_shapes=[pltpu.VMEM((HALO + tm, POOL_COLS), F32)],
        compiler_params=_params("parallel"),
        name="merge",
    )(x, norm_g, u, u, w_gm, o_nsa, pool_w, pool_scale, w_bp, w_bn, w_out)


def kernel(x, ffn1_norm, ffn1_w_gate, ffn1_w_up, ffn1_w_down, mix_norm, w_in, cmp_pos, cmp_k_w1, cmp_k_w2, cmp_v_w1, cmp_v_w2, pool_w, pool_scale, w_branch_pool, w_branch_nsa, w_out, ffn2_norm, ffn2_w_gate, ffn2_w_up, ffn2_w_down, final_norm):
    batch, seq, d = x.shape
    depth = w_in.shape[0]
    xf = x.reshape(batch * seq, d)
    bf = lambda a: a.astype(BF16)
    row = lambda a: a.reshape(1, -1)
    ffn1 = (bf(ffn1_w_gate), bf(ffn1_w_up), bf(ffn1_w_down))
    ffn2 = (bf(ffn2_w_gate), bf(ffn2_w_up), bf(ffn2_w_down))
    w_packed, w_gm = _pack_w_in(w_in)
    cmp_w = (jnp.concatenate([cmp_pos] * N_GROUPS, axis=-1),
             _block_diag_w1(cmp_k_w1), bf(cmp_k_w2), _block_diag_w1(cmp_v_w1), bf(cmp_v_w2))
    w_bn = bf(w_branch_nsa).reshape(depth, N_GROUPS, HEADS_PER_GROUP, HEAD_DIM, d).transpose(0, 2, 1, 3, 4)
    w_bn = w_bn.reshape(depth, -1, d)
    for l in range(depth):
        xf = _ffn(xf, row(ffn1_norm[l]), *ffn1, row(final_norm), l, final_norm=False)

        q, kc, vc, ks, kw, vs, vw, gates, u = _proj(xf, row(mix_norm[l]), w_packed, l, batch, seq)
        k_cmp, v_cmp = _compress(kc, vc, cmp_w, l, batch, seq)
        o_nsa = _nsa(q, k_cmp, v_cmp, ks, vs, kw, vw, gates)
        xf = _merge(xf, row(mix_norm[l]), u, w_gm, o_nsa, bf(pool_w), row(pool_scale[l]), bf(w_branch_pool), w_bn,
                    bf(w_out), l, seq)

        xf = _ffn(xf, row(ffn2_norm[l]), *ffn2, row(final_norm), l, final_norm=(l == depth - 1))
    return xf.reshape(batch, seq, d)
```

```python
import functools

import jax
import jax.numpy as jnp
import numpy as np
from jax import lax
from jax.experimental import pallas as pl
from jax.experimental.pallas import tpu as pltpu

N_HEADS = 16
N_GROUPS = 2
HEADS_PER_GROUP = N_HEADS // N_GROUPS
HEAD_DIM = 64
CMP_LEN = 32
CMP_STRIDE = 16
SEL_LEN = 64
N_SELECT = 16
WINDOW = 512
POOL_WINDOWS = (2, 4, 8, 16)
POOL_GROUP_DIM = 128
RMS_EPS = 1e-6
ALIBI_MAX_BIAS = 8.0

LANES = 128
NEG_BIG = -1e30
TOPK_BIG = 1e30
N_FORCED = 3
TOPK_TAKEN = -float(2 ** 101)
VMEM_LIMIT = 48 * 1024 * 1024

BF16 = jnp.bfloat16
F32 = jnp.float32


def _dot(a, b):
    return jnp.dot(a, b, preferred_element_type=F32)


def _dot_nt(a, b):
    return lax.dot_general(a, b, (((1,), (1,)), ((), ())), preferred_element_type=F32)


def _rms(x, g):
    return x * lax.rsqrt(jnp.mean(x * x, axis=-1, keepdims=True) + RMS_EPS) * g


def _params(*sem):
    return pltpu.CompilerParams(dimension_semantics=sem, vmem_limit_bytes=VMEM_LIMIT)


def _ffn_kernel(x_ref, g_ref, wg_ref, wu_ref, wd_ref, fin_ref, o_ref, *, final_norm):
    x = x_ref[...]
    xn = _rms(x, g_ref[...]).astype(BF16)
    gate = _dot(xn, wg_ref[...])
    up = _dot(xn, wu_ref[...])
    act = (gate * jax.nn.sigmoid(gate)) * up
    y = x + 0.5 * _dot(act.astype(BF16), wd_ref[...])
    if final_norm:
        y = _rms(y, fin_ref[...])
    o_ref[...] = y


def _ffn(x, norm_g, wg, wu, wd, fin_g, layer, *, final_norm, tm=512):
    n, d = x.shape
    row = lambda i: (i, 0)
    resident = lambda a: pl.BlockSpec(a.shape, lambda i: (0, 0), pipeline_mode=pl.Buffered(1))
    of_layer = lambda a: pl.BlockSpec((None,) + a.shape[1:], lambda i: (layer, 0, 0), pipeline_mode=pl.Buffered(1))
    return pl.pallas_call(
        functools.partial(_ffn_kernel, final_norm=final_norm),
        out_shape=jax.ShapeDtypeStruct((n, d), F32),
        grid=(n // tm,),
        in_specs=[pl.BlockSpec((tm, d), row), resident(norm_g), of_layer(wg), of_layer(wu), of_layer(wd),
                  resident(fin_g)],
        out_specs=pl.BlockSpec((tm, d), row),
        compiler_params=_params("parallel"),
        name="ffn",
    )(x, norm_g, wg, wu, wd, fin_g)


LOG2E = 1.4426950408889634
N_FEAT = 6


def _bf16_pieces(x):
    x = np.asarray(x, np.float32)
    s1 = x.astype(BF16).astype(np.float32)
    s2 = (x - s1).astype(BF16).astype(np.float32)
    s3 = (x - s1 - s2).astype(BF16).astype(np.float32)
    return s1, s2, s3


def _query_features():
    slopes = np.float32(2.0) ** (-ALIBI_MAX_BIAS * np.arange(1, N_HEADS + 1, dtype=np.float32) / N_HEADS)
    s1, s2, s3 = _bf16_pieces(slopes * np.float32(LOG2E))
    feat = np.zeros((N_HEADS, LANES), np.float32)
    feat[:, HEAD_DIM:HEAD_DIM + N_FEAT] = np.stack([s1, s2, s3, SEL_LEN * s1, SEL_LEN * s2, SEL_LEN * s3], axis=1)
    return jnp.asarray(feat)


def _key_features(pos, width, offset):
    pos = np.asarray(pos)
    a, b = (pos // SEL_LEN).astype(np.float32), (pos % SEL_LEN).astype(np.float32)
    feat = np.zeros((len(pos), width), np.float32)
    feat[:, offset:offset + N_FEAT] = np.stack([b, b, b, a, a, a], axis=1)
    return jnp.asarray(feat)


Q_COLS = N_HEADS * HEAD_DIM
CMP_OFF = Q_COLS
KEY_OFF = CMP_OFF + 2 * LANES
VAL_OFF = KEY_OFF + 2 * LANES
GATE_OFF = VAL_OFF + 2 * LANES
POOL_OFF = GATE_OFF + LANES
POOL_COLS = len(POOL_WINDOWS) * POOL_GROUP_DIM


V_ROWS = HEAD_DIM + 16


def _store_values_t(v_ref, v):
    vt = v.T
    for g in range(N_GROUPS):
        v_ref[0, g, 0:HEAD_DIM, :] = vt[g * HEAD_DIM:(g + 1) * HEAD_DIM, :].astype(BF16)
        v_ref[0, g, HEAD_DIM:V_ROWS, :] = jnp.ones((V_ROWS - HEAD_DIM, v.shape[0]), BF16)


def _proj_kernel(x_ref, g_ref, w_ref, qf_ref, kf_ref, q_ref, kc_ref, vc_ref, ks_ref, kw_ref, vs_ref, vw_ref,
                 gate_ref, u_ref):
    hn = _rms(x_ref[...], g_ref[...]).astype(BF16)
    q_scale = HEAD_DIM ** -0.5 * LOG2E
    low = lax.broadcasted_iota(jnp.int32, (x_ref.shape[0], LANES), 1) < HEAD_DIM
    for j in range(N_HEADS // 4):
        q4 = _dot(hn, w_ref[:, j * 2 * LANES:(j + 1) * 2 * LANES]) * q_scale
        for b in range(2):
            both = q4[:, b * LANES:(b + 1) * LANES]
            h = 4 * j + 2 * b
            q_ref[0, h] = (jnp.where(low, both, 0.0) + qf_ref[h:h + 1, :]).astype(BF16)
            q_ref[0, h + 1] = (jnp.where(low, pltpu.roll(both, HEAD_DIM, 1), 0.0) + qf_ref[h + 1:h + 2, :]).astype(BF16)
    cmp_in = _dot(hn, w_ref[:, CMP_OFF:CMP_OFF + 2 * LANES])
    kc_ref[...] = cmp_in[:, 0:LANES]
    vc_ref[...] = cmp_in[:, LANES:2 * LANES]
    keys = _dot(hn, w_ref[:, KEY_OFF:KEY_OFF + 2 * LANES])
    kf = kf_ref[...]
    for i, k_ref in enumerate((ks_ref, kw_ref)):
        both = keys[:, i * LANES:(i + 1) * LANES]
        k_ref[0, 0] = (jnp.where(low, both, 0.0) + kf).astype(BF16)
        k_ref[0, 1] = (jnp.where(low, pltpu.roll(both, HEAD_DIM, 1), 0.0) + kf).astype(BF16)
    vals = _dot(hn, w_ref[:, VAL_OFF:VAL_OFF + 2 * LANES])
    for i, v_ref in enumerate((vs_ref, vw_ref)):
        _store_values_t(v_ref, vals[:, i * LANES:(i + 1) * LANES])
    gate_ref[...] = jax.nn.sigmoid(_dot(hn, w_ref[:, GATE_OFF:GATE_OFF + LANES]))
    u_ref[...] = _dot(hn, w_ref[:, POOL_OFF:POOL_OFF + POOL_COLS])


def _pack_w_in(w_in):
    w_in = w_in.astype(BF16)
    qw = N_HEADS * HEAD_DIM
    kvw = N_GROUPS * HEAD_DIM
    kc, vc, ks, vs, kw, vw = (w_in[..., qw + i * kvw:qw + (i + 1) * kvw] for i in range(6))
    off = qw + 6 * kvw
    n_g = 3 * N_HEADS
    w_g = jnp.pad(w_in[..., off:off + n_g], ((0, 0), (0, 0), (0, LANES - n_g)))
    w_pool = w_in[..., off + n_g:off + n_g + POOL_COLS]
    w_merge = w_in[..., off + n_g + POOL_COLS:]
    return jnp.concatenate([w_in[..., :qw], kc, vc, ks, kw, vs, vw, w_g, w_pool], axis=-1), w_merge


def _ffn_proj_kernel(x_ref, fg_ref, wg_ref, wu_ref, wd_ref, g_ref, w_ref, qf_ref, kf_ref, y_ref, *proj_out_refs):
    _ffn_kernel(x_ref, fg_ref, wg_ref, wu_ref, wd_ref, None, y_ref, final_norm=False)
    _proj_kernel(y_ref, g_ref, w_ref, qf_ref, kf_ref, *proj_out_refs)


def _ffn_proj(x, ffn_g, wg, wu, wd, norm_g, w_packed, layer, batch, seq, *, tm=512):
    n, d = x.shape
    tiles_per_seq = seq // tm
    row = lambda i: (i, 0)
    const = lambda i: (0, 0)
    of_layer = lambda a: pl.BlockSpec((None,) + a.shape[1:], lambda i: (layer, 0, 0), pipeline_mode=pl.Buffered(1))
    per_group = lambda i: (i // tiles_per_seq, 0, i % tiles_per_seq, 0)
    qf = _query_features()
    kf = _key_features(np.arange(seq), LANES, HEAD_DIM)
    flat = lambda width, dtype: (jax.ShapeDtypeStruct((n, width), dtype), pl.BlockSpec((tm, width), row))
    grouped = lambda count: (jax.ShapeDtypeStruct((batch, count, seq, LANES), BF16),
                             pl.BlockSpec((1, count, tm, LANES), per_group))
    values_t = (jax.ShapeDtypeStruct((batch, N_GROUPS, V_ROWS, seq), BF16),
                pl.BlockSpec((1, N_GROUPS, V_ROWS, tm), lambda i: (i // tiles_per_seq, 0, 0, i % tiles_per_seq)))
    outs = [flat(d, F32), grouped(N_HEADS), flat(LANES, F32), flat(LANES, F32), grouped(N_GROUPS), grouped(N_GROUPS),
            values_t, values_t, flat(LANES, F32), flat(POOL_COLS, F32)]
    return pl.pallas_call(
        _ffn_proj_kernel,
        out_shape=[o[0] for o in outs],
        grid=(n // tm,),
        in_specs=[pl.BlockSpec((tm, d), row), pl.BlockSpec((1, d), const), of_layer(wg), of_layer(wu), of_layer(wd),
                  pl.BlockSpec((1, d), const), of_layer(w_packed),
                  pl.BlockSpec(qf.shape, const), pl.BlockSpec((tm, LANES), lambda i: (i % tiles_per_seq, 0))],
        out_specs=[o[1] for o in outs],
        compiler_params=_params("parallel"),
        name="ffn_proj",
    )(x, ffn_g, wg, wu, wd, norm_g, w_packed, qf, kf)


def _compress_kernel(k_ref, v_ref, pos_ref, wk1_ref, wk2_ref, wv1_ref, wv2_ref, feat_ref, ko_ref, vo_ref):
    n_chunk = k_ref.shape[0] // CMP_STRIDE
    hidden = wk2_ref.shape[0]

    def compress(x_ref, w1_ref, w2_ref):
        first = second = None
        for r in range(CMP_STRIDE):
            rows = x_ref[pl.ds(r, n_chunk, stride=CMP_STRIDE), :]
            a = _dot((rows + pos_ref[r:r + 1, :]).astype(BF16), w1_ref[r])
            b = _dot((rows + pos_ref[CMP_STRIDE + r:CMP_STRIDE + r + 1, :]).astype(BF16), w1_ref[CMP_STRIDE + r])
            first = a if first is None else first + a
            second = b if second is None else second + b
        act = jax.nn.gelu(first + pltpu.roll(second, n_chunk - 1, 0)).astype(BF16)
        return [_dot(act[:, g * hidden:(g + 1) * hidden], w2_ref[...]) for g in range(N_GROUPS)]

    for g, k_cmp in enumerate(compress(k_ref, wk1_ref, wk2_ref)):
        ko_ref[0, g] = jnp.concatenate([k_cmp, feat_ref[...]], axis=-1).astype(BF16)
    _store_values_t(vo_ref, jnp.concatenate(compress(v_ref, wv1_ref, wv2_ref), axis=-1))


def _block_diag_w1(w1):
    w = w1.astype(BF16).reshape(w1.shape[0], CMP_LEN, HEAD_DIM, -1)
    z = jnp.zeros_like(w)
    return jnp.concatenate([jnp.concatenate([w, z], axis=3), jnp.concatenate([z, w], axis=3)], axis=2)


def _compress(kc, vc, weights, layer, batch, seq):
    n_chunk = seq // CMP_STRIDE
    rows = pl.BlockSpec((seq, LANES), lambda b: (b, 0))
    of_layer = lambda a: pl.BlockSpec((None,) + a.shape[1:], lambda b: (layer,) + (0,) * (a.ndim - 1),
                                      pipeline_mode=pl.Buffered(1))
    feat = _key_features(np.arange(n_chunk) * CMP_STRIDE + CMP_LEN - 1, HEAD_DIM, 0)
    return pl.pallas_call(
        _compress_kernel,
        out_shape=[jax.ShapeDtypeStruct((batch, N_GROUPS, n_chunk, LANES), BF16),
                   jax.ShapeDtypeStruct((batch, N_GROUPS, V_ROWS, n_chunk), BF16)],
        grid=(batch,),
        in_specs=[rows, rows] + [of_layer(a) for a in weights]
                 + [pl.BlockSpec(feat.shape, lambda b: (0, 0), pipeline_mode=pl.Buffered(1))],
        out_specs=[pl.BlockSpec((1, N_GROUPS, n_chunk, LANES), lambda b: (b, 0, 0, 0)),
                   pl.BlockSpec((1, N_GROUPS, V_ROWS, n_chunk), lambda b: (b, 0, 0, 0))],
        compiler_params=_params("parallel"),
        name="compress",
    )(kc, vc, *weights, feat)


SLC_CHUNK = 256
BLOCKS_PER_CHUNK = SLC_CHUNK // SEL_LEN
HEAD_BLOCKS = 2


def _nsa_kernel(q_ref, kc_ref, vct_ref, ks_ref, vst_ref, kw_ref, vwt_ref, gate_ref, o_ref,
                st_ref, p_ref, mx_ref, gt_ref, selt_ref, list_ref, m_ref, acc_ref, out_ref, *, tq, seq):
    n_sel = seq // SEL_LEN
    n_chunks = seq // SLC_CHUNK
    n_cmp = kc_ref.shape[2]
    n_win = WINDOW + tq
    q0 = pl.program_id(1) * tq
    gt_ref[...] = gate_ref[...].T
    pairs = [(g, g * HEADS_PER_GROUP + 2 * j) for g in range(N_GROUPS) for j in range(HEADS_PER_GROUP // 2)]

    def gate_row(head, branch):
        c = 3 * head + branch
        return gt_ref[c:c + 1, :]

    def out_rows(head):
        g, hh = divmod(head, HEADS_PER_GROUP)
        return hh, slice(g * HEAD_DIM, (g + 1) * HEAD_DIM)

    def stage_scores(keys, n_keys, row0=0):
        for g, head0 in pairs:
            q_pair = q_ref[0, head0:head0 + 2].reshape(2 * tq, LANES)
            scores = _dot_nt(keys(g), q_pair)
            for i in range(2):
                st_ref[head0 + i, row0:row0 + n_keys, :] = scores[:, i * tq:(i + 1) * tq]

    def finish_pair(head0, pv, branch, first, ok=None):
        invs = []
        for i in range(2):
            cols = slice(i * tq, (i + 1) * tq)
            inv = 1.0 / pv[HEAD_DIM:HEAD_DIM + 1, cols]
            if ok is not None:
                inv = jnp.where(ok, inv, 0.0)
            hh, rows = out_rows(head0 + i)
            contrib = pv[0:HEAD_DIM, cols] * (inv * gate_row(head0 + i, branch))
            out_ref[hh, rows, :] = contrib if first else out_ref[hh, rows, :] + contrib
            invs.append(inv)
        return invs

    t_c = q0 + lax.broadcasted_iota(jnp.int32, (n_cmp, tq), 1)
    end_c = lax.broadcasted_iota(jnp.int32, (n_cmp, tq), 0) * CMP_STRIDE + (CMP_LEN - 1)
    bias_c = jnp.where(t_c >= end_c, 0.0, NEG_BIG)
    has_cmp = (q0 + lax.broadcasted_iota(jnp.int32, (1, tq), 1)) >= CMP_LEN - 1
    inv_c = [None] * N_HEADS
    raw = [None] * (N_HEADS // 2)

    def cmp_pair(g, head0):
        for i in range(2):
            x = st_ref[head0 + i, 0:n_cmp, :] + bias_c
            p_ref[head0 // 2, 0:n_cmp, i * tq:(i + 1) * tq] = jnp.exp2(x - jnp.max(x, axis=0, keepdims=True)).astype(BF16)
        both = _dot(vct_ref[0, g], p_ref[head0 // 2, 0:n_cmp, :])
        raw[head0 // 2] = both[V_ROWS:, :]
        inv_c[head0], inv_c[head0 + 1] = finish_pair(head0, both[0:V_ROWS, :], 0, True, has_cmp)

    stage_scores(lambda g: kc_ref[0, g], n_cmp)
    for g, head0 in pairs:
        cmp_pair(g, head0)

    w0 = pl.multiple_of(jnp.maximum(q0 - WINDOW, 0), tq)
    stage_scores(lambda g: kw_ref[0, g, pl.ds(w0, n_win), :], n_win)
    head_keys = HEAD_BLOCKS * SEL_LEN
    stage_scores(lambda g: ks_ref[0, g, 0:head_keys, :], head_keys, n_win)

    dist_w = (q0 + lax.broadcasted_iota(jnp.int32, (n_win, tq), 1)) - (w0 + lax.broadcasted_iota(jnp.int32, (n_win, tq), 0))
    bias_w = jnp.where((dist_w >= 0) & (dist_w < WINDOW), 0.0, NEG_BIG)

    def win_pair(g, head0):
        for i in range(2):
            x = st_ref[head0 + i, 0:n_win, :] + bias_w
            p_ref[head0 // 2, 0:n_win, i * tq:(i + 1) * tq] = jnp.exp2(x - jnp.max(x, axis=0, keepdims=True)).astype(BF16)
        pv = _dot(vwt_ref[0, g, :, pl.ds(w0, n_win)], p_ref[head0 // 2, 0:n_win, :])
        finish_pair(head0, pv, 2, False)

    for g, head0 in pairs:
        win_pair(g, head0)

    blk = lax.broadcasted_iota(jnp.int32, (n_sel, tq), 0)
    cur = (q0 + lax.broadcasted_iota(jnp.int32, (n_sel, tq), 1)) // SEL_LEN
    valid = blk <= cur
    forced = valid & ((blk == 0) | (blk == cur) | (blk == cur - 1))
    blk_f = blk.astype(F32)
    score = []
    for g in range(N_GROUPS):
        total = None
        for head in range(g * HEADS_PER_GROUP, (g + 1) * HEADS_PER_GROUP):
            part = raw[head // 2][:, (head % 2) * tq:(head % 2 + 1) * tq] * inv_c[head]
            total = part if total is None else total + part
        score.append(jnp.where(forced, TOPK_TAKEN, jnp.where(valid, total, -TOPK_BIG)))

    for _ in range(min(N_SELECT, n_sel) - N_FORCED):
        for g in range(N_GROUPS):
            best = jnp.max(score[g], axis=0, keepdims=True)
            pick = jnp.min(jnp.where(score[g] == best, blk_f, float(n_sel)), axis=0, keepdims=True)
            score[g] = jnp.where(blk_f == pick, TOPK_TAKEN, score[g])
    sel = [jnp.where(valid & (score[g] == TOPK_TAKEN), 1.0, 0.0) for g in range(N_GROUPS)]
    for g in range(N_GROUPS):
        selt_ref[g] = jnp.where(blk >= HEAD_BLOCKS, sel[g], 0.0)
    sel_any = jnp.where(blk >= HEAD_BLOCKS, sel[0] + sel[1], 0.0)
    n_items = jnp.int32(0)
    for c in range(n_chunks):
        any_sel = jnp.max(sel_any[c * BLOCKS_PER_CHUNK:(c + 1) * BLOCKS_PER_CHUNK, :])
        list_ref[n_items] = c
        n_items = n_items + (any_sel > 0.5).astype(jnp.int32)

    causal_h = (q0 + lax.broadcasted_iota(jnp.int32, (head_keys, tq), 1)) >= lax.broadcasted_iota(jnp.int32, (head_keys, tq), 0)
    head_rows = slice(n_win, n_win + head_keys)
    for g, head0 in pairs:
        if head0 % HEADS_PER_GROUP == 0:
            sel_keys = jnp.concatenate([jnp.broadcast_to(sel[g][i:i + 1, :], (SEL_LEN, tq)) for i in range(HEAD_BLOCKS)], axis=0)
            bias = jnp.where(causal_h & (sel_keys > 0.5), 0.0, NEG_BIG)
        for i in range(2):
            x = st_ref[head0 + i, head_rows, :] + bias
            m_new = jnp.max(x, axis=0, keepdims=True)
            m_ref[head0 + i:head0 + i + 1, :] = m_new
            p_ref[head0 // 2, head_rows, i * tq:(i + 1) * tq] = jnp.exp2(x - m_new).astype(BF16)
    for g, head0 in pairs:
        acc_ref[head0 // 2] = _dot(vst_ref[0, g, :, 0:head_keys], p_ref[head0 // 2, head_rows, :])

    base = (lax.broadcasted_iota(jnp.int32, (SLC_CHUNK, tq), 1)
            - lax.broadcasted_iota(jnp.int32, (SLC_CHUNK, tq), 0))

    def stage_chunk(item, row0):
        c = list_ref[item]
        k0 = pl.multiple_of(c * SLC_CHUNK, SLC_CHUNK)
        causal = base >= k0 - q0
        for g, head0 in pairs:
            if head0 % HEADS_PER_GROUP == 0:
                sel_keys = jnp.concatenate(
                    [jnp.broadcast_to(selt_ref[g, pl.ds(c * BLOCKS_PER_CHUNK + i, 1), :], (SEL_LEN, tq))
                     for i in range(BLOCKS_PER_CHUNK)], axis=0)
                bias = jnp.where(causal & (sel_keys > 0.5), 0.0, NEG_BIG)
                bias2 = jnp.concatenate([bias, bias], axis=1)
            q_pair = q_ref[0, head0:head0 + 2].reshape(2 * tq, LANES)
            scores = _dot_nt(ks_ref[0, g, pl.ds(k0, SLC_CHUNK), :], q_pair) + bias2
            slot = (row0 // SLC_CHUNK) * (N_HEADS // 2) + head0 // 2
            mx_ref[slot:slot + 1, :] = jnp.max(scores, axis=0, keepdims=True)
            for i in range(2):
                st_ref[head0 + i, row0:row0 + SLC_CHUNK, :] = scores[:, i * tq:(i + 1) * tq]

    def consume_chunk(item, row0):
        k0 = pl.multiple_of(list_ref[item] * SLC_CHUNK, SLC_CHUNK)
        for g, head0 in pairs:
            alphas = []
            for i, head in enumerate((head0, head0 + 1)):
                slot = (row0 // SLC_CHUNK) * (N_HEADS // 2) + head0 // 2
                m_old = m_ref[head:head + 1, :]
                m_new = jnp.maximum(m_old, mx_ref[slot:slot + 1, i * tq:(i + 1) * tq])
                alphas.append(jnp.exp2(m_old - m_new))
                m_ref[head:head + 1, :] = m_new
                x = st_ref[head, row0:row0 + SLC_CHUNK, :]
                p_ref[head0 // 2, 0:SLC_CHUNK, i * tq:(i + 1) * tq] = jnp.exp2(x - m_new).astype(BF16)
            pv = _dot(vst_ref[0, g, :, pl.ds(k0, SLC_CHUNK)], p_ref[head0 // 2, 0:SLC_CHUNK, :])
            acc_ref[head0 // 2] = acc_ref[head0 // 2] * jnp.concatenate(alphas, axis=1) + pv

    last = jnp.maximum(n_items - 1, 0)
    stage_chunk(0, 0)

    def two_chunks(j, carry):
        stage_chunk(jnp.minimum(2 * j + 1, last), SLC_CHUNK)
        consume_chunk(2 * j, 0)

        @pl.when(2 * j + 1 < n_items)
        def _():
            stage_chunk(jnp.minimum(2 * j + 2, last), 0)
            consume_chunk(2 * j + 1, SLC_CHUNK)
        return carry

    lax.fori_loop(0, (n_items + 1) // 2, two_chunks, 0)
    for g, head0 in pairs:
        finish_pair(head0, acc_ref[head0 // 2], 1, False)

    for hh in range(HEADS_PER_GROUP):
        o_ref[:, hh * LANES:(hh + 1) * LANES] = out_ref[hh].T.astype(BF16)


def _overlap_t(seq):
    n_cmp = (seq - CMP_LEN) // CMP_STRIDE + 1
    n_sel = seq // SEL_LEN
    cs = np.arange(n_cmp)[:, None] * CMP_STRIDE
    ss = np.arange(n_sel)[None, :] * SEL_LEN
    ov = np.clip(np.minimum(cs + CMP_LEN, ss + SEL_LEN) - np.maximum(cs, ss), 0, None) / CMP_LEN
    out = np.zeros((n_sel, seq // CMP_STRIDE), np.float32)
    out[:, :n_cmp] = ov.T
    return jnp.asarray(out, dtype=BF16)


def _nsa(q, kc, vct, ks, vst, kw, vwt, gates, *, tq=128):
    batch, _, seq, _ = q.shape
    n_cmp = kc.shape[2]
    n_sel = seq // SEL_LEN
    assert seq % SLC_CHUNK == 0 and seq >= WINDOW + tq and SLC_CHUNK % tq == 0 and tq == LANES
    assert n_sel <= 2 * SEL_LEN
    ovt = jnp.broadcast_to(_overlap_t(seq), (batch, N_GROUPS, n_sel, n_cmp))
    vct = jnp.concatenate([vct, ovt], axis=2)
    keys = lambda n: pl.BlockSpec((1, N_GROUPS, n, LANES), lambda b, i: (b, 0, 0, 0))
    vals = lambda n: pl.BlockSpec((1, N_GROUPS, V_ROWS, n), lambda b, i: (b, 0, 0, 0))
    tiles = seq // tq
    s_rows = max(n_cmp, WINDOW + tq + HEAD_BLOCKS * SEL_LEN, 2 * SLC_CHUNK)
    return pl.pallas_call(
        functools.partial(_nsa_kernel, tq=tq, seq=seq),
        out_shape=jax.ShapeDtypeStruct((batch * seq, HEADS_PER_GROUP * LANES), BF16),
        grid=(batch, tiles),
        in_specs=[
            pl.BlockSpec((1, N_HEADS, tq, LANES), lambda b, i: (b, 0, i, 0)),
            keys(n_cmp), pl.BlockSpec((1, N_GROUPS, V_ROWS + n_sel, n_cmp), lambda b, i: (b, 0, 0, 0)),
            keys(seq), vals(seq), keys(seq), vals(seq),
            pl.BlockSpec((tq, LANES), lambda b, i: (b * tiles + i, 0)),
        ],
        out_specs=pl.BlockSpec((tq, HEADS_PER_GROUP * LANES), lambda b, i: (b * tiles + i, 0)),
        scratch_shapes=[
            pltpu.VMEM((N_HEADS, s_rows, tq), F32),
            pltpu.VMEM((N_HEADS // 2, s_rows, 2 * tq), BF16),
            pltpu.VMEM((N_HEADS, 2 * tq), F32),
            pltpu.VMEM((LANES, tq), F32),
            pltpu.VMEM((N_GROUPS, n_sel, tq), F32),
            pltpu.SMEM((seq // SLC_CHUNK,), jnp.int32),
            pltpu.VMEM((2 * N_HEADS, tq), F32),
            pltpu.VMEM((N_HEADS // 2, V_ROWS, 2 * tq), F32),
            pltpu.VMEM((HEADS_PER_GROUP, N_GROUPS * HEAD_DIM, tq), F32),
        ],
        compiler_params=_params("parallel", "arbitrary"),
        name="nsa",
    )(q, kc, vct, ks, vst, kw, vwt, gates)


HALO = max(POOL_WINDOWS)
assert all(w & (w - 1) == 0 for w in POOL_WINDOWS)


def _merge_kernel(x_ref, g_ref, u_ref, halo_ref, wgm_ref, on_ref, pw_ref, ps_ref, wbp_ref, wbn_ref, wo_ref,
                  o_ref, ext_ref, *, tm, seq):
    d = x_ref.shape[1]
    gm = jax.nn.sigmoid(_dot(_rms(x_ref[...], g_ref[...]).astype(BF16), wgm_ref[...]))
    b = _dot(on_ref[...], wbn_ref[...])
    pos0 = (pl.program_id(0) * tm) % seq
    ext_ref[0:HALO, :] = jnp.where(pos0 == 0, 0.0, halo_ref[...])
    ext_ref[HALO:HALO + tm, :] = u_ref[...]
    pos = (pos0 + lax.broadcasted_iota(jnp.int32, (tm, POOL_GROUP_DIM), 0)).astype(F32)
    mixed = []
    for gi, w in enumerate(POOL_WINDOWS):
        cols = slice(gi * POOL_GROUP_DIM, (gi + 1) * POOL_GROUP_DIM)
        run = ext_ref[:, cols]
        for k in range(w.bit_length() - 1):
            run = run + pltpu.roll(run, 1 << k, 0)
        u = ext_ref[HALO:HALO + tm, cols]
        delta = run[HALO:HALO + tm] / jnp.minimum(pos + 1.0, float(w)) - u
        mixed.append(_dot(delta.astype(BF16), pw_ref[gi]) * ps_ref[:, cols])
    mixed = jnp.concatenate(mixed, axis=-1).astype(BF16)
    a = _dot(mixed, wbp_ref[...])
    merged = gm[:, 0:d] * a + gm[:, d:2 * d] * b
    o_ref[...] = x_ref[...] + _dot(merged.astype(BF16), wo_ref[...])


def _merge(x, norm_g, u, w_gm, o_nsa, pool_w, pool_scale, w_bp, w_bn, w_out, layer, seq, *, tm=1024):
    n, d = x.shape
    row = lambda i: (i, 0)
    full = lambda a: pl.BlockSpec(a.shape, lambda i: (0,) * a.ndim, pipeline_mode=pl.Buffered(1))
    of_layer = lambda a: pl.BlockSpec((None,) + a.shape[1:], lambda i: (layer,) + (0,) * (a.ndim - 1),
                                      pipeline_mode=pl.Buffered(1))
    halo_blocks = tm // HALO
    return pl.pallas_call(
        functools.partial(_merge_kernel, tm=tm, seq=seq),
        out_shape=jax.ShapeDtypeStruct((n, d), F32),
        grid=(n // tm,),
        in_specs=[
            pl.BlockSpec((tm, d), row),
            full(norm_g),
            pl.BlockSpec((tm, POOL_COLS), row),
            pl.BlockSpec((HALO, POOL_COLS), lambda i: (jnp.maximum(i * halo_blocks - 1, 0), 0)),
            of_layer(w_gm),
            pl.BlockSpec((tm, d), row),
            of_layer(pool_w), full(pool_scale), of_layer(w_bp), of_layer(w_bn), of_layer(w_out),
        ],
        out_specs=pl.BlockSpec((tm, d), row),
        scratch_shapes=[pltpu.VMEM((HALO + tm, POOL_COLS), F32)],
        compiler_params=_params("parallel"),
        name="merge",
    )(x, norm_g, u, u, w_gm, o_nsa, pool_w, pool_scale, w_bp, w_bn, w_out)


def kernel(x, ffn1_norm, ffn1_w_gate, ffn1_w_up, ffn1_w_down, mix_norm, w_in, cmp_pos, cmp_k_w1, cmp_k_w2, cmp_v_w1, cmp_v_w2, pool_w, pool_scale, w_branch_pool, w_branch_nsa, w_out, ffn2_norm, ffn2_w_gate, ffn2_w_up, ffn2_w_down, final_norm):
    batch, seq, d = x.shape
    depth = w_in.shape[0]
    xf = x.reshape(batch * seq, d)
    bf = lambda a: a.astype(BF16)
    row = lambda a: a.reshape(1, -1)
    ffn1 = (bf(ffn1_w_gate), bf(ffn1_w_up), bf(ffn1_w_down))
    ffn2 = (bf(ffn2_w_gate), bf(ffn2_w_up), bf(ffn2_w_down))
    w_packed, w_gm = _pack_w_in(w_in)
    cmp_w = (jnp.concatenate([cmp_pos] * N_GROUPS, axis=-1),
             _block_diag_w1(cmp_k_w1), bf(cmp_k_w2), _block_diag_w1(cmp_v_w1), bf(cmp_v_w2))
    w_bn = bf(w_branch_nsa).reshape(depth, N_GROUPS, HEADS_PER_GROUP, HEAD_DIM, d).transpose(0, 2, 1, 3, 4)
    w_bn = w_bn.reshape(depth, -1, d)
    for l in range(depth):
        xf, q, kc, vc, ks, kw, vs, vw, gates, u = _ffn_proj(xf, row(ffn1_norm[l]), *ffn1, row(mix_norm[l]), w_packed, l,
                                                            batch, seq)
        k_cmp, v_cmp = _compress(kc, vc, cmp_w, l, batch, seq)
        o_nsa = _nsa(q, k_cmp, v_cmp, ks, vs, kw, vw, gates)
        xf = _merge(xf, row(mix_norm[l]), u, w_gm, o_nsa, bf(pool_w), row(pool_scale[l]), bf(w_branch_pool), w_bn,
                    bf(w_out), l, seq)

        xf = _ffn(xf, row(ffn2_norm[l]), *ffn2, row(final_norm), l, final_norm=(l == depth - 1))
    return xf.reshape(batch, seq, d)
```

```python
import functools

import jax
import jax.numpy as jnp
import numpy as np
from jax import lax
from jax.experimental import pallas as pl
from jax.experimental.pallas import tpu as pltpu

N_HEADS = 16
N_GROUPS = 2
HEADS_PER_GROUP = N_HEADS // N_GROUPS
HEAD_DIM = 64
CMP_LEN = 32
CMP_STRIDE = 16
SEL_LEN = 64
N_SELECT = 16
WINDOW = 512
POOL_WINDOWS = (2, 4, 8, 16)
POOL_GROUP_DIM = 128
RMS_EPS = 1e-6
ALIBI_MAX_BIAS = 8.0

LANES = 128
NEG_BIG = -1e30
TOPK_BIG = 1e30
N_FORCED = 3
TOPK_TAKEN = -float(2 ** 101)
VMEM_LIMIT = 48 * 1024 * 1024

BF16 = jnp.bfloat16
F32 = jnp.float32


def _dot(a, b):
    return jnp.dot(a, b, preferred_element_type=F32)


def _dot_nt(a, b):
    return lax.dot_general(a, b, (((1,), (1,)), ((), ())), preferred_element_type=F32)


def _rms(x, g):
    return x * lax.rsqrt(jnp.mean(x * x, axis=-1, keepdims=True) + RMS_EPS) * g


def _params(*sem):
    return pltpu.CompilerParams(dimension_semantics=sem, vmem_limit_bytes=VMEM_LIMIT)


def _ffn_kernel(x_ref, g_ref, wg_ref, wu_ref, wd_ref, fin_ref, o_ref, *, final_norm):
    x = x_ref[...]
    xn = _rms(x, g_ref[...]).astype(BF16)
    gate = _dot(xn, wg_ref[...])
    up = _dot(xn, wu_ref[...])
    act = (gate * jax.nn.sigmoid(gate)) * up
    y = x + 0.5 * _dot(act.astype(BF16), wd_ref[...])
    if final_norm:
        y = _rms(y, fin_ref[...])
    o_ref[...] = y


LOG2E = 1.4426950408889634
N_FEAT = 6


def _bf16_pieces(x):
    x = np.asarray(x, np.float32)
    s1 = x.astype(BF16).astype(np.float32)
    s2 = (x - s1).astype(BF16).astype(np.float32)
    s3 = (x - s1 - s2).astype(BF16).astype(np.float32)
    return s1, s2, s3


def _query_features():
    slopes = np.float32(2.0) ** (-ALIBI_MAX_BIAS * np.arange(1, N_HEADS + 1, dtype=np.float32) / N_HEADS)
    s1, s2, s3 = _bf16_pieces(slopes * np.float32(LOG2E))
    feat = np.zeros((N_HEADS, LANES), np.float32)
    feat[:, HEAD_DIM:HEAD_DIM + N_FEAT] = np.stack([s1, s2, s3, SEL_LEN * s1, SEL_LEN * s2, SEL_LEN * s3], axis=1)
    return jnp.asarray(feat)


def _key_features(pos, width, offset):
    pos = np.asarray(pos)
    a, b = (pos // SEL_LEN).astype(np.float32), (pos % SEL_LEN).astype(np.float32)
    feat = np.zeros((len(pos), width), np.float32)
    feat[:, offset:offset + N_FEAT] = np.stack([b, b, b, a, a, a], axis=1)
    return jnp.asarray(feat)


Q_COLS = N_HEADS * HEAD_DIM
CMP_OFF = Q_COLS
KEY_OFF = CMP_OFF + 2 * LANES
VAL_OFF = KEY_OFF + 2 * LANES
GATE_OFF = VAL_OFF + 2 * LANES
POOL_OFF = GATE_OFF + LANES
POOL_COLS = len(POOL_WINDOWS) * POOL_GROUP_DIM


V_ROWS = HEAD_DIM + 16


def _store_values_t(v_ref, v):
    vt = v.T
    for g in range(N_GROUPS):
        v_ref[0, g, 0:HEAD_DIM, :] = vt[g * HEAD_DIM:(g + 1) * HEAD_DIM, :].astype(BF16)
        v_ref[0, g, HEAD_DIM:V_ROWS, :] = jnp.ones((V_ROWS - HEAD_DIM, v.shape[0]), BF16)


def _proj_kernel(x_ref, g_ref, w_ref, qf_ref, kf_ref, q_ref, kc_ref, vc_ref, ks_ref, kw_ref, vs_ref, vw_ref,
                 gate_ref, u_ref):
    hn = _rms(x_ref[...], g_ref[...]).astype(BF16)
    q_scale = HEAD_DIM ** -0.5 * LOG2E
    low = lax.broadcasted_iota(jnp.int32, (x_ref.shape[0], LANES), 1) < HEAD_DIM
    for j in range(N_HEADS // 4):
        q4 = _dot(hn, w_ref[:, j * 2 * LANES:(j + 1) * 2 * LANES]) * q_scale
        for b in range(2):
            both = q4[:, b * LANES:(b + 1) * LANES]
            h = 4 * j + 2 * b
            q_ref[0, h] = (jnp.where(low, both, 0.0) + qf_ref[h:h + 1, :]).astype(BF16)
            q_ref[0, h + 1] = (jnp.where(low, pltpu.roll(both, HEAD_DIM, 1), 0.0) + qf_ref[h + 1:h + 2, :]).astype(BF16)
    cmp_in = _dot(hn, w_ref[:, CMP_OFF:CMP_OFF + 2 * LANES])
    kc_ref[...] = cmp_in[:, 0:LANES]
    vc_ref[...] = cmp_in[:, LANES:2 * LANES]
    keys = _dot(hn, w_ref[:, KEY_OFF:KEY_OFF + 2 * LANES])
    kf = kf_ref[...]
    for i, k_ref in enumerate((ks_ref, kw_ref)):
        both = keys[:, i * LANES:(i + 1) * LANES]
        k_ref[0, 0] = (jnp.where(low, both, 0.0) + kf).astype(BF16)
        k_ref[0, 1] = (jnp.where(low, pltpu.roll(both, HEAD_DIM, 1), 0.0) + kf).astype(BF16)
    vals = _dot(hn, w_ref[:, VAL_OFF:VAL_OFF + 2 * LANES])
    for i, v_ref in enumerate((vs_ref, vw_ref)):
        _store_values_t(v_ref, vals[:, i * LANES:(i + 1) * LANES])
    gate_ref[...] = jax.nn.sigmoid(_dot(hn, w_ref[:, GATE_OFF:GATE_OFF + LANES]))
    u_ref[...] = _dot(hn, w_ref[:, POOL_OFF:POOL_OFF + POOL_COLS])


def _pack_w_in(w_in):
    w_in = w_in.astype(BF16)
    qw = N_HEADS * HEAD_DIM
    kvw = N_GROUPS * HEAD_DIM
    kc, vc, ks, vs, kw, vw = (w_in[..., qw + i * kvw:qw + (i + 1) * kvw] for i in range(6))
    off = qw + 6 * kvw
    n_g = 3 * N_HEADS
    w_g = jnp.pad(w_in[..., off:off + n_g], ((0, 0), (0, 0), (0, LANES - n_g)))
    w_pool = w_in[..., off + n_g:off + n_g + POOL_COLS]
    w_merge = w_in[..., off + n_g + POOL_COLS:]
    return jnp.concatenate([w_in[..., :qw], kc, vc, ks, kw, vs, vw, w_g, w_pool], axis=-1), w_merge


def _ffn_proj_kernel(x_ref, fg_ref, wg_ref, wu_ref, wd_ref, g_ref, w_ref, qf_ref, kf_ref, y_ref, *proj_out_refs):
    _ffn_kernel(x_ref, fg_ref, wg_ref, wu_ref, wd_ref, None, y_ref, final_norm=False)
    _proj_kernel(y_ref, g_ref, w_ref, qf_ref, kf_ref, *proj_out_refs)


def _ffn_proj(x, ffn_g, wg, wu, wd, norm_g, w_packed, layer, batch, seq, *, tm=512):
    n, d = x.shape
    tiles_per_seq = seq // tm
    row = lambda i: (i, 0)
    const = lambda i: (0, 0)
    of_layer = lambda a: pl.BlockSpec((None,) + a.shape[1:], lambda i: (layer, 0, 0), pipeline_mode=pl.Buffered(1))
    per_group = lambda i: (i // tiles_per_seq, 0, i % tiles_per_seq, 0)
    qf = _query_features()
    kf = _key_features(np.arange(seq), LANES, HEAD_DIM)
    flat = lambda width, dtype: (jax.ShapeDtypeStruct((n, width), dtype), pl.BlockSpec((tm, width), row))
    grouped = lambda count: (jax.ShapeDtypeStruct((batch, count, seq, LANES), BF16),
                             pl.BlockSpec((1, count, tm, LANES), per_group))
    values_t = (jax.ShapeDtypeStruct((batch, N_GROUPS, V_ROWS, seq), BF16),
                pl.BlockSpec((1, N_GROUPS, V_ROWS, tm), lambda i: (i // tiles_per_seq, 0, 0, i % tiles_per_seq)))
    outs = [flat(d, F32), grouped(N_HEADS), flat(LANES, F32), flat(LANES, F32), grouped(N_GROUPS), grouped(N_GROUPS),
            values_t, values_t, flat(LANES, F32), flat(POOL_COLS, F32)]
    return pl.pallas_call(
        _ffn_proj_kernel,
        out_shape=[o[0] for o in outs],
        grid=(n // tm,),
        in_specs=[pl.BlockSpec((tm, d), row), pl.BlockSpec((1, d), const), of_layer(wg), of_layer(wu), of_layer(wd),
                  pl.BlockSpec((1, d), const), of_layer(w_packed),
                  pl.BlockSpec(qf.shape, const), pl.BlockSpec((tm, LANES), lambda i: (i % tiles_per_seq, 0))],
        out_specs=[o[1] for o in outs],
        compiler_params=_params("parallel"),
        name="ffn_proj",
    )(x, ffn_g, wg, wu, wd, norm_g, w_packed, qf, kf)


def _compress_kernel(k_ref, v_ref, pos_ref, wk1_ref, wk2_ref, wv1_ref, wv2_ref, feat_ref, ko_ref, vo_ref):
    n_chunk = k_ref.shape[0] // CMP_STRIDE
    hidden = wk2_ref.shape[0]

    def compress(x_ref, w1_ref, w2_ref):
        first = second = None
        for r in range(CMP_STRIDE):
            rows = x_ref[pl.ds(r, n_chunk, stride=CMP_STRIDE), :]
            a = _dot((rows + pos_ref[r:r + 1, :]).astype(BF16), w1_ref[r])
            b = _dot((rows + pos_ref[CMP_STRIDE + r:CMP_STRIDE + r + 1, :]).astype(BF16), w1_ref[CMP_STRIDE + r])
            first = a if first is None else first + a
            second = b if second is None else second + b
        act = jax.nn.gelu(first + pltpu.roll(second, n_chunk - 1, 0)).astype(BF16)
        return [_dot(act[:, g * hidden:(g + 1) * hidden], w2_ref[...]) for g in range(N_GROUPS)]

    for g, k_cmp in enumerate(compress(k_ref, wk1_ref, wk2_ref)):
        ko_ref[0, g] = jnp.concatenate([k_cmp, feat_ref[...]], axis=-1).astype(BF16)
    _store_values_t(vo_ref, jnp.concatenate(compress(v_ref, wv1_ref, wv2_ref), axis=-1))


def _block_diag_w1(w1):
    w = w1.astype(BF16).reshape(w1.shape[0], CMP_LEN, HEAD_DIM, -1)
    z = jnp.zeros_like(w)
    return jnp.concatenate([jnp.concatenate([w, z], axis=3), jnp.concatenate([z, w], axis=3)], axis=2)


def _compress(kc, vc, weights, layer, batch, seq):
    n_chunk = seq // CMP_STRIDE
    rows = pl.BlockSpec((seq, LANES), lambda b: (b, 0))
    of_layer = lambda a: pl.BlockSpec((None,) + a.shape[1:], lambda b: (layer,) + (0,) * (a.ndim - 1),
                                      pipeline_mode=pl.Buffered(1))
    feat = _key_features(np.arange(n_chunk) * CMP_STRIDE + CMP_LEN - 1, HEAD_DIM, 0)
    return pl.pallas_call(
        _compress_kernel,
        out_shape=[jax.ShapeDtypeStruct((batch, N_GROUPS, n_chunk, LANES), BF16),
                   jax.ShapeDtypeStruct((batch, N_GROUPS, V_ROWS, n_chunk), BF16)],
        grid=(batch,),
        in_specs=[rows, rows] + [of_layer(a) for a in weights]
                 + [pl.BlockSpec(feat.shape, lambda b: (0, 0), pipeline_mode=pl.Buffered(1))],
        out_specs=[pl.BlockSpec((1, N_GROUPS, n_chunk, LANES), lambda b: (b, 0, 0, 0)),
                   pl.BlockSpec((1, N_GROUPS, V_ROWS, n_chunk), lambda b: (b, 0, 0, 0))],
        compiler_params=_params("parallel"),
        name="compress",
    )(kc, vc, *weights, feat)


SLC_CHUNK = 256
BLOCKS_PER_CHUNK = SLC_CHUNK // SEL_LEN
HEAD_BLOCKS = 2


def _nsa_kernel(q_ref, kc_ref, vct_ref, ks_ref, vst_ref, kw_ref, vwt_ref, gate_ref, o_ref,
                st_ref, p_ref, mx_ref, gt_ref, selt_ref, list_ref, m_ref, acc_ref, out_ref, *, tq, seq):
    n_sel = seq // SEL_LEN
    n_chunks = seq // SLC_CHUNK
    n_cmp = kc_ref.shape[2]
    n_win = WINDOW + tq
    q0 = pl.program_id(1) * tq
    gt_ref[...] = gate_ref[...].T
    pairs = [(g, g * HEADS_PER_GROUP + 2 * j) for g in range(N_GROUPS) for j in range(HEADS_PER_GROUP // 2)]

    def gate_row(head, branch):
        c = 3 * head + branch
        return gt_ref[c:c + 1, :]

    def out_rows(head):
        g, hh = divmod(head, HEADS_PER_GROUP)
        return hh, slice(g * HEAD_DIM, (g + 1) * HEAD_DIM)

    def stage_scores(keys, n_keys, row0=0):
        for g, head0 in pairs:
            q_pair = q_ref[0, head0:head0 + 2].reshape(2 * tq, LANES)
            scores = _dot_nt(keys(g), q_pair)
            for i in range(2):
                st_ref[head0 + i, row0:row0 + n_keys, :] = scores[:, i * tq:(i + 1) * tq]

    def finish_pair(head0, pv, branch, first, ok=None):
        invs = []
        for i in range(2):
            cols = slice(i * tq, (i + 1) * tq)
            inv = 1.0 / pv[HEAD_DIM:HEAD_DIM + 1, cols]
            if ok is not None:
                inv = jnp.where(ok, inv, 0.0)
            hh, rows = out_rows(head0 + i)
            contrib = pv[0:HEAD_DIM, cols] * (inv * gate_row(head0 + i, branch))
            out_ref[hh, rows, :] = contrib if first else out_ref[hh, rows, :] + contrib
            invs.append(inv)
        return invs

    t_c = q0 + lax.broadcasted_iota(jnp.int32, (n_cmp, tq), 1)
    end_c = lax.broadcasted_iota(jnp.int32, (n_cmp, tq), 0) * CMP_STRIDE + (CMP_LEN - 1)
    bias_c = jnp.where(t_c >= end_c, 0.0, NEG_BIG)
    has_cmp = (q0 + lax.broadcasted_iota(jnp.int32, (1, tq), 1)) >= CMP_LEN - 1
    inv_c = [None] * N_HEADS
    raw = [None] * (N_HEADS // 2)

    def cmp_pair(g, head0):
        for i in range(2):
            x = st_ref[head0 + i, 0:n_cmp, :] + bias_c
            p_ref[head0 // 2, 0:n_cmp, i * tq:(i + 1) * tq] = jnp.exp2(x - jnp.max(x, axis=0, keepdims=True)).astype(BF16)
        both = _dot(vct_ref[0, g], p_ref[head0 // 2, 0:n_cmp, :])
        raw[head0 // 2] = both[V_ROWS:, :]
        inv_c[head0], inv_c[head0 + 1] = finish_pair(head0, both[0:V_ROWS, :], 0, True, has_cmp)

    stage_scores(lambda g: kc_ref[0, g], n_cmp)
    for g, head0 in pairs:
        cmp_pair(g, head0)

    w0 = pl.multiple_of(jnp.maximum(q0 - WINDOW, 0), tq)
    stage_scores(lambda g: kw_ref[0, g, pl.ds(w0, n_win), :], n_win)
    head_keys = HEAD_BLOCKS * SEL_LEN
    stage_scores(lambda g: ks_ref[0, g, 0:head_keys, :], head_keys, n_win)

    dist_w = (q0 + lax.broadcasted_iota(jnp.int32, (n_win, tq), 1)) - (w0 + lax.broadcasted_iota(jnp.int32, (n_win, tq), 0))
    bias_w = jnp.where((dist_w >= 0) & (dist_w < WINDOW), 0.0, NEG_BIG)

    def win_pair(g, head0):
        for i in range(2):
            x = st_ref[head0 + i, 0:n_win, :] + bias_w
            p_ref[head0 // 2, 0:n_win, i * tq:(i + 1) * tq] = jnp.exp2(x - jnp.max(x, axis=0, keepdims=True)).astype(BF16)
        pv = _dot(vwt_ref[0, g, :, pl.ds(w0, n_win)], p_ref[head0 // 2, 0:n_win, :])
        finish_pair(head0, pv, 2, False)

    for g, head0 in pairs:
        win_pair(g, head0)

    blk = lax.broadcasted_iota(jnp.int32, (n_sel, tq), 0)
    cur = (q0 + lax.broadcasted_iota(jnp.int32, (n_sel, tq), 1)) // SEL_LEN
    valid = blk <= cur
    forced = valid & ((blk == 0) | (blk == cur) | (blk == cur - 1))
    blk_f = blk.astype(F32)
    score = []
    for g in range(N_GROUPS):
        total = None
        for head in range(g * HEADS_PER_GROUP, (g + 1) * HEADS_PER_GROUP):
            part = raw[head // 2][:, (head % 2) * tq:(head % 2 + 1) * tq] * inv_c[head]
            total = part if total is None else total + part
        score.append(jnp.where(forced, TOPK_TAKEN, jnp.where(valid, total, -TOPK_BIG)))

    for _ in range(min(N_SELECT, n_sel) - N_FORCED):
        for g in range(N_GROUPS):
            best = jnp.max(score[g], axis=0, keepdims=True)
            pick = jnp.min(jnp.where(score[g] == best, blk_f, float(n_sel)), axis=0, keepdims=True)
            score[g] = jnp.where(blk_f == pick, TOPK_TAKEN, score[g])
    sel = [jnp.where(valid & (score[g] == TOPK_TAKEN), 1.0, 0.0) for g in range(N_GROUPS)]
    for g in range(N_GROUPS):
        selt_ref[g] = jnp.where(blk >= HEAD_BLOCKS, sel[g], 0.0)
    sel_any = jnp.where(blk >= HEAD_BLOCKS, sel[0] + sel[1], 0.0)
    n_items = jnp.int32(0)
    for c in range(n_chunks):
        any_sel = jnp.max(sel_any[c * BLOCKS_PER_CHUNK:(c + 1) * BLOCKS_PER_CHUNK, :])
        list_ref[n_items] = c
        n_items = n_items + (any_sel > 0.5).astype(jnp.int32)

    causal_h = (q0 + lax.broadcasted_iota(jnp.int32, (head_keys, tq), 1)) >= lax.broadcasted_iota(jnp.int32, (head_keys, tq), 0)
    head_rows = slice(n_win, n_win + head_keys)
    for g, head0 in pairs:
        if head0 % HEADS_PER_GROUP == 0:
            sel_keys = jnp.concatenate([jnp.broadcast_to(sel[g][i:i + 1, :], (SEL_LEN, tq)) for i in range(HEAD_BLOCKS)], axis=0)
            bias = jnp.where(causal_h & (sel_keys > 0.5), 0.0, NEG_BIG)
        for i in range(2):
            x = st_ref[head0 + i, head_rows, :] + bias
            m_new = jnp.max(x, axis=0, keepdims=True)
            m_ref[head0 + i:head0 + i + 1, :] = m_new
            p_ref[head0 // 2, head_rows, i * tq:(i + 1) * tq] = jnp.exp2(x - m_new).astype(BF16)
    for g, head0 in pairs:
        acc_ref[head0 // 2] = _dot(vst_ref[0, g, :, 0:head_keys], p_ref[head0 // 2, head_rows, :])

    base = (lax.broadcasted_iota(jnp.int32, (SLC_CHUNK, tq), 1)
            - lax.broadcasted_iota(jnp.int32, (SLC_CHUNK, tq), 0))

    def stage_chunk(item, row0):
        c = list_ref[item]
        k0 = pl.multiple_of(c * SLC_CHUNK, SLC_CHUNK)
        causal = base >= k0 - q0
        for g, head0 in pairs:
            if head0 % HEADS_PER_GROUP == 0:
                sel_keys = jnp.concatenate(
                    [jnp.broadcast_to(selt_ref[g, pl.ds(c * BLOCKS_PER_CHUNK + i, 1), :], (SEL_LEN, tq))
                     for i in range(BLOCKS_PER_CHUNK)], axis=0)
                bias = jnp.where(causal & (sel_keys > 0.5), 0.0, NEG_BIG)
                bias2 = jnp.concatenate([bias, bias], axis=1)
            q_pair = q_ref[0, head0:head0 + 2].reshape(2 * tq, LANES)
            scores = _dot_nt(ks_ref[0, g, pl.ds(k0, SLC_CHUNK), :], q_pair) + bias2
            slot = (row0 // SLC_CHUNK) * (N_HEADS // 2) + head0 // 2
            mx_ref[slot:slot + 1, :] = jnp.max(scores, axis=0, keepdims=True)
            for i in range(2):
                st_ref[head0 + i, row0:row0 + SLC_CHUNK, :] = scores[:, i * tq:(i + 1) * tq]

    def consume_chunk(item, row0):
        k0 = pl.multiple_of(list_ref[item] * SLC_CHUNK, SLC_CHUNK)
        for g, head0 in pairs:
            alphas = []
            for i, head in enumerate((head0, head0 + 1)):
                slot = (row0 // SLC_CHUNK) * (N_HEADS // 2) + head0 // 2
                m_old = m_ref[head:head + 1, :]
                m_new = jnp.maximum(m_old, mx_ref[slot:slot + 1, i * tq:(i + 1) * tq])
                alphas.append(jnp.exp2(m_old - m_new))
                m_ref[head:head + 1, :] = m_new
                x = st_ref[head, row0:row0 + SLC_CHUNK, :]
                p_ref[head0 // 2, 0:SLC_CHUNK, i * tq:(i + 1) * tq] = jnp.exp2(x - m_new).astype(BF16)
            pv = _dot(vst_ref[0, g, :, pl.ds(k0, SLC_CHUNK)], p_ref[head0 // 2, 0:SLC_CHUNK, :])
            acc_ref[head0 // 2] = acc_ref[head0 // 2] * jnp.concatenate(alphas, axis=1) + pv

    last = jnp.maximum(n_items - 1, 0)
    stage_chunk(0, 0)

    def two_chunks(j, carry):
        stage_chunk(jnp.minimum(2 * j + 1, last), SLC_CHUNK)
        consume_chunk(2 * j, 0)

        @pl.when(2 * j + 1 < n_items)
        def _():
            stage_chunk(jnp.minimum(2 * j + 2, last), 0)
            consume_chunk(2 * j + 1, SLC_CHUNK)
        return carry

    lax.fori_loop(0, (n_items + 1) // 2, two_chunks, 0)
    for g, head0 in pairs:
        finish_pair(head0, acc_ref[head0 // 2], 1, False)

    for hh in range(HEADS_PER_GROUP):
        o_ref[:, hh * LANES:(hh + 1) * LANES] = out_ref[hh].T.astype(BF16)


def _overlap_t(seq):
    n_cmp = (seq - CMP_LEN) // CMP_STRIDE + 1
    n_sel = seq // SEL_LEN
    cs = np.arange(n_cmp)[:, None] * CMP_STRIDE
    ss = np.arange(n_sel)[None, :] * SEL_LEN
    ov = np.clip(np.minimum(cs + CMP_LEN, ss + SEL_LEN) - np.maximum(cs, ss), 0, None) / CMP_LEN
    out = np.zeros((n_sel, seq // CMP_STRIDE), np.float32)
    out[:, :n_cmp] = ov.T
    return jnp.asarray(out, dtype=BF16)


def _nsa(q, kc, vct, ks, vst, kw, vwt, gates, *, tq=128):
    batch, _, seq, _ = q.shape
    n_cmp = kc.shape[2]
    n_sel = seq // SEL_LEN
    assert seq % SLC_CHUNK == 0 and seq >= WINDOW + tq and SLC_CHUNK % tq == 0 and tq == LANES
    assert n_sel <= 2 * SEL_LEN
    ovt = jnp.broadcast_to(_overlap_t(seq), (batch, N_GROUPS, n_sel, n_cmp))
    vct = jnp.concatenate([vct, ovt], axis=2)
    keys = lambda n: pl.BlockSpec((1, N_GROUPS, n, LANES), lambda b, i: (b, 0, 0, 0))
    vals = lambda n: pl.BlockSpec((1, N_GROUPS, V_ROWS, n), lambda b, i: (b, 0, 0, 0))
    tiles = seq // tq
    s_rows = max(n_cmp, WINDOW + tq + HEAD_BLOCKS * SEL_LEN, 2 * SLC_CHUNK)
    return pl.pallas_call(
        functools.partial(_nsa_kernel, tq=tq, seq=seq),
        out_shape=jax.ShapeDtypeStruct((batch * seq, HEADS_PER_GROUP * LANES), BF16),
        grid=(batch, tiles),
        in_specs=[
            pl.BlockSpec((1, N_HEADS, tq, LANES), lambda b, i: (b, 0, i, 0)),
            keys(n_cmp), pl.BlockSpec((1, N_GROUPS, V_ROWS + n_sel, n_cmp), lambda b, i: (b, 0, 0, 0)),
            keys(seq), vals(seq), keys(seq), vals(seq),
            pl.BlockSpec((tq, LANES), lambda b, i: (b * tiles + i, 0)),
        ],
        out_specs=pl.BlockSpec((tq, HEADS_PER_GROUP * LANES), lambda b, i: (b * tiles + i, 0)),
        scratch_shapes=[
            pltpu.VMEM((N_HEADS, s_rows, tq), F32),
            pltpu.VMEM((N_HEADS // 2, s_rows, 2 * tq), BF16),
            pltpu.VMEM((N_HEADS, 2 * tq), F32),
            pltpu.VMEM((LANES, tq), F32),
            pltpu.VMEM((N_GROUPS, n_sel, tq), F32),
            pltpu.SMEM((seq // SLC_CHUNK,), jnp.int32),
            pltpu.VMEM((2 * N_HEADS, tq), F32),
            pltpu.VMEM((N_HEADS // 2, V_ROWS, 2 * tq), F32),
            pltpu.VMEM((HEADS_PER_GROUP, N_GROUPS * HEAD_DIM, tq), F32),
        ],
        compiler_params=_params("parallel", "arbitrary"),
        name="nsa",
    )(q, kc, vct, ks, vst, kw, vwt, gates)


HALO = max(POOL_WINDOWS)
assert all(w & (w - 1) == 0 for w in POOL_WINDOWS)


def _merge_kernel(x_ref, g_ref, u_ref, halo_ref, wgm_ref, on_ref, pw_ref, ps_ref, wbp_ref, wbn_ref, wo_ref,
                  o_ref, ext_ref, *, tm, seq):
    d = x_ref.shape[1]
    gm = jax.nn.sigmoid(_dot(_rms(x_ref[...], g_ref[...]).astype(BF16), wgm_ref[...]))
    b = _dot(on_ref[...], wbn_ref[...])
    pos0 = (pl.program_id(0) * tm) % seq
    ext_ref[0:HALO, :] = jnp.where(pos0 == 0, 0.0, halo_ref[...])
    ext_ref[HALO:HALO + tm, :] = u_ref[...]
    pos = (pos0 + lax.broadcasted_iota(jnp.int32, (tm, POOL_GROUP_DIM), 0)).astype(F32)
    mixed = []
    for gi, w in enumerate(POOL_WINDOWS):
        cols = slice(gi * POOL_GROUP_DIM, (gi + 1) * POOL_GROUP_DIM)
        run = ext_ref[:, cols]
        for k in range(w.bit_length() - 1):
            run = run + pltpu.roll(run, 1 << k, 0)
        u = ext_ref[HALO:HALO + tm, cols]
        delta = run[HALO:HALO + tm] / jnp.minimum(pos + 1.0, float(w)) - u
        mixed.append(_dot(delta.astype(BF16), pw_ref[gi]) * ps_ref[:, cols])
    mixed = jnp.concatenate(mixed, axis=-1).astype(BF16)
    a = _dot(mixed, wbp_ref[...])
    merged = gm[:, 0:d] * a + gm[:, d:2 * d] * b
    o_ref[...] = x_ref[...] + _dot(merged.astype(BF16), wo_ref[...])


def _merge_ffn_kernel(x_ref, g_ref, u_ref, halo_ref, wgm_ref, on_ref, pw_ref, ps_ref, wbp_ref, wbn_ref, wo_ref,
                      fg_ref, wg_ref, wu_ref, wd_ref, fin_ref, o_ref, ext_ref, mid_ref, *, tm, seq, final_norm):
    _merge_kernel(x_ref, g_ref, u_ref, halo_ref, wgm_ref, on_ref, pw_ref, ps_ref, wbp_ref, wbn_ref, wo_ref,
                  mid_ref, ext_ref, tm=tm, seq=seq)
    _ffn_kernel(mid_ref, fg_ref, wg_ref, wu_ref, wd_ref, fin_ref, o_ref, final_norm=final_norm)


def _merge_ffn(x, norm_g, u, w_gm, o_nsa, pool_w, pool_scale, w_bp, w_bn, w_out, ffn_g, ffn_w, fin_g, layer, seq, *,
               final_norm, tm=512):
    n, d = x.shape
    row = lambda i: (i, 0)
    full = lambda a: pl.BlockSpec(a.shape, lambda i: (0,) * a.ndim, pipeline_mode=pl.Buffered(1))
    of_layer = lambda a: pl.BlockSpec((None,) + a.shape[1:], lambda i: (layer,) + (0,) * (a.ndim - 1),
                                      pipeline_mode=pl.Buffered(1))
    halo_blocks = tm // HALO
    wg, wu, wd = ffn_w
    return pl.pallas_call(
        functools.partial(_merge_ffn_kernel, tm=tm, seq=seq, final_norm=final_norm),
        out_shape=jax.ShapeDtypeStruct((n, d), F32),
        grid=(n // tm,),
        in_specs=[
            pl.BlockSpec((tm, d), row),
            full(norm_g),
            pl.BlockSpec((tm, POOL_COLS), row),
            pl.BlockSpec((HALO, POOL_COLS), lambda i: (jnp.maximum(i * halo_blocks - 1, 0), 0)),
            of_layer(w_gm),
            pl.BlockSpec((tm, d), row),
            of_layer(pool_w), full(pool_scale), of_layer(w_bp), of_layer(w_bn), of_layer(w_out),
            full(ffn_g), of_layer(wg), of_layer(wu), of_layer(wd), full(fin_g),
        ],
        out_specs=pl.BlockSpec((tm, d), row),
        scratch_shapes=[pltpu.VMEM((HALO + tm, POOL_COLS), F32), pltpu.VMEM((tm, d), F32)],
        compiler_params=_params("parallel"),
        name="merge_ffn",
    )(x, norm_g, u, u, w_gm, o_nsa, pool_w, pool_scale, w_bp, w_bn, w_out, ffn_g, wg, wu, wd, fin_g)


def kernel(x, ffn1_norm, ffn1_w_gate, ffn1_w_up, ffn1_w_down, mix_norm, w_in, cmp_pos, cmp_k_w1, cmp_k_w2, cmp_v_w1, cmp_v_w2, pool_w, pool_scale, w_branch_pool, w_branch_nsa, w_out, ffn2_norm, ffn2_w_gate, ffn2_w_up, ffn2_w_down, final_norm):
    batch, seq, d = x.shape
    depth = w_in.shape[0]
    xf = x.reshape(batch * seq, d)
    bf = lambda a: a.astype(BF16)
    row = lambda a: a.reshape(1, -1)
    ffn1 = (bf(ffn1_w_gate), bf(ffn1_w_up), bf(ffn1_w_down))
    ffn2 = (bf(ffn2_w_gate), bf(ffn2_w_up), bf(ffn2_w_down))
    w_packed, w_gm = _pack_w_in(w_in)
    cmp_w = (jnp.concatenate([cmp_pos] * N_GROUPS, axis=-1),
             _block_diag_w1(cmp_k_w1), bf(cmp_k_w2), _block_diag_w1(cmp_v_w1), bf(cmp_v_w2))
    w_bn = bf(w_branch_nsa).reshape(depth, N_GROUPS, HEADS_PER_GROUP, HEAD_DIM, d).transpose(0, 2, 1, 3, 4)
    w_bn = w_bn.reshape(depth, -1, d)
    for l in range(depth):
        xf, q, kc, vc, ks, kw, vs, vw, gates, u = _ffn_proj(xf, row(ffn1_norm[l]), *ffn1, row(mix_norm[l]), w_packed, l,
                                                            batch, seq)
        k_cmp, v_cmp = _compress(kc, vc, cmp_w, l, batch, seq)
        o_nsa = _nsa(q, k_cmp, v_cmp, ks, vs, kw, vw, gates)
        xf = _merge_ffn(xf, row(mix_norm[l]), u, w_gm, o_nsa, bf(pool_w), row(pool_scale[l]), bf(w_branch_pool), w_bn,
                        bf(w_out), row(ffn2_norm[l]), ffn2, row(final_norm), l, seq, final_norm=(l == depth - 1))
    return xf.reshape(batch, seq, d)
```
